```python
import math
import jax, jax.numpy as jnp
from jax import lax
import numpy as np

D_MODEL = 2048
BATCH = 8
SEQ = 2048
DEPTH = 4

N_MIXERS = 3
N_A = (DEPTH + 2) // 3
N_B = (DEPTH + 1) // 3
N_C = DEPTH // 3

MIX_WIDTH = 3 * D_MODEL // 4
MEM_LEN = 256
MEM_HEADS = 4
MEM_HEAD_DIM = D_MODEL // 16
MEM_WIDTH = MEM_HEADS * MEM_HEAD_DIM
OUT_WIDTH = MIX_WIDTH + MEM_WIDTH
NORM_EPS = 1e-6

SWA_HEAD_DIM = 64
SWA_Q_HEADS = MIX_WIDTH // SWA_HEAD_DIM
SWA_KV_HEADS = 4
SWA_GROUP = SWA_Q_HEADS // SWA_KV_HEADS
SWA_WINDOW = 128
SWA_BLOCK = 128
A_WIDTHS = (SWA_Q_HEADS * SWA_HEAD_DIM, SWA_KV_HEADS * SWA_HEAD_DIM, SWA_KV_HEADS * SWA_HEAD_DIM, MEM_WIDTH)

RWKV_HEAD_DIM = 64
RWKV_HEADS = MIX_WIDTH // RWKV_HEAD_DIM
RWKV_DECAY_RANK = 96
RWKV_ICLR_RANK = 96
RWKV_GATE_RANK = 256
RWKV_GN_EPS = 64e-5
B_SHIFT_WIDTHS = (MIX_WIDTH, MIX_WIDTH, MIX_WIDTH, RWKV_DECAY_RANK, RWKV_ICLR_RANK, RWKV_GATE_RANK)
B_SHIFT = sum(B_SHIFT_WIDTHS)

GDN_HEAD_DIM = 128
GDN_V_HEADS = MIX_WIDTH // GDN_HEAD_DIM
GDN_QK_HEADS = GDN_V_HEADS // 2
GDN_CONV = 4
GDN_CHUNK = 64
GDN_QK_WIDTH = GDN_QK_HEADS * GDN_HEAD_DIM
GDN_CONV_WIDTH = 2 * GDN_QK_WIDTH + MIX_WIDTH
C_WIDTHS = (GDN_QK_WIDTH, GDN_QK_WIDTH, MIX_WIDTH, MIX_WIDTH, GDN_V_HEADS, GDN_V_HEADS, MEM_WIDTH)

D_FF = 5632
FFN_CONV = 3

kernel_name = "hybrid_swa_rwkv7_gdn_memxattn_convffn"


def split_cols(p, widths):
    return jnp.split(p, [int(i) for i in np.cumsum(widths)[:-1]], axis=-1)


def rmsnorm(x, g):
    xf = x.astype(jnp.float32)
    y = xf * lax.rsqrt(jnp.mean(xf * xf, axis=-1, keepdims=True) + NORM_EPS)
    return (y * g.astype(jnp.float32)).astype(x.dtype)


def l2norm(x):
    x = x.astype(jnp.float32)
    return x * lax.rsqrt(jnp.sum(x * x, axis=-1, keepdims=True) + 1e-6)


def token_shift(x):
    return jnp.pad(x, ((0, 0), (1, 0), (0, 0)))[:, :-1]


def causal_dwconv(x, w):
    k_w, s = w.shape[0], x.shape[1]
    xp = jnp.pad(x, ((0, 0), (k_w - 1, 0), (0, 0)))
    out = xp[:, :s] * w[0]
    for j in range(1, k_w):
        out = out + xp[:, j:j + s] * w[j]
    return out


def alibi_slopes(n):
    return jnp.exp2(-8.0 * (jnp.arange(n, dtype=jnp.float32) + 1.0) / n)


def swa_sink_attention(q, k, v, sinks):
    b, s, _ = q.shape
    t, nb, dh = SWA_BLOCK, s // SWA_BLOCK, SWA_HEAD_DIM
    qb = q.reshape(b, nb, t, SWA_KV_HEADS, SWA_GROUP, dh)

    def banded(z):
        zb = z.reshape(b, nb, t, SWA_KV_HEADS, dh)
        prev = jnp.pad(zb, ((0, 0), (1, 0), (0, 0), (0, 0), (0, 0)))[:, :-1]
        return jnp.concatenate([prev, zb], axis=2)

    kb, vb = banded(k), banded(v)
    scores = jnp.einsum('bntkgd,bnjkd->bnkgtj', qb, kb).astype(jnp.float32) * (dh ** -0.5)
    blk = jnp.arange(nb)[:, None]
    qpos = blk * t + jnp.arange(t)[None, :]
    kpos = (blk - 1) * t + jnp.arange(2 * t)[None, :]
    dist = qpos[:, :, None] - kpos[:, None, :]
    valid = (dist >= 0) & (dist < SWA_WINDOW) & (kpos[:, None, :] >= 0)
    slopes = alibi_slopes(SWA_Q_HEADS).reshape(SWA_KV_HEADS, SWA_GROUP)
    bias = -slopes[None, :, :, None, None] * dist[:, None, None].astype(jnp.float32)
    scores = jnp.where(valid[:, None, None], scores + bias, -jnp.inf)
    sink = jnp.broadcast_to(sinks.astype(jnp.float32).reshape(1, 1, SWA_KV_HEADS, SWA_GROUP, 1, 1),
                            scores.shape[:-1] + (1,))
    probs = jax.nn.softmax(jnp.concatenate([scores, sink], axis=-1), axis=-1)[..., :-1]
    out = jnp.einsum('bnkgtj,bnjkd->bntkgd', probs.astype(v.dtype), vb)
    return out.reshape(b, s, SWA_Q_HEADS * dh)


def memory_attention(q, mem_kv):
    b, s, _ = q.shape
    qh = q.reshape(b, s, MEM_HEADS, MEM_HEAD_DIM)
    k, v = jnp.split(mem_kv, 2, axis=-1)
    kh = k.reshape(b, -1, MEM_HEADS, MEM_HEAD_DIM)
    vh = v.reshape(b, -1, MEM_HEADS, MEM_HEAD_DIM)
    scores = jnp.einsum('bshd,bmhd->bhsm', qh, kh).astype(jnp.float32) * (MEM_HEAD_DIM ** -0.5)
    probs = jax.nn.softmax(scores, axis=-1).astype(vh.dtype)
    return jnp.einsum('bhsm,bmhd->bshd', probs, vh).reshape(b, s, MEM_WIDTH)


def rwkv7_scan(r, w, k, v, kk, a):
    b, s, h, n = r.shape

    def step(state, inp):
        r_t, w_t, k_t, v_t, kk_t, a_t = inp
        sa = jnp.einsum('bhvk,bhk->bhv', state, -kk_t)
        state = (state * w_t[:, :, None, :] + sa[..., None] * (kk_t * a_t)[:, :, None, :]
                 + v_t[..., None] * k_t[:, :, None, :])
        return state, jnp.einsum('bhvk,bhk->bhv', state, r_t)

    xs = tuple(jnp.moveaxis(z, 1, 0) for z in (r, w, k, v, kk, a))
    _, ys = lax.scan(step, jnp.zeros((b, h, n, n), jnp.float32), xs)
    return jnp.moveaxis(ys, 0, 1)


def rwkv7_time_mix(p, mu, w0, w_decay_up, a0, w_iclr_up, w_gate_up, k_k, k_a, r_k, gn_g, gn_b):
    b, s, _ = p.shape
    f32 = jnp.float32
    h, n = RWKV_HEADS, RWKV_HEAD_DIM
    p = p + (token_shift(p) - p) * mu
    r, k, v, wd, ad, gd = split_cols(p, B_SHIFT_WIDTHS)
    w_log = -jax.nn.softplus(-(w0 + jnp.tanh(wd) @ w_decay_up).astype(f32)) - 0.5
    decay = jnp.exp(-jnp.exp(w_log))
    a = jax.nn.sigmoid((a0 + ad @ w_iclr_up).astype(f32))
    g = (jax.nn.sigmoid(gd) @ w_gate_up).astype(f32)
    k = k.astype(f32)
    heads = lambda z: z.reshape(b, s, h, n)
    kk = l2norm(heads(k * k_k))
    k = k * (1.0 + (a - 1.0) * k_a)
    rh, kh, vh = heads(r.astype(f32)), heads(k), heads(v.astype(f32))
    y = rwkv7_scan(rh, heads(decay), kh, vh, kk, heads(a))
    mean = jnp.mean(y, axis=-1, keepdims=True)
    var = jnp.mean(jnp.square(y - mean), axis=-1, keepdims=True)
    y = ((y - mean) * lax.rsqrt(var + RWKV_GN_EPS)).reshape(b, s, MIX_WIDTH) * gn_g + gn_b
    bonus = jnp.sum(rh * kh * r_k, axis=-1, keepdims=True) * vh
    y = y + bonus.reshape(b, s, MIX_WIDTH)
    return (y * g).astype(p.dtype)


def chunk_gated_delta_rule(q, k, v, g, beta):
    b, s, h, dk = q.shape
    dv = v.shape[-1]
    c = GDN_CHUNK
    nc = s // c
    chunks = lambda z: jnp.moveaxis(z.reshape(b, nc, c, h, -1), 3, 1)
    q = chunks(q * (dk ** -0.5))
    k = chunks(k)
    v = chunks(v)
    beta = chunks(beta[..., None])
    gc = jnp.cumsum(chunks(g[..., None])[..., 0], axis=-1)
    idx = jnp.arange(c)
    causal = idx[:, None] >= idx[None, :]
    strict = idx[:, None] > idx[None, :]
    decay = jnp.exp(jnp.where(causal, gc[..., :, None] - gc[..., None, :], -jnp.inf))
    kb = k * beta
    lmat = jnp.where(strict, jnp.einsum('bhnid,bhnjd->bhnij', kb, k) * decay, 0.0)
    eye = jnp.eye(c, dtype=lmat.dtype)
    tmat = lax.linalg.triangular_solve(lmat + eye, jnp.broadcast_to(eye, lmat.shape),
                                       left_side=True, lower=True, unit_diagonal=True)
    u = tmat @ (v * beta)
    w = tmat @ (kb * jnp.exp(gc)[..., None])
    a_qk = jnp.where(causal, jnp.einsum('bhnid,bhnjd->bhnij', q, k) * decay, 0.0)
    q_dec = q * jnp.exp(gc)[..., None]
    g_last = gc[..., -1]
    k_dec = k * jnp.exp(g_last[..., None] - gc)[..., None]

    def step(state, inp):
        u_c, w_c, qd_c, a_c, kd_c, gl_c = inp
        v_new = u_c - w_c @ state
        out = qd_c @ state + a_c @ v_new
        state = state * jnp.exp(gl_c)[..., None, None] + jnp.swapaxes(kd_c, -1, -2) @ v_new
        return state, out

    xs = tuple(jnp.moveaxis(z, 2, 0) for z in (u, w, q_dec, a_qk, k_dec, g_last))
    _, out = lax.scan(step, jnp.zeros((b, h, dk, dv), jnp.float32), xs)
    return jnp.transpose(out, (1, 0, 3, 2, 4)).reshape(b, s, h, dv)


def gated_deltanet(p, conv_w, a_log, dt_bias, norm_g):
    b, s, _ = p.shape
    f32 = jnp.float32
    qkv, z, bt, at = split_cols(p, (GDN_CONV_WIDTH, MIX_WIDTH, GDN_V_HEADS, GDN_V_HEADS))
    qkv = jax.nn.silu(causal_dwconv(qkv, conv_w))
    q, k, v = split_cols(qkv, (GDN_QK_WIDTH, GDN_QK_WIDTH, MIX_WIDTH))
    rep = GDN_V_HEADS // GDN_QK_HEADS
    q = jnp.repeat(l2norm(q.reshape(b, s, GDN_QK_HEADS, GDN_HEAD_DIM)), rep, axis=2)
    k = jnp.repeat(l2norm(k.reshape(b, s, GDN_QK_HEADS, GDN_HEAD_DIM)), rep, axis=2)
    v = v.reshape(b, s, GDN_V_HEADS, GDN_HEAD_DIM).astype(f32)
    beta = jax.nn.sigmoid(bt.astype(f32))
    g = -jnp.exp(a_log.astype(f32)) * jax.nn.softplus(at.astype(f32) + dt_bias.astype(f32))
    o = chunk_gated_delta_rule(q, k, v, g, beta)
    o = o * lax.rsqrt(jnp.mean(o * o, axis=-1, keepdims=True) + NORM_EPS) * norm_g.astype(f32)
    o = o.reshape(b, s, MIX_WIDTH) * jax.nn.silu(z.astype(f32))
    return o.astype(p.dtype)


def _fwd_setup_inputs(seed: int = 0) -> dict:
    key = jax.random.key(seed)
    ks = iter(jax.random.split(key, 48))
    nrm = lambda shape, scale: jax.random.normal(next(ks), shape, jnp.float32) * scale
    gain = lambda shape: 1.0 + nrm(shape, 0.02)
    unif = lambda shape, lo, hi: jax.random.uniform(next(ks), shape, jnp.float32, lo, hi)
    d = D_MODEL
    a_in, b_in, c_in = sum(A_WIDTHS), B_SHIFT + MEM_WIDTH, sum(C_WIDTHS)
    dt = jnp.exp(unif((N_C, GDN_V_HEADS), math.log(1e-3), math.log(1e-1)))
    return {
        "x": nrm((BATCH, SEQ, d), 1.0),
        "mem": nrm((BATCH, MEM_LEN, d), 1.0),
        "attn_norm": gain((DEPTH, d)),
        "mem_norm": gain((DEPTH, d)),
        "w_mem_kv": nrm((DEPTH, d, 2 * MEM_WIDTH), d ** -0.5),
        "w_out": nrm((DEPTH, OUT_WIDTH, d), OUT_WIDTH ** -0.5),
        "ffn_norm": gain((DEPTH, d)),
        "w_ffn_up": nrm((DEPTH, d, 2 * D_FF), d ** -0.5),
        "ffn_conv": nrm((DEPTH, FFN_CONV, 2 * D_FF), FFN_CONV ** -0.5),
        "w_ffn_down": nrm((DEPTH, D_FF, d), D_FF ** -0.5),
        "final_norm": gain((d,)),
        "a_w_in": nrm((N_A, d, a_in), d ** -0.5),
        "a_sinks": nrm((N_A, SWA_Q_HEADS), 0.5),
        "b_w_in": nrm((N_B, d, b_in), d ** -0.5),
        "b_mu": unif((N_B, B_SHIFT), 0.0, 1.0),
        "b_w0": unif((N_B, MIX_WIDTH), -6.0, -1.0),
        "b_w_decay_up": nrm((N_B, RWKV_DECAY_RANK, MIX_WIDTH), 0.1 * RWKV_DECAY_RANK ** -0.5),
        "b_a0": nrm((N_B, MIX_WIDTH), 0.1),
        "b_w_iclr_up": nrm((N_B, RWKV_ICLR_RANK, MIX_WIDTH), RWKV_ICLR_RANK ** -0.5),
        "b_w_gate_up": nrm((N_B, RWKV_GATE_RANK, MIX_WIDTH), RWKV_GATE_RANK ** -0.5),
        "b_k_k": 0.85 + nrm((N_B, MIX_WIDTH), 0.02),
        "b_k_a": gain((N_B, MIX_WIDTH)),
        "b_r_k": nrm((N_B, RWKV_HEADS, RWKV_HEAD_DIM), 0.1),
        "b_gn_g": gain((N_B, MIX_WIDTH)),
        "b_gn_b": nrm((N_B, MIX_WIDTH), 0.02),
        "c_w_in": nrm((N_C, d, c_in), d ** -0.5),
        "c_conv": nrm((N_C, GDN_CONV, GDN_CONV_WIDTH), GDN_CONV ** -0.5),
        "c_a_log": jnp.log(unif((N_C, GDN_V_HEADS), 1.0, 16.0)),
        "c_dt_bias": dt + jnp.log(-jnp.expm1(-dt)),
        "c_norm_g": gain((N_C, GDN_HEAD_DIM)),
    }


def _fwd_reference(x, mem, attn_norm, mem_norm, w_mem_kv, w_out, ffn_norm, w_ffn_up, ffn_conv, w_ffn_down,
              final_norm, a_w_in, a_sinks, b_w_in, b_mu, b_w0, b_w_decay_up, b_a0, b_w_iclr_up,
              b_w_gate_up, b_k_k, b_k_a, b_r_k, b_gn_g, b_gn_b, c_w_in, c_conv, c_a_log, c_dt_bias,
              c_norm_g):
    for i in range(DEPTH):
        kind, j = i % N_MIXERS, i // N_MIXERS
        h = rmsnorm(x, attn_norm[i])
        mem_kv = rmsnorm(mem, mem_norm[i]) @ w_mem_kv[i]
        if kind == 0:
            q, k, v, q_mem = split_cols(h @ a_w_in[j], A_WIDTHS)
            y = swa_sink_attention(q, k, v, a_sinks[j])
        elif kind == 1:
            p = h @ b_w_in[j]
            p_mix, q_mem = p[..., :B_SHIFT], p[..., B_SHIFT:]
            y = rwkv7_time_mix(p_mix, b_mu[j], b_w0[j], b_w_decay_up[j], b_a0[j], b_w_iclr_up[j],
                               b_w_gate_up[j], b_k_k[j], b_k_a[j], b_r_k[j], b_gn_g[j], b_gn_b[j])
        else:
            p = h @ c_w_in[j]
            p_mix, q_mem = p[..., :-MEM_WIDTH], p[..., -MEM_WIDTH:]
            y = gated_deltanet(p_mix, c_conv[j], c_a_log[j], c_dt_bias[j], c_norm_g[j])
        y_mem = memory_attention(q_mem, mem_kv)
        x = x + jnp.concatenate([y, y_mem], axis=-1) @ w_out[i]
        hf = rmsnorm(x, ffn_norm[i])
        u = causal_dwconv(hf @ w_ffn_up[i], ffn_conv[i])
        u_gate, u_val = jnp.split(u, 2, axis=-1)
        x = x + (jax.nn.silu(u_gate) * u_val) @ w_ffn_down[i]
    return rmsnorm(x, final_norm)


import jax as _jax
import jax.numpy as _jnp

TWIN_FORMAT = 'train_step'
FWD_PARAMS = ['x', 'mem', 'attn_norm', 'mem_norm', 'w_mem_kv', 'w_out', 'ffn_norm', 'w_ffn_up', 'ffn_conv', 'w_ffn_down', 'final_norm', 'a_w_in', 'a_sinks', 'b_w_in', 'b_mu', 'b_w0', 'b_w_decay_up', 'b_a0', 'b_w_iclr_up', 'b_w_gate_up', 'b_k_k', 'b_k_a', 'b_r_k', 'b_gn_g', 'b_gn_b', 'c_w_in', 'c_conv', 'c_a_log', 'c_dt_bias', 'c_norm_g']
TWIN_WEIGHTS = ['attn_norm', 'mem_norm', 'w_mem_kv', 'w_out', 'ffn_norm', 'w_ffn_up', 'ffn_conv', 'w_ffn_down', 'final_norm', 'a_w_in', 'a_sinks', 'b_w_in', 'b_mu', 'b_w0', 'b_w_decay_up', 'b_a0', 'b_w_iclr_up', 'b_w_gate_up', 'b_k_k', 'b_k_a', 'b_r_k', 'b_gn_g', 'b_gn_b', 'c_w_in', 'c_conv', 'c_a_log', 'c_dt_bias', 'c_norm_g']
TWIN_DIFF_INPUT = 'x'
TWIN_INPUTS = ['x', 'mem', 'attn_norm', 'mem_norm', 'w_mem_kv', 'w_out', 'ffn_norm', 'w_ffn_up', 'ffn_conv', 'w_ffn_down', 'final_norm', 'a_w_in', 'a_sinks', 'b_w_in', 'b_mu', 'b_w0', 'b_w_decay_up', 'b_a0', 'b_w_iclr_up', 'b_w_gate_up', 'b_k_k', 'b_k_a', 'b_r_k', 'b_gn_g', 'b_gn_b', 'c_w_in', 'c_conv', 'c_a_log', 'c_dt_bias', 'c_norm_g', 'loss_target', 'm_attn_norm', 'm_mem_norm', 'm_w_mem_kv', 'm_w_out', 'm_ffn_norm', 'm_w_ffn_up', 'm_ffn_conv', 'm_w_ffn_down', 'm_final_norm', 'm_a_w_in', 'm_a_sinks', 'm_b_w_in', 'm_b_mu', 'm_b_w0', 'm_b_w_decay_up', 'm_b_a0', 'm_b_w_iclr_up', 'm_b_w_gate_up', 'm_b_k_k', 'm_b_k_a', 'm_b_r_k', 'm_b_gn_g', 'm_b_gn_b', 'm_c_w_in', 'm_c_conv', 'm_c_a_log', 'm_c_dt_bias', 'm_c_norm_g', 'v_attn_norm', 'v_mem_norm', 'v_w_mem_kv', 'v_w_out', 'v_ffn_norm', 'v_w_ffn_up', 'v_ffn_conv', 'v_w_ffn_down', 'v_final_norm', 'v_a_w_in', 'v_a_sinks', 'v_b_w_in', 'v_b_mu', 'v_b_w0', 'v_b_w_decay_up', 'v_b_a0', 'v_b_w_iclr_up', 'v_b_w_gate_up', 'v_b_k_k', 'v_b_k_a', 'v_b_r_k', 'v_b_gn_g', 'v_b_gn_b', 'v_c_w_in', 'v_c_conv', 'v_c_a_log', 'v_c_dt_bias', 'v_c_norm_g']
TWIN_OUTPUTS = ['loss', 'grad_x', 'grad_attn_norm', 'grad_mem_norm', 'grad_w_mem_kv', 'grad_w_out', 'grad_ffn_norm', 'grad_w_ffn_up', 'grad_ffn_conv', 'grad_w_ffn_down', 'grad_final_norm', 'grad_a_w_in', 'grad_a_sinks', 'grad_b_w_in', 'grad_b_mu', 'grad_b_w0', 'grad_b_w_decay_up', 'grad_b_a0', 'grad_b_w_iclr_up', 'grad_b_w_gate_up', 'grad_b_k_k', 'grad_b_k_a', 'grad_b_r_k', 'grad_b_gn_g', 'grad_b_gn_b', 'grad_c_w_in', 'grad_c_conv', 'grad_c_a_log', 'grad_c_dt_bias', 'grad_c_norm_g', 'delta_attn_norm', 'delta_mem_norm', 'delta_w_mem_kv', 'delta_w_out', 'delta_ffn_norm', 'delta_w_ffn_up', 'delta_ffn_conv', 'delta_w_ffn_down', 'delta_final_norm', 'delta_a_w_in', 'delta_a_sinks', 'delta_b_w_in', 'delta_b_mu', 'delta_b_w0', 'delta_b_w_decay_up', 'delta_b_a0', 'delta_b_w_iclr_up', 'delta_b_w_gate_up', 'delta_b_k_k', 'delta_b_k_a', 'delta_b_r_k', 'delta_b_gn_g', 'delta_b_gn_b', 'delta_c_w_in', 'delta_c_conv', 'delta_c_a_log', 'delta_c_dt_bias', 'delta_c_norm_g', 'new_m_attn_norm', 'new_m_mem_norm', 'new_m_w_mem_kv', 'new_m_w_out', 'new_m_ffn_norm', 'new_m_w_ffn_up', 'new_m_ffn_conv', 'new_m_w_ffn_down', 'new_m_final_norm', 'new_m_a_w_in', 'new_m_a_sinks', 'new_m_b_w_in', 'new_m_b_mu', 'new_m_b_w0', 'new_m_b_w_decay_up', 'new_m_b_a0', 'new_m_b_w_iclr_up', 'new_m_b_w_gate_up', 'new_m_b_k_k', 'new_m_b_k_a', 'new_m_b_r_k', 'new_m_b_gn_g', 'new_m_b_gn_b', 'new_m_c_w_in', 'new_m_c_conv', 'new_m_c_a_log', 'new_m_c_dt_bias', 'new_m_c_norm_g', 'new_v_attn_norm', 'new_v_mem_norm', 'new_v_w_mem_kv', 'new_v_w_out', 'new_v_ffn_norm', 'new_v_w_ffn_up', 'new_v_ffn_conv', 'new_v_w_ffn_down', 'new_v_final_norm', 'new_v_a_w_in', 'new_v_a_sinks', 'new_v_b_w_in', 'new_v_b_mu', 'new_v_b_w0', 'new_v_b_w_decay_up', 'new_v_b_a0', 'new_v_b_w_iclr_up', 'new_v_b_w_gate_up', 'new_v_b_k_k', 'new_v_b_k_a', 'new_v_b_r_k', 'new_v_b_gn_g', 'new_v_b_gn_b', 'new_v_c_w_in', 'new_v_c_conv', 'new_v_c_a_log', 'new_v_c_dt_bias', 'new_v_c_norm_g']
TWIN_LEAF_KINDS = {'loss': 'loss', 'grad_x': 'grad_x', 'grad_attn_norm': 'grad_w', 'grad_mem_norm': 'grad_w', 'grad_w_mem_kv': 'grad_w', 'grad_w_out': 'grad_w', 'grad_ffn_norm': 'grad_w', 'grad_w_ffn_up': 'grad_w', 'grad_ffn_conv': 'grad_w', 'grad_w_ffn_down': 'grad_w', 'grad_final_norm': 'grad_w', 'grad_a_w_in': 'grad_w', 'grad_a_sinks': 'grad_w', 'grad_b_w_in': 'grad_w', 'grad_b_mu': 'grad_w', 'grad_b_w0': 'grad_w', 'grad_b_w_decay_up': 'grad_w', 'grad_b_a0': 'grad_w', 'grad_b_w_iclr_up': 'grad_w', 'grad_b_w_gate_up': 'grad_w', 'grad_b_k_k': 'grad_w', 'grad_b_k_a': 'grad_w', 'grad_b_r_k': 'grad_w', 'grad_b_gn_g': 'grad_w', 'grad_b_gn_b': 'grad_w', 'grad_c_w_in': 'grad_w', 'grad_c_conv': 'grad_w', 'grad_c_a_log': 'grad_w', 'grad_c_dt_bias': 'grad_w', 'grad_c_norm_g': 'grad_w', 'delta_attn_norm': 'delta_w', 'delta_mem_norm': 'delta_w', 'delta_w_mem_kv': 'delta_w', 'delta_w_out': 'delta_w', 'delta_ffn_norm': 'delta_w', 'delta_w_ffn_up': 'delta_w', 'delta_ffn_conv': 'delta_w', 'delta_w_ffn_down': 'delta_w', 'delta_final_norm': 'delta_w', 'delta_a_w_in': 'delta_w', 'delta_a_sinks': 'delta_w', 'delta_b_w_in': 'delta_w', 'delta_b_mu': 'delta_w', 'delta_b_w0': 'delta_w', 'delta_b_w_decay_up': 'delta_w', 'delta_b_a0': 'delta_w', 'delta_b_w_iclr_up': 'delta_w', 'delta_b_w_gate_up': 'delta_w', 'delta_b_k_k': 'delta_w', 'delta_b_k_a': 'delta_w', 'delta_b_r_k': 'delta_w', 'delta_b_gn_g': 'delta_w', 'delta_b_gn_b': 'delta_w', 'delta_c_w_in': 'delta_w', 'delta_c_conv': 'delta_w', 'delta_c_a_log': 'delta_w', 'delta_c_dt_bias': 'delta_w', 'delta_c_norm_g': 'delta_w', 'new_m_attn_norm': 'new_m', 'new_m_mem_norm': 'new_m', 'new_m_w_mem_kv': 'new_m', 'new_m_w_out': 'new_m', 'new_m_ffn_norm': 'new_m', 'new_m_w_ffn_up': 'new_m', 'new_m_ffn_conv': 'new_m', 'new_m_w_ffn_down': 'new_m', 'new_m_final_norm': 'new_m', 'new_m_a_w_in': 'new_m', 'new_m_a_sinks': 'new_m', 'new_m_b_w_in': 'new_m', 'new_m_b_mu': 'new_m', 'new_m_b_w0': 'new_m', 'new_m_b_w_decay_up': 'new_m', 'new_m_b_a0': 'new_m', 'new_m_b_w_iclr_up': 'new_m', 'new_m_b_w_gate_up': 'new_m', 'new_m_b_k_k': 'new_m', 'new_m_b_k_a': 'new_m', 'new_m_b_r_k': 'new_m', 'new_m_b_gn_g': 'new_m', 'new_m_b_gn_b': 'new_m', 'new_m_c_w_in': 'new_m', 'new_m_c_conv': 'new_m', 'new_m_c_a_log': 'new_m', 'new_m_c_dt_bias': 'new_m', 'new_m_c_norm_g': 'new_m', 'new_v_attn_norm': 'new_v', 'new_v_mem_norm': 'new_v', 'new_v_w_mem_kv': 'new_v', 'new_v_w_out': 'new_v', 'new_v_ffn_norm': 'new_v', 'new_v_w_ffn_up': 'new_v', 'new_v_ffn_conv': 'new_v', 'new_v_w_ffn_down': 'new_v', 'new_v_final_norm': 'new_v', 'new_v_a_w_in': 'new_v', 'new_v_a_sinks': 'new_v', 'new_v_b_w_in': 'new_v', 'new_v_b_mu': 'new_v', 'new_v_b_w0': 'new_v', 'new_v_b_w_decay_up': 'new_v', 'new_v_b_a0': 'new_v', 'new_v_b_w_iclr_up': 'new_v', 'new_v_b_w_gate_up': 'new_v', 'new_v_b_k_k': 'new_v', 'new_v_b_k_a': 'new_v', 'new_v_b_r_k': 'new_v', 'new_v_b_gn_g': 'new_v', 'new_v_b_gn_b': 'new_v', 'new_v_c_w_in': 'new_v', 'new_v_c_conv': 'new_v', 'new_v_c_a_log': 'new_v', 'new_v_c_dt_bias': 'new_v', 'new_v_c_norm_g': 'new_v'}


def _forward(args):
    return _fwd_reference(*[args[k] for k in FWD_PARAMS])


def _output_shape():
    out = _jax.eval_shape(lambda: _forward(_fwd_setup_inputs(0)))
    return out.shape, out.dtype

N_MICROBATCH = 1
ADAM_LR = 0.001
ADAM_B1 = 0.9
ADAM_B2 = 0.999
ADAM_EPS = 1e-08
ADAM_WD = 0.01
ADAM_STEP = 10
PER_EXAMPLE_BATCH_AXIS = {'x': 0, 'mem': 0, 'loss_target': 0}
SHARED_INPUTS = []
_WEIGHT_DTYPES = {'attn_norm': _jnp.float32, 'mem_norm': _jnp.float32, 'w_mem_kv': _jnp.float32, 'w_out': _jnp.float32, 'ffn_norm': _jnp.float32, 'w_ffn_up': _jnp.float32, 'ffn_conv': _jnp.float32, 'w_ffn_down': _jnp.float32, 'final_norm': _jnp.float32, 'a_w_in': _jnp.float32, 'a_sinks': _jnp.float32, 'b_w_in': _jnp.float32, 'b_mu': _jnp.float32, 'b_w0': _jnp.float32, 'b_w_decay_up': _jnp.float32, 'b_a0': _jnp.float32, 'b_w_iclr_up': _jnp.float32, 'b_w_gate_up': _jnp.float32, 'b_k_k': _jnp.float32, 'b_k_a': _jnp.float32, 'b_r_k': _jnp.float32, 'b_gn_g': _jnp.float32, 'b_gn_b': _jnp.float32, 'c_w_in': _jnp.float32, 'c_conv': _jnp.float32, 'c_a_log': _jnp.float32, 'c_dt_bias': _jnp.float32, 'c_norm_g': _jnp.float32}
MOMENT_SCALE = {'attn_norm': 4.425109e-02, 'mem_norm': 5.502230e-03, 'w_mem_kv': 7.546555e-03, 'w_out': 2.635870e-02, 'ffn_norm': 5.061047e-02, 'w_ffn_up': 2.155987e-02, 'ffn_conv': 2.145341e-02, 'w_ffn_down': 3.528865e-02, 'final_norm': 7.999908e+00, 'a_w_in': 3.034824e-02, 'a_sinks': 8.758278e-02, 'b_w_in': 3.866804e-02, 'b_mu': 6.608114e-02, 'b_w0': 1.541681e-02, 'b_w_decay_up': 1.885890e-03, 'b_a0': 1.514665e-02, 'b_w_iclr_up': 1.367875e-02, 'b_w_gate_up': 3.842131e-02, 'b_k_k': 4.419527e-02, 'b_k_a': 4.447093e-02, 'b_r_k': 8.621645e-02, 'b_gn_g': 4.107803e-02, 'b_gn_b': 3.824827e-02, 'c_w_in': 2.642649e-02, 'c_conv': 2.752234e-02, 'c_a_log': 9.696483e-02, 'c_dt_bias': 9.713308e-02, 'c_norm_g': 9.579126e-02}


def _to_microbatches(a, axis):
    t = _jnp.moveaxis(a, axis, 0)
    t = t.reshape((N_MICROBATCH, t.shape[0] // N_MICROBATCH) + t.shape[1:])
    return _jnp.moveaxis(t, 1, axis + 1)


def setup_inputs(seed: int = 0) -> dict:
    inp = _fwd_setup_inputs(seed)
    key = _jax.random.fold_in(_jax.random.key(seed), 7919)
    shape, _ = _output_shape()
    out = dict(inp)
    out["loss_target"] = _jax.random.normal(_jax.random.fold_in(key, 0), shape, _jnp.float32)
    for i, name in enumerate(TWIN_WEIGHTS):
        w = inp[name].astype(_jnp.float32)
        if MOMENT_SCALE is None:
            s = _jnp.sqrt(_jnp.mean(_jnp.square(w)) + 1e-30)
        else:
            s = MOMENT_SCALE[name]
        km, kv = _jax.random.split(_jax.random.fold_in(key, i + 1))
        out[name] = w
        out["m_" + name] = s * _jax.random.normal(km, w.shape, _jnp.float32)
        out["v_" + name] = (s * s) * _jax.random.uniform(kv, w.shape, _jnp.float32, 0.5, 1.5)
    if N_MICROBATCH > 1:
        for name, axis in PER_EXAMPLE_BATCH_AXIS.items():
            out[name] = _to_microbatches(out[name], axis)
    return {'x': out['x'], 'mem': out['mem'], 'attn_norm': out['attn_norm'], 'mem_norm': out['mem_norm'], 'w_mem_kv': out['w_mem_kv'], 'w_out': out['w_out'], 'ffn_norm': out['ffn_norm'], 'w_ffn_up': out['w_ffn_up'], 'ffn_conv': out['ffn_conv'], 'w_ffn_down': out['w_ffn_down'], 'final_norm': out['final_norm'], 'a_w_in': out['a_w_in'], 'a_sinks': out['a_sinks'], 'b_w_in': out['b_w_in'], 'b_mu': out['b_mu'], 'b_w0': out['b_w0'], 'b_w_decay_up': out['b_w_decay_up'], 'b_a0': out['b_a0'], 'b_w_iclr_up': out['b_w_iclr_up'], 'b_w_gate_up': out['b_w_gate_up'], 'b_k_k': out['b_k_k'], 'b_k_a': out['b_k_a'], 'b_r_k': out['b_r_k'], 'b_gn_g': out['b_gn_g'], 'b_gn_b': out['b_gn_b'], 'c_w_in': out['c_w_in'], 'c_conv': out['c_conv'], 'c_a_log': out['c_a_log'], 'c_dt_bias': out['c_dt_bias'], 'c_norm_g': out['c_norm_g'], 'loss_target': out['loss_target'], 'm_attn_norm': out['m_attn_norm'], 'm_mem_norm': out['m_mem_norm'], 'm_w_mem_kv': out['m_w_mem_kv'], 'm_w_out': out['m_w_out'], 'm_ffn_norm': out['m_ffn_norm'], 'm_w_ffn_up': out['m_w_ffn_up'], 'm_ffn_conv': out['m_ffn_conv'], 'm_w_ffn_down': out['m_w_ffn_down'], 'm_final_norm': out['m_final_norm'], 'm_a_w_in': out['m_a_w_in'], 'm_a_sinks': out['m_a_sinks'], 'm_b_w_in': out['m_b_w_in'], 'm_b_mu': out['m_b_mu'], 'm_b_w0': out['m_b_w0'], 'm_b_w_decay_up': out['m_b_w_decay_up'], 'm_b_a0': out['m_b_a0'], 'm_b_w_iclr_up': out['m_b_w_iclr_up'], 'm_b_w_gate_up': out['m_b_w_gate_up'], 'm_b_k_k': out['m_b_k_k'], 'm_b_k_a': out['m_b_k_a'], 'm_b_r_k': out['m_b_r_k'], 'm_b_gn_g': out['m_b_gn_g'], 'm_b_gn_b': out['m_b_gn_b'], 'm_c_w_in': out['m_c_w_in'], 'm_c_conv': out['m_c_conv'], 'm_c_a_log': out['m_c_a_log'], 'm_c_dt_bias': out['m_c_dt_bias'], 'm_c_norm_g': out['m_c_norm_g'], 'v_attn_norm': out['v_attn_norm'], 'v_mem_norm': out['v_mem_norm'], 'v_w_mem_kv': out['v_w_mem_kv'], 'v_w_out': out['v_w_out'], 'v_ffn_norm': out['v_ffn_norm'], 'v_w_ffn_up': out['v_w_ffn_up'], 'v_ffn_conv': out['v_ffn_conv'], 'v_w_ffn_down': out['v_w_ffn_down'], 'v_final_norm': out['v_final_norm'], 'v_a_w_in': out['v_a_w_in'], 'v_a_sinks': out['v_a_sinks'], 'v_b_w_in': out['v_b_w_in'], 'v_b_mu': out['v_b_mu'], 'v_b_w0': out['v_b_w0'], 'v_b_w_decay_up': out['v_b_w_decay_up'], 'v_b_a0': out['v_b_a0'], 'v_b_w_iclr_up': out['v_b_w_iclr_up'], 'v_b_w_gate_up': out['v_b_w_gate_up'], 'v_b_k_k': out['v_b_k_k'], 'v_b_k_a': out['v_b_k_a'], 'v_b_r_k': out['v_b_r_k'], 'v_b_gn_g': out['v_b_gn_g'], 'v_b_gn_b': out['v_b_gn_b'], 'v_c_w_in': out['v_c_w_in'], 'v_c_conv': out['v_c_conv'], 'v_c_a_log': out['v_c_a_log'], 'v_c_dt_bias': out['v_c_dt_bias'], 'v_c_norm_g': out['v_c_norm_g']}


def _loss(weights, diff, rest, loss_target):
    with _jax.named_scope("forward"):
        args = {**rest, TWIN_DIFF_INPUT: diff, **{k: w.astype(_WEIGHT_DTYPES[k]) for k, w in weights.items()}}
        y = _forward(args)
    with _jax.named_scope("loss_head"):
        err = _jnp.square(y.astype(_jnp.float32) - loss_target)
        return 0.5 * _jnp.sum(_jnp.mean(err, axis=-1)) if err.ndim else 0.5 * err


def _adamw(w, g, m, v):
    m = ADAM_B1 * m + (1.0 - ADAM_B1) * g
    v = ADAM_B2 * v + (1.0 - ADAM_B2) * _jnp.square(g)
    m_hat = m / (1.0 - ADAM_B1 ** ADAM_STEP)
    v_hat = v / (1.0 - ADAM_B2 ** ADAM_STEP)
    delta = -ADAM_LR * (m_hat / (_jnp.sqrt(v_hat) + ADAM_EPS) + ADAM_WD * w)
    return delta, m, v


def reference(x, mem, attn_norm, mem_norm, w_mem_kv, w_out, ffn_norm, w_ffn_up, ffn_conv, w_ffn_down, final_norm, a_w_in, a_sinks, b_w_in, b_mu, b_w0, b_w_decay_up, b_a0, b_w_iclr_up, b_w_gate_up, b_k_k, b_k_a, b_r_k, b_gn_g, b_gn_b, c_w_in, c_conv, c_a_log, c_dt_bias, c_norm_g, loss_target, m_attn_norm, m_mem_norm, m_w_mem_kv, m_w_out, m_ffn_norm, m_w_ffn_up, m_ffn_conv, m_w_ffn_down, m_final_norm, m_a_w_in, m_a_sinks, m_b_w_in, m_b_mu, m_b_w0, m_b_w_decay_up, m_b_a0, m_b_w_iclr_up, m_b_w_gate_up, m_b_k_k, m_b_k_a, m_b_r_k, m_b_gn_g, m_b_gn_b, m_c_w_in, m_c_conv, m_c_a_log, m_c_dt_bias, m_c_norm_g, v_attn_norm, v_mem_norm, v_w_mem_kv, v_w_out, v_ffn_norm, v_w_ffn_up, v_ffn_conv, v_w_ffn_down, v_final_norm, v_a_w_in, v_a_sinks, v_b_w_in, v_b_mu, v_b_w0, v_b_w_decay_up, v_b_a0, v_b_w_iclr_up, v_b_w_gate_up, v_b_k_k, v_b_k_a, v_b_r_k, v_b_gn_g, v_b_gn_b, v_c_w_in, v_c_conv, v_c_a_log, v_c_dt_bias, v_c_norm_g):
    given = dict(x=x, mem=mem, attn_norm=attn_norm, mem_norm=mem_norm, w_mem_kv=w_mem_kv, w_out=w_out, ffn_norm=ffn_norm, w_ffn_up=w_ffn_up, ffn_conv=ffn_conv, w_ffn_down=w_ffn_down, final_norm=final_norm, a_w_in=a_w_in, a_sinks=a_sinks, b_w_in=b_w_in, b_mu=b_mu, b_w0=b_w0, b_w_decay_up=b_w_decay_up, b_a0=b_a0, b_w_iclr_up=b_w_iclr_up, b_w_gate_up=b_w_gate_up, b_k_k=b_k_k, b_k_a=b_k_a, b_r_k=b_r_k, b_gn_g=b_gn_g, b_gn_b=b_gn_b, c_w_in=c_w_in, c_conv=c_conv, c_a_log=c_a_log, c_dt_bias=c_dt_bias, c_norm_g=c_norm_g, loss_target=loss_target, m_attn_norm=m_attn_norm, m_mem_norm=m_mem_norm, m_w_mem_kv=m_w_mem_kv, m_w_out=m_w_out, m_ffn_norm=m_ffn_norm, m_w_ffn_up=m_w_ffn_up, m_ffn_conv=m_ffn_conv, m_w_ffn_down=m_w_ffn_down, m_final_norm=m_final_norm, m_a_w_in=m_a_w_in, m_a_sinks=m_a_sinks, m_b_w_in=m_b_w_in, m_b_mu=m_b_mu, m_b_w0=m_b_w0, m_b_w_decay_up=m_b_w_decay_up, m_b_a0=m_b_a0, m_b_w_iclr_up=m_b_w_iclr_up, m_b_w_gate_up=m_b_w_gate_up, m_b_k_k=m_b_k_k, m_b_k_a=m_b_k_a, m_b_r_k=m_b_r_k, m_b_gn_g=m_b_gn_g, m_b_gn_b=m_b_gn_b, m_c_w_in=m_c_w_in, m_c_conv=m_c_conv, m_c_a_log=m_c_a_log, m_c_dt_bias=m_c_dt_bias, m_c_norm_g=m_c_norm_g, v_attn_norm=v_attn_norm, v_mem_norm=v_mem_norm, v_w_mem_kv=v_w_mem_kv, v_w_out=v_w_out, v_ffn_norm=v_ffn_norm, v_w_ffn_up=v_w_ffn_up, v_ffn_conv=v_ffn_conv, v_w_ffn_down=v_w_ffn_down, v_final_norm=v_final_norm, v_a_w_in=v_a_w_in, v_a_sinks=v_a_sinks, v_b_w_in=v_b_w_in, v_b_mu=v_b_mu, v_b_w0=v_b_w0, v_b_w_decay_up=v_b_w_decay_up, v_b_a0=v_b_a0, v_b_w_iclr_up=v_b_w_iclr_up, v_b_w_gate_up=v_b_w_gate_up, v_b_k_k=v_b_k_k, v_b_k_a=v_b_k_a, v_b_r_k=v_b_r_k, v_b_gn_g=v_b_gn_g, v_b_gn_b=v_b_gn_b, v_c_w_in=v_c_w_in, v_c_conv=v_c_conv, v_c_a_log=v_c_a_log, v_c_dt_bias=v_c_dt_bias, v_c_norm_g=v_c_norm_g)
    weights = {n: given[n] for n in TWIN_WEIGHTS}
    shared = {n: given[n] for n in SHARED_INPUTS}
    per_example = {n: given[n] for n in ['x', 'mem']}
    grad_fn = _jax.value_and_grad(_loss, argnums=(0, 1))

    def one_microbatch(ex, loss_target):
        ex = dict(ex)
        diff = ex.pop(TWIN_DIFF_INPUT)
        return grad_fn(weights, diff, {**shared, **ex}, loss_target)

    if N_MICROBATCH == 1:
        loss, (grad_w, grad_x) = one_microbatch(per_example, given["loss_target"])
    else:
        def body(carry, xs):
            loss_sum, grad_sum = carry
            l_k, (gw_k, gx_k) = one_microbatch(xs[0], xs[1])
            with _jax.named_scope("update"):
                return (loss_sum + l_k, _jax.tree.map(_jnp.add, grad_sum, gw_k)), gx_k

        init = (_jnp.zeros((), _jnp.float32), _jax.tree.map(_jnp.zeros_like, weights))
        (loss, grad_w), grad_x = _jax.lax.scan(body, init, (per_example, given["loss_target"]))
    with _jax.named_scope("update"):
        delta_w, new_m, new_v = {}, {}, {}
        for n in TWIN_WEIGHTS:
            delta_w[n], new_m[n], new_v[n] = _adamw(weights[n], grad_w[n], given["m_" + n], given["v_" + n])
    return (loss, grad_x, *[grad_w[n] for n in TWIN_WEIGHTS], *[delta_w[n] for n in TWIN_WEIGHTS],
            *[new_m[n] for n in TWIN_WEIGHTS], *[new_v[n] for n in TWIN_WEIGHTS])
```

```python
import functools
import math

import numpy as np
import jax
import jax.numpy as jnp
from jax import lax
from jax.experimental import pallas as pl
from jax.experimental.pallas import tpu as pltpu

F32, BF16 = jnp.float32, jnp.bfloat16
HI = lax.Precision.HIGHEST
V7X_VMEM_BYTES = 64 * 1024 * 1024
VMEM_LIMIT = V7X_VMEM_BYTES - 8 * 1024 * 1024
SUBLANES, LANES = 8, 128
N_DEV = 8

D_MODEL = 2048
DEPTH = 4
MIX_WIDTH = 1536
MEM_HEADS, MEM_HEAD_DIM, MEM_WIDTH = 4, 128, 512
NORM_EPS = 1e-6
SWA_HEAD_DIM, SWA_Q_HEADS, SWA_KV_HEADS, SWA_GROUP, SWA_BLOCK = 64, 24, 4, 6, 128
RWKV_HEADS, RWKV_HEAD_DIM, RWKV_GN_EPS = 24, 64, 64e-5
RWKV_DECAY_RANK, RWKV_ICLR_RANK, RWKV_GATE_RANK = 96, 96, 256
GDN_HEAD_DIM, GDN_V_HEADS, GDN_QK_HEADS, GDN_CONV, GDN_CHUNK = 128, 12, 6, 4, 64
GDN_QK_WIDTH = GDN_QK_HEADS * GDN_HEAD_DIM
GDN_CONV_WIDTH = 2 * GDN_QK_WIDTH + MIX_WIDTH
D_FF, FFN_CONV = 5632, 3
ADAM_LR, ADAM_B1, ADAM_B2, ADAM_EPS, ADAM_WD, ADAM_STEP = 0.001, 0.9, 0.999, 1e-08, 0.01, 10
MESH = pl.DeviceIdType.MESH


def _cp(sem=None):
    return pltpu.CompilerParams(dimension_semantics=sem, vmem_limit_bytes=VMEM_LIMIT)


def _tile(n, cands):
    for c in cands:
        if n % c == 0:
            return c
    return n


def _dot(a, b):
    return jnp.dot(a, b, precision=HI, preferred_element_type=F32)


def _dot_nt(a, b):
    return lax.dot_general(a, b, (((1,), (1,)), ((), ())), precision=HI, preferred_element_type=F32)


def _dot_tn(a, b):
    return lax.dot_general(a, b, (((0,), (0,)), ((), ())), precision=HI, preferred_element_type=F32)


def _sigmoid(x):
    return 1.0 / (1.0 + jnp.exp(-x))


def _softplus(x):
    return jnp.maximum(x, 0.0) + jnp.log(1.0 + jnp.exp(-jnp.abs(x)))


def _silu(x):
    return x * _sigmoid(x)


def mm(a, b, *, ta=False, tb=False, res=None, out_dtype=F32, name):
    (k_a, m) = a.shape if ta else a.shape[::-1]
    (k_b, n) = b.shape[::-1] if tb else b.shape
    assert k_a == k_b, (a.shape, b.shape, ta, tb)
    kdim = k_a
    tm = _tile(m, (1024, 512, 256))
    tn = _tile(n, (1024, 768, 512, 384, 256, 128))
    tk = _tile(kdim, (512, 256, 128))
    nk = kdim // tk
    dims = (((0 if ta else 1,), (1 if tb else 0,)), ((), ()))

    def body(*refs):
        if res is None:
            a_ref, b_ref, o_ref, acc = refs
        else:
            a_ref, b_ref, r_ref, o_ref, acc = refs
        kk = pl.program_id(2)

        @pl.when(kk == 0)
        def _():
            acc[...] = jnp.zeros_like(acc)

        acc[...] += lax.dot_general(a_ref[...].astype(BF16), b_ref[...].astype(BF16), dims,
                                    preferred_element_type=F32)

        @pl.when(kk == nk - 1)
        def _():
            out = acc[...] if res is None else acc[...] + r_ref[...]
            o_ref[...] = out.astype(o_ref.dtype)

    a_spec = pl.BlockSpec((tk, tm), lambda i, j, k: (k, i)) if ta else pl.BlockSpec((tm, tk), lambda i, j, k: (i, k))
    b_spec = pl.BlockSpec((tn, tk), lambda i, j, k: (j, k)) if tb else pl.BlockSpec((tk, tn), lambda i, j, k: (k, j))
    o_spec = pl.BlockSpec((tm, tn), lambda i, j, k: (i, j))
    in_specs, args = [a_spec, b_spec], [a, b]
    if res is not None:
        in_specs.append(o_spec)
        args.append(res)
    return pl.pallas_call(
        body, grid=(m // tm, n // tn, nk), in_specs=in_specs, out_specs=o_spec,
        out_shape=jax.ShapeDtypeStruct((m, n), out_dtype), scratch_shapes=[pltpu.VMEM((tm, tn), F32)],
        compiler_params=_cp(("parallel", "parallel", "arbitrary")), name=name)(*args)


def _coords():
    return lax.axis_index("x"), lax.axis_index("y"), lax.axis_index("c")


def _block_index(p):
    return 4 * p[0] + 2 * p[1] + p[2]


def all_gather_many(xs, name):
    n = len(xs)

    def body(*refs):
        x_refs, o_refs = refs[:n], refs[n:2 * n]
        send, recv, loc = refs[2 * n:]
        x, y, c = _coords()
        me, sib = (x, y, c), (x, y, 1 - c)
        chips = [(1 - x, y), (x, 1 - y), (1 - x, 1 - y)]

        def cp(i, k, block, to, src=None):
            dst = o_refs[i].at[_block_index(block)]
            return pltpu.make_async_remote_copy(
                src_ref=dst if src is None else src, dst_ref=dst, send_sem=send.at[i, k], recv_sem=recv.at[i, k],
                device_id=to, device_id_type=MESH)

        mine = [pltpu.make_async_copy(x_refs[i], o_refs[i].at[_block_index(me)], loc.at[i]) for i in range(n)]
        for m_ in mine:
            m_.start()
        first = []
        for i in range(n):
            first.append(cp(i, 0, me, sib, src=x_refs[i]))
            for j, chip in enumerate(chips):
                first.append(cp(i, 1 + j, me, (*chip, c), src=x_refs[i]))
        for f in first:
            f.start()
        passed = []
        for j, chip in enumerate(chips):
            for i in range(n):
                cp(i, 1 + j, (*chip, c), me).wait_recv()
                p = cp(i, 4 + j, (*chip, c), sib)
                p.start()
                passed.append(p)
        for i in range(n):
            cp(i, 0, sib, me).wait_recv()
            for j, chip in enumerate(chips):
                cp(i, 4 + j, (*chip, 1 - c), me).wait_recv()
        for f in first + passed:
            f.wait_send()
        for m_ in mine:
            m_.wait()

    any_spec = pl.BlockSpec(memory_space=pl.ANY)
    return pl.pallas_call(
        body, in_specs=[any_spec] * n, out_specs=[any_spec] * n,
        out_shape=[jax.ShapeDtypeStruct((N_DEV,) + x.shape, x.dtype) for x in xs],
        scratch_shapes=[pltpu.SemaphoreType.DMA((n, 7)), pltpu.SemaphoreType.DMA((n, 7)),
                        pltpu.SemaphoreType.DMA((n,))],
        name=name)(*xs)


def all_to_all_many(xs, name):
    n = len(xs)

    def body(*refs):
        x_refs, o_refs = refs[:n], refs[n:2 * n]
        send, recv, loc = refs[2 * n:]
        x, y, c = _coords()
        me = _block_index((x, y, c))
        mine = [pltpu.make_async_copy(x_refs[i].at[me], o_refs[i].at[me], loc.at[i]) for i in range(n)]
        for m_ in mine:
            m_.start()
        copies = []
        for r in range(1, N_DEV):
            peer = (1 - x if r & 4 else x, 1 - y if r & 2 else y, 1 - c if r & 1 else c)
            pidx = _block_index(peer)
            for i in range(n):
                copies.append((
                    pltpu.make_async_remote_copy(
                        src_ref=x_refs[i].at[pidx], dst_ref=o_refs[i].at[me], send_sem=send.at[i, r - 1],
                        recv_sem=recv.at[i, r - 1], device_id=peer, device_id_type=MESH),
                    pltpu.make_async_remote_copy(
                        src_ref=x_refs[i].at[pidx], dst_ref=o_refs[i].at[pidx], send_sem=send.at[i, r - 1],
                        recv_sem=recv.at[i, r - 1], device_id=peer, device_id_type=MESH)))
        for s, _ in copies:
            s.start()
        for s, w in copies:
            w.wait_recv()
            s.wait_send()
        for m_ in mine:
            m_.wait()

    any_spec = pl.BlockSpec(memory_space=pl.ANY)
    return pl.pallas_call(
        body, in_specs=[any_spec] * n, out_specs=[any_spec] * n,
        out_shape=[jax.ShapeDtypeStruct(x.shape, x.dtype) for x in xs],
        scratch_shapes=[pltpu.SemaphoreType.DMA((n, 7)), pltpu.SemaphoreType.DMA((n, 7)),
                        pltpu.SemaphoreType.DMA((n,))],
        name=name)(*xs)


def adamw_sum(pieces, w, m, v, name):
    rows, cols = w.shape
    tr = _tile(rows, (128, 64, 32, 16, 8))
    c1 = 1.0 - ADAM_B1 ** ADAM_STEP
    c2 = 1.0 - ADAM_B2 ** ADAM_STEP

    def body(p_ref, w_ref, m_ref, v_ref, g_out, d_out, m_out, v_out):
        g = p_ref[0].astype(F32)
        for s in range(1, N_DEV):
            g = g + p_ref[s].astype(F32)
        m_new = ADAM_B1 * m_ref[...] + (1.0 - ADAM_B1) * g
        v_new = ADAM_B2 * v_ref[...] + (1.0 - ADAM_B2) * (g * g)
        m_hat = m_new / c1
        v_hat = v_new / c2
        g_out[...] = g
        d_out[...] = -ADAM_LR * (m_hat / (jnp.sqrt(v_hat) + ADAM_EPS) + ADAM_WD * w_ref[...])
        m_out[...] = m_new
        v_out[...] = v_new

    spec = pl.BlockSpec((tr, cols), lambda i: (i, 0))
    out = jax.ShapeDtypeStruct((rows, cols), F32)
    return pl.pallas_call(
        body, grid=(rows // tr,), in_specs=[pl.BlockSpec((N_DEV, tr, cols), lambda i: (0, i, 0)), spec, spec, spec],
        out_specs=[spec] * 4, out_shape=[out] * 4, compiler_params=_cp(("parallel",)), name=name)(pieces, w, m, v)


def rmsnorm_fwd(x, g, out_dtype, name):
    s, d = x.shape
    tr = _tile(s, (256, 128, 64, 32, 16))

    def body(x_ref, g_ref, o_ref):
        xv = x_ref[...]
        rstd = lax.rsqrt(jnp.mean(xv * xv, axis=-1, keepdims=True) + NORM_EPS)
        o_ref[...] = (xv * rstd * g_ref[...]).astype(o_ref.dtype)

    return pl.pallas_call(
        body, grid=(s // tr,), in_specs=[pl.BlockSpec((tr, d), lambda i: (i, 0)), pl.BlockSpec((1, d), lambda i: (0, 0))],
        out_specs=pl.BlockSpec((tr, d), lambda i: (i, 0)), out_shape=jax.ShapeDtypeStruct((s, d), out_dtype),
        compiler_params=_cp(("parallel",)), name=name)(x, g.reshape(1, d))


def rmsnorm_bwd(x, g, dh, dres, name):
    s, d = x.shape
    tr = _tile(s, (256, 128, 64, 32, 16))

    def body(*refs):
        if dres is None:
            x_ref, g_ref, dh_ref, dx_ref, dg_ref = refs
        else:
            x_ref, g_ref, dh_ref, dr_ref, dx_ref, dg_ref = refs
        xv = x_ref[...]
        rstd = lax.rsqrt(jnp.mean(xv * xv, axis=-1, keepdims=True) + NORM_EPS)
        xhat = xv * rstd
        dhv = dh_ref[...].astype(F32)
        dhg = dhv * g_ref[...]
        dx = rstd * (dhg - xhat * jnp.mean(dhg * xhat, axis=-1, keepdims=True))
        if dres is not None:
            dx = dx + dr_ref[...]
        dx_ref[...] = dx

        @pl.when(pl.program_id(0) == 0)
        def _():
            dg_ref[...] = jnp.zeros_like(dg_ref)

        dg_ref[...] += jnp.sum(dhv * xhat, axis=0, keepdims=True)

    row = pl.BlockSpec((tr, d), lambda i: (i, 0))
    vec = pl.BlockSpec((1, d), lambda i: (0, 0))
    ins = [x, g.reshape(1, d), dh] + ([] if dres is None else [dres])
    dx, dg = pl.pallas_call(
        body, grid=(s // tr,), in_specs=[row, vec, row] + ([] if dres is None else [row]), out_specs=[row, vec],
        out_shape=[jax.ShapeDtypeStruct((s, d), F32), jax.ShapeDtypeStruct((1, d), F32)],
        compiler_params=_cp(("arbitrary",)), name=name)(*ins)
    return dx, dg.reshape(d)


def final_loss(x, g, target, name):
    s, d = x.shape
    tr = _tile(s, (256, 128, 64, 32, 16))

    def body(x_ref, g_ref, t_ref, l_ref, dx_ref, dg_ref):
        xv = x_ref[...]
        rstd = lax.rsqrt(jnp.mean(xv * xv, axis=-1, keepdims=True) + NORM_EPS)
        xhat = xv * rstd
        err = xhat * g_ref[...] - t_ref[...]
        dy = err * (1.0 / d)
        dhg = dy * g_ref[...]
        dx_ref[...] = rstd * (dhg - xhat * jnp.mean(dhg * xhat, axis=-1, keepdims=True))

        @pl.when(pl.program_id(0) == 0)
        def _():
            dg_ref[...] = jnp.zeros_like(dg_ref)
            l_ref[...] = jnp.zeros_like(l_ref)

        dg_ref[...] += jnp.sum(dy * xhat, axis=0, keepdims=True)
        part = 0.5 * jnp.sum(jnp.mean(err * err, axis=-1, keepdims=True), axis=0, keepdims=True)
        l_ref[...] += jnp.broadcast_to(part, l_ref.shape)

    row = pl.BlockSpec((tr, d), lambda i: (i, 0))
    vec = pl.BlockSpec((1, d), lambda i: (0, 0))
    lspec = pl.BlockSpec((1, LANES), lambda i: (0, 0))
    loss, dx, dg = pl.pallas_call(
        body, grid=(s // tr,), in_specs=[row, vec, row], out_specs=[lspec, row, vec],
        out_shape=[jax.ShapeDtypeStruct((1, LANES), F32), jax.ShapeDtypeStruct((s, d), F32),
                   jax.ShapeDtypeStruct((1, d), F32)],
        compiler_params=_cp(("arbitrary",)), name=name)(x, g.reshape(1, d), target)
    return loss[0, 0], dx, dg.reshape(d)


def _shift_down(tile, prev8, j):
    if j == 0:
        return tile
    rt = pltpu.roll(tile, j, 0)
    rp = pltpu.roll(prev8, j, 0)
    rows = lax.broadcasted_iota(jnp.int32, prev8.shape, 0)
    top = jnp.where(rows < j, rp, rt[:SUBLANES])
    return jnp.concatenate([top, rt[SUBLANES:]], axis=0)


def _shift_up(tile, next8, j):
    if j == 0:
        return tile
    t = tile.shape[0]
    rt = pltpu.roll(tile, t - j, 0)
    rn = pltpu.roll(next8, SUBLANES - j, 0)
    rows = lax.broadcasted_iota(jnp.int32, next8.shape, 0)
    bot = jnp.where(rows >= SUBLANES - j, rn, rt[t - SUBLANES:])
    return jnp.concatenate([rt[:t - SUBLANES], bot], axis=0)


def _halo_specs(t_rows, s_rows, cols, col_of):
    per, last = t_rows // SUBLANES, s_rows // SUBLANES - 1
    prev = pl.BlockSpec((SUBLANES, cols), lambda j, i: (jnp.maximum(i * per - 1, 0), col_of(j)))
    nxt = pl.BlockSpec((SUBLANES, cols), lambda j, i: (jnp.minimum((i + 1) * per, last), col_of(j)))
    return prev, nxt


def ffn_act_fwd(u0, conv, name):
    s, two_f = u0.shape
    f = two_f // 2
    t, c = _tile(s, (256, 128, 64)), 512
    nc = f // c

    def body(g_ref, v_ref, gp_ref, vp_ref, wg_ref, wv_ref, a_ref):
        first = (pl.program_id(1) > 0).astype(F32)

        def conv_of(x_ref, p_ref, w_ref):
            x, p = x_ref[...], p_ref[...] * first
            return (w_ref[0:1, :] * _shift_down(x, p, 2) + w_ref[1:2, :] * _shift_down(x, p, 1) + w_ref[2:3, :] * x)

        ug = conv_of(g_ref, gp_ref, wg_ref)
        uv = conv_of(v_ref, vp_ref, wv_ref)
        a_ref[...] = (_silu(ug) * uv).astype(a_ref.dtype)

    gate = pl.BlockSpec((t, c), lambda j, i: (i, j))
    val = pl.BlockSpec((t, c), lambda j, i: (i, j + nc))
    gp, _ = _halo_specs(t, s, c, lambda j: j)
    vp, _ = _halo_specs(t, s, c, lambda j: j + nc)
    wg = pl.BlockSpec((FFN_CONV, c), lambda j, i: (0, j))
    wv = pl.BlockSpec((FFN_CONV, c), lambda j, i: (0, j + nc))
    return pl.pallas_call(
        body, grid=(nc, s // t), in_specs=[gate, val, gp, vp, wg, wv], out_specs=pl.BlockSpec((t, c), lambda j, i: (i, j)),
        out_shape=jax.ShapeDtypeStruct((s, f), BF16), compiler_params=_cp(("parallel", "parallel")),
        name=name)(u0, u0, u0, u0, conv, conv)


def ffn_act_bwd(u0, conv, da, name):
    s, two_f = u0.shape
    f = two_f // 2
    t, c = _tile(s, (256, 128, 64)), 512
    nc, nt = f // c, s // t

    def body(g_ref, v_ref, gp_ref, vp_ref, gn_ref, vn_ref, wg_ref, wv_ref, da_ref, dan_ref,
             dg_ref, dv_ref, dwg_ref, dwv_ref):
        i = pl.program_id(1)
        first, last = (i > 0).astype(F32), (i < nt - 1).astype(F32)
        zeros8 = jnp.zeros((SUBLANES, c), F32)

        def ext(x_ref, p_ref, n_ref):
            return jnp.concatenate([p_ref[...] * first, x_ref[...], n_ref[...] * last], axis=0)

        def taps(e):
            return pltpu.roll(e, 2, 0), pltpu.roll(e, 1, 0), e

        def conv_of(sh, w_ref):
            return w_ref[0:1, :] * sh[0] + w_ref[1:2, :] * sh[1] + w_ref[2:3, :] * sh[2]

        def conv_t(du, w_ref):
            n = du.shape[0]
            return w_ref[2:3, :] * du + w_ref[1:2, :] * pltpu.roll(du, n - 1, 0) + w_ref[0:1, :] * pltpu.roll(du, n - 2, 0)

        sg, sv = taps(ext(g_ref, gp_ref, gn_ref)), taps(ext(v_ref, vp_ref, vn_ref))
        ug, uv = conv_of(sg, wg_ref), conv_of(sv, wv_ref)
        dae = jnp.concatenate([zeros8, da_ref[...], dan_ref[...] * last], axis=0)
        sig = _sigmoid(ug)
        dug = dae * uv * (sig * (1.0 + ug * (1.0 - sig)))
        duv = dae * (ug * sig)
        dg_ref[...] = conv_t(dug, wg_ref)[SUBLANES:t + SUBLANES].astype(dg_ref.dtype)
        dv_ref[...] = conv_t(duv, wv_ref)[SUBLANES:t + SUBLANES].astype(dv_ref.dtype)

        @pl.when(i == 0)
        def _():
            dwg_ref[...] = jnp.zeros_like(dwg_ref)
            dwv_ref[...] = jnp.zeros_like(dwv_ref)

        def dconv(du, sh):
            d = du[SUBLANES:t + SUBLANES]
            return jnp.concatenate([jnp.sum(d * x[SUBLANES:t + SUBLANES], axis=0, keepdims=True) for x in sh], axis=0)

        dwg_ref[...] += dconv(dug, sg)
        dwv_ref[...] += dconv(duv, sv)

    gate = pl.BlockSpec((t, c), lambda j, i: (i, j))
    val = pl.BlockSpec((t, c), lambda j, i: (i, j + nc))
    gp, gn = _halo_specs(t, s, c, lambda j: j)
    vp, vn = _halo_specs(t, s, c, lambda j: j + nc)
    wg = pl.BlockSpec((FFN_CONV, c), lambda j, i: (0, j))
    wv = pl.BlockSpec((FFN_CONV, c), lambda j, i: (0, j + nc))
    wout = pl.BlockSpec((FFN_CONV, c), lambda j, i: (0, j))
    half = jax.ShapeDtypeStruct((s, f), BF16)
    dwh = jax.ShapeDtypeStruct((FFN_CONV, f), F32)
    return pl.pallas_call(
        body, grid=(nc, nt), in_specs=[gate, val, gp, vp, gn, vn, wg, wv, gate, gn],
        out_specs=[gate, gate, wout, wout], out_shape=[half, half, dwh, dwh],
        compiler_params=_cp(("parallel", "arbitrary")), name=name)(u0, u0, u0, u0, u0, u0, conv, conv, da, da)


def _softmax_rows(s, extra=None):
    m = jnp.max(s, axis=-1, keepdims=True)
    if extra is not None:
        m = jnp.maximum(m, extra)
    m = lax.stop_gradient(m)
    e = jnp.exp(s - m)
    den = jnp.sum(e, axis=-1, keepdims=True)
    if extra is not None:
        den = den + jnp.exp(extra - m)
    return e / den


def _mem_attn_fn(qs, ks, vs):
    outs = []
    for q, k, v in zip(qs, ks, vs):
        p = _softmax_rows(_dot_nt(q, k) * (MEM_HEAD_DIM ** -0.5))
        outs.append(_dot(p, v))
    return outs


def _mem_heads(q_ref, kv_ref):
    d = MEM_HEAD_DIM
    qs = [q_ref[:, h * d:(h + 1) * d] for h in range(MEM_HEADS)]
    ks = [kv_ref[:, h * d:(h + 1) * d] for h in range(MEM_HEADS)]
    vs = [kv_ref[:, MEM_WIDTH + h * d:MEM_WIDTH + (h + 1) * d] for h in range(MEM_HEADS)]
    return qs, ks, vs


def mem_attn_fwd(p, q_col, kv, name):
    s = p.shape[0]
    t = _tile(s, (256, 128))
    m = kv.shape[0]

    def body(q_ref, kv_ref, o_ref):
        outs = _mem_attn_fn(*_mem_heads(q_ref, kv_ref))
        o_ref[...] = jnp.concatenate(outs, axis=1).astype(o_ref.dtype)

    return pl.pallas_call(
        body, grid=(s // t,),
        in_specs=[pl.BlockSpec((t, MEM_WIDTH), lambda i: (i, q_col // MEM_WIDTH)),
                  pl.BlockSpec((m, 2 * MEM_WIDTH), lambda i: (0, 0))],
        out_specs=pl.BlockSpec((t, MEM_WIDTH), lambda i: (i, 0)), out_shape=jax.ShapeDtypeStruct((s, MEM_WIDTH), BF16),
        compiler_params=_cp(("parallel",)), name=name)(p, kv)


def mem_attn_bwd(p, q_col, kv, dcat, name):
    s = p.shape[0]
    t = _tile(s, (256, 128))
    m = kv.shape[0]
    d = MEM_HEAD_DIM

    def body(q_ref, kv_ref, dy_ref, dq_ref, dkv_ref):
        qs, ks, vs = _mem_heads(q_ref, kv_ref)
        _, vjp = jax.vjp(_mem_attn_fn, qs, ks, vs)
        dqs, dks, dvs = vjp([dy_ref[:, h * d:(h + 1) * d] for h in range(MEM_HEADS)])
        dq_ref[...] = jnp.concatenate(dqs, axis=1).astype(dq_ref.dtype)

        @pl.when(pl.program_id(0) == 0)
        def _():
            dkv_ref[...] = jnp.zeros_like(dkv_ref)

        dkv_ref[...] += jnp.concatenate(dks + dvs, axis=1)

    return pl.pallas_call(
        body, grid=(s // t,),
        in_specs=[pl.BlockSpec((t, MEM_WIDTH), lambda i: (i, q_col // MEM_WIDTH)),
                  pl.BlockSpec((m, 2 * MEM_WIDTH), lambda i: (0, 0)),
                  pl.BlockSpec((t, MEM_WIDTH), lambda i: (i, MIX_WIDTH // MEM_WIDTH))],
        out_specs=[pl.BlockSpec((t, MEM_WIDTH), lambda i: (i, 0)), pl.BlockSpec((m, 2 * MEM_WIDTH), lambda i: (0, 0))],
        out_shape=[jax.ShapeDtypeStruct((s, MEM_WIDTH), BF16), jax.ShapeDtypeStruct((m, 2 * MEM_WIDTH), F32)],
        compiler_params=_cp(("arbitrary",)), name=name)(p, kv, dcat)


def _swa_fn(qs, kcs, kps, vcs, vps, sinks, not_first):
    t = SWA_BLOCK
    qi = lax.broadcasted_iota(jnp.int32, (t, 2 * t), 0)
    kj = lax.broadcasted_iota(jnp.int32, (t, 2 * t), 1)
    dist = t + qi - kj
    valid = (dist >= 0) & (dist < t) & ((kj >= t) | not_first)
    distf = dist.astype(F32)
    outs = []
    for kh in range(SWA_KV_HEADS):
        kb = jnp.concatenate([kps[kh], kcs[kh]], axis=0)
        vb = jnp.concatenate([vps[kh], vcs[kh]], axis=0)
        for g in range(SWA_GROUP):
            h = kh * SWA_GROUP + g
            slope = 2.0 ** (-8.0 * (h + 1) / SWA_Q_HEADS)
            sc = _dot_nt(qs[h], kb) * (SWA_HEAD_DIM ** -0.5) - slope * distf
            sc = jnp.where(valid, sc, -jnp.inf)
            outs.append(_dot(_softmax_rows(sc, extra=sinks[h]), vb))
    return outs


def _swa_args(q_ref, kc_ref, kp_ref, vc_ref, vp_ref, sink_ref):
    d = SWA_HEAD_DIM
    qs = [q_ref[:, h * d:(h + 1) * d] for h in range(SWA_Q_HEADS)]
    per_kv = lambda ref: [ref[:, h * d:(h + 1) * d] for h in range(SWA_KV_HEADS)]
    sinks = [sink_ref[0:1, h:h + 1] for h in range(SWA_Q_HEADS)]
    return qs, per_kv(kc_ref), per_kv(kp_ref), per_kv(vc_ref), per_kv(vp_ref), sinks


def _swa_specs(nb, order):
    t, kvw = SWA_BLOCK, SWA_KV_HEADS * SWA_HEAD_DIM
    k_col, v_col = MIX_WIDTH // kvw, MIX_WIDTH // kvw + 1
    q = pl.BlockSpec((t, MIX_WIDTH), lambda n: (order(n), 0))
    kc = pl.BlockSpec((t, kvw), lambda n: (order(n), k_col))
    kp = pl.BlockSpec((t, kvw), lambda n: (jnp.maximum(order(n) - 1, 0), k_col))
    vc = pl.BlockSpec((t, kvw), lambda n: (order(n), v_col))
    vp = pl.BlockSpec((t, kvw), lambda n: (jnp.maximum(order(n) - 1, 0), v_col))
    sink = pl.BlockSpec((1, LANES), lambda n: (0, 0))
    return [q, kc, kp, vc, vp, sink]


def _pad_lanes(v):
    return jnp.pad(v.reshape(1, -1), ((0, 0), (0, LANES - v.size)))


def swa_fwd(p, sinks, name):
    s = p.shape[0]
    nb = s // SWA_BLOCK

    def body(q_ref, kc_ref, kp_ref, vc_ref, vp_ref, sink_ref, o_ref):
        outs = _swa_fn(*_swa_args(q_ref, kc_ref, kp_ref, vc_ref, vp_ref, sink_ref), pl.program_id(0) > 0)
        o_ref[...] = jnp.concatenate(outs, axis=1).astype(o_ref.dtype)

    return pl.pallas_call(
        body, grid=(nb,), in_specs=_swa_specs(nb, lambda n: n),
        out_specs=pl.BlockSpec((SWA_BLOCK, MIX_WIDTH), lambda n: (n, 0)),
        out_shape=jax.ShapeDtypeStruct((s, MIX_WIDTH), BF16), compiler_params=_cp(("parallel",)),
        name=name)(p, p, p, p, p, _pad_lanes(sinks))


def swa_bwd(p, sinks, dcat, name):
    s = p.shape[0]
    nb = s // SWA_BLOCK
    t, d, kvw = SWA_BLOCK, SWA_HEAD_DIM, SWA_KV_HEADS * SWA_HEAD_DIM
    rev = lambda n: nb - 1 - n

    def body(q_ref, kc_ref, kp_ref, vc_ref, vp_ref, sink_ref, dy_ref, dq_ref, dk_ref, dv_ref, ds_ref, ck, cv):
        n = pl.program_id(0)

        @pl.when(n == 0)
        def _():
            ck[...] = jnp.zeros_like(ck)
            cv[...] = jnp.zeros_like(cv)
            ds_ref[...] = jnp.zeros_like(ds_ref)

        args = _swa_args(q_ref, kc_ref, kp_ref, vc_ref, vp_ref, sink_ref)
        _, vjp = jax.vjp(functools.partial(_swa_fn, not_first=rev(n) > 0), *args)
        dqs, dkcs, dkps, dvcs, dvps, dsinks = vjp([dy_ref[:, h * d:(h + 1) * d] for h in range(SWA_Q_HEADS)])
        dq_ref[...] = jnp.concatenate(dqs, axis=1).astype(dq_ref.dtype)
        dk_ref[...] = (jnp.concatenate(dkcs, axis=1) + ck[...]).astype(dk_ref.dtype)
        dv_ref[...] = (jnp.concatenate(dvcs, axis=1) + cv[...]).astype(dv_ref.dtype)
        ck[...] = jnp.concatenate(dkps, axis=1)
        cv[...] = jnp.concatenate(dvps, axis=1)
        lane = lax.broadcasted_iota(jnp.int32, (1, LANES), 1)
        acc = jnp.zeros((1, LANES), F32)
        for h in range(SWA_Q_HEADS):
            acc = acc + jnp.where(lane == h, dsinks[h], 0.0)
        ds_ref[...] += acc

    dy = pl.BlockSpec((t, MIX_WIDTH), lambda n: (rev(n), 0))
    kv_out = pl.BlockSpec((t, kvw), lambda n: (rev(n), 0))
    dq, dk, dv, ds = pl.pallas_call(
        body, grid=(nb,), in_specs=_swa_specs(nb, rev) + [dy],
        out_specs=[dy, kv_out, kv_out, pl.BlockSpec((1, LANES), lambda n: (0, 0))],
        out_shape=[jax.ShapeDtypeStruct((s, MIX_WIDTH), BF16), jax.ShapeDtypeStruct((s, kvw), BF16),
                   jax.ShapeDtypeStruct((s, kvw), BF16), jax.ShapeDtypeStruct((1, LANES), F32)],
        scratch_shapes=[pltpu.VMEM((t, kvw), F32), pltpu.VMEM((t, kvw), F32)],
        compiler_params=_cp(("arbitrary",)), name=name)(p, p, p, p, p, _pad_lanes(sinks), dcat)
    return dq, dk, dv, ds[0, :SWA_Q_HEADS]


RW_SHIFT = 5120
RW_R, RW_K, RW_V, RW_WD, RW_AD, RW_GD = 0, 1536, 3072, 4608, 4736, 4864


def _head_matrix(width, head_dim):
    e = (np.arange(width)[:, None] // head_dim == np.arange(LANES)[None, :]).astype(np.float32)
    return jnp.asarray(e), jnp.asarray(e.T)


def _rwkv_pre_fn(pieces, shifted, mus, w0, wdu, a0, wiu, wgu, k_k, k_a, e, et):
    r, k, v, wd, ad, gd = [p + (s - p) * mu for p, s, mu in zip(pieces, shifted, mus)]
    w_log = -_softplus(-(w0 + _dot(jnp.tanh(wd), wdu))) - 0.5
    w = jnp.exp(-jnp.exp(w_log))
    a = _sigmoid(a0 + _dot(ad, wiu))
    g = _dot(_sigmoid(gd), wgu)
    kkr = k * k_k
    kk = kkr * _dot(lax.rsqrt(_dot(kkr * kkr, e) + 1e-6), et)
    k2 = k * (1.0 + (a - 1.0) * k_a)
    return r, w, k2, v, kk, kk * a, g


_RW_GROUPS = ((RW_R, MIX_WIDTH), (RW_K, MIX_WIDTH), (RW_V, MIX_WIDTH), (RW_WD, LANES), (RW_AD, LANES), (RW_GD, 2 * LANES))


def _rwkv_pre_inputs(p_ref, prev_ref, mu_ref, first):
    pieces = [p_ref[:, o:o + n] for o, n in _RW_GROUPS]
    shifted = [_shift_down(p_ref[:, o:o + n], prev_ref[:, o:o + n] * first, 1) for o, n in _RW_GROUPS]
    mus = [mu_ref[:, o:o + n] for o, n in _RW_GROUPS]
    return pieces, shifted, mus


def _rwkv_param_specs():
    vec = lambda n: pl.BlockSpec((1, n), lambda i: (0, 0))
    mat = lambda r, c: pl.BlockSpec((r, c), lambda i: (0, 0))
    return [vec(RW_SHIFT), vec(MIX_WIDTH), mat(LANES, MIX_WIDTH), vec(MIX_WIDTH), mat(LANES, MIX_WIDTH),
            mat(2 * LANES, MIX_WIDTH), vec(MIX_WIDTH), vec(MIX_WIDTH), mat(MIX_WIDTH, LANES), mat(LANES, MIX_WIDTH)]


def rwkv_pre_fwd(p, params, name):
    s = p.shape[0]
    t = _tile(s, (128, 64))

    def body(p_ref, prev_ref, mu_ref, *rest):
        prm, outs = rest[:9], rest[9:]
        first = (pl.program_id(0) > 0).astype(F32)
        pieces, shifted, mus = _rwkv_pre_inputs(p_ref, prev_ref, mu_ref, first)
        res = _rwkv_pre_fn(pieces, shifted, mus, *[q[...] for q in prm])
        for o_ref, val in zip(outs, res):
            o_ref[...] = val

    row = pl.BlockSpec((t, RW_SHIFT), lambda i: (i, 0))
    prev = pl.BlockSpec((SUBLANES, RW_SHIFT), lambda i: (jnp.maximum(i * (t // SUBLANES) - 1, 0), 0))
    out = pl.BlockSpec((t, MIX_WIDTH), lambda i: (i, 0))
    return pl.pallas_call(
        body, grid=(s // t,), in_specs=[row, prev] + _rwkv_param_specs(), out_specs=[out] * 7,
        out_shape=[jax.ShapeDtypeStruct((s, MIX_WIDTH), F32)] * 7, compiler_params=_cp(("parallel",)),
        name=name)(p, p, *params)


def rwkv_pre_bwd(p, params, cots, name):
    s = p.shape[0]
    t = _tile(s, (64, 32))

    def body(p_ref, prev_ref, mu_ref, *rest):
        prm, cot, outs = rest[:9], rest[9:19], rest[19:]
        dp_ref, dps_ref, grads = outs[0], outs[1], outs[2:]
        i = pl.program_id(0)
        first = (i > 0).astype(F32)
        pieces, shifted, mus = _rwkv_pre_inputs(p_ref, prev_ref, mu_ref, first)
        prm_v = [q[...] for q in prm]
        fn = lambda pieces, shifted, mus, *small: _rwkv_pre_fn(pieces, shifted, mus, *small, prm_v[7], prm_v[8])
        _, vjp = jax.vjp(fn, pieces, shifted, mus, *prm_v[:7])
        dr, dw, dk2, dv, dkk, db, dr2, dk22, dv2, dg = [c[...] for c in cot]
        res = vjp((dr + dr2, dw, dk2 + dk22, dv + dv2, dkk, db, dg))
        dpieces, dshifted, dmus, dsmall = res[0], res[1], res[2], res[3:]
        for (o, n), dpi, dsi in zip(_RW_GROUPS, dpieces, dshifted):
            dp_ref[:, o:o + n] = dpi
            dps_ref[:, o:o + n] = dsi

        @pl.when(i == 0)
        def _():
            for g_ref in grads:
                g_ref[...] = jnp.zeros_like(g_ref)

        for (o, n), dmu in zip(_RW_GROUPS, dmus):
            grads[0][:, o:o + n] += dmu
        for g_ref, dval in zip(grads[1:], dsmall):
            g_ref[...] += dval

    row = pl.BlockSpec((t, RW_SHIFT), lambda i: (i, 0))
    prev = pl.BlockSpec((SUBLANES, RW_SHIFT), lambda i: (jnp.maximum(i * (t // SUBLANES) - 1, 0), 0))
    act = pl.BlockSpec((t, MIX_WIDTH), lambda i: (i, 0))
    pspecs = _rwkv_param_specs()
    full = jax.ShapeDtypeStruct((s, RW_SHIFT), F32)
    gshapes = [jax.ShapeDtypeStruct(q.shape, F32) for q in params[:8]]
    return pl.pallas_call(
        body, grid=(s // t,), in_specs=[row, prev] + pspecs + [act] * 10, out_specs=[row, row] + pspecs[:8],
        out_shape=[full, full] + gshapes, compiler_params=_cp(("arbitrary",)), name=name)(p, p, *params, *cots)


def shift_add(a, b, js, out_dtype, name):
    s, c = a.shape
    t = _tile(s, (256, 128, 64))
    tc = _tile(c, (1024, 768, 512, 640, 384, 256, 128))
    nt, nb = s // t, len(b)

    def body(a_ref, *rest):
        b_refs, n_refs, o_ref = rest[:nb], rest[nb:2 * nb], rest[2 * nb]
        last = (pl.program_id(1) < nt - 1).astype(F32)
        acc = a_ref[...]
        for b_ref, n_ref, j in zip(b_refs, n_refs, js):
            acc = acc + _shift_up(b_ref[...], n_ref[...] * last, j)
        o_ref[...] = acc.astype(o_ref.dtype)

    tile = pl.BlockSpec((t, tc), lambda j, i: (i, j))
    _, nxt = _halo_specs(t, s, tc, lambda j: j)
    return pl.pallas_call(
        body, grid=(c // tc, nt), in_specs=[tile] * (1 + nb) + [nxt] * nb, out_specs=tile,
        out_shape=jax.ShapeDtypeStruct((s, c), out_dtype), compiler_params=_cp(("parallel", "parallel")),
        name=name)(a, *b, *b)


def _rwkv_post_fn(y, r, k2, v, g, gn_g, gn_b, r_k, e, et):
    n = RWKV_HEAD_DIM
    yc = y - _dot(_dot(y, e), et) * (1.0 / n)
    rstd = lax.rsqrt(_dot(yc * yc, e) * (1.0 / n) + RWKV_GN_EPS)
    yn = yc * _dot(rstd, et) * gn_g + gn_b
    bonus = _dot(_dot(r * k2 * r_k, e), et) * v
    return (yn + bonus) * g


def rwkv_post_fwd(acts, params, name):
    s = acts[0].shape[0]
    t = _tile(s, (256, 128))

    def body(*refs):
        vals = [q[...] for q in refs[:10]]
        refs[10][...] = _rwkv_post_fn(*vals).astype(refs[10].dtype)

    act = pl.BlockSpec((t, MIX_WIDTH), lambda i: (i, 0))
    vec = pl.BlockSpec((1, MIX_WIDTH), lambda i: (0, 0))
    mats = [pl.BlockSpec((MIX_WIDTH, LANES), lambda i: (0, 0)), pl.BlockSpec((LANES, MIX_WIDTH), lambda i: (0, 0))]
    return pl.pallas_call(
        body, grid=(s // t,), in_specs=[act] * 5 + [vec] * 3 + mats, out_specs=act,
        out_shape=jax.ShapeDtypeStruct((s, MIX_WIDTH), BF16), compiler_params=_cp(("parallel",)),
        name=name)(*acts, *params)


def rwkv_post_bwd(acts, params, dcat, name):
    s = acts[0].shape[0]
    t = _tile(s, (128, 64))

    def body(*refs):
        ins, dy_ref, outs = refs[:10], refs[10], refs[11:]
        vals = [q[...] for q in ins]
        fn = lambda *a: _rwkv_post_fn(*a, vals[8], vals[9])
        _, vjp = jax.vjp(fn, *vals[:8])
        res = vjp(dy_ref[...])
        for o_ref, val in zip(outs[:5], res[:5]):
            o_ref[...] = val

        @pl.when(pl.program_id(0) == 0)
        def _():
            for g_ref in outs[5:]:
                g_ref[...] = jnp.zeros_like(g_ref)

        for g_ref, val in zip(outs[5:], res[5:]):
            g_ref[...] += val

    act = pl.BlockSpec((t, MIX_WIDTH), lambda i: (i, 0))
    vec = pl.BlockSpec((1, MIX_WIDTH), lambda i: (0, 0))
    mats = [pl.BlockSpec((MIX_WIDTH, LANES), lambda i: (0, 0)), pl.BlockSpec((LANES, MIX_WIDTH), lambda i: (0, 0))]
    a_shape = jax.ShapeDtypeStruct((s, MIX_WIDTH), F32)
    v_shape = jax.ShapeDtypeStruct((1, MIX_WIDTH), F32)
    return pl.pallas_call(
        body, grid=(s // t,), in_specs=[act] * 5 + [vec] * 3 + mats + [act], out_specs=[act] * 5 + [vec] * 3,
        out_shape=[a_shape] * 5 + [v_shape] * 3, compiler_params=_cp(("arbitrary",)), name=name)(*acts, *params, dcat)


RW_T, RW_CHUNK = 16, 64
RW_PAIRS_FWD, RW_PAIRS_BWD = 2, 1


def _rw_spread_matrix():
    r = (np.arange(RW_T)[:, None] == np.arange(RW_T * LANES)[None, :] // LANES).astype(np.float32)
    return jnp.asarray(r)


def _rw_spread(x_blk, spread, is_a):
    full = _dot_tn(x_blk, spread)
    return jnp.where(is_a, full[:RWKV_HEAD_DIM], full[RWKV_HEAD_DIM:])


def rwkv_scan_fwd(r, w, k, v, kk, b, name):
    s = r.shape[0]
    n = RWKV_HEAD_DIM
    pairs, width = RW_PAIRS_FWD, LANES * RW_PAIRS_FWD
    nchunk, nsub = s // RW_CHUNK, RW_CHUNK // RW_T

    def body(r_ref, w_ref, k_ref, v_ref, kk_ref, b_ref, sp_ref, y_ref, ck_ref, cbuf):
        lane = lax.broadcasted_iota(jnp.int32, (n, RW_T * LANES), 1)
        is_a = (lane % LANES) < n
        rowi = lax.broadcasted_iota(jnp.int32, (RW_T, LANES), 0)

        def chunk(ci, states):
            for pp in range(pairs):
                ck_ref[ci, :, pp * LANES:(pp + 1) * LANES] = states[pp]

            def sub(si, states):
                rows = pl.ds(pl.multiple_of(ci * RW_CHUNK + si * RW_T, RW_T), RW_T)
                states = list(states)
                vblk, ys = [], []
                for pp in range(pairs):
                    cols = slice(pp * LANES, (pp + 1) * LANES)
                    for q, ref in enumerate((kk_ref, w_ref, b_ref, k_ref, r_ref)):
                        cbuf[q, pp] = _rw_spread(ref[rows, cols], sp_ref[...], is_a)
                    vblk.append(v_ref[rows, cols])
                    ys.append(jnp.zeros((RW_T, LANES), F32))
                for t in range(RW_T):
                    sl = slice(t * LANES, (t + 1) * LANES)
                    for pp in range(pairs):
                        st = states[pp]
                        sa = -jnp.sum(st * cbuf[0, pp, :, sl], axis=0, keepdims=True)
                        st = st * cbuf[1, pp, :, sl] + cbuf[2, pp, :, sl] * sa + cbuf[3, pp, :, sl] * vblk[pp][t:t + 1, :]
                        yrow = jnp.sum(st * cbuf[4, pp, :, sl], axis=0, keepdims=True)
                        ys[pp] = jnp.where(rowi == t, yrow, ys[pp])
                        states[pp] = st
                for pp in range(pairs):
                    y_ref[rows, pp * LANES:(pp + 1) * LANES] = ys[pp]
                return tuple(states)

            return lax.fori_loop(0, nsub, sub, states)

        lax.fori_loop(0, nchunk, chunk, tuple(jnp.zeros((n, LANES), F32) for _ in range(pairs)))

    col = pl.BlockSpec((s, width), lambda j: (0, j))
    return pl.pallas_call(
        body, grid=(MIX_WIDTH // width,),
        in_specs=[col] * 6 + [pl.BlockSpec((RW_T, RW_T * LANES), lambda j: (0, 0))],
        out_specs=[col, pl.BlockSpec((nchunk, n, width), lambda j: (0, 0, j))],
        out_shape=[jax.ShapeDtypeStruct((s, MIX_WIDTH), F32), jax.ShapeDtypeStruct((nchunk, n, MIX_WIDTH), F32)],
        scratch_shapes=[pltpu.VMEM((5, pairs, n, RW_T * LANES), F32)],
        compiler_params=_cp(("parallel",)), name=name)(r, w, k, v, kk, b, _rw_spread_matrix())


def rwkv_scan_bwd(r, w, k, v, kk, b, ck, dy, name):
    s = r.shape[0]
    n = RWKV_HEAD_DIM
    pairs, width = RW_PAIRS_BWD, LANES * RW_PAIRS_BWD
    nchunk, nsub = s // RW_CHUNK, RW_CHUNK // RW_T

    def body(r_ref, w_ref, k_ref, v_ref, kk_ref, b_ref, ck_ref, dy_ref, sp_ref,
             dr_ref, dw_ref, dk_ref, dv_ref, dkk_ref, db_ref, cbuf, sbuf, sabuf):
        lane_c = lax.broadcasted_iota(jnp.int32, (n, RW_T * LANES), 1)
        is_a_c = (lane_c % LANES) < n
        lane = lax.broadcasted_iota(jnp.int32, (n, LANES), 1)
        is_a = lane < n
        rowi = lax.broadcasted_iota(jnp.int32, (RW_T, LANES), 0)
        col_refs = (dr_ref, dk_ref, dw_ref, db_ref, dkk_ref)

        def seg_cols(x):
            return (jnp.sum(jnp.where(is_a, x, 0.0), axis=1, keepdims=True),
                    jnp.sum(jnp.where(is_a, 0.0, x), axis=1, keepdims=True))

        def chunk(cr, d_states):
            ci = nchunk - 1 - cr
            states = []
            for pp in range(pairs):
                st = ck_ref[ci, :, pp * LANES:(pp + 1) * LANES]
                sbuf[pp, 0] = st
                states.append(st)

            def fsub(si, states):
                rows = pl.ds(pl.multiple_of(ci * RW_CHUNK + si * RW_T, RW_T), RW_T)
                states = list(states)
                vblk, sas = [], []
                for pp in range(pairs):
                    cols = slice(pp * LANES, (pp + 1) * LANES)
                    for q, ref in enumerate((kk_ref, w_ref, b_ref, k_ref)):
                        cbuf[q, pp] = _rw_spread(ref[rows, cols], sp_ref[...], is_a_c)
                    vblk.append(v_ref[rows, cols])
                    sas.append(jnp.zeros((RW_T, LANES), F32))
                for t in range(RW_T):
                    sl = slice(t * LANES, (t + 1) * LANES)
                    for pp in range(pairs):
                        st = states[pp]
                        sa = -jnp.sum(st * cbuf[0, pp, :, sl], axis=0, keepdims=True)
                        st = st * cbuf[1, pp, :, sl] + cbuf[2, pp, :, sl] * sa + cbuf[3, pp, :, sl] * vblk[pp][t:t + 1, :]
                        sbuf[pp, si * RW_T + t + 1] = st
                        sas[pp] = jnp.where(rowi == t, sa, sas[pp])
                        states[pp] = st
                for pp in range(pairs):
                    sabuf[pp, pl.ds(pl.multiple_of(si * RW_T, RW_T), RW_T), :] = sas[pp]
                return tuple(states)

            lax.fori_loop(0, nsub, fsub, tuple(states))

            def bsub(sr, d_states):
                si = nsub - 1 - sr
                rows = pl.ds(pl.multiple_of(ci * RW_CHUNK + si * RW_T, RW_T), RW_T)
                d_states = list(d_states)
                vblk, dyblk, sablk, dvs, accs = [], [], [], [], []
                for pp in range(pairs):
                    cols = slice(pp * LANES, (pp + 1) * LANES)
                    for q, ref in enumerate((kk_ref, w_ref, b_ref, k_ref, r_ref)):
                        cbuf[q, pp] = _rw_spread(ref[rows, cols], sp_ref[...], is_a_c)
                    vblk.append(v_ref[rows, cols])
                    dyblk.append(dy_ref[rows, cols])
                    sablk.append(sabuf[pp, pl.ds(pl.multiple_of(si * RW_T, RW_T), RW_T), :])
                    dvs.append(jnp.zeros((RW_T, LANES), F32))
                    accs.append([jnp.zeros((n, LANES), F32) for _ in range(5)])
                for t in reversed(range(RW_T)):
                    sl = slice(t * LANES, (t + 1) * LANES)
                    for pp in range(pairs):
                        sp_, sc_ = sbuf[pp, si * RW_T + t], sbuf[pp, si * RW_T + t + 1]
                        dyr, vr, sar = dyblk[pp][t:t + 1, :], vblk[pp][t:t + 1, :], sablk[pp][t:t + 1, :]
                        c_kk, c_w, c_b, c_k, c_r = [cbuf[q, pp, :, sl] for q in range(5)]
                        ds = d_states[pp] + c_r * dyr
                        dsa = jnp.sum(ds * c_b, axis=0, keepdims=True)
                        cols5 = (seg_cols(sc_ * dyr), seg_cols(ds * vr), seg_cols(ds * sp_), seg_cols(ds * sar),
                                 seg_cols(-(sp_ * dsa)))
                        dvs[pp] = jnp.where(rowi == t, jnp.sum(ds * c_k, axis=0, keepdims=True), dvs[pp])
                        d_states[pp] = ds * c_w - c_kk * dsa
                        for q in range(5):
                            accs[pp][q] = jnp.where(lane == t, cols5[q][0],
                                                    jnp.where(lane == n + t, cols5[q][1], accs[pp][q]))
                for pp in range(pairs):
                    cols = slice(pp * LANES, (pp + 1) * LANES)
                    dv_ref[rows, cols] = dvs[pp]
                    for q in range(5):
                        at = accs[pp][q].T
                        col_refs[q][rows, cols] = jnp.concatenate([at[:RW_T], at[n:n + RW_T]], axis=1)
                return tuple(d_states)

            return lax.fori_loop(0, nsub, bsub, d_states)

        lax.fori_loop(0, nchunk, chunk, tuple(jnp.zeros((n, LANES), F32) for _ in range(pairs)))

    col = pl.BlockSpec((s, width), lambda j: (0, j))
    out = jax.ShapeDtypeStruct((s, MIX_WIDTH), F32)
    return pl.pallas_call(
        body, grid=(MIX_WIDTH // width,),
        in_specs=[col] * 6 + [pl.BlockSpec((nchunk, n, width), lambda j: (0, 0, j)), col,
                              pl.BlockSpec((RW_T, RW_T * LANES), lambda j: (0, 0))],
        out_specs=[col] * 6, out_shape=[out] * 6,
        scratch_shapes=[pltpu.VMEM((5, pairs, n, RW_T * LANES), F32),
                        pltpu.VMEM((pairs, RW_CHUNK + 1, n, LANES), F32),
                        pltpu.VMEM((pairs, RW_CHUNK, LANES), F32)],
        compiler_params=_cp(("parallel",)), name=name)(r, w, k, v, kk, b, ck, dy, _rw_spread_matrix())


GD_Q, GD_K, GD_V, GD_Z, GD_QMEM, GD_BT, GD_AT, GD_COLS = 0, 768, 1536, 3072, 4608, 5120, 5248, 5376
_GD_GROUPS = ((GD_Q, GDN_QK_WIDTH), (GD_K, GDN_QK_WIDTH), (GD_V, MIX_WIDTH))


def _gdn_pre_fn(xs, convs, bt, at, a_log, dt_bias, e6, e6t, ebc):
    k_w = GDN_CONV
    acts = [_silu(sum(convs[g][j] * xs[g][k_w - 1 - j] for j in range(k_w))) for g in range(3)]
    l2 = lambda x: x * _dot(lax.rsqrt(_dot(x * x, e6) + 1e-6), e6t)
    beta = _sigmoid(bt)
    g = -jnp.exp(a_log) * _softplus(at + dt_bias)
    return l2(acts[0]), l2(acts[1]), acts[2], _dot(g, ebc), _dot(beta, ebc)


def _gdn_pre_inputs(x_ref, prev_ref, conv_ref, first):
    xs = [[_shift_down(x_ref[:, o:o + n], prev_ref[:, o:o + n] * first, j) for j in range(GDN_CONV)]
          for o, n in _GD_GROUPS]
    convs = [[conv_ref[j:j + 1, o:o + n] for j in range(GDN_CONV)] for o, n in _GD_GROUPS]
    return xs, convs


def _gdn_pre_specs(t):
    x = pl.BlockSpec((t, GDN_CONV_WIDTH), lambda i: (i, 0))
    prev = pl.BlockSpec((SUBLANES, GDN_CONV_WIDTH), lambda i: (jnp.maximum(i * (t // SUBLANES) - 1, 0), 0))
    bta = pl.BlockSpec((t, 2 * LANES), lambda i: (i, GD_BT // (2 * LANES)))
    conv = pl.BlockSpec((GDN_CONV, GDN_CONV_WIDTH), lambda i: (0, 0))
    vec = pl.BlockSpec((1, LANES), lambda i: (0, 0))
    mats = [pl.BlockSpec((GDN_QK_WIDTH, LANES), lambda i: (0, 0)), pl.BlockSpec((LANES, GDN_QK_WIDTH), lambda i: (0, 0)),
            pl.BlockSpec((LANES, MIX_WIDTH), lambda i: (0, 0))]
    return [x, prev, bta, conv, vec, vec] + mats


def gdn_pre_fwd(p, params, name):
    s = p.shape[0]
    t = _tile(s, (128, 64))

    def body(x_ref, prev_ref, bta_ref, conv_ref, al_ref, dt_ref, e6_ref, e6t_ref, ebc_ref, *outs):
        first = (pl.program_id(0) > 0).astype(F32)
        xs, convs = _gdn_pre_inputs(x_ref, prev_ref, conv_ref, first)
        res = _gdn_pre_fn(xs, convs, bta_ref[:, :LANES], bta_ref[:, LANES:], al_ref[...], dt_ref[...],
                          e6_ref[...], e6t_ref[...], ebc_ref[...])
        for o_ref, val in zip(outs, res):
            o_ref[...] = val

    qk = pl.BlockSpec((t, GDN_QK_WIDTH), lambda i: (i, 0))
    wide = pl.BlockSpec((t, MIX_WIDTH), lambda i: (i, 0))
    qk_s, wide_s = jax.ShapeDtypeStruct((s, GDN_QK_WIDTH), F32), jax.ShapeDtypeStruct((s, MIX_WIDTH), F32)
    return pl.pallas_call(
        body, grid=(s // t,), in_specs=_gdn_pre_specs(t), out_specs=[qk, qk, wide, wide, wide],
        out_shape=[qk_s, qk_s, wide_s, wide_s, wide_s], compiler_params=_cp(("parallel",)), name=name)(p, p, p, *params)


def gdn_pre_bwd(p, params, cots, name):
    s = p.shape[0]
    t = _tile(s, (64, 32))

    def body(x_ref, prev_ref, bta_ref, conv_ref, al_ref, dt_ref, e6_ref, e6t_ref, ebc_ref, *rest):
        cot, outs = rest[:5], rest[5:]
        dxs, dbta_ref, dconv_ref, dal_ref, ddt_ref = outs[:4], outs[4], outs[5], outs[6], outs[7]
        i = pl.program_id(0)
        first = (i > 0).astype(F32)
        xs, convs = _gdn_pre_inputs(x_ref, prev_ref, conv_ref, first)
        mats = (e6_ref[...], e6t_ref[...], ebc_ref[...])
        fn = lambda xs, convs, bt, at, al, dt: _gdn_pre_fn(xs, convs, bt, at, al, dt, *mats)
        _, vjp = jax.vjp(fn, xs, convs, bta_ref[:, :LANES], bta_ref[:, LANES:], al_ref[...], dt_ref[...])
        d_xs, d_convs, d_bt, d_at, d_al, d_dt = vjp(tuple(c[...] for c in cot))
        for g, (o, n) in enumerate(_GD_GROUPS):
            for j in range(GDN_CONV):
                dxs[j][:, o:o + n] = d_xs[g][j]
        dbta_ref[...] = jnp.concatenate([d_bt, d_at], axis=1).astype(dbta_ref.dtype)

        @pl.when(i == 0)
        def _():
            dconv_ref[...] = jnp.zeros_like(dconv_ref)
            dal_ref[...] = jnp.zeros_like(dal_ref)
            ddt_ref[...] = jnp.zeros_like(ddt_ref)

        for g, (o, n) in enumerate(_GD_GROUPS):
            for j in range(GDN_CONV):
                dconv_ref[j:j + 1, o:o + n] += d_convs[g][j]
        dal_ref[...] += d_al
        ddt_ref[...] += d_dt

    specs = _gdn_pre_specs(t)
    qk = pl.BlockSpec((t, GDN_QK_WIDTH), lambda i: (i, 0))
    wide = pl.BlockSpec((t, MIX_WIDTH), lambda i: (i, 0))
    x_s = jax.ShapeDtypeStruct((s, GDN_CONV_WIDTH), F32)
    vec_s = jax.ShapeDtypeStruct((1, LANES), F32)
    return pl.pallas_call(
        body, grid=(s // t,), in_specs=specs + [qk, qk, wide, wide, wide],
        out_specs=[specs[0]] * 4 + [pl.BlockSpec((t, 2 * LANES), lambda i: (i, 0)), specs[3], specs[4], specs[5]],
        out_shape=[x_s] * 4 + [jax.ShapeDtypeStruct((s, 2 * LANES), BF16),
                               jax.ShapeDtypeStruct((GDN_CONV, GDN_CONV_WIDTH), F32), vec_s, vec_s],
        compiler_params=_cp(("arbitrary",)), name=name)(p, p, p, *params, *cots)


def _gdn_post_fn(o, z, norm_g, e12, e12t, trep):
    rstd = lax.rsqrt(_dot(o * o, e12) * (1.0 / GDN_HEAD_DIM) + NORM_EPS)
    return o * _dot(rstd, e12t) * _dot(norm_g, trep) * _silu(z)


def _gdn_post_specs(t):
    act = pl.BlockSpec((t, MIX_WIDTH), lambda i: (i, 0))
    z = pl.BlockSpec((t, MIX_WIDTH), lambda i: (i, GD_Z // MIX_WIDTH))
    mats = [pl.BlockSpec((SUBLANES, LANES), lambda i: (0, 0)), pl.BlockSpec((MIX_WIDTH, LANES), lambda i: (0, 0)),
            pl.BlockSpec((LANES, MIX_WIDTH), lambda i: (0, 0)), pl.BlockSpec((LANES, MIX_WIDTH), lambda i: (0, 0))]
    return [act, z] + mats


def gdn_post_fwd(o, p, params, name):
    s = o.shape[0]
    t = _tile(s, (256, 128))

    def body(o_ref, z_ref, ng_ref, e_ref, et_ref, tr_ref, out_ref):
        res = _gdn_post_fn(o_ref[...], z_ref[...], ng_ref[0:1, :], e_ref[...], et_ref[...], tr_ref[...])
        out_ref[...] = res.astype(out_ref.dtype)

    act = pl.BlockSpec((t, MIX_WIDTH), lambda i: (i, 0))
    return pl.pallas_call(
        body, grid=(s // t,), in_specs=_gdn_post_specs(t), out_specs=act,
        out_shape=jax.ShapeDtypeStruct((s, MIX_WIDTH), BF16), compiler_params=_cp(("parallel",)),
        name=name)(o, p, *params)


def gdn_post_bwd(o, p, params, dcat, name):
    s = o.shape[0]
    t = _tile(s, (128, 64))

    def body(o_ref, z_ref, ng_ref, e_ref, et_ref, tr_ref, dy_ref, do_ref, dz_ref, dng_ref):
        mats = (e_ref[...], et_ref[...], tr_ref[...])
        fn = lambda o, z, ng: _gdn_post_fn(o, z, ng, *mats)
        _, vjp = jax.vjp(fn, o_ref[...], z_ref[...], ng_ref[0:1, :])
        d_o, d_z, d_ng = vjp(dy_ref[...])
        do_ref[...] = d_o
        dz_ref[...] = d_z.astype(dz_ref.dtype)

        @pl.when(pl.program_id(0) == 0)
        def _():
            dng_ref[...] = jnp.zeros_like(dng_ref)

        dng_ref[...] += d_ng

    act = pl.BlockSpec((t, MIX_WIDTH), lambda i: (i, 0))
    return pl.pallas_call(
        body, grid=(s // t,), in_specs=_gdn_post_specs(t) + [act],
        out_specs=[act, act, pl.BlockSpec((1, LANES), lambda i: (0, 0))],
        out_shape=[jax.ShapeDtypeStruct((s, MIX_WIDTH), F32), jax.ShapeDtypeStruct((s, MIX_WIDTH), BF16),
                   jax.ShapeDtypeStruct((1, LANES), F32)],
        compiler_params=_cp(("arbitrary",)), name=name)(o, p, *params, dcat)


def _gdn_chunk_fn(q, k, v, gb, bb, gb64, state):
    c = GDN_CHUNK
    ri = lax.broadcasted_iota(jnp.int32, (c, c), 0)
    ci = lax.broadcasted_iota(jnp.int32, (c, c), 1)
    causal, strict = ri >= ci, ri > ci
    ltri = causal.astype(F32)
    eye = (ri == ci).astype(F32)
    first_col = (ci == 0).astype(F32)
    last_col = (ci == c - 1).astype(F32)
    last_col_tall = (lax.broadcasted_iota(jnp.int32, (GDN_HEAD_DIM, c), 1) == c - 1).astype(F32)

    qs = q * (GDN_HEAD_DIM ** -0.5)
    gc = _dot(ltri, gb)
    gd = _dot(ltri, gb64)
    diff = gd - _dot_nt(first_col, gd)
    decay = jnp.exp(jnp.where(causal, diff, -jnp.inf))
    kb = k * bb
    lmat = jnp.where(strict, _dot_nt(kb, k) * decay, 0.0)
    tmat, pw = eye - lmat, lmat
    for _ in range(5):
        pw = _dot(pw, pw)
        tmat = tmat + _dot(tmat, pw)
    eg = jnp.exp(gc)
    u = _dot(tmat, v * bb)
    w = _dot(tmat, kb * eg)
    a_qk = jnp.where(causal, _dot_nt(qs, k) * decay, 0.0)
    g_last = _dot(last_col, gc)
    k_dec = k * jnp.exp(g_last - gc)
    v_new = u - _dot(w, state)
    out = _dot(qs * eg, state) + _dot(a_qk, v_new)
    new_state = state * jnp.exp(_dot(last_col_tall, gc)) + _dot_tn(k_dec, v_new)
    return out, new_state


def _gdn_chunk_specs(order):
    c, d = GDN_CHUNK, GDN_HEAD_DIM
    qk = pl.BlockSpec((c, d), lambda j, n, rep: (order(n), j))
    vh = pl.BlockSpec((c, d), lambda j, n, rep: (order(n), 2 * j + rep))
    return qk, vh


def gdn_chunk_fwd(q, k, v, gb, bb, name):
    s = q.shape[0]
    nc, d = s // GDN_CHUNK, GDN_HEAD_DIM

    def body(q_ref, k_ref, v_ref, gb_ref, bb_ref, o_ref, st_ref, carry):
        n, rep = pl.program_id(1), pl.program_id(2)

        @pl.when(n == 0)
        def _():
            carry[rep] = jnp.zeros((d, d), F32)

        state = carry[rep]
        st_ref[0, 0] = state
        out, new_state = _gdn_chunk_fn(q_ref[...], k_ref[...], v_ref[...], gb_ref[...], bb_ref[...],
                                       gb_ref[:, :GDN_CHUNK], state)
        o_ref[...] = out
        carry[rep] = new_state

    qk, vh = _gdn_chunk_specs(lambda n: n)
    st = pl.BlockSpec((1, 1, d, d), lambda j, n, rep: (2 * j + rep, n, 0, 0))
    return pl.pallas_call(
        body, grid=(GDN_QK_HEADS, nc, 2), in_specs=[qk, qk, vh, vh, vh], out_specs=[vh, st],
        out_shape=[jax.ShapeDtypeStruct((s, MIX_WIDTH), F32), jax.ShapeDtypeStruct((GDN_V_HEADS, nc, d, d), F32)],
        scratch_shapes=[pltpu.VMEM((2, d, d), F32)],
        compiler_params=_cp(("parallel", "arbitrary", "arbitrary")), name=name)(q, k, v, gb, bb)


def gdn_chunk_bwd(q, k, v, gb, bb, states, do, name):
    s = q.shape[0]
    nc, d = s // GDN_CHUNK, GDN_HEAD_DIM
    rev = lambda n: nc - 1 - n

    def body(q_ref, k_ref, v_ref, gb_ref, bb_ref, st_ref, do_ref, dq_ref, dk_ref, dv_ref, dg_ref, db_ref, carry):
        n, rep = pl.program_id(1), pl.program_id(2)

        @pl.when(n == 0)
        def _():
            carry[rep] = jnp.zeros((d, d), F32)

        args = (q_ref[...], k_ref[...], v_ref[...], gb_ref[...], bb_ref[...], gb_ref[:, :GDN_CHUNK], st_ref[0, 0])
        _, vjp = jax.vjp(_gdn_chunk_fn, *args)
        d_q, d_k, d_v, d_gb, d_bb, d_gb64, d_state = vjp((do_ref[...], carry[rep]))
        carry[rep] = d_state
        dv_ref[...] = d_v
        db_ref[...] = d_bb
        dg_ref[...] = d_gb
        dg_ref[:, :GDN_CHUNK] += d_gb64

        @pl.when(rep == 0)
        def _():
            dq_ref[...] = d_q
            dk_ref[...] = d_k

        @pl.when(rep == 1)
        def _():
            dq_ref[...] += d_q
            dk_ref[...] += d_k

    qk, vh = _gdn_chunk_specs(rev)
    st = pl.BlockSpec((1, 1, d, d), lambda j, n, rep: (2 * j + rep, rev(n), 0, 0))
    qk_s, wide_s = jax.ShapeDtypeStruct((s, GDN_QK_WIDTH), F32), jax.ShapeDtypeStruct((s, MIX_WIDTH), F32)
    return pl.pallas_call(
        body, grid=(GDN_QK_HEADS, nc, 2), in_specs=[qk, qk, vh, vh, vh, st, vh], out_specs=[qk, qk, vh, vh, vh],
        out_shape=[qk_s, qk_s, wide_s, wide_s, wide_s], scratch_shapes=[pltpu.VMEM((2, d, d), F32)],
        compiler_params=_cp(("parallel", "arbitrary", "arbitrary")), name=name)(q, k, v, gb, bb, states, do)


WEIGHTS = ['attn_norm', 'mem_norm', 'w_mem_kv', 'w_out', 'ffn_norm', 'w_ffn_up', 'ffn_conv', 'w_ffn_down', 'final_norm',
           'a_w_in', 'a_sinks', 'b_w_in', 'b_mu', 'b_w0', 'b_w_decay_up', 'b_a0', 'b_w_iclr_up', 'b_w_gate_up', 'b_k_k',
           'b_k_a', 'b_r_k', 'b_gn_g', 'b_gn_b', 'c_w_in', 'c_conv', 'c_a_log', 'c_dt_bias', 'c_norm_g']
INPUTS = ['x', 'mem'] + WEIGHTS + ['loss_target'] + ['m_' + n for n in WEIGHTS] + ['v_' + n for n in WEIGHTS]
REPLICATED = ['attn_norm', 'mem_norm', 'ffn_norm', 'final_norm', 'a_sinks', 'b_mu', 'b_w0', 'b_a0', 'b_k_k', 'b_k_a',
              'b_r_k', 'b_gn_g', 'b_gn_b', 'c_a_log', 'c_dt_bias', 'c_norm_g']
B_IN = 5056 + MEM_WIDTH
C_MIX = GDN_CONV_WIDTH + MIX_WIDTH


def _cols_to_shards(full):
    rows, cols = full.shape
    return full.reshape(rows, N_DEV, cols // N_DEV).transpose(1, 0, 2)


def _shards_to_cols(g):
    return g.transpose(1, 0, 2).reshape(g.shape[1], N_DEV * g.shape[2])


def _pad_to(x, n, axis):
    pad = [(0, 0)] * x.ndim
    pad[axis] = (0, n - x.shape[axis])
    return jnp.pad(x, pad)


def _b_pad_cols(w):
    parts = [w[..., :4608], _pad_to(w[..., 4608:4704], LANES, -1), _pad_to(w[..., 4704:4800], LANES, -1), w[..., 4800:5056]]
    if w.shape[-1] > 5056:
        parts.append(w[..., 5056:])
    return jnp.concatenate(parts, axis=-1)


def _b_unpad_cols(w):
    parts = [w[..., :4608], w[..., RW_WD:RW_WD + RWKV_DECAY_RANK], w[..., RW_AD:RW_AD + RWKV_ICLR_RANK], w[..., RW_GD:RW_SHIFT]]
    if w.shape[-1] > RW_SHIFT:
        parts.append(w[..., RW_SHIFT:])
    return jnp.concatenate(parts, axis=-1)


def _c_pad_cols(w):
    return jnp.concatenate([w[..., :C_MIX], w[..., C_MIX + 24:], _pad_to(w[..., C_MIX:C_MIX + 12], LANES, -1),
                            _pad_to(w[..., C_MIX + 12:C_MIX + 24], LANES, -1)], axis=-1)


def _c_unpad_cols(w):
    return jnp.concatenate([w[..., :C_MIX], w[..., GD_BT:GD_BT + GDN_V_HEADS], w[..., GD_AT:GD_AT + GDN_V_HEADS],
                            w[..., GD_QMEM:GD_BT]], axis=-1)


def _pack(arrays):
    flat = jnp.concatenate([a.reshape(-1).astype(F32) for a in arrays])
    unit = SUBLANES * LANES
    return _pad_to(flat, -(-flat.size // unit) * unit, 0).reshape(-1, LANES)


def _unpack(packed, shapes):
    flat, out, at = packed.reshape(-1), [], 0
    for shp in shapes:
        n = int(np.prod(shp))
        out.append(flat[at:at + n].reshape(shp))
        at += n
    return out


def kernel(*args):
    a = dict(zip(INPUTS, args))
    x0, mem, target = a['x'][0], a['mem'][0], a['loss_target'][0]
    s = x0.shape[0]
    e64, e64t = _head_matrix(MIX_WIDTH, RWKV_HEAD_DIM)
    e6, e6t = _head_matrix(GDN_QK_WIDTH, GDN_HEAD_DIM)
    e12, e12t = _head_matrix(MIX_WIDTH, GDN_HEAD_DIM)
    trep = jnp.asarray((np.arange(LANES)[:, None] == np.arange(MIX_WIDTH)[None, :] % LANES).astype(np.float32))
    row = lambda v: v.reshape(1, -1)

    def in_proj_shard(l):
        kind, j = l % 3, l // 3
        return (a['a_w_in'], a['b_w_in'], a['c_w_in'])[kind][j]

    def small_shards(l):
        kind, j = l % 3, l // 3
        if kind == 1:
            return [a['b_w_decay_up'][j], a['b_w_iclr_up'][j], a['b_w_gate_up'][j]]
        if kind == 2:
            return [a['c_conv'][j]]
        return []

    layers = []
    for l in range(DEPTH):
        kind = l % 3
        big = [a['w_mem_kv'][l], a['w_out'][l], a['w_ffn_up'][l], a['w_ffn_down'][l], in_proj_shard(l)]
        small = [a['ffn_conv'][l]] + small_shards(l)
        g = all_gather_many([w.astype(BF16) for w in big] + small, f"gather_weights_{l}")
        w_in = _shards_to_cols(g[4])
        lw = dict(w_kv=g[0].reshape(D_MODEL, 2 * MEM_WIDTH), w_out=g[1].reshape(D_MODEL, D_MODEL),
                  w_up=_shards_to_cols(g[2]), w_down=g[3].reshape(D_FF, D_MODEL), conv=_shards_to_cols(g[5]))
        if kind == 0:
            lw['w_in'] = w_in
        elif kind == 1:
            lw['w_in'] = _b_pad_cols(w_in)
            lw['wdu'] = _pad_to(_shards_to_cols(g[6]), LANES, 0)
            lw['wiu'] = _pad_to(_shards_to_cols(g[7]), LANES, 0)
            lw['wgu'] = _shards_to_cols(g[8])
        else:
            lw['w_in'] = _c_pad_cols(w_in)
            lw['c_conv'] = _shards_to_cols(g[6])
        layers.append(lw)

    def rwkv_params(j, lw):
        return (row(_b_pad_cols(a['b_mu'][j])), row(a['b_w0'][j]), lw['wdu'], row(a['b_a0'][j]), lw['wiu'], lw['wgu'],
                row(a['b_k_k'][j]), row(a['b_k_a'][j]), e64, e64t)

    def rwkv_post_params(j):
        return (row(a['b_gn_g'][j]), row(a['b_gn_b'][j]), row(a['b_r_k'][j]), e64, e64t)

    def gdn_params(j, lw):
        return (lw['c_conv'], _pad_lanes(a['c_a_log'][j]), _pad_lanes(a['c_dt_bias'][j]), e6, e6t, e12t)

    def gdn_post_params(j):
        return (jnp.tile(row(a['c_norm_g'][j]), (SUBLANES, 1)), e12, e12t, trep)

    x = x0
    saved = []
    for l, lw in enumerate(layers):
        kind, j = l % 3, l // 3
        sv = dict(x=x)
        h = rmsnorm_fwd(x, a['attn_norm'][l], BF16, f"attn_norm_{l}")
        memn = rmsnorm_fwd(mem, a['mem_norm'][l], BF16, f"mem_norm_{l}")
        mem_kv = mm(memn, lw['w_kv'], name=f"mem_kv_{l}")
        p = mm(h, lw['w_in'], name=f"in_proj_{l}")
        if kind == 0:
            y = swa_fwd(p, a['a_sinks'][j], f"swa_{l}")
            q_col = MIX_WIDTH + 2 * SWA_KV_HEADS * SWA_HEAD_DIM
        elif kind == 1:
            pre = rwkv_pre_fwd(p, rwkv_params(j, lw), f"rwkv_pre_{l}")
            yscan, ck = rwkv_scan_fwd(*pre[:6], f"rwkv_scan_{l}")
            post_in = (yscan, pre[0], pre[2], pre[3], pre[6])
            y = rwkv_post_fwd(post_in, rwkv_post_params(j), f"rwkv_post_{l}")
            sv.update(pre=pre, ck=ck, post_in=post_in)
            q_col = RW_SHIFT
        else:
            pre = gdn_pre_fwd(p, gdn_params(j, lw), f"gdn_pre_{l}")
            o, states = gdn_chunk_fwd(*pre, f"gdn_chunk_{l}")
            y = gdn_post_fwd(o, p, gdn_post_params(j), f"gdn_post_{l}")
            sv.update(pre=pre, o=o, states=states)
            q_col = GD_QMEM
        y_mem = mem_attn_fwd(p, q_col, mem_kv, f"mem_attn_{l}")
        cat = jnp.concatenate([y, y_mem], axis=1)
        x1 = mm(cat, lw['w_out'], res=x, name=f"out_proj_{l}")
        hf = rmsnorm_fwd(x1, a['ffn_norm'][l], BF16, f"ffn_norm_{l}")
        u0 = mm(hf, lw['w_up'], name=f"ffn_up_{l}")
        act = ffn_act_fwd(u0, lw['conv'], f"ffn_act_{l}")
        x = mm(act, lw['w_down'], res=x1, name=f"ffn_down_{l}")
        sv.update(h=h, memn=memn, mem_kv=mem_kv, p=p, q_col=q_col, cat=cat, x1=x1, hf=hf, u0=u0, act=act)
        saved.append(sv)

    loss_part, dx, d_final_norm = final_loss(x, a['final_norm'], target, "final_loss")

    rep_grads = {n: [None] * a[n].shape[0] for n in ('attn_norm', 'mem_norm', 'ffn_norm', 'a_sinks')}
    rep_grads['final_norm'] = d_final_norm
    results = {}

    def apply_adam(name, idx, pieces, tag):
        w, m, v = a[name][idx], a['m_' + name][idx], a['v_' + name][idx]
        shp = w.shape
        two_d = (-1, shp[-1])
        out = adamw_sum(pieces.reshape((N_DEV,) + w.reshape(two_d).shape), w.reshape(two_d), m.reshape(two_d),
                        v.reshape(two_d), f"adamw_{name}_{tag}")
        results.setdefault(name, {})[idx] = [o.reshape(shp) for o in out]

    for l in reversed(range(DEPTH)):
        kind, j = l % 3, l // 3
        lw, sv = layers[l], saved[l]
        p, q_col = sv['p'], sv['q_col']
        d_act = mm(dx, lw['w_down'], tb=True, name=f"d_ffn_act_{l}")
        dw_down = mm(sv['act'], dx, ta=True, out_dtype=BF16, name=f"dw_ffn_down_{l}")
        dug, duv, dcg, dcv = ffn_act_bwd(sv['u0'], lw['conv'], d_act, f"ffn_act_bwd_{l}")
        du0 = jnp.concatenate([dug, duv], axis=1)
        d_conv = jnp.concatenate([dcg, dcv], axis=1)
        d_hf = mm(du0, lw['w_up'], tb=True, name=f"d_ffn_norm_out_{l}")
        dw_up = mm(sv['hf'], du0, ta=True, out_dtype=BF16, name=f"dw_ffn_up_{l}")
        dx1, rep_grads['ffn_norm'][l] = rmsnorm_bwd(sv['x1'], a['ffn_norm'][l], d_hf, dx, f"ffn_norm_bwd_{l}")
        dcat = mm(dx1, lw['w_out'], tb=True, name=f"d_cat_{l}")
        dw_out = mm(sv['cat'], dx1, ta=True, out_dtype=BF16, name=f"dw_out_{l}")
        dq_mem, d_mem_kv = mem_attn_bwd(p, q_col, sv['mem_kv'], dcat, f"mem_attn_bwd_{l}")
        small_grads = []
        if kind == 0:
            dq, dk, dv, rep_grads['a_sinks'][j] = swa_bwd(p, a['a_sinks'][j], dcat, f"swa_bwd_{l}")
            dp = jnp.concatenate([dq, dk, dv, dq_mem], axis=1)
        elif kind == 1:
            post = rwkv_post_bwd(sv['post_in'], rwkv_post_params(j), dcat, f"rwkv_post_bwd_{l}")
            scan = rwkv_scan_bwd(*sv['pre'][:6], sv['ck'], post[0], f"rwkv_scan_bwd_{l}")
            res = rwkv_pre_bwd(p, rwkv_params(j, lw), tuple(scan) + tuple(post[1:5]), f"rwkv_pre_bwd_{l}")
            dp_mix = shift_add(res[0], [res[1]], [1], BF16, f"rwkv_shift_bwd_{l}")
            dp = jnp.concatenate([dp_mix, dq_mem], axis=1)
            for n, val in zip(('b_mu', 'b_w0', 'b_a0', 'b_k_k', 'b_k_a'), (_b_unpad_cols(res[2]), res[3], res[5], res[8], res[9])):
                rep_grads[n] = val
            rep_grads.update(b_gn_g=post[5], b_gn_b=post[6], b_r_k=post[7])
            small_grads = [_cols_to_shards(res[4][:RWKV_DECAY_RANK]), _cols_to_shards(res[6][:RWKV_ICLR_RANK]),
                           _cols_to_shards(res[7])]
        else:
            d_o, dz, rep_grads['c_norm_g'] = gdn_post_bwd(sv['o'], p, gdn_post_params(j), dcat, f"gdn_post_bwd_{l}")
            chunk = gdn_chunk_bwd(*sv['pre'], sv['states'], d_o, f"gdn_chunk_bwd_{l}")
            res = gdn_pre_bwd(p, gdn_params(j, lw), chunk, f"gdn_pre_bwd_{l}")
            dqkv = shift_add(res[0], list(res[1:4]), [1, 2, 3], BF16, f"gdn_shift_bwd_{l}")
            dp = jnp.concatenate([dqkv, dz, dq_mem, res[4]], axis=1)
            rep_grads.update(c_a_log=res[6][:, :GDN_V_HEADS], c_dt_bias=res[7][:, :GDN_V_HEADS])
            small_grads = [_cols_to_shards(res[5])]
        d_h = mm(dp, lw['w_in'], tb=True, name=f"d_attn_norm_out_{l}")
        dw_in = mm(sv['h'], dp, ta=True, out_dtype=BF16, name=f"dw_in_{l}")
        dx, rep_grads['attn_norm'][l] = rmsnorm_bwd(sv['x'], a['attn_norm'][l], d_h, dx1, f"attn_norm_bwd_{l}")
        d_memn = mm(d_mem_kv, lw['w_kv'], tb=True, name=f"d_mem_norm_out_{l}")
        dw_kv = mm(sv['memn'], d_mem_kv, ta=True, out_dtype=BF16, name=f"dw_mem_kv_{l}")
        _, rep_grads['mem_norm'][l] = rmsnorm_bwd(mem, a['mem_norm'][l], d_memn, None, f"mem_norm_bwd_{l}")

        if kind == 1:
            dw_in = _b_unpad_cols(dw_in)
        elif kind == 2:
            dw_in = _c_unpad_cols(dw_in)
        pieces = [dw_kv.reshape(N_DEV, -1, 2 * MEM_WIDTH), dw_out.reshape(N_DEV, -1, D_MODEL), _cols_to_shards(dw_up),
                  dw_down.reshape(N_DEV, -1, D_MODEL), _cols_to_shards(dw_in), _cols_to_shards(d_conv)] + small_grads
        got = all_to_all_many(pieces, f"exchange_grads_{l}")
        in_name = ('a_w_in', 'b_w_in', 'c_w_in')[kind]
        for name, idx, pc in (('w_mem_kv', l, got[0]), ('w_out', l, got[1]), ('w_ffn_up', l, got[2]),
                              ('w_ffn_down', l, got[3]), (in_name, j, got[4]), ('ffn_conv', l, got[5])):
            apply_adam(name, idx, pc, l)
        if kind == 1:
            for name, pc in zip(('b_w_decay_up', 'b_w_iclr_up', 'b_w_gate_up'), got[6:]):
                apply_adam(name, j, pc, l)
        elif kind == 2:
            apply_adam('c_conv', j, got[6], l)

    rep_vals = []
    for n in REPLICATED:
        gval = rep_grads[n]
        gval = jnp.stack(gval) if isinstance(gval, list) else gval
        rep_vals.append(gval.reshape(a[n].shape))
    shapes = [a[n].shape for n in REPLICATED] + [(1,)]
    part = _pack(rep_vals + [loss_part.reshape(1)])
    gathered = all_gather_many([part], "gather_small_grads")[0]
    zero = jnp.zeros((1,), F32)
    packed = lambda pre: _pack([a[pre + n] for n in REPLICATED] + [zero])
    rep_out = adamw_sum(gathered, packed(''), packed('m_'), packed('v_'), "adamw_replicated")
    rep_out = [_unpack(o, shapes) for o in rep_out]
    loss = rep_out[0][-1][0]
    for i, n in enumerate(REPLICATED):
        results[n] = [o[i] for o in rep_out]

    def leaf(name, which):
        r = results[name]
        if isinstance(r, dict):
            return jnp.stack([r[i][which] for i in range(len(r))])
        return r[which]

    outs = [loss, dx[None]]
    for which in range(4):
        outs += [leaf(n, which) for n in WEIGHTS]
    return tuple(outs)
```

```python
import functools

import numpy as np
import jax
import jax.numpy as jnp
from jax import lax
from jax.experimental import pallas as pl
from jax.experimental.pallas import tpu as pltpu

F32, BF16 = jnp.float32, jnp.bfloat16
HI = lax.Precision.HIGHEST
V7X_VMEM_BYTES = 64 * 1024 * 1024
VMEM_LIMIT = V7X_VMEM_BYTES - 8 * 1024 * 1024
SUBLANES, LANES = 8, 128
N_DEV = 8

D_MODEL = 2048
DEPTH = 4
MIX_WIDTH = 1536
MEM_HEADS, MEM_HEAD_DIM, MEM_WIDTH = 4, 128, 512
NORM_EPS = 1e-6
SWA_HEAD_DIM, SWA_Q_HEADS, SWA_KV_HEADS, SWA_GROUP, SWA_BLOCK = 64, 24, 4, 6, 128
RWKV_HEADS, RWKV_HEAD_DIM, RWKV_GN_EPS = 24, 64, 64e-5
RWKV_DECAY_RANK, RWKV_ICLR_RANK, RWKV_GATE_RANK = 96, 96, 256
GDN_HEAD_DIM, GDN_V_HEADS, GDN_QK_HEADS, GDN_CONV, GDN_CHUNK = 128, 12, 6, 4, 64
GDN_QK_WIDTH = GDN_QK_HEADS * GDN_HEAD_DIM
GDN_CONV_WIDTH = 2 * GDN_QK_WIDTH + MIX_WIDTH
D_FF, FFN_CONV = 5632, 3
ADAM_LR, ADAM_B1, ADAM_B2, ADAM_EPS, ADAM_WD, ADAM_STEP = 0.001, 0.9, 0.999, 1e-08, 0.01, 10
MESH = pl.DeviceIdType.MESH


def _cp(sem=None):
    return pltpu.CompilerParams(dimension_semantics=sem, vmem_limit_bytes=VMEM_LIMIT)


def _tile(n, cands):
    for c in cands:
        if n % c == 0:
            return c
    return n


def _dot(a, b):
    return jnp.dot(a, b, precision=HI, preferred_element_type=F32)


def _dot_nt(a, b):
    return lax.dot_general(a, b, (((1,), (1,)), ((), ())), precision=HI, preferred_element_type=F32)


def _dot_tn(a, b):
    return lax.dot_general(a, b, (((0,), (0,)), ((), ())), precision=HI, preferred_element_type=F32)


def _sigmoid(x):
    return 1.0 / (1.0 + jnp.exp(-x))


def _softplus(x):
    return jnp.maximum(x, 0.0) + jnp.log(1.0 + jnp.exp(-jnp.abs(x)))


def _silu(x):
    return x * _sigmoid(x)


def mm(a, b, *, ta=False, tb=False, res=None, out_dtype=F32, name):
    (k_a, m) = a.shape if ta else a.shape[::-1]
    (k_b, n) = b.shape[::-1] if tb else b.shape
    assert k_a == k_b, (a.shape, b.shape, ta, tb)
    kdim = k_a
    tm = _tile(m, (1024, 512, 256))
    tn = _tile(n, (1024, 768, 512, 384, 256, 128))
    tk = _tile(kdim, (512, 256, 128))
    nk = kdim // tk
    dims = (((0 if ta else 1,), (1 if tb else 0,)), ((), ()))

    def body(*refs):
        if res is None:
            a_ref, b_ref, o_ref, acc = refs
        else:
            a_ref, b_ref, r_ref, o_ref, acc = refs
        kk = pl.program_id(2)

        @pl.when(kk == 0)
        def _():
            acc[...] = jnp.zeros_like(acc)

        acc[...] += lax.dot_general(a_ref[...].astype(BF16), b_ref[...].astype(BF16), dims,
                                    preferred_element_type=F32)

        @pl.when(kk == nk - 1)
        def _():
            out = acc[...] if res is None else acc[...] + r_ref[...]
            o_ref[...] = out.astype(o_ref.dtype)

    a_spec = pl.BlockSpec((tk, tm), lambda i, j, k: (k, i)) if ta else pl.BlockSpec((tm, tk), lambda i, j, k: (i, k))
    b_spec = pl.BlockSpec((tn, tk), lambda i, j, k: (j, k)) if tb else pl.BlockSpec((tk, tn), lambda i, j, k: (k, j))
    o_spec = pl.BlockSpec((tm, tn), lambda i, j, k: (i, j))
    in_specs, args = [a_spec, b_spec], [a, b]
    if res is not None:
        in_specs.append(o_spec)
        args.append(res)
    return pl.pallas_call(
        body, grid=(m // tm, n // tn, nk), in_specs=in_specs, out_specs=o_spec,
        out_shape=jax.ShapeDtypeStruct((m, n), out_dtype), scratch_shapes=[pltpu.VMEM((tm, tn), F32)],
        compiler_params=_cp(("parallel", "parallel", "arbitrary")), name=name)(*args)


def _coords():
    return lax.axis_index("x"), lax.axis_index("y"), lax.axis_index("c")


def _block_index(p):
    return 4 * p[0] + 2 * p[1] + p[2]


def all_gather_many(xs, name):
    n = len(xs)

    def body(*refs):
        x_refs, o_refs = refs[:n], refs[n:2 * n]
        send, recv, loc = refs[2 * n:]
        x, y, c = _coords()
        me, sib = (x, y, c), (x, y, 1 - c)
        chips = [(1 - x, y), (x, 1 - y), (1 - x, 1 - y)]

        def cp(i, k, block, to, src=None):
            dst = o_refs[i].at[_block_index(block)]
            return pltpu.make_async_remote_copy(
                src_ref=dst if src is None else src, dst_ref=dst, send_sem=send.at[i, k], recv_sem=recv.at[i, k],
                device_id=to, device_id_type=MESH)

        mine = [pltpu.make_async_copy(x_refs[i], o_refs[i].at[_block_index(me)], loc.at[i]) for i in range(n)]
        for m_ in mine:
            m_.start()
        first = []
        for i in range(n):
            first.append(cp(i, 0, me, sib, src=x_refs[i]))
            for j, chip in enumerate(chips):
                first.append(cp(i, 1 + j, me, (*chip, c), src=x_refs[i]))
        for f in first:
            f.start()
        passed = []
        for j, chip in enumerate(chips):
            for i in range(n):
                cp(i, 1 + j, (*chip, c), me).wait_recv()
                p = cp(i, 4 + j, (*chip, c), sib)
                p.start()
                passed.append(p)
        for i in range(n):
            cp(i, 0, sib, me).wait_recv()
            for j, chip in enumerate(chips):
                cp(i, 4 + j, (*chip, 1 - c), me).wait_recv()
        for f in first + passed:
            f.wait_send()
        for m_ in mine:
            m_.wait()

    any_spec = pl.BlockSpec(memory_space=pl.ANY)
    return pl.pallas_call(
        body, in_specs=[any_spec] * n, out_specs=[any_spec] * n,
        out_shape=[jax.ShapeDtypeStruct((N_DEV,) + x.shape, x.dtype) for x in xs],
        scratch_shapes=[pltpu.SemaphoreType.DMA((n, 7)), pltpu.SemaphoreType.DMA((n, 7)),
                        pltpu.SemaphoreType.DMA((n,))],
        name=name)(*xs)


def all_to_all_many(xs, name):
    n = len(xs)

    def body(*refs):
        x_refs, o_refs = refs[:n], refs[n:2 * n]
        send, recv, loc = refs[2 * n:]
        x, y, c = _coords()
        me = _block_index((x, y, c))
        mine = [pltpu.make_async_copy(x_refs[i].at[me], o_refs[i].at[me], loc.at[i]) for i in range(n)]
        for m_ in mine:
            m_.start()
        copies = []
        for r in range(1, N_DEV):
            peer = (1 - x if r & 4 else x, 1 - y if r & 2 else y, 1 - c if r & 1 else c)
            pidx = _block_index(peer)
            for i in range(n):
                copies.append((
                    pltpu.make_async_remote_copy(
                        src_ref=x_refs[i].at[pidx], dst_ref=o_refs[i].at[me], send_sem=send.at[i, r - 1],
                        recv_sem=recv.at[i, r - 1], device_id=peer, device_id_type=MESH),
                    pltpu.make_async_remote_copy(
                        src_ref=x_refs[i].at[pidx], dst_ref=o_refs[i].at[pidx], send_sem=send.at[i, r - 1],
                        recv_sem=recv.at[i, r - 1], device_id=peer, device_id_type=MESH)))
        for s, _ in copies:
            s.start()
        for s, w in copies:
            w.wait_recv()
            s.wait_send()
        for m_ in mine:
            m_.wait()

    any_spec = pl.BlockSpec(memory_space=pl.ANY)
    return pl.pallas_call(
        body, in_specs=[any_spec] * n, out_specs=[any_spec] * n,
        out_shape=[jax.ShapeDtypeStruct(x.shape, x.dtype) for x in xs],
        scratch_shapes=[pltpu.SemaphoreType.DMA((n, 7)), pltpu.SemaphoreType.DMA((n, 7)),
                        pltpu.SemaphoreType.DMA((n,))],
        name=name)(*xs)


def adamw_sum(pieces, w, m, v, name):
    rows, cols = w.shape
    tr = _tile(rows, (128, 64, 32, 16, 8))
    c1 = 1.0 - ADAM_B1 ** ADAM_STEP
    c2 = 1.0 - ADAM_B2 ** ADAM_STEP

    def body(p_ref, w_ref, m_ref, v_ref, g_out, d_out, m_out, v_out):
        g = p_ref[0].astype(F32)
        for s in range(1, N_DEV):
            g = g + p_ref[s].astype(F32)
        m_new = ADAM_B1 * m_ref[...] + (1.0 - ADAM_B1) * g
        v_new = ADAM_B2 * v_ref[...] + (1.0 - ADAM_B2) * (g * g)
        m_hat = m_new / c1
        v_hat = v_new / c2
        g_out[...] = g
        d_out[...] = -ADAM_LR * (m_hat / (jnp.sqrt(v_hat) + ADAM_EPS) + ADAM_WD * w_ref[...])
        m_out[...] = m_new
        v_out[...] = v_new

    spec = pl.BlockSpec((tr, cols), lambda i: (i, 0))
    out = jax.ShapeDtypeStruct((rows, cols), F32)
    return pl.pallas_call(
        body, grid=(rows // tr,), in_specs=[pl.BlockSpec((N_DEV, tr, cols), lambda i: (0, i, 0)), spec, spec, spec],
        out_specs=[spec] * 4, out_shape=[out] * 4, compiler_params=_cp(("parallel",)), name=name)(pieces, w, m, v)


def rmsnorm_fwd(x, g, out_dtype, name):
    s, d = x.shape
    tr = _tile(s, (256, 128, 64, 32, 16))

    def body(x_ref, g_ref, o_ref):
        xv = x_ref[...]
        rstd = lax.rsqrt(jnp.mean(xv * xv, axis=-1, keepdims=True) + NORM_EPS)
        o_ref[...] = (xv * rstd * g_ref[...]).astype(o_ref.dtype)

    return pl.pallas_call(
        body, grid=(s // tr,), in_specs=[pl.BlockSpec((tr, d), lambda i: (i, 0)), pl.BlockSpec((1, d), lambda i: (0, 0))],
        out_specs=pl.BlockSpec((tr, d), lambda i: (i, 0)), out_shape=jax.ShapeDtypeStruct((s, d), out_dtype),
        compiler_params=_cp(("parallel",)), name=name)(x, g.reshape(1, d))


def rmsnorm_bwd(x, g, dh, dres, name):
    s, d = x.shape
    tr = _tile(s, (256, 128, 64, 32, 16))

    def body(*refs):
        if dres is None:
            x_ref, g_ref, dh_ref, dx_ref, dg_ref = refs
        else:
            x_ref, g_ref, dh_ref, dr_ref, dx_ref, dg_ref = refs
        xv = x_ref[...]
        rstd = lax.rsqrt(jnp.mean(xv * xv, axis=-1, keepdims=True) + NORM_EPS)
        xhat = xv * rstd
        dhv = dh_ref[...].astype(F32)
        dhg = dhv * g_ref[...]
        dx = rstd * (dhg - xhat * jnp.mean(dhg * xhat, axis=-1, keepdims=True))
        if dres is not None:
            dx = dx + dr_ref[...]
        dx_ref[...] = dx

        @pl.when(pl.program_id(0) == 0)
        def _():
            dg_ref[...] = jnp.zeros_like(dg_ref)

        dg_ref[...] += jnp.sum(dhv * xhat, axis=0, keepdims=True)

    row = pl.BlockSpec((tr, d), lambda i: (i, 0))
    vec = pl.BlockSpec((1, d), lambda i: (0, 0))
    ins = [x, g.reshape(1, d), dh] + ([] if dres is None else [dres])
    dx, dg = pl.pallas_call(
        body, grid=(s // tr,), in_specs=[row, vec, row] + ([] if dres is None else [row]), out_specs=[row, vec],
        out_shape=[jax.ShapeDtypeStruct((s, d), F32), jax.ShapeDtypeStruct((1, d), F32)],
        compiler_params=_cp(("arbitrary",)), name=name)(*ins)
    return dx, dg.reshape(d)


def final_loss(x, g, target, name):
    s, d = x.shape
    tr = _tile(s, (256, 128, 64, 32, 16))

    def body(x_ref, g_ref, t_ref, l_ref, dx_ref, dg_ref):
        xv = x_ref[...]
        rstd = lax.rsqrt(jnp.mean(xv * xv, axis=-1, keepdims=True) + NORM_EPS)
        xhat = xv * rstd
        err = xhat * g_ref[...] - t_ref[...]
        dy = err * (1.0 / d)
        dhg = dy * g_ref[...]
        dx_ref[...] = rstd * (dhg - xhat * jnp.mean(dhg * xhat, axis=-1, keepdims=True))

        @pl.when(pl.program_id(0) == 0)
        def _():
            dg_ref[...] = jnp.zeros_like(dg_ref)
            l_ref[...] = jnp.zeros_like(l_ref)

        dg_ref[...] += jnp.sum(dy * xhat, axis=0, keepdims=True)
        part = 0.5 * jnp.sum(jnp.mean(err * err, axis=-1, keepdims=True), axis=0, keepdims=True)
        l_ref[...] += jnp.broadcast_to(part, l_ref.shape)

    row = pl.BlockSpec((tr, d), lambda i: (i, 0))
    vec = pl.BlockSpec((1, d), lambda i: (0, 0))
    lspec = pl.BlockSpec((1, LANES), lambda i: (0, 0))
    loss, dx, dg = pl.pallas_call(
        body, grid=(s // tr,), in_specs=[row, vec, row], out_specs=[lspec, row, vec],
        out_shape=[jax.ShapeDtypeStruct((1, LANES), F32), jax.ShapeDtypeStruct((s, d), F32),
                   jax.ShapeDtypeStruct((1, d), F32)],
        compiler_params=_cp(("arbitrary",)), name=name)(x, g.reshape(1, d), target)
    return loss[0, 0], dx, dg.reshape(d)


def _shift_down(tile, prev8, j):
    if j == 0:
        return tile
    rt = pltpu.roll(tile, j, 0)
    rp = pltpu.roll(prev8, j, 0)
    rows = lax.broadcasted_iota(jnp.int32, prev8.shape, 0)
    top = jnp.where(rows < j, rp, rt[:SUBLANES])
    return jnp.concatenate([top, rt[SUBLANES:]], axis=0)


def _shift_up(tile, next8, j):
    if j == 0:
        return tile
    t = tile.shape[0]
    rt = pltpu.roll(tile, t - j, 0)
    rn = pltpu.roll(next8, SUBLANES - j, 0)
    rows = lax.broadcasted_iota(jnp.int32, next8.shape, 0)
    bot = jnp.where(rows >= SUBLANES - j, rn, rt[t - SUBLANES:])
    return jnp.concatenate([rt[:t - SUBLANES], bot], axis=0)


def _halo_specs(t_rows, s_rows, cols, col_of):
    per, last = t_rows // SUBLANES, s_rows // SUBLANES - 1
    prev = pl.BlockSpec((SUBLANES, cols), lambda j, i: (jnp.maximum(i * per - 1, 0), col_of(j)))
    nxt = pl.BlockSpec((SUBLANES, cols), lambda j, i: (jnp.minimum((i + 1) * per, last), col_of(j)))
    return prev, nxt


def ffn_act_fwd(u0, conv, name):
    s, two_f = u0.shape
    f = two_f // 2
    t, c = _tile(s, (256, 128, 64)), 512
    nc = f // c

    def body(g_ref, v_ref, gp_ref, vp_ref, wg_ref, wv_ref, a_ref):
        first = (pl.program_id(1) > 0).astype(F32)

        def conv_of(x_ref, p_ref, w_ref):
            x, p = x_ref[...], p_ref[...] * first
            return (w_ref[0:1, :] * _shift_down(x, p, 2) + w_ref[1:2, :] * _shift_down(x, p, 1) + w_ref[2:3, :] * x)

        ug = conv_of(g_ref, gp_ref, wg_ref)
        uv = conv_of(v_ref, vp_ref, wv_ref)
        a_ref[...] = (_silu(ug) * uv).astype(a_ref.dtype)

    gate = pl.BlockSpec((t, c), lambda j, i: (i, j))
    val = pl.BlockSpec((t, c), lambda j, i: (i, j + nc))
    gp, _ = _halo_specs(t, s, c, lambda j: j)
    vp, _ = _halo_specs(t, s, c, lambda j: j + nc)
    wg = pl.BlockSpec((FFN_CONV, c), lambda j, i: (0, j))
    wv = pl.BlockSpec((FFN_CONV, c), lambda j, i: (0, j + nc))
    return pl.pallas_call(
        body, grid=(nc, s // t), in_specs=[gate, val, gp, vp, wg, wv], out_specs=pl.BlockSpec((t, c), lambda j, i: (i, j)),
        out_shape=jax.ShapeDtypeStruct((s, f), BF16), compiler_params=_cp(("parallel", "parallel")),
        name=name)(u0, u0, u0, u0, conv, conv)


def ffn_act_bwd(u0, conv, da, name):
    s, two_f = u0.shape
    f = two_f // 2
    t, c = _tile(s, (256, 128, 64)), 512
    nc, nt = f // c, s // t

    def body(g_ref, v_ref, gp_ref, vp_ref, gn_ref, vn_ref, wg_ref, wv_ref, da_ref, dan_ref,
             dg_ref, dv_ref, dwg_ref, dwv_ref):
        i = pl.program_id(1)
        first, last = (i > 0).astype(F32), (i < nt - 1).astype(F32)
        zeros8 = jnp.zeros((SUBLANES, c), F32)

        def ext(x_ref, p_ref, n_ref):
            return jnp.concatenate([p_ref[...] * first, x_ref[...], n_ref[...] * last], axis=0)

        def taps(e):
            return pltpu.roll(e, 2, 0), pltpu.roll(e, 1, 0), e

        def conv_of(sh, w_ref):
            return w_ref[0:1, :] * sh[0] + w_ref[1:2, :] * sh[1] + w_ref[2:3, :] * sh[2]

        def conv_t(du, w_ref):
            n = du.shape[0]
            return w_ref[2:3, :] * du + w_ref[1:2, :] * pltpu.roll(du, n - 1, 0) + w_ref[0:1, :] * pltpu.roll(du, n - 2, 0)

        sg, sv = taps(ext(g_ref, gp_ref, gn_ref)), taps(ext(v_ref, vp_ref, vn_ref))
        ug, uv = conv_of(sg, wg_ref), conv_of(sv, wv_ref)
        dae = jnp.concatenate([zeros8, da_ref[...], dan_ref[...] * last], axis=0)
        sig = _sigmoid(ug)
        dug = dae * uv * (sig * (1.0 + ug * (1.0 - sig)))
        duv = dae * (ug * sig)
        dg_ref[...] = conv_t(dug, wg_ref)[SUBLANES:t + SUBLANES].astype(dg_ref.dtype)
        dv_ref[...] = conv_t(duv, wv_ref)[SUBLANES:t + SUBLANES].astype(dv_ref.dtype)

        @pl.when(i == 0)
        def _():
            dwg_ref[...] = jnp.zeros_like(dwg_ref)
            dwv_ref[...] = jnp.zeros_like(dwv_ref)

        def dconv(du, sh):
            d = du[SUBLANES:t + SUBLANES]
            return jnp.concatenate([jnp.sum(d * x[SUBLANES:t + SUBLANES], axis=0, keepdims=True) for x in sh], axis=0)

        dwg_ref[...] += dconv(dug, sg)
        dwv_ref[...] += dconv(duv, sv)

    gate = pl.BlockSpec((t, c), lambda j, i: (i, j))
    val = pl.BlockSpec((t, c), lambda j, i: (i, j + nc))
    gp, gn = _halo_specs(t, s, c, lambda j: j)
    vp, vn = _halo_specs(t, s, c, lambda j: j + nc)
    wg = pl.BlockSpec((FFN_CONV, c), lambda j, i: (0, j))
    wv = pl.BlockSpec((FFN_CONV, c), lambda j, i: (0, j + nc))
    wout = pl.BlockSpec((FFN_CONV, c), lambda j, i: (0, j))
    half = jax.ShapeDtypeStruct((s, f), BF16)
    dwh = jax.ShapeDtypeStruct((FFN_CONV, f), F32)
    return pl.pallas_call(
        body, grid=(nc, nt), in_specs=[gate, val, gp, vp, gn, vn, wg, wv, gate, gn],
        out_specs=[gate, gate, wout, wout], out_shape=[half, half, dwh, dwh],
        compiler_params=_cp(("parallel", "arbitrary")), name=name)(u0, u0, u0, u0, u0, u0, conv, conv, da, da)


def _softmax_rows(s, extra=None):
    m = jnp.max(s, axis=-1, keepdims=True)
    if extra is not None:
        m = jnp.maximum(m, extra)
    m = lax.stop_gradient(m)
    e = jnp.exp(s - m)
    den = jnp.sum(e, axis=-1, keepdims=True)
    if extra is not None:
        den = den + jnp.exp(extra - m)
    return e / den


def _mem_attn_fn(qs, ks, vs):
    outs = []
    for q, k, v in zip(qs, ks, vs):
        p = _softmax_rows(_dot_nt(q, k) * (MEM_HEAD_DIM ** -0.5))
        outs.append(_dot(p, v))
    return outs


def _mem_heads(q_ref, kv_ref):
    d = MEM_HEAD_DIM
    qs = [q_ref[:, h * d:(h + 1) * d] for h in range(MEM_HEADS)]
    ks = [kv_ref[:, h * d:(h + 1) * d] for h in range(MEM_HEADS)]
    vs = [kv_ref[:, MEM_WIDTH + h * d:MEM_WIDTH + (h + 1) * d] for h in range(MEM_HEADS)]
    return qs, ks, vs


def mem_attn_fwd(p, q_col, kv, name):
    s = p.shape[0]
    t = _tile(s, (256, 128))
    m = kv.shape[0]

    def body(q_ref, kv_ref, o_ref):
        outs = _mem_attn_fn(*_mem_heads(q_ref, kv_ref))
        o_ref[...] = jnp.concatenate(outs, axis=1).astype(o_ref.dtype)

    return pl.pallas_call(
        body, grid=(s // t,),
        in_specs=[pl.BlockSpec((t, MEM_WIDTH), lambda i: (i, q_col // MEM_WIDTH)),
                  pl.BlockSpec((m, 2 * MEM_WIDTH), lambda i: (0, 0))],
        out_specs=pl.BlockSpec((t, MEM_WIDTH), lambda i: (i, 0)), out_shape=jax.ShapeDtypeStruct((s, MEM_WIDTH), BF16),
        compiler_params=_cp(("parallel",)), name=name)(p, kv)


def mem_attn_bwd(p, q_col, kv, dcat, name):
    s = p.shape[0]
    t = _tile(s, (256, 128))
    m = kv.shape[0]
    d = MEM_HEAD_DIM

    def body(q_ref, kv_ref, dy_ref, dq_ref, dkv_ref):
        qs, ks, vs = _mem_heads(q_ref, kv_ref)
        _, vjp = jax.vjp(_mem_attn_fn, qs, ks, vs)
        dqs, dks, dvs = vjp([dy_ref[:, h * d:(h + 1) * d] for h in range(MEM_HEADS)])
        dq_ref[...] = jnp.concatenate(dqs, axis=1).astype(dq_ref.dtype)

        @pl.when(pl.program_id(0) == 0)
        def _():
            dkv_ref[...] = jnp.zeros_like(dkv_ref)

        dkv_ref[...] += jnp.concatenate(dks + dvs, axis=1)

    return pl.pallas_call(
        body, grid=(s // t,),
        in_specs=[pl.BlockSpec((t, MEM_WIDTH), lambda i: (i, q_col // MEM_WIDTH)),
                  pl.BlockSpec((m, 2 * MEM_WIDTH), lambda i: (0, 0)),
                  pl.BlockSpec((t, MEM_WIDTH), lambda i: (i, MIX_WIDTH // MEM_WIDTH))],
        out_specs=[pl.BlockSpec((t, MEM_WIDTH), lambda i: (i, 0)), pl.BlockSpec((m, 2 * MEM_WIDTH), lambda i: (0, 0))],
        out_shape=[jax.ShapeDtypeStruct((s, MEM_WIDTH), BF16), jax.ShapeDtypeStruct((m, 2 * MEM_WIDTH), F32)],
        compiler_params=_cp(("arbitrary",)), name=name)(p, kv, dcat)


def _swa_fn(qs, kcs, kps, vcs, vps, sinks, not_first):
    t = SWA_BLOCK
    qi = lax.broadcasted_iota(jnp.int32, (t, 2 * t), 0)
    kj = lax.broadcasted_iota(jnp.int32, (t, 2 * t), 1)
    dist = t + qi - kj
    valid = (dist >= 0) & (dist < t) & ((kj >= t) | not_first)
    distf = dist.astype(F32)
    outs = []
    for kh in range(SWA_KV_HEADS):
        kb = jnp.concatenate([kps[kh], kcs[kh]], axis=0)
        vb = jnp.concatenate([vps[kh], vcs[kh]], axis=0)
        for g in range(SWA_GROUP):
            h = kh * SWA_GROUP + g
            slope = 2.0 ** (-8.0 * (h + 1) / SWA_Q_HEADS)
            sc = _dot_nt(qs[h], kb) * (SWA_HEAD_DIM ** -0.5) - slope * distf
            sc = jnp.where(valid, sc, -jnp.inf)
            outs.append(_dot(_softmax_rows(sc, extra=sinks[h]), vb))
    return outs


def _swa_args(q_ref, kc_ref, kp_ref, vc_ref, vp_ref, sink_ref):
    d = SWA_HEAD_DIM
    qs = [q_ref[:, h * d:(h + 1) * d] for h in range(SWA_Q_HEADS)]
    per_kv = lambda ref: [ref[:, h * d:(h + 1) * d] for h in range(SWA_KV_HEADS)]
    sinks = [sink_ref[0:1, h:h + 1] for h in range(SWA_Q_HEADS)]
    return qs, per_kv(kc_ref), per_kv(kp_ref), per_kv(vc_ref), per_kv(vp_ref), sinks


def _swa_specs(nb, order):
    t, kvw = SWA_BLOCK, SWA_KV_HEADS * SWA_HEAD_DIM
    k_col, v_col = MIX_WIDTH // kvw, MIX_WIDTH // kvw + 1
    q = pl.BlockSpec((t, MIX_WIDTH), lambda n: (order(n), 0))
    kc = pl.BlockSpec((t, kvw), lambda n: (order(n), k_col))
    kp = pl.BlockSpec((t, kvw), lambda n: (jnp.maximum(order(n) - 1, 0), k_col))
    vc = pl.BlockSpec((t, kvw), lambda n: (order(n), v_col))
    vp = pl.BlockSpec((t, kvw), lambda n: (jnp.maximum(order(n) - 1, 0), v_col))
    sink = pl.BlockSpec((1, LANES), lambda n: (0, 0))
    return [q, kc, kp, vc, vp, sink]


def _pad_lanes(v):
    return jnp.pad(v.reshape(1, -1), ((0, 0), (0, LANES - v.size)))


def swa_fwd(p, sinks, name):
    s = p.shape[0]
    nb = s // SWA_BLOCK

    def body(q_ref, kc_ref, kp_ref, vc_ref, vp_ref, sink_ref, o_ref):
        outs = _swa_fn(*_swa_args(q_ref, kc_ref, kp_ref, vc_ref, vp_ref, sink_ref), pl.program_id(0) > 0)
        o_ref[...] = jnp.concatenate(outs, axis=1).astype(o_ref.dtype)

    return pl.pallas_call(
        body, grid=(nb,), in_specs=_swa_specs(nb, lambda n: n),
        out_specs=pl.BlockSpec((SWA_BLOCK, MIX_WIDTH), lambda n: (n, 0)),
        out_shape=jax.ShapeDtypeStruct((s, MIX_WIDTH), BF16), compiler_params=_cp(("parallel",)),
        name=name)(p, p, p, p, p, _pad_lanes(sinks))


def swa_bwd(p, sinks, dcat, name):
    s = p.shape[0]
    nb = s // SWA_BLOCK
    t, d, kvw = SWA_BLOCK, SWA_HEAD_DIM, SWA_KV_HEADS * SWA_HEAD_DIM
    rev = lambda n: nb - 1 - n

    def body(q_ref, kc_ref, kp_ref, vc_ref, vp_ref, sink_ref, dy_ref, dq_ref, dk_ref, dv_ref, ds_ref, ck, cv):
        n = pl.program_id(0)

        @pl.when(n == 0)
        def _():
            ck[...] = jnp.zeros_like(ck)
            cv[...] = jnp.zeros_like(cv)
            ds_ref[...] = jnp.zeros_like(ds_ref)

        args = _swa_args(q_ref, kc_ref, kp_ref, vc_ref, vp_ref, sink_ref)
        _, vjp = jax.vjp(functools.partial(_swa_fn, not_first=rev(n) > 0), *args)
        dqs, dkcs, dkps, dvcs, dvps, dsinks = vjp([dy_ref[:, h * d:(h + 1) * d] for h in range(SWA_Q_HEADS)])
        dq_ref[...] = jnp.concatenate(dqs, axis=1).astype(dq_ref.dtype)
        dk_ref[...] = (jnp.concatenate(dkcs, axis=1) + ck[...]).astype(dk_ref.dtype)
        dv_ref[...] = (jnp.concatenate(dvcs, axis=1) + cv[...]).astype(dv_ref.dtype)
        ck[...] = jnp.concatenate(dkps, axis=1)
        cv[...] = jnp.concatenate(dvps, axis=1)
        lane = lax.broadcasted_iota(jnp.int32, (1, LANES), 1)
        acc = jnp.zeros((1, LANES), F32)
        for h in range(SWA_Q_HEADS):
            acc = acc + jnp.where(lane == h, dsinks[h], 0.0)
        ds_ref[...] += acc

    dy = pl.BlockSpec((t, MIX_WIDTH), lambda n: (rev(n), 0))
    kv_out = pl.BlockSpec((t, kvw), lambda n: (rev(n), 0))
    dq, dk, dv, ds = pl.pallas_call(
        body, grid=(nb,), in_specs=_swa_specs(nb, rev) + [dy],
        out_specs=[dy, kv_out, kv_out, pl.BlockSpec((1, LANES), lambda n: (0, 0))],
        out_shape=[jax.ShapeDtypeStruct((s, MIX_WIDTH), BF16), jax.ShapeDtypeStruct((s, kvw), BF16),
                   jax.ShapeDtypeStruct((s, kvw), BF16), jax.ShapeDtypeStruct((1, LANES), F32)],
        scratch_shapes=[pltpu.VMEM((t, kvw), F32), pltpu.VMEM((t, kvw), F32)],
        compiler_params=_cp(("arbitrary",)), name=name)(p, p, p, p, p, _pad_lanes(sinks), dcat)
    return dq, dk, dv, ds[0, :SWA_Q_HEADS]


RW_SHIFT = 5120
RW_R, RW_K, RW_V, RW_WD, RW_AD, RW_GD = 0, 1536, 3072, 4608, 4736, 4864


def _head_matrix(width, head_dim):
    e = (np.arange(width)[:, None] // head_dim == np.arange(LANES)[None, :]).astype(np.float32)
    return jnp.asarray(e), jnp.asarray(e.T)


def _rwkv_pre_fn(pieces, shifted, mus, w0, wdu, a0, wiu, wgu, k_k, k_a, e, et):
    r, k, v, wd, ad, gd = [p + (s - p) * mu for p, s, mu in zip(pieces, shifted, mus)]
    w_log = -_softplus(-(w0 + _dot(jnp.tanh(wd), wdu))) - 0.5
    lw = -jnp.exp(w_log)
    a = _sigmoid(a0 + _dot(ad, wiu))
    g = _dot(_sigmoid(gd), wgu)
    kkr = k * k_k
    kk = kkr * _dot(lax.rsqrt(_dot(kkr * kkr, e) + 1e-6), et)
    k2 = k * (1.0 + (a - 1.0) * k_a)
    return r, lw, k2, v, kk, kk * a, g


_RW_GROUPS = ((RW_R, MIX_WIDTH), (RW_K, MIX_WIDTH), (RW_V, MIX_WIDTH), (RW_WD, LANES), (RW_AD, LANES), (RW_GD, 2 * LANES))


def _rwkv_pre_inputs(p_ref, prev_ref, mu_ref, first):
    pieces = [p_ref[:, o:o + n] for o, n in _RW_GROUPS]
    shifted = [_shift_down(p_ref[:, o:o + n], prev_ref[:, o:o + n] * first, 1) for o, n in _RW_GROUPS]
    mus = [mu_ref[:, o:o + n] for o, n in _RW_GROUPS]
    return pieces, shifted, mus


def _rwkv_param_specs():
    vec = lambda n: pl.BlockSpec((1, n), lambda i: (0, 0))
    mat = lambda r, c: pl.BlockSpec((r, c), lambda i: (0, 0))
    return [vec(RW_SHIFT), vec(MIX_WIDTH), mat(LANES, MIX_WIDTH), vec(MIX_WIDTH), mat(LANES, MIX_WIDTH),
            mat(2 * LANES, MIX_WIDTH), vec(MIX_WIDTH), vec(MIX_WIDTH), mat(MIX_WIDTH, LANES), mat(LANES, MIX_WIDTH)]


def rwkv_pre_fwd(p, params, name):
    s = p.shape[0]
    t = _tile(s, (128, 64))

    def body(p_ref, prev_ref, mu_ref, *rest):
        prm, outs = rest[:9], rest[9:]
        first = (pl.program_id(0) > 0).astype(F32)
        pieces, shifted, mus = _rwkv_pre_inputs(p_ref, prev_ref, mu_ref, first)
        res = _rwkv_pre_fn(pieces, shifted, mus, *[q[...] for q in prm])
        for o_ref, val in zip(outs, res):
            o_ref[...] = val

    row = pl.BlockSpec((t, RW_SHIFT), lambda i: (i, 0))
    prev = pl.BlockSpec((SUBLANES, RW_SHIFT), lambda i: (jnp.maximum(i * (t // SUBLANES) - 1, 0), 0))
    out = pl.BlockSpec((t, MIX_WIDTH), lambda i: (i, 0))
    return pl.pallas_call(
        body, grid=(s // t,), in_specs=[row, prev] + _rwkv_param_specs(), out_specs=[out] * 7,
        out_shape=[jax.ShapeDtypeStruct((s, MIX_WIDTH), F32)] * 7, compiler_params=_cp(("parallel",)),
        name=name)(p, p, *params)


def rwkv_pre_bwd(p, params, cots, name):
    s = p.shape[0]
    t = _tile(s, (64, 32))

    def body(p_ref, prev_ref, mu_ref, *rest):
        prm, cot, outs = rest[:9], rest[9:19], rest[19:]
        dp_ref, dps_ref, grads = outs[0], outs[1], outs[2:]
        i = pl.program_id(0)
        first = (i > 0).astype(F32)
        pieces, shifted, mus = _rwkv_pre_inputs(p_ref, prev_ref, mu_ref, first)
        prm_v = [q[...] for q in prm]
        fn = lambda pieces, shifted, mus, *small: _rwkv_pre_fn(pieces, shifted, mus, *small, prm_v[7], prm_v[8])
        _, vjp = jax.vjp(fn, pieces, shifted, mus, *prm_v[:7])
        dr, dw, dk2, dv, dkk, db, dr2, dk22, dv2, dg = [c[...] for c in cot]
        res = vjp((dr + dr2, dw, dk2 + dk22, dv + dv2, dkk, db, dg))
        dpieces, dshifted, dmus, dsmall = res[0], res[1], res[2], res[3:]
        for (o, n), dpi, dsi in zip(_RW_GROUPS, dpieces, dshifted):
            dp_ref[:, o:o + n] = dpi
            dps_ref[:, o:o + n] = dsi

        @pl.when(i == 0)
        def _():
            for g_ref in grads:
                g_ref[...] = jnp.zeros_like(g_ref)

        for (o, n), dmu in zip(_RW_GROUPS, dmus):
            grads[0][:, o:o + n] += dmu
        for g_ref, dval in zip(grads[1:], dsmall):
            g_ref[...] += dval

    row = pl.BlockSpec((t, RW_SHIFT), lambda i: (i, 0))
    prev = pl.BlockSpec((SUBLANES, RW_SHIFT), lambda i: (jnp.maximum(i * (t // SUBLANES) - 1, 0), 0))
    act = pl.BlockSpec((t, MIX_WIDTH), lambda i: (i, 0))
    pspecs = _rwkv_param_specs()
    full = jax.ShapeDtypeStruct((s, RW_SHIFT), F32)
    gshapes = [jax.ShapeDtypeStruct(q.shape, F32) for q in params[:8]]
    return pl.pallas_call(
        body, grid=(s // t,), in_specs=[row, prev] + pspecs + [act] * 10, out_specs=[row, row] + pspecs[:8],
        out_shape=[full, full] + gshapes, compiler_params=_cp(("arbitrary",)), name=name)(p, p, *params, *cots)


def shift_add(a, b, js, out_dtype, name):
    s, c = a.shape
    t = _tile(s, (256, 128, 64))
    tc = _tile(c, (1024, 768, 512, 640, 384, 256, 128))
    nt, nb = s // t, len(b)

    def body(a_ref, *rest):
        b_refs, n_refs, o_ref = rest[:nb], rest[nb:2 * nb], rest[2 * nb]
        last = (pl.program_id(1) < nt - 1).astype(F32)
        acc = a_ref[...]
        for b_ref, n_ref, j in zip(b_refs, n_refs, js):
            acc = acc + _shift_up(b_ref[...], n_ref[...] * last, j)
        o_ref[...] = acc.astype(o_ref.dtype)

    tile = pl.BlockSpec((t, tc), lambda j, i: (i, j))
    _, nxt = _halo_specs(t, s, tc, lambda j: j)
    return pl.pallas_call(
        body, grid=(c // tc, nt), in_specs=[tile] * (1 + nb) + [nxt] * nb, out_specs=tile,
        out_shape=jax.ShapeDtypeStruct((s, c), out_dtype), compiler_params=_cp(("parallel", "parallel")),
        name=name)(a, *b, *b)


def _rwkv_post_fn(y, r, k2, v, g, gn_g, gn_b, r_k, e, et):
    n = RWKV_HEAD_DIM
    yc = y - _dot(_dot(y, e), et) * (1.0 / n)
    rstd = lax.rsqrt(_dot(yc * yc, e) * (1.0 / n) + RWKV_GN_EPS)
    yn = yc * _dot(rstd, et) * gn_g + gn_b
    bonus = _dot(_dot(r * k2 * r_k, e), et) * v
    return (yn + bonus) * g


def rwkv_post_fwd(acts, params, name):
    s = acts[0].shape[0]
    t = _tile(s, (256, 128))

    def body(*refs):
        vals = [q[...] for q in refs[:10]]
        refs[10][...] = _rwkv_post_fn(*vals).astype(refs[10].dtype)

    act = pl.BlockSpec((t, MIX_WIDTH), lambda i: (i, 0))
    vec = pl.BlockSpec((1, MIX_WIDTH), lambda i: (0, 0))
    mats = [pl.BlockSpec((MIX_WIDTH, LANES), lambda i: (0, 0)), pl.BlockSpec((LANES, MIX_WIDTH), lambda i: (0, 0))]
    return pl.pallas_call(
        body, grid=(s // t,), in_specs=[act] * 5 + [vec] * 3 + mats, out_specs=act,
        out_shape=jax.ShapeDtypeStruct((s, MIX_WIDTH), BF16), compiler_params=_cp(("parallel",)),
        name=name)(*acts, *params)


def rwkv_post_bwd(acts, params, dcat, name):
    s = acts[0].shape[0]
    t = _tile(s, (128, 64))

    def body(*refs):
        ins, dy_ref, outs = refs[:10], refs[10], refs[11:]
        vals = [q[...] for q in ins]
        fn = lambda *a: _rwkv_post_fn(*a, vals[8], vals[9])
        _, vjp = jax.vjp(fn, *vals[:8])
        res = vjp(dy_ref[...])
        for o_ref, val in zip(outs[:5], res[:5]):
            o_ref[...] = val

        @pl.when(pl.program_id(0) == 0)
        def _():
            for g_ref in outs[5:]:
                g_ref[...] = jnp.zeros_like(g_ref)

        for g_ref, val in zip(outs[5:], res[5:]):
            g_ref[...] += val

    act = pl.BlockSpec((t, MIX_WIDTH), lambda i: (i, 0))
    vec = pl.BlockSpec((1, MIX_WIDTH), lambda i: (0, 0))
    mats = [pl.BlockSpec((MIX_WIDTH, LANES), lambda i: (0, 0)), pl.BlockSpec((LANES, MIX_WIDTH), lambda i: (0, 0))]
    a_shape = jax.ShapeDtypeStruct((s, MIX_WIDTH), F32)
    v_shape = jax.ShapeDtypeStruct((1, MIX_WIDTH), F32)
    return pl.pallas_call(
        body, grid=(s // t,), in_specs=[act] * 5 + [vec] * 3 + mats + [act], out_specs=[act] * 5 + [vec] * 3,
        out_shape=[a_shape] * 5 + [v_shape] * 3, compiler_params=_cp(("arbitrary",)), name=name)(*acts, *params, dcat)


RW_CHUNK = 64


RW_HEADS_PER_STEP = 4


def _rwkv_chunk_fn(r, lw, k, v, kk, b, st):
    c = RW_CHUNK
    ri = lax.broadcasted_iota(jnp.int32, (c, c), 0)
    ci = lax.broadcasted_iota(jnp.int32, (c, c), 1)
    incl, strict = ri >= ci, ri > ci
    eye = (ri == ci).astype(F32)
    last_col = (ci == c - 1).astype(F32)
    last_row = (ri == c - 1).astype(F32)
    gc = _dot(incl.astype(F32), lw)
    a_t = -kk * jnp.exp(gc - lw)
    e_neg = jnp.exp(-gc)
    b_t, k_t, r_t = b * e_neg, k * e_neg, r * jnp.exp(gc)
    m_ab = jnp.where(strict, _dot_nt(a_t, b_t), 0.0)
    m_ak = jnp.where(strict, _dot_nt(a_t, k_t), 0.0)
    m_rb = jnp.where(incl, _dot_nt(r_t, b_t), 0.0)
    m_rk = jnp.where(incl, _dot_nt(r_t, k_t), 0.0)
    tinv, pw = eye + m_ab, m_ab
    for _ in range(5):
        pw = _dot(pw, pw)
        tinv = tinv + _dot(tinv, pw)
    u = _dot(tinv, _dot(a_t, st) + _dot(m_ak, v))
    y = _dot(r_t, st) + _dot(m_rb, u) + _dot(m_rk, v)
    dec = jnp.exp(_dot(last_col, gc) - gc)
    g_end = _dot_tn(gc, last_row)
    new_st = st * jnp.exp(g_end) + _dot_tn(b * dec, u) + _dot_tn(k * dec, v)
    return y, new_st


def rwkv_scan_fwd(r, lw, k, v, kk, b, name):
    s = r.shape[0]
    n, hp = RWKV_HEAD_DIM, RW_HEADS_PER_STEP
    nchunk, width = s // RW_CHUNK, RWKV_HEAD_DIM * RW_HEADS_PER_STEP

    def body(r_ref, w_ref, k_ref, v_ref, kk_ref, b_ref, y_ref, ck_ref, carry):
        @pl.when(pl.program_id(1) == 0)
        def _():
            carry[...] = jnp.zeros_like(carry)

        ck_ref[0] = carry[...]
        ys, sts = [], []
        for h in range(hp):
            cols = slice(h * n, (h + 1) * n)
            y, st = _rwkv_chunk_fn(*[q[:, cols] for q in (r_ref, w_ref, k_ref, v_ref, kk_ref, b_ref)], carry[:, cols])
            ys.append(y)
            sts.append(st)
        y_ref[...] = jnp.concatenate(ys, axis=1)
        carry[...] = jnp.concatenate(sts, axis=1)

    blk = pl.BlockSpec((RW_CHUNK, width), lambda j, c: (c, j))
    return pl.pallas_call(
        body, grid=(MIX_WIDTH // width, nchunk), in_specs=[blk] * 6,
        out_specs=[blk, pl.BlockSpec((1, n, width), lambda j, c: (c, 0, j))],
        out_shape=[jax.ShapeDtypeStruct((s, MIX_WIDTH), F32), jax.ShapeDtypeStruct((nchunk, n, MIX_WIDTH), F32)],
        scratch_shapes=[pltpu.VMEM((n, width), F32)],
        compiler_params=_cp(("parallel", "arbitrary")), name=name)(r, lw, k, v, kk, b)


def rwkv_scan_bwd(r, lw, k, v, kk, b, ck, dy, name):
    s = r.shape[0]
    n, hp = RWKV_HEAD_DIM, RW_HEADS_PER_STEP
    nchunk, width = s // RW_CHUNK, RWKV_HEAD_DIM * RW_HEADS_PER_STEP
    rev = lambda c: nchunk - 1 - c

    def body(r_ref, w_ref, k_ref, v_ref, kk_ref, b_ref, ck_ref, dy_ref, *rest):
        outs, carry = rest[:6], rest[6]

        @pl.when(pl.program_id(1) == 0)
        def _():
            carry[...] = jnp.zeros_like(carry)

        grads = []
        for h in range(hp):
            cols = slice(h * n, (h + 1) * n)
            args = [q[:, cols] for q in (r_ref, w_ref, k_ref, v_ref, kk_ref, b_ref)] + [ck_ref[0, :, cols]]
            _, vjp = jax.vjp(_rwkv_chunk_fn, *args)
            grads.append(vjp((dy_ref[:, cols], carry[:, cols])))
        for q in range(6):
            outs[q][...] = jnp.concatenate([g[q] for g in grads], axis=1)
        carry[...] = jnp.concatenate([g[6] for g in grads], axis=1)

    blk = pl.BlockSpec((RW_CHUNK, width), lambda j, c: (rev(c), j))
    out = jax.ShapeDtypeStruct((s, MIX_WIDTH), F32)
    return pl.pallas_call(
        body, grid=(MIX_WIDTH // width, nchunk),
        in_specs=[blk] * 6 + [pl.BlockSpec((1, n, width), lambda j, c: (rev(c), 0, j)), blk],
        out_specs=[blk] * 6, out_shape=[out] * 6, scratch_shapes=[pltpu.VMEM((n, width), F32)],
        compiler_params=_cp(("parallel", "arbitrary")), name=name)(r, lw, k, v, kk, b, ck, dy)


GD_Q, GD_K, GD_V, GD_Z, GD_QMEM, GD_BT, GD_AT, GD_COLS = 0, 768, 1536, 3072, 4608, 5120, 5248, 5376
_GD_GROUPS = ((GD_Q, GDN_QK_WIDTH), (GD_K, GDN_QK_WIDTH), (GD_V, MIX_WIDTH))


def _gdn_pre_fn(xs, convs, bt, at, a_log, dt_bias, e6, e6t, ebc):
    k_w = GDN_CONV
    acts = [_silu(sum(convs[g][j] * xs[g][k_w - 1 - j] for j in range(k_w))) for g in range(3)]
    l2 = lambda x: x * _dot(lax.rsqrt(_dot(x * x, e6) + 1e-6), e6t)
    beta = _sigmoid(bt)
    g = -jnp.exp(a_log) * _softplus(at + dt_bias)
    return l2(acts[0]), l2(acts[1]), acts[2], _dot(g, ebc), _dot(beta, ebc)


def _gdn_pre_inputs(x_ref, prev_ref, conv_ref, first):
    xs = [[_shift_down(x_ref[:, o:o + n], prev_ref[:, o:o + n] * first, j) for j in range(GDN_CONV)]
          for o, n in _GD_GROUPS]
    convs = [[conv_ref[j:j + 1, o:o + n] for j in range(GDN_CONV)] for o, n in _GD_GROUPS]
    return xs, convs


def _gdn_pre_specs(t):
    x = pl.BlockSpec((t, GDN_CONV_WIDTH), lambda i: (i, 0))
    prev = pl.BlockSpec((SUBLANES, GDN_CONV_WIDTH), lambda i: (jnp.maximum(i * (t // SUBLANES) - 1, 0), 0))
    bta = pl.BlockSpec((t, 2 * LANES), lambda i: (i, GD_BT // (2 * LANES)))
    conv = pl.BlockSpec((GDN_CONV, GDN_CONV_WIDTH), lambda i: (0, 0))
    vec = pl.BlockSpec((1, LANES), lambda i: (0, 0))
    mats = [pl.BlockSpec((GDN_QK_WIDTH, LANES), lambda i: (0, 0)), pl.BlockSpec((LANES, GDN_QK_WIDTH), lambda i: (0, 0)),
            pl.BlockSpec((LANES, MIX_WIDTH), lambda i: (0, 0))]
    return [x, prev, bta, conv, vec, vec] + mats


def gdn_pre_fwd(p, params, name):
    s = p.shape[0]
    t = _tile(s, (128, 64))

    def body(x_ref, prev_ref, bta_ref, conv_ref, al_ref, dt_ref, e6_ref, e6t_ref, ebc_ref, *outs):
        first = (pl.program_id(0) > 0).astype(F32)
        xs, convs = _gdn_pre_inputs(x_ref, prev_ref, conv_ref, first)
        res = _gdn_pre_fn(xs, convs, bta_ref[:, :LANES], bta_ref[:, LANES:], al_ref[...], dt_ref[...],
                          e6_ref[...], e6t_ref[...], ebc_ref[...])
        for o_ref, val in zip(outs, res):
            o_ref[...] = val

    qk = pl.BlockSpec((t, GDN_QK_WIDTH), lambda i: (i, 0))
    wide = pl.BlockSpec((t, MIX_WIDTH), lambda i: (i, 0))
    qk_s, wide_s = jax.ShapeDtypeStruct((s, GDN_QK_WIDTH), F32), jax.ShapeDtypeStruct((s, MIX_WIDTH), F32)
    return pl.pallas_call(
        body, grid=(s // t,), in_specs=_gdn_pre_specs(t), out_specs=[qk, qk, wide, wide, wide],
        out_shape=[qk_s, qk_s, wide_s, wide_s, wide_s], compiler_params=_cp(("parallel",)), name=name)(p, p, p, *params)


def gdn_pre_bwd(p, params, cots, name):
    s = p.shape[0]
    t = _tile(s, (64, 32))

    def body(x_ref, prev_ref, bta_ref, conv_ref, al_ref, dt_ref, e6_ref, e6t_ref, ebc_ref, *rest):
        cot, outs = rest[:5], rest[5:]
        dxs, dbta_ref, dconv_ref, dal_ref, ddt_ref = outs[:4], outs[4], outs[5], outs[6], outs[7]
        i = pl.program_id(0)
        first = (i > 0).astype(F32)
        xs, convs = _gdn_pre_inputs(x_ref, prev_ref, conv_ref, first)
        mats = (e6_ref[...], e6t_ref[...], ebc_ref[...])
        fn = lambda xs, convs, bt, at, al, dt: _gdn_pre_fn(xs, convs, bt, at, al, dt, *mats)
        _, vjp = jax.vjp(fn, xs, convs, bta_ref[:, :LANES], bta_ref[:, LANES:], al_ref[...], dt_ref[...])
        d_xs, d_convs, d_bt, d_at, d_al, d_dt = vjp(tuple(c[...] for c in cot))
        for g, (o, n) in enumerate(_GD_GROUPS):
            for j in range(GDN_CONV):
                dxs[j][:, o:o + n] = d_xs[g][j]
        dbta_ref[...] = jnp.concatenate([d_bt, d_at], axis=1).astype(dbta_ref.dtype)

        @pl.when(i == 0)
        def _():
            dconv_ref[...] = jnp.zeros_like(dconv_ref)
            dal_ref[...] = jnp.zeros_like(dal_ref)
            ddt_ref[...] = jnp.zeros_like(ddt_ref)

        for g, (o, n) in enumerate(_GD_GROUPS):
            for j in range(GDN_CONV):
                dconv_ref[j:j + 1, o:o + n] += d_convs[g][j]
        dal_ref[...] += d_al
        ddt_ref[...] += d_dt

    specs = _gdn_pre_specs(t)
    qk = pl.BlockSpec((t, GDN_QK_WIDTH), lambda i: (i, 0))
    wide = pl.BlockSpec((t, MIX_WIDTH), lambda i: (i, 0))
    x_s = jax.ShapeDtypeStruct((s, GDN_CONV_WIDTH), F32)
    vec_s = jax.ShapeDtypeStruct((1, LANES), F32)
    return pl.pallas_call(
        body, grid=(s // t,), in_specs=specs + [qk, qk, wide, wide, wide],
        out_specs=[specs[0]] * 4 + [pl.BlockSpec((t, 2 * LANES), lambda i: (i, 0)), specs[3], specs[4], specs[5]],
        out_shape=[x_s] * 4 + [jax.ShapeDtypeStruct((s, 2 * LANES), BF16),
                               jax.ShapeDtypeStruct((GDN_CONV, GDN_CONV_WIDTH), F32), vec_s, vec_s],
        compiler_params=_cp(("arbitrary",)), name=name)(p, p, p, *params, *cots)


def _gdn_post_fn(o, z, norm_g, e12, e12t, trep):
    rstd = lax.rsqrt(_dot(o * o, e12) * (1.0 / GDN_HEAD_DIM) + NORM_EPS)
    return o * _dot(rstd, e12t) * _dot(norm_g, trep) * _silu(z)


def _gdn_post_specs(t):
    act = pl.BlockSpec((t, MIX_WIDTH), lambda i: (i, 0))
    z = pl.BlockSpec((t, MIX_WIDTH), lambda i: (i, GD_Z // MIX_WIDTH))
    mats = [pl.BlockSpec((SUBLANES, LANES), lambda i: (0, 0)), pl.BlockSpec((MIX_WIDTH, LANES), lambda i: (0, 0)),
            pl.BlockSpec((LANES, MIX_WIDTH), lambda i: (0, 0)), pl.BlockSpec((LANES, MIX_WIDTH), lambda i: (0, 0))]
    return [act, z] + mats


def gdn_post_fwd(o, p, params, name):
    s = o.shape[0]
    t = _tile(s, (256, 128))

    def body(o_ref, z_ref, ng_ref, e_ref, et_ref, tr_ref, out_ref):
        res = _gdn_post_fn(o_ref[...], z_ref[...], ng_ref[0:1, :], e_ref[...], et_ref[...], tr_ref[...])
        out_ref[...] = res.astype(out_ref.dtype)

    act = pl.BlockSpec((t, MIX_WIDTH), lambda i: (i, 0))
    return pl.pallas_call(
        body, grid=(s // t,), in_specs=_gdn_post_specs(t), out_specs=act,
        out_shape=jax.ShapeDtypeStruct((s, MIX_WIDTH), BF16), compiler_params=_cp(("parallel",)),
        name=name)(o, p, *params)


def gdn_post_bwd(o, p, params, dcat, name):
    s = o.shape[0]
    t = _tile(s, (128, 64))

    def body(o_ref, z_ref, ng_ref, e_ref, et_ref, tr_ref, dy_ref, do_ref, dz_ref, dng_ref):
        mats = (e_ref[...], et_ref[...], tr_ref[...])
        fn = lambda o, z, ng: _gdn_post_fn(o, z, ng, *mats)
        _, vjp = jax.vjp(fn, o_ref[...], z_ref[...], ng_ref[0:1, :])
        d_o, d_z, d_ng = vjp(dy_ref[...])
        do_ref[...] = d_o
        dz_ref[...] = d_z.astype(dz_ref.dtype)

        @pl.when(pl.program_id(0) == 0)
        def _():
            dng_ref[...] = jnp.zeros_like(dng_ref)

        dng_ref[...] += d_ng

    act = pl.BlockSpec((t, MIX_WIDTH), lambda i: (i, 0))
    return pl.pallas_call(
        body, grid=(s // t,), in_specs=_gdn_post_specs(t) + [act],
        out_specs=[act, act, pl.BlockSpec((1, LANES), lambda i: (0, 0))],
        out_shape=[jax.ShapeDtypeStruct((s, MIX_WIDTH), F32), jax.ShapeDtypeStruct((s, MIX_WIDTH), BF16),
                   jax.ShapeDtypeStruct((1, LANES), F32)],
        compiler_params=_cp(("arbitrary",)), name=name)(o, p, *params, dcat)


def _gdn_chunk_fn(q, k, v, gb, bb, gb64, state):
    c = GDN_CHUNK
    ri = lax.broadcasted_iota(jnp.int32, (c, c), 0)
    ci = lax.broadcasted_iota(jnp.int32, (c, c), 1)
    causal, strict = ri >= ci, ri > ci
    ltri = causal.astype(F32)
    eye = (ri == ci).astype(F32)
    first_col = (ci == 0).astype(F32)
    last_col = (ci == c - 1).astype(F32)
    last_col_tall = (lax.broadcasted_iota(jnp.int32, (GDN_HEAD_DIM, c), 1) == c - 1).astype(F32)

    qs = q * (GDN_HEAD_DIM ** -0.5)
    gc = _dot(ltri, gb)
    gd = _dot(ltri, gb64)
    diff = gd - _dot_nt(first_col, gd)
    decay = jnp.exp(jnp.where(causal, diff, -jnp.inf))
    kb = k * bb
    lmat = jnp.where(strict, _dot_nt(kb, k) * decay, 0.0)
    tmat, pw = eye - lmat, lmat
    for _ in range(5):
        pw = _dot(pw, pw)
        tmat = tmat + _dot(tmat, pw)
    eg = jnp.exp(gc)
    u = _dot(tmat, v * bb)
    w = _dot(tmat, kb * eg)
    a_qk = jnp.where(causal, _dot_nt(qs, k) * decay, 0.0)
    g_last = _dot(last_col, gc)
    k_dec = k * jnp.exp(g_last - gc)
    v_new = u - _dot(w, state)
    out = _dot(qs * eg, state) + _dot(a_qk, v_new)
    new_state = state * jnp.exp(_dot(last_col_tall, gc)) + _dot_tn(k_dec, v_new)
    return out, new_state


GDN_REP = GDN_V_HEADS // GDN_QK_HEADS


def _gdn_chunk_specs(order):
    c, d = GDN_CHUNK, GDN_HEAD_DIM
    qk = pl.BlockSpec((c, d), lambda j, n: (order(n), j))
    vh = pl.BlockSpec((c, GDN_REP * d), lambda j, n: (order(n), j))
    st = pl.BlockSpec((GDN_REP, 1, d, d), lambda j, n: (j, order(n), 0, 0))
    return qk, vh, st


def gdn_chunk_fwd(q, k, v, gb, bb, name):
    s = q.shape[0]
    nc, d = s // GDN_CHUNK, GDN_HEAD_DIM

    def body(q_ref, k_ref, v_ref, gb_ref, bb_ref, o_ref, st_ref, carry):
        @pl.when(pl.program_id(1) == 0)
        def _():
            carry[...] = jnp.zeros_like(carry)

        for rep in range(GDN_REP):
            cols = slice(rep * d, (rep + 1) * d)
            state = carry[rep]
            st_ref[rep, 0] = state
            out, new_state = _gdn_chunk_fn(q_ref[...], k_ref[...], v_ref[:, cols], gb_ref[:, cols], bb_ref[:, cols],
                                           gb_ref[:, rep * d:rep * d + GDN_CHUNK], state)
            o_ref[:, cols] = out
            carry[rep] = new_state

    qk, vh, st = _gdn_chunk_specs(lambda n: n)
    return pl.pallas_call(
        body, grid=(GDN_QK_HEADS, nc), in_specs=[qk, qk, vh, vh, vh], out_specs=[vh, st],
        out_shape=[jax.ShapeDtypeStruct((s, MIX_WIDTH), F32), jax.ShapeDtypeStruct((GDN_V_HEADS, nc, d, d), F32)],
        scratch_shapes=[pltpu.VMEM((GDN_REP, d, d), F32)],
        compiler_params=_cp(("parallel", "arbitrary")), name=name)(q, k, v, gb, bb)


def gdn_chunk_bwd(q, k, v, gb, bb, states, do, name):
    s = q.shape[0]
    nc, d = s // GDN_CHUNK, GDN_HEAD_DIM
    rev = lambda n: nc - 1 - n

    def body(q_ref, k_ref, v_ref, gb_ref, bb_ref, st_ref, do_ref, dq_ref, dk_ref, dv_ref, dg_ref, db_ref, carry):
        @pl.when(pl.program_id(1) == 0)
        def _():
            carry[...] = jnp.zeros_like(carry)

        d_qs, d_ks = [], []
        for rep in range(GDN_REP):
            cols = slice(rep * d, (rep + 1) * d)
            lead = slice(rep * d, rep * d + GDN_CHUNK)
            args = (q_ref[...], k_ref[...], v_ref[:, cols], gb_ref[:, cols], bb_ref[:, cols], gb_ref[:, lead],
                    st_ref[rep, 0])
            _, vjp = jax.vjp(_gdn_chunk_fn, *args)
            d_q, d_k, d_v, d_gb, d_bb, d_gb64, d_state = vjp((do_ref[:, cols], carry[rep]))
            carry[rep] = d_state
            dv_ref[:, cols] = d_v
            db_ref[:, cols] = d_bb
            dg_ref[:, cols] = d_gb
            dg_ref[:, lead] += d_gb64
            d_qs.append(d_q)
            d_ks.append(d_k)
        dq_ref[...] = sum(d_qs[1:], d_qs[0])
        dk_ref[...] = sum(d_ks[1:], d_ks[0])

    qk, vh, st = _gdn_chunk_specs(rev)
    qk_s, wide_s = jax.ShapeDtypeStruct((s, GDN_QK_WIDTH), F32), jax.ShapeDtypeStruct((s, MIX_WIDTH), F32)
    return pl.pallas_call(
        body, grid=(GDN_QK_HEADS, nc), in_specs=[qk, qk, vh, vh, vh, st, vh], out_specs=[qk, qk, vh, vh, vh],
        out_shape=[qk_s, qk_s, wide_s, wide_s, wide_s], scratch_shapes=[pltpu.VMEM((GDN_REP, d, d), F32)],
        compiler_params=_cp(("parallel", "arbitrary")), name=name)(q, k, v, gb, bb, states, do)


WEIGHTS = ['attn_norm', 'mem_norm', 'w_mem_kv', 'w_out', 'ffn_norm', 'w_ffn_up', 'ffn_conv', 'w_ffn_down', 'final_norm',
           'a_w_in', 'a_sinks', 'b_w_in', 'b_mu', 'b_w0', 'b_w_decay_up', 'b_a0', 'b_w_iclr_up', 'b_w_gate_up', 'b_k_k',
           'b_k_a', 'b_r_k', 'b_gn_g', 'b_gn_b', 'c_w_in', 'c_conv', 'c_a_log', 'c_dt_bias', 'c_norm_g']
INPUTS = ['x', 'mem'] + WEIGHTS + ['loss_target'] + ['m_' + n for n in WEIGHTS] + ['v_' + n for n in WEIGHTS]
REPLICATED = ['attn_norm', 'mem_norm', 'ffn_norm', 'final_norm', 'a_sinks', 'b_mu', 'b_w0', 'b_a0', 'b_k_k', 'b_k_a',
              'b_r_k', 'b_gn_g', 'b_gn_b', 'c_a_log', 'c_dt_bias', 'c_norm_g']
C_MIX = GDN_CONV_WIDTH + MIX_WIDTH


def _cols_to_shards(full):
    rows, cols = full.shape
    return full.reshape(rows, N_DEV, cols // N_DEV).transpose(1, 0, 2)


def _shards_to_cols(g):
    return g.transpose(1, 0, 2).reshape(g.shape[1], N_DEV * g.shape[2])


def _pad_to(x, n, axis):
    pad = [(0, 0)] * x.ndim
    pad[axis] = (0, n - x.shape[axis])
    return jnp.pad(x, pad)


def _b_pad_cols(w):
    parts = [w[..., :4608], _pad_to(w[..., 4608:4704], LANES, -1), _pad_to(w[..., 4704:4800], LANES, -1), w[..., 4800:5056]]
    if w.shape[-1] > 5056:
        parts.append(w[..., 5056:])
    return jnp.concatenate(parts, axis=-1)


def _b_unpad_cols(w):
    parts = [w[..., :4608], w[..., RW_WD:RW_WD + RWKV_DECAY_RANK], w[..., RW_AD:RW_AD + RWKV_ICLR_RANK], w[..., RW_GD:RW_SHIFT]]
    if w.shape[-1] > RW_SHIFT:
        parts.append(w[..., RW_SHIFT:])
    return jnp.concatenate(parts, axis=-1)


def _c_pad_cols(w):
    return jnp.concatenate([w[..., :C_MIX], w[..., C_MIX + 24:], _pad_to(w[..., C_MIX:C_MIX + 12], LANES, -1),
                            _pad_to(w[..., C_MIX + 12:C_MIX + 24], LANES, -1)], axis=-1)


def _c_unpad_cols(w):
    return jnp.concatenate([w[..., :C_MIX], w[..., GD_BT:GD_BT + GDN_V_HEADS], w[..., GD_AT:GD_AT + GDN_V_HEADS],
                            w[..., GD_QMEM:GD_BT]], axis=-1)


def _pack(arrays):
    flat = jnp.concatenate([a.reshape(-1).astype(F32) for a in arrays])
    unit = SUBLANES * LANES
    return _pad_to(flat, -(-flat.size // unit) * unit, 0).reshape(-1, LANES)


def _unpack(packed, shapes):
    flat, out, at = packed.reshape(-1), [], 0
    for shp in shapes:
        n = int(np.prod(shp))
        out.append(flat[at:at + n].reshape(shp))
        at += n
    return out


def kernel(*args):
    a = dict(zip(INPUTS, args))
    x0, mem, target = a['x'][0], a['mem'][0], a['loss_target'][0]
    s = x0.shape[0]
    e64, e64t = _head_matrix(MIX_WIDTH, RWKV_HEAD_DIM)
    e6, e6t = _head_matrix(GDN_QK_WIDTH, GDN_HEAD_DIM)
    e12, e12t = _head_matrix(MIX_WIDTH, GDN_HEAD_DIM)
    trep = jnp.asarray((np.arange(LANES)[:, None] == np.arange(MIX_WIDTH)[None, :] % LANES).astype(np.float32))
    row = lambda v: v.reshape(1, -1)

    def in_proj_shard(l):
        kind, j = l % 3, l // 3
        return (a['a_w_in'], a['b_w_in'], a['c_w_in'])[kind][j]

    def small_shards(l):
        kind, j = l % 3, l // 3
        if kind == 1:
            return [a['b_w_decay_up'][j], a['b_w_iclr_up'][j], a['b_w_gate_up'][j]]
        if kind == 2:
            return [a['c_conv'][j]]
        return []

    layers = []
    for l in range(DEPTH):
        kind = l % 3
        big = [a['w_mem_kv'][l], a['w_out'][l], a['w_ffn_up'][l], a['w_ffn_down'][l], in_proj_shard(l)]
        small = [a['ffn_conv'][l]] + small_shards(l)
        g = all_gather_many([w.astype(BF16) for w in big] + small, f"gather_weights_{l}")
        w_in = _shards_to_cols(g[4])
        lw = dict(w_kv=g[0].reshape(D_MODEL, 2 * MEM_WIDTH), w_out=g[1].reshape(D_MODEL, D_MODEL),
                  w_up=_shards_to_cols(g[2]), w_down=g[3].reshape(D_FF, D_MODEL), conv=_shards_to_cols(g[5]))
        if kind == 0:
            lw['w_in'] = w_in
        elif kind == 1:
            lw['w_in'] = _b_pad_cols(w_in)
            lw['wdu'] = _pad_to(_shards_to_cols(g[6]), LANES, 0)
            lw['wiu'] = _pad_to(_shards_to_cols(g[7]), LANES, 0)
            lw['wgu'] = _shards_to_cols(g[8])
        else:
            lw['w_in'] = _c_pad_cols(w_in)
            lw['c_conv'] = _shards_to_cols(g[6])
        layers.append(lw)

    def rwkv_params(j, lw):
        return (row(_b_pad_cols(a['b_mu'][j])), row(a['b_w0'][j]), lw['wdu'], row(a['b_a0'][j]), lw['wiu'], lw['wgu'],
                row(a['b_k_k'][j]), row(a['b_k_a'][j]), e64, e64t)

    def rwkv_post_params(j):
        return (row(a['b_gn_g'][j]), row(a['b_gn_b'][j]), row(a['b_r_k'][j]), e64, e64t)

    def gdn_params(j, lw):
        return (lw['c_conv'], _pad_lanes(a['c_a_log'][j]), _pad_lanes(a['c_dt_bias'][j]), e6, e6t, e12t)

    def gdn_post_params(j):
        return (jnp.tile(row(a['c_norm_g'][j]), (SUBLANES, 1)), e12, e12t, trep)

    x = x0
    saved = []
    for l, lw in enumerate(layers):
        kind, j = l % 3, l // 3
        sv = dict(x=x)
        h = rmsnorm_fwd(x, a['attn_norm'][l], BF16, f"attn_norm_{l}")
        memn = rmsnorm_fwd(mem, a['mem_norm'][l], BF16, f"mem_norm_{l}")
        mem_kv = mm(memn, lw['w_kv'], name=f"mem_kv_{l}")
        p = mm(h, lw['w_in'], name=f"in_proj_{l}")
        if kind == 0:
            y = swa_fwd(p, a['a_sinks'][j], f"swa_{l}")
            q_col = MIX_WIDTH + 2 * SWA_KV_HEADS * SWA_HEAD_DIM
        elif kind == 1:
            pre = rwkv_pre_fwd(p, rwkv_params(j, lw), f"rwkv_pre_{l}")
            yscan, ck = rwkv_scan_fwd(*pre[:6], f"rwkv_scan_{l}")
            post_in = (yscan, pre[0], pre[2], pre[3], pre[6])
            y = rwkv_post_fwd(post_in, rwkv_post_params(j), f"rwkv_post_{l}")
            sv.update(pre=pre, ck=ck, post_in=post_in)
            q_col = RW_SHIFT
        else:
            pre = gdn_pre_fwd(p, gdn_params(j, lw), f"gdn_pre_{l}")
            o, states = gdn_chunk_fwd(*pre, f"gdn_chunk_{l}")
            y = gdn_post_fwd(o, p, gdn_post_params(j), f"gdn_post_{l}")
            sv.update(pre=pre, o=o, states=states)
            q_col = GD_QMEM
        y_mem = mem_attn_fwd(p, q_col, mem_kv, f"mem_attn_{l}")
        cat = jnp.concatenate([y, y_mem], axis=1)
        x1 = mm(cat, lw['w_out'], res=x, name=f"out_proj_{l}")
        hf = rmsnorm_fwd(x1, a['ffn_norm'][l], BF16, f"ffn_norm_{l}")
        u0 = mm(hf, lw['w_up'], name=f"ffn_up_{l}")
        act = ffn_act_fwd(u0, lw['conv'], f"ffn_act_{l}")
        x = mm(act, lw['w_down'], res=x1, name=f"ffn_down_{l}")
        sv.update(h=h, memn=memn, mem_kv=mem_kv, p=p, q_col=q_col, cat=cat, x1=x1, hf=hf, u0=u0, act=act)
        saved.append(sv)

    loss_part, dx, d_final_norm = final_loss(x, a['final_norm'], target, "final_loss")

    rep_grads = {n: [None] * a[n].shape[0] for n in ('attn_norm', 'mem_norm', 'ffn_norm', 'a_sinks')}
    rep_grads['final_norm'] = d_final_norm
    results = {}

    def apply_adam(name, idx, pieces, tag):
        w, m, v = a[name][idx], a['m_' + name][idx], a['v_' + name][idx]
        shp = w.shape
        two_d = (-1, shp[-1])
        out = adamw_sum(pieces.reshape((N_DEV,) + w.reshape(two_d).shape), w.reshape(two_d), m.reshape(two_d),
                        v.reshape(two_d), f"adamw_{name}_{tag}")
        results.setdefault(name, {})[idx] = [o.reshape(shp) for o in out]

    for l in reversed(range(DEPTH)):
        kind, j = l % 3, l // 3
        lw, sv = layers[l], saved[l]
        p, q_col = sv['p'], sv['q_col']
        d_act = mm(dx, lw['w_down'], tb=True, name=f"d_ffn_act_{l}")
        dw_down = mm(sv['act'], dx, ta=True, out_dtype=BF16, name=f"dw_ffn_down_{l}")
        dug, duv, dcg, dcv = ffn_act_bwd(sv['u0'], lw['conv'], d_act, f"ffn_act_bwd_{l}")
        du0 = jnp.concatenate([dug, duv], axis=1)
        d_conv = jnp.concatenate([dcg, dcv], axis=1)
        d_hf = mm(du0, lw['w_up'], tb=True, name=f"d_ffn_norm_out_{l}")
        dw_up = mm(sv['hf'], du0, ta=True, out_dtype=BF16, name=f"dw_ffn_up_{l}")
        dx1, rep_grads['ffn_norm'][l] = rmsnorm_bwd(sv['x1'], a['ffn_norm'][l], d_hf, dx, f"ffn_norm_bwd_{l}")
        dcat = mm(dx1, lw['w_out'], tb=True, name=f"d_cat_{l}")
        dw_out = mm(sv['cat'], dx1, ta=True, out_dtype=BF16, name=f"dw_out_{l}")
        dq_mem, d_mem_kv = mem_attn_bwd(p, q_col, sv['mem_kv'], dcat, f"mem_attn_bwd_{l}")
        small_grads = []
        if kind == 0:
            dq, dk, dv, rep_grads['a_sinks'][j] = swa_bwd(p, a['a_sinks'][j], dcat, f"swa_bwd_{l}")
            dp = jnp.concatenate([dq, dk, dv, dq_mem], axis=1)
        elif kind == 1:
            post = rwkv_post_bwd(sv['post_in'], rwkv_post_params(j), dcat, f"rwkv_post_bwd_{l}")
            scan = rwkv_scan_bwd(*sv['pre'][:6], sv['ck'], post[0], f"rwkv_scan_bwd_{l}")
            res = rwkv_pre_bwd(p, rwkv_params(j, lw), tuple(scan) + tuple(post[1:5]), f"rwkv_pre_bwd_{l}")
            dp_mix = shift_add(res[0], [res[1]], [1], BF16, f"rwkv_shift_bwd_{l}")
            dp = jnp.concatenate([dp_mix, dq_mem], axis=1)
            for n, val in zip(('b_mu', 'b_w0', 'b_a0', 'b_k_k', 'b_k_a'), (_b_unpad_cols(res[2]), res[3], res[5], res[8], res[9])):
                rep_grads[n] = val
            rep_grads.update(b_gn_g=post[5], b_gn_b=post[6], b_r_k=post[7])
            small_grads = [_cols_to_shards(res[4][:RWKV_DECAY_RANK]), _cols_to_shards(res[6][:RWKV_ICLR_RANK]),
                           _cols_to_shards(res[7])]
        else:
            d_o, dz, rep_grads['c_norm_g'] = gdn_post_bwd(sv['o'], p, gdn_post_params(j), dcat, f"gdn_post_bwd_{l}")
            chunk = gdn_chunk_bwd(*sv['pre'], sv['states'], d_o, f"gdn_chunk_bwd_{l}")
            res = gdn_pre_bwd(p, gdn_params(j, lw), chunk, f"gdn_pre_bwd_{l}")
            dqkv = shift_add(res[0], list(res[1:4]), [1, 2, 3], BF16, f"gdn_shift_bwd_{l}")
            dp = jnp.concatenate([dqkv, dz, dq_mem, res[4]], axis=1)
            rep_grads.update(c_a_log=res[6][:, :GDN_V_HEADS], c_dt_bias=res[7][:, :GDN_V_HEADS])
            small_grads = [_cols_to_shards(res[5])]
        d_h = mm(dp, lw['w_in'], tb=True, name=f"d_attn_norm_out_{l}")
        dw_in = mm(sv['h'], dp, ta=True, out_dtype=BF16, name=f"dw_in_{l}")
        dx, rep_grads['attn_norm'][l] = rmsnorm_bwd(sv['x'], a['attn_norm'][l], d_h, dx1, f"attn_norm_bwd_{l}")
        d_memn = mm(d_mem_kv, lw['w_kv'], tb=True, name=f"d_mem_norm_out_{l}")
        dw_kv = mm(sv['memn'], d_mem_kv, ta=True, out_dtype=BF16, name=f"dw_mem_kv_{l}")
        _, rep_grads['mem_norm'][l] = rmsnorm_bwd(mem, a['mem_norm'][l], d_memn, None, f"mem_norm_bwd_{l}")

        if kind == 1:
            dw_in = _b_unpad_cols(dw_in)
        elif kind == 2:
            dw_in = _c_unpad_cols(dw_in)
        pieces = [dw_kv.reshape(N_DEV, -1, 2 * MEM_WIDTH), dw_out.reshape(N_DEV, -1, D_MODEL), _cols_to_shards(dw_up),
                  dw_down.reshape(N_DEV, -1, D_MODEL), _cols_to_shards(dw_in), _cols_to_shards(d_conv)] + small_grads
        got = all_to_all_many(pieces, f"exchange_grads_{l}")
        in_name = ('a_w_in', 'b_w_in', 'c_w_in')[kind]
        for name, idx, pc in (('w_mem_kv', l, got[0]), ('w_out', l, got[1]), ('w_ffn_up', l, got[2]),
                              ('w_ffn_down', l, got[3]), (in_name, j, got[4]), ('ffn_conv', l, got[5])):
            apply_adam(name, idx, pc, l)
        if kind == 1:
            for name, pc in zip(('b_w_decay_up', 'b_w_iclr_up', 'b_w_gate_up'), got[6:]):
                apply_adam(name, j, pc, l)
        elif kind == 2:
            apply_adam('c_conv', j, got[6], l)

    rep_vals = []
    for n in REPLICATED:
        gval = rep_grads[n]
        gval = jnp.stack(gval) if isinstance(gval, list) else gval
        rep_vals.append(gval.reshape(a[n].shape))
    shapes = [a[n].shape for n in REPLICATED] + [(1,)]
    part = _pack(rep_vals + [loss_part.reshape(1)])
    gathered = all_gather_many([part], "gather_small_grads")[0]
    zero = jnp.zeros((1,), F32)
    packed = lambda pre: _pack([a[pre + n] for n in REPLICATED] + [zero])
    rep_out = adamw_sum(gathered, packed(''), packed('m_'), packed('v_'), "adamw_replicated")
    rep_out = [_unpack(o, shapes) for o in rep_out]
    loss = rep_out[0][-1][0]
    for i, n in enumerate(REPLICATED):
        results[n] = [o[i] for o in rep_out]

    def leaf(name, which):
        r = results[name]
        if isinstance(r, dict):
            return jnp.stack([r[i][which] for i in range(len(r))])
        return r[which]

    outs = [loss, dx[None]]
    for which in range(4):
        outs += [leaf(n, which) for n in WEIGHTS]
    return tuple(outs)
```

```python
import functools

import numpy as np
import jax
import jax.numpy as jnp
from jax import lax
from jax.experimental import pallas as pl
from jax.experimental.pallas import tpu as pltpu
from jax.experimental.pallas import tpu_sc as plsc

F32, BF16 = jnp.float32, jnp.bfloat16
HI = lax.Precision.HIGHEST
V7X_VMEM_BYTES = 64 * 1024 * 1024
VMEM_LIMIT = V7X_VMEM_BYTES - 8 * 1024 * 1024
SUBLANES, LANES = 8, 128
N_DEV = 8

D_MODEL = 2048
DEPTH = 4
MIX_WIDTH = 1536
MEM_HEADS, MEM_HEAD_DIM, MEM_WIDTH = 4, 128, 512
NORM_EPS = 1e-6
SWA_HEAD_DIM, SWA_Q_HEADS, SWA_KV_HEADS, SWA_GROUP, SWA_BLOCK = 64, 24, 4, 6, 128
RWKV_HEADS, RWKV_HEAD_DIM, RWKV_GN_EPS = 24, 64, 64e-5
RWKV_DECAY_RANK, RWKV_ICLR_RANK, RWKV_GATE_RANK = 96, 96, 256
GDN_HEAD_DIM, GDN_V_HEADS, GDN_QK_HEADS, GDN_CONV, GDN_CHUNK = 128, 12, 6, 4, 64
GDN_QK_WIDTH = GDN_QK_HEADS * GDN_HEAD_DIM
GDN_CONV_WIDTH = 2 * GDN_QK_WIDTH + MIX_WIDTH
D_FF, FFN_CONV = 5632, 3
ADAM_LR, ADAM_B1, ADAM_B2, ADAM_EPS, ADAM_WD, ADAM_STEP = 0.001, 0.9, 0.999, 1e-08, 0.01, 10
MESH = pl.DeviceIdType.MESH


def _cp(sem=None):
    return pltpu.CompilerParams(dimension_semantics=sem, vmem_limit_bytes=VMEM_LIMIT)


def _tile(n, cands):
    for c in cands:
        if n % c == 0:
            return c
    return n


def _dot(a, b):
    return jnp.dot(a, b, precision=HI, preferred_element_type=F32)


def _dot_nt(a, b):
    return lax.dot_general(a, b, (((1,), (1,)), ((), ())), precision=HI, preferred_element_type=F32)


def _dot_tn(a, b):
    return lax.dot_general(a, b, (((0,), (0,)), ((), ())), precision=HI, preferred_element_type=F32)


def _sigmoid(x):
    return 1.0 / (1.0 + jnp.exp(-x))


def _softplus(x):
    return jnp.maximum(x, 0.0) + jnp.log(1.0 + jnp.exp(-jnp.abs(x)))


def _silu(x):
    return x * _sigmoid(x)


def mm(a, b, *, ta=False, tb=False, res=None, out_dtype=F32, name):
    (k_a, m) = a.shape if ta else a.shape[::-1]
    (k_b, n) = b.shape[::-1] if tb else b.shape
    assert k_a == k_b, (a.shape, b.shape, ta, tb)
    kdim = k_a
    tm = _tile(m, (1024, 512, 256))
    tn = _tile(n, (1024, 768, 512, 384, 256, 128))
    tk = _tile(kdim, (512, 256, 128))
    nk = kdim // tk
    dims = (((0 if ta else 1,), (1 if tb else 0,)), ((), ()))

    def body(*refs):
        if res is None:
            a_ref, b_ref, o_ref, acc = refs
        else:
            a_ref, b_ref, r_ref, o_ref, acc = refs
        kk = pl.program_id(2)

        @pl.when(kk == 0)
        def _():
            acc[...] = jnp.zeros_like(acc)

        acc[...] += lax.dot_general(a_ref[...].astype(BF16), b_ref[...].astype(BF16), dims,
                                    preferred_element_type=F32)

        @pl.when(kk == nk - 1)
        def _():
            out = acc[...] if res is None else acc[...] + r_ref[...]
            o_ref[...] = out.astype(o_ref.dtype)

    a_spec = pl.BlockSpec((tk, tm), lambda i, j, k: (k, i)) if ta else pl.BlockSpec((tm, tk), lambda i, j, k: (i, k))
    b_spec = pl.BlockSpec((tn, tk), lambda i, j, k: (j, k)) if tb else pl.BlockSpec((tk, tn), lambda i, j, k: (k, j))
    o_spec = pl.BlockSpec((tm, tn), lambda i, j, k: (i, j))
    in_specs, args = [a_spec, b_spec], [a, b]
    if res is not None:
        in_specs.append(o_spec)
        args.append(res)
    return pl.pallas_call(
        body, grid=(m // tm, n // tn, nk), in_specs=in_specs, out_specs=o_spec,
        out_shape=jax.ShapeDtypeStruct((m, n), out_dtype), scratch_shapes=[pltpu.VMEM((tm, tn), F32)],
        compiler_params=_cp(("parallel", "parallel", "arbitrary")), name=name)(*args)


def _coords():
    return lax.axis_index("x"), lax.axis_index("y"), lax.axis_index("c")


def _block_index(p):
    return 4 * p[0] + 2 * p[1] + p[2]


def _all_gather_body(x_refs, o_refs, send, recv, loc):
    n = len(x_refs)
    x, y, c = _coords()
    me, sib = (x, y, c), (x, y, 1 - c)
    chips = [(1 - x, y), (x, 1 - y), (1 - x, 1 - y)]

    def cp(i, k, block, to, src=None):
        dst = o_refs[i].at[_block_index(block)]
        return pltpu.make_async_remote_copy(
            src_ref=dst if src is None else src, dst_ref=dst, send_sem=send.at[i, k], recv_sem=recv.at[i, k],
            device_id=to, device_id_type=MESH)

    mine = [pltpu.make_async_copy(x_refs[i], o_refs[i].at[_block_index(me)], loc.at[i]) for i in range(n)]
    for m_ in mine:
        m_.start()
    first = []
    for i in range(n):
        first.append(cp(i, 0, me, sib, src=x_refs[i]))
        for j, chip in enumerate(chips):
            first.append(cp(i, 1 + j, me, (*chip, c), src=x_refs[i]))
    for f in first:
        f.start()
    passed = []
    for j, chip in enumerate(chips):
        for i in range(n):
            cp(i, 1 + j, (*chip, c), me).wait_recv()
            p = cp(i, 4 + j, (*chip, c), sib)
            p.start()
            passed.append(p)
    for i in range(n):
        cp(i, 0, sib, me).wait_recv()
        for j, chip in enumerate(chips):
            cp(i, 4 + j, (*chip, 1 - c), me).wait_recv()
    for f in first + passed:
        f.wait_send()
    for m_ in mine:
        m_.wait()


def _all_gather_peers():
    x, y, c = _coords()
    return [(x, y, 1 - c), (1 - x, y, c), (x, 1 - y, c), (1 - x, 1 - y, c)]


def _all_to_all_peers():
    x, y, c = _coords()
    return [(1 - x if r & 4 else x, 1 - y if r & 2 else y, 1 - c if r & 1 else c) for r in range(1, N_DEV)]


def _all_to_all_body(x_refs, o_refs, send, recv, loc):
    n = len(x_refs)
    x, y, c = _coords()
    me = _block_index((x, y, c))
    mine = [pltpu.make_async_copy(x_refs[i].at[me], o_refs[i].at[me], loc.at[i]) for i in range(n)]
    for m_ in mine:
        m_.start()
    copies = []
    for r, peer in enumerate(_all_to_all_peers()):
        pidx = _block_index(peer)
        for i in range(n):
            copies.append((
                pltpu.make_async_remote_copy(
                    src_ref=x_refs[i].at[pidx], dst_ref=o_refs[i].at[me], send_sem=send.at[i, r],
                    recv_sem=recv.at[i, r], device_id=peer, device_id_type=MESH),
                pltpu.make_async_remote_copy(
                    src_ref=x_refs[i].at[pidx], dst_ref=o_refs[i].at[pidx], send_sem=send.at[i, r],
                    recv_sem=recv.at[i, r], device_id=peer, device_id_type=MESH)))
    for s, _ in copies:
        s.start()
    for s, w in copies:
        w.wait_recv()
        s.wait_send()
    for m_ in mine:
        m_.wait()


def _comm_scratch(n):
    return [pltpu.SemaphoreType.DMA((n, 7)), pltpu.SemaphoreType.DMA((n, 7)), pltpu.SemaphoreType.DMA((n,))]


def all_gather_many(xs, name):
    n = len(xs)

    def body(*refs):
        _all_gather_body(refs[:n], refs[n:2 * n], *refs[2 * n:])

    any_spec = pl.BlockSpec(memory_space=pl.ANY)
    return pl.pallas_call(
        body, in_specs=[any_spec] * n, out_specs=[any_spec] * n,
        out_shape=[jax.ShapeDtypeStruct((N_DEV,) + x.shape, x.dtype) for x in xs],
        scratch_shapes=_comm_scratch(n), name=name)(*xs)


def _on_sequencer(exchange, peers, xs, out_shapes, name, collective_id):
    x_refs = [jax.new_ref(x, memory_space=pltpu.MemorySpace.HBM) for x in xs]
    o_refs = [jax.empty_ref(s, memory_space=pltpu.MemorySpace.HBM) for s in out_shapes]

    @pl.kernel(mesh=plsc.ScalarSubcoreMesh(axis_name="sequencer", num_cores=1), name=name,
               scratch_types=tuple(_comm_scratch(len(xs))),
               compiler_params=pltpu.CompilerParams(collective_id=collective_id))
    def launch(send, recv, loc):
        barrier = pltpu.get_barrier_semaphore()
        ids = peers()
        for peer in ids:
            pl.semaphore_signal(barrier, inc=1, device_id=peer, device_id_type=MESH)
        pl.semaphore_wait(barrier, len(ids))
        exchange(x_refs, o_refs, send, recv, loc)

    launch()
    return [o[...] for o in o_refs]


def all_gather_many_async(xs, name, collective_id):
    shapes = [jax.ShapeDtypeStruct((N_DEV,) + x.shape, x.dtype) for x in xs]
    return _on_sequencer(_all_gather_body, _all_gather_peers, xs, shapes, name, collective_id)


def all_to_all_many_async(xs, name, collective_id):
    shapes = [jax.ShapeDtypeStruct(x.shape, x.dtype) for x in xs]
    return _on_sequencer(_all_to_all_body, _all_to_all_peers, xs, shapes, name, collective_id)


def adamw_sum(pieces, w, m, v, name):
    rows, cols = w.shape
    tr = _tile(rows, (128, 64, 32, 16, 8))
    c1 = 1.0 - ADAM_B1 ** ADAM_STEP
    c2 = 1.0 - ADAM_B2 ** ADAM_STEP

    def body(p_ref, w_ref, m_ref, v_ref, g_out, d_out, m_out, v_out):
        g = p_ref[0].astype(F32)
        for s in range(1, N_DEV):
            g = g + p_ref[s].astype(F32)
        m_new = ADAM_B1 * m_ref[...] + (1.0 - ADAM_B1) * g
        v_new = ADAM_B2 * v_ref[...] + (1.0 - ADAM_B2) * (g * g)
        m_hat = m_new / c1
        v_hat = v_new / c2
        g_out[...] = g
        d_out[...] = -ADAM_LR * (m_hat / (jnp.sqrt(v_hat) + ADAM_EPS) + ADAM_WD * w_ref[...])
        m_out[...] = m_new
        v_out[...] = v_new

    spec = pl.BlockSpec((tr, cols), lambda i: (i, 0))
    out = jax.ShapeDtypeStruct((rows, cols), F32)
    return pl.pallas_call(
        body, grid=(rows // tr,), in_specs=[pl.BlockSpec((N_DEV, tr, cols), lambda i: (0, i, 0)), spec, spec, spec],
        out_specs=[spec] * 4, out_shape=[out] * 4, compiler_params=_cp(("parallel",)), name=name)(pieces, w, m, v)


def rmsnorm_fwd(x, g, out_dtype, name):
    s, d = x.shape
    tr = _tile(s, (256, 128, 64, 32, 16))

    def body(x_ref, g_ref, o_ref):
        xv = x_ref[...]
        rstd = lax.rsqrt(jnp.mean(xv * xv, axis=-1, keepdims=True) + NORM_EPS)
        o_ref[...] = (xv * rstd * g_ref[...]).astype(o_ref.dtype)

    return pl.pallas_call(
        body, grid=(s // tr,), in_specs=[pl.BlockSpec((tr, d), lambda i: (i, 0)), pl.BlockSpec((1, d), lambda i: (0, 0))],
        out_specs=pl.BlockSpec((tr, d), lambda i: (i, 0)), out_shape=jax.ShapeDtypeStruct((s, d), out_dtype),
        compiler_params=_cp(("parallel",)), name=name)(x, g.reshape(1, d))


def rmsnorm_bwd(x, g, dh, dres, name):
    s, d = x.shape
    tr = _tile(s, (256, 128, 64, 32, 16))

    def body(*refs):
        if dres is None:
            x_ref, g_ref, dh_ref, dx_ref, dg_ref = refs
        else:
            x_ref, g_ref, dh_ref, dr_ref, dx_ref, dg_ref = refs
        xv = x_ref[...]
        rstd = lax.rsqrt(jnp.mean(xv * xv, axis=-1, keepdims=True) + NORM_EPS)
        xhat = xv * rstd
        dhv = dh_ref[...].astype(F32)
        dhg = dhv * g_ref[...]
        dx = rstd * (dhg - xhat * jnp.mean(dhg * xhat, axis=-1, keepdims=True))
        if dres is not None:
            dx = dx + dr_ref[...]
        dx_ref[...] = dx

        @pl.when(pl.program_id(0) == 0)
        def _():
            dg_ref[...] = jnp.zeros_like(dg_ref)

        dg_ref[...] += jnp.sum(dhv * xhat, axis=0, keepdims=True)

    row = pl.BlockSpec((tr, d), lambda i: (i, 0))
    vec = pl.BlockSpec((1, d), lambda i: (0, 0))
    ins = [x, g.reshape(1, d), dh] + ([] if dres is None else [dres])
    dx, dg = pl.pallas_call(
        body, grid=(s // tr,), in_specs=[row, vec, row] + ([] if dres is None else [row]), out_specs=[row, vec],
        out_shape=[jax.ShapeDtypeStruct((s, d), F32), jax.ShapeDtypeStruct((1, d), F32)],
        compiler_params=_cp(("arbitrary",)), name=name)(*ins)
    return dx, dg.reshape(d)


def final_loss(x, g, target, name):
    s, d = x.shape
    tr = _tile(s, (256, 128, 64, 32, 16))

    def body(x_ref, g_ref, t_ref, l_ref, dx_ref, dg_ref):
        xv = x_ref[...]
        rstd = lax.rsqrt(jnp.mean(xv * xv, axis=-1, keepdims=True) + NORM_EPS)
        xhat = xv * rstd
        err = xhat * g_ref[...] - t_ref[...]
        dy = err * (1.0 / d)
        dhg = dy * g_ref[...]
        dx_ref[...] = rstd * (dhg - xhat * jnp.mean(dhg * xhat, axis=-1, keepdims=True))

        @pl.when(pl.program_id(0) == 0)
        def _():
            dg_ref[...] = jnp.zeros_like(dg_ref)
            l_ref[...] = jnp.zeros_like(l_ref)

        dg_ref[...] += jnp.sum(dy * xhat, axis=0, keepdims=True)
        part = 0.5 * jnp.sum(jnp.mean(err * err, axis=-1, keepdims=True), axis=0, keepdims=True)
        l_ref[...] += jnp.broadcast_to(part, l_ref.shape)

    row = pl.BlockSpec((tr, d), lambda i: (i, 0))
    vec = pl.BlockSpec((1, d), lambda i: (0, 0))
    lspec = pl.BlockSpec((1, LANES), lambda i: (0, 0))
    loss, dx, dg = pl.pallas_call(
        body, grid=(s // tr,), in_specs=[row, vec, row], out_specs=[lspec, row, vec],
        out_shape=[jax.ShapeDtypeStruct((1, LANES), F32), jax.ShapeDtypeStruct((s, d), F32),
                   jax.ShapeDtypeStruct((1, d), F32)],
        compiler_params=_cp(("arbitrary",)), name=name)(x, g.reshape(1, d), target)
    return loss[0, 0], dx, dg.reshape(d)


def _shift_down(tile, prev8, j):
    if j == 0:
        return tile
    rt = pltpu.roll(tile, j, 0)
    rp = pltpu.roll(prev8, j, 0)
    rows = lax.broadcasted_iota(jnp.int32, prev8.shape, 0)
    top = jnp.where(rows < j, rp, rt[:SUBLANES])
    return jnp.concatenate([top, rt[SUBLANES:]], axis=0)


def _shift_up(tile, next8, j):
    if j == 0:
        return tile
    t = tile.shape[0]
    rt = pltpu.roll(tile, t - j, 0)
    rn = pltpu.roll(next8, SUBLANES - j, 0)
    rows = lax.broadcasted_iota(jnp.int32, next8.shape, 0)
    bot = jnp.where(rows >= SUBLANES - j, rn, rt[t - SUBLANES:])
    return jnp.concatenate([rt[:t - SUBLANES], bot], axis=0)


def _halo_specs(t_rows, s_rows, cols, col_of):
    per, last = t_rows // SUBLANES, s_rows // SUBLANES - 1
    prev = pl.BlockSpec((SUBLANES, cols), lambda j, i: (jnp.maximum(i * per - 1, 0), col_of(j)))
    nxt = pl.BlockSpec((SUBLANES, cols), lambda j, i: (jnp.minimum((i + 1) * per, last), col_of(j)))
    return prev, nxt


def ffn_act_fwd(u0, conv, name):
    s, two_f = u0.shape
    f = two_f // 2
    t, c = _tile(s, (256, 128, 64)), 512
    nc = f // c

    def body(g_ref, v_ref, gp_ref, vp_ref, wg_ref, wv_ref, a_ref):
        first = (pl.program_id(1) > 0).astype(F32)

        def conv_of(x_ref, p_ref, w_ref):
            x, p = x_ref[...], p_ref[...] * first
            return (w_ref[0:1, :] * _shift_down(x, p, 2) + w_ref[1:2, :] * _shift_down(x, p, 1) + w_ref[2:3, :] * x)

        ug = conv_of(g_ref, gp_ref, wg_ref)
        uv = conv_of(v_ref, vp_ref, wv_ref)
        a_ref[...] = (_silu(ug) * uv).astype(a_ref.dtype)

    gate = pl.BlockSpec((t, c), lambda j, i: (i, j))
    val = pl.BlockSpec((t, c), lambda j, i: (i, j + nc))
    gp, _ = _halo_specs(t, s, c, lambda j: j)
    vp, _ = _halo_specs(t, s, c, lambda j: j + nc)
    wg = pl.BlockSpec((FFN_CONV, c), lambda j, i: (0, j))
    wv = pl.BlockSpec((FFN_CONV, c), lambda j, i: (0, j + nc))
    return pl.pallas_call(
        body, grid=(nc, s // t), in_specs=[gate, val, gp, vp, wg, wv], out_specs=pl.BlockSpec((t, c), lambda j, i: (i, j)),
        out_shape=jax.ShapeDtypeStruct((s, f), BF16), compiler_params=_cp(("parallel", "parallel")),
        name=name)(u0, u0, u0, u0, conv, conv)


def ffn_act_bwd(u0, conv, da, name):
    s, two_f = u0.shape
    f = two_f // 2
    t, c = _tile(s, (256, 128, 64)), 512
    nc, nt = f // c, s // t

    def body(g_ref, v_ref, gp_ref, vp_ref, gn_ref, vn_ref, wg_ref, wv_ref, da_ref, dan_ref,
             dg_ref, dv_ref, dwg_ref, dwv_ref):
        i = pl.program_id(1)
        first, last = (i > 0).astype(F32), (i < nt - 1).astype(F32)
        zeros8 = jnp.zeros((SUBLANES, c), F32)

        def ext(x_ref, p_ref, n_ref):
            return jnp.concatenate([p_ref[...] * first, x_ref[...], n_ref[...] * last], axis=0)

        def taps(e):
            return pltpu.roll(e, 2, 0), pltpu.roll(e, 1, 0), e

        def conv_of(sh, w_ref):
            return w_ref[0:1, :] * sh[0] + w_ref[1:2, :] * sh[1] + w_ref[2:3, :] * sh[2]

        def conv_t(du, w_ref):
            n = du.shape[0]
            return w_ref[2:3, :] * du + w_ref[1:2, :] * pltpu.roll(du, n - 1, 0) + w_ref[0:1, :] * pltpu.roll(du, n - 2, 0)

        sg, sv = taps(ext(g_ref, gp_ref, gn_ref)), taps(ext(v_ref, vp_ref, vn_ref))
        ug, uv = conv_of(sg, wg_ref), conv_of(sv, wv_ref)
        dae = jnp.concatenate([zeros8, da_ref[...], dan_ref[...] * last], axis=0)
        sig = _sigmoid(ug)
        dug = dae * uv * (sig * (1.0 + ug * (1.0 - sig)))
        duv = dae * (ug * sig)
        dg_ref[...] = conv_t(dug, wg_ref)[SUBLANES:t + SUBLANES].astype(dg_ref.dtype)
        dv_ref[...] = conv_t(duv, wv_ref)[SUBLANES:t + SUBLANES].astype(dv_ref.dtype)

        @pl.when(i == 0)
        def _():
            dwg_ref[...] = jnp.zeros_like(dwg_ref)
            dwv_ref[...] = jnp.zeros_like(dwv_ref)

        def dconv(du, sh):
            d = du[SUBLANES:t + SUBLANES]
            return jnp.concatenate([jnp.sum(d * x[SUBLANES:t + SUBLANES], axis=0, keepdims=True) for x in sh], axis=0)

        dwg_ref[...] += dconv(dug, sg)
        dwv_ref[...] += dconv(duv, sv)

    gate = pl.BlockSpec((t, c), lambda j, i: (i, j))
    val = pl.BlockSpec((t, c), lambda j, i: (i, j + nc))
    gp, gn = _halo_specs(t, s, c, lambda j: j)
    vp, vn = _halo_specs(t, s, c, lambda j: j + nc)
    wg = pl.BlockSpec((FFN_CONV, c), lambda j, i: (0, j))
    wv = pl.BlockSpec((FFN_CONV, c), lambda j, i: (0, j + nc))
    wout = pl.BlockSpec((FFN_CONV, c), lambda j, i: (0, j))
    half = jax.ShapeDtypeStruct((s, f), BF16)
    dwh = jax.ShapeDtypeStruct((FFN_CONV, f), F32)
    return pl.pallas_call(
        body, grid=(nc, nt), in_specs=[gate, val, gp, vp, gn, vn, wg, wv, gate, gn],
        out_specs=[gate, gate, wout, wout], out_shape=[half, half, dwh, dwh],
        compiler_params=_cp(("parallel", "arbitrary")), name=name)(u0, u0, u0, u0, u0, u0, conv, conv, da, da)


def _softmax_rows(s, extra=None):
    m = jnp.max(s, axis=-1, keepdims=True)
    if extra is not None:
        m = jnp.maximum(m, extra)
    m = lax.stop_gradient(m)
    e = jnp.exp(s - m)
    den = jnp.sum(e, axis=-1, keepdims=True)
    if extra is not None:
        den = den + jnp.exp(extra - m)
    return e / den


def _mem_attn_fn(qs, ks, vs):
    outs = []
    for q, k, v in zip(qs, ks, vs):
        p = _softmax_rows(_dot_nt(q, k) * (MEM_HEAD_DIM ** -0.5))
        outs.append(_dot(p, v))
    return outs


def _mem_heads(q_ref, kv_ref):
    d = MEM_HEAD_DIM
    qs = [q_ref[:, h * d:(h + 1) * d] for h in range(MEM_HEADS)]
    ks = [kv_ref[:, h * d:(h + 1) * d] for h in range(MEM_HEADS)]
    vs = [kv_ref[:, MEM_WIDTH + h * d:MEM_WIDTH + (h + 1) * d] for h in range(MEM_HEADS)]
    return qs, ks, vs


def mem_attn_fwd(p, q_col, kv, name):
    s = p.shape[0]
    t = _tile(s, (256, 128))
    m = kv.shape[0]

    def body(q_ref, kv_ref, o_ref):
        outs = _mem_attn_fn(*_mem_heads(q_ref, kv_ref))
        o_ref[...] = jnp.concatenate(outs, axis=1).astype(o_ref.dtype)

    return pl.pallas_call(
        body, grid=(s // t,),
        in_specs=[pl.BlockSpec((t, MEM_WIDTH), lambda i: (i, q_col // MEM_WIDTH)),
                  pl.BlockSpec((m, 2 * MEM_WIDTH), lambda i: (0, 0))],
        out_specs=pl.BlockSpec((t, MEM_WIDTH), lambda i: (i, 0)), out_shape=jax.ShapeDtypeStruct((s, MEM_WIDTH), BF16),
        compiler_params=_cp(("parallel",)), name=name)(p, kv)


def mem_attn_bwd(p, q_col, kv, dcat, name):
    s = p.shape[0]
    t = _tile(s, (256, 128))
    m = kv.shape[0]
    d = MEM_HEAD_DIM

    def body(q_ref, kv_ref, dy_ref, dq_ref, dkv_ref):
        qs, ks, vs = _mem_heads(q_ref, kv_ref)
        _, vjp = jax.vjp(_mem_attn_fn, qs, ks, vs)
        dqs, dks, dvs = vjp([dy_ref[:, h * d:(h + 1) * d] for h in range(MEM_HEADS)])
        dq_ref[...] = jnp.concatenate(dqs, axis=1).astype(dq_ref.dtype)

        @pl.when(pl.program_id(0) == 0)
        def _():
            dkv_ref[...] = jnp.zeros_like(dkv_ref)

        dkv_ref[...] += jnp.concatenate(dks + dvs, axis=1)

    return pl.pallas_call(
        body, grid=(s // t,),
        in_specs=[pl.BlockSpec((t, MEM_WIDTH), lambda i: (i, q_col // MEM_WIDTH)),
                  pl.BlockSpec((m, 2 * MEM_WIDTH), lambda i: (0, 0)),
                  pl.BlockSpec((t, MEM_WIDTH), lambda i: (i, MIX_WIDTH // MEM_WIDTH))],
        out_specs=[pl.BlockSpec((t, MEM_WIDTH), lambda i: (i, 0)), pl.BlockSpec((m, 2 * MEM_WIDTH), lambda i: (0, 0))],
        out_shape=[jax.ShapeDtypeStruct((s, MEM_WIDTH), BF16), jax.ShapeDtypeStruct((m, 2 * MEM_WIDTH), F32)],
        compiler_params=_cp(("arbitrary",)), name=name)(p, kv, dcat)


def _swa_fn(qs, kcs, kps, vcs, vps, sinks, not_first):
    t = SWA_BLOCK
    qi = lax.broadcasted_iota(jnp.int32, (t, 2 * t), 0)
    kj = lax.broadcasted_iota(jnp.int32, (t, 2 * t), 1)
    dist = t + qi - kj
    valid = (dist >= 0) & (dist < t) & ((kj >= t) | not_first)
    distf = dist.astype(F32)
    outs = []
    for kh in range(SWA_KV_HEADS):
        kb = jnp.concatenate([kps[kh], kcs[kh]], axis=0)
        vb = jnp.concatenate([vps[kh], vcs[kh]], axis=0)
        for g in range(SWA_GROUP):
            h = kh * SWA_GROUP + g
            slope = 2.0 ** (-8.0 * (h + 1) / SWA_Q_HEADS)
            sc = _dot_nt(qs[h], kb) * (SWA_HEAD_DIM ** -0.5) - slope * distf
            sc = jnp.where(valid, sc, -jnp.inf)
            outs.append(_dot(_softmax_rows(sc, extra=sinks[h]), vb))
    return outs


def _swa_args(q_ref, kc_ref, kp_ref, vc_ref, vp_ref, sink_ref):
    d = SWA_HEAD_DIM
    qs = [q_ref[:, h * d:(h + 1) * d] for h in range(SWA_Q_HEADS)]
    per_kv = lambda ref: [ref[:, h * d:(h + 1) * d] for h in range(SWA_KV_HEADS)]
    sinks = [sink_ref[0:1, h:h + 1] for h in range(SWA_Q_HEADS)]
    return qs, per_kv(kc_ref), per_kv(kp_ref), per_kv(vc_ref), per_kv(vp_ref), sinks


def _swa_specs(nb, order):
    t, kvw = SWA_BLOCK, SWA_KV_HEADS * SWA_HEAD_DIM
    k_col, v_col = MIX_WIDTH // kvw, MIX_WIDTH // kvw + 1
    q = pl.BlockSpec((t, MIX_WIDTH), lambda n: (order(n), 0))
    kc = pl.BlockSpec((t, kvw), lambda n: (order(n), k_col))
    kp = pl.BlockSpec((t, kvw), lambda n: (jnp.maximum(order(n) - 1, 0), k_col))
    vc = pl.BlockSpec((t, kvw), lambda n: (order(n), v_col))
    vp = pl.BlockSpec((t, kvw), lambda n: (jnp.maximum(order(n) - 1, 0), v_col))
    sink = pl.BlockSpec((1, LANES), lambda n: (0, 0))
    return [q, kc, kp, vc, vp, sink]


def _pad_lanes(v):
    return jnp.pad(v.reshape(1, -1), ((0, 0), (0, LANES - v.size)))


def swa_fwd(p, sinks, name):
    s = p.shape[0]
    nb = s // SWA_BLOCK

    def body(q_ref, kc_ref, kp_ref, vc_ref, vp_ref, sink_ref, o_ref):
        outs = _swa_fn(*_swa_args(q_ref, kc_ref, kp_ref, vc_ref, vp_ref, sink_ref), pl.program_id(0) > 0)
        o_ref[...] = jnp.concatenate(outs, axis=1).astype(o_ref.dtype)

    return pl.pallas_call(
        body, grid=(nb,), in_specs=_swa_specs(nb, lambda n: n),
        out_specs=pl.BlockSpec((SWA_BLOCK, MIX_WIDTH), lambda n: (n, 0)),
        out_shape=jax.ShapeDtypeStruct((s, MIX_WIDTH), BF16), compiler_params=_cp(("parallel",)),
        name=name)(p, p, p, p, p, _pad_lanes(sinks))


def swa_bwd(p, sinks, dcat, name):
    s = p.shape[0]
    nb = s // SWA_BLOCK
    t, d, kvw = SWA_BLOCK, SWA_HEAD_DIM, SWA_KV_HEADS * SWA_HEAD_DIM
    rev = lambda n: nb - 1 - n

    def body(q_ref, kc_ref, kp_ref, vc_ref, vp_ref, sink_ref, dy_ref, dq_ref, dk_ref, dv_ref, ds_ref, ck, cv):
        n = pl.program_id(0)

        @pl.when(n == 0)
        def _():
            ck[...] = jnp.zeros_like(ck)
            cv[...] = jnp.zeros_like(cv)
            ds_ref[...] = jnp.zeros_like(ds_ref)

        args = _swa_args(q_ref, kc_ref, kp_ref, vc_ref, vp_ref, sink_ref)
        _, vjp = jax.vjp(functools.partial(_swa_fn, not_first=rev(n) > 0), *args)
        dqs, dkcs, dkps, dvcs, dvps, dsinks = vjp([dy_ref[:, h * d:(h + 1) * d] for h in range(SWA_Q_HEADS)])
        dq_ref[...] = jnp.concatenate(dqs, axis=1).astype(dq_ref.dtype)
        dk_ref[...] = (jnp.concatenate(dkcs, axis=1) + ck[...]).astype(dk_ref.dtype)
        dv_ref[...] = (jnp.concatenate(dvcs, axis=1) + cv[...]).astype(dv_ref.dtype)
        ck[...] = jnp.concatenate(dkps, axis=1)
        cv[...] = jnp.concatenate(dvps, axis=1)
        lane = lax.broadcasted_iota(jnp.int32, (1, LANES), 1)
        acc = jnp.zeros((1, LANES), F32)
        for h in range(SWA_Q_HEADS):
            acc = acc + jnp.where(lane == h, dsinks[h], 0.0)
        ds_ref[...] += acc

    dy = pl.BlockSpec((t, MIX_WIDTH), lambda n: (rev(n), 0))
    kv_out = pl.BlockSpec((t, kvw), lambda n: (rev(n), 0))
    dq, dk, dv, ds = pl.pallas_call(
        body, grid=(nb,), in_specs=_swa_specs(nb, rev) + [dy],
        out_specs=[dy, kv_out, kv_out, pl.BlockSpec((1, LANES), lambda n: (0, 0))],
        out_shape=[jax.ShapeDtypeStruct((s, MIX_WIDTH), BF16), jax.ShapeDtypeStruct((s, kvw), BF16),
                   jax.ShapeDtypeStruct((s, kvw), BF16), jax.ShapeDtypeStruct((1, LANES), F32)],
        scratch_shapes=[pltpu.VMEM((t, kvw), F32), pltpu.VMEM((t, kvw), F32)],
        compiler_params=_cp(("arbitrary",)), name=name)(p, p, p, p, p, _pad_lanes(sinks), dcat)
    return dq, dk, dv, ds[0, :SWA_Q_HEADS]


RW_SHIFT = 5120
RW_R, RW_K, RW_V, RW_WD, RW_AD, RW_GD = 0, 1536, 3072, 4608, 4736, 4864


def _head_matrix(width, head_dim):
    e = (np.arange(width)[:, None] // head_dim == np.arange(LANES)[None, :]).astype(np.float32)
    return jnp.asarray(e), jnp.asarray(e.T)


def _rwkv_pre_fn(pieces, shifted, mus, w0, wdu, a0, wiu, wgu, k_k, k_a, e, et):
    r, k, v, wd, ad, gd = [p + (s - p) * mu for p, s, mu in zip(pieces, shifted, mus)]
    w_log = -_softplus(-(w0 + _dot(jnp.tanh(wd), wdu))) - 0.5
    lw = -jnp.exp(w_log)
    a = _sigmoid(a0 + _dot(ad, wiu))
    g = _dot(_sigmoid(gd), wgu)
    kkr = k * k_k
    kk = kkr * _dot(lax.rsqrt(_dot(kkr * kkr, e) + 1e-6), et)
    k2 = k * (1.0 + (a - 1.0) * k_a)
    return r, lw, k2, v, kk, kk * a, g


_RW_GROUPS = ((RW_R, MIX_WIDTH), (RW_K, MIX_WIDTH), (RW_V, MIX_WIDTH), (RW_WD, LANES), (RW_AD, LANES), (RW_GD, 2 * LANES))


def _rwkv_pre_inputs(p_ref, prev_ref, mu_ref, first):
    pieces = [p_ref[:, o:o + n] for o, n in _RW_GROUPS]
    shifted = [_shift_down(p_ref[:, o:o + n], prev_ref[:, o:o + n] * first, 1) for o, n in _RW_GROUPS]
    mus = [mu_ref[:, o:o + n] for o, n in _RW_GROUPS]
    return pieces, shifted, mus


def _rwkv_param_specs():
    vec = lambda n: pl.BlockSpec((1, n), lambda i: (0, 0))
    mat = lambda r, c: pl.BlockSpec((r, c), lambda i: (0, 0))
    return [vec(RW_SHIFT), vec(MIX_WIDTH), mat(LANES, MIX_WIDTH), vec(MIX_WIDTH), mat(LANES, MIX_WIDTH),
            mat(2 * LANES, MIX_WIDTH), vec(MIX_WIDTH), vec(MIX_WIDTH), mat(MIX_WIDTH, LANES), mat(LANES, MIX_WIDTH)]


def rwkv_pre_fwd(p, params, name):
    s = p.shape[0]
    t = _tile(s, (128, 64))

    def body(p_ref, prev_ref, mu_ref, *rest):
        prm, outs = rest[:9], rest[9:]
        first = (pl.program_id(0) > 0).astype(F32)
        pieces, shifted, mus = _rwkv_pre_inputs(p_ref, prev_ref, mu_ref, first)
        res = _rwkv_pre_fn(pieces, shifted, mus, *[q[...] for q in prm])
        for o_ref, val in zip(outs, res):
            o_ref[...] = val

    row = pl.BlockSpec((t, RW_SHIFT), lambda i: (i, 0))
    prev = pl.BlockSpec((SUBLANES, RW_SHIFT), lambda i: (jnp.maximum(i * (t // SUBLANES) - 1, 0), 0))
    out = pl.BlockSpec((t, MIX_WIDTH), lambda i: (i, 0))
    return pl.pallas_call(
        body, grid=(s // t,), in_specs=[row, prev] + _rwkv_param_specs(), out_specs=[out] * 7,
        out_shape=[jax.ShapeDtypeStruct((s, MIX_WIDTH), F32)] * 7, compiler_params=_cp(("parallel",)),
        name=name)(p, p, *params)


def rwkv_pre_bwd(p, params, cots, name):
    s = p.shape[0]
    t = _tile(s, (64, 32))

    def body(p_ref, prev_ref, mu_ref, *rest):
        prm, cot, outs = rest[:9], rest[9:19], rest[19:]
        dp_ref, dps_ref, grads = outs[0], outs[1], outs[2:]
        i = pl.program_id(0)
        first = (i > 0).astype(F32)
        pieces, shifted, mus = _rwkv_pre_inputs(p_ref, prev_ref, mu_ref, first)
        prm_v = [q[...] for q in prm]
        fn = lambda pieces, shifted, mus, *small: _rwkv_pre_fn(pieces, shifted, mus, *small, prm_v[7], prm_v[8])
        _, vjp = jax.vjp(fn, pieces, shifted, mus, *prm_v[:7])
        dr, dw, dk2, dv, dkk, db, dr2, dk22, dv2, dg = [c[...] for c in cot]
        res = vjp((dr + dr2, dw, dk2 + dk22, dv + dv2, dkk, db, dg))
        dpieces, dshifted, dmus, dsmall = res[0], res[1], res[2], res[3:]
        for (o, n), dpi, dsi in zip(_RW_GROUPS, dpieces, dshifted):
            dp_ref[:, o:o + n] = dpi
            dps_ref[:, o:o + n] = dsi

        @pl.when(i == 0)
        def _():
            for g_ref in grads:
                g_ref[...] = jnp.zeros_like(g_ref)

        for (o, n), dmu in zip(_RW_GROUPS, dmus):
            grads[0][:, o:o + n] += dmu
        for g_ref, dval in zip(grads[1:], dsmall):
            g_ref[...] += dval

    row = pl.BlockSpec((t, RW_SHIFT), lambda i: (i, 0))
    prev = pl.BlockSpec((SUBLANES, RW_SHIFT), lambda i: (jnp.maximum(i * (t // SUBLANES) - 1, 0), 0))
    act = pl.BlockSpec((t, MIX_WIDTH), lambda i: (i, 0))
    pspecs = _rwkv_param_specs()
    full = jax.ShapeDtypeStruct((s, RW_SHIFT), F32)
    gshapes = [jax.ShapeDtypeStruct(q.shape, F32) for q in params[:8]]
    return pl.pallas_call(
        body, grid=(s // t,), in_specs=[row, prev] + pspecs + [act] * 10, out_specs=[row, row] + pspecs[:8],
        out_shape=[full, full] + gshapes, compiler_params=_cp(("arbitrary",)), name=name)(p, p, *params, *cots)


def shift_add(a, b, js, out_dtype, name):
    s, c = a.shape
    t = _tile(s, (256, 128, 64))
    tc = _tile(c, (1024, 768, 512, 640, 384, 256, 128))
    nt, nb = s // t, len(b)

    def body(a_ref, *rest):
        b_refs, n_refs, o_ref = rest[:nb], rest[nb:2 * nb], rest[2 * nb]
        last = (pl.program_id(1) < nt - 1).astype(F32)
        acc = a_ref[...]
        for b_ref, n_ref, j in zip(b_refs, n_refs, js):
            acc = acc + _shift_up(b_ref[...], n_ref[...] * last, j)
        o_ref[...] = acc.astype(o_ref.dtype)

    tile = pl.BlockSpec((t, tc), lambda j, i: (i, j))
    _, nxt = _halo_specs(t, s, tc, lambda j: j)
    return pl.pallas_call(
        body, grid=(c // tc, nt), in_specs=[tile] * (1 + nb) + [nxt] * nb, out_specs=tile,
        out_shape=jax.ShapeDtypeStruct((s, c), out_dtype), compiler_params=_cp(("parallel", "parallel")),
        name=name)(a, *b, *b)


def _rwkv_post_fn(y, r, k2, v, g, gn_g, gn_b, r_k, e, et):
    n = RWKV_HEAD_DIM
    yc = y - _dot(_dot(y, e), et) * (1.0 / n)
    rstd = lax.rsqrt(_dot(yc * yc, e) * (1.0 / n) + RWKV_GN_EPS)
    yn = yc * _dot(rstd, et) * gn_g + gn_b
    bonus = _dot(_dot(r * k2 * r_k, e), et) * v
    return (yn + bonus) * g


def rwkv_post_fwd(acts, params, name):
    s = acts[0].shape[0]
    t = _tile(s, (256, 128))

    def body(*refs):
        vals = [q[...] for q in refs[:10]]
        refs[10][...] = _rwkv_post_fn(*vals).astype(refs[10].dtype)

    act = pl.BlockSpec((t, MIX_WIDTH), lambda i: (i, 0))
    vec = pl.BlockSpec((1, MIX_WIDTH), lambda i: (0, 0))
    mats = [pl.BlockSpec((MIX_WIDTH, LANES), lambda i: (0, 0)), pl.BlockSpec((LANES, MIX_WIDTH), lambda i: (0, 0))]
    return pl.pallas_call(
        body, grid=(s // t,), in_specs=[act] * 5 + [vec] * 3 + mats, out_specs=act,
        out_shape=jax.ShapeDtypeStruct((s, MIX_WIDTH), BF16), compiler_params=_cp(("parallel",)),
        name=name)(*acts, *params)


def rwkv_post_bwd(acts, params, dcat, name):
    s = acts[0].shape[0]
    t = _tile(s, (128, 64))

    def body(*refs):
        ins, dy_ref, outs = refs[:10], refs[10], refs[11:]
        vals = [q[...] for q in ins]
        fn = lambda *a: _rwkv_post_fn(*a, vals[8], vals[9])
        _, vjp = jax.vjp(fn, *vals[:8])
        res = vjp(dy_ref[...])
        for o_ref, val in zip(outs[:5], res[:5]):
            o_ref[...] = val

        @pl.when(pl.program_id(0) == 0)
        def _():
            for g_ref in outs[5:]:
                g_ref[...] = jnp.zeros_like(g_ref)

        for g_ref, val in zip(outs[5:], res[5:]):
            g_ref[...] += val

    act = pl.BlockSpec((t, MIX_WIDTH), lambda i: (i, 0))
    vec = pl.BlockSpec((1, MIX_WIDTH), lambda i: (0, 0))
    mats = [pl.BlockSpec((MIX_WIDTH, LANES), lambda i: (0, 0)), pl.BlockSpec((LANES, MIX_WIDTH), lambda i: (0, 0))]
    a_shape = jax.ShapeDtypeStruct((s, MIX_WIDTH), F32)
    v_shape = jax.ShapeDtypeStruct((1, MIX_WIDTH), F32)
    return pl.pallas_call(
        body, grid=(s // t,), in_specs=[act] * 5 + [vec] * 3 + mats + [act], out_specs=[act] * 5 + [vec] * 3,
        out_shape=[a_shape] * 5 + [v_shape] * 3, compiler_params=_cp(("arbitrary",)), name=name)(*acts, *params, dcat)


RW_CHUNK = 64


RW_HEADS_PER_STEP = 4


def _rwkv_chunk_fn(r, lw, k, v, kk, b, st):
    c = RW_CHUNK
    ri = lax.broadcasted_iota(jnp.int32, (c, c), 0)
    ci = lax.broadcasted_iota(jnp.int32, (c, c), 1)
    incl, strict = ri >= ci, ri > ci
    eye = (ri == ci).astype(F32)
    last_col = (ci == c - 1).astype(F32)
    last_row = (ri == c - 1).astype(F32)
    gc = _dot(incl.astype(F32), lw)
    a_t = -kk * jnp.exp(gc - lw)
    e_neg = jnp.exp(-gc)
    b_t, k_t, r_t = b * e_neg, k * e_neg, r * jnp.exp(gc)
    m_ab = jnp.where(strict, _dot_nt(a_t, b_t), 0.0)
    m_ak = jnp.where(strict, _dot_nt(a_t, k_t), 0.0)
    m_rb = jnp.where(incl, _dot_nt(r_t, b_t), 0.0)
    m_rk = jnp.where(incl, _dot_nt(r_t, k_t), 0.0)
    tinv, pw = eye + m_ab, m_ab
    for _ in range(5):
        pw = _dot(pw, pw)
        tinv = tinv + _dot(tinv, pw)
    u = _dot(tinv, _dot(a_t, st) + _dot(m_ak, v))
    y = _dot(r_t, st) + _dot(m_rb, u) + _dot(m_rk, v)
    dec = jnp.exp(_dot(last_col, gc) - gc)
    g_end = _dot_tn(gc, last_row)
    new_st = st * jnp.exp(g_end) + _dot_tn(b * dec, u) + _dot_tn(k * dec, v)
    return y, new_st


def rwkv_scan_fwd(r, lw, k, v, kk, b, name):
    s = r.shape[0]
    n, hp = RWKV_HEAD_DIM, RW_HEADS_PER_STEP
    nchunk, width = s // RW_CHUNK, RWKV_HEAD_DIM * RW_HEADS_PER_STEP

    def body(r_ref, w_ref, k_ref, v_ref, kk_ref, b_ref, y_ref, ck_ref, carry):
        @pl.when(pl.program_id(1) == 0)
        def _():
            carry[...] = jnp.zeros_like(carry)

        ck_ref[0] = carry[...]
        ys, sts = [], []
        for h in range(hp):
            cols = slice(h * n, (h + 1) * n)
            y, st = _rwkv_chunk_fn(*[q[:, cols] for q in (r_ref, w_ref, k_ref, v_ref, kk_ref, b_ref)], carry[:, cols])
            ys.append(y)
            sts.append(st)
        y_ref[...] = jnp.concatenate(ys, axis=1)
        carry[...] = jnp.concatenate(sts, axis=1)

    blk = pl.BlockSpec((RW_CHUNK, width), lambda j, c: (c, j))
    return pl.pallas_call(
        body, grid=(MIX_WIDTH // width, nchunk), in_specs=[blk] * 6,
        out_specs=[blk, pl.BlockSpec((1, n, width), lambda j, c: (c, 0, j))],
        out_shape=[jax.ShapeDtypeStruct((s, MIX_WIDTH), F32), jax.ShapeDtypeStruct((nchunk, n, MIX_WIDTH), F32)],
        scratch_shapes=[pltpu.VMEM((n, width), F32)],
        compiler_params=_cp(("parallel", "arbitrary")), name=name)(r, lw, k, v, kk, b)


def rwkv_scan_bwd(r, lw, k, v, kk, b, ck, dy, name):
    s = r.shape[0]
    n, hp = RWKV_HEAD_DIM, RW_HEADS_PER_STEP
    nchunk, width = s // RW_CHUNK, RWKV_HEAD_DIM * RW_HEADS_PER_STEP
    rev = lambda c: nchunk - 1 - c

    def body(r_ref, w_ref, k_ref, v_ref, kk_ref, b_ref, ck_ref, dy_ref, *rest):
        outs, carry = rest[:6], rest[6]

        @pl.when(pl.program_id(1) == 0)
        def _():
            carry[...] = jnp.zeros_like(carry)

        grads = []
        for h in range(hp):
            cols = slice(h * n, (h + 1) * n)
            args = [q[:, cols] for q in (r_ref, w_ref, k_ref, v_ref, kk_ref, b_ref)] + [ck_ref[0, :, cols]]
            _, vjp = jax.vjp(_rwkv_chunk_fn, *args)
            grads.append(vjp((dy_ref[:, cols], carry[:, cols])))
        for q in range(6):
            outs[q][...] = jnp.concatenate([g[q] for g in grads], axis=1)
        carry[...] = jnp.concatenate([g[6] for g in grads], axis=1)

    blk = pl.BlockSpec((RW_CHUNK, width), lambda j, c: (rev(c), j))
    out = jax.ShapeDtypeStruct((s, MIX_WIDTH), F32)
    return pl.pallas_call(
        body, grid=(MIX_WIDTH // width, nchunk),
        in_specs=[blk] * 6 + [pl.BlockSpec((1, n, width), lambda j, c: (rev(c), 0, j)), blk],
        out_specs=[blk] * 6, out_shape=[out] * 6, scratch_shapes=[pltpu.VMEM((n, width), F32)],
        compiler_params=_cp(("parallel", "arbitrary")), name=name)(r, lw, k, v, kk, b, ck, dy)


GD_Q, GD_K, GD_V, GD_Z, GD_QMEM, GD_BT, GD_AT, GD_COLS = 0, 768, 1536, 3072, 4608, 5120, 5248, 5376
_GD_GROUPS = ((GD_Q, GDN_QK_WIDTH), (GD_K, GDN_QK_WIDTH), (GD_V, MIX_WIDTH))


def _gdn_pre_fn(xs, convs, bt, at, a_log, dt_bias, e6, e6t, ebc):
    k_w = GDN_CONV
    acts = [_silu(sum(convs[g][j] * xs[g][k_w - 1 - j] for j in range(k_w))) for g in range(3)]
    l2 = lambda x: x * _dot(lax.rsqrt(_dot(x * x, e6) + 1e-6), e6t)
    beta = _sigmoid(bt)
    g = -jnp.exp(a_log) * _softplus(at + dt_bias)
    return l2(acts[0]), l2(acts[1]), acts[2], _dot(g, ebc), _dot(beta, ebc)


def _gdn_pre_inputs(x_ref, prev_ref, conv_ref, first):
    xs = [[_shift_down(x_ref[:, o:o + n], prev_ref[:, o:o + n] * first, j) for j in range(GDN_CONV)]
          for o, n in _GD_GROUPS]
    convs = [[conv_ref[j:j + 1, o:o + n] for j in range(GDN_CONV)] for o, n in _GD_GROUPS]
    return xs, convs


def _gdn_pre_specs(t):
    x = pl.BlockSpec((t, GDN_CONV_WIDTH), lambda i: (i, 0))
    prev = pl.BlockSpec((SUBLANES, GDN_CONV_WIDTH), lambda i: (jnp.maximum(i * (t // SUBLANES) - 1, 0), 0))
    bta = pl.BlockSpec((t, 2 * LANES), lambda i: (i, GD_BT // (2 * LANES)))
    conv = pl.BlockSpec((GDN_CONV, GDN_CONV_WIDTH), lambda i: (0, 0))
    vec = pl.BlockSpec((1, LANES), lambda i: (0, 0))
    mats = [pl.BlockSpec((GDN_QK_WIDTH, LANES), lambda i: (0, 0)), pl.BlockSpec((LANES, GDN_QK_WIDTH), lambda i: (0, 0)),
            pl.BlockSpec((LANES, MIX_WIDTH), lambda i: (0, 0))]
    return [x, prev, bta, conv, vec, vec] + mats


def gdn_pre_fwd(p, params, name):
    s = p.shape[0]
    t = _tile(s, (128, 64))

    def body(x_ref, prev_ref, bta_ref, conv_ref, al_ref, dt_ref, e6_ref, e6t_ref, ebc_ref, *outs):
        first = (pl.program_id(0) > 0).astype(F32)
        xs, convs = _gdn_pre_inputs(x_ref, prev_ref, conv_ref, first)
        res = _gdn_pre_fn(xs, convs, bta_ref[:, :LANES], bta_ref[:, LANES:], al_ref[...], dt_ref[...],
                          e6_ref[...], e6t_ref[...], ebc_ref[...])
        for o_ref, val in zip(outs, res):
            o_ref[...] = val

    qk = pl.BlockSpec((t, GDN_QK_WIDTH), lambda i: (i, 0))
    wide = pl.BlockSpec((t, MIX_WIDTH), lambda i: (i, 0))
    qk_s, wide_s = jax.ShapeDtypeStruct((s, GDN_QK_WIDTH), F32), jax.ShapeDtypeStruct((s, MIX_WIDTH), F32)
    return pl.pallas_call(
        body, grid=(s // t,), in_specs=_gdn_pre_specs(t), out_specs=[qk, qk, wide, wide, wide],
        out_shape=[qk_s, qk_s, wide_s, wide_s, wide_s], compiler_params=_cp(("parallel",)), name=name)(p, p, p, *params)


def gdn_pre_bwd(p, params, cots, name):
    s = p.shape[0]
    t = _tile(s, (64, 32))

    def body(x_ref, prev_ref, bta_ref, conv_ref, al_ref, dt_ref, e6_ref, e6t_ref, ebc_ref, *rest):
        cot, outs = rest[:5], rest[5:]
        dxs, dbta_ref, dconv_ref, dal_ref, ddt_ref = outs[:4], outs[4], outs[5], outs[6], outs[7]
        i = pl.program_id(0)
        first = (i > 0).astype(F32)
        xs, convs = _gdn_pre_inputs(x_ref, prev_ref, conv_ref, first)
        mats = (e6_ref[...], e6t_ref[...], ebc_ref[...])
        fn = lambda xs, convs, bt, at, al, dt: _gdn_pre_fn(xs, convs, bt, at, al, dt, *mats)
        _, vjp = jax.vjp(fn, xs, convs, bta_ref[:, :LANES], bta_ref[:, LANES:], al_ref[...], dt_ref[...])
        d_xs, d_convs, d_bt, d_at, d_al, d_dt = vjp(tuple(c[...] for c in cot))
        for g, (o, n) in enumerate(_GD_GROUPS):
            for j in range(GDN_CONV):
                dxs[j][:, o:o + n] = d_xs[g][j]
        dbta_ref[...] = jnp.concatenate([d_bt, d_at], axis=1).astype(dbta_ref.dtype)

        @pl.when(i == 0)
        def _():
            dconv_ref[...] = jnp.zeros_like(dconv_ref)
            dal_ref[...] = jnp.zeros_like(dal_ref)
            ddt_ref[...] = jnp.zeros_like(ddt_ref)

        for g, (o, n) in enumerate(_GD_GROUPS):
            for j in range(GDN_CONV):
                dconv_ref[j:j + 1, o:o + n] += d_convs[g][j]
        dal_ref[...] += d_al
        ddt_ref[...] += d_dt

    specs = _gdn_pre_specs(t)
    qk = pl.BlockSpec((t, GDN_QK_WIDTH), lambda i: (i, 0))
    wide = pl.BlockSpec((t, MIX_WIDTH), lambda i: (i, 0))
    x_s = jax.ShapeDtypeStruct((s, GDN_CONV_WIDTH), F32)
    vec_s = jax.ShapeDtypeStruct((1, LANES), F32)
    return pl.pallas_call(
        body, grid=(s // t,), in_specs=specs + [qk, qk, wide, wide, wide],
        out_specs=[specs[0]] * 4 + [pl.BlockSpec((t, 2 * LANES), lambda i: (i, 0)), specs[3], specs[4], specs[5]],
        out_shape=[x_s] * 4 + [jax.ShapeDtypeStruct((s, 2 * LANES), BF16),
                               jax.ShapeDtypeStruct((GDN_CONV, GDN_CONV_WIDTH), F32), vec_s, vec_s],
        compiler_params=_cp(("arbitrary",)), name=name)(p, p, p, *params, *cots)


def _gdn_post_fn(o, z, norm_g, e12, e12t, trep):
    rstd = lax.rsqrt(_dot(o * o, e12) * (1.0 / GDN_HEAD_DIM) + NORM_EPS)
    return o * _dot(rstd, e12t) * _dot(norm_g, trep) * _silu(z)


def _gdn_post_specs(t):
    act = pl.BlockSpec((t, MIX_WIDTH), lambda i: (i, 0))
    z = pl.BlockSpec((t, MIX_WIDTH), lambda i: (i, GD_Z // MIX_WIDTH))
    mats = [pl.BlockSpec((SUBLANES, LANES), lambda i: (0, 0)), pl.BlockSpec((MIX_WIDTH, LANES), lambda i: (0, 0)),
            pl.BlockSpec((LANES, MIX_WIDTH), lambda i: (0, 0)), pl.BlockSpec((LANES, MIX_WIDTH), lambda i: (0, 0))]
    return [act, z] + mats


def gdn_post_fwd(o, p, params, name):
    s = o.shape[0]
    t = _tile(s, (256, 128))

    def body(o_ref, z_ref, ng_ref, e_ref, et_ref, tr_ref, out_ref):
        res = _gdn_post_fn(o_ref[...], z_ref[...], ng_ref[0:1, :], e_ref[...], et_ref[...], tr_ref[...])
        out_ref[...] = res.astype(out_ref.dtype)

    act = pl.BlockSpec((t, MIX_WIDTH), lambda i: (i, 0))
    return pl.pallas_call(
        body, grid=(s // t,), in_specs=_gdn_post_specs(t), out_specs=act,
        out_shape=jax.ShapeDtypeStruct((s, MIX_WIDTH), BF16), compiler_params=_cp(("parallel",)),
        name=name)(o, p, *params)


def gdn_post_bwd(o, p, params, dcat, name):
    s = o.shape[0]
    t = _tile(s, (128, 64))

    def body(o_ref, z_ref, ng_ref, e_ref, et_ref, tr_ref, dy_ref, do_ref, dz_ref, dng_ref):
        mats = (e_ref[...], et_ref[...], tr_ref[...])
        fn = lambda o, z, ng: _gdn_post_fn(o, z, ng, *mats)
        _, vjp = jax.vjp(fn, o_ref[...], z_ref[...], ng_ref[0:1, :])
        d_o, d_z, d_ng = vjp(dy_ref[...])
        do_ref[...] = d_o
        dz_ref[...] = d_z.astype(dz_ref.dtype)

        @pl.when(pl.program_id(0) == 0)
        def _():
            dng_ref[...] = jnp.zeros_like(dng_ref)

        dng_ref[...] += d_ng

    act = pl.BlockSpec((t, MIX_WIDTH), lambda i: (i, 0))
    return pl.pallas_call(
        body, grid=(s // t,), in_specs=_gdn_post_specs(t) + [act],
        out_specs=[act, act, pl.BlockSpec((1, LANES), lambda i: (0, 0))],
        out_shape=[jax.ShapeDtypeStruct((s, MIX_WIDTH), F32), jax.ShapeDtypeStruct((s, MIX_WIDTH), BF16),
                   jax.ShapeDtypeStruct((1, LANES), F32)],
        compiler_params=_cp(("arbitrary",)), name=name)(o, p, *params, dcat)


def _gdn_chunk_fn(q, k, v, gb, bb, gb64, state):
    c = GDN_CHUNK
    ri = lax.broadcasted_iota(jnp.int32, (c, c), 0)
    ci = lax.broadcasted_iota(jnp.int32, (c, c), 1)
    causal, strict = ri >= ci, ri > ci
    ltri = causal.astype(F32)
    eye = (ri == ci).astype(F32)
    first_col = (ci == 0).astype(F32)
    last_col = (ci == c - 1).astype(F32)
    last_col_tall = (lax.broadcasted_iota(jnp.int32, (GDN_HEAD_DIM, c), 1) == c - 1).astype(F32)

    qs = q * (GDN_HEAD_DIM ** -0.5)
    gc = _dot(ltri, gb)
    gd = _dot(ltri, gb64)
    diff = gd - _dot_nt(first_col, gd)
    decay = jnp.exp(jnp.where(causal, diff, -jnp.inf))
    kb = k * bb
    lmat = jnp.where(strict, _dot_nt(kb, k) * decay, 0.0)
    tmat, pw = eye - lmat, lmat
    for _ in range(5):
        pw = _dot(pw, pw)
        tmat = tmat + _dot(tmat, pw)
    eg = jnp.exp(gc)
    u = _dot(tmat, v * bb)
    w = _dot(tmat, kb * eg)
    a_qk = jnp.where(causal, _dot_nt(qs, k) * decay, 0.0)
    g_last = _dot(last_col, gc)
    k_dec = k * jnp.exp(g_last - gc)
    v_new = u - _dot(w, state)
    out = _dot(qs * eg, state) + _dot(a_qk, v_new)
    new_state = state * jnp.exp(_dot(last_col_tall, gc)) + _dot_tn(k_dec, v_new)
    return out, new_state


GDN_REP = GDN_V_HEADS // GDN_QK_HEADS


def _gdn_chunk_specs(order):
    c, d = GDN_CHUNK, GDN_HEAD_DIM
    qk = pl.BlockSpec((c, d), lambda j, n: (order(n), j))
    vh = pl.BlockSpec((c, GDN_REP * d), lambda j, n: (order(n), j))
    st = pl.BlockSpec((GDN_REP, 1, d, d), lambda j, n: (j, order(n), 0, 0))
    return qk, vh, st


def gdn_chunk_fwd(q, k, v, gb, bb, name):
    s = q.shape[0]
    nc, d = s // GDN_CHUNK, GDN_HEAD_DIM

    def body(q_ref, k_ref, v_ref, gb_ref, bb_ref, o_ref, st_ref, carry):
        @pl.when(pl.program_id(1) == 0)
        def _():
            carry[...] = jnp.zeros_like(carry)

        for rep in range(GDN_REP):
            cols = slice(rep * d, (rep + 1) * d)
            state = carry[rep]
            st_ref[rep, 0] = state
            out, new_state = _gdn_chunk_fn(q_ref[...], k_ref[...], v_ref[:, cols], gb_ref[:, cols], bb_ref[:, cols],
                                           gb_ref[:, rep * d:rep * d + GDN_CHUNK], state)
            o_ref[:, cols] = out
            carry[rep] = new_state

    qk, vh, st = _gdn_chunk_specs(lambda n: n)
    return pl.pallas_call(
        body, grid=(GDN_QK_HEADS, nc), in_specs=[qk, qk, vh, vh, vh], out_specs=[vh, st],
        out_shape=[jax.ShapeDtypeStruct((s, MIX_WIDTH), F32), jax.ShapeDtypeStruct((GDN_V_HEADS, nc, d, d), F32)],
        scratch_shapes=[pltpu.VMEM((GDN_REP, d, d), F32)],
        compiler_params=_cp(("parallel", "arbitrary")), name=name)(q, k, v, gb, bb)


def gdn_chunk_bwd(q, k, v, gb, bb, states, do, name):
    s = q.shape[0]
    nc, d = s // GDN_CHUNK, GDN_HEAD_DIM
    rev = lambda n: nc - 1 - n

    def body(q_ref, k_ref, v_ref, gb_ref, bb_ref, st_ref, do_ref, dq_ref, dk_ref, dv_ref, dg_ref, db_ref, carry):
        @pl.when(pl.program_id(1) == 0)
        def _():
            carry[...] = jnp.zeros_like(carry)

        d_qs, d_ks = [], []
        for rep in range(GDN_REP):
            cols = slice(rep * d, (rep + 1) * d)
            lead = slice(rep * d, rep * d + GDN_CHUNK)
            args = (q_ref[...], k_ref[...], v_ref[:, cols], gb_ref[:, cols], bb_ref[:, cols], gb_ref[:, lead],
                    st_ref[rep, 0])
            _, vjp = jax.vjp(_gdn_chunk_fn, *args)
            d_q, d_k, d_v, d_gb, d_bb, d_gb64, d_state = vjp((do_ref[:, cols], carry[rep]))
            carry[rep] = d_state
            dv_ref[:, cols] = d_v
            db_ref[:, cols] = d_bb
            dg_ref[:, cols] = d_gb
            dg_ref[:, lead] += d_gb64
            d_qs.append(d_q)
            d_ks.append(d_k)
        dq_ref[...] = sum(d_qs[1:], d_qs[0])
        dk_ref[...] = sum(d_ks[1:], d_ks[0])

    qk, vh, st = _gdn_chunk_specs(rev)
    qk_s, wide_s = jax.ShapeDtypeStruct((s, GDN_QK_WIDTH), F32), jax.ShapeDtypeStruct((s, MIX_WIDTH), F32)
    return pl.pallas_call(
        body, grid=(GDN_QK_HEADS, nc), in_specs=[qk, qk, vh, vh, vh, st, vh], out_specs=[qk, qk, vh, vh, vh],
        out_shape=[qk_s, qk_s, wide_s, wide_s, wide_s], scratch_shapes=[pltpu.VMEM((GDN_REP, d, d), F32)],
        compiler_params=_cp(("parallel", "arbitrary")), name=name)(q, k, v, gb, bb, states, do)


WEIGHTS = ['attn_norm', 'mem_norm', 'w_mem_kv', 'w_out', 'ffn_norm', 'w_ffn_up', 'ffn_conv', 'w_ffn_down', 'final_norm',
           'a_w_in', 'a_sinks', 'b_w_in', 'b_mu', 'b_w0', 'b_w_decay_up', 'b_a0', 'b_w_iclr_up', 'b_w_gate_up', 'b_k_k',
           'b_k_a', 'b_r_k', 'b_gn_g', 'b_gn_b', 'c_w_in', 'c_conv', 'c_a_log', 'c_dt_bias', 'c_norm_g']
INPUTS = ['x', 'mem'] + WEIGHTS + ['loss_target'] + ['m_' + n for n in WEIGHTS] + ['v_' + n for n in WEIGHTS]
REPLICATED = ['attn_norm', 'mem_norm', 'ffn_norm', 'final_norm', 'a_sinks', 'b_mu', 'b_w0', 'b_a0', 'b_k_k', 'b_k_a',
              'b_r_k', 'b_gn_g', 'b_gn_b', 'c_a_log', 'c_dt_bias', 'c_norm_g']
C_MIX = GDN_CONV_WIDTH + MIX_WIDTH
GATHER_ID, EXCHANGE_ID = 1, 1 + DEPTH


def _cols_to_shards(full):
    rows, cols = full.shape
    return full.reshape(rows, N_DEV, cols // N_DEV).transpose(1, 0, 2)


def _shards_to_cols(g):
    return g.transpose(1, 0, 2).reshape(g.shape[1], N_DEV * g.shape[2])


def _pad_to(x, n, axis):
    pad = [(0, 0)] * x.ndim
    pad[axis] = (0, n - x.shape[axis])
    return jnp.pad(x, pad)


def _b_pad_cols(w):
    parts = [w[..., :4608], _pad_to(w[..., 4608:4704], LANES, -1), _pad_to(w[..., 4704:4800], LANES, -1), w[..., 4800:5056]]
    if w.shape[-1] > 5056:
        parts.append(w[..., 5056:])
    return jnp.concatenate(parts, axis=-1)


def _b_unpad_cols(w):
    parts = [w[..., :4608], w[..., RW_WD:RW_WD + RWKV_DECAY_RANK], w[..., RW_AD:RW_AD + RWKV_ICLR_RANK], w[..., RW_GD:RW_SHIFT]]
    if w.shape[-1] > RW_SHIFT:
        parts.append(w[..., RW_SHIFT:])
    return jnp.concatenate(parts, axis=-1)


def _c_pad_cols(w):
    return jnp.concatenate([w[..., :C_MIX], w[..., C_MIX + 24:], _pad_to(w[..., C_MIX:C_MIX + 12], LANES, -1),
                            _pad_to(w[..., C_MIX + 12:C_MIX + 24], LANES, -1)], axis=-1)


def _c_unpad_cols(w):
    return jnp.concatenate([w[..., :C_MIX], w[..., GD_BT:GD_BT + GDN_V_HEADS], w[..., GD_AT:GD_AT + GDN_V_HEADS],
                            w[..., GD_QMEM:GD_BT]], axis=-1)


def _pack(arrays):
    flat = jnp.concatenate([a.reshape(-1).astype(F32) for a in arrays])
    unit = SUBLANES * LANES
    return _pad_to(flat, -(-flat.size // unit) * unit, 0).reshape(-1, LANES)


def _unpack(packed, shapes):
    flat, out, at = packed.reshape(-1), [], 0
    for shp in shapes:
        n = int(np.prod(shp))
        out.append(flat[at:at + n].reshape(shp))
        at += n
    return out


def kernel(*args):
    a = dict(zip(INPUTS, args))
    x0, mem, target = a['x'][0], a['mem'][0], a['loss_target'][0]
    s = x0.shape[0]
    e64, e64t = _head_matrix(MIX_WIDTH, RWKV_HEAD_DIM)
    e6, e6t = _head_matrix(GDN_QK_WIDTH, GDN_HEAD_DIM)
    e12, e12t = _head_matrix(MIX_WIDTH, GDN_HEAD_DIM)
    trep = jnp.asarray((np.arange(LANES)[:, None] == np.arange(MIX_WIDTH)[None, :] % LANES).astype(np.float32))
    row = lambda v: v.reshape(1, -1)

    def in_proj_shard(l):
        kind, j = l % 3, l // 3
        return (a['a_w_in'], a['b_w_in'], a['c_w_in'])[kind][j]

    def small_shards(l):
        kind, j = l % 3, l // 3
        if kind == 1:
            return [a['b_w_decay_up'][j], a['b_w_iclr_up'][j], a['b_w_gate_up'][j]]
        if kind == 2:
            return [a['c_conv'][j]]
        return []

    layers = []
    for l in range(DEPTH):
        kind = l % 3
        big = [a['w_mem_kv'][l], a['w_out'][l], a['w_ffn_up'][l], a['w_ffn_down'][l], in_proj_shard(l)]
        small = [a['ffn_conv'][l]] + small_shards(l)
        g = all_gather_many_async([w.astype(BF16) for w in big] + small, f"gather_weights_{l}", GATHER_ID + l)
        w_in = _shards_to_cols(g[4])
        lw = dict(w_kv=g[0].reshape(D_MODEL, 2 * MEM_WIDTH), w_out=g[1].reshape(D_MODEL, D_MODEL),
                  w_up=_shards_to_cols(g[2]), w_down=g[3].reshape(D_FF, D_MODEL), conv=_shards_to_cols(g[5]))
        if kind == 0:
            lw['w_in'] = w_in
        elif kind == 1:
            lw['w_in'] = _b_pad_cols(w_in)
            lw['wdu'] = _pad_to(_shards_to_cols(g[6]), LANES, 0)
            lw['wiu'] = _pad_to(_shards_to_cols(g[7]), LANES, 0)
            lw['wgu'] = _shards_to_cols(g[8])
        else:
            lw['w_in'] = _c_pad_cols(w_in)
            lw['c_conv'] = _shards_to_cols(g[6])
        layers.append(lw)

    def rwkv_params(j, lw):
        return (row(_b_pad_cols(a['b_mu'][j])), row(a['b_w0'][j]), lw['wdu'], row(a['b_a0'][j]), lw['wiu'], lw['wgu'],
                row(a['b_k_k'][j]), row(a['b_k_a'][j]), e64, e64t)

    def rwkv_post_params(j):
        return (row(a['b_gn_g'][j]), row(a['b_gn_b'][j]), row(a['b_r_k'][j]), e64, e64t)

    def gdn_params(j, lw):
        return (lw['c_conv'], _pad_lanes(a['c_a_log'][j]), _pad_lanes(a['c_dt_bias'][j]), e6, e6t, e12t)

    def gdn_post_params(j):
        return (jnp.tile(row(a['c_norm_g'][j]), (SUBLANES, 1)), e12, e12t, trep)

    x = x0
    saved = []
    for l, lw in enumerate(layers):
        kind, j = l % 3, l // 3
        sv = dict(x=x)
        h = rmsnorm_fwd(x, a['attn_norm'][l], BF16, f"attn_norm_{l}")
        memn = rmsnorm_fwd(mem, a['mem_norm'][l], BF16, f"mem_norm_{l}")
        mem_kv = mm(memn, lw['w_kv'], name=f"mem_kv_{l}")
        p = mm(h, lw['w_in'], name=f"in_proj_{l}")
        if kind == 0:
            y = swa_fwd(p, a['a_sinks'][j], f"swa_{l}")
            q_col = MIX_WIDTH + 2 * SWA_KV_HEADS * SWA_HEAD_DIM
        elif kind == 1:
            pre = rwkv_pre_fwd(p, rwkv_params(j, lw), f"rwkv_pre_{l}")
            yscan, ck = rwkv_scan_fwd(*pre[:6], f"rwkv_scan_{l}")
            post_in = (yscan, pre[0], pre[2], pre[3], pre[6])
            y = rwkv_post_fwd(post_in, rwkv_post_params(j), f"rwkv_post_{l}")
            sv.update(pre=pre, ck=ck, post_in=post_in)
            q_col = RW_SHIFT
        else:
            pre = gdn_pre_fwd(p, gdn_params(j, lw), f"gdn_pre_{l}")
            o, states = gdn_chunk_fwd(*pre, f"gdn_chunk_{l}")
            y = gdn_post_fwd(o, p, gdn_post_params(j), f"gdn_post_{l}")
            sv.update(pre=pre, o=o, states=states)
            q_col = GD_QMEM
        y_mem = mem_attn_fwd(p, q_col, mem_kv, f"mem_attn_{l}")
        cat = jnp.concatenate([y, y_mem], axis=1)
        x1 = mm(cat, lw['w_out'], res=x, name=f"out_proj_{l}")
        hf = rmsnorm_fwd(x1, a['ffn_norm'][l], BF16, f"ffn_norm_{l}")
        u0 = mm(hf, lw['w_up'], name=f"ffn_up_{l}")
        act = ffn_act_fwd(u0, lw['conv'], f"ffn_act_{l}")
        x = mm(act, lw['w_down'], res=x1, name=f"ffn_down_{l}")
        sv.update(h=h, memn=memn, mem_kv=mem_kv, p=p, q_col=q_col, cat=cat, x1=x1, hf=hf, u0=u0, act=act)
        saved.append(sv)

    loss_part, dx, d_final_norm = final_loss(x, a['final_norm'], target, "final_loss")

    rep_grads = {n: [None] * a[n].shape[0] for n in ('attn_norm', 'mem_norm', 'ffn_norm', 'a_sinks')}
    rep_grads['final_norm'] = d_final_norm
    results = {}
    exchanged = {}

    def apply_adam(name, idx, pieces, tag):
        w, m, v = a[name][idx], a['m_' + name][idx], a['v_' + name][idx]
        shp = w.shape
        two_d = (-1, shp[-1])
        out = adamw_sum(pieces.reshape((N_DEV,) + w.reshape(two_d).shape), w.reshape(two_d), m.reshape(two_d),
                        v.reshape(two_d), f"adamw_{name}_{tag}")
        results.setdefault(name, {})[idx] = [o.reshape(shp) for o in out]

    for l in reversed(range(DEPTH)):
        kind, j = l % 3, l // 3
        lw, sv = layers[l], saved[l]
        p, q_col = sv['p'], sv['q_col']
        d_act = mm(dx, lw['w_down'], tb=True, name=f"d_ffn_act_{l}")
        dw_down = mm(sv['act'], dx, ta=True, out_dtype=BF16, name=f"dw_ffn_down_{l}")
        dug, duv, dcg, dcv = ffn_act_bwd(sv['u0'], lw['conv'], d_act, f"ffn_act_bwd_{l}")
        du0 = jnp.concatenate([dug, duv], axis=1)
        d_conv = jnp.concatenate([dcg, dcv], axis=1)
        d_hf = mm(du0, lw['w_up'], tb=True, name=f"d_ffn_norm_out_{l}")
        dw_up = mm(sv['hf'], du0, ta=True, out_dtype=BF16, name=f"dw_ffn_up_{l}")
        dx1, rep_grads['ffn_norm'][l] = rmsnorm_bwd(sv['x1'], a['ffn_norm'][l], d_hf, dx, f"ffn_norm_bwd_{l}")
        dcat = mm(dx1, lw['w_out'], tb=True, name=f"d_cat_{l}")
        dw_out = mm(sv['cat'], dx1, ta=True, out_dtype=BF16, name=f"dw_out_{l}")
        dq_mem, d_mem_kv = mem_attn_bwd(p, q_col, sv['mem_kv'], dcat, f"mem_attn_bwd_{l}")
        small_grads = []
        if kind == 0:
            dq, dk, dv, rep_grads['a_sinks'][j] = swa_bwd(p, a['a_sinks'][j], dcat, f"swa_bwd_{l}")
            dp = jnp.concatenate([dq, dk, dv, dq_mem], axis=1)
        elif kind == 1:
            post = rwkv_post_bwd(sv['post_in'], rwkv_post_params(j), dcat, f"rwkv_post_bwd_{l}")
            scan = rwkv_scan_bwd(*sv['pre'][:6], sv['ck'], post[0], f"rwkv_scan_bwd_{l}")
            res = rwkv_pre_bwd(p, rwkv_params(j, lw), tuple(scan) + tuple(post[1:5]), f"rwkv_pre_bwd_{l}")
            dp_mix = shift_add(res[0], [res[1]], [1], BF16, f"rwkv_shift_bwd_{l}")
            dp = jnp.concatenate([dp_mix, dq_mem], axis=1)
            for n, val in zip(('b_mu', 'b_w0', 'b_a0', 'b_k_k', 'b_k_a'), (_b_unpad_cols(res[2]), res[3], res[5], res[8], res[9])):
                rep_grads[n] = val
            rep_grads.update(b_gn_g=post[5], b_gn_b=post[6], b_r_k=post[7])
            small_grads = [_cols_to_shards(res[4][:RWKV_DECAY_RANK]), _cols_to_shards(res[6][:RWKV_ICLR_RANK]),
                           _cols_to_shards(res[7])]
        else:
            d_o, dz, rep_grads['c_norm_g'] = gdn_post_bwd(sv['o'], p, gdn_post_params(j), dcat, f"gdn_post_bwd_{l}")
            chunk = gdn_chunk_bwd(*sv['pre'], sv['states'], d_o, f"gdn_chunk_bwd_{l}")
            res = gdn_pre_bwd(p, gdn_params(j, lw), chunk, f"gdn_pre_bwd_{l}")
            dqkv = shift_add(res[0], list(res[1:4]), [1, 2, 3], BF16, f"gdn_shift_bwd_{l}")
            dp = jnp.concatenate([dqkv, dz, dq_mem, res[4]], axis=1)
            rep_grads.update(c_a_log=res[6][:, :GDN_V_HEADS], c_dt_bias=res[7][:, :GDN_V_HEADS])
            small_grads = [_cols_to_shards(res[5])]
        d_h = mm(dp, lw['w_in'], tb=True, name=f"d_attn_norm_out_{l}")
        dw_in = mm(sv['h'], dp, ta=True, out_dtype=BF16, name=f"dw_in_{l}")
        dx, rep_grads['attn_norm'][l] = rmsnorm_bwd(sv['x'], a['attn_norm'][l], d_h, dx1, f"attn_norm_bwd_{l}")
        d_memn = mm(d_mem_kv, lw['w_kv'], tb=True, name=f"d_mem_norm_out_{l}")
        dw_kv = mm(sv['memn'], d_mem_kv, ta=True, out_dtype=BF16, name=f"dw_mem_kv_{l}")
        _, rep_grads['mem_norm'][l] = rmsnorm_bwd(mem, a['mem_norm'][l], d_memn, None, f"mem_norm_bwd_{l}")

        if kind == 1:
            dw_in = _b_unpad_cols(dw_in)
        elif kind == 2:
            dw_in = _c_unpad_cols(dw_in)
        pieces = [dw_kv.reshape(N_DEV, -1, 2 * MEM_WIDTH), dw_out.reshape(N_DEV, -1, D_MODEL), _cols_to_shards(dw_up),
                  dw_down.reshape(N_DEV, -1, D_MODEL), _cols_to_shards(dw_in), _cols_to_shards(d_conv)] + small_grads
        exchanged[l] = all_to_all_many_async(pieces, f"exchange_grads_{l}", EXCHANGE_ID + l)

    for l in reversed(range(DEPTH)):
        kind, j = l % 3, l // 3
        got = exchanged[l]
        in_name = ('a_w_in', 'b_w_in', 'c_w_in')[kind]
        for name, idx, pc in (('w_mem_kv', l, got[0]), ('w_out', l, got[1]), ('w_ffn_up', l, got[2]),
                              ('w_ffn_down', l, got[3]), (in_name, j, got[4]), ('ffn_conv', l, got[5])):
            apply_adam(name, idx, pc, l)
        if kind == 1:
            for name, pc in zip(('b_w_decay_up', 'b_w_iclr_up', 'b_w_gate_up'), got[6:]):
                apply_adam(name, j, pc, l)
        elif kind == 2:
            apply_adam('c_conv', j, got[6], l)

    rep_vals = []
    for n in REPLICATED:
        gval = rep_grads[n]
        gval = jnp.stack(gval) if isinstance(gval, list) else gval
        rep_vals.append(gval.reshape(a[n].shape))
    shapes = [a[n].shape for n in REPLICATED] + [(1,)]
    part = _pack(rep_vals + [loss_part.reshape(1)])
    gathered = all_gather_many([part], "gather_small_grads")[0]
    zero = jnp.zeros((1,), F32)
    packed = lambda pre: _pack([a[pre + n] for n in REPLICATED] + [zero])
    rep_out = adamw_sum(gathered, packed(''), packed('m_'), packed('v_'), "adamw_replicated")
    rep_out = [_unpack(o, shapes) for o in rep_out]
    loss = rep_out[0][-1][0]
    for i, n in enumerate(REPLICATED):
        results[n] = [o[i] for o in rep_out]

    def leaf(name, which):
        r = results[name]
        if isinstance(r, dict):
            return jnp.stack([r[i][which] for i in range(len(r))])
        return r[which]

    outs = [loss, dx[None]]
    for which in range(4):
        outs += [leaf(n, which) for n in WEIGHTS]
    return tuple(outs)
```

```python
import functools

import numpy as np
import jax
import jax.numpy as jnp
from jax import lax
from jax.experimental import pallas as pl
from jax.experimental.pallas import tpu as pltpu
from jax.experimental.pallas import tpu_sc as plsc

F32, BF16 = jnp.float32, jnp.bfloat16
HI = lax.Precision.HIGHEST
V7X_VMEM_BYTES = 64 * 1024 * 1024
VMEM_LIMIT = V7X_VMEM_BYTES - 8 * 1024 * 1024
SUBLANES, LANES = 8, 128
N_DEV = 8

D_MODEL = 2048
DEPTH = 4
MIX_WIDTH = 1536
MEM_HEADS, MEM_HEAD_DIM, MEM_WIDTH = 4, 128, 512
NORM_EPS = 1e-6
SWA_HEAD_DIM, SWA_Q_HEADS, SWA_KV_HEADS, SWA_GROUP, SWA_BLOCK = 64, 24, 4, 6, 128
RWKV_HEADS, RWKV_HEAD_DIM, RWKV_GN_EPS = 24, 64, 64e-5
RWKV_DECAY_RANK, RWKV_ICLR_RANK, RWKV_GATE_RANK = 96, 96, 256
GDN_HEAD_DIM, GDN_V_HEADS, GDN_QK_HEADS, GDN_CONV, GDN_CHUNK = 128, 12, 6, 4, 64
GDN_QK_WIDTH = GDN_QK_HEADS * GDN_HEAD_DIM
GDN_CONV_WIDTH = 2 * GDN_QK_WIDTH + MIX_WIDTH
D_FF, FFN_CONV = 5632, 3
ADAM_LR, ADAM_B1, ADAM_B2, ADAM_EPS, ADAM_WD, ADAM_STEP = 0.001, 0.9, 0.999, 1e-08, 0.01, 10
MESH = pl.DeviceIdType.MESH


def _cp(sem=None):
    return pltpu.CompilerParams(dimension_semantics=sem, vmem_limit_bytes=VMEM_LIMIT)


def _tile(n, cands):
    for c in cands:
        if n % c == 0:
            return c
    return n


def _dot(a, b):
    return jnp.dot(a, b, precision=HI, preferred_element_type=F32)


def _dot_nt(a, b):
    return lax.dot_general(a, b, (((1,), (1,)), ((), ())), precision=HI, preferred_element_type=F32)


def _dot_tn(a, b):
    return lax.dot_general(a, b, (((0,), (0,)), ((), ())), precision=HI, preferred_element_type=F32)


def _sigmoid(x):
    return 1.0 / (1.0 + jnp.exp(-x))


def _softplus(x):
    return jnp.maximum(x, 0.0) + jnp.log(1.0 + jnp.exp(-jnp.abs(x)))


def _silu(x):
    return x * _sigmoid(x)


def mm(a, b, *, ta=False, tb=False, res=None, out_dtype=F32, name):
    (k_a, m) = a.shape if ta else a.shape[::-1]
    (k_b, n) = b.shape[::-1] if tb else b.shape
    assert k_a == k_b, (a.shape, b.shape, ta, tb)
    kdim = k_a
    tm = _tile(m, (1024, 512, 256))
    tn = _tile(n, (1024, 768, 512, 384, 256, 128))
    tk = _tile(kdim, (512, 256, 128))
    nk = kdim // tk
    dims = (((0 if ta else 1,), (1 if tb else 0,)), ((), ()))

    def body(*refs):
        if res is None:
            a_ref, b_ref, o_ref, acc = refs
        else:
            a_ref, b_ref, r_ref, o_ref, acc = refs
        kk = pl.program_id(2)

        @pl.when(kk == 0)
        def _():
            acc[...] = jnp.zeros_like(acc)

        acc[...] += lax.dot_general(a_ref[...].astype(BF16), b_ref[...].astype(BF16), dims,
                                    preferred_element_type=F32)

        @pl.when(kk == nk - 1)
        def _():
            out = acc[...] if res is None else acc[...] + r_ref[...]
            o_ref[...] = out.astype(o_ref.dtype)

    a_spec = pl.BlockSpec((tk, tm), lambda i, j, k: (k, i)) if ta else pl.BlockSpec((tm, tk), lambda i, j, k: (i, k))
    b_spec = pl.BlockSpec((tn, tk), lambda i, j, k: (j, k)) if tb else pl.BlockSpec((tk, tn), lambda i, j, k: (k, j))
    o_spec = pl.BlockSpec((tm, tn), lambda i, j, k: (i, j))
    in_specs, args = [a_spec, b_spec], [a, b]
    if res is not None:
        in_specs.append(o_spec)
        args.append(res)
    return pl.pallas_call(
        body, grid=(m // tm, n // tn, nk), in_specs=in_specs, out_specs=o_spec,
        out_shape=jax.ShapeDtypeStruct((m, n), out_dtype), scratch_shapes=[pltpu.VMEM((tm, tn), F32)],
        compiler_params=_cp(("parallel", "parallel", "arbitrary")), name=name)(*args)


def _coords():
    return lax.axis_index("x"), lax.axis_index("y"), lax.axis_index("c")


def _block_index(p):
    return 4 * p[0] + 2 * p[1] + p[2]


def _all_gather_body(x_refs, o_refs, send, recv, loc):
    n = len(x_refs)
    x, y, c = _coords()
    me, sib = (x, y, c), (x, y, 1 - c)
    chips = [(1 - x, y), (x, 1 - y), (1 - x, 1 - y)]

    def cp(i, k, block, to, src=None):
        dst = o_refs[i].at[_block_index(block)]
        return pltpu.make_async_remote_copy(
            src_ref=dst if src is None else src, dst_ref=dst, send_sem=send.at[i, k], recv_sem=recv.at[i, k],
            device_id=to, device_id_type=MESH)

    mine = [pltpu.make_async_copy(x_refs[i], o_refs[i].at[_block_index(me)], loc.at[i]) for i in range(n)]
    for m_ in mine:
        m_.start()
    first = []
    for i in range(n):
        first.append(cp(i, 0, me, sib, src=x_refs[i]))
        for j, chip in enumerate(chips):
            first.append(cp(i, 1 + j, me, (*chip, c), src=x_refs[i]))
    for f in first:
        f.start()
    passed = []
    for j, chip in enumerate(chips):
        for i in range(n):
            cp(i, 1 + j, (*chip, c), me).wait_recv()
            p = cp(i, 4 + j, (*chip, c), sib)
            p.start()
            passed.append(p)
    for i in range(n):
        cp(i, 0, sib, me).wait_recv()
        for j, chip in enumerate(chips):
            cp(i, 4 + j, (*chip, 1 - c), me).wait_recv()
    for f in first + passed:
        f.wait_send()
    for m_ in mine:
        m_.wait()


def _all_gather_peers():
    x, y, c = _coords()
    return [(x, y, 1 - c), (1 - x, y, c), (x, 1 - y, c), (1 - x, 1 - y, c)]


def _all_to_all_peers():
    x, y, c = _coords()
    return [(1 - x if r & 4 else x, 1 - y if r & 2 else y, 1 - c if r & 1 else c) for r in range(1, N_DEV)]


def _all_to_all_body(x_refs, o_refs, send, recv, loc):
    n = len(x_refs)
    x, y, c = _coords()
    me = _block_index((x, y, c))
    mine = [pltpu.make_async_copy(x_refs[i].at[me], o_refs[i].at[me], loc.at[i]) for i in range(n)]
    for m_ in mine:
        m_.start()
    copies = []
    for r, peer in enumerate(_all_to_all_peers()):
        pidx = _block_index(peer)
        for i in range(n):
            copies.append((
                pltpu.make_async_remote_copy(
                    src_ref=x_refs[i].at[pidx], dst_ref=o_refs[i].at[me], send_sem=send.at[i, r],
                    recv_sem=recv.at[i, r], device_id=peer, device_id_type=MESH),
                pltpu.make_async_remote_copy(
                    src_ref=x_refs[i].at[pidx], dst_ref=o_refs[i].at[pidx], send_sem=send.at[i, r],
                    recv_sem=recv.at[i, r], device_id=peer, device_id_type=MESH)))
    for s, _ in copies:
        s.start()
    for s, w in copies:
        w.wait_recv()
        s.wait_send()
    for m_ in mine:
        m_.wait()


def _comm_scratch(n):
    return [pltpu.SemaphoreType.DMA((n, 7)), pltpu.SemaphoreType.DMA((n, 7)), pltpu.SemaphoreType.DMA((n,))]


def all_gather_many(xs, name):
    n = len(xs)

    def body(*refs):
        _all_gather_body(refs[:n], refs[n:2 * n], *refs[2 * n:])

    any_spec = pl.BlockSpec(memory_space=pl.ANY)
    return pl.pallas_call(
        body, in_specs=[any_spec] * n, out_specs=[any_spec] * n,
        out_shape=[jax.ShapeDtypeStruct((N_DEV,) + x.shape, x.dtype) for x in xs],
        scratch_shapes=_comm_scratch(n), name=name)(*xs)


def _on_sequencer(exchange, peers, xs, out_shapes, name, collective_id):
    x_refs = [jax.new_ref(x, memory_space=pltpu.MemorySpace.HBM) for x in xs]
    o_refs = [jax.empty_ref(s, memory_space=pltpu.MemorySpace.HBM) for s in out_shapes]

    @pl.kernel(mesh=plsc.ScalarSubcoreMesh(axis_name="sequencer", num_cores=1), name=name,
               scratch_types=tuple(_comm_scratch(len(xs))),
               compiler_params=pltpu.CompilerParams(collective_id=collective_id))
    def launch(send, recv, loc):
        barrier = pltpu.get_barrier_semaphore()
        ids = peers()
        for peer in ids:
            pl.semaphore_signal(barrier, inc=1, device_id=peer, device_id_type=MESH)
        pl.semaphore_wait(barrier, len(ids))
        exchange(x_refs, o_refs, send, recv, loc)

    launch()
    return [o[...] for o in o_refs]


def all_gather_many_async(xs, name, collective_id):
    shapes = [jax.ShapeDtypeStruct((N_DEV,) + x.shape, x.dtype) for x in xs]
    return _on_sequencer(_all_gather_body, _all_gather_peers, xs, shapes, name, collective_id)


def all_to_all_many_async(xs, name, collective_id):
    shapes = [jax.ShapeDtypeStruct(x.shape, x.dtype) for x in xs]
    return _on_sequencer(_all_to_all_body, _all_to_all_peers, xs, shapes, name, collective_id)


def adamw_sum(pieces, w, m, v, name):
    rows, cols = w.shape
    tr = _tile(rows, (128, 64, 32, 16, 8))
    c1 = 1.0 - ADAM_B1 ** ADAM_STEP
    c2 = 1.0 - ADAM_B2 ** ADAM_STEP

    def body(p_ref, w_ref, m_ref, v_ref, g_out, d_out, m_out, v_out):
        g = p_ref[0].astype(F32)
        for s in range(1, N_DEV):
            g = g + p_ref[s].astype(F32)
        m_new = ADAM_B1 * m_ref[...] + (1.0 - ADAM_B1) * g
        v_new = ADAM_B2 * v_ref[...] + (1.0 - ADAM_B2) * (g * g)
        m_hat = m_new / c1
        v_hat = v_new / c2
        g_out[...] = g
        d_out[...] = -ADAM_LR * (m_hat / (jnp.sqrt(v_hat) + ADAM_EPS) + ADAM_WD * w_ref[...])
        m_out[...] = m_new
        v_out[...] = v_new

    spec = pl.BlockSpec((tr, cols), lambda i: (i, 0))
    out = jax.ShapeDtypeStruct((rows, cols), F32)
    return pl.pallas_call(
        body, grid=(rows // tr,), in_specs=[pl.BlockSpec((N_DEV, tr, cols), lambda i: (0, i, 0)), spec, spec, spec],
        out_specs=[spec] * 4, out_shape=[out] * 4, compiler_params=_cp(("parallel",)), name=name)(pieces, w, m, v)


def rmsnorm_fwd(x, g, out_dtype, name):
    s, d = x.shape
    tr = _tile(s, (256, 128, 64, 32, 16))

    def body(x_ref, g_ref, o_ref):
        xv = x_ref[...]
        rstd = lax.rsqrt(jnp.mean(xv * xv, axis=-1, keepdims=True) + NORM_EPS)
        o_ref[...] = (xv * rstd * g_ref[...]).astype(o_ref.dtype)

    return pl.pallas_call(
        body, grid=(s // tr,), in_specs=[pl.BlockSpec((tr, d), lambda i: (i, 0)), pl.BlockSpec((1, d), lambda i: (0, 0))],
        out_specs=pl.BlockSpec((tr, d), lambda i: (i, 0)), out_shape=jax.ShapeDtypeStruct((s, d), out_dtype),
        compiler_params=_cp(("parallel",)), name=name)(x, g.reshape(1, d))


def rmsnorm_bwd(x, g, dh, dres, name):
    s, d = x.shape
    tr = _tile(s, (256, 128, 64, 32, 16))

    def body(*refs):
        if dres is None:
            x_ref, g_ref, dh_ref, dx_ref, dg_ref = refs
        else:
            x_ref, g_ref, dh_ref, dr_ref, dx_ref, dg_ref = refs
        xv = x_ref[...]
        rstd = lax.rsqrt(jnp.mean(xv * xv, axis=-1, keepdims=True) + NORM_EPS)
        xhat = xv * rstd
        dhv = dh_ref[...].astype(F32)
        dhg = dhv * g_ref[...]
        dx = rstd * (dhg - xhat * jnp.mean(dhg * xhat, axis=-1, keepdims=True))
        if dres is not None:
            dx = dx + dr_ref[...]
        dx_ref[...] = dx

        @pl.when(pl.program_id(0) == 0)
        def _():
            dg_ref[...] = jnp.zeros_like(dg_ref)

        dg_ref[...] += jnp.sum(dhv * xhat, axis=0, keepdims=True)

    row = pl.BlockSpec((tr, d), lambda i: (i, 0))
    vec = pl.BlockSpec((1, d), lambda i: (0, 0))
    ins = [x, g.reshape(1, d), dh] + ([] if dres is None else [dres])
    dx, dg = pl.pallas_call(
        body, grid=(s // tr,), in_specs=[row, vec, row] + ([] if dres is None else [row]), out_specs=[row, vec],
        out_shape=[jax.ShapeDtypeStruct((s, d), F32), jax.ShapeDtypeStruct((1, d), F32)],
        compiler_params=_cp(("arbitrary",)), name=name)(*ins)
    return dx, dg.reshape(d)


def final_loss(x, g, target, name):
    s, d = x.shape
    tr = _tile(s, (256, 128, 64, 32, 16))

    def body(x_ref, g_ref, t_ref, l_ref, dx_ref, dg_ref):
        xv = x_ref[...]
        rstd = lax.rsqrt(jnp.mean(xv * xv, axis=-1, keepdims=True) + NORM_EPS)
        xhat = xv * rstd
        err = xhat * g_ref[...] - t_ref[...]
        dy = err * (1.0 / d)
        dhg = dy * g_ref[...]
        dx_ref[...] = rstd * (dhg - xhat * jnp.mean(dhg * xhat, axis=-1, keepdims=True))

        @pl.when(pl.program_id(0) == 0)
        def _():
            dg_ref[...] = jnp.zeros_like(dg_ref)
            l_ref[...] = jnp.zeros_like(l_ref)

        dg_ref[...] += jnp.sum(dy * xhat, axis=0, keepdims=True)
        part = 0.5 * jnp.sum(jnp.mean(err * err, axis=-1, keepdims=True), axis=0, keepdims=True)
        l_ref[...] += jnp.broadcast_to(part, l_ref.shape)

    row = pl.BlockSpec((tr, d), lambda i: (i, 0))
    vec = pl.BlockSpec((1, d), lambda i: (0, 0))
    lspec = pl.BlockSpec((1, LANES), lambda i: (0, 0))
    loss, dx, dg = pl.pallas_call(
        body, grid=(s // tr,), in_specs=[row, vec, row], out_specs=[lspec, row, vec],
        out_shape=[jax.ShapeDtypeStruct((1, LANES), F32), jax.ShapeDtypeStruct((s, d), F32),
                   jax.ShapeDtypeStruct((1, d), F32)],
        compiler_params=_cp(("arbitrary",)), name=name)(x, g.reshape(1, d), target)
    return loss[0, 0], dx, dg.reshape(d)


def _shift_down(tile, prev8, j):
    if j == 0:
        return tile
    rt = pltpu.roll(tile, j, 0)
    rp = pltpu.roll(prev8, j, 0)
    rows = lax.broadcasted_iota(jnp.int32, prev8.shape, 0)
    top = jnp.where(rows < j, rp, rt[:SUBLANES])
    return jnp.concatenate([top, rt[SUBLANES:]], axis=0)


def _shift_up(tile, next8, j):
    if j == 0:
        return tile
    t = tile.shape[0]
    rt = pltpu.roll(tile, t - j, 0)
    rn = pltpu.roll(next8, SUBLANES - j, 0)
    rows = lax.broadcasted_iota(jnp.int32, next8.shape, 0)
    bot = jnp.where(rows >= SUBLANES - j, rn, rt[t - SUBLANES:])
    return jnp.concatenate([rt[:t - SUBLANES], bot], axis=0)


def _halo_specs(t_rows, s_rows, cols, col_of):
    per, last = t_rows // SUBLANES, s_rows // SUBLANES - 1
    prev = pl.BlockSpec((SUBLANES, cols), lambda j, i: (jnp.maximum(i * per - 1, 0), col_of(j)))
    nxt = pl.BlockSpec((SUBLANES, cols), lambda j, i: (jnp.minimum((i + 1) * per, last), col_of(j)))
    return prev, nxt


def ffn_act_fwd(u0, conv, name):
    s, two_f = u0.shape
    f = two_f // 2
    t, c = _tile(s, (256, 128, 64)), 512
    nc = f // c

    def body(g_ref, v_ref, gp_ref, vp_ref, wg_ref, wv_ref, a_ref):
        first = (pl.program_id(1) > 0).astype(F32)

        def conv_of(x_ref, p_ref, w_ref):
            x, p = x_ref[...], p_ref[...] * first
            return (w_ref[0:1, :] * _shift_down(x, p, 2) + w_ref[1:2, :] * _shift_down(x, p, 1) + w_ref[2:3, :] * x)

        ug = conv_of(g_ref, gp_ref, wg_ref)
        uv = conv_of(v_ref, vp_ref, wv_ref)
        a_ref[...] = (_silu(ug) * uv).astype(a_ref.dtype)

    gate = pl.BlockSpec((t, c), lambda j, i: (i, j))
    val = pl.BlockSpec((t, c), lambda j, i: (i, j + nc))
    gp, _ = _halo_specs(t, s, c, lambda j: j)
    vp, _ = _halo_specs(t, s, c, lambda j: j + nc)
    wg = pl.BlockSpec((FFN_CONV, c), lambda j, i: (0, j))
    wv = pl.BlockSpec((FFN_CONV, c), lambda j, i: (0, j + nc))
    return pl.pallas_call(
        body, grid=(nc, s // t), in_specs=[gate, val, gp, vp, wg, wv], out_specs=pl.BlockSpec((t, c), lambda j, i: (i, j)),
        out_shape=jax.ShapeDtypeStruct((s, f), BF16), compiler_params=_cp(("parallel", "parallel")),
        name=name)(u0, u0, u0, u0, conv, conv)


def ffn_act_bwd(u0, conv, da, name):
    s, two_f = u0.shape
    f = two_f // 2
    t, c = _tile(s, (256, 128, 64)), 512
    nc, nt = f // c, s // t

    def body(g_ref, v_ref, gp_ref, vp_ref, gn_ref, vn_ref, wg_ref, wv_ref, da_ref, dan_ref,
             dg_ref, dv_ref, dwg_ref, dwv_ref):
        i = pl.program_id(1)
        first, last = (i > 0).astype(F32), (i < nt - 1).astype(F32)
        zeros8 = jnp.zeros((SUBLANES, c), F32)

        def ext(x_ref, p_ref, n_ref):
            return jnp.concatenate([p_ref[...] * first, x_ref[...], n_ref[...] * last], axis=0)

        def taps(e):
            return pltpu.roll(e, 2, 0), pltpu.roll(e, 1, 0), e

        def conv_of(sh, w_ref):
            return w_ref[0:1, :] * sh[0] + w_ref[1:2, :] * sh[1] + w_ref[2:3, :] * sh[2]

        def conv_t(du, w_ref):
            n = du.shape[0]
            return w_ref[2:3, :] * du + w_ref[1:2, :] * pltpu.roll(du, n - 1, 0) + w_ref[0:1, :] * pltpu.roll(du, n - 2, 0)

        sg, sv = taps(ext(g_ref, gp_ref, gn_ref)), taps(ext(v_ref, vp_ref, vn_ref))
        ug, uv = conv_of(sg, wg_ref), conv_of(sv, wv_ref)
        dae = jnp.concatenate([zeros8, da_ref[...], dan_ref[...] * last], axis=0)
        sig = _sigmoid(ug)
        dug = dae * uv * (sig * (1.0 + ug * (1.0 - sig)))
        duv = dae * (ug * sig)
        dg_ref[...] = conv_t(dug, wg_ref)[SUBLANES:t + SUBLANES].astype(dg_ref.dtype)
        dv_ref[...] = conv_t(duv, wv_ref)[SUBLANES:t + SUBLANES].astype(dv_ref.dtype)

        @pl.when(i == 0)
        def _():
            dwg_ref[...] = jnp.zeros_like(dwg_ref)
            dwv_ref[...] = jnp.zeros_like(dwv_ref)

        def dconv(du, sh):
            d = du[SUBLANES:t + SUBLANES]
            return jnp.concatenate([jnp.sum(d * x[SUBLANES:t + SUBLANES], axis=0, keepdims=True) for x in sh], axis=0)

        dwg_ref[...] += dconv(dug, sg)
        dwv_ref[...] += dconv(duv, sv)

    gate = pl.BlockSpec((t, c), lambda j, i: (i, j))
    val = pl.BlockSpec((t, c), lambda j, i: (i, j + nc))
    gp, gn = _halo_specs(t, s, c, lambda j: j)
    vp, vn = _halo_specs(t, s, c, lambda j: j + nc)
    wg = pl.BlockSpec((FFN_CONV, c), lambda j, i: (0, j))
    wv = pl.BlockSpec((FFN_CONV, c), lambda j, i: (0, j + nc))
    wout = pl.BlockSpec((FFN_CONV, c), lambda j, i: (0, j))
    half = jax.ShapeDtypeStruct((s, f), BF16)
    dwh = jax.ShapeDtypeStruct((FFN_CONV, f), F32)
    return pl.pallas_call(
        body, grid=(nc, nt), in_specs=[gate, val, gp, vp, gn, vn, wg, wv, gate, gn],
        out_specs=[gate, gate, wout, wout], out_shape=[half, half, dwh, dwh],
        compiler_params=_cp(("parallel", "arbitrary")), name=name)(u0, u0, u0, u0, u0, u0, conv, conv, da, da)


def _softmax_rows(s, extra=None):
    m = jnp.max(s, axis=-1, keepdims=True)
    if extra is not None:
        m = jnp.maximum(m, extra)
    m = lax.stop_gradient(m)
    e = jnp.exp(s - m)
    den = jnp.sum(e, axis=-1, keepdims=True)
    if extra is not None:
        den = den + jnp.exp(extra - m)
    return e / den


def _mem_attn_fn(qs, ks, vs):
    outs = []
    for q, k, v in zip(qs, ks, vs):
        p = _softmax_rows(_dot_nt(q, k) * (MEM_HEAD_DIM ** -0.5))
        outs.append(_dot(p, v))
    return outs


def _mem_heads(q_ref, kv_ref):
    d = MEM_HEAD_DIM
    qs = [q_ref[:, h * d:(h + 1) * d] for h in range(MEM_HEADS)]
    ks = [kv_ref[:, h * d:(h + 1) * d] for h in range(MEM_HEADS)]
    vs = [kv_ref[:, MEM_WIDTH + h * d:MEM_WIDTH + (h + 1) * d] for h in range(MEM_HEADS)]
    return qs, ks, vs


def mem_attn_fwd(p, q_col, kv, name):
    s = p.shape[0]
    t = _tile(s, (256, 128))
    m = kv.shape[0]

    def body(q_ref, kv_ref, o_ref):
        outs = _mem_attn_fn(*_mem_heads(q_ref, kv_ref))
        o_ref[...] = jnp.concatenate(outs, axis=1).astype(o_ref.dtype)

    return pl.pallas_call(
        body, grid=(s // t,),
        in_specs=[pl.BlockSpec((t, MEM_WIDTH), lambda i: (i, q_col // MEM_WIDTH)),
                  pl.BlockSpec((m, 2 * MEM_WIDTH), lambda i: (0, 0))],
        out_specs=pl.BlockSpec((t, MEM_WIDTH), lambda i: (i, 0)), out_shape=jax.ShapeDtypeStruct((s, MEM_WIDTH), BF16),
        compiler_params=_cp(("parallel",)), name=name)(p, kv)


def mem_attn_bwd(p, q_col, kv, dcat, name):
    s = p.shape[0]
    t = _tile(s, (256, 128))
    m = kv.shape[0]
    d = MEM_HEAD_DIM

    def body(q_ref, kv_ref, dy_ref, dq_ref, dkv_ref):
        qs, ks, vs = _mem_heads(q_ref, kv_ref)
        _, vjp = jax.vjp(_mem_attn_fn, qs, ks, vs)
        dqs, dks, dvs = vjp([dy_ref[:, h * d:(h + 1) * d] for h in range(MEM_HEADS)])
        dq_ref[...] = jnp.concatenate(dqs, axis=1).astype(dq_ref.dtype)

        @pl.when(pl.program_id(0) == 0)
        def _():
            dkv_ref[...] = jnp.zeros_like(dkv_ref)

        dkv_ref[...] += jnp.concatenate(dks + dvs, axis=1)

    return pl.pallas_call(
        body, grid=(s // t,),
        in_specs=[pl.BlockSpec((t, MEM_WIDTH), lambda i: (i, q_col // MEM_WIDTH)),
                  pl.BlockSpec((m, 2 * MEM_WIDTH), lambda i: (0, 0)),
                  pl.BlockSpec((t, MEM_WIDTH), lambda i: (i, MIX_WIDTH // MEM_WIDTH))],
        out_specs=[pl.BlockSpec((t, MEM_WIDTH), lambda i: (i, 0)), pl.BlockSpec((m, 2 * MEM_WIDTH), lambda i: (0, 0))],
        out_shape=[jax.ShapeDtypeStruct((s, MEM_WIDTH), BF16), jax.ShapeDtypeStruct((m, 2 * MEM_WIDTH), F32)],
        compiler_params=_cp(("arbitrary",)), name=name)(p, kv, dcat)


def _swa_fn(qs, kcs, kps, vcs, vps, sinks, not_first):
    t = SWA_BLOCK
    qi = lax.broadcasted_iota(jnp.int32, (t, 2 * t), 0)
    kj = lax.broadcasted_iota(jnp.int32, (t, 2 * t), 1)
    dist = t + qi - kj
    valid = (dist >= 0) & (dist < t) & ((kj >= t) | not_first)
    distf = dist.astype(F32)
    outs = []
    for kh in range(SWA_KV_HEADS):
        kb = jnp.concatenate([kps[kh], kcs[kh]], axis=0)
        vb = jnp.concatenate([vps[kh], vcs[kh]], axis=0)
        for g in range(SWA_GROUP):
            h = kh * SWA_GROUP + g
            slope = 2.0 ** (-8.0 * (h + 1) / SWA_Q_HEADS)
            sc = _dot_nt(qs[h], kb) * (SWA_HEAD_DIM ** -0.5) - slope * distf
            sc = jnp.where(valid, sc, -jnp.inf)
            outs.append(_dot(_softmax_rows(sc, extra=sinks[h]), vb))
    return outs


def _swa_args(q_ref, kc_ref, kp_ref, vc_ref, vp_ref, sink_ref):
    d = SWA_HEAD_DIM
    qs = [q_ref[:, h * d:(h + 1) * d] for h in range(SWA_Q_HEADS)]
    per_kv = lambda ref: [ref[:, h * d:(h + 1) * d] for h in range(SWA_KV_HEADS)]
    sinks = [sink_ref[0:1, h:h + 1] for h in range(SWA_Q_HEADS)]
    return qs, per_kv(kc_ref), per_kv(kp_ref), per_kv(vc_ref), per_kv(vp_ref), sinks


def _swa_specs(nb, order):
    t, kvw = SWA_BLOCK, SWA_KV_HEADS * SWA_HEAD_DIM
    k_col, v_col = MIX_WIDTH // kvw, MIX_WIDTH // kvw + 1
    q = pl.BlockSpec((t, MIX_WIDTH), lambda n: (order(n), 0))
    kc = pl.BlockSpec((t, kvw), lambda n: (order(n), k_col))
    kp = pl.BlockSpec((t, kvw), lambda n: (jnp.maximum(order(n) - 1, 0), k_col))
    vc = pl.BlockSpec((t, kvw), lambda n: (order(n), v_col))
    vp = pl.BlockSpec((t, kvw), lambda n: (jnp.maximum(order(n) - 1, 0), v_col))
    sink = pl.BlockSpec((1, LANES), lambda n: (0, 0))
    return [q, kc, kp, vc, vp, sink]


def _pad_lanes(v):
    return jnp.pad(v.reshape(1, -1), ((0, 0), (0, LANES - v.size)))


def swa_fwd(p, sinks, name):
    s = p.shape[0]
    nb = s // SWA_BLOCK

    def body(q_ref, kc_ref, kp_ref, vc_ref, vp_ref, sink_ref, o_ref):
        outs = _swa_fn(*_swa_args(q_ref, kc_ref, kp_ref, vc_ref, vp_ref, sink_ref), pl.program_id(0) > 0)
        o_ref[...] = jnp.concatenate(outs, axis=1).astype(o_ref.dtype)

    return pl.pallas_call(
        body, grid=(nb,), in_specs=_swa_specs(nb, lambda n: n),
        out_specs=pl.BlockSpec((SWA_BLOCK, MIX_WIDTH), lambda n: (n, 0)),
        out_shape=jax.ShapeDtypeStruct((s, MIX_WIDTH), BF16), compiler_params=_cp(("parallel",)),
        name=name)(p, p, p, p, p, _pad_lanes(sinks))


def swa_bwd(p, sinks, dcat, name):
    s = p.shape[0]
    nb = s // SWA_BLOCK
    t, d, kvw = SWA_BLOCK, SWA_HEAD_DIM, SWA_KV_HEADS * SWA_HEAD_DIM
    rev = lambda n: nb - 1 - n

    def body(q_ref, kc_ref, kp_ref, vc_ref, vp_ref, sink_ref, dy_ref, dq_ref, dk_ref, dv_ref, ds_ref, ck, cv):
        n = pl.program_id(0)

        @pl.when(n == 0)
        def _():
            ck[...] = jnp.zeros_like(ck)
            cv[...] = jnp.zeros_like(cv)
            ds_ref[...] = jnp.zeros_like(ds_ref)

        args = _swa_args(q_ref, kc_ref, kp_ref, vc_ref, vp_ref, sink_ref)
        _, vjp = jax.vjp(functools.partial(_swa_fn, not_first=rev(n) > 0), *args)
        dqs, dkcs, dkps, dvcs, dvps, dsinks = vjp([dy_ref[:, h * d:(h + 1) * d] for h in range(SWA_Q_HEADS)])
        dq_ref[...] = jnp.concatenate(dqs, axis=1).astype(dq_ref.dtype)
        dk_ref[...] = (jnp.concatenate(dkcs, axis=1) + ck[...]).astype(dk_ref.dtype)
        dv_ref[...] = (jnp.concatenate(dvcs, axis=1) + cv[...]).astype(dv_ref.dtype)
        ck[...] = jnp.concatenate(dkps, axis=1)
        cv[...] = jnp.concatenate(dvps, axis=1)
        lane = lax.broadcasted_iota(jnp.int32, (1, LANES), 1)
        acc = jnp.zeros((1, LANES), F32)
        for h in range(SWA_Q_HEADS):
            acc = acc + jnp.where(lane == h, dsinks[h], 0.0)
        ds_ref[...] += acc

    dy = pl.BlockSpec((t, MIX_WIDTH), lambda n: (rev(n), 0))
    kv_out = pl.BlockSpec((t, kvw), lambda n: (rev(n), 0))
    dq, dk, dv, ds = pl.pallas_call(
        body, grid=(nb,), in_specs=_swa_specs(nb, rev) + [dy],
        out_specs=[dy, kv_out, kv_out, pl.BlockSpec((1, LANES), lambda n: (0, 0))],
        out_shape=[jax.ShapeDtypeStruct((s, MIX_WIDTH), BF16), jax.ShapeDtypeStruct((s, kvw), BF16),
                   jax.ShapeDtypeStruct((s, kvw), BF16), jax.ShapeDtypeStruct((1, LANES), F32)],
        scratch_shapes=[pltpu.VMEM((t, kvw), F32), pltpu.VMEM((t, kvw), F32)],
        compiler_params=_cp(("arbitrary",)), name=name)(p, p, p, p, p, _pad_lanes(sinks), dcat)
    return dq, dk, dv, ds[0, :SWA_Q_HEADS]


RW_SHIFT = 5120
RW_R, RW_K, RW_V, RW_WD, RW_AD, RW_GD = 0, 1536, 3072, 4608, 4736, 4864


def _head_matrix(width, head_dim):
    e = (np.arange(width)[:, None] // head_dim == np.arange(LANES)[None, :]).astype(np.float32)
    return jnp.asarray(e), jnp.asarray(e.T)


def _rwkv_pre_fn(pieces, shifted, mus, w0, wdu, a0, wiu, wgu, k_k, k_a, e, et):
    r, k, v, wd, ad, gd = [p + (s - p) * mu for p, s, mu in zip(pieces, shifted, mus)]
    w_log = -_softplus(-(w0 + _dot(jnp.tanh(wd), wdu))) - 0.5
    lw = -jnp.exp(w_log)
    a = _sigmoid(a0 + _dot(ad, wiu))
    g = _dot(_sigmoid(gd), wgu)
    kkr = k * k_k
    kk = kkr * _dot(lax.rsqrt(_dot(kkr * kkr, e) + 1e-6), et)
    k2 = k * (1.0 + (a - 1.0) * k_a)
    return r, lw, k2, v, kk, kk * a, g


_RW_GROUPS = ((RW_R, MIX_WIDTH), (RW_K, MIX_WIDTH), (RW_V, MIX_WIDTH), (RW_WD, LANES), (RW_AD, LANES), (RW_GD, 2 * LANES))


def _rwkv_pre_inputs(p_ref, prev_ref, mu_ref, first):
    pieces = [p_ref[:, o:o + n] for o, n in _RW_GROUPS]
    shifted = [_shift_down(p_ref[:, o:o + n], prev_ref[:, o:o + n] * first, 1) for o, n in _RW_GROUPS]
    mus = [mu_ref[:, o:o + n] for o, n in _RW_GROUPS]
    return pieces, shifted, mus


def _rwkv_param_specs():
    vec = lambda n: pl.BlockSpec((1, n), lambda i: (0, 0))
    mat = lambda r, c: pl.BlockSpec((r, c), lambda i: (0, 0))
    return [vec(RW_SHIFT), vec(MIX_WIDTH), mat(LANES, MIX_WIDTH), vec(MIX_WIDTH), mat(LANES, MIX_WIDTH),
            mat(2 * LANES, MIX_WIDTH), vec(MIX_WIDTH), vec(MIX_WIDTH), mat(MIX_WIDTH, LANES), mat(LANES, MIX_WIDTH)]


def rwkv_pre_fwd(p, params, name):
    s = p.shape[0]
    t = _tile(s, (128, 64))

    def body(p_ref, prev_ref, mu_ref, *rest):
        prm, outs = rest[:9], rest[9:]
        first = (pl.program_id(0) > 0).astype(F32)
        pieces, shifted, mus = _rwkv_pre_inputs(p_ref, prev_ref, mu_ref, first)
        res = _rwkv_pre_fn(pieces, shifted, mus, *[q[...] for q in prm])
        for o_ref, val in zip(outs, res):
            o_ref[...] = val

    row = pl.BlockSpec((t, RW_SHIFT), lambda i: (i, 0))
    prev = pl.BlockSpec((SUBLANES, RW_SHIFT), lambda i: (jnp.maximum(i * (t // SUBLANES) - 1, 0), 0))
    out = pl.BlockSpec((t, MIX_WIDTH), lambda i: (i, 0))
    return pl.pallas_call(
        body, grid=(s // t,), in_specs=[row, prev] + _rwkv_param_specs(), out_specs=[out] * 7,
        out_shape=[jax.ShapeDtypeStruct((s, MIX_WIDTH), F32)] * 7, compiler_params=_cp(("parallel",)),
        name=name)(p, p, *params)


def rwkv_pre_bwd(p, params, cots, name):
    s = p.shape[0]
    t = _tile(s, (64, 32))

    def body(p_ref, prev_ref, mu_ref, *rest):
        prm, cot, outs = rest[:9], rest[9:19], rest[19:]
        dp_ref, dps_ref, grads = outs[0], outs[1], outs[2:]
        i = pl.program_id(0)
        first = (i > 0).astype(F32)
        pieces, shifted, mus = _rwkv_pre_inputs(p_ref, prev_ref, mu_ref, first)
        prm_v = [q[...] for q in prm]
        fn = lambda pieces, shifted, mus, *small: _rwkv_pre_fn(pieces, shifted, mus, *small, prm_v[7], prm_v[8])
        _, vjp = jax.vjp(fn, pieces, shifted, mus, *prm_v[:7])
        dr, dw, dk2, dv, dkk, db, dr2, dk22, dv2, dg = [c[...] for c in cot]
        res = vjp((dr + dr2, dw, dk2 + dk22, dv + dv2, dkk, db, dg))
        dpieces, dshifted, dmus, dsmall = res[0], res[1], res[2], res[3:]
        for (o, n), dpi, dsi in zip(_RW_GROUPS, dpieces, dshifted):
            dp_ref[:, o:o + n] = dpi
            dps_ref[:, o:o + n] = dsi

        @pl.when(i == 0)
        def _():
            for g_ref in grads:
                g_ref[...] = jnp.zeros_like(g_ref)

        for (o, n), dmu in zip(_RW_GROUPS, dmus):
            grads[0][:, o:o + n] += dmu
        for g_ref, dval in zip(grads[1:], dsmall):
            g_ref[...] += dval

    row = pl.BlockSpec((t, RW_SHIFT), lambda i: (i, 0))
    prev = pl.BlockSpec((SUBLANES, RW_SHIFT), lambda i: (jnp.maximum(i * (t // SUBLANES) - 1, 0), 0))
    act = pl.BlockSpec((t, MIX_WIDTH), lambda i: (i, 0))
    pspecs = _rwkv_param_specs()
    full = jax.ShapeDtypeStruct((s, RW_SHIFT), F32)
    gshapes = [jax.ShapeDtypeStruct(q.shape, F32) for q in params[:8]]
    return pl.pallas_call(
        body, grid=(s // t,), in_specs=[row, prev] + pspecs + [act] * 10, out_specs=[row, row] + pspecs[:8],
        out_shape=[full, full] + gshapes, compiler_params=_cp(("arbitrary",)), name=name)(p, p, *params, *cots)


def shift_add(a, b, js, out_dtype, name):
    s, c = a.shape
    t = _tile(s, (256, 128, 64))
    tc = _tile(c, (1024, 768, 512, 640, 384, 256, 128))
    nt, nb = s // t, len(b)

    def body(a_ref, *rest):
        b_refs, n_refs, o_ref = rest[:nb], rest[nb:2 * nb], rest[2 * nb]
        last = (pl.program_id(1) < nt - 1).astype(F32)
        acc = a_ref[...]
        for b_ref, n_ref, j in zip(b_refs, n_refs, js):
            acc = acc + _shift_up(b_ref[...], n_ref[...] * last, j)
        o_ref[...] = acc.astype(o_ref.dtype)

    tile = pl.BlockSpec((t, tc), lambda j, i: (i, j))
    _, nxt = _halo_specs(t, s, tc, lambda j: j)
    return pl.pallas_call(
        body, grid=(c // tc, nt), in_specs=[tile] * (1 + nb) + [nxt] * nb, out_specs=tile,
        out_shape=jax.ShapeDtypeStruct((s, c), out_dtype), compiler_params=_cp(("parallel", "parallel")),
        name=name)(a, *b, *b)


def _rwkv_post_fn(y, r, k2, v, g, gn_g, gn_b, r_k, e, et):
    n = RWKV_HEAD_DIM
    yc = y - _dot(_dot(y, e), et) * (1.0 / n)
    rstd = lax.rsqrt(_dot(yc * yc, e) * (1.0 / n) + RWKV_GN_EPS)
    yn = yc * _dot(rstd, et) * gn_g + gn_b
    bonus = _dot(_dot(r * k2 * r_k, e), et) * v
    return (yn + bonus) * g


def rwkv_post_fwd(acts, params, name):
    s = acts[0].shape[0]
    t = _tile(s, (256, 128))

    def body(*refs):
        vals = [q[...] for q in refs[:10]]
        refs[10][...] = _rwkv_post_fn(*vals).astype(refs[10].dtype)

    act = pl.BlockSpec((t, MIX_WIDTH), lambda i: (i, 0))
    vec = pl.BlockSpec((1, MIX_WIDTH), lambda i: (0, 0))
    mats = [pl.BlockSpec((MIX_WIDTH, LANES), lambda i: (0, 0)), pl.BlockSpec((LANES, MIX_WIDTH), lambda i: (0, 0))]
    return pl.pallas_call(
        body, grid=(s // t,), in_specs=[act] * 5 + [vec] * 3 + mats, out_specs=act,
        out_shape=jax.ShapeDtypeStruct((s, MIX_WIDTH), BF16), compiler_params=_cp(("parallel",)),
        name=name)(*acts, *params)


def rwkv_post_bwd(acts, params, dcat, name):
    s = acts[0].shape[0]
    t = _tile(s, (128, 64))

    def body(*refs):
        ins, dy_ref, outs = refs[:10], refs[10], refs[11:]
        vals = [q[...] for q in ins]
        fn = lambda *a: _rwkv_post_fn(*a, vals[8], vals[9])
        _, vjp = jax.vjp(fn, *vals[:8])
        res = vjp(dy_ref[...])
        for o_ref, val in zip(outs[:5], res[:5]):
            o_ref[...] = val

        @pl.when(pl.program_id(0) == 0)
        def _():
            for g_ref in outs[5:]:
                g_ref[...] = jnp.zeros_like(g_ref)

        for g_ref, val in zip(outs[5:], res[5:]):
            g_ref[...] += val

    act = pl.BlockSpec((t, MIX_WIDTH), lambda i: (i, 0))
    vec = pl.BlockSpec((1, MIX_WIDTH), lambda i: (0, 0))
    mats = [pl.BlockSpec((MIX_WIDTH, LANES), lambda i: (0, 0)), pl.BlockSpec((LANES, MIX_WIDTH), lambda i: (0, 0))]
    a_shape = jax.ShapeDtypeStruct((s, MIX_WIDTH), F32)
    v_shape = jax.ShapeDtypeStruct((1, MIX_WIDTH), F32)
    return pl.pallas_call(
        body, grid=(s // t,), in_specs=[act] * 5 + [vec] * 3 + mats + [act], out_specs=[act] * 5 + [vec] * 3,
        out_shape=[a_shape] * 5 + [v_shape] * 3, compiler_params=_cp(("arbitrary",)), name=name)(*acts, *params, dcat)


RW_CHUNK = 64


RW_HEADS_PER_STEP = 4


def _rwkv_chunk_fn(r, lw, k, v, kk, b, st):
    c = RW_CHUNK
    ri = lax.broadcasted_iota(jnp.int32, (c, c), 0)
    ci = lax.broadcasted_iota(jnp.int32, (c, c), 1)
    incl, strict = ri >= ci, ri > ci
    eye = (ri == ci).astype(F32)
    last_col = (ci == c - 1).astype(F32)
    last_row = (ri == c - 1).astype(F32)
    gc = _dot(incl.astype(F32), lw)
    a_t = -kk * jnp.exp(gc - lw)
    e_neg = jnp.exp(-gc)
    b_t, k_t, r_t = b * e_neg, k * e_neg, r * jnp.exp(gc)
    m_ab = jnp.where(strict, _dot_nt(a_t, b_t), 0.0)
    m_ak = jnp.where(strict, _dot_nt(a_t, k_t), 0.0)
    m_rb = jnp.where(incl, _dot_nt(r_t, b_t), 0.0)
    m_rk = jnp.where(incl, _dot_nt(r_t, k_t), 0.0)
    tinv, pw = eye + m_ab, m_ab
    for _ in range(5):
        pw = _dot(pw, pw)
        tinv = tinv + _dot(tinv, pw)
    u = _dot(tinv, _dot(a_t, st) + _dot(m_ak, v))
    y = _dot(r_t, st) + _dot(m_rb, u) + _dot(m_rk, v)
    dec = jnp.exp(_dot(last_col, gc) - gc)
    g_end = _dot_tn(gc, last_row)
    new_st = st * jnp.exp(g_end) + _dot_tn(b * dec, u) + _dot_tn(k * dec, v)
    return y, new_st


def rwkv_scan_fwd(r, lw, k, v, kk, b, name):
    s = r.shape[0]
    n, hp = RWKV_HEAD_DIM, RW_HEADS_PER_STEP
    nchunk, width = s // RW_CHUNK, RWKV_HEAD_DIM * RW_HEADS_PER_STEP

    def body(r_ref, w_ref, k_ref, v_ref, kk_ref, b_ref, y_ref, ck_ref, carry):
        @pl.when(pl.program_id(1) == 0)
        def _():
            carry[...] = jnp.zeros_like(carry)

        ck_ref[0] = carry[...]
        ys, sts = [], []
        for h in range(hp):
            cols = slice(h * n, (h + 1) * n)
            y, st = _rwkv_chunk_fn(*[q[:, cols] for q in (r_ref, w_ref, k_ref, v_ref, kk_ref, b_ref)], carry[:, cols])
            ys.append(y)
            sts.append(st)
        y_ref[...] = jnp.concatenate(ys, axis=1)
        carry[...] = jnp.concatenate(sts, axis=1)

    blk = pl.BlockSpec((RW_CHUNK, width), lambda j, c: (c, j))
    return pl.pallas_call(
        body, grid=(MIX_WIDTH // width, nchunk), in_specs=[blk] * 6,
        out_specs=[blk, pl.BlockSpec((1, n, width), lambda j, c: (c, 0, j))],
        out_shape=[jax.ShapeDtypeStruct((s, MIX_WIDTH), F32), jax.ShapeDtypeStruct((nchunk, n, MIX_WIDTH), F32)],
        scratch_shapes=[pltpu.VMEM((n, width), F32)],
        compiler_params=_cp(("parallel", "arbitrary")), name=name)(r, lw, k, v, kk, b)


def rwkv_scan_bwd(r, lw, k, v, kk, b, ck, dy, name):
    s = r.shape[0]
    n, hp = RWKV_HEAD_DIM, RW_HEADS_PER_STEP
    nchunk, width = s // RW_CHUNK, RWKV_HEAD_DIM * RW_HEADS_PER_STEP
    rev = lambda c: nchunk - 1 - c

    def body(r_ref, w_ref, k_ref, v_ref, kk_ref, b_ref, ck_ref, dy_ref, *rest):
        outs, carry = rest[:6], rest[6]

        @pl.when(pl.program_id(1) == 0)
        def _():
            carry[...] = jnp.zeros_like(carry)

        grads = []
        for h in range(hp):
            cols = slice(h * n, (h + 1) * n)
            args = [q[:, cols] for q in (r_ref, w_ref, k_ref, v_ref, kk_ref, b_ref)] + [ck_ref[0, :, cols]]
            _, vjp = jax.vjp(_rwkv_chunk_fn, *args)
            grads.append(vjp((dy_ref[:, cols], carry[:, cols])))
        for q in range(6):
            outs[q][...] = jnp.concatenate([g[q] for g in grads], axis=1)
        carry[...] = jnp.concatenate([g[6] for g in grads], axis=1)

    blk = pl.BlockSpec((RW_CHUNK, width), lambda j, c: (rev(c), j))
    out = jax.ShapeDtypeStruct((s, MIX_WIDTH), F32)
    return pl.pallas_call(
        body, grid=(MIX_WIDTH // width, nchunk),
        in_specs=[blk] * 6 + [pl.BlockSpec((1, n, width), lambda j, c: (rev(c), 0, j)), blk],
        out_specs=[blk] * 6, out_shape=[out] * 6, scratch_shapes=[pltpu.VMEM((n, width), F32)],
        compiler_params=_cp(("parallel", "arbitrary")), name=name)(r, lw, k, v, kk, b, ck, dy)


GD_Q, GD_K, GD_V, GD_Z, GD_QMEM, GD_BT, GD_AT, GD_COLS = 0, 768, 1536, 3072, 4608, 5120, 5248, 5376
_GD_GROUPS = ((GD_Q, GDN_QK_WIDTH), (GD_K, GDN_QK_WIDTH), (GD_V, MIX_WIDTH))


def _gdn_pre_fn(xs, convs, bt, at, a_log, dt_bias, e6, e6t, ebc):
    k_w = GDN_CONV
    acts = [_silu(sum(convs[g][j] * xs[g][k_w - 1 - j] for j in range(k_w))) for g in range(3)]
    l2 = lambda x: x * _dot(lax.rsqrt(_dot(x * x, e6) + 1e-6), e6t)
    beta = _sigmoid(bt)
    g = -jnp.exp(a_log) * _softplus(at + dt_bias)
    return l2(acts[0]), l2(acts[1]), acts[2], _dot(g, ebc), _dot(beta, ebc)


def _gdn_pre_inputs(x_ref, prev_ref, conv_ref, first):
    xs = [[_shift_down(x_ref[:, o:o + n], prev_ref[:, o:o + n] * first, j) for j in range(GDN_CONV)]
          for o, n in _GD_GROUPS]
    convs = [[conv_ref[j:j + 1, o:o + n] for j in range(GDN_CONV)] for o, n in _GD_GROUPS]
    return xs, convs


def _gdn_pre_specs(t):
    x = pl.BlockSpec((t, GDN_CONV_WIDTH), lambda i: (i, 0))
    prev = pl.BlockSpec((SUBLANES, GDN_CONV_WIDTH), lambda i: (jnp.maximum(i * (t // SUBLANES) - 1, 0), 0))
    bta = pl.BlockSpec((t, 2 * LANES), lambda i: (i, GD_BT // (2 * LANES)))
    conv = pl.BlockSpec((GDN_CONV, GDN_CONV_WIDTH), lambda i: (0, 0))
    vec = pl.BlockSpec((1, LANES), lambda i: (0, 0))
    mats = [pl.BlockSpec((GDN_QK_WIDTH, LANES), lambda i: (0, 0)), pl.BlockSpec((LANES, GDN_QK_WIDTH), lambda i: (0, 0)),
            pl.BlockSpec((LANES, MIX_WIDTH), lambda i: (0, 0))]
    return [x, prev, bta, conv, vec, vec] + mats


def gdn_pre_fwd(p, params, name):
    s = p.shape[0]
    t = _tile(s, (128, 64))

    def body(x_ref, prev_ref, bta_ref, conv_ref, al_ref, dt_ref, e6_ref, e6t_ref, ebc_ref, *outs):
        first = (pl.program_id(0) > 0).astype(F32)
        xs, convs = _gdn_pre_inputs(x_ref, prev_ref, conv_ref, first)
        res = _gdn_pre_fn(xs, convs, bta_ref[:, :LANES], bta_ref[:, LANES:], al_ref[...], dt_ref[...],
                          e6_ref[...], e6t_ref[...], ebc_ref[...])
        for o_ref, val in zip(outs, res):
            o_ref[...] = val

    qk = pl.BlockSpec((t, GDN_QK_WIDTH), lambda i: (i, 0))
    wide = pl.BlockSpec((t, MIX_WIDTH), lambda i: (i, 0))
    qk_s, wide_s = jax.ShapeDtypeStruct((s, GDN_QK_WIDTH), F32), jax.ShapeDtypeStruct((s, MIX_WIDTH), F32)
    return pl.pallas_call(
        body, grid=(s // t,), in_specs=_gdn_pre_specs(t), out_specs=[qk, qk, wide, wide, wide],
        out_shape=[qk_s, qk_s, wide_s, wide_s, wide_s], compiler_params=_cp(("parallel",)), name=name)(p, p, p, *params)


def gdn_pre_bwd(p, params, cots, name):
    s = p.shape[0]
    t = _tile(s, (64, 32))

    def body(x_ref, prev_ref, bta_ref, conv_ref, al_ref, dt_ref, e6_ref, e6t_ref, ebc_ref, *rest):
        cot, outs = rest[:5], rest[5:]
        dxs, dbta_ref, dconv_ref, dal_ref, ddt_ref = outs[:4], outs[4], outs[5], outs[6], outs[7]
        i = pl.program_id(0)
        first = (i > 0).astype(F32)
        xs, convs = _gdn_pre_inputs(x_ref, prev_ref, conv_ref, first)
        mats = (e6_ref[...], e6t_ref[...], ebc_ref[...])
        fn = lambda xs, convs, bt, at, al, dt: _gdn_pre_fn(xs, convs, bt, at, al, dt, *mats)
        _, vjp = jax.vjp(fn, xs, convs, bta_ref[:, :LANES], bta_ref[:, LANES:], al_ref[...], dt_ref[...])
        d_xs, d_convs, d_bt, d_at, d_al, d_dt = vjp(tuple(c[...] for c in cot))
        for g, (o, n) in enumerate(_GD_GROUPS):
            for j in range(GDN_CONV):
                dxs[j][:, o:o + n] = d_xs[g][j]
        dbta_ref[...] = jnp.concatenate([d_bt, d_at], axis=1).astype(dbta_ref.dtype)

        @pl.when(i == 0)
        def _():
            dconv_ref[...] = jnp.zeros_like(dconv_ref)
            dal_ref[...] = jnp.zeros_like(dal_ref)
            ddt_ref[...] = jnp.zeros_like(ddt_ref)

        for g, (o, n) in enumerate(_GD_GROUPS):
            for j in range(GDN_CONV):
                dconv_ref[j:j + 1, o:o + n] += d_convs[g][j]
        dal_ref[...] += d_al
        ddt_ref[...] += d_dt

    specs = _gdn_pre_specs(t)
    qk = pl.BlockSpec((t, GDN_QK_WIDTH), lambda i: (i, 0))
    wide = pl.BlockSpec((t, MIX_WIDTH), lambda i: (i, 0))
    x_s = jax.ShapeDtypeStruct((s, GDN_CONV_WIDTH), F32)
    vec_s = jax.ShapeDtypeStruct((1, LANES), F32)
    return pl.pallas_call(
        body, grid=(s // t,), in_specs=specs + [qk, qk, wide, wide, wide],
        out_specs=[specs[0]] * 4 + [pl.BlockSpec((t, 2 * LANES), lambda i: (i, 0)), specs[3], specs[4], specs[5]],
        out_shape=[x_s] * 4 + [jax.ShapeDtypeStruct((s, 2 * LANES), BF16),
                               jax.ShapeDtypeStruct((GDN_CONV, GDN_CONV_WIDTH), F32), vec_s, vec_s],
        compiler_params=_cp(("arbitrary",)), name=name)(p, p, p, *params, *cots)


def _gdn_post_fn(o, z, norm_g, e12, e12t, trep):
    rstd = lax.rsqrt(_dot(o * o, e12) * (1.0 / GDN_HEAD_DIM) + NORM_EPS)
    return o * _dot(rstd, e12t) * _dot(norm_g, trep) * _silu(z)


def _gdn_post_specs(t):
    act = pl.BlockSpec((t, MIX_WIDTH), lambda i: (i, 0))
    z = pl.BlockSpec((t, MIX_WIDTH), lambda i: (i, GD_Z // MIX_WIDTH))
    mats = [pl.BlockSpec((SUBLANES, LANES), lambda i: (0, 0)), pl.BlockSpec((MIX_WIDTH, LANES), lambda i: (0, 0)),
            pl.BlockSpec((LANES, MIX_WIDTH), lambda i: (0, 0)), pl.BlockSpec((LANES, MIX_WIDTH), lambda i: (0, 0))]
    return [act, z] + mats


def gdn_post_fwd(o, p, params, name):
    s = o.shape[0]
    t = _tile(s, (256, 128))

    def body(o_ref, z_ref, ng_ref, e_ref, et_ref, tr_ref, out_ref):
        res = _gdn_post_fn(o_ref[...], z_ref[...], ng_ref[0:1, :], e_ref[...], et_ref[...], tr_ref[...])
        out_ref[...] = res.astype(out_ref.dtype)

    act = pl.BlockSpec((t, MIX_WIDTH), lambda i: (i, 0))
    return pl.pallas_call(
        body, grid=(s // t,), in_specs=_gdn_post_specs(t), out_specs=act,
        out_shape=jax.ShapeDtypeStruct((s, MIX_WIDTH), BF16), compiler_params=_cp(("parallel",)),
        name=name)(o, p, *params)


def gdn_post_bwd(o, p, params, dcat, name):
    s = o.shape[0]
    t = _tile(s, (128, 64))

    def body(o_ref, z_ref, ng_ref, e_ref, et_ref, tr_ref, dy_ref, do_ref, dz_ref, dng_ref):
        mats = (e_ref[...], et_ref[...], tr_ref[...])
        fn = lambda o, z, ng: _gdn_post_fn(o, z, ng, *mats)
        _, vjp = jax.vjp(fn, o_ref[...], z_ref[...], ng_ref[0:1, :])
        d_o, d_z, d_ng = vjp(dy_ref[...])
        do_ref[...] = d_o
        dz_ref[...] = d_z.astype(dz_ref.dtype)

        @pl.when(pl.program_id(0) == 0)
        def _():
            dng_ref[...] = jnp.zeros_like(dng_ref)

        dng_ref[...] += d_ng

    act = pl.BlockSpec((t, MIX_WIDTH), lambda i: (i, 0))
    return pl.pallas_call(
        body, grid=(s // t,), in_specs=_gdn_post_specs(t) + [act],
        out_specs=[act, act, pl.BlockSpec((1, LANES), lambda i: (0, 0))],
        out_shape=[jax.ShapeDtypeStruct((s, MIX_WIDTH), F32), jax.ShapeDtypeStruct((s, MIX_WIDTH), BF16),
                   jax.ShapeDtypeStruct((1, LANES), F32)],
        compiler_params=_cp(("arbitrary",)), name=name)(o, p, *params, dcat)


def _gdn_chunk_fn(q, k, v, gb, bb, gb64, state):
    c = GDN_CHUNK
    ri = lax.broadcasted_iota(jnp.int32, (c, c), 0)
    ci = lax.broadcasted_iota(jnp.int32, (c, c), 1)
    causal, strict = ri >= ci, ri > ci
    ltri = causal.astype(F32)
    eye = (ri == ci).astype(F32)
    first_col = (ci == 0).astype(F32)
    last_col = (ci == c - 1).astype(F32)
    last_col_tall = (lax.broadcasted_iota(jnp.int32, (GDN_HEAD_DIM, c), 1) == c - 1).astype(F32)

    qs = q * (GDN_HEAD_DIM ** -0.5)
    gc = _dot(ltri, gb)
    gd = _dot(ltri, gb64)
    diff = gd - _dot_nt(first_col, gd)
    decay = jnp.exp(jnp.where(causal, diff, -jnp.inf))
    kb = k * bb
    lmat = jnp.where(strict, _dot_nt(kb, k) * decay, 0.0)
    tmat, pw = eye - lmat, lmat
    for _ in range(5):
        pw = _dot(pw, pw)
        tmat = tmat + _dot(tmat, pw)
    eg = jnp.exp(gc)
    u = _dot(tmat, v * bb)
    w = _dot(tmat, kb * eg)
    a_qk = jnp.where(causal, _dot_nt(qs, k) * decay, 0.0)
    g_last = _dot(last_col, gc)
    k_dec = k * jnp.exp(g_last - gc)
    v_new = u - _dot(w, state)
    out = _dot(qs * eg, state) + _dot(a_qk, v_new)
    new_state = state * jnp.exp(_dot(last_col_tall, gc)) + _dot_tn(k_dec, v_new)
    return out, new_state


GDN_REP = GDN_V_HEADS // GDN_QK_HEADS


def _gdn_chunk_specs(order):
    c, d = GDN_CHUNK, GDN_HEAD_DIM
    qk = pl.BlockSpec((c, d), lambda j, n: (order(n), j))
    vh = pl.BlockSpec((c, GDN_REP * d), lambda j, n: (order(n), j))
    st = pl.BlockSpec((GDN_REP, 1, d, d), lambda j, n: (j, order(n), 0, 0))
    return qk, vh, st


def gdn_chunk_fwd(q, k, v, gb, bb, name):
    s = q.shape[0]
    nc, d = s // GDN_CHUNK, GDN_HEAD_DIM

    def body(q_ref, k_ref, v_ref, gb_ref, bb_ref, o_ref, st_ref, carry):
        @pl.when(pl.program_id(1) == 0)
        def _():
            carry[...] = jnp.zeros_like(carry)

        for rep in range(GDN_REP):
            cols = slice(rep * d, (rep + 1) * d)
            state = carry[rep]
            st_ref[rep, 0] = state
            out, new_state = _gdn_chunk_fn(q_ref[...], k_ref[...], v_ref[:, cols], gb_ref[:, cols], bb_ref[:, cols],
                                           gb_ref[:, rep * d:rep * d + GDN_CHUNK], state)
            o_ref[:, cols] = out
            carry[rep] = new_state

    qk, vh, st = _gdn_chunk_specs(lambda n: n)
    return pl.pallas_call(
        body, grid=(GDN_QK_HEADS, nc), in_specs=[qk, qk, vh, vh, vh], out_specs=[vh, st],
        out_shape=[jax.ShapeDtypeStruct((s, MIX_WIDTH), F32), jax.ShapeDtypeStruct((GDN_V_HEADS, nc, d, d), F32)],
        scratch_shapes=[pltpu.VMEM((GDN_REP, d, d), F32)],
        compiler_params=_cp(("parallel", "arbitrary")), name=name)(q, k, v, gb, bb)


def gdn_chunk_bwd(q, k, v, gb, bb, states, do, name):
    s = q.shape[0]
    nc, d = s // GDN_CHUNK, GDN_HEAD_DIM
    rev = lambda n: nc - 1 - n

    def body(q_ref, k_ref, v_ref, gb_ref, bb_ref, st_ref, do_ref, dq_ref, dk_ref, dv_ref, dg_ref, db_ref, carry):
        @pl.when(pl.program_id(1) == 0)
        def _():
            carry[...] = jnp.zeros_like(carry)

        d_qs, d_ks = [], []
        for rep in range(GDN_REP):
            cols = slice(rep * d, (rep + 1) * d)
            lead = slice(rep * d, rep * d + GDN_CHUNK)
            args = (q_ref[...], k_ref[...], v_ref[:, cols], gb_ref[:, cols], bb_ref[:, cols], gb_ref[:, lead],
                    st_ref[rep, 0])
            _, vjp = jax.vjp(_gdn_chunk_fn, *args)
            d_q, d_k, d_v, d_gb, d_bb, d_gb64, d_state = vjp((do_ref[:, cols], carry[rep]))
            carry[rep] = d_state
            dv_ref[:, cols] = d_v
            db_ref[:, cols] = d_bb
            dg_ref[:, cols] = d_gb
            dg_ref[:, lead] += d_gb64
            d_qs.append(d_q)
            d_ks.append(d_k)
        dq_ref[...] = sum(d_qs[1:], d_qs[0])
        dk_ref[...] = sum(d_ks[1:], d_ks[0])

    qk, vh, st = _gdn_chunk_specs(rev)
    qk_s, wide_s = jax.ShapeDtypeStruct((s, GDN_QK_WIDTH), F32), jax.ShapeDtypeStruct((s, MIX_WIDTH), F32)
    return pl.pallas_call(
        body, grid=(GDN_QK_HEADS, nc), in_specs=[qk, qk, vh, vh, vh, st, vh], out_specs=[qk, qk, vh, vh, vh],
        out_shape=[qk_s, qk_s, wide_s, wide_s, wide_s], scratch_shapes=[pltpu.VMEM((GDN_REP, d, d), F32)],
        compiler_params=_cp(("parallel", "arbitrary")), name=name)(q, k, v, gb, bb, states, do)


WEIGHTS = ['attn_norm', 'mem_norm', 'w_mem_kv', 'w_out', 'ffn_norm', 'w_ffn_up', 'ffn_conv', 'w_ffn_down', 'final_norm',
           'a_w_in', 'a_sinks', 'b_w_in', 'b_mu', 'b_w0', 'b_w_decay_up', 'b_a0', 'b_w_iclr_up', 'b_w_gate_up', 'b_k_k',
           'b_k_a', 'b_r_k', 'b_gn_g', 'b_gn_b', 'c_w_in', 'c_conv', 'c_a_log', 'c_dt_bias', 'c_norm_g']
INPUTS = ['x', 'mem'] + WEIGHTS + ['loss_target'] + ['m_' + n for n in WEIGHTS] + ['v_' + n for n in WEIGHTS]
REPLICATED = ['attn_norm', 'mem_norm', 'ffn_norm', 'final_norm', 'a_sinks', 'b_mu', 'b_w0', 'b_a0', 'b_k_k', 'b_k_a',
              'b_r_k', 'b_gn_g', 'b_gn_b', 'c_a_log', 'c_dt_bias', 'c_norm_g']
C_MIX = GDN_CONV_WIDTH + MIX_WIDTH
GATHER_ID, EXCHANGE_ID = 1, 1 + DEPTH


def _cols_to_shards(full):
    rows, cols = full.shape
    return full.reshape(rows, N_DEV, cols // N_DEV).transpose(1, 0, 2)


def _shards_to_cols(g):
    return g.transpose(1, 0, 2).reshape(g.shape[1], N_DEV * g.shape[2])


def _pad_to(x, n, axis):
    pad = [(0, 0)] * x.ndim
    pad[axis] = (0, n - x.shape[axis])
    return jnp.pad(x, pad)


def _b_pad_cols(w):
    parts = [w[..., :4608], _pad_to(w[..., 4608:4704], LANES, -1), _pad_to(w[..., 4704:4800], LANES, -1), w[..., 4800:5056]]
    if w.shape[-1] > 5056:
        parts.append(w[..., 5056:])
    return jnp.concatenate(parts, axis=-1)


def _b_unpad_cols(w):
    parts = [w[..., :4608], w[..., RW_WD:RW_WD + RWKV_DECAY_RANK], w[..., RW_AD:RW_AD + RWKV_ICLR_RANK], w[..., RW_GD:RW_SHIFT]]
    if w.shape[-1] > RW_SHIFT:
        parts.append(w[..., RW_SHIFT:])
    return jnp.concatenate(parts, axis=-1)


def _c_pad_cols(w):
    return jnp.concatenate([w[..., :C_MIX], w[..., C_MIX + 24:], _pad_to(w[..., C_MIX:C_MIX + 12], LANES, -1),
                            _pad_to(w[..., C_MIX + 12:C_MIX + 24], LANES, -1)], axis=-1)


def _c_unpad_cols(w):
    return jnp.concatenate([w[..., :C_MIX], w[..., GD_BT:GD_BT + GDN_V_HEADS], w[..., GD_AT:GD_AT + GDN_V_HEADS],
                            w[..., GD_QMEM:GD_BT]], axis=-1)


def _pack(arrays):
    flat = jnp.concatenate([a.reshape(-1).astype(F32) for a in arrays])
    unit = SUBLANES * LANES
    return _pad_to(flat, -(-flat.size // unit) * unit, 0).reshape(-1, LANES)


def _unpack(packed, shapes):
    flat, out, at = packed.reshape(-1), [], 0
    for shp in shapes:
        n = int(np.prod(shp))
        out.append(flat[at:at + n].reshape(shp))
        at += n
    return out


def kernel(*args):
    a = dict(zip(INPUTS, args))
    x0, mem, target = a['x'][0], a['mem'][0], a['loss_target'][0]
    s = x0.shape[0]
    e64, e64t = _head_matrix(MIX_WIDTH, RWKV_HEAD_DIM)
    e6, e6t = _head_matrix(GDN_QK_WIDTH, GDN_HEAD_DIM)
    e12, e12t = _head_matrix(MIX_WIDTH, GDN_HEAD_DIM)
    trep = jnp.asarray((np.arange(LANES)[:, None] == np.arange(MIX_WIDTH)[None, :] % LANES).astype(np.float32))
    row = lambda v: v.reshape(1, -1)

    def in_proj_shard(l):
        kind, j = l % 3, l // 3
        return (a['a_w_in'], a['b_w_in'], a['c_w_in'])[kind][j]

    def small_shards(l):
        kind, j = l % 3, l // 3
        if kind == 1:
            return [a['b_w_decay_up'][j], a['b_w_iclr_up'][j], a['b_w_gate_up'][j]]
        if kind == 2:
            return [a['c_conv'][j]]
        return []

    gathered = []
    for l in range(DEPTH):
        big = [a['w_mem_kv'][l], a['w_out'][l], a['w_ffn_up'][l], a['w_ffn_down'][l], in_proj_shard(l)]
        shards = [w.astype(BF16) for w in big] + [a['ffn_conv'][l]] + small_shards(l)
        if gathered:
            shards, _ = lax.optimization_barrier((shards, gathered[-1]))
        gathered.append(all_gather_many_async(shards, f"gather_weights_{l}", GATHER_ID + l))

    def layer_weights(l, g):
        kind = l % 3
        w_in = _shards_to_cols(g[4])
        lw = dict(w_kv=g[0].reshape(D_MODEL, 2 * MEM_WIDTH), w_out=g[1].reshape(D_MODEL, D_MODEL),
                  w_up=_shards_to_cols(g[2]), w_down=g[3].reshape(D_FF, D_MODEL), conv=_shards_to_cols(g[5]))
        if kind == 0:
            lw['w_in'] = w_in
        elif kind == 1:
            lw['w_in'] = _b_pad_cols(w_in)
            lw['wdu'] = _pad_to(_shards_to_cols(g[6]), LANES, 0)
            lw['wiu'] = _pad_to(_shards_to_cols(g[7]), LANES, 0)
            lw['wgu'] = _shards_to_cols(g[8])
        else:
            lw['w_in'] = _c_pad_cols(w_in)
            lw['c_conv'] = _shards_to_cols(g[6])
        return lw

    def rwkv_params(j, lw):
        return (row(_b_pad_cols(a['b_mu'][j])), row(a['b_w0'][j]), lw['wdu'], row(a['b_a0'][j]), lw['wiu'], lw['wgu'],
                row(a['b_k_k'][j]), row(a['b_k_a'][j]), e64, e64t)

    def rwkv_post_params(j):
        return (row(a['b_gn_g'][j]), row(a['b_gn_b'][j]), row(a['b_r_k'][j]), e64, e64t)

    def gdn_params(j, lw):
        return (lw['c_conv'], _pad_lanes(a['c_a_log'][j]), _pad_lanes(a['c_dt_bias'][j]), e6, e6t, e12t)

    def gdn_post_params(j):
        return (jnp.tile(row(a['c_norm_g'][j]), (SUBLANES, 1)), e12, e12t, trep)

    x = x0
    saved, layers = [], []
    for l in range(DEPTH):
        kind, j = l % 3, l // 3
        g = gathered[l]
        if l > 0:
            x, g = lax.optimization_barrier((x, g))
        lw = layer_weights(l, g)
        layers.append(lw)
        sv = dict(x=x)
        h = rmsnorm_fwd(x, a['attn_norm'][l], BF16, f"attn_norm_{l}")
        memn = rmsnorm_fwd(mem, a['mem_norm'][l], BF16, f"mem_norm_{l}")
        mem_kv = mm(memn, lw['w_kv'], name=f"mem_kv_{l}")
        p = mm(h, lw['w_in'], name=f"in_proj_{l}")
        if kind == 0:
            y = swa_fwd(p, a['a_sinks'][j], f"swa_{l}")
            q_col = MIX_WIDTH + 2 * SWA_KV_HEADS * SWA_HEAD_DIM
        elif kind == 1:
            pre = rwkv_pre_fwd(p, rwkv_params(j, lw), f"rwkv_pre_{l}")
            yscan, ck = rwkv_scan_fwd(*pre[:6], f"rwkv_scan_{l}")
            post_in = (yscan, pre[0], pre[2], pre[3], pre[6])
            y = rwkv_post_fwd(post_in, rwkv_post_params(j), f"rwkv_post_{l}")
            sv.update(pre=pre, ck=ck, post_in=post_in)
            q_col = RW_SHIFT
        else:
            pre = gdn_pre_fwd(p, gdn_params(j, lw), f"gdn_pre_{l}")
            o, states = gdn_chunk_fwd(*pre, f"gdn_chunk_{l}")
            y = gdn_post_fwd(o, p, gdn_post_params(j), f"gdn_post_{l}")
            sv.update(pre=pre, o=o, states=states)
            q_col = GD_QMEM
        y_mem = mem_attn_fwd(p, q_col, mem_kv, f"mem_attn_{l}")
        cat = jnp.concatenate([y, y_mem], axis=1)
        x1 = mm(cat, lw['w_out'], res=x, name=f"out_proj_{l}")
        hf = rmsnorm_fwd(x1, a['ffn_norm'][l], BF16, f"ffn_norm_{l}")
        u0 = mm(hf, lw['w_up'], name=f"ffn_up_{l}")
        act = ffn_act_fwd(u0, lw['conv'], f"ffn_act_{l}")
        x = mm(act, lw['w_down'], res=x1, name=f"ffn_down_{l}")
        sv.update(h=h, memn=memn, mem_kv=mem_kv, p=p, q_col=q_col, cat=cat, x1=x1, hf=hf, u0=u0, act=act)
        saved.append(sv)

    loss_part, dx, d_final_norm = final_loss(x, a['final_norm'], target, "final_loss")

    rep_grads = {n: [None] * a[n].shape[0] for n in ('attn_norm', 'mem_norm', 'ffn_norm', 'a_sinks')}
    rep_grads['final_norm'] = d_final_norm
    results = {}
    exchanged = {}

    def apply_adam(name, idx, pieces, tag):
        w, m, v = a[name][idx], a['m_' + name][idx], a['v_' + name][idx]
        shp = w.shape
        two_d = (-1, shp[-1])
        out = adamw_sum(pieces.reshape((N_DEV,) + w.reshape(two_d).shape), w.reshape(two_d), m.reshape(two_d),
                        v.reshape(two_d), f"adamw_{name}_{tag}")
        results.setdefault(name, {})[idx] = [o.reshape(shp) for o in out]

    for l in reversed(range(DEPTH)):
        kind, j = l % 3, l // 3
        lw, sv = layers[l], saved[l]
        p, q_col = sv['p'], sv['q_col']
        d_act = mm(dx, lw['w_down'], tb=True, name=f"d_ffn_act_{l}")
        dw_down = mm(sv['act'], dx, ta=True, out_dtype=BF16, name=f"dw_ffn_down_{l}")
        dug, duv, dcg, dcv = ffn_act_bwd(sv['u0'], lw['conv'], d_act, f"ffn_act_bwd_{l}")
        du0 = jnp.concatenate([dug, duv], axis=1)
        d_conv = jnp.concatenate([dcg, dcv], axis=1)
        d_hf = mm(du0, lw['w_up'], tb=True, name=f"d_ffn_norm_out_{l}")
        dw_up = mm(sv['hf'], du0, ta=True, out_dtype=BF16, name=f"dw_ffn_up_{l}")
        dx1, rep_grads['ffn_norm'][l] = rmsnorm_bwd(sv['x1'], a['ffn_norm'][l], d_hf, dx, f"ffn_norm_bwd_{l}")
        dcat = mm(dx1, lw['w_out'], tb=True, name=f"d_cat_{l}")
        dw_out = mm(sv['cat'], dx1, ta=True, out_dtype=BF16, name=f"dw_out_{l}")
        dq_mem, d_mem_kv = mem_attn_bwd(p, q_col, sv['mem_kv'], dcat, f"mem_attn_bwd_{l}")
        small_grads = []
        if kind == 0:
            dq, dk, dv, rep_grads['a_sinks'][j] = swa_bwd(p, a['a_sinks'][j], dcat, f"swa_bwd_{l}")
            dp = jnp.concatenate([dq, dk, dv, dq_mem], axis=1)
        elif kind == 1:
            post = rwkv_post_bwd(sv['post_in'], rwkv_post_params(j), dcat, f"rwkv_post_bwd_{l}")
            scan = rwkv_scan_bwd(*sv['pre'][:6], sv['ck'], post[0], f"rwkv_scan_bwd_{l}")
            res = rwkv_pre_bwd(p, rwkv_params(j, lw), tuple(scan) + tuple(post[1:5]), f"rwkv_pre_bwd_{l}")
            dp_mix = shift_add(res[0], [res[1]], [1], BF16, f"rwkv_shift_bwd_{l}")
            dp = jnp.concatenate([dp_mix, dq_mem], axis=1)
            for n, val in zip(('b_mu', 'b_w0', 'b_a0', 'b_k_k', 'b_k_a'), (_b_unpad_cols(res[2]), res[3], res[5], res[8], res[9])):
                rep_grads[n] = val
            rep_grads.update(b_gn_g=post[5], b_gn_b=post[6], b_r_k=post[7])
            small_grads = [_cols_to_shards(res[4][:RWKV_DECAY_RANK]), _cols_to_shards(res[6][:RWKV_ICLR_RANK]),
                           _cols_to_shards(res[7])]
        else:
            d_o, dz, rep_grads['c_norm_g'] = gdn_post_bwd(sv['o'], p, gdn_post_params(j), dcat, f"gdn_post_bwd_{l}")
            chunk = gdn_chunk_bwd(*sv['pre'], sv['states'], d_o, f"gdn_chunk_bwd_{l}")
            res = gdn_pre_bwd(p, gdn_params(j, lw), chunk, f"gdn_pre_bwd_{l}")
            dqkv = shift_add(res[0], list(res[1:4]), [1, 2, 3], BF16, f"gdn_shift_bwd_{l}")
            dp = jnp.concatenate([dqkv, dz, dq_mem, res[4]], axis=1)
            rep_grads.update(c_a_log=res[6][:, :GDN_V_HEADS], c_dt_bias=res[7][:, :GDN_V_HEADS])
            small_grads = [_cols_to_shards(res[5])]
        d_h = mm(dp, lw['w_in'], tb=True, name=f"d_attn_norm_out_{l}")
        dw_in = mm(sv['h'], dp, ta=True, out_dtype=BF16, name=f"dw_in_{l}")
        dx, rep_grads['attn_norm'][l] = rmsnorm_bwd(sv['x'], a['attn_norm'][l], d_h, dx1, f"attn_norm_bwd_{l}")
        d_memn = mm(d_mem_kv, lw['w_kv'], tb=True, name=f"d_mem_norm_out_{l}")
        dw_kv = mm(sv['memn'], d_mem_kv, ta=True, out_dtype=BF16, name=f"dw_mem_kv_{l}")
        _, rep_grads['mem_norm'][l] = rmsnorm_bwd(mem, a['mem_norm'][l], d_memn, None, f"mem_norm_bwd_{l}")

        if kind == 1:
            dw_in = _b_unpad_cols(dw_in)
        elif kind == 2:
            dw_in = _c_unpad_cols(dw_in)
        pieces = [dw_kv.reshape(N_DEV, -1, 2 * MEM_WIDTH), dw_out.reshape(N_DEV, -1, D_MODEL), _cols_to_shards(dw_up),
                  dw_down.reshape(N_DEV, -1, D_MODEL), _cols_to_shards(dw_in), _cols_to_shards(d_conv)] + small_grads
        if l + 1 < DEPTH:
            exchanged[l + 1], dx, pieces = lax.optimization_barrier((exchanged[l + 1], dx, pieces))
        else:
            dx, pieces = lax.optimization_barrier((dx, pieces))
        exchanged[l] = all_to_all_many_async(pieces, f"exchange_grads_{l}", EXCHANGE_ID + l)

    for l in reversed(range(DEPTH)):
        kind, j = l % 3, l // 3
        got = exchanged[l]
        in_name = ('a_w_in', 'b_w_in', 'c_w_in')[kind]
        for name, idx, pc in (('w_mem_kv', l, got[0]), ('w_out', l, got[1]), ('w_ffn_up', l, got[2]),
                              ('w_ffn_down', l, got[3]), (in_name, j, got[4]), ('ffn_conv', l, got[5])):
            apply_adam(name, idx, pc, l)
        if kind == 1:
            for name, pc in zip(('b_w_decay_up', 'b_w_iclr_up', 'b_w_gate_up'), got[6:]):
                apply_adam(name, j, pc, l)
        elif kind == 2:
            apply_adam('c_conv', j, got[6], l)

    rep_vals = []
    for n in REPLICATED:
        gval = rep_grads[n]
        gval = jnp.stack(gval) if isinstance(gval, list) else gval
        rep_vals.append(gval.reshape(a[n].shape))
    shapes = [a[n].shape for n in REPLICATED] + [(1,)]
    part = _pack(rep_vals + [loss_part.reshape(1)])
    gathered = all_gather_many([part], "gather_small_grads")[0]
    zero = jnp.zeros((1,), F32)
    packed = lambda pre: _pack([a[pre + n] for n in REPLICATED] + [zero])
    rep_out = adamw_sum(gathered, packed(''), packed('m_'), packed('v_'), "adamw_replicated")
    rep_out = [_unpack(o, shapes) for o in rep_out]
    loss = rep_out[0][-1][0]
    for i, n in enumerate(REPLICATED):
        results[n] = [o[i] for o in rep_out]

    def leaf(name, which):
        r = results[name]
        if isinstance(r, dict):
            return jnp.stack([r[i][which] for i in range(len(r))])
        return r[which]

    outs = [loss, dx[None]]
    for which in range(4):
        outs += [leaf(n, which) for n in WEIGHTS]
    return tuple(outs)
```

```python
import functools

import numpy as np
import jax
import jax.numpy as jnp
from jax import lax
from jax.experimental import pallas as pl
from jax.experimental.pallas import tpu as pltpu
from jax.experimental.pallas import tpu_sc as plsc

F32, BF16 = jnp.float32, jnp.bfloat16
HI = lax.Precision.HIGHEST
V7X_VMEM_BYTES = 64 * 1024 * 1024
VMEM_LIMIT = V7X_VMEM_BYTES - 8 * 1024 * 1024
SUBLANES, LANES = 8, 128
N_DEV = 8

D_MODEL = 2048
DEPTH = 4
MIX_WIDTH = 1536
MEM_HEADS, MEM_HEAD_DIM, MEM_WIDTH = 4, 128, 512
NORM_EPS = 1e-6
SWA_HEAD_DIM, SWA_Q_HEADS, SWA_KV_HEADS, SWA_GROUP, SWA_BLOCK = 64, 24, 4, 6, 128
RWKV_HEADS, RWKV_HEAD_DIM, RWKV_GN_EPS = 24, 64, 64e-5
RWKV_DECAY_RANK, RWKV_ICLR_RANK, RWKV_GATE_RANK = 96, 96, 256
GDN_HEAD_DIM, GDN_V_HEADS, GDN_QK_HEADS, GDN_CONV, GDN_CHUNK = 128, 12, 6, 4, 64
GDN_QK_WIDTH = GDN_QK_HEADS * GDN_HEAD_DIM
GDN_CONV_WIDTH = 2 * GDN_QK_WIDTH + MIX_WIDTH
D_FF, FFN_CONV = 5632, 3
ADAM_LR, ADAM_B1, ADAM_B2, ADAM_EPS, ADAM_WD, ADAM_STEP = 0.001, 0.9, 0.999, 1e-08, 0.01, 10
MESH = pl.DeviceIdType.MESH


def _cp(sem=None):
    return pltpu.CompilerParams(dimension_semantics=sem, vmem_limit_bytes=VMEM_LIMIT)


def _tile(n, cands):
    for c in cands:
        if n % c == 0:
            return c
    return n


def _dot(a, b):
    return jnp.dot(a, b, precision=HI, preferred_element_type=F32)


def _dot_nt(a, b):
    return lax.dot_general(a, b, (((1,), (1,)), ((), ())), precision=HI, preferred_element_type=F32)


def _dot_tn(a, b):
    return lax.dot_general(a, b, (((0,), (0,)), ((), ())), precision=HI, preferred_element_type=F32)


def _sigmoid(x):
    return 1.0 / (1.0 + jnp.exp(-x))


def _softplus(x):
    return jnp.maximum(x, 0.0) + jnp.log(1.0 + jnp.exp(-jnp.abs(x)))


def _silu(x):
    return x * _sigmoid(x)


def mm(a, b, *, ta=False, tb=False, res=None, out_dtype=F32, name):
    (k_a, m) = a.shape if ta else a.shape[::-1]
    (k_b, n) = b.shape[::-1] if tb else b.shape
    assert k_a == k_b, (a.shape, b.shape, ta, tb)
    kdim = k_a
    tm = _tile(m, (1024, 512, 256))
    tn = _tile(n, (1024, 768, 512, 384, 256, 128))
    tk = _tile(kdim, (512, 256, 128))
    nk = kdim // tk
    dims = (((0 if ta else 1,), (1 if tb else 0,)), ((), ()))

    def body(*refs):
        if res is None:
            a_ref, b_ref, o_ref, acc = refs
        else:
            a_ref, b_ref, r_ref, o_ref, acc = refs
        kk = pl.program_id(2)

        @pl.when(kk == 0)
        def _():
            acc[...] = jnp.zeros_like(acc)

        acc[...] += lax.dot_general(a_ref[...].astype(BF16), b_ref[...].astype(BF16), dims,
                                    preferred_element_type=F32)

        @pl.when(kk == nk - 1)
        def _():
            out = acc[...] if res is None else acc[...] + r_ref[...]
            o_ref[...] = out.astype(o_ref.dtype)

    a_spec = pl.BlockSpec((tk, tm), lambda i, j, k: (k, i)) if ta else pl.BlockSpec((tm, tk), lambda i, j, k: (i, k))
    b_spec = pl.BlockSpec((tn, tk), lambda i, j, k: (j, k)) if tb else pl.BlockSpec((tk, tn), lambda i, j, k: (k, j))
    o_spec = pl.BlockSpec((tm, tn), lambda i, j, k: (i, j))
    in_specs, args = [a_spec, b_spec], [a, b]
    if res is not None:
        in_specs.append(o_spec)
        args.append(res)
    return pl.pallas_call(
        body, grid=(m // tm, n // tn, nk), in_specs=in_specs, out_specs=o_spec,
        out_shape=jax.ShapeDtypeStruct((m, n), out_dtype), scratch_shapes=[pltpu.VMEM((tm, tn), F32)],
        compiler_params=_cp(("parallel", "parallel", "arbitrary")), name=name)(*args)


def _coords():
    return lax.axis_index("x"), lax.axis_index("y"), lax.axis_index("c")


def _block_index(p):
    return 4 * p[0] + 2 * p[1] + p[2]


def _all_gather_body(x_refs, o_refs, send, recv, loc):
    n = len(x_refs)
    x, y, c = _coords()
    me, sib = (x, y, c), (x, y, 1 - c)
    chips = [(1 - x, y), (x, 1 - y), (1 - x, 1 - y)]

    def cp(i, k, block, to, src=None):
        dst = o_refs[i].at[_block_index(block)]
        return pltpu.make_async_remote_copy(
            src_ref=dst if src is None else src, dst_ref=dst, send_sem=send.at[i, k], recv_sem=recv.at[i, k],
            device_id=to, device_id_type=MESH)

    mine = [pltpu.make_async_copy(x_refs[i], o_refs[i].at[_block_index(me)], loc.at[i]) for i in range(n)]
    for m_ in mine:
        m_.start()
    first = []
    for i in range(n):
        first.append(cp(i, 0, me, sib, src=x_refs[i]))
        for j, chip in enumerate(chips):
            first.append(cp(i, 1 + j, me, (*chip, c), src=x_refs[i]))
    for f in first:
        f.start()
    passed = []
    for j, chip in enumerate(chips):
        for i in range(n):
            cp(i, 1 + j, (*chip, c), me).wait_recv()
            p = cp(i, 4 + j, (*chip, c), sib)
            p.start()
            passed.append(p)
    for i in range(n):
        cp(i, 0, sib, me).wait_recv()
        for j, chip in enumerate(chips):
            cp(i, 4 + j, (*chip, 1 - c), me).wait_recv()
    for f in first + passed:
        f.wait_send()
    for m_ in mine:
        m_.wait()


def _all_gather_peers():
    x, y, c = _coords()
    return [(x, y, 1 - c), (1 - x, y, c), (x, 1 - y, c), (1 - x, 1 - y, c)]


def _all_to_all_peers():
    x, y, c = _coords()
    return [(1 - x if r & 4 else x, 1 - y if r & 2 else y, 1 - c if r & 1 else c) for r in range(1, N_DEV)]


def _comm_scratch(n):
    return [pltpu.SemaphoreType.DMA((n, 7)), pltpu.SemaphoreType.DMA((n, 7)), pltpu.SemaphoreType.DMA((n,))]


def all_gather_many(xs, name):
    n = len(xs)

    def body(*refs):
        _all_gather_body(refs[:n], refs[n:2 * n], *refs[2 * n:])

    any_spec = pl.BlockSpec(memory_space=pl.ANY)
    return pl.pallas_call(
        body, in_specs=[any_spec] * n, out_specs=[any_spec] * n,
        out_shape=[jax.ShapeDtypeStruct((N_DEV,) + x.shape, x.dtype) for x in xs],
        scratch_shapes=_comm_scratch(n), name=name)(*xs)


def _on_sequencer(exchange, peers, xs, out_shapes, name, collective_id):
    x_refs = [jax.new_ref(x, memory_space=pltpu.MemorySpace.HBM) for x in xs]
    o_refs = [jax.empty_ref(s, memory_space=pltpu.MemorySpace.HBM) for s in out_shapes]

    @pl.kernel(mesh=plsc.ScalarSubcoreMesh(axis_name="sequencer", num_cores=1), name=name,
               scratch_types=tuple(_comm_scratch(len(xs))),
               compiler_params=pltpu.CompilerParams(collective_id=collective_id))
    def launch(send, recv, loc):
        barrier = pltpu.get_barrier_semaphore()
        ids = peers()
        for peer in ids:
            pl.semaphore_signal(barrier, inc=1, device_id=peer, device_id_type=MESH)
        pl.semaphore_wait(barrier, len(ids))
        exchange(x_refs, o_refs, send, recv, loc)

    launch()
    return [o[...] for o in o_refs]


def _all_to_all_copies(x_refs, o_refs, send, recv, arrivals):
    x, y, c = _coords()
    me = _block_index((x, y, c))
    copies = []
    for r, peer in enumerate(_all_to_all_peers()):
        pidx = _block_index(peer)
        for i in range(len(x_refs)):
            copies.append(pltpu.make_async_remote_copy(
                src_ref=x_refs[i].at[pidx], dst_ref=o_refs[i].at[pidx if arrivals else me],
                send_sem=send.at[7 * i + r], recv_sem=recv.at[7 * i + r], device_id=peer, device_id_type=MESH))
    return copies


def all_to_all_start(xs, name):
    n = len(xs)
    x, y, c = _coords()
    me = _block_index((x, y, c))
    lands = [lax.dynamic_update_slice_in_dim(lax.empty(v.shape, v.dtype), lax.dynamic_slice_in_dim(v, me, 1, 0), me, 0)
             for v in xs]

    def body(*refs):
        x_refs, o_refs = refs[:n], refs[n:2 * n]
        send, recv = refs[2 * n], refs[2 * n + 1]
        token = refs[-1]
        for s in _all_to_all_copies(x_refs, o_refs, send, recv, arrivals=False):
            s.start()
        token[...] = jnp.zeros_like(token)

    hbm = pl.BlockSpec(memory_space=pltpu.HBM)
    sem = pl.BlockSpec(memory_space=pltpu.SEMAPHORE)
    out = pl.pallas_call(
        body, name=name,
        out_shape=(pltpu.SemaphoreType.DMA((7 * n,)), pltpu.SemaphoreType.DMA((7 * n,)),
                   *[pltpu.HBM(v.shape, v.dtype) for v in xs], *[pltpu.HBM(v.shape, v.dtype) for v in xs],
                   jax.ShapeDtypeStruct((SUBLANES, LANES), F32)),
        in_specs=[hbm] * (2 * n), out_specs=(sem, sem, *([hbm] * (2 * n)), pl.BlockSpec(memory_space=pltpu.VMEM)),
        input_output_aliases={i: 2 + i for i in range(2 * n)},
        compiler_params=pltpu.CompilerParams(has_side_effects=pltpu.SideEffectType.DATAFLOW_SIDE_EFFECTING),
    )(*[pltpu.with_memory_space_constraint(v, pltpu.HBM) for v in xs],
      *[pltpu.with_memory_space_constraint(v, pltpu.HBM) for v in lands])
    return out[0], out[1], list(out[2:2 + n]), list(out[2 + n:2 + 2 * n]), out[-1]


def all_to_all_wait(send, recv, xs, lands, after, name):
    n = len(xs)

    def body(*refs):
        x_refs, o_refs = refs[:n], refs[n:2 * n]
        send_ref, recv_ref = refs[2 * n], refs[2 * n + 1]
        for s in _all_to_all_copies(x_refs, o_refs, send_ref, recv_ref, arrivals=False):
            s.wait_send()
        for w in _all_to_all_copies(x_refs, o_refs, send_ref, recv_ref, arrivals=True):
            w.wait_recv()

    hbm = pl.BlockSpec(memory_space=pltpu.HBM)
    sem = pl.BlockSpec(memory_space=pltpu.SEMAPHORE)
    out = pl.pallas_call(
        body, name=name,
        out_shape=tuple(pltpu.HBM(v.shape, v.dtype) for v in list(xs) + list(lands)),
        in_specs=[hbm] * (2 * n) + [sem, sem, pl.BlockSpec(memory_space=pl.ANY)], out_specs=tuple([hbm] * (2 * n)),
        input_output_aliases={i: i for i in range(2 * n)},
        compiler_params=pltpu.CompilerParams(has_side_effects=pltpu.SideEffectType.DATAFLOW_SIDE_EFFECTING),
    )(*xs, *lands, send, recv, after)
    return list(out[n:])


def all_gather_many_async(xs, name, collective_id):
    shapes = [jax.ShapeDtypeStruct((N_DEV,) + x.shape, x.dtype) for x in xs]
    return _on_sequencer(_all_gather_body, _all_gather_peers, xs, shapes, name, collective_id)


def adamw_sum(pieces, w, m, v, name):
    rows, cols = w.shape
    tr = _tile(rows, (128, 64, 32, 16, 8))
    c1 = 1.0 - ADAM_B1 ** ADAM_STEP
    c2 = 1.0 - ADAM_B2 ** ADAM_STEP

    def body(p_ref, w_ref, m_ref, v_ref, g_out, d_out, m_out, v_out):
        g = p_ref[0].astype(F32)
        for s in range(1, N_DEV):
            g = g + p_ref[s].astype(F32)
        m_new = ADAM_B1 * m_ref[...] + (1.0 - ADAM_B1) * g
        v_new = ADAM_B2 * v_ref[...] + (1.0 - ADAM_B2) * (g * g)
        m_hat = m_new / c1
        v_hat = v_new / c2
        g_out[...] = g
        d_out[...] = -ADAM_LR * (m_hat / (jnp.sqrt(v_hat) + ADAM_EPS) + ADAM_WD * w_ref[...])
        m_out[...] = m_new
        v_out[...] = v_new

    spec = pl.BlockSpec((tr, cols), lambda i: (i, 0))
    out = jax.ShapeDtypeStruct((rows, cols), F32)
    return pl.pallas_call(
        body, grid=(rows // tr,), in_specs=[pl.BlockSpec((N_DEV, tr, cols), lambda i: (0, i, 0)), spec, spec, spec],
        out_specs=[spec] * 4, out_shape=[out] * 4, compiler_params=_cp(("parallel",)), name=name)(pieces, w, m, v)


def rmsnorm_fwd(x, g, out_dtype, name):
    s, d = x.shape
    tr = _tile(s, (256, 128, 64, 32, 16))

    def body(x_ref, g_ref, o_ref):
        xv = x_ref[...]
        rstd = lax.rsqrt(jnp.mean(xv * xv, axis=-1, keepdims=True) + NORM_EPS)
        o_ref[...] = (xv * rstd * g_ref[...]).astype(o_ref.dtype)

    return pl.pallas_call(
        body, grid=(s // tr,), in_specs=[pl.BlockSpec((tr, d), lambda i: (i, 0)), pl.BlockSpec((1, d), lambda i: (0, 0))],
        out_specs=pl.BlockSpec((tr, d), lambda i: (i, 0)), out_shape=jax.ShapeDtypeStruct((s, d), out_dtype),
        compiler_params=_cp(("parallel",)), name=name)(x, g.reshape(1, d))


def rmsnorm_bwd(x, g, dh, dres, name):
    s, d = x.shape
    tr = _tile(s, (256, 128, 64, 32, 16))

    def body(*refs):
        if dres is None:
            x_ref, g_ref, dh_ref, dx_ref, dg_ref = refs
        else:
            x_ref, g_ref, dh_ref, dr_ref, dx_ref, dg_ref = refs
        xv = x_ref[...]
        rstd = lax.rsqrt(jnp.mean(xv * xv, axis=-1, keepdims=True) + NORM_EPS)
        xhat = xv * rstd
        dhv = dh_ref[...].astype(F32)
        dhg = dhv * g_ref[...]
        dx = rstd * (dhg - xhat * jnp.mean(dhg * xhat, axis=-1, keepdims=True))
        if dres is not None:
            dx = dx + dr_ref[...]
        dx_ref[...] = dx

        @pl.when(pl.program_id(0) == 0)
        def _():
            dg_ref[...] = jnp.zeros_like(dg_ref)

        dg_ref[...] += jnp.sum(dhv * xhat, axis=0, keepdims=True)

    row = pl.BlockSpec((tr, d), lambda i: (i, 0))
    vec = pl.BlockSpec((1, d), lambda i: (0, 0))
    ins = [x, g.reshape(1, d), dh] + ([] if dres is None else [dres])
    dx, dg = pl.pallas_call(
        body, grid=(s // tr,), in_specs=[row, vec, row] + ([] if dres is None else [row]), out_specs=[row, vec],
        out_shape=[jax.ShapeDtypeStruct((s, d), F32), jax.ShapeDtypeStruct((1, d), F32)],
        compiler_params=_cp(("arbitrary",)), name=name)(*ins)
    return dx, dg.reshape(d)


def final_loss(x, g, target, name):
    s, d = x.shape
    tr = _tile(s, (256, 128, 64, 32, 16))

    def body(x_ref, g_ref, t_ref, l_ref, dx_ref, dg_ref):
        xv = x_ref[...]
        rstd = lax.rsqrt(jnp.mean(xv * xv, axis=-1, keepdims=True) + NORM_EPS)
        xhat = xv * rstd
        err = xhat * g_ref[...] - t_ref[...]
        dy = err * (1.0 / d)
        dhg = dy * g_ref[...]
        dx_ref[...] = rstd * (dhg - xhat * jnp.mean(dhg * xhat, axis=-1, keepdims=True))

        @pl.when(pl.program_id(0) == 0)
        def _():
            dg_ref[...] = jnp.zeros_like(dg_ref)
            l_ref[...] = jnp.zeros_like(l_ref)

        dg_ref[...] += jnp.sum(dy * xhat, axis=0, keepdims=True)
        part = 0.5 * jnp.sum(jnp.mean(err * err, axis=-1, keepdims=True), axis=0, keepdims=True)
        l_ref[...] += jnp.broadcast_to(part, l_ref.shape)

    row = pl.BlockSpec((tr, d), lambda i: (i, 0))
    vec = pl.BlockSpec((1, d), lambda i: (0, 0))
    lspec = pl.BlockSpec((1, LANES), lambda i: (0, 0))
    loss, dx, dg = pl.pallas_call(
        body, grid=(s // tr,), in_specs=[row, vec, row], out_specs=[lspec, row, vec],
        out_shape=[jax.ShapeDtypeStruct((1, LANES), F32), jax.ShapeDtypeStruct((s, d), F32),
                   jax.ShapeDtypeStruct((1, d), F32)],
        compiler_params=_cp(("arbitrary",)), name=name)(x, g.reshape(1, d), target)
    return loss[0, 0], dx, dg.reshape(d)


def _shift_down(tile, prev8, j):
    if j == 0:
        return tile
    rt = pltpu.roll(tile, j, 0)
    rp = pltpu.roll(prev8, j, 0)
    rows = lax.broadcasted_iota(jnp.int32, prev8.shape, 0)
    top = jnp.where(rows < j, rp, rt[:SUBLANES])
    return jnp.concatenate([top, rt[SUBLANES:]], axis=0)


def _shift_up(tile, next8, j):
    if j == 0:
        return tile
    t = tile.shape[0]
    rt = pltpu.roll(tile, t - j, 0)
    rn = pltpu.roll(next8, SUBLANES - j, 0)
    rows = lax.broadcasted_iota(jnp.int32, next8.shape, 0)
    bot = jnp.where(rows >= SUBLANES - j, rn, rt[t - SUBLANES:])
    return jnp.concatenate([rt[:t - SUBLANES], bot], axis=0)


def _halo_specs(t_rows, s_rows, cols, col_of):
    per, last = t_rows // SUBLANES, s_rows // SUBLANES - 1
    prev = pl.BlockSpec((SUBLANES, cols), lambda j, i: (jnp.maximum(i * per - 1, 0), col_of(j)))
    nxt = pl.BlockSpec((SUBLANES, cols), lambda j, i: (jnp.minimum((i + 1) * per, last), col_of(j)))
    return prev, nxt


def ffn_act_fwd(u0, conv, name):
    s, two_f = u0.shape
    f = two_f // 2
    t, c = _tile(s, (256, 128, 64)), 512
    nc = f // c

    def body(g_ref, v_ref, gp_ref, vp_ref, wg_ref, wv_ref, a_ref):
        first = (pl.program_id(1) > 0).astype(F32)

        def conv_of(x_ref, p_ref, w_ref):
            x, p = x_ref[...], p_ref[...] * first
            return (w_ref[0:1, :] * _shift_down(x, p, 2) + w_ref[1:2, :] * _shift_down(x, p, 1) + w_ref[2:3, :] * x)

        ug = conv_of(g_ref, gp_ref, wg_ref)
        uv = conv_of(v_ref, vp_ref, wv_ref)
        a_ref[...] = (_silu(ug) * uv).astype(a_ref.dtype)

    gate = pl.BlockSpec((t, c), lambda j, i: (i, j))
    val = pl.BlockSpec((t, c), lambda j, i: (i, j + nc))
    gp, _ = _halo_specs(t, s, c, lambda j: j)
    vp, _ = _halo_specs(t, s, c, lambda j: j + nc)
    wg = pl.BlockSpec((FFN_CONV, c), lambda j, i: (0, j))
    wv = pl.BlockSpec((FFN_CONV, c), lambda j, i: (0, j + nc))
    return pl.pallas_call(
        body, grid=(nc, s // t), in_specs=[gate, val, gp, vp, wg, wv], out_specs=pl.BlockSpec((t, c), lambda j, i: (i, j)),
        out_shape=jax.ShapeDtypeStruct((s, f), BF16), compiler_params=_cp(("parallel", "parallel")),
        name=name)(u0, u0, u0, u0, conv, conv)


def ffn_act_bwd(u0, conv, da, name):
    s, two_f = u0.shape
    f = two_f // 2
    t, c = _tile(s, (256, 128, 64)), 512
    nc, nt = f // c, s // t

    def body(g_ref, v_ref, gp_ref, vp_ref, gn_ref, vn_ref, wg_ref, wv_ref, da_ref, dan_ref,
             dg_ref, dv_ref, dwg_ref, dwv_ref):
        i = pl.program_id(1)
        first, last = (i > 0).astype(F32), (i < nt - 1).astype(F32)
        zeros8 = jnp.zeros((SUBLANES, c), F32)

        def ext(x_ref, p_ref, n_ref):
            return jnp.concatenate([p_ref[...] * first, x_ref[...], n_ref[...] * last], axis=0)

        def taps(e):
            return pltpu.roll(e, 2, 0), pltpu.roll(e, 1, 0), e

        def conv_of(sh, w_ref):
            return w_ref[0:1, :] * sh[0] + w_ref[1:2, :] * sh[1] + w_ref[2:3, :] * sh[2]

        def conv_t(du, w_ref):
            n = du.shape[0]
            return w_ref[2:3, :] * du + w_ref[1:2, :] * pltpu.roll(du, n - 1, 0) + w_ref[0:1, :] * pltpu.roll(du, n - 2, 0)

        sg, sv = taps(ext(g_ref, gp_ref, gn_ref)), taps(ext(v_ref, vp_ref, vn_ref))
        ug, uv = conv_of(sg, wg_ref), conv_of(sv, wv_ref)
        dae = jnp.concatenate([zeros8, da_ref[...], dan_ref[...] * last], axis=0)
        sig = _sigmoid(ug)
        dug = dae * uv * (sig * (1.0 + ug * (1.0 - sig)))
        duv = dae * (ug * sig)
        dg_ref[...] = conv_t(dug, wg_ref)[SUBLANES:t + SUBLANES].astype(dg_ref.dtype)
        dv_ref[...] = conv_t(duv, wv_ref)[SUBLANES:t + SUBLANES].astype(dv_ref.dtype)

        @pl.when(i == 0)
        def _():
            dwg_ref[...] = jnp.zeros_like(dwg_ref)
            dwv_ref[...] = jnp.zeros_like(dwv_ref)

        def dconv(du, sh):
            d = du[SUBLANES:t + SUBLANES]
            return jnp.concatenate([jnp.sum(d * x[SUBLANES:t + SUBLANES], axis=0, keepdims=True) for x in sh], axis=0)

        dwg_ref[...] += dconv(dug, sg)
        dwv_ref[...] += dconv(duv, sv)

    gate = pl.BlockSpec((t, c), lambda j, i: (i, j))
    val = pl.BlockSpec((t, c), lambda j, i: (i, j + nc))
    gp, gn = _halo_specs(t, s, c, lambda j: j)
    vp, vn = _halo_specs(t, s, c, lambda j: j + nc)
    wg = pl.BlockSpec((FFN_CONV, c), lambda j, i: (0, j))
    wv = pl.BlockSpec((FFN_CONV, c), lambda j, i: (0, j + nc))
    wout = pl.BlockSpec((FFN_CONV, c), lambda j, i: (0, j))
    half = jax.ShapeDtypeStruct((s, f), BF16)
    dwh = jax.ShapeDtypeStruct((FFN_CONV, f), F32)
    return pl.pallas_call(
        body, grid=(nc, nt), in_specs=[gate, val, gp, vp, gn, vn, wg, wv, gate, gn],
        out_specs=[gate, gate, wout, wout], out_shape=[half, half, dwh, dwh],
        compiler_params=_cp(("parallel", "arbitrary")), name=name)(u0, u0, u0, u0, u0, u0, conv, conv, da, da)


def _softmax_rows(s, extra=None):
    m = jnp.max(s, axis=-1, keepdims=True)
    if extra is not None:
        m = jnp.maximum(m, extra)
    m = lax.stop_gradient(m)
    e = jnp.exp(s - m)
    den = jnp.sum(e, axis=-1, keepdims=True)
    if extra is not None:
        den = den + jnp.exp(extra - m)
    return e / den


def _mem_attn_fn(qs, ks, vs):
    outs = []
    for q, k, v in zip(qs, ks, vs):
        p = _softmax_rows(_dot_nt(q, k) * (MEM_HEAD_DIM ** -0.5))
        outs.append(_dot(p, v))
    return outs


def _mem_heads(q_ref, kv_ref):
    d = MEM_HEAD_DIM
    qs = [q_ref[:, h * d:(h + 1) * d] for h in range(MEM_HEADS)]
    ks = [kv_ref[:, h * d:(h + 1) * d] for h in range(MEM_HEADS)]
    vs = [kv_ref[:, MEM_WIDTH + h * d:MEM_WIDTH + (h + 1) * d] for h in range(MEM_HEADS)]
    return qs, ks, vs


def mem_attn_fwd(p, q_col, kv, name):
    s = p.shape[0]
    t = _tile(s, (256, 128))
    m = kv.shape[0]

    def body(q_ref, kv_ref, o_ref):
        outs = _mem_attn_fn(*_mem_heads(q_ref, kv_ref))
        o_ref[...] = jnp.concatenate(outs, axis=1).astype(o_ref.dtype)

    return pl.pallas_call(
        body, grid=(s // t,),
        in_specs=[pl.BlockSpec((t, MEM_WIDTH), lambda i: (i, q_col // MEM_WIDTH)),
                  pl.BlockSpec((m, 2 * MEM_WIDTH), lambda i: (0, 0))],
        out_specs=pl.BlockSpec((t, MEM_WIDTH), lambda i: (i, 0)), out_shape=jax.ShapeDtypeStruct((s, MEM_WIDTH), BF16),
        compiler_params=_cp(("parallel",)), name=name)(p, kv)


def mem_attn_bwd(p, q_col, kv, dcat, name):
    s = p.shape[0]
    t = _tile(s, (256, 128))
    m = kv.shape[0]
    d = MEM_HEAD_DIM

    def body(q_ref, kv_ref, dy_ref, dq_ref, dkv_ref):
        qs, ks, vs = _mem_heads(q_ref, kv_ref)
        _, vjp = jax.vjp(_mem_attn_fn, qs, ks, vs)
        dqs, dks, dvs = vjp([dy_ref[:, h * d:(h + 1) * d] for h in range(MEM_HEADS)])
        dq_ref[...] = jnp.concatenate(dqs, axis=1).astype(dq_ref.dtype)

        @pl.when(pl.program_id(0) == 0)
        def _():
            dkv_ref[...] = jnp.zeros_like(dkv_ref)

        dkv_ref[...] += jnp.concatenate(dks + dvs, axis=1)

    return pl.pallas_call(
        body, grid=(s // t,),
        in_specs=[pl.BlockSpec((t, MEM_WIDTH), lambda i: (i, q_col // MEM_WIDTH)),
                  pl.BlockSpec((m, 2 * MEM_WIDTH), lambda i: (0, 0)),
                  pl.BlockSpec((t, MEM_WIDTH), lambda i: (i, MIX_WIDTH // MEM_WIDTH))],
        out_specs=[pl.BlockSpec((t, MEM_WIDTH), lambda i: (i, 0)), pl.BlockSpec((m, 2 * MEM_WIDTH), lambda i: (0, 0))],
        out_shape=[jax.ShapeDtypeStruct((s, MEM_WIDTH), BF16), jax.ShapeDtypeStruct((m, 2 * MEM_WIDTH), F32)],
        compiler_params=_cp(("arbitrary",)), name=name)(p, kv, dcat)


def _swa_fn(qs, kcs, kps, vcs, vps, sinks, not_first):
    t = SWA_BLOCK
    qi = lax.broadcasted_iota(jnp.int32, (t, 2 * t), 0)
    kj = lax.broadcasted_iota(jnp.int32, (t, 2 * t), 1)
    dist = t + qi - kj
    valid = (dist >= 0) & (dist < t) & ((kj >= t) | not_first)
    distf = dist.astype(F32)
    outs = []
    for kh in range(SWA_KV_HEADS):
        kb = jnp.concatenate([kps[kh], kcs[kh]], axis=0)
        vb = jnp.concatenate([vps[kh], vcs[kh]], axis=0)
        for g in range(SWA_GROUP):
            h = kh * SWA_GROUP + g
            slope = 2.0 ** (-8.0 * (h + 1) / SWA_Q_HEADS)
            sc = _dot_nt(qs[h], kb) * (SWA_HEAD_DIM ** -0.5) - slope * distf
            sc = jnp.where(valid, sc, -jnp.inf)
            outs.append(_dot(_softmax_rows(sc, extra=sinks[h]), vb))
    return outs


def _swa_args(q_ref, kc_ref, kp_ref, vc_ref, vp_ref, sink_ref):
    d = SWA_HEAD_DIM
    qs = [q_ref[:, h * d:(h + 1) * d] for h in range(SWA_Q_HEADS)]
    per_kv = lambda ref: [ref[:, h * d:(h + 1) * d] for h in range(SWA_KV_HEADS)]
    sinks = [sink_ref[0:1, h:h + 1] for h in range(SWA_Q_HEADS)]
    return qs, per_kv(kc_ref), per_kv(kp_ref), per_kv(vc_ref), per_kv(vp_ref), sinks


def _swa_specs(nb, order):
    t, kvw = SWA_BLOCK, SWA_KV_HEADS * SWA_HEAD_DIM
    k_col, v_col = MIX_WIDTH // kvw, MIX_WIDTH // kvw + 1
    q = pl.BlockSpec((t, MIX_WIDTH), lambda n: (order(n), 0))
    kc = pl.BlockSpec((t, kvw), lambda n: (order(n), k_col))
    kp = pl.BlockSpec((t, kvw), lambda n: (jnp.maximum(order(n) - 1, 0), k_col))
    vc = pl.BlockSpec((t, kvw), lambda n: (order(n), v_col))
    vp = pl.BlockSpec((t, kvw), lambda n: (jnp.maximum(order(n) - 1, 0), v_col))
    sink = pl.BlockSpec((1, LANES), lambda n: (0, 0))
    return [q, kc, kp, vc, vp, sink]


def _pad_lanes(v):
    return jnp.pad(v.reshape(1, -1), ((0, 0), (0, LANES - v.size)))


def swa_fwd(p, sinks, name):
    s = p.shape[0]
    nb = s // SWA_BLOCK

    def body(q_ref, kc_ref, kp_ref, vc_ref, vp_ref, sink_ref, o_ref):
        outs = _swa_fn(*_swa_args(q_ref, kc_ref, kp_ref, vc_ref, vp_ref, sink_ref), pl.program_id(0) > 0)
        o_ref[...] = jnp.concatenate(outs, axis=1).astype(o_ref.dtype)

    return pl.pallas_call(
        body, grid=(nb,), in_specs=_swa_specs(nb, lambda n: n),
        out_specs=pl.BlockSpec((SWA_BLOCK, MIX_WIDTH), lambda n: (n, 0)),
        out_shape=jax.ShapeDtypeStruct((s, MIX_WIDTH), BF16), compiler_params=_cp(("parallel",)),
        name=name)(p, p, p, p, p, _pad_lanes(sinks))


def swa_bwd(p, sinks, dcat, name):
    s = p.shape[0]
    nb = s // SWA_BLOCK
    t, d, kvw = SWA_BLOCK, SWA_HEAD_DIM, SWA_KV_HEADS * SWA_HEAD_DIM
    rev = lambda n: nb - 1 - n

    def body(q_ref, kc_ref, kp_ref, vc_ref, vp_ref, sink_ref, dy_ref, dq_ref, dk_ref, dv_ref, ds_ref, ck, cv):
        n = pl.program_id(0)

        @pl.when(n == 0)
        def _():
            ck[...] = jnp.zeros_like(ck)
            cv[...] = jnp.zeros_like(cv)
            ds_ref[...] = jnp.zeros_like(ds_ref)

        args = _swa_args(q_ref, kc_ref, kp_ref, vc_ref, vp_ref, sink_ref)
        _, vjp = jax.vjp(functools.partial(_swa_fn, not_first=rev(n) > 0), *args)
        dqs, dkcs, dkps, dvcs, dvps, dsinks = vjp([dy_ref[:, h * d:(h + 1) * d] for h in range(SWA_Q_HEADS)])
        dq_ref[...] = jnp.concatenate(dqs, axis=1).astype(dq_ref.dtype)
        dk_ref[...] = (jnp.concatenate(dkcs, axis=1) + ck[...]).astype(dk_ref.dtype)
        dv_ref[...] = (jnp.concatenate(dvcs, axis=1) + cv[...]).astype(dv_ref.dtype)
        ck[...] = jnp.concatenate(dkps, axis=1)
        cv[...] = jnp.concatenate(dvps, axis=1)
        lane = lax.broadcasted_iota(jnp.int32, (1, LANES), 1)
        acc = jnp.zeros((1, LANES), F32)
        for h in range(SWA_Q_HEADS):
            acc = acc + jnp.where(lane == h, dsinks[h], 0.0)
        ds_ref[...] += acc

    dy = pl.BlockSpec((t, MIX_WIDTH), lambda n: (rev(n), 0))
    kv_out = pl.BlockSpec((t, kvw), lambda n: (rev(n), 0))
    dq, dk, dv, ds = pl.pallas_call(
        body, grid=(nb,), in_specs=_swa_specs(nb, rev) + [dy],
        out_specs=[dy, kv_out, kv_out, pl.BlockSpec((1, LANES), lambda n: (0, 0))],
        out_shape=[jax.ShapeDtypeStruct((s, MIX_WIDTH), BF16), jax.ShapeDtypeStruct((s, kvw), BF16),
                   jax.ShapeDtypeStruct((s, kvw), BF16), jax.ShapeDtypeStruct((1, LANES), F32)],
        scratch_shapes=[pltpu.VMEM((t, kvw), F32), pltpu.VMEM((t, kvw), F32)],
        compiler_params=_cp(("arbitrary",)), name=name)(p, p, p, p, p, _pad_lanes(sinks), dcat)
    return dq, dk, dv, ds[0, :SWA_Q_HEADS]


RW_SHIFT = 5120
RW_R, RW_K, RW_V, RW_WD, RW_AD, RW_GD = 0, 1536, 3072, 4608, 4736, 4864


def _head_matrix(width, head_dim):
    e = (np.arange(width)[:, None] // head_dim == np.arange(LANES)[None, :]).astype(np.float32)
    return jnp.asarray(e), jnp.asarray(e.T)


def _rwkv_pre_fn(pieces, shifted, mus, w0, wdu, a0, wiu, wgu, k_k, k_a, e, et):
    r, k, v, wd, ad, gd = [p + (s - p) * mu for p, s, mu in zip(pieces, shifted, mus)]
    w_log = -_softplus(-(w0 + _dot(jnp.tanh(wd), wdu))) - 0.5
    lw = -jnp.exp(w_log)
    a = _sigmoid(a0 + _dot(ad, wiu))
    g = _dot(_sigmoid(gd), wgu)
    kkr = k * k_k
    kk = kkr * _dot(lax.rsqrt(_dot(kkr * kkr, e) + 1e-6), et)
    k2 = k * (1.0 + (a - 1.0) * k_a)
    return r, lw, k2, v, kk, kk * a, g


_RW_GROUPS = ((RW_R, MIX_WIDTH), (RW_K, MIX_WIDTH), (RW_V, MIX_WIDTH), (RW_WD, LANES), (RW_AD, LANES), (RW_GD, 2 * LANES))


def _rwkv_pre_inputs(p_ref, prev_ref, mu_ref, first):
    pieces = [p_ref[:, o:o + n] for o, n in _RW_GROUPS]
    shifted = [_shift_down(p_ref[:, o:o + n], prev_ref[:, o:o + n] * first, 1) for o, n in _RW_GROUPS]
    mus = [mu_ref[:, o:o + n] for o, n in _RW_GROUPS]
    return pieces, shifted, mus


def _rwkv_param_specs():
    vec = lambda n: pl.BlockSpec((1, n), lambda i: (0, 0))
    mat = lambda r, c: pl.BlockSpec((r, c), lambda i: (0, 0))
    return [vec(RW_SHIFT), vec(MIX_WIDTH), mat(LANES, MIX_WIDTH), vec(MIX_WIDTH), mat(LANES, MIX_WIDTH),
            mat(2 * LANES, MIX_WIDTH), vec(MIX_WIDTH), vec(MIX_WIDTH), mat(MIX_WIDTH, LANES), mat(LANES, MIX_WIDTH)]


def rwkv_pre_fwd(p, params, name):
    s = p.shape[0]
    t = _tile(s, (128, 64))

    def body(p_ref, prev_ref, mu_ref, *rest):
        prm, outs = rest[:9], rest[9:]
        first = (pl.program_id(0) > 0).astype(F32)
        pieces, shifted, mus = _rwkv_pre_inputs(p_ref, prev_ref, mu_ref, first)
        res = _rwkv_pre_fn(pieces, shifted, mus, *[q[...] for q in prm])
        for o_ref, val in zip(outs, res):
            o_ref[...] = val

    row = pl.BlockSpec((t, RW_SHIFT), lambda i: (i, 0))
    prev = pl.BlockSpec((SUBLANES, RW_SHIFT), lambda i: (jnp.maximum(i * (t // SUBLANES) - 1, 0), 0))
    out = pl.BlockSpec((t, MIX_WIDTH), lambda i: (i, 0))
    return pl.pallas_call(
        body, grid=(s // t,), in_specs=[row, prev] + _rwkv_param_specs(), out_specs=[out] * 7,
        out_shape=[jax.ShapeDtypeStruct((s, MIX_WIDTH), F32)] * 7, compiler_params=_cp(("parallel",)),
        name=name)(p, p, *params)


def rwkv_pre_bwd(p, params, cots, name):
    s = p.shape[0]
    t = _tile(s, (64, 32))

    def body(p_ref, prev_ref, mu_ref, *rest):
        prm, cot, outs = rest[:9], rest[9:19], rest[19:]
        dp_ref, dps_ref, grads = outs[0], outs[1], outs[2:]
        i = pl.program_id(0)
        first = (i > 0).astype(F32)
        pieces, shifted, mus = _rwkv_pre_inputs(p_ref, prev_ref, mu_ref, first)
        prm_v = [q[...] for q in prm]
        fn = lambda pieces, shifted, mus, *small: _rwkv_pre_fn(pieces, shifted, mus, *small, prm_v[7], prm_v[8])
        _, vjp = jax.vjp(fn, pieces, shifted, mus, *prm_v[:7])
        dr, dw, dk2, dv, dkk, db, dr2, dk22, dv2, dg = [c[...] for c in cot]
        res = vjp((dr + dr2, dw, dk2 + dk22, dv + dv2, dkk, db, dg))
        dpieces, dshifted, dmus, dsmall = res[0], res[1], res[2], res[3:]
        for (o, n), dpi, dsi in zip(_RW_GROUPS, dpieces, dshifted):
            dp_ref[:, o:o + n] = dpi
            dps_ref[:, o:o + n] = dsi

        @pl.when(i == 0)
        def _():
            for g_ref in grads:
                g_ref[...] = jnp.zeros_like(g_ref)

        for (o, n), dmu in zip(_RW_GROUPS, dmus):
            grads[0][:, o:o + n] += dmu
        for g_ref, dval in zip(grads[1:], dsmall):
            g_ref[...] += dval

    row = pl.BlockSpec((t, RW_SHIFT), lambda i: (i, 0))
    prev = pl.BlockSpec((SUBLANES, RW_SHIFT), lambda i: (jnp.maximum(i * (t // SUBLANES) - 1, 0), 0))
    act = pl.BlockSpec((t, MIX_WIDTH), lambda i: (i, 0))
    pspecs = _rwkv_param_specs()
    full = jax.ShapeDtypeStruct((s, RW_SHIFT), F32)
    gshapes = [jax.ShapeDtypeStruct(q.shape, F32) for q in params[:8]]
    return pl.pallas_call(
        body, grid=(s // t,), in_specs=[row, prev] + pspecs + [act] * 10, out_specs=[row, row] + pspecs[:8],
        out_shape=[full, full] + gshapes, compiler_params=_cp(("arbitrary",)), name=name)(p, p, *params, *cots)


def shift_add(a, b, js, out_dtype, name):
    s, c = a.shape
    t = _tile(s, (256, 128, 64))
    tc = _tile(c, (1024, 768, 512, 640, 384, 256, 128))
    nt, nb = s // t, len(b)

    def body(a_ref, *rest):
        b_refs, n_refs, o_ref = rest[:nb], rest[nb:2 * nb], rest[2 * nb]
        last = (pl.program_id(1) < nt - 1).astype(F32)
        acc = a_ref[...]
        for b_ref, n_ref, j in zip(b_refs, n_refs, js):
            acc = acc + _shift_up(b_ref[...], n_ref[...] * last, j)
        o_ref[...] = acc.astype(o_ref.dtype)

    tile = pl.BlockSpec((t, tc), lambda j, i: (i, j))
    _, nxt = _halo_specs(t, s, tc, lambda j: j)
    return pl.pallas_call(
        body, grid=(c // tc, nt), in_specs=[tile] * (1 + nb) + [nxt] * nb, out_specs=tile,
        out_shape=jax.ShapeDtypeStruct((s, c), out_dtype), compiler_params=_cp(("parallel", "parallel")),
        name=name)(a, *b, *b)


def _rwkv_post_fn(y, r, k2, v, g, gn_g, gn_b, r_k, e, et):
    n = RWKV_HEAD_DIM
    yc = y - _dot(_dot(y, e), et) * (1.0 / n)
    rstd = lax.rsqrt(_dot(yc * yc, e) * (1.0 / n) + RWKV_GN_EPS)
    yn = yc * _dot(rstd, et) * gn_g + gn_b
    bonus = _dot(_dot(r * k2 * r_k, e), et) * v
    return (yn + bonus) * g


def rwkv_post_fwd(acts, params, name):
    s = acts[0].shape[0]
    t = _tile(s, (256, 128))

    def body(*refs):
        vals = [q[...] for q in refs[:10]]
        refs[10][...] = _rwkv_post_fn(*vals).astype(refs[10].dtype)

    act = pl.BlockSpec((t, MIX_WIDTH), lambda i: (i, 0))
    vec = pl.BlockSpec((1, MIX_WIDTH), lambda i: (0, 0))
    mats = [pl.BlockSpec((MIX_WIDTH, LANES), lambda i: (0, 0)), pl.BlockSpec((LANES, MIX_WIDTH), lambda i: (0, 0))]
    return pl.pallas_call(
        body, grid=(s // t,), in_specs=[act] * 5 + [vec] * 3 + mats, out_specs=act,
        out_shape=jax.ShapeDtypeStruct((s, MIX_WIDTH), BF16), compiler_params=_cp(("parallel",)),
        name=name)(*acts, *params)


def rwkv_post_bwd(acts, params, dcat, name):
    s = acts[0].shape[0]
    t = _tile(s, (128, 64))

    def body(*refs):
        ins, dy_ref, outs = refs[:10], refs[10], refs[11:]
        vals = [q[...] for q in ins]
        fn = lambda *a: _rwkv_post_fn(*a, vals[8], vals[9])
        _, vjp = jax.vjp(fn, *vals[:8])
        res = vjp(dy_ref[...])
        for o_ref, val in zip(outs[:5], res[:5]):
            o_ref[...] = val

        @pl.when(pl.program_id(0) == 0)
        def _():
            for g_ref in outs[5:]:
                g_ref[...] = jnp.zeros_like(g_ref)

        for g_ref, val in zip(outs[5:], res[5:]):
            g_ref[...] += val

    act = pl.BlockSpec((t, MIX_WIDTH), lambda i: (i, 0))
    vec = pl.BlockSpec((1, MIX_WIDTH), lambda i: (0, 0))
    mats = [pl.BlockSpec((MIX_WIDTH, LANES), lambda i: (0, 0)), pl.BlockSpec((LANES, MIX_WIDTH), lambda i: (0, 0))]
    a_shape = jax.ShapeDtypeStruct((s, MIX_WIDTH), F32)
    v_shape = jax.ShapeDtypeStruct((1, MIX_WIDTH), F32)
    return pl.pallas_call(
        body, grid=(s // t,), in_specs=[act] * 5 + [vec] * 3 + mats + [act], out_specs=[act] * 5 + [vec] * 3,
        out_shape=[a_shape] * 5 + [v_shape] * 3, compiler_params=_cp(("arbitrary",)), name=name)(*acts, *params, dcat)


RW_CHUNK = 64


RW_HEADS_PER_STEP = 4


def _rwkv_chunk_fn(r, lw, k, v, kk, b, st):
    c = RW_CHUNK
    ri = lax.broadcasted_iota(jnp.int32, (c, c), 0)
    ci = lax.broadcasted_iota(jnp.int32, (c, c), 1)
    incl, strict = ri >= ci, ri > ci
    eye = (ri == ci).astype(F32)
    last_col = (ci == c - 1).astype(F32)
    last_row = (ri == c - 1).astype(F32)
    gc = _dot(incl.astype(F32), lw)
    a_t = -kk * jnp.exp(gc - lw)
    e_neg = jnp.exp(-gc)
    b_t, k_t, r_t = b * e_neg, k * e_neg, r * jnp.exp(gc)
    m_ab = jnp.where(strict, _dot_nt(a_t, b_t), 0.0)
    m_ak = jnp.where(strict, _dot_nt(a_t, k_t), 0.0)
    m_rb = jnp.where(incl, _dot_nt(r_t, b_t), 0.0)
    m_rk = jnp.where(incl, _dot_nt(r_t, k_t), 0.0)
    tinv, pw = eye + m_ab, m_ab
    for _ in range(5):
        pw = _dot(pw, pw)
        tinv = tinv + _dot(tinv, pw)
    u = _dot(tinv, _dot(a_t, st) + _dot(m_ak, v))
    y = _dot(r_t, st) + _dot(m_rb, u) + _dot(m_rk, v)
    dec = jnp.exp(_dot(last_col, gc) - gc)
    g_end = _dot_tn(gc, last_row)
    new_st = st * jnp.exp(g_end) + _dot_tn(b * dec, u) + _dot_tn(k * dec, v)
    return y, new_st


def rwkv_scan_fwd(r, lw, k, v, kk, b, name):
    s = r.shape[0]
    n, hp = RWKV_HEAD_DIM, RW_HEADS_PER_STEP
    nchunk, width = s // RW_CHUNK, RWKV_HEAD_DIM * RW_HEADS_PER_STEP

    def body(r_ref, w_ref, k_ref, v_ref, kk_ref, b_ref, y_ref, ck_ref, carry):
        @pl.when(pl.program_id(1) == 0)
        def _():
            carry[...] = jnp.zeros_like(carry)

        ck_ref[0] = carry[...]
        ys, sts = [], []
        for h in range(hp):
            cols = slice(h * n, (h + 1) * n)
            y, st = _rwkv_chunk_fn(*[q[:, cols] for q in (r_ref, w_ref, k_ref, v_ref, kk_ref, b_ref)], carry[:, cols])
            ys.append(y)
            sts.append(st)
        y_ref[...] = jnp.concatenate(ys, axis=1)
        carry[...] = jnp.concatenate(sts, axis=1)

    blk = pl.BlockSpec((RW_CHUNK, width), lambda j, c: (c, j))
    return pl.pallas_call(
        body, grid=(MIX_WIDTH // width, nchunk), in_specs=[blk] * 6,
        out_specs=[blk, pl.BlockSpec((1, n, width), lambda j, c: (c, 0, j))],
        out_shape=[jax.ShapeDtypeStruct((s, MIX_WIDTH), F32), jax.ShapeDtypeStruct((nchunk, n, MIX_WIDTH), F32)],
        scratch_shapes=[pltpu.VMEM((n, width), F32)],
        compiler_params=_cp(("parallel", "arbitrary")), name=name)(r, lw, k, v, kk, b)


def rwkv_scan_bwd(r, lw, k, v, kk, b, ck, dy, name):
    s = r.shape[0]
    n, hp = RWKV_HEAD_DIM, RW_HEADS_PER_STEP
    nchunk, width = s // RW_CHUNK, RWKV_HEAD_DIM * RW_HEADS_PER_STEP
    rev = lambda c: nchunk - 1 - c

    def body(r_ref, w_ref, k_ref, v_ref, kk_ref, b_ref, ck_ref, dy_ref, *rest):
        outs, carry = rest[:6], rest[6]

        @pl.when(pl.program_id(1) == 0)
        def _():
            carry[...] = jnp.zeros_like(carry)

        grads = []
        for h in range(hp):
            cols = slice(h * n, (h + 1) * n)
            args = [q[:, cols] for q in (r_ref, w_ref, k_ref, v_ref, kk_ref, b_ref)] + [ck_ref[0, :, cols]]
            _, vjp = jax.vjp(_rwkv_chunk_fn, *args)
            grads.append(vjp((dy_ref[:, cols], carry[:, cols])))
        for q in range(6):
            outs[q][...] = jnp.concatenate([g[q] for g in grads], axis=1)
        carry[...] = jnp.concatenate([g[6] for g in grads], axis=1)

    blk = pl.BlockSpec((RW_CHUNK, width), lambda j, c: (rev(c), j))
    out = jax.ShapeDtypeStruct((s, MIX_WIDTH), F32)
    return pl.pallas_call(
        body, grid=(MIX_WIDTH // width, nchunk),
        in_specs=[blk] * 6 + [pl.BlockSpec((1, n, width), lambda j, c: (rev(c), 0, j)), blk],
        out_specs=[blk] * 6, out_shape=[out] * 6, scratch_shapes=[pltpu.VMEM((n, width), F32)],
        compiler_params=_cp(("parallel", "arbitrary")), name=name)(r, lw, k, v, kk, b, ck, dy)


GD_Q, GD_K, GD_V, GD_Z, GD_QMEM, GD_BT, GD_AT, GD_COLS = 0, 768, 1536, 3072, 4608, 5120, 5248, 5376
_GD_GROUPS = ((GD_Q, GDN_QK_WIDTH), (GD_K, GDN_QK_WIDTH), (GD_V, MIX_WIDTH))


def _gdn_pre_fn(xs, convs, bt, at, a_log, dt_bias, e6, e6t, ebc):
    k_w = GDN_CONV
    acts = [_silu(sum(convs[g][j] * xs[g][k_w - 1 - j] for j in range(k_w))) for g in range(3)]
    l2 = lambda x: x * _dot(lax.rsqrt(_dot(x * x, e6) + 1e-6), e6t)
    beta = _sigmoid(bt)
    g = -jnp.exp(a_log) * _softplus(at + dt_bias)
    return l2(acts[0]), l2(acts[1]), acts[2], _dot(g, ebc), _dot(beta, ebc)


def _gdn_pre_inputs(x_ref, prev_ref, conv_ref, first):
    xs = [[_shift_down(x_ref[:, o:o + n], prev_ref[:, o:o + n] * first, j) for j in range(GDN_CONV)]
          for o, n in _GD_GROUPS]
    convs = [[conv_ref[j:j + 1, o:o + n] for j in range(GDN_CONV)] for o, n in _GD_GROUPS]
    return xs, convs


def _gdn_pre_specs(t):
    x = pl.BlockSpec((t, GDN_CONV_WIDTH), lambda i: (i, 0))
    prev = pl.BlockSpec((SUBLANES, GDN_CONV_WIDTH), lambda i: (jnp.maximum(i * (t // SUBLANES) - 1, 0), 0))
    bta = pl.BlockSpec((t, 2 * LANES), lambda i: (i, GD_BT // (2 * LANES)))
    conv = pl.BlockSpec((GDN_CONV, GDN_CONV_WIDTH), lambda i: (0, 0))
    vec = pl.BlockSpec((1, LANES), lambda i: (0, 0))
    mats = [pl.BlockSpec((GDN_QK_WIDTH, LANES), lambda i: (0, 0)), pl.BlockSpec((LANES, GDN_QK_WIDTH), lambda i: (0, 0)),
            pl.BlockSpec((LANES, MIX_WIDTH), lambda i: (0, 0))]
    return [x, prev, bta, conv, vec, vec] + mats


def gdn_pre_fwd(p, params, name):
    s = p.shape[0]
    t = _tile(s, (128, 64))

    def body(x_ref, prev_ref, bta_ref, conv_ref, al_ref, dt_ref, e6_ref, e6t_ref, ebc_ref, *outs):
        first = (pl.program_id(0) > 0).astype(F32)
        xs, convs = _gdn_pre_inputs(x_ref, prev_ref, conv_ref, first)
        res = _gdn_pre_fn(xs, convs, bta_ref[:, :LANES], bta_ref[:, LANES:], al_ref[...], dt_ref[...],
                          e6_ref[...], e6t_ref[...], ebc_ref[...])
        for o_ref, val in zip(outs, res):
            o_ref[...] = val

    qk = pl.BlockSpec((t, GDN_QK_WIDTH), lambda i: (i, 0))
    wide = pl.BlockSpec((t, MIX_WIDTH), lambda i: (i, 0))
    qk_s, wide_s = jax.ShapeDtypeStruct((s, GDN_QK_WIDTH), F32), jax.ShapeDtypeStruct((s, MIX_WIDTH), F32)
    return pl.pallas_call(
        body, grid=(s // t,), in_specs=_gdn_pre_specs(t), out_specs=[qk, qk, wide, wide, wide],
        out_shape=[qk_s, qk_s, wide_s, wide_s, wide_s], compiler_params=_cp(("parallel",)), name=name)(p, p, p, *params)


def gdn_pre_bwd(p, params, cots, name):
    s = p.shape[0]
    t = _tile(s, (64, 32))

    def body(x_ref, prev_ref, bta_ref, conv_ref, al_ref, dt_ref, e6_ref, e6t_ref, ebc_ref, *rest):
        cot, outs = rest[:5], rest[5:]
        dxs, dbta_ref, dconv_ref, dal_ref, ddt_ref = outs[:4], outs[4], outs[5], outs[6], outs[7]
        i = pl.program_id(0)
        first = (i > 0).astype(F32)
        xs, convs = _gdn_pre_inputs(x_ref, prev_ref, conv_ref, first)
        mats = (e6_ref[...], e6t_ref[...], ebc_ref[...])
        fn = lambda xs, convs, bt, at, al, dt: _gdn_pre_fn(xs, convs, bt, at, al, dt, *mats)
        _, vjp = jax.vjp(fn, xs, convs, bta_ref[:, :LANES], bta_ref[:, LANES:], al_ref[...], dt_ref[...])
        d_xs, d_convs, d_bt, d_at, d_al, d_dt = vjp(tuple(c[...] for c in cot))
        for g, (o, n) in enumerate(_GD_GROUPS):
            for j in range(GDN_CONV):
                dxs[j][:, o:o + n] = d_xs[g][j]
        dbta_ref[...] = jnp.concatenate([d_bt, d_at], axis=1).astype(dbta_ref.dtype)

        @pl.when(i == 0)
        def _():
            dconv_ref[...] = jnp.zeros_like(dconv_ref)
            dal_ref[...] = jnp.zeros_like(dal_ref)
            ddt_ref[...] = jnp.zeros_like(ddt_ref)

        for g, (o, n) in enumerate(_GD_GROUPS):
            for j in range(GDN_CONV):
                dconv_ref[j:j + 1, o:o + n] += d_convs[g][j]
        dal_ref[...] += d_al
        ddt_ref[...] += d_dt

    specs = _gdn_pre_specs(t)
    qk = pl.BlockSpec((t, GDN_QK_WIDTH), lambda i: (i, 0))
    wide = pl.BlockSpec((t, MIX_WIDTH), lambda i: (i, 0))
    x_s = jax.ShapeDtypeStruct((s, GDN_CONV_WIDTH), F32)
    vec_s = jax.ShapeDtypeStruct((1, LANES), F32)
    return pl.pallas_call(
        body, grid=(s // t,), in_specs=specs + [qk, qk, wide, wide, wide],
        out_specs=[specs[0]] * 4 + [pl.BlockSpec((t, 2 * LANES), lambda i: (i, 0)), specs[3], specs[4], specs[5]],
        out_shape=[x_s] * 4 + [jax.ShapeDtypeStruct((s, 2 * LANES), BF16),
                               jax.ShapeDtypeStruct((GDN_CONV, GDN_CONV_WIDTH), F32), vec_s, vec_s],
        compiler_params=_cp(("arbitrary",)), name=name)(p, p, p, *params, *cots)


def _gdn_post_fn(o, z, norm_g, e12, e12t, trep):
    rstd = lax.rsqrt(_dot(o * o, e12) * (1.0 / GDN_HEAD_DIM) + NORM_EPS)
    return o * _dot(rstd, e12t) * _dot(norm_g, trep) * _silu(z)


def _gdn_post_specs(t):
    act = pl.BlockSpec((t, MIX_WIDTH), lambda i: (i, 0))
    z = pl.BlockSpec((t, MIX_WIDTH), lambda i: (i, GD_Z // MIX_WIDTH))
    mats = [pl.BlockSpec((SUBLANES, LANES), lambda i: (0, 0)), pl.BlockSpec((MIX_WIDTH, LANES), lambda i: (0, 0)),
            pl.BlockSpec((LANES, MIX_WIDTH), lambda i: (0, 0)), pl.BlockSpec((LANES, MIX_WIDTH), lambda i: (0, 0))]
    return [act, z] + mats


def gdn_post_fwd(o, p, params, name):
    s = o.shape[0]
    t = _tile(s, (256, 128))

    def body(o_ref, z_ref, ng_ref, e_ref, et_ref, tr_ref, out_ref):
        res = _gdn_post_fn(o_ref[...], z_ref[...], ng_ref[0:1, :], e_ref[...], et_ref[...], tr_ref[...])
        out_ref[...] = res.astype(out_ref.dtype)

    act = pl.BlockSpec((t, MIX_WIDTH), lambda i: (i, 0))
    return pl.pallas_call(
        body, grid=(s // t,), in_specs=_gdn_post_specs(t), out_specs=act,
        out_shape=jax.ShapeDtypeStruct((s, MIX_WIDTH), BF16), compiler_params=_cp(("parallel",)),
        name=name)(o, p, *params)


def gdn_post_bwd(o, p, params, dcat, name):
    s = o.shape[0]
    t = _tile(s, (128, 64))

    def body(o_ref, z_ref, ng_ref, e_ref, et_ref, tr_ref, dy_ref, do_ref, dz_ref, dng_ref):
        mats = (e_ref[...], et_ref[...], tr_ref[...])
        fn = lambda o, z, ng: _gdn_post_fn(o, z, ng, *mats)
        _, vjp = jax.vjp(fn, o_ref[...], z_ref[...], ng_ref[0:1, :])
        d_o, d_z, d_ng = vjp(dy_ref[...])
        do_ref[...] = d_o
        dz_ref[...] = d_z.astype(dz_ref.dtype)

        @pl.when(pl.program_id(0) == 0)
        def _():
            dng_ref[...] = jnp.zeros_like(dng_ref)

        dng_ref[...] += d_ng

    act = pl.BlockSpec((t, MIX_WIDTH), lambda i: (i, 0))
    return pl.pallas_call(
        body, grid=(s // t,), in_specs=_gdn_post_specs(t) + [act],
        out_specs=[act, act, pl.BlockSpec((1, LANES), lambda i: (0, 0))],
        out_shape=[jax.ShapeDtypeStruct((s, MIX_WIDTH), F32), jax.ShapeDtypeStruct((s, MIX_WIDTH), BF16),
                   jax.ShapeDtypeStruct((1, LANES), F32)],
        compiler_params=_cp(("arbitrary",)), name=name)(o, p, *params, dcat)


def _gdn_chunk_fn(q, k, v, gb, bb, gb64, state):
    c = GDN_CHUNK
    ri = lax.broadcasted_iota(jnp.int32, (c, c), 0)
    ci = lax.broadcasted_iota(jnp.int32, (c, c), 1)
    causal, strict = ri >= ci, ri > ci
    ltri = causal.astype(F32)
    eye = (ri == ci).astype(F32)
    first_col = (ci == 0).astype(F32)
    last_col = (ci == c - 1).astype(F32)
    last_col_tall = (lax.broadcasted_iota(jnp.int32, (GDN_HEAD_DIM, c), 1) == c - 1).astype(F32)

    qs = q * (GDN_HEAD_DIM ** -0.5)
    gc = _dot(ltri, gb)
    gd = _dot(ltri, gb64)
    diff = gd - _dot_nt(first_col, gd)
    decay = jnp.exp(jnp.where(causal, diff, -jnp.inf))
    kb = k * bb
    lmat = jnp.where(strict, _dot_nt(kb, k) * decay, 0.0)
    tmat, pw = eye - lmat, lmat
    for _ in range(5):
        pw = _dot(pw, pw)
        tmat = tmat + _dot(tmat, pw)
    eg = jnp.exp(gc)
    u = _dot(tmat, v * bb)
    w = _dot(tmat, kb * eg)
    a_qk = jnp.where(causal, _dot_nt(qs, k) * decay, 0.0)
    g_last = _dot(last_col, gc)
    k_dec = k * jnp.exp(g_last - gc)
    v_new = u - _dot(w, state)
    out = _dot(qs * eg, state) + _dot(a_qk, v_new)
    new_state = state * jnp.exp(_dot(last_col_tall, gc)) + _dot_tn(k_dec, v_new)
    return out, new_state


GDN_REP = GDN_V_HEADS // GDN_QK_HEADS


def _gdn_chunk_specs(order):
    c, d = GDN_CHUNK, GDN_HEAD_DIM
    qk = pl.BlockSpec((c, d), lambda j, n: (order(n), j))
    vh = pl.BlockSpec((c, GDN_REP * d), lambda j, n: (order(n), j))
    st = pl.BlockSpec((GDN_REP, 1, d, d), lambda j, n: (j, order(n), 0, 0))
    return qk, vh, st


def gdn_chunk_fwd(q, k, v, gb, bb, name):
    s = q.shape[0]
    nc, d = s // GDN_CHUNK, GDN_HEAD_DIM

    def body(q_ref, k_ref, v_ref, gb_ref, bb_ref, o_ref, st_ref, carry):
        @pl.when(pl.program_id(1) == 0)
        def _():
            carry[...] = jnp.zeros_like(carry)

        for rep in range(GDN_REP):
            cols = slice(rep * d, (rep + 1) * d)
            state = carry[rep]
            st_ref[rep, 0] = state
            out, new_state = _gdn_chunk_fn(q_ref[...], k_ref[...], v_ref[:, cols], gb_ref[:, cols], bb_ref[:, cols],
                                           gb_ref[:, rep * d:rep * d + GDN_CHUNK], state)
            o_ref[:, cols] = out
            carry[rep] = new_state

    qk, vh, st = _gdn_chunk_specs(lambda n: n)
    return pl.pallas_call(
        body, grid=(GDN_QK_HEADS, nc), in_specs=[qk, qk, vh, vh, vh], out_specs=[vh, st],
        out_shape=[jax.ShapeDtypeStruct((s, MIX_WIDTH), F32), jax.ShapeDtypeStruct((GDN_V_HEADS, nc, d, d), F32)],
        scratch_shapes=[pltpu.VMEM((GDN_REP, d, d), F32)],
        compiler_params=_cp(("parallel", "arbitrary")), name=name)(q, k, v, gb, bb)


def gdn_chunk_bwd(q, k, v, gb, bb, states, do, name):
    s = q.shape[0]
    nc, d = s // GDN_CHUNK, GDN_HEAD_DIM
    rev = lambda n: nc - 1 - n

    def body(q_ref, k_ref, v_ref, gb_ref, bb_ref, st_ref, do_ref, dq_ref, dk_ref, dv_ref, dg_ref, db_ref, carry):
        @pl.when(pl.program_id(1) == 0)
        def _():
            carry[...] = jnp.zeros_like(carry)

        d_qs, d_ks = [], []
        for rep in range(GDN_REP):
            cols = slice(rep * d, (rep + 1) * d)
            lead = slice(rep * d, rep * d + GDN_CHUNK)
            args = (q_ref[...], k_ref[...], v_ref[:, cols], gb_ref[:, cols], bb_ref[:, cols], gb_ref[:, lead],
                    st_ref[rep, 0])
            _, vjp = jax.vjp(_gdn_chunk_fn, *args)
            d_q, d_k, d_v, d_gb, d_bb, d_gb64, d_state = vjp((do_ref[:, cols], carry[rep]))
            carry[rep] = d_state
            dv_ref[:, cols] = d_v
            db_ref[:, cols] = d_bb
            dg_ref[:, cols] = d_gb
            dg_ref[:, lead] += d_gb64
            d_qs.append(d_q)
            d_ks.append(d_k)
        dq_ref[...] = sum(d_qs[1:], d_qs[0])
        dk_ref[...] = sum(d_ks[1:], d_ks[0])

    qk, vh, st = _gdn_chunk_specs(rev)
    qk_s, wide_s = jax.ShapeDtypeStruct((s, GDN_QK_WIDTH), F32), jax.ShapeDtypeStruct((s, MIX_WIDTH), F32)
    return pl.pallas_call(
        body, grid=(GDN_QK_HEADS, nc), in_specs=[qk, qk, vh, vh, vh, st, vh], out_specs=[qk, qk, vh, vh, vh],
        out_shape=[qk_s, qk_s, wide_s, wide_s, wide_s], scratch_shapes=[pltpu.VMEM((GDN_REP, d, d), F32)],
        compiler_params=_cp(("parallel", "arbitrary")), name=name)(q, k, v, gb, bb, states, do)


WEIGHTS = ['attn_norm', 'mem_norm', 'w_mem_kv', 'w_out', 'ffn_norm', 'w_ffn_up', 'ffn_conv', 'w_ffn_down', 'final_norm',
           'a_w_in', 'a_sinks', 'b_w_in', 'b_mu', 'b_w0', 'b_w_decay_up', 'b_a0', 'b_w_iclr_up', 'b_w_gate_up', 'b_k_k',
           'b_k_a', 'b_r_k', 'b_gn_g', 'b_gn_b', 'c_w_in', 'c_conv', 'c_a_log', 'c_dt_bias', 'c_norm_g']
INPUTS = ['x', 'mem'] + WEIGHTS + ['loss_target'] + ['m_' + n for n in WEIGHTS] + ['v_' + n for n in WEIGHTS]
REPLICATED = ['attn_norm', 'mem_norm', 'ffn_norm', 'final_norm', 'a_sinks', 'b_mu', 'b_w0', 'b_a0', 'b_k_k', 'b_k_a',
              'b_r_k', 'b_gn_g', 'b_gn_b', 'c_a_log', 'c_dt_bias', 'c_norm_g']
C_MIX = GDN_CONV_WIDTH + MIX_WIDTH
GATHER_ID = 1


def _cols_to_shards(full):
    rows, cols = full.shape
    return full.reshape(rows, N_DEV, cols // N_DEV).transpose(1, 0, 2)


def _shards_to_cols(g):
    return g.transpose(1, 0, 2).reshape(g.shape[1], N_DEV * g.shape[2])


def _pad_to(x, n, axis):
    pad = [(0, 0)] * x.ndim
    pad[axis] = (0, n - x.shape[axis])
    return jnp.pad(x, pad)


def _b_pad_cols(w):
    parts = [w[..., :4608], _pad_to(w[..., 4608:4704], LANES, -1), _pad_to(w[..., 4704:4800], LANES, -1), w[..., 4800:5056]]
    if w.shape[-1] > 5056:
        parts.append(w[..., 5056:])
    return jnp.concatenate(parts, axis=-1)


def _b_unpad_cols(w):
    parts = [w[..., :4608], w[..., RW_WD:RW_WD + RWKV_DECAY_RANK], w[..., RW_AD:RW_AD + RWKV_ICLR_RANK], w[..., RW_GD:RW_SHIFT]]
    if w.shape[-1] > RW_SHIFT:
        parts.append(w[..., RW_SHIFT:])
    return jnp.concatenate(parts, axis=-1)


def _c_pad_cols(w):
    return jnp.concatenate([w[..., :C_MIX], w[..., C_MIX + 24:], _pad_to(w[..., C_MIX:C_MIX + 12], LANES, -1),
                            _pad_to(w[..., C_MIX + 12:C_MIX + 24], LANES, -1)], axis=-1)


def _c_unpad_cols(w):
    return jnp.concatenate([w[..., :C_MIX], w[..., GD_BT:GD_BT + GDN_V_HEADS], w[..., GD_AT:GD_AT + GDN_V_HEADS],
                            w[..., GD_QMEM:GD_BT]], axis=-1)


def _pack(arrays):
    flat = jnp.concatenate([a.reshape(-1).astype(F32) for a in arrays])
    unit = SUBLANES * LANES
    return _pad_to(flat, -(-flat.size // unit) * unit, 0).reshape(-1, LANES)


def _unpack(packed, shapes):
    flat, out, at = packed.reshape(-1), [], 0
    for shp in shapes:
        n = int(np.prod(shp))
        out.append(flat[at:at + n].reshape(shp))
        at += n
    return out


def kernel(*args):
    a = dict(zip(INPUTS, args))
    x0, mem, target = a['x'][0], a['mem'][0], a['loss_target'][0]
    s = x0.shape[0]
    e64, e64t = _head_matrix(MIX_WIDTH, RWKV_HEAD_DIM)
    e6, e6t = _head_matrix(GDN_QK_WIDTH, GDN_HEAD_DIM)
    e12, e12t = _head_matrix(MIX_WIDTH, GDN_HEAD_DIM)
    trep = jnp.asarray((np.arange(LANES)[:, None] == np.arange(MIX_WIDTH)[None, :] % LANES).astype(np.float32))
    row = lambda v: v.reshape(1, -1)

    def in_proj_shard(l):
        kind, j = l % 3, l // 3
        return (a['a_w_in'], a['b_w_in'], a['c_w_in'])[kind][j]

    def small_shards(l):
        kind, j = l % 3, l // 3
        if kind == 1:
            return [a['b_w_decay_up'][j], a['b_w_iclr_up'][j], a['b_w_gate_up'][j]]
        if kind == 2:
            return [a['c_conv'][j]]
        return []

    gathered = []
    for l in range(DEPTH):
        big = [a['w_mem_kv'][l], a['w_out'][l], a['w_ffn_up'][l], a['w_ffn_down'][l], in_proj_shard(l)]
        shards = [w.astype(BF16) for w in big] + [a['ffn_conv'][l]] + small_shards(l)
        if gathered:
            shards, _ = lax.optimization_barrier((shards, gathered[-1]))
        gathered.append(all_gather_many_async(shards, f"gather_weights_{l}", GATHER_ID + l))

    def layer_weights(l, g):
        kind = l % 3
        w_in = _shards_to_cols(g[4])
        lw = dict(w_kv=g[0].reshape(D_MODEL, 2 * MEM_WIDTH), w_out=g[1].reshape(D_MODEL, D_MODEL),
                  w_up=_shards_to_cols(g[2]), w_down=g[3].reshape(D_FF, D_MODEL), conv=_shards_to_cols(g[5]))
        if kind == 0:
            lw['w_in'] = w_in
        elif kind == 1:
            lw['w_in'] = _b_pad_cols(w_in)
            lw['wdu'] = _pad_to(_shards_to_cols(g[6]), LANES, 0)
            lw['wiu'] = _pad_to(_shards_to_cols(g[7]), LANES, 0)
            lw['wgu'] = _shards_to_cols(g[8])
        else:
            lw['w_in'] = _c_pad_cols(w_in)
            lw['c_conv'] = _shards_to_cols(g[6])
        return lw

    def rwkv_params(j, lw):
        return (row(_b_pad_cols(a['b_mu'][j])), row(a['b_w0'][j]), lw['wdu'], row(a['b_a0'][j]), lw['wiu'], lw['wgu'],
                row(a['b_k_k'][j]), row(a['b_k_a'][j]), e64, e64t)

    def rwkv_post_params(j):
        return (row(a['b_gn_g'][j]), row(a['b_gn_b'][j]), row(a['b_r_k'][j]), e64, e64t)

    def gdn_params(j, lw):
        return (lw['c_conv'], _pad_lanes(a['c_a_log'][j]), _pad_lanes(a['c_dt_bias'][j]), e6, e6t, e12t)

    def gdn_post_params(j):
        return (jnp.tile(row(a['c_norm_g'][j]), (SUBLANES, 1)), e12, e12t, trep)

    x = x0
    saved, layers = [], []
    for l in range(DEPTH):
        kind, j = l % 3, l // 3
        g = gathered[l]
        if l > 0:
            x, g = lax.optimization_barrier((x, g))
        lw = layer_weights(l, g)
        layers.append(lw)
        sv = dict(x=x)
        h = rmsnorm_fwd(x, a['attn_norm'][l], BF16, f"attn_norm_{l}")
        memn = rmsnorm_fwd(mem, a['mem_norm'][l], BF16, f"mem_norm_{l}")
        mem_kv = mm(memn, lw['w_kv'], name=f"mem_kv_{l}")
        p = mm(h, lw['w_in'], name=f"in_proj_{l}")
        if kind == 0:
            y = swa_fwd(p, a['a_sinks'][j], f"swa_{l}")
            q_col = MIX_WIDTH + 2 * SWA_KV_HEADS * SWA_HEAD_DIM
        elif kind == 1:
            pre = rwkv_pre_fwd(p, rwkv_params(j, lw), f"rwkv_pre_{l}")
            yscan, ck = rwkv_scan_fwd(*pre[:6], f"rwkv_scan_{l}")
            post_in = (yscan, pre[0], pre[2], pre[3], pre[6])
            y = rwkv_post_fwd(post_in, rwkv_post_params(j), f"rwkv_post_{l}")
            sv.update(pre=pre, ck=ck, post_in=post_in)
            q_col = RW_SHIFT
        else:
            pre = gdn_pre_fwd(p, gdn_params(j, lw), f"gdn_pre_{l}")
            o, states = gdn_chunk_fwd(*pre, f"gdn_chunk_{l}")
            y = gdn_post_fwd(o, p, gdn_post_params(j), f"gdn_post_{l}")
            sv.update(pre=pre, o=o, states=states)
            q_col = GD_QMEM
        y_mem = mem_attn_fwd(p, q_col, mem_kv, f"mem_attn_{l}")
        cat = jnp.concatenate([y, y_mem], axis=1)
        x1 = mm(cat, lw['w_out'], res=x, name=f"out_proj_{l}")
        hf = rmsnorm_fwd(x1, a['ffn_norm'][l], BF16, f"ffn_norm_{l}")
        u0 = mm(hf, lw['w_up'], name=f"ffn_up_{l}")
        act = ffn_act_fwd(u0, lw['conv'], f"ffn_act_{l}")
        x = mm(act, lw['w_down'], res=x1, name=f"ffn_down_{l}")
        sv.update(h=h, memn=memn, mem_kv=mem_kv, p=p, q_col=q_col, cat=cat, x1=x1, hf=hf, u0=u0, act=act)
        saved.append(sv)

    loss_part, dx, d_final_norm = final_loss(x, a['final_norm'], target, "final_loss")

    rep_grads = {n: [None] * a[n].shape[0] for n in ('attn_norm', 'mem_norm', 'ffn_norm', 'a_sinks')}
    rep_grads['final_norm'] = d_final_norm
    results = {}
    exchanged, pending = {}, {}

    def apply_adam(name, idx, pieces, tag):
        w, m, v = a[name][idx], a['m_' + name][idx], a['v_' + name][idx]
        shp = w.shape
        two_d = (-1, shp[-1])
        out = adamw_sum(pieces.reshape((N_DEV,) + w.reshape(two_d).shape), w.reshape(two_d), m.reshape(two_d),
                        v.reshape(two_d), f"adamw_{name}_{tag}")
        results.setdefault(name, {})[idx] = [o.reshape(shp) for o in out]

    for l in reversed(range(DEPTH)):
        kind, j = l % 3, l // 3
        lw, sv = layers[l], saved[l]
        p, q_col = sv['p'], sv['q_col']
        d_act = mm(dx, lw['w_down'], tb=True, name=f"d_ffn_act_{l}")
        dw_down = mm(sv['act'], dx, ta=True, out_dtype=BF16, name=f"dw_ffn_down_{l}")
        dug, duv, dcg, dcv = ffn_act_bwd(sv['u0'], lw['conv'], d_act, f"ffn_act_bwd_{l}")
        du0 = jnp.concatenate([dug, duv], axis=1)
        d_conv = jnp.concatenate([dcg, dcv], axis=1)
        d_hf = mm(du0, lw['w_up'], tb=True, name=f"d_ffn_norm_out_{l}")
        dw_up = mm(sv['hf'], du0, ta=True, out_dtype=BF16, name=f"dw_ffn_up_{l}")
        dx1, rep_grads['ffn_norm'][l] = rmsnorm_bwd(sv['x1'], a['ffn_norm'][l], d_hf, dx, f"ffn_norm_bwd_{l}")
        dcat = mm(dx1, lw['w_out'], tb=True, name=f"d_cat_{l}")
        dw_out = mm(sv['cat'], dx1, ta=True, out_dtype=BF16, name=f"dw_out_{l}")
        dq_mem, d_mem_kv = mem_attn_bwd(p, q_col, sv['mem_kv'], dcat, f"mem_attn_bwd_{l}")
        small_grads = []
        if kind == 0:
            dq, dk, dv, rep_grads['a_sinks'][j] = swa_bwd(p, a['a_sinks'][j], dcat, f"swa_bwd_{l}")
            dp = jnp.concatenate([dq, dk, dv, dq_mem], axis=1)
        elif kind == 1:
            post = rwkv_post_bwd(sv['post_in'], rwkv_post_params(j), dcat, f"rwkv_post_bwd_{l}")
            scan = rwkv_scan_bwd(*sv['pre'][:6], sv['ck'], post[0], f"rwkv_scan_bwd_{l}")
            res = rwkv_pre_bwd(p, rwkv_params(j, lw), tuple(scan) + tuple(post[1:5]), f"rwkv_pre_bwd_{l}")
            dp_mix = shift_add(res[0], [res[1]], [1], BF16, f"rwkv_shift_bwd_{l}")
            dp = jnp.concatenate([dp_mix, dq_mem], axis=1)
            for n, val in zip(('b_mu', 'b_w0', 'b_a0', 'b_k_k', 'b_k_a'), (_b_unpad_cols(res[2]), res[3], res[5], res[8], res[9])):
                rep_grads[n] = val
            rep_grads.update(b_gn_g=post[5], b_gn_b=post[6], b_r_k=post[7])
            small_grads = [_cols_to_shards(res[4][:RWKV_DECAY_RANK]), _cols_to_shards(res[6][:RWKV_ICLR_RANK]),
                           _cols_to_shards(res[7])]
        else:
            d_o, dz, rep_grads['c_norm_g'] = gdn_post_bwd(sv['o'], p, gdn_post_params(j), dcat, f"gdn_post_bwd_{l}")
            chunk = gdn_chunk_bwd(*sv['pre'], sv['states'], d_o, f"gdn_chunk_bwd_{l}")
            res = gdn_pre_bwd(p, gdn_params(j, lw), chunk, f"gdn_pre_bwd_{l}")
            dqkv = shift_add(res[0], list(res[1:4]), [1, 2, 3], BF16, f"gdn_shift_bwd_{l}")
            dp = jnp.concatenate([dqkv, dz, dq_mem, res[4]], axis=1)
            rep_grads.update(c_a_log=res[6][:, :GDN_V_HEADS], c_dt_bias=res[7][:, :GDN_V_HEADS])
            small_grads = [_cols_to_shards(res[5])]
        d_h = mm(dp, lw['w_in'], tb=True, name=f"d_attn_norm_out_{l}")
        dw_in = mm(sv['h'], dp, ta=True, out_dtype=BF16, name=f"dw_in_{l}")
        dx, rep_grads['attn_norm'][l] = rmsnorm_bwd(sv['x'], a['attn_norm'][l], d_h, dx1, f"attn_norm_bwd_{l}")
        d_memn = mm(d_mem_kv, lw['w_kv'], tb=True, name=f"d_mem_norm_out_{l}")
        dw_kv = mm(sv['memn'], d_mem_kv, ta=True, out_dtype=BF16, name=f"dw_mem_kv_{l}")
        _, rep_grads['mem_norm'][l] = rmsnorm_bwd(mem, a['mem_norm'][l], d_memn, None, f"mem_norm_bwd_{l}")

        if kind == 1:
            dw_in = _b_unpad_cols(dw_in)
        elif kind == 2:
            dw_in = _c_unpad_cols(dw_in)
        pieces = [dw_kv.reshape(N_DEV, -1, 2 * MEM_WIDTH), dw_out.reshape(N_DEV, -1, D_MODEL), _cols_to_shards(dw_up),
                  dw_down.reshape(N_DEV, -1, D_MODEL), _cols_to_shards(dw_in), _cols_to_shards(d_conv)] + small_grads
        if l + 1 < DEPTH:
            exchanged[l + 1] = all_to_all_wait(*pending.pop(l + 1), dx, f"exchange_wait_{l + 1}")
        send, recv, thru, lands, token = all_to_all_start(pieces, f"exchange_start_{l}")
        dx, token = lax.optimization_barrier((dx, token))
        pending[l] = (send, recv, thru, lands)

    for l in reversed(range(DEPTH)):
        kind, j = l % 3, l // 3
        if l == 0:
            exchanged[0] = all_to_all_wait(*pending.pop(0), results['w_ffn_up'][1][0], "exchange_wait_0")
        got = exchanged[l]
        in_name = ('a_w_in', 'b_w_in', 'c_w_in')[kind]
        for name, idx, pc in (('w_mem_kv', l, got[0]), ('w_out', l, got[1]), ('w_ffn_up', l, got[2]),
                              ('w_ffn_down', l, got[3]), (in_name, j, got[4]), ('ffn_conv', l, got[5])):
            apply_adam(name, idx, pc, l)
        if kind == 1:
            for name, pc in zip(('b_w_decay_up', 'b_w_iclr_up', 'b_w_gate_up'), got[6:]):
                apply_adam(name, j, pc, l)
        elif kind == 2:
            apply_adam('c_conv', j, got[6], l)

    rep_vals = []
    for n in REPLICATED:
        gval = rep_grads[n]
        gval = jnp.stack(gval) if isinstance(gval, list) else gval
        rep_vals.append(gval.reshape(a[n].shape))
    shapes = [a[n].shape for n in REPLICATED] + [(1,)]
    part = _pack(rep_vals + [loss_part.reshape(1)])
    gathered = all_gather_many([part], "gather_small_grads")[0]
    zero = jnp.zeros((1,), F32)
    packed = lambda pre: _pack([a[pre + n] for n in REPLICATED] + [zero])
    rep_out = adamw_sum(gathered, packed(''), packed('m_'), packed('v_'), "adamw_replicated")
    rep_out = [_unpack(o, shapes) for o in rep_out]
    loss = rep_out[0][-1][0]
    for i, n in enumerate(REPLICATED):
        results[n] = [o[i] for o in rep_out]

    def leaf(name, which):
        r = results[name]
        if isinstance(r, dict):
            return jnp.stack([r[i][which] for i in range(len(r))])
        return r[which]

    outs = [loss, dx[None]]
    for which in range(4):
        outs += [leaf(n, which) for n in WEIGHTS]
    return tuple(outs)
```

```python
import functools

import numpy as np
import jax
import jax.numpy as jnp
from jax import lax
from jax.experimental import pallas as pl
from jax.experimental.pallas import tpu as pltpu
from jax.experimental.pallas import tpu_sc as plsc

F32, BF16 = jnp.float32, jnp.bfloat16
HI = lax.Precision.HIGHEST
V7X_VMEM_BYTES = 64 * 1024 * 1024
VMEM_LIMIT = V7X_VMEM_BYTES - 8 * 1024 * 1024
MM_TILE_BUDGET = 40 * 1024 * 1024
SUBLANES, LANES = 8, 128
N_DEV = 8

D_MODEL = 2048
DEPTH = 4
MIX_WIDTH = 1536
MEM_HEADS, MEM_HEAD_DIM, MEM_WIDTH = 4, 128, 512
NORM_EPS = 1e-6
SWA_HEAD_DIM, SWA_Q_HEADS, SWA_KV_HEADS, SWA_GROUP, SWA_BLOCK = 64, 24, 4, 6, 128
RWKV_HEADS, RWKV_HEAD_DIM, RWKV_GN_EPS = 24, 64, 64e-5
RWKV_DECAY_RANK, RWKV_ICLR_RANK, RWKV_GATE_RANK = 96, 96, 256
GDN_HEAD_DIM, GDN_V_HEADS, GDN_QK_HEADS, GDN_CONV, GDN_CHUNK = 128, 12, 6, 4, 64
GDN_QK_WIDTH = GDN_QK_HEADS * GDN_HEAD_DIM
GDN_CONV_WIDTH = 2 * GDN_QK_WIDTH + MIX_WIDTH
D_FF, FFN_CONV = 5632, 3
ADAM_LR, ADAM_B1, ADAM_B2, ADAM_EPS, ADAM_WD, ADAM_STEP = 0.001, 0.9, 0.999, 1e-08, 0.01, 10
MESH = pl.DeviceIdType.MESH


def _cp(sem=None):
    return pltpu.CompilerParams(dimension_semantics=sem, vmem_limit_bytes=VMEM_LIMIT)


def _tile(n, cands):
    for c in cands:
        if n % c == 0:
            return c
    return n


def _dot(a, b):
    return jnp.dot(a, b, precision=HI, preferred_element_type=F32)


def _dot_nt(a, b):
    return lax.dot_general(a, b, (((1,), (1,)), ((), ())), precision=HI, preferred_element_type=F32)


def _dot_tn(a, b):
    return lax.dot_general(a, b, (((0,), (0,)), ((), ())), precision=HI, preferred_element_type=F32)


def _sigmoid(x):
    return 1.0 / (1.0 + jnp.exp(-x))


def _softplus(x):
    return jnp.maximum(x, 0.0) + jnp.log(1.0 + jnp.exp(-jnp.abs(x)))


def _silu(x):
    return x * _sigmoid(x)


def mm(a, b, *, ta=False, tb=False, res=None, out_dtype=F32, name):
    (k_a, m) = a.shape if ta else a.shape[::-1]
    (k_b, n) = b.shape[::-1] if tb else b.shape
    assert k_a == k_b, (a.shape, b.shape, ta, tb)
    kdim = k_a
    tm = _tile(m, (1024, 512, 256))
    tn = _tile(n, (1024, 768, 512, 384, 256, 128))

    def vmem_bytes(tk):
        tiles = tk * (tm * a.dtype.itemsize + tn * b.dtype.itemsize) + tm * tn * jnp.dtype(out_dtype).itemsize
        tiles += 0 if res is None else tm * tn * res.dtype.itemsize
        return 2 * tiles + (0 if tk == kdim else tm * tn * 4)

    tks = [t for t in (kdim, kdim // 2, kdim // 4, 1024, 512, 256, 128) if kdim % t == 0 and t % LANES == 0]
    tk = next(t for t in tks if vmem_bytes(t) <= MM_TILE_BUDGET)
    nk = kdim // tk
    dims = (((0 if ta else 1,), (1 if tb else 0,)), ((), ()))

    def body(*refs):
        a_ref, b_ref = refs[:2]
        r_ref = None if res is None else refs[2]
        o_ref = refs[2 if res is None else 3]
        part = lax.dot_general(a_ref[...].astype(BF16), b_ref[...].astype(BF16), dims, preferred_element_type=F32)

        def finish(total):
            o_ref[...] = (total if res is None else total + r_ref[...]).astype(o_ref.dtype)

        if nk == 1:
            finish(part)
            return
        acc, kk = refs[-1], pl.program_id(2)

        @pl.when(kk == 0)
        def _():
            acc[...] = part

        @pl.when((kk > 0) & (kk < nk - 1))
        def _():
            acc[...] += part

        @pl.when(kk == nk - 1)
        def _():
            finish(acc[...] + part)

    a_spec = pl.BlockSpec((tk, tm), lambda i, j, k: (k, i)) if ta else pl.BlockSpec((tm, tk), lambda i, j, k: (i, k))
    b_spec = pl.BlockSpec((tn, tk), lambda i, j, k: (j, k)) if tb else pl.BlockSpec((tk, tn), lambda i, j, k: (k, j))
    o_spec = pl.BlockSpec((tm, tn), lambda i, j, k: (i, j))
    in_specs, args = [a_spec, b_spec], [a, b]
    if res is not None:
        in_specs.append(o_spec)
        args.append(res)
    return pl.pallas_call(
        body, grid=(m // tm, n // tn, nk), in_specs=in_specs, out_specs=o_spec,
        out_shape=jax.ShapeDtypeStruct((m, n), out_dtype),
        scratch_shapes=[] if nk == 1 else [pltpu.VMEM((tm, tn), F32)],
        compiler_params=_cp(("parallel", "parallel", "arbitrary")), name=name)(*args)


def _coords():
    return lax.axis_index("x"), lax.axis_index("y"), lax.axis_index("c")


def _block_index(p):
    return 4 * p[0] + 2 * p[1] + p[2]


def _all_gather_body(x_refs, o_refs, send, recv, loc):
    n = len(x_refs)
    x, y, c = _coords()
    me, sib = (x, y, c), (x, y, 1 - c)
    chips = [(1 - x, y), (x, 1 - y), (1 - x, 1 - y)]

    def cp(i, k, block, to, src=None):
        dst = o_refs[i].at[_block_index(block)]
        return pltpu.make_async_remote_copy(
            src_ref=dst if src is None else src, dst_ref=dst, send_sem=send.at[i, k], recv_sem=recv.at[i, k],
            device_id=to, device_id_type=MESH)

    mine = [pltpu.make_async_copy(x_refs[i], o_refs[i].at[_block_index(me)], loc.at[i]) for i in range(n)]
    for m_ in mine:
        m_.start()
    first = []
    for i in range(n):
        first.append(cp(i, 0, me, sib, src=x_refs[i]))
        for j, chip in enumerate(chips):
            first.append(cp(i, 1 + j, me, (*chip, c), src=x_refs[i]))
    for f in first:
        f.start()
    passed = []
    for j, chip in enumerate(chips):
        for i in range(n):
            cp(i, 1 + j, (*chip, c), me).wait_recv()
            p = cp(i, 4 + j, (*chip, c), sib)
            p.start()
            passed.append(p)
    for i in range(n):
        cp(i, 0, sib, me).wait_recv()
        for j, chip in enumerate(chips):
            cp(i, 4 + j, (*chip, 1 - c), me).wait_recv()
    for f in first + passed:
        f.wait_send()
    for m_ in mine:
        m_.wait()


def _all_gather_peers():
    x, y, c = _coords()
    return [(x, y, 1 - c), (1 - x, y, c), (x, 1 - y, c), (1 - x, 1 - y, c)]


def _all_to_all_peers():
    x, y, c = _coords()
    return [(1 - x if r & 4 else x, 1 - y if r & 2 else y, 1 - c if r & 1 else c) for r in range(1, N_DEV)]


def _comm_scratch(n):
    return [pltpu.SemaphoreType.DMA((n, 7)), pltpu.SemaphoreType.DMA((n, 7)), pltpu.SemaphoreType.DMA((n,))]


def all_gather_many(xs, name):
    n = len(xs)

    def body(*refs):
        _all_gather_body(refs[:n], refs[n:2 * n], *refs[2 * n:])

    any_spec = pl.BlockSpec(memory_space=pl.ANY)
    return pl.pallas_call(
        body, in_specs=[any_spec] * n, out_specs=[any_spec] * n,
        out_shape=[jax.ShapeDtypeStruct((N_DEV,) + x.shape, x.dtype) for x in xs],
        scratch_shapes=_comm_scratch(n), name=name)(*xs)


def _on_sequencer(exchange, peers, xs, out_shapes, name, collective_id):
    x_refs = [jax.new_ref(x, memory_space=pltpu.MemorySpace.HBM) for x in xs]
    o_refs = [jax.empty_ref(s, memory_space=pltpu.MemorySpace.HBM) for s in out_shapes]

    @pl.kernel(mesh=plsc.ScalarSubcoreMesh(axis_name="sequencer", num_cores=1), name=name,
               scratch_types=tuple(_comm_scratch(len(xs))),
               compiler_params=pltpu.CompilerParams(collective_id=collective_id))
    def launch(send, recv, loc):
        barrier = pltpu.get_barrier_semaphore()
        ids = peers()
        for peer in ids:
            pl.semaphore_signal(barrier, inc=1, device_id=peer, device_id_type=MESH)
        pl.semaphore_wait(barrier, len(ids))
        exchange(x_refs, o_refs, send, recv, loc)

    launch()
    return [o[...] for o in o_refs]


def _all_to_all_copies(x_refs, o_refs, send, recv, arrivals):
    x, y, c = _coords()
    me = _block_index((x, y, c))
    copies = []
    for r, peer in enumerate(_all_to_all_peers()):
        pidx = _block_index(peer)
        for i in range(len(x_refs)):
            copies.append(pltpu.make_async_remote_copy(
                src_ref=x_refs[i].at[pidx], dst_ref=o_refs[i].at[pidx if arrivals else me],
                send_sem=send.at[7 * i + r], recv_sem=recv.at[7 * i + r], device_id=peer, device_id_type=MESH))
    return copies


def all_to_all_start(xs, name):
    n = len(xs)
    x, y, c = _coords()
    me = _block_index((x, y, c))
    lands = [lax.dynamic_update_slice_in_dim(lax.empty(v.shape, v.dtype), lax.dynamic_slice_in_dim(v, me, 1, 0), me, 0)
             for v in xs]

    def body(*refs):
        x_refs, o_refs = refs[:n], refs[n:2 * n]
        send, recv = refs[2 * n], refs[2 * n + 1]
        token = refs[-1]
        for s in _all_to_all_copies(x_refs, o_refs, send, recv, arrivals=False):
            s.start()
        token[...] = jnp.zeros_like(token)

    hbm = pl.BlockSpec(memory_space=pltpu.HBM)
    sem = pl.BlockSpec(memory_space=pltpu.SEMAPHORE)
    out = pl.pallas_call(
        body, name=name,
        out_shape=(pltpu.SemaphoreType.DMA((7 * n,)), pltpu.SemaphoreType.DMA((7 * n,)),
                   *[pltpu.HBM(v.shape, v.dtype) for v in xs], *[pltpu.HBM(v.shape, v.dtype) for v in xs],
                   jax.ShapeDtypeStruct((SUBLANES, LANES), F32)),
        in_specs=[hbm] * (2 * n), out_specs=(sem, sem, *([hbm] * (2 * n)), pl.BlockSpec(memory_space=pltpu.VMEM)),
        input_output_aliases={i: 2 + i for i in range(2 * n)},
        compiler_params=pltpu.CompilerParams(has_side_effects=pltpu.SideEffectType.DATAFLOW_SIDE_EFFECTING),
    )(*[pltpu.with_memory_space_constraint(v, pltpu.HBM) for v in xs],
      *[pltpu.with_memory_space_constraint(v, pltpu.HBM) for v in lands])
    return out[0], out[1], list(out[2:2 + n]), list(out[2 + n:2 + 2 * n]), out[-1]


def all_to_all_wait(send, recv, xs, lands, after, name):
    n = len(xs)

    def body(*refs):
        x_refs, o_refs = refs[:n], refs[n:2 * n]
        send_ref, recv_ref = refs[2 * n], refs[2 * n + 1]
        for s in _all_to_all_copies(x_refs, o_refs, send_ref, recv_ref, arrivals=False):
            s.wait_send()
        for w in _all_to_all_copies(x_refs, o_refs, send_ref, recv_ref, arrivals=True):
            w.wait_recv()

    hbm = pl.BlockSpec(memory_space=pltpu.HBM)
    sem = pl.BlockSpec(memory_space=pltpu.SEMAPHORE)
    out = pl.pallas_call(
        body, name=name,
        out_shape=tuple(pltpu.HBM(v.shape, v.dtype) for v in list(xs) + list(lands)),
        in_specs=[hbm] * (2 * n) + [sem, sem, pl.BlockSpec(memory_space=pl.ANY)], out_specs=tuple([hbm] * (2 * n)),
        input_output_aliases={i: i for i in range(2 * n)},
        compiler_params=pltpu.CompilerParams(has_side_effects=pltpu.SideEffectType.DATAFLOW_SIDE_EFFECTING),
    )(*xs, *lands, send, recv, after)
    return list(out[n:])


def all_gather_many_async(xs, name, collective_id):
    shapes = [jax.ShapeDtypeStruct((N_DEV,) + x.shape, x.dtype) for x in xs]
    return _on_sequencer(_all_gather_body, _all_gather_peers, xs, shapes, name, collective_id)


def adamw_sum(pieces, w, m, v, name):
    rows, cols = w.shape
    tr = _tile(rows, (128, 64, 32, 16, 8))
    c1 = 1.0 - ADAM_B1 ** ADAM_STEP
    c2 = 1.0 - ADAM_B2 ** ADAM_STEP

    def body(p_ref, w_ref, m_ref, v_ref, g_out, d_out, m_out, v_out):
        g = p_ref[0].astype(F32)
        for s in range(1, N_DEV):
            g = g + p_ref[s].astype(F32)
        m_new = ADAM_B1 * m_ref[...] + (1.0 - ADAM_B1) * g
        v_new = ADAM_B2 * v_ref[...] + (1.0 - ADAM_B2) * (g * g)
        m_hat = m_new / c1
        v_hat = v_new / c2
        g_out[...] = g
        d_out[...] = -ADAM_LR * (m_hat / (jnp.sqrt(v_hat) + ADAM_EPS) + ADAM_WD * w_ref[...])
        m_out[...] = m_new
        v_out[...] = v_new

    spec = pl.BlockSpec((tr, cols), lambda i: (i, 0))
    out = jax.ShapeDtypeStruct((rows, cols), F32)
    return pl.pallas_call(
        body, grid=(rows // tr,), in_specs=[pl.BlockSpec((N_DEV, tr, cols), lambda i: (0, i, 0)), spec, spec, spec],
        out_specs=[spec] * 4, out_shape=[out] * 4, compiler_params=_cp(("parallel",)), name=name)(pieces, w, m, v)


def rmsnorm_fwd(x, g, out_dtype, name):
    s, d = x.shape
    tr = _tile(s, (256, 128, 64, 32, 16))

    def body(x_ref, g_ref, o_ref):
        xv = x_ref[...]
        rstd = lax.rsqrt(jnp.mean(xv * xv, axis=-1, keepdims=True) + NORM_EPS)
        o_ref[...] = (xv * rstd * g_ref[...]).astype(o_ref.dtype)

    return pl.pallas_call(
        body, grid=(s // tr,), in_specs=[pl.BlockSpec((tr, d), lambda i: (i, 0)), pl.BlockSpec((1, d), lambda i: (0, 0))],
        out_specs=pl.BlockSpec((tr, d), lambda i: (i, 0)), out_shape=jax.ShapeDtypeStruct((s, d), out_dtype),
        compiler_params=_cp(("parallel",)), name=name)(x, g.reshape(1, d))


def rmsnorm_bwd(x, g, dh, dres, name):
    s, d = x.shape
    tr = _tile(s, (256, 128, 64, 32, 16))

    def body(*refs):
        if dres is None:
            x_ref, g_ref, dh_ref, dx_ref, dg_ref = refs
        else:
            x_ref, g_ref, dh_ref, dr_ref, dx_ref, dg_ref = refs
        xv = x_ref[...]
        rstd = lax.rsqrt(jnp.mean(xv * xv, axis=-1, keepdims=True) + NORM_EPS)
        xhat = xv * rstd
        dhv = dh_ref[...].astype(F32)
        dhg = dhv * g_ref[...]
        dx = rstd * (dhg - xhat * jnp.mean(dhg * xhat, axis=-1, keepdims=True))
        if dres is not None:
            dx = dx + dr_ref[...]
        dx_ref[...] = dx

        @pl.when(pl.program_id(0) == 0)
        def _():
            dg_ref[...] = jnp.zeros_like(dg_ref)

        dg_ref[...] += jnp.sum(dhv * xhat, axis=0, keepdims=True)

    row = pl.BlockSpec((tr, d), lambda i: (i, 0))
    vec = pl.BlockSpec((1, d), lambda i: (0, 0))
    ins = [x, g.reshape(1, d), dh] + ([] if dres is None else [dres])
    dx, dg = pl.pallas_call(
        body, grid=(s // tr,), in_specs=[row, vec, row] + ([] if dres is None else [row]), out_specs=[row, vec],
        out_shape=[jax.ShapeDtypeStruct((s, d), F32), jax.ShapeDtypeStruct((1, d), F32)],
        compiler_params=_cp(("arbitrary",)), name=name)(*ins)
    return dx, dg.reshape(d)


def final_loss(x, g, target, name):
    s, d = x.shape
    tr = _tile(s, (256, 128, 64, 32, 16))

    def body(x_ref, g_ref, t_ref, l_ref, dx_ref, dg_ref):
        xv = x_ref[...]
        rstd = lax.rsqrt(jnp.mean(xv * xv, axis=-1, keepdims=True) + NORM_EPS)
        xhat = xv * rstd
        err = xhat * g_ref[...] - t_ref[...]
        dy = err * (1.0 / d)
        dhg = dy * g_ref[...]
        dx_ref[...] = rstd * (dhg - xhat * jnp.mean(dhg * xhat, axis=-1, keepdims=True))

        @pl.when(pl.program_id(0) == 0)
        def _():
            dg_ref[...] = jnp.zeros_like(dg_ref)
            l_ref[...] = jnp.zeros_like(l_ref)

        dg_ref[...] += jnp.sum(dy * xhat, axis=0, keepdims=True)
        part = 0.5 * jnp.sum(jnp.mean(err * err, axis=-1, keepdims=True), axis=0, keepdims=True)
        l_ref[...] += jnp.broadcast_to(part, l_ref.shape)

    row = pl.BlockSpec((tr, d), lambda i: (i, 0))
    vec = pl.BlockSpec((1, d), lambda i: (0, 0))
    lspec = pl.BlockSpec((1, LANES), lambda i: (0, 0))
    loss, dx, dg = pl.pallas_call(
        body, grid=(s // tr,), in_specs=[row, vec, row], out_specs=[lspec, row, vec],
        out_shape=[jax.ShapeDtypeStruct((1, LANES), F32), jax.ShapeDtypeStruct((s, d), F32),
                   jax.ShapeDtypeStruct((1, d), F32)],
        compiler_params=_cp(("arbitrary",)), name=name)(x, g.reshape(1, d), target)
    return loss[0, 0], dx, dg.reshape(d)


def _shift_down(tile, prev8, j):
    if j == 0:
        return tile
    rt = pltpu.roll(tile, j, 0)
    rp = pltpu.roll(prev8, j, 0)
    rows = lax.broadcasted_iota(jnp.int32, prev8.shape, 0)
    top = jnp.where(rows < j, rp, rt[:SUBLANES])
    return jnp.concatenate([top, rt[SUBLANES:]], axis=0)


def _shift_up(tile, next8, j):
    if j == 0:
        return tile
    t = tile.shape[0]
    rt = pltpu.roll(tile, t - j, 0)
    rn = pltpu.roll(next8, SUBLANES - j, 0)
    rows = lax.broadcasted_iota(jnp.int32, next8.shape, 0)
    bot = jnp.where(rows >= SUBLANES - j, rn, rt[t - SUBLANES:])
    return jnp.concatenate([rt[:t - SUBLANES], bot], axis=0)


def _halo_specs(t_rows, s_rows, cols, col_of):
    per, last = t_rows // SUBLANES, s_rows // SUBLANES - 1
    prev = pl.BlockSpec((SUBLANES, cols), lambda j, i: (jnp.maximum(i * per - 1, 0), col_of(j)))
    nxt = pl.BlockSpec((SUBLANES, cols), lambda j, i: (jnp.minimum((i + 1) * per, last), col_of(j)))
    return prev, nxt


def ffn_act_fwd(u0, conv, name):
    s, two_f = u0.shape
    f = two_f // 2
    t, c = _tile(s, (256, 128, 64)), 512
    nc = f // c

    def body(g_ref, v_ref, gp_ref, vp_ref, wg_ref, wv_ref, a_ref):
        first = (pl.program_id(1) > 0).astype(F32)

        def conv_of(x_ref, p_ref, w_ref):
            x, p = x_ref[...], p_ref[...] * first
            return (w_ref[0:1, :] * _shift_down(x, p, 2) + w_ref[1:2, :] * _shift_down(x, p, 1) + w_ref[2:3, :] * x)

        ug = conv_of(g_ref, gp_ref, wg_ref)
        uv = conv_of(v_ref, vp_ref, wv_ref)
        a_ref[...] = (_silu(ug) * uv).astype(a_ref.dtype)

    gate = pl.BlockSpec((t, c), lambda j, i: (i, j))
    val = pl.BlockSpec((t, c), lambda j, i: (i, j + nc))
    gp, _ = _halo_specs(t, s, c, lambda j: j)
    vp, _ = _halo_specs(t, s, c, lambda j: j + nc)
    wg = pl.BlockSpec((FFN_CONV, c), lambda j, i: (0, j))
    wv = pl.BlockSpec((FFN_CONV, c), lambda j, i: (0, j + nc))
    return pl.pallas_call(
        body, grid=(nc, s // t), in_specs=[gate, val, gp, vp, wg, wv], out_specs=pl.BlockSpec((t, c), lambda j, i: (i, j)),
        out_shape=jax.ShapeDtypeStruct((s, f), BF16), compiler_params=_cp(("parallel", "parallel")),
        name=name)(u0, u0, u0, u0, conv, conv)


def ffn_act_bwd(u0, conv, da, name):
    s, two_f = u0.shape
    f = two_f // 2
    t, c = _tile(s, (256, 128, 64)), 512
    nc, nt = f // c, s // t

    def body(g_ref, v_ref, gp_ref, vp_ref, gn_ref, vn_ref, wg_ref, wv_ref, da_ref, dan_ref,
             dg_ref, dv_ref, dwg_ref, dwv_ref):
        i = pl.program_id(1)
        first, last = (i > 0).astype(F32), (i < nt - 1).astype(F32)
        zeros8 = jnp.zeros((SUBLANES, c), F32)

        def ext(x_ref, p_ref, n_ref):
            return jnp.concatenate([p_ref[...] * first, x_ref[...], n_ref[...] * last], axis=0)

        def taps(e):
            return pltpu.roll(e, 2, 0), pltpu.roll(e, 1, 0), e

        def conv_of(sh, w_ref):
            return w_ref[0:1, :] * sh[0] + w_ref[1:2, :] * sh[1] + w_ref[2:3, :] * sh[2]

        def conv_t(du, w_ref):
            n = du.shape[0]
            return w_ref[2:3, :] * du + w_ref[1:2, :] * pltpu.roll(du, n - 1, 0) + w_ref[0:1, :] * pltpu.roll(du, n - 2, 0)

        sg, sv = taps(ext(g_ref, gp_ref, gn_ref)), taps(ext(v_ref, vp_ref, vn_ref))
        ug, uv = conv_of(sg, wg_ref), conv_of(sv, wv_ref)
        dae = jnp.concatenate([zeros8, da_ref[...], dan_ref[...] * last], axis=0)
        sig = _sigmoid(ug)
        dug = dae * uv * (sig * (1.0 + ug * (1.0 - sig)))
        duv = dae * (ug * sig)
        dg_ref[...] = conv_t(dug, wg_ref)[SUBLANES:t + SUBLANES].astype(dg_ref.dtype)
        dv_ref[...] = conv_t(duv, wv_ref)[SUBLANES:t + SUBLANES].astype(dv_ref.dtype)

        @pl.when(i == 0)
        def _():
            dwg_ref[...] = jnp.zeros_like(dwg_ref)
            dwv_ref[...] = jnp.zeros_like(dwv_ref)

        def dconv(du, sh):
            d = du[SUBLANES:t + SUBLANES]
            return jnp.concatenate([jnp.sum(d * x[SUBLANES:t + SUBLANES], axis=0, keepdims=True) for x in sh], axis=0)

        dwg_ref[...] += dconv(dug, sg)
        dwv_ref[...] += dconv(duv, sv)

    gate = pl.BlockSpec((t, c), lambda j, i: (i, j))
    val = pl.BlockSpec((t, c), lambda j, i: (i, j + nc))
    gp, gn = _halo_specs(t, s, c, lambda j: j)
    vp, vn = _halo_specs(t, s, c, lambda j: j + nc)
    wg = pl.BlockSpec((FFN_CONV, c), lambda j, i: (0, j))
    wv = pl.BlockSpec((FFN_CONV, c), lambda j, i: (0, j + nc))
    wout = pl.BlockSpec((FFN_CONV, c), lambda j, i: (0, j))
    half = jax.ShapeDtypeStruct((s, f), BF16)
    dwh = jax.ShapeDtypeStruct((FFN_CONV, f), F32)
    return pl.pallas_call(
        body, grid=(nc, nt), in_specs=[gate, val, gp, vp, gn, vn, wg, wv, gate, gn],
        out_specs=[gate, gate, wout, wout], out_shape=[half, half, dwh, dwh],
        compiler_params=_cp(("parallel", "arbitrary")), name=name)(u0, u0, u0, u0, u0, u0, conv, conv, da, da)


def _softmax_rows(s, extra=None):
    m = jnp.max(s, axis=-1, keepdims=True)
    if extra is not None:
        m = jnp.maximum(m, extra)
    m = lax.stop_gradient(m)
    e = jnp.exp(s - m)
    den = jnp.sum(e, axis=-1, keepdims=True)
    if extra is not None:
        den = den + jnp.exp(extra - m)
    return e / den


def _mem_attn_fn(qs, ks, vs):
    outs = []
    for q, k, v in zip(qs, ks, vs):
        p = _softmax_rows(_dot_nt(q, k) * (MEM_HEAD_DIM ** -0.5))
        outs.append(_dot(p, v))
    return outs


def _mem_heads(q_ref, kv_ref):
    d = MEM_HEAD_DIM
    qs = [q_ref[:, h * d:(h + 1) * d] for h in range(MEM_HEADS)]
    ks = [kv_ref[:, h * d:(h + 1) * d] for h in range(MEM_HEADS)]
    vs = [kv_ref[:, MEM_WIDTH + h * d:MEM_WIDTH + (h + 1) * d] for h in range(MEM_HEADS)]
    return qs, ks, vs


def mem_attn_fwd(p, q_col, kv, name):
    s = p.shape[0]
    t = _tile(s, (256, 128))
    m = kv.shape[0]

    def body(q_ref, kv_ref, o_ref):
        outs = _mem_attn_fn(*_mem_heads(q_ref, kv_ref))
        o_ref[...] = jnp.concatenate(outs, axis=1).astype(o_ref.dtype)

    return pl.pallas_call(
        body, grid=(s // t,),
        in_specs=[pl.BlockSpec((t, MEM_WIDTH), lambda i: (i, q_col // MEM_WIDTH)),
                  pl.BlockSpec((m, 2 * MEM_WIDTH), lambda i: (0, 0))],
        out_specs=pl.BlockSpec((t, MEM_WIDTH), lambda i: (i, 0)), out_shape=jax.ShapeDtypeStruct((s, MEM_WIDTH), BF16),
        compiler_params=_cp(("parallel",)), name=name)(p, kv)


def mem_attn_bwd(p, q_col, kv, dcat, name):
    s = p.shape[0]
    t = _tile(s, (256, 128))
    m = kv.shape[0]
    d = MEM_HEAD_DIM

    def body(q_ref, kv_ref, dy_ref, dq_ref, dkv_ref):
        qs, ks, vs = _mem_heads(q_ref, kv_ref)
        _, vjp = jax.vjp(_mem_attn_fn, qs, ks, vs)
        dqs, dks, dvs = vjp([dy_ref[:, h * d:(h + 1) * d] for h in range(MEM_HEADS)])
        dq_ref[...] = jnp.concatenate(dqs, axis=1).astype(dq_ref.dtype)

        @pl.when(pl.program_id(0) == 0)
        def _():
            dkv_ref[...] = jnp.zeros_like(dkv_ref)

        dkv_ref[...] += jnp.concatenate(dks + dvs, axis=1)

    return pl.pallas_call(
        body, grid=(s // t,),
        in_specs=[pl.BlockSpec((t, MEM_WIDTH), lambda i: (i, q_col // MEM_WIDTH)),
                  pl.BlockSpec((m, 2 * MEM_WIDTH), lambda i: (0, 0)),
                  pl.BlockSpec((t, MEM_WIDTH), lambda i: (i, MIX_WIDTH // MEM_WIDTH))],
        out_specs=[pl.BlockSpec((t, MEM_WIDTH), lambda i: (i, 0)), pl.BlockSpec((m, 2 * MEM_WIDTH), lambda i: (0, 0))],
        out_shape=[jax.ShapeDtypeStruct((s, MEM_WIDTH), BF16), jax.ShapeDtypeStruct((m, 2 * MEM_WIDTH), F32)],
        compiler_params=_cp(("arbitrary",)), name=name)(p, kv, dcat)


def _swa_fn(qs, kcs, kps, vcs, vps, sinks, not_first):
    t = SWA_BLOCK
    qi = lax.broadcasted_iota(jnp.int32, (t, 2 * t), 0)
    kj = lax.broadcasted_iota(jnp.int32, (t, 2 * t), 1)
    dist = t + qi - kj
    valid = (dist >= 0) & (dist < t) & ((kj >= t) | not_first)
    distf = dist.astype(F32)
    outs = []
    for kh in range(SWA_KV_HEADS):
        kb = jnp.concatenate([kps[kh], kcs[kh]], axis=0)
        vb = jnp.concatenate([vps[kh], vcs[kh]], axis=0)
        for g in range(SWA_GROUP):
            h = kh * SWA_GROUP + g
            slope = 2.0 ** (-8.0 * (h + 1) / SWA_Q_HEADS)
            sc = _dot_nt(qs[h], kb) * (SWA_HEAD_DIM ** -0.5) - slope * distf
            sc = jnp.where(valid, sc, -jnp.inf)
            outs.append(_dot(_softmax_rows(sc, extra=sinks[h]), vb))
    return outs


def _swa_args(q_ref, kc_ref, kp_ref, vc_ref, vp_ref, sink_ref):
    d = SWA_HEAD_DIM
    qs = [q_ref[:, h * d:(h + 1) * d] for h in range(SWA_Q_HEADS)]
    per_kv = lambda ref: [ref[:, h * d:(h + 1) * d] for h in range(SWA_KV_HEADS)]
    sinks = [sink_ref[0:1, h:h + 1] for h in range(SWA_Q_HEADS)]
    return qs, per_kv(kc_ref), per_kv(kp_ref), per_kv(vc_ref), per_kv(vp_ref), sinks


def _swa_specs(nb, order):
    t, kvw = SWA_BLOCK, SWA_KV_HEADS * SWA_HEAD_DIM
    k_col, v_col = MIX_WIDTH // kvw, MIX_WIDTH // kvw + 1
    q = pl.BlockSpec((t, MIX_WIDTH), lambda n: (order(n), 0))
    kc = pl.BlockSpec((t, kvw), lambda n: (order(n), k_col))
    kp = pl.BlockSpec((t, kvw), lambda n: (jnp.maximum(order(n) - 1, 0), k_col))
    vc = pl.BlockSpec((t, kvw), lambda n: (order(n), v_col))
    vp = pl.BlockSpec((t, kvw), lambda n: (jnp.maximum(order(n) - 1, 0), v_col))
    sink = pl.BlockSpec((1, LANES), lambda n: (0, 0))
    return [q, kc, kp, vc, vp, sink]


def _pad_lanes(v):
    return jnp.pad(v.reshape(1, -1), ((0, 0), (0, LANES - v.size)))


def swa_fwd(p, sinks, name):
    s = p.shape[0]
    nb = s // SWA_BLOCK

    def body(q_ref, kc_ref, kp_ref, vc_ref, vp_ref, sink_ref, o_ref):
        outs = _swa_fn(*_swa_args(q_ref, kc_ref, kp_ref, vc_ref, vp_ref, sink_ref), pl.program_id(0) > 0)
        o_ref[...] = jnp.concatenate(outs, axis=1).astype(o_ref.dtype)

    return pl.pallas_call(
        body, grid=(nb,), in_specs=_swa_specs(nb, lambda n: n),
        out_specs=pl.BlockSpec((SWA_BLOCK, MIX_WIDTH), lambda n: (n, 0)),
        out_shape=jax.ShapeDtypeStruct((s, MIX_WIDTH), BF16), compiler_params=_cp(("parallel",)),
        name=name)(p, p, p, p, p, _pad_lanes(sinks))


def swa_bwd(p, sinks, dcat, name):
    s = p.shape[0]
    nb = s // SWA_BLOCK
    t, d, kvw = SWA_BLOCK, SWA_HEAD_DIM, SWA_KV_HEADS * SWA_HEAD_DIM
    rev = lambda n: nb - 1 - n

    def body(q_ref, kc_ref, kp_ref, vc_ref, vp_ref, sink_ref, dy_ref, dq_ref, dk_ref, dv_ref, ds_ref, ck, cv):
        n = pl.program_id(0)

        @pl.when(n == 0)
        def _():
            ck[...] = jnp.zeros_like(ck)
            cv[...] = jnp.zeros_like(cv)
            ds_ref[...] = jnp.zeros_like(ds_ref)

        args = _swa_args(q_ref, kc_ref, kp_ref, vc_ref, vp_ref, sink_ref)
        _, vjp = jax.vjp(functools.partial(_swa_fn, not_first=rev(n) > 0), *args)
        dqs, dkcs, dkps, dvcs, dvps, dsinks = vjp([dy_ref[:, h * d:(h + 1) * d] for h in range(SWA_Q_HEADS)])
        dq_ref[...] = jnp.concatenate(dqs, axis=1).astype(dq_ref.dtype)
        dk_ref[...] = (jnp.concatenate(dkcs, axis=1) + ck[...]).astype(dk_ref.dtype)
        dv_ref[...] = (jnp.concatenate(dvcs, axis=1) + cv[...]).astype(dv_ref.dtype)
        ck[...] = jnp.concatenate(dkps, axis=1)
        cv[...] = jnp.concatenate(dvps, axis=1)
        lane = lax.broadcasted_iota(jnp.int32, (1, LANES), 1)
        acc = jnp.zeros((1, LANES), F32)
        for h in range(SWA_Q_HEADS):
            acc = acc + jnp.where(lane == h, dsinks[h], 0.0)
        ds_ref[...] += acc

    dy = pl.BlockSpec((t, MIX_WIDTH), lambda n: (rev(n), 0))
    kv_out = pl.BlockSpec((t, kvw), lambda n: (rev(n), 0))
    dq, dk, dv, ds = pl.pallas_call(
        body, grid=(nb,), in_specs=_swa_specs(nb, rev) + [dy],
        out_specs=[dy, kv_out, kv_out, pl.BlockSpec((1, LANES), lambda n: (0, 0))],
        out_shape=[jax.ShapeDtypeStruct((s, MIX_WIDTH), BF16), jax.ShapeDtypeStruct((s, kvw), BF16),
                   jax.ShapeDtypeStruct((s, kvw), BF16), jax.ShapeDtypeStruct((1, LANES), F32)],
        scratch_shapes=[pltpu.VMEM((t, kvw), F32), pltpu.VMEM((t, kvw), F32)],
        compiler_params=_cp(("arbitrary",)), name=name)(p, p, p, p, p, _pad_lanes(sinks), dcat)
    return dq, dk, dv, ds[0, :SWA_Q_HEADS]


RW_SHIFT = 5120
RW_R, RW_K, RW_V, RW_WD, RW_AD, RW_GD = 0, 1536, 3072, 4608, 4736, 4864


def _head_matrix(width, head_dim):
    e = (np.arange(width)[:, None] // head_dim == np.arange(LANES)[None, :]).astype(np.float32)
    return jnp.asarray(e), jnp.asarray(e.T)


def _rwkv_pre_fn(pieces, shifted, mus, w0, wdu, a0, wiu, wgu, k_k, k_a, e, et):
    r, k, v, wd, ad, gd = [p + (s - p) * mu for p, s, mu in zip(pieces, shifted, mus)]
    w_log = -_softplus(-(w0 + _dot(jnp.tanh(wd), wdu))) - 0.5
    lw = -jnp.exp(w_log)
    a = _sigmoid(a0 + _dot(ad, wiu))
    g = _dot(_sigmoid(gd), wgu)
    kkr = k * k_k
    kk = kkr * _dot(lax.rsqrt(_dot(kkr * kkr, e) + 1e-6), et)
    k2 = k * (1.0 + (a - 1.0) * k_a)
    return r, lw, k2, v, kk, kk * a, g


_RW_GROUPS = ((RW_R, MIX_WIDTH), (RW_K, MIX_WIDTH), (RW_V, MIX_WIDTH), (RW_WD, LANES), (RW_AD, LANES), (RW_GD, 2 * LANES))


def _rwkv_pre_inputs(p_ref, prev_ref, mu_ref, first):
    pieces = [p_ref[:, o:o + n] for o, n in _RW_GROUPS]
    shifted = [_shift_down(p_ref[:, o:o + n], prev_ref[:, o:o + n] * first, 1) for o, n in _RW_GROUPS]
    mus = [mu_ref[:, o:o + n] for o, n in _RW_GROUPS]
    return pieces, shifted, mus


def _rwkv_param_specs():
    vec = lambda n: pl.BlockSpec((1, n), lambda i: (0, 0))
    mat = lambda r, c: pl.BlockSpec((r, c), lambda i: (0, 0))
    return [vec(RW_SHIFT), vec(MIX_WIDTH), mat(LANES, MIX_WIDTH), vec(MIX_WIDTH), mat(LANES, MIX_WIDTH),
            mat(2 * LANES, MIX_WIDTH), vec(MIX_WIDTH), vec(MIX_WIDTH), mat(MIX_WIDTH, LANES), mat(LANES, MIX_WIDTH)]


def rwkv_pre_fwd(p, params, name):
    s = p.shape[0]
    t = _tile(s, (128, 64))

    def body(p_ref, prev_ref, mu_ref, *rest):
        prm, outs = rest[:9], rest[9:]
        first = (pl.program_id(0) > 0).astype(F32)
        pieces, shifted, mus = _rwkv_pre_inputs(p_ref, prev_ref, mu_ref, first)
        res = _rwkv_pre_fn(pieces, shifted, mus, *[q[...] for q in prm])
        for o_ref, val in zip(outs, res):
            o_ref[...] = val

    row = pl.BlockSpec((t, RW_SHIFT), lambda i: (i, 0))
    prev = pl.BlockSpec((SUBLANES, RW_SHIFT), lambda i: (jnp.maximum(i * (t // SUBLANES) - 1, 0), 0))
    out = pl.BlockSpec((t, MIX_WIDTH), lambda i: (i, 0))
    return pl.pallas_call(
        body, grid=(s // t,), in_specs=[row, prev] + _rwkv_param_specs(), out_specs=[out] * 7,
        out_shape=[jax.ShapeDtypeStruct((s, MIX_WIDTH), F32)] * 7, compiler_params=_cp(("parallel",)),
        name=name)(p, p, *params)


def rwkv_pre_bwd(p, params, cots, name):
    s = p.shape[0]
    t = _tile(s, (64, 32))

    def body(p_ref, prev_ref, mu_ref, *rest):
        prm, cot, outs = rest[:9], rest[9:19], rest[19:]
        dp_ref, dps_ref, grads = outs[0], outs[1], outs[2:]
        i = pl.program_id(0)
        first = (i > 0).astype(F32)
        pieces, shifted, mus = _rwkv_pre_inputs(p_ref, prev_ref, mu_ref, first)
        prm_v = [q[...] for q in prm]
        fn = lambda pieces, shifted, mus, *small: _rwkv_pre_fn(pieces, shifted, mus, *small, prm_v[7], prm_v[8])
        _, vjp = jax.vjp(fn, pieces, shifted, mus, *prm_v[:7])
        dr, dw, dk2, dv, dkk, db, dr2, dk22, dv2, dg = [c[...] for c in cot]
        res = vjp((dr + dr2, dw, dk2 + dk22, dv + dv2, dkk, db, dg))
        dpieces, dshifted, dmus, dsmall = res[0], res[1], res[2], res[3:]
        for (o, n), dpi, dsi in zip(_RW_GROUPS, dpieces, dshifted):
            dp_ref[:, o:o + n] = dpi
            dps_ref[:, o:o + n] = dsi

        @pl.when(i == 0)
        def _():
            for g_ref in grads:
                g_ref[...] = jnp.zeros_like(g_ref)

        for (o, n), dmu in zip(_RW_GROUPS, dmus):
            grads[0][:, o:o + n] += dmu
        for g_ref, dval in zip(grads[1:], dsmall):
            g_ref[...] += dval

    row = pl.BlockSpec((t, RW_SHIFT), lambda i: (i, 0))
    prev = pl.BlockSpec((SUBLANES, RW_SHIFT), lambda i: (jnp.maximum(i * (t // SUBLANES) - 1, 0), 0))
    act = pl.BlockSpec((t, MIX_WIDTH), lambda i: (i, 0))
    pspecs = _rwkv_param_specs()
    full = jax.ShapeDtypeStruct((s, RW_SHIFT), F32)
    gshapes = [jax.ShapeDtypeStruct(q.shape, F32) for q in params[:8]]
    return pl.pallas_call(
        body, grid=(s // t,), in_specs=[row, prev] + pspecs + [act] * 10, out_specs=[row, row] + pspecs[:8],
        out_shape=[full, full] + gshapes, compiler_params=_cp(("arbitrary",)), name=name)(p, p, *params, *cots)


def shift_add(a, b, js, out_dtype, name):
    s, c = a.shape
    t = _tile(s, (256, 128, 64))
    tc = _tile(c, (1024, 768, 512, 640, 384, 256, 128))
    nt, nb = s // t, len(b)

    def body(a_ref, *rest):
        b_refs, n_refs, o_ref = rest[:nb], rest[nb:2 * nb], rest[2 * nb]
        last = (pl.program_id(1) < nt - 1).astype(F32)
        acc = a_ref[...]
        for b_ref, n_ref, j in zip(b_refs, n_refs, js):
            acc = acc + _shift_up(b_ref[...], n_ref[...] * last, j)
        o_ref[...] = acc.astype(o_ref.dtype)

    tile = pl.BlockSpec((t, tc), lambda j, i: (i, j))
    _, nxt = _halo_specs(t, s, tc, lambda j: j)
    return pl.pallas_call(
        body, grid=(c // tc, nt), in_specs=[tile] * (1 + nb) + [nxt] * nb, out_specs=tile,
        out_shape=jax.ShapeDtypeStruct((s, c), out_dtype), compiler_params=_cp(("parallel", "parallel")),
        name=name)(a, *b, *b)


def _rwkv_post_fn(y, r, k2, v, g, gn_g, gn_b, r_k, e, et):
    n = RWKV_HEAD_DIM
    yc = y - _dot(_dot(y, e), et) * (1.0 / n)
    rstd = lax.rsqrt(_dot(yc * yc, e) * (1.0 / n) + RWKV_GN_EPS)
    yn = yc * _dot(rstd, et) * gn_g + gn_b
    bonus = _dot(_dot(r * k2 * r_k, e), et) * v
    return (yn + bonus) * g


def rwkv_post_fwd(acts, params, name):
    s = acts[0].shape[0]
    t = _tile(s, (256, 128))

    def body(*refs):
        vals = [q[...] for q in refs[:10]]
        refs[10][...] = _rwkv_post_fn(*vals).astype(refs[10].dtype)

    act = pl.BlockSpec((t, MIX_WIDTH), lambda i: (i, 0))
    vec = pl.BlockSpec((1, MIX_WIDTH), lambda i: (0, 0))
    mats = [pl.BlockSpec((MIX_WIDTH, LANES), lambda i: (0, 0)), pl.BlockSpec((LANES, MIX_WIDTH), lambda i: (0, 0))]
    return pl.pallas_call(
        body, grid=(s // t,), in_specs=[act] * 5 + [vec] * 3 + mats, out_specs=act,
        out_shape=jax.ShapeDtypeStruct((s, MIX_WIDTH), BF16), compiler_params=_cp(("parallel",)),
        name=name)(*acts, *params)


def rwkv_post_bwd(acts, params, dcat, name):
    s = acts[0].shape[0]
    t = _tile(s, (128, 64))

    def body(*refs):
        ins, dy_ref, outs = refs[:10], refs[10], refs[11:]
        vals = [q[...] for q in ins]
        fn = lambda *a: _rwkv_post_fn(*a, vals[8], vals[9])
        _, vjp = jax.vjp(fn, *vals[:8])
        res = vjp(dy_ref[...])
        for o_ref, val in zip(outs[:5], res[:5]):
            o_ref[...] = val

        @pl.when(pl.program_id(0) == 0)
        def _():
            for g_ref in outs[5:]:
                g_ref[...] = jnp.zeros_like(g_ref)

        for g_ref, val in zip(outs[5:], res[5:]):
            g_ref[...] += val

    act = pl.BlockSpec((t, MIX_WIDTH), lambda i: (i, 0))
    vec = pl.BlockSpec((1, MIX_WIDTH), lambda i: (0, 0))
    mats = [pl.BlockSpec((MIX_WIDTH, LANES), lambda i: (0, 0)), pl.BlockSpec((LANES, MIX_WIDTH), lambda i: (0, 0))]
    a_shape = jax.ShapeDtypeStruct((s, MIX_WIDTH), F32)
    v_shape = jax.ShapeDtypeStruct((1, MIX_WIDTH), F32)
    return pl.pallas_call(
        body, grid=(s // t,), in_specs=[act] * 5 + [vec] * 3 + mats + [act], out_specs=[act] * 5 + [vec] * 3,
        out_shape=[a_shape] * 5 + [v_shape] * 3, compiler_params=_cp(("arbitrary",)), name=name)(*acts, *params, dcat)


RW_CHUNK = 64


RW_HEADS_PER_STEP = 4


def _rwkv_chunk_fn(r, lw, k, v, kk, b, st):
    c = RW_CHUNK
    ri = lax.broadcasted_iota(jnp.int32, (c, c), 0)
    ci = lax.broadcasted_iota(jnp.int32, (c, c), 1)
    incl, strict = ri >= ci, ri > ci
    eye = (ri == ci).astype(F32)
    last_col = (ci == c - 1).astype(F32)
    last_row = (ri == c - 1).astype(F32)
    gc = _dot(incl.astype(F32), lw)
    a_t = -kk * jnp.exp(gc - lw)
    e_neg = jnp.exp(-gc)
    b_t, k_t, r_t = b * e_neg, k * e_neg, r * jnp.exp(gc)
    m_ab = jnp.where(strict, _dot_nt(a_t, b_t), 0.0)
    m_ak = jnp.where(strict, _dot_nt(a_t, k_t), 0.0)
    m_rb = jnp.where(incl, _dot_nt(r_t, b_t), 0.0)
    m_rk = jnp.where(incl, _dot_nt(r_t, k_t), 0.0)
    tinv, pw = eye + m_ab, m_ab
    for _ in range(5):
        pw = _dot(pw, pw)
        tinv = tinv + _dot(tinv, pw)
    u = _dot(tinv, _dot(a_t, st) + _dot(m_ak, v))
    y = _dot(r_t, st) + _dot(m_rb, u) + _dot(m_rk, v)
    dec = jnp.exp(_dot(last_col, gc) - gc)
    g_end = _dot_tn(gc, last_row)
    new_st = st * jnp.exp(g_end) + _dot_tn(b * dec, u) + _dot_tn(k * dec, v)
    return y, new_st


def rwkv_scan_fwd(r, lw, k, v, kk, b, name):
    s = r.shape[0]
    n, hp = RWKV_HEAD_DIM, RW_HEADS_PER_STEP
    nchunk, width = s // RW_CHUNK, RWKV_HEAD_DIM * RW_HEADS_PER_STEP

    def body(r_ref, w_ref, k_ref, v_ref, kk_ref, b_ref, y_ref, ck_ref, carry):
        @pl.when(pl.program_id(1) == 0)
        def _():
            carry[...] = jnp.zeros_like(carry)

        ck_ref[0] = carry[...]
        ys, sts = [], []
        for h in range(hp):
            cols = slice(h * n, (h + 1) * n)
            y, st = _rwkv_chunk_fn(*[q[:, cols] for q in (r_ref, w_ref, k_ref, v_ref, kk_ref, b_ref)], carry[:, cols])
            ys.append(y)
            sts.append(st)
        y_ref[...] = jnp.concatenate(ys, axis=1)
        carry[...] = jnp.concatenate(sts, axis=1)

    blk = pl.BlockSpec((RW_CHUNK, width), lambda j, c: (c, j))
    return pl.pallas_call(
        body, grid=(MIX_WIDTH // width, nchunk), in_specs=[blk] * 6,
        out_specs=[blk, pl.BlockSpec((1, n, width), lambda j, c: (c, 0, j))],
        out_shape=[jax.ShapeDtypeStruct((s, MIX_WIDTH), F32), jax.ShapeDtypeStruct((nchunk, n, MIX_WIDTH), F32)],
        scratch_shapes=[pltpu.VMEM((n, width), F32)],
        compiler_params=_cp(("parallel", "arbitrary")), name=name)(r, lw, k, v, kk, b)


def rwkv_scan_bwd(r, lw, k, v, kk, b, ck, dy, name):
    s = r.shape[0]
    n, hp = RWKV_HEAD_DIM, RW_HEADS_PER_STEP
    nchunk, width = s // RW_CHUNK, RWKV_HEAD_DIM * RW_HEADS_PER_STEP
    rev = lambda c: nchunk - 1 - c

    def body(r_ref, w_ref, k_ref, v_ref, kk_ref, b_ref, ck_ref, dy_ref, *rest):
        outs, carry = rest[:6], rest[6]

        @pl.when(pl.program_id(1) == 0)
        def _():
            carry[...] = jnp.zeros_like(carry)

        grads = []
        for h in range(hp):
            cols = slice(h * n, (h + 1) * n)
            args = [q[:, cols] for q in (r_ref, w_ref, k_ref, v_ref, kk_ref, b_ref)] + [ck_ref[0, :, cols]]
            _, vjp = jax.vjp(_rwkv_chunk_fn, *args)
            grads.append(vjp((dy_ref[:, cols], carry[:, cols])))
        for q in range(6):
            outs[q][...] = jnp.concatenate([g[q] for g in grads], axis=1)
        carry[...] = jnp.concatenate([g[6] for g in grads], axis=1)

    blk = pl.BlockSpec((RW_CHUNK, width), lambda j, c: (rev(c), j))
    out = jax.ShapeDtypeStruct((s, MIX_WIDTH), F32)
    return pl.pallas_call(
        body, grid=(MIX_WIDTH // width, nchunk),
        in_specs=[blk] * 6 + [pl.BlockSpec((1, n, width), lambda j, c: (rev(c), 0, j)), blk],
        out_specs=[blk] * 6, out_shape=[out] * 6, scratch_shapes=[pltpu.VMEM((n, width), F32)],
        compiler_params=_cp(("parallel", "arbitrary")), name=name)(r, lw, k, v, kk, b, ck, dy)


GD_Q, GD_K, GD_V, GD_Z, GD_QMEM, GD_BT, GD_AT, GD_COLS = 0, 768, 1536, 3072, 4608, 5120, 5248, 5376
_GD_GROUPS = ((GD_Q, GDN_QK_WIDTH), (GD_K, GDN_QK_WIDTH), (GD_V, MIX_WIDTH))


def _gdn_pre_fn(xs, convs, bt, at, a_log, dt_bias, e6, e6t, ebc):
    k_w = GDN_CONV
    acts = [_silu(sum(convs[g][j] * xs[g][k_w - 1 - j] for j in range(k_w))) for g in range(3)]
    l2 = lambda x: x * _dot(lax.rsqrt(_dot(x * x, e6) + 1e-6), e6t)
    beta = _sigmoid(bt)
    g = -jnp.exp(a_log) * _softplus(at + dt_bias)
    return l2(acts[0]), l2(acts[1]), acts[2], _dot(g, ebc), _dot(beta, ebc)


def _gdn_pre_inputs(x_ref, prev_ref, conv_ref, first):
    xs = [[_shift_down(x_ref[:, o:o + n], prev_ref[:, o:o + n] * first, j) for j in range(GDN_CONV)]
          for o, n in _GD_GROUPS]
    convs = [[conv_ref[j:j + 1, o:o + n] for j in range(GDN_CONV)] for o, n in _GD_GROUPS]
    return xs, convs


def _gdn_pre_specs(t):
    x = pl.BlockSpec((t, GDN_CONV_WIDTH), lambda i: (i, 0))
    prev = pl.BlockSpec((SUBLANES, GDN_CONV_WIDTH), lambda i: (jnp.maximum(i * (t // SUBLANES) - 1, 0), 0))
    bta = pl.BlockSpec((t, 2 * LANES), lambda i: (i, GD_BT // (2 * LANES)))
    conv = pl.BlockSpec((GDN_CONV, GDN_CONV_WIDTH), lambda i: (0, 0))
    vec = pl.BlockSpec((1, LANES), lambda i: (0, 0))
    mats = [pl.BlockSpec((GDN_QK_WIDTH, LANES), lambda i: (0, 0)), pl.BlockSpec((LANES, GDN_QK_WIDTH), lambda i: (0, 0)),
            pl.BlockSpec((LANES, MIX_WIDTH), lambda i: (0, 0))]
    return [x, prev, bta, conv, vec, vec] + mats


def gdn_pre_fwd(p, params, name):
    s = p.shape[0]
    t = _tile(s, (128, 64))

    def body(x_ref, prev_ref, bta_ref, conv_ref, al_ref, dt_ref, e6_ref, e6t_ref, ebc_ref, *outs):
        first = (pl.program_id(0) > 0).astype(F32)
        xs, convs = _gdn_pre_inputs(x_ref, prev_ref, conv_ref, first)
        res = _gdn_pre_fn(xs, convs, bta_ref[:, :LANES], bta_ref[:, LANES:], al_ref[...], dt_ref[...],
                          e6_ref[...], e6t_ref[...], ebc_ref[...])
        for o_ref, val in zip(outs, res):
            o_ref[...] = val

    qk = pl.BlockSpec((t, GDN_QK_WIDTH), lambda i: (i, 0))
    wide = pl.BlockSpec((t, MIX_WIDTH), lambda i: (i, 0))
    qk_s, wide_s = jax.ShapeDtypeStruct((s, GDN_QK_WIDTH), F32), jax.ShapeDtypeStruct((s, MIX_WIDTH), F32)
    return pl.pallas_call(
        body, grid=(s // t,), in_specs=_gdn_pre_specs(t), out_specs=[qk, qk, wide, wide, wide],
        out_shape=[qk_s, qk_s, wide_s, wide_s, wide_s], compiler_params=_cp(("parallel",)), name=name)(p, p, p, *params)


def gdn_pre_bwd(p, params, cots, name):
    s = p.shape[0]
    t = _tile(s, (64, 32))

    def body(x_ref, prev_ref, bta_ref, conv_ref, al_ref, dt_ref, e6_ref, e6t_ref, ebc_ref, *rest):
        cot, outs = rest[:5], rest[5:]
        dxs, dbta_ref, dconv_ref, dal_ref, ddt_ref = outs[:4], outs[4], outs[5], outs[6], outs[7]
        i = pl.program_id(0)
        first = (i > 0).astype(F32)
        xs, convs = _gdn_pre_inputs(x_ref, prev_ref, conv_ref, first)
        mats = (e6_ref[...], e6t_ref[...], ebc_ref[...])
        fn = lambda xs, convs, bt, at, al, dt: _gdn_pre_fn(xs, convs, bt, at, al, dt, *mats)
        _, vjp = jax.vjp(fn, xs, convs, bta_ref[:, :LANES], bta_ref[:, LANES:], al_ref[...], dt_ref[...])
        d_xs, d_convs, d_bt, d_at, d_al, d_dt = vjp(tuple(c[...] for c in cot))
        for g, (o, n) in enumerate(_GD_GROUPS):
            for j in range(GDN_CONV):
                dxs[j][:, o:o + n] = d_xs[g][j]
        dbta_ref[...] = jnp.concatenate([d_bt, d_at], axis=1).astype(dbta_ref.dtype)

        @pl.when(i == 0)
        def _():
            dconv_ref[...] = jnp.zeros_like(dconv_ref)
            dal_ref[...] = jnp.zeros_like(dal_ref)
            ddt_ref[...] = jnp.zeros_like(ddt_ref)

        for g, (o, n) in enumerate(_GD_GROUPS):
            for j in range(GDN_CONV):
                dconv_ref[j:j + 1, o:o + n] += d_convs[g][j]
        dal_ref[...] += d_al
        ddt_ref[...] += d_dt

    specs = _gdn_pre_specs(t)
    qk = pl.BlockSpec((t, GDN_QK_WIDTH), lambda i: (i, 0))
    wide = pl.BlockSpec((t, MIX_WIDTH), lambda i: (i, 0))
    x_s = jax.ShapeDtypeStruct((s, GDN_CONV_WIDTH), F32)
    vec_s = jax.ShapeDtypeStruct((1, LANES), F32)
    return pl.pallas_call(
        body, grid=(s // t,), in_specs=specs + [qk, qk, wide, wide, wide],
        out_specs=[specs[0]] * 4 + [pl.BlockSpec((t, 2 * LANES), lambda i: (i, 0)), specs[3], specs[4], specs[5]],
        out_shape=[x_s] * 4 + [jax.ShapeDtypeStruct((s, 2 * LANES), BF16),
                               jax.ShapeDtypeStruct((GDN_CONV, GDN_CONV_WIDTH), F32), vec_s, vec_s],
        compiler_params=_cp(("arbitrary",)), name=name)(p, p, p, *params, *cots)


def _gdn_post_fn(o, z, norm_g, e12, e12t, trep):
    rstd = lax.rsqrt(_dot(o * o, e12) * (1.0 / GDN_HEAD_DIM) + NORM_EPS)
    return o * _dot(rstd, e12t) * _dot(norm_g, trep) * _silu(z)


def _gdn_post_specs(t):
    act = pl.BlockSpec((t, MIX_WIDTH), lambda i: (i, 0))
    z = pl.BlockSpec((t, MIX_WIDTH), lambda i: (i, GD_Z // MIX_WIDTH))
    mats = [pl.BlockSpec((SUBLANES, LANES), lambda i: (0, 0)), pl.BlockSpec((MIX_WIDTH, LANES), lambda i: (0, 0)),
            pl.BlockSpec((LANES, MIX_WIDTH), lambda i: (0, 0)), pl.BlockSpec((LANES, MIX_WIDTH), lambda i: (0, 0))]
    return [act, z] + mats


def gdn_post_fwd(o, p, params, name):
    s = o.shape[0]
    t = _tile(s, (256, 128))

    def body(o_ref, z_ref, ng_ref, e_ref, et_ref, tr_ref, out_ref):
        res = _gdn_post_fn(o_ref[...], z_ref[...], ng_ref[0:1, :], e_ref[...], et_ref[...], tr_ref[...])
        out_ref[...] = res.astype(out_ref.dtype)

    act = pl.BlockSpec((t, MIX_WIDTH), lambda i: (i, 0))
    return pl.pallas_call(
        body, grid=(s // t,), in_specs=_gdn_post_specs(t), out_specs=act,
        out_shape=jax.ShapeDtypeStruct((s, MIX_WIDTH), BF16), compiler_params=_cp(("parallel",)),
        name=name)(o, p, *params)


def gdn_post_bwd(o, p, params, dcat, name):
    s = o.shape[0]
    t = _tile(s, (128, 64))

    def body(o_ref, z_ref, ng_ref, e_ref, et_ref, tr_ref, dy_ref, do_ref, dz_ref, dng_ref):
        mats = (e_ref[...], et_ref[...], tr_ref[...])
        fn = lambda o, z, ng: _gdn_post_fn(o, z, ng, *mats)
        _, vjp = jax.vjp(fn, o_ref[...], z_ref[...], ng_ref[0:1, :])
        d_o, d_z, d_ng = vjp(dy_ref[...])
        do_ref[...] = d_o
        dz_ref[...] = d_z.astype(dz_ref.dtype)

        @pl.when(pl.program_id(0) == 0)
        def _():
            dng_ref[...] = jnp.zeros_like(dng_ref)

        dng_ref[...] += d_ng

    act = pl.BlockSpec((t, MIX_WIDTH), lambda i: (i, 0))
    return pl.pallas_call(
        body, grid=(s // t,), in_specs=_gdn_post_specs(t) + [act],
        out_specs=[act, act, pl.BlockSpec((1, LANES), lambda i: (0, 0))],
        out_shape=[jax.ShapeDtypeStruct((s, MIX_WIDTH), F32), jax.ShapeDtypeStruct((s, MIX_WIDTH), BF16),
                   jax.ShapeDtypeStruct((1, LANES), F32)],
        compiler_params=_cp(("arbitrary",)), name=name)(o, p, *params, dcat)


def _gdn_chunk_fn(q, k, v, gb, bb, gb64, state):
    c = GDN_CHUNK
    ri = lax.broadcasted_iota(jnp.int32, (c, c), 0)
    ci = lax.broadcasted_iota(jnp.int32, (c, c), 1)
    causal, strict = ri >= ci, ri > ci
    ltri = causal.astype(F32)
    eye = (ri == ci).astype(F32)
    first_col = (ci == 0).astype(F32)
    last_col = (ci == c - 1).astype(F32)
    last_col_tall = (lax.broadcasted_iota(jnp.int32, (GDN_HEAD_DIM, c), 1) == c - 1).astype(F32)

    qs = q * (GDN_HEAD_DIM ** -0.5)
    gc = _dot(ltri, gb)
    gd = _dot(ltri, gb64)
    diff = gd - _dot_nt(first_col, gd)
    decay = jnp.exp(jnp.where(causal, diff, -jnp.inf))
    kb = k * bb
    lmat = jnp.where(strict, _dot_nt(kb, k) * decay, 0.0)
    tmat, pw = eye - lmat, lmat
    for _ in range(5):
        pw = _dot(pw, pw)
        tmat = tmat + _dot(tmat, pw)
    eg = jnp.exp(gc)
    u = _dot(tmat, v * bb)
    w = _dot(tmat, kb * eg)
    a_qk = jnp.where(causal, _dot_nt(qs, k) * decay, 0.0)
    g_last = _dot(last_col, gc)
    k_dec = k * jnp.exp(g_last - gc)
    v_new = u - _dot(w, state)
    out = _dot(qs * eg, state) + _dot(a_qk, v_new)
    new_state = state * jnp.exp(_dot(last_col_tall, gc)) + _dot_tn(k_dec, v_new)
    return out, new_state


GDN_REP = GDN_V_HEADS // GDN_QK_HEADS


def _gdn_chunk_specs(order):
    c, d = GDN_CHUNK, GDN_HEAD_DIM
    qk = pl.BlockSpec((c, d), lambda j, n: (order(n), j))
    vh = pl.BlockSpec((c, GDN_REP * d), lambda j, n: (order(n), j))
    st = pl.BlockSpec((GDN_REP, 1, d, d), lambda j, n: (j, order(n), 0, 0))
    return qk, vh, st


def gdn_chunk_fwd(q, k, v, gb, bb, name):
    s = q.shape[0]
    nc, d = s // GDN_CHUNK, GDN_HEAD_DIM

    def body(q_ref, k_ref, v_ref, gb_ref, bb_ref, o_ref, st_ref, carry):
        @pl.when(pl.program_id(1) == 0)
        def _():
            carry[...] = jnp.zeros_like(carry)

        for rep in range(GDN_REP):
            cols = slice(rep * d, (rep + 1) * d)
            state = carry[rep]
            st_ref[rep, 0] = state
            out, new_state = _gdn_chunk_fn(q_ref[...], k_ref[...], v_ref[:, cols], gb_ref[:, cols], bb_ref[:, cols],
                                           gb_ref[:, rep * d:rep * d + GDN_CHUNK], state)
            o_ref[:, cols] = out
            carry[rep] = new_state

    qk, vh, st = _gdn_chunk_specs(lambda n: n)
    return pl.pallas_call(
        body, grid=(GDN_QK_HEADS, nc), in_specs=[qk, qk, vh, vh, vh], out_specs=[vh, st],
        out_shape=[jax.ShapeDtypeStruct((s, MIX_WIDTH), F32), jax.ShapeDtypeStruct((GDN_V_HEADS, nc, d, d), F32)],
        scratch_shapes=[pltpu.VMEM((GDN_REP, d, d), F32)],
        compiler_params=_cp(("parallel", "arbitrary")), name=name)(q, k, v, gb, bb)


def gdn_chunk_bwd(q, k, v, gb, bb, states, do, name):
    s = q.shape[0]
    nc, d = s // GDN_CHUNK, GDN_HEAD_DIM
    rev = lambda n: nc - 1 - n

    def body(q_ref, k_ref, v_ref, gb_ref, bb_ref, st_ref, do_ref, dq_ref, dk_ref, dv_ref, dg_ref, db_ref, carry):
        @pl.when(pl.program_id(1) == 0)
        def _():
            carry[...] = jnp.zeros_like(carry)

        d_qs, d_ks = [], []
        for rep in range(GDN_REP):
            cols = slice(rep * d, (rep + 1) * d)
            lead = slice(rep * d, rep * d + GDN_CHUNK)
            args = (q_ref[...], k_ref[...], v_ref[:, cols], gb_ref[:, cols], bb_ref[:, cols], gb_ref[:, lead],
                    st_ref[rep, 0])
            _, vjp = jax.vjp(_gdn_chunk_fn, *args)
            d_q, d_k, d_v, d_gb, d_bb, d_gb64, d_state = vjp((do_ref[:, cols], carry[rep]))
            carry[rep] = d_state
            dv_ref[:, cols] = d_v
            db_ref[:, cols] = d_bb
            dg_ref[:, cols] = d_gb
            dg_ref[:, lead] += d_gb64
            d_qs.append(d_q)
            d_ks.append(d_k)
        dq_ref[...] = sum(d_qs[1:], d_qs[0])
        dk_ref[...] = sum(d_ks[1:], d_ks[0])

    qk, vh, st = _gdn_chunk_specs(rev)
    qk_s, wide_s = jax.ShapeDtypeStruct((s, GDN_QK_WIDTH), F32), jax.ShapeDtypeStruct((s, MIX_WIDTH), F32)
    return pl.pallas_call(
        body, grid=(GDN_QK_HEADS, nc), in_specs=[qk, qk, vh, vh, vh, st, vh], out_specs=[qk, qk, vh, vh, vh],
        out_shape=[qk_s, qk_s, wide_s, wide_s, wide_s], scratch_shapes=[pltpu.VMEM((GDN_REP, d, d), F32)],
        compiler_params=_cp(("parallel", "arbitrary")), name=name)(q, k, v, gb, bb, states, do)


WEIGHTS = ['attn_norm', 'mem_norm', 'w_mem_kv', 'w_out', 'ffn_norm', 'w_ffn_up', 'ffn_conv', 'w_ffn_down', 'final_norm',
           'a_w_in', 'a_sinks', 'b_w_in', 'b_mu', 'b_w0', 'b_w_decay_up', 'b_a0', 'b_w_iclr_up', 'b_w_gate_up', 'b_k_k',
           'b_k_a', 'b_r_k', 'b_gn_g', 'b_gn_b', 'c_w_in', 'c_conv', 'c_a_log', 'c_dt_bias', 'c_norm_g']
INPUTS = ['x', 'mem'] + WEIGHTS + ['loss_target'] + ['m_' + n for n in WEIGHTS] + ['v_' + n for n in WEIGHTS]
REPLICATED = ['attn_norm', 'mem_norm', 'ffn_norm', 'final_norm', 'a_sinks', 'b_mu', 'b_w0', 'b_a0', 'b_k_k', 'b_k_a',
              'b_r_k', 'b_gn_g', 'b_gn_b', 'c_a_log', 'c_dt_bias', 'c_norm_g']
C_MIX = GDN_CONV_WIDTH + MIX_WIDTH
GATHER_ID = 1


def _cols_to_shards(full):
    rows, cols = full.shape
    return full.reshape(rows, N_DEV, cols // N_DEV).transpose(1, 0, 2)


def _shards_to_cols(g):
    return g.transpose(1, 0, 2).reshape(g.shape[1], N_DEV * g.shape[2])


def _pad_to(x, n, axis):
    pad = [(0, 0)] * x.ndim
    pad[axis] = (0, n - x.shape[axis])
    return jnp.pad(x, pad)


def _b_pad_cols(w):
    parts = [w[..., :4608], _pad_to(w[..., 4608:4704], LANES, -1), _pad_to(w[..., 4704:4800], LANES, -1), w[..., 4800:5056]]
    if w.shape[-1] > 5056:
        parts.append(w[..., 5056:])
    return jnp.concatenate(parts, axis=-1)


def _b_unpad_cols(w):
    parts = [w[..., :4608], w[..., RW_WD:RW_WD + RWKV_DECAY_RANK], w[..., RW_AD:RW_AD + RWKV_ICLR_RANK], w[..., RW_GD:RW_SHIFT]]
    if w.shape[-1] > RW_SHIFT:
        parts.append(w[..., RW_SHIFT:])
    return jnp.concatenate(parts, axis=-1)


def _c_pad_cols(w):
    return jnp.concatenate([w[..., :C_MIX], w[..., C_MIX + 24:], _pad_to(w[..., C_MIX:C_MIX + 12], LANES, -1),
                            _pad_to(w[..., C_MIX + 12:C_MIX + 24], LANES, -1)], axis=-1)


def _c_unpad_cols(w):
    return jnp.concatenate([w[..., :C_MIX], w[..., GD_BT:GD_BT + GDN_V_HEADS], w[..., GD_AT:GD_AT + GDN_V_HEADS],
                            w[..., GD_QMEM:GD_BT]], axis=-1)


def _pack(arrays):
    flat = jnp.concatenate([a.reshape(-1).astype(F32) for a in arrays])
    unit = SUBLANES * LANES
    return _pad_to(flat, -(-flat.size // unit) * unit, 0).reshape(-1, LANES)


def _unpack(packed, shapes):
    flat, out, at = packed.reshape(-1), [], 0
    for shp in shapes:
        n = int(np.prod(shp))
        out.append(flat[at:at + n].reshape(shp))
        at += n
    return out


def kernel(*args):
    a = dict(zip(INPUTS, args))
    x0, mem, target = a['x'][0], a['mem'][0], a['loss_target'][0]
    s = x0.shape[0]
    e64, e64t = _head_matrix(MIX_WIDTH, RWKV_HEAD_DIM)
    e6, e6t = _head_matrix(GDN_QK_WIDTH, GDN_HEAD_DIM)
    e12, e12t = _head_matrix(MIX_WIDTH, GDN_HEAD_DIM)
    trep = jnp.asarray((np.arange(LANES)[:, None] == np.arange(MIX_WIDTH)[None, :] % LANES).astype(np.float32))
    row = lambda v: v.reshape(1, -1)

    def in_proj_shard(l):
        kind, j = l % 3, l // 3
        return (a['a_w_in'], a['b_w_in'], a['c_w_in'])[kind][j]

    def small_shards(l):
        kind, j = l % 3, l // 3
        if kind == 1:
            return [a['b_w_decay_up'][j], a['b_w_iclr_up'][j], a['b_w_gate_up'][j]]
        if kind == 2:
            return [a['c_conv'][j]]
        return []

    gathered = []
    for l in range(DEPTH):
        big = [a['w_mem_kv'][l], a['w_out'][l], a['w_ffn_up'][l], a['w_ffn_down'][l], in_proj_shard(l)]
        shards = [w.astype(BF16) for w in big] + [a['ffn_conv'][l]] + small_shards(l)
        if gathered:
            shards, _ = lax.optimization_barrier((shards, gathered[-1]))
        gathered.append(all_gather_many_async(shards, f"gather_weights_{l}", GATHER_ID + l))

    def layer_weights(l, g):
        kind = l % 3
        w_in = _shards_to_cols(g[4])
        lw = dict(w_kv=g[0].reshape(D_MODEL, 2 * MEM_WIDTH), w_out=g[1].reshape(D_MODEL, D_MODEL),
                  w_up=_shards_to_cols(g[2]), w_down=g[3].reshape(D_FF, D_MODEL), conv=_shards_to_cols(g[5]))
        if kind == 0:
            lw['w_in'] = w_in
        elif kind == 1:
            lw['w_in'] = _b_pad_cols(w_in)
            lw['wdu'] = _pad_to(_shards_to_cols(g[6]), LANES, 0)
            lw['wiu'] = _pad_to(_shards_to_cols(g[7]), LANES, 0)
            lw['wgu'] = _shards_to_cols(g[8])
        else:
            lw['w_in'] = _c_pad_cols(w_in)
            lw['c_conv'] = _shards_to_cols(g[6])
        return lw

    def rwkv_params(j, lw):
        return (row(_b_pad_cols(a['b_mu'][j])), row(a['b_w0'][j]), lw['wdu'], row(a['b_a0'][j]), lw['wiu'], lw['wgu'],
                row(a['b_k_k'][j]), row(a['b_k_a'][j]), e64, e64t)

    def rwkv_post_params(j):
        return (row(a['b_gn_g'][j]), row(a['b_gn_b'][j]), row(a['b_r_k'][j]), e64, e64t)

    def gdn_params(j, lw):
        return (lw['c_conv'], _pad_lanes(a['c_a_log'][j]), _pad_lanes(a['c_dt_bias'][j]), e6, e6t, e12t)

    def gdn_post_params(j):
        return (jnp.tile(row(a['c_norm_g'][j]), (SUBLANES, 1)), e12, e12t, trep)

    x = x0
    saved, layers = [], []
    for l in range(DEPTH):
        kind, j = l % 3, l // 3
        g = gathered[l]
        if l > 0:
            x, g = lax.optimization_barrier((x, g))
        lw = layer_weights(l, g)
        layers.append(lw)
        sv = dict(x=x)
        h = rmsnorm_fwd(x, a['attn_norm'][l], BF16, f"attn_norm_{l}")
        memn = rmsnorm_fwd(mem, a['mem_norm'][l], BF16, f"mem_norm_{l}")
        mem_kv = mm(memn, lw['w_kv'], name=f"mem_kv_{l}")
        p = mm(h, lw['w_in'], name=f"in_proj_{l}")
        if kind == 0:
            y = swa_fwd(p, a['a_sinks'][j], f"swa_{l}")
            q_col = MIX_WIDTH + 2 * SWA_KV_HEADS * SWA_HEAD_DIM
        elif kind == 1:
            pre = rwkv_pre_fwd(p, rwkv_params(j, lw), f"rwkv_pre_{l}")
            yscan, ck = rwkv_scan_fwd(*pre[:6], f"rwkv_scan_{l}")
            post_in = (yscan, pre[0], pre[2], pre[3], pre[6])
            y = rwkv_post_fwd(post_in, rwkv_post_params(j), f"rwkv_post_{l}")
            sv.update(pre=pre, ck=ck, post_in=post_in)
            q_col = RW_SHIFT
        else:
            pre = gdn_pre_fwd(p, gdn_params(j, lw), f"gdn_pre_{l}")
            o, states = gdn_chunk_fwd(*pre, f"gdn_chunk_{l}")
            y = gdn_post_fwd(o, p, gdn_post_params(j), f"gdn_post_{l}")
            sv.update(pre=pre, o=o, states=states)
            q_col = GD_QMEM
        y_mem = mem_attn_fwd(p, q_col, mem_kv, f"mem_attn_{l}")
        cat = jnp.concatenate([y, y_mem], axis=1)
        x1 = mm(cat, lw['w_out'], res=x, name=f"out_proj_{l}")
        hf = rmsnorm_fwd(x1, a['ffn_norm'][l], BF16, f"ffn_norm_{l}")
        u0 = mm(hf, lw['w_up'], name=f"ffn_up_{l}")
        act = ffn_act_fwd(u0, lw['conv'], f"ffn_act_{l}")
        x = mm(act, lw['w_down'], res=x1, name=f"ffn_down_{l}")
        sv.update(h=h, memn=memn, mem_kv=mem_kv, p=p, q_col=q_col, cat=cat, x1=x1, hf=hf, u0=u0, act=act)
        saved.append(sv)

    loss_part, dx, d_final_norm = final_loss(x, a['final_norm'], target, "final_loss")

    rep_grads = {n: [None] * a[n].shape[0] for n in ('attn_norm', 'mem_norm', 'ffn_norm', 'a_sinks')}
    rep_grads['final_norm'] = d_final_norm
    results = {}
    exchanged, pending = {}, {}

    def apply_adam(name, idx, pieces, tag):
        w, m, v = a[name][idx], a['m_' + name][idx], a['v_' + name][idx]
        shp = w.shape
        two_d = (-1, shp[-1])
        out = adamw_sum(pieces.reshape((N_DEV,) + w.reshape(two_d).shape), w.reshape(two_d), m.reshape(two_d),
                        v.reshape(two_d), f"adamw_{name}_{tag}")
        results.setdefault(name, {})[idx] = [o.reshape(shp) for o in out]

    for l in reversed(range(DEPTH)):
        kind, j = l % 3, l // 3
        lw, sv = layers[l], saved[l]
        p, q_col = sv['p'], sv['q_col']
        d_act = mm(dx, lw['w_down'], tb=True, name=f"d_ffn_act_{l}")
        dw_down = mm(sv['act'], dx, ta=True, out_dtype=BF16, name=f"dw_ffn_down_{l}")
        dug, duv, dcg, dcv = ffn_act_bwd(sv['u0'], lw['conv'], d_act, f"ffn_act_bwd_{l}")
        du0 = jnp.concatenate([dug, duv], axis=1)
        d_conv = jnp.concatenate([dcg, dcv], axis=1)
        d_hf = mm(du0, lw['w_up'], tb=True, name=f"d_ffn_norm_out_{l}")
        dw_up = mm(sv['hf'], du0, ta=True, out_dtype=BF16, name=f"dw_ffn_up_{l}")
        dx1, rep_grads['ffn_norm'][l] = rmsnorm_bwd(sv['x1'], a['ffn_norm'][l], d_hf, dx, f"ffn_norm_bwd_{l}")
        dcat = mm(dx1, lw['w_out'], tb=True, name=f"d_cat_{l}")
        dw_out = mm(sv['cat'], dx1, ta=True, out_dtype=BF16, name=f"dw_out_{l}")
        dq_mem, d_mem_kv = mem_attn_bwd(p, q_col, sv['mem_kv'], dcat, f"mem_attn_bwd_{l}")
        small_grads = []
        if kind == 0:
            dq, dk, dv, rep_grads['a_sinks'][j] = swa_bwd(p, a['a_sinks'][j], dcat, f"swa_bwd_{l}")
            dp = jnp.concatenate([dq, dk, dv, dq_mem], axis=1)
        elif kind == 1:
            post = rwkv_post_bwd(sv['post_in'], rwkv_post_params(j), dcat, f"rwkv_post_bwd_{l}")
            scan = rwkv_scan_bwd(*sv['pre'][:6], sv['ck'], post[0], f"rwkv_scan_bwd_{l}")
            res = rwkv_pre_bwd(p, rwkv_params(j, lw), tuple(scan) + tuple(post[1:5]), f"rwkv_pre_bwd_{l}")
            dp_mix = shift_add(res[0], [res[1]], [1], BF16, f"rwkv_shift_bwd_{l}")
            dp = jnp.concatenate([dp_mix, dq_mem], axis=1)
            for n, val in zip(('b_mu', 'b_w0', 'b_a0', 'b_k_k', 'b_k_a'), (_b_unpad_cols(res[2]), res[3], res[5], res[8], res[9])):
                rep_grads[n] = val
            rep_grads.update(b_gn_g=post[5], b_gn_b=post[6], b_r_k=post[7])
            small_grads = [_cols_to_shards(res[4][:RWKV_DECAY_RANK]), _cols_to_shards(res[6][:RWKV_ICLR_RANK]),
                           _cols_to_shards(res[7])]
        else:
            d_o, dz, rep_grads['c_norm_g'] = gdn_post_bwd(sv['o'], p, gdn_post_params(j), dcat, f"gdn_post_bwd_{l}")
            chunk = gdn_chunk_bwd(*sv['pre'], sv['states'], d_o, f"gdn_chunk_bwd_{l}")
            res = gdn_pre_bwd(p, gdn_params(j, lw), chunk, f"gdn_pre_bwd_{l}")
            dqkv = shift_add(res[0], list(res[1:4]), [1, 2, 3], BF16, f"gdn_shift_bwd_{l}")
            dp = jnp.concatenate([dqkv, dz, dq_mem, res[4]], axis=1)
            rep_grads.update(c_a_log=res[6][:, :GDN_V_HEADS], c_dt_bias=res[7][:, :GDN_V_HEADS])
            small_grads = [_cols_to_shards(res[5])]
        d_h = mm(dp, lw['w_in'], tb=True, name=f"d_attn_norm_out_{l}")
        dw_in = mm(sv['h'], dp, ta=True, out_dtype=BF16, name=f"dw_in_{l}")
        dx, rep_grads['attn_norm'][l] = rmsnorm_bwd(sv['x'], a['attn_norm'][l], d_h, dx1, f"attn_norm_bwd_{l}")
        d_memn = mm(d_mem_kv, lw['w_kv'], tb=True, name=f"d_mem_norm_out_{l}")
        dw_kv = mm(sv['memn'], d_mem_kv, ta=True, out_dtype=BF16, name=f"dw_mem_kv_{l}")
        _, rep_grads['mem_norm'][l] = rmsnorm_bwd(mem, a['mem_norm'][l], d_memn, None, f"mem_norm_bwd_{l}")

        if kind == 1:
            dw_in = _b_unpad_cols(dw_in)
        elif kind == 2:
            dw_in = _c_unpad_cols(dw_in)
        pieces = [dw_kv.reshape(N_DEV, -1, 2 * MEM_WIDTH), dw_out.reshape(N_DEV, -1, D_MODEL), _cols_to_shards(dw_up),
                  dw_down.reshape(N_DEV, -1, D_MODEL), _cols_to_shards(dw_in), _cols_to_shards(d_conv)] + small_grads
        if l + 1 < DEPTH:
            exchanged[l + 1] = all_to_all_wait(*pending.pop(l + 1), dx, f"exchange_wait_{l + 1}")
        send, recv, thru, lands, token = all_to_all_start(pieces, f"exchange_start_{l}")
        if l == 0:
            upper = {i: exchanged[i] for i in range(1, DEPTH)}
            upper, token = lax.optimization_barrier((upper, token))
            exchanged.update(upper)
        dx = dx + token[0, 0]
        pending[l] = (send, recv, thru, lands)

    for l in reversed(range(DEPTH)):
        kind, j = l % 3, l // 3
        if l == 0:
            keys = [(n, i) for n in ('w_ffn_up', 'w_ffn_down') for i in range(1, DEPTH)]
            done_above = lax.optimization_barrier(tuple(results[n][i][0] for n, i in keys))
            for (n, i), val in zip(keys, done_above):
                results[n][i][0] = val
            exchanged[0] = all_to_all_wait(*pending.pop(0), done_above[0], "exchange_wait_0")
        got = exchanged[l]
        in_name = ('a_w_in', 'b_w_in', 'c_w_in')[kind]
        for name, idx, pc in (('w_mem_kv', l, got[0]), ('w_out', l, got[1]), ('w_ffn_up', l, got[2]),
                              ('w_ffn_down', l, got[3]), (in_name, j, got[4]), ('ffn_conv', l, got[5])):
            apply_adam(name, idx, pc, l)
        if kind == 1:
            for name, pc in zip(('b_w_decay_up', 'b_w_iclr_up', 'b_w_gate_up'), got[6:]):
                apply_adam(name, j, pc, l)
        elif kind == 2:
            apply_adam('c_conv', j, got[6], l)

    rep_vals = []
    for n in REPLICATED:
        gval = rep_grads[n]
        gval = jnp.stack(gval) if isinstance(gval, list) else gval
        rep_vals.append(gval.reshape(a[n].shape))
    shapes = [a[n].shape for n in REPLICATED] + [(1,)]
    part = _pack(rep_vals + [loss_part.reshape(1)])
    gathered = all_gather_many([part], "gather_small_grads")[0]
    zero = jnp.zeros((1,), F32)
    packed = lambda pre: _pack([a[pre + n] for n in REPLICATED] + [zero])
    rep_out = adamw_sum(gathered, packed(''), packed('m_'), packed('v_'), "adamw_replicated")
    rep_out = [_unpack(o, shapes) for o in rep_out]
    loss = rep_out[0][-1][0]
    for i, n in enumerate(REPLICATED):
        results[n] = [o[i] for o in rep_out]

    def leaf(name, which):
        r = results[name]
        if isinstance(r, dict):
            return jnp.stack([r[i][which] for i in range(len(r))])
        return r[which]

    outs = [loss, dx[None]]
    for which in range(4):
        outs += [leaf(n, which) for n in WEIGHTS]
    return tuple(outs)
```

```python
import functools

import numpy as np
import jax
import jax.numpy as jnp
from jax import lax
from jax.experimental import pallas as pl
from jax.experimental.pallas import tpu as pltpu
from jax.experimental.pallas import tpu_sc as plsc

F32, BF16 = jnp.float32, jnp.bfloat16
HI = lax.Precision.HIGHEST
V7X_VMEM_BYTES = 64 * 1024 * 1024
VMEM_LIMIT = V7X_VMEM_BYTES - 8 * 1024 * 1024
MM_TILE_BUDGET = 40 * 1024 * 1024
SUBLANES, LANES = 8, 128
N_DEV = 8

D_MODEL = 2048
DEPTH = 4
MIX_WIDTH = 1536
MEM_HEADS, MEM_HEAD_DIM, MEM_WIDTH = 4, 128, 512
NORM_EPS = 1e-6
SWA_HEAD_DIM, SWA_Q_HEADS, SWA_KV_HEADS, SWA_GROUP, SWA_BLOCK = 64, 24, 4, 6, 128
RWKV_HEADS, RWKV_HEAD_DIM, RWKV_GN_EPS = 24, 64, 64e-5
RWKV_DECAY_RANK, RWKV_ICLR_RANK, RWKV_GATE_RANK = 96, 96, 256
GDN_HEAD_DIM, GDN_V_HEADS, GDN_QK_HEADS, GDN_CONV, GDN_CHUNK = 128, 12, 6, 4, 64
GDN_QK_WIDTH = GDN_QK_HEADS * GDN_HEAD_DIM
GDN_CONV_WIDTH = 2 * GDN_QK_WIDTH + MIX_WIDTH
D_FF, FFN_CONV = 5632, 3
ADAM_LR, ADAM_B1, ADAM_B2, ADAM_EPS, ADAM_WD, ADAM_STEP = 0.001, 0.9, 0.999, 1e-08, 0.01, 10
MESH = pl.DeviceIdType.MESH


def _cp(sem=None):
    return pltpu.CompilerParams(dimension_semantics=sem, vmem_limit_bytes=VMEM_LIMIT)


def _tile(n, cands):
    for c in cands:
        if n % c == 0:
            return c
    return n


_DIMS = {'nn': (((1,), (0,)), ((), ())), 'nt': (((1,), (1,)), ((), ())), 'tn': (((0,), (0,)), ((), ()))}


def _split(x):
    hi = lax.bitcast_convert_type(lax.bitcast_convert_type(x, jnp.uint32) & jnp.uint32(0xFFFF0000), F32)
    return hi.astype(BF16), (x - hi).astype(BF16)


def _dot3_raw(a, b, form):
    one = lambda p, q: lax.dot_general(p, q, _DIMS[form], preferred_element_type=F32)
    (a_hi, a_lo), (b_hi, b_lo) = _split(a), _split(b)
    return one(a_hi, b_hi) + (one(a_hi, b_lo) + one(a_lo, b_hi))


@functools.partial(jax.custom_vjp, nondiff_argnums=(2,))
def _dot3(a, b, form):
    return _dot3_raw(a, b, form)


def _dot3_fwd(a, b, form):
    return _dot3_raw(a, b, form), (a, b)


def _dot3_bwd(form, saved, dc):
    a, b = saved
    if form == 'nn':
        return _dot3(dc, b, 'nt'), _dot3(a, dc, 'tn')
    if form == 'nt':
        return _dot3(dc, b, 'nn'), _dot3(dc, a, 'tn')
    return _dot3(b, dc, 'nt'), _dot3(a, dc, 'nn')


_dot3.defvjp(_dot3_fwd, _dot3_bwd)


def _dot(a, b):
    return _dot3(a, b, 'nn')


def _dot_nt(a, b):
    return _dot3(a, b, 'nt')


def _dot_tn(a, b):
    return _dot3(a, b, 'tn')


def _sigmoid(x):
    return 1.0 / (1.0 + jnp.exp(-x))


def _softplus(x):
    return jnp.maximum(x, 0.0) + jnp.log(1.0 + jnp.exp(-jnp.abs(x)))


def _silu(x):
    return x * _sigmoid(x)


def mm(a, b, *, ta=False, tb=False, res=None, out_dtype=F32, name):
    (k_a, m) = a.shape if ta else a.shape[::-1]
    (k_b, n) = b.shape[::-1] if tb else b.shape
    assert k_a == k_b, (a.shape, b.shape, ta, tb)
    kdim = k_a
    tm = _tile(m, (1024, 512, 256))
    tn = _tile(n, (1024, 768, 512, 384, 256, 128))

    def vmem_bytes(tk):
        tiles = tk * (tm * a.dtype.itemsize + tn * b.dtype.itemsize) + tm * tn * jnp.dtype(out_dtype).itemsize
        tiles += 0 if res is None else tm * tn * res.dtype.itemsize
        return 2 * tiles + (0 if tk == kdim else tm * tn * 4)

    tks = [t for t in (kdim, kdim // 2, kdim // 4, 1024, 512, 256, 128) if kdim % t == 0 and t % LANES == 0]
    tk = next(t for t in tks if vmem_bytes(t) <= MM_TILE_BUDGET)
    nk = kdim // tk
    dims = (((0 if ta else 1,), (1 if tb else 0,)), ((), ()))

    def body(*refs):
        a_ref, b_ref = refs[:2]
        r_ref = None if res is None else refs[2]
        o_ref = refs[2 if res is None else 3]
        part = lax.dot_general(a_ref[...].astype(BF16), b_ref[...].astype(BF16), dims, preferred_element_type=F32)

        def finish(total):
            o_ref[...] = (total if res is None else total + r_ref[...]).astype(o_ref.dtype)

        if nk == 1:
            finish(part)
            return
        acc, kk = refs[-1], pl.program_id(2)

        @pl.when(kk == 0)
        def _():
            acc[...] = part

        @pl.when((kk > 0) & (kk < nk - 1))
        def _():
            acc[...] += part

        @pl.when(kk == nk - 1)
        def _():
            finish(acc[...] + part)

    a_spec = pl.BlockSpec((tk, tm), lambda i, j, k: (k, i)) if ta else pl.BlockSpec((tm, tk), lambda i, j, k: (i, k))
    b_spec = pl.BlockSpec((tn, tk), lambda i, j, k: (j, k)) if tb else pl.BlockSpec((tk, tn), lambda i, j, k: (k, j))
    o_spec = pl.BlockSpec((tm, tn), lambda i, j, k: (i, j))
    in_specs, args = [a_spec, b_spec], [a, b]
    if res is not None:
        in_specs.append(o_spec)
        args.append(res)
    return pl.pallas_call(
        body, grid=(m // tm, n // tn, nk), in_specs=in_specs, out_specs=o_spec,
        out_shape=jax.ShapeDtypeStruct((m, n), out_dtype),
        scratch_shapes=[] if nk == 1 else [pltpu.VMEM((tm, tn), F32)],
        compiler_params=_cp(("parallel", "parallel", "arbitrary")), name=name)(*args)


def _coords():
    return lax.axis_index("x"), lax.axis_index("y"), lax.axis_index("c")


def _block_index(p):
    return 4 * p[0] + 2 * p[1] + p[2]


def _all_gather_body(x_refs, o_refs, send, recv, loc):
    n = len(x_refs)
    x, y, c = _coords()
    me, sib = (x, y, c), (x, y, 1 - c)
    chips = [(1 - x, y), (x, 1 - y), (1 - x, 1 - y)]

    def cp(i, k, block, to, src=None):
        dst = o_refs[i].at[_block_index(block)]
        return pltpu.make_async_remote_copy(
            src_ref=dst if src is None else src, dst_ref=dst, send_sem=send.at[i, k], recv_sem=recv.at[i, k],
            device_id=to, device_id_type=MESH)

    mine = [pltpu.make_async_copy(x_refs[i], o_refs[i].at[_block_index(me)], loc.at[i]) for i in range(n)]
    for m_ in mine:
        m_.start()
    first = []
    for i in range(n):
        first.append(cp(i, 0, me, sib, src=x_refs[i]))
        for j, chip in enumerate(chips):
            first.append(cp(i, 1 + j, me, (*chip, c), src=x_refs[i]))
    for f in first:
        f.start()
    passed = []
    for j, chip in enumerate(chips):
        for i in range(n):
            cp(i, 1 + j, (*chip, c), me).wait_recv()
            p = cp(i, 4 + j, (*chip, c), sib)
            p.start()
            passed.append(p)
    for i in range(n):
        cp(i, 0, sib, me).wait_recv()
        for j, chip in enumerate(chips):
            cp(i, 4 + j, (*chip, 1 - c), me).wait_recv()
    for f in first + passed:
        f.wait_send()
    for m_ in mine:
        m_.wait()


def _all_gather_peers():
    x, y, c = _coords()
    return [(x, y, 1 - c), (1 - x, y, c), (x, 1 - y, c), (1 - x, 1 - y, c)]


def _all_to_all_peers():
    x, y, c = _coords()
    return [(1 - x if r & 4 else x, 1 - y if r & 2 else y, 1 - c if r & 1 else c) for r in range(1, N_DEV)]


def _comm_scratch(n):
    return [pltpu.SemaphoreType.DMA((n, 7)), pltpu.SemaphoreType.DMA((n, 7)), pltpu.SemaphoreType.DMA((n,))]


def all_gather_many(xs, name):
    n = len(xs)

    def body(*refs):
        _all_gather_body(refs[:n], refs[n:2 * n], *refs[2 * n:])

    any_spec = pl.BlockSpec(memory_space=pl.ANY)
    return pl.pallas_call(
        body, in_specs=[any_spec] * n, out_specs=[any_spec] * n,
        out_shape=[jax.ShapeDtypeStruct((N_DEV,) + x.shape, x.dtype) for x in xs],
        scratch_shapes=_comm_scratch(n), name=name)(*xs)


def _on_sequencer(exchange, peers, xs, out_shapes, name, collective_id):
    x_refs = [jax.new_ref(x, memory_space=pltpu.MemorySpace.HBM) for x in xs]
    o_refs = [jax.empty_ref(s, memory_space=pltpu.MemorySpace.HBM) for s in out_shapes]

    @pl.kernel(mesh=plsc.ScalarSubcoreMesh(axis_name="sequencer", num_cores=1), name=name,
               scratch_types=tuple(_comm_scratch(len(xs))),
               compiler_params=pltpu.CompilerParams(collective_id=collective_id))
    def launch(send, recv, loc):
        barrier = pltpu.get_barrier_semaphore()
        ids = peers()
        for peer in ids:
            pl.semaphore_signal(barrier, inc=1, device_id=peer, device_id_type=MESH)
        pl.semaphore_wait(barrier, len(ids))
        exchange(x_refs, o_refs, send, recv, loc)

    launch()
    return [o[...] for o in o_refs]


def _all_to_all_copies(x_refs, o_refs, send, recv, arrivals):
    x, y, c = _coords()
    me = _block_index((x, y, c))
    copies = []
    for r, peer in enumerate(_all_to_all_peers()):
        pidx = _block_index(peer)
        for i in range(len(x_refs)):
            copies.append(pltpu.make_async_remote_copy(
                src_ref=x_refs[i].at[pidx], dst_ref=o_refs[i].at[pidx if arrivals else me],
                send_sem=send.at[7 * i + r], recv_sem=recv.at[7 * i + r], device_id=peer, device_id_type=MESH))
    return copies


def all_to_all_start(xs, name):
    n = len(xs)
    x, y, c = _coords()
    me = _block_index((x, y, c))
    lands = [lax.dynamic_update_slice_in_dim(lax.empty(v.shape, v.dtype), lax.dynamic_slice_in_dim(v, me, 1, 0), me, 0)
             for v in xs]

    def body(*refs):
        x_refs, o_refs = refs[:n], refs[n:2 * n]
        send, recv = refs[2 * n], refs[2 * n + 1]
        token = refs[-1]
        for s in _all_to_all_copies(x_refs, o_refs, send, recv, arrivals=False):
            s.start()
        token[...] = jnp.zeros_like(token)

    hbm = pl.BlockSpec(memory_space=pltpu.HBM)
    sem = pl.BlockSpec(memory_space=pltpu.SEMAPHORE)
    out = pl.pallas_call(
        body, name=name,
        out_shape=(pltpu.SemaphoreType.DMA((7 * n,)), pltpu.SemaphoreType.DMA((7 * n,)),
                   *[pltpu.HBM(v.shape, v.dtype) for v in xs], *[pltpu.HBM(v.shape, v.dtype) for v in xs],
                   jax.ShapeDtypeStruct((SUBLANES, LANES), F32)),
        in_specs=[hbm] * (2 * n), out_specs=(sem, sem, *([hbm] * (2 * n)), pl.BlockSpec(memory_space=pltpu.VMEM)),
        input_output_aliases={i: 2 + i for i in range(2 * n)},
        compiler_params=pltpu.CompilerParams(has_side_effects=pltpu.SideEffectType.DATAFLOW_SIDE_EFFECTING),
    )(*[pltpu.with_memory_space_constraint(v, pltpu.HBM) for v in xs],
      *[pltpu.with_memory_space_constraint(v, pltpu.HBM) for v in lands])
    return out[0], out[1], list(out[2:2 + n]), list(out[2 + n:2 + 2 * n]), out[-1]


def all_to_all_wait(send, recv, xs, lands, after, name):
    n = len(xs)

    def body(*refs):
        x_refs, o_refs = refs[:n], refs[n:2 * n]
        send_ref, recv_ref = refs[2 * n], refs[2 * n + 1]
        for s in _all_to_all_copies(x_refs, o_refs, send_ref, recv_ref, arrivals=False):
            s.wait_send()
        for w in _all_to_all_copies(x_refs, o_refs, send_ref, recv_ref, arrivals=True):
            w.wait_recv()

    hbm = pl.BlockSpec(memory_space=pltpu.HBM)
    sem = pl.BlockSpec(memory_space=pltpu.SEMAPHORE)
    out = pl.pallas_call(
        body, name=name,
        out_shape=tuple(pltpu.HBM(v.shape, v.dtype) for v in list(xs) + list(lands)),
        in_specs=[hbm] * (2 * n) + [sem, sem, pl.BlockSpec(memory_space=pl.ANY)], out_specs=tuple([hbm] * (2 * n)),
        input_output_aliases={i: i for i in range(2 * n)},
        compiler_params=pltpu.CompilerParams(has_side_effects=pltpu.SideEffectType.DATAFLOW_SIDE_EFFECTING),
    )(*xs, *lands, send, recv, after)
    return list(out[n:])


def all_gather_many_async(xs, name, collective_id):
    shapes = [jax.ShapeDtypeStruct((N_DEV,) + x.shape, x.dtype) for x in xs]
    return _on_sequencer(_all_gather_body, _all_gather_peers, xs, shapes, name, collective_id)


def adamw_sum(pieces, w, m, v, name):
    rows, cols = w.shape
    tr = _tile(rows, (128, 64, 32, 16, 8))
    c1 = 1.0 - ADAM_B1 ** ADAM_STEP
    c2 = 1.0 - ADAM_B2 ** ADAM_STEP

    def body(p_ref, w_ref, m_ref, v_ref, g_out, d_out, m_out, v_out):
        g = p_ref[0].astype(F32)
        for s in range(1, N_DEV):
            g = g + p_ref[s].astype(F32)
        m_new = ADAM_B1 * m_ref[...] + (1.0 - ADAM_B1) * g
        v_new = ADAM_B2 * v_ref[...] + (1.0 - ADAM_B2) * (g * g)
        m_hat = m_new / c1
        v_hat = v_new / c2
        g_out[...] = g
        d_out[...] = -ADAM_LR * (m_hat / (jnp.sqrt(v_hat) + ADAM_EPS) + ADAM_WD * w_ref[...])
        m_out[...] = m_new
        v_out[...] = v_new

    spec = pl.BlockSpec((tr, cols), lambda i: (i, 0))
    out = jax.ShapeDtypeStruct((rows, cols), F32)
    return pl.pallas_call(
        body, grid=(rows // tr,), in_specs=[pl.BlockSpec((N_DEV, tr, cols), lambda i: (0, i, 0)), spec, spec, spec],
        out_specs=[spec] * 4, out_shape=[out] * 4, compiler_params=_cp(("parallel",)), name=name)(pieces, w, m, v)


def rmsnorm_fwd(x, g, out_dtype, name):
    s, d = x.shape
    tr = _tile(s, (256, 128, 64, 32, 16))

    def body(x_ref, g_ref, o_ref):
        xv = x_ref[...]
        rstd = lax.rsqrt(jnp.mean(xv * xv, axis=-1, keepdims=True) + NORM_EPS)
        o_ref[...] = (xv * rstd * g_ref[...]).astype(o_ref.dtype)

    return pl.pallas_call(
        body, grid=(s // tr,), in_specs=[pl.BlockSpec((tr, d), lambda i: (i, 0)), pl.BlockSpec((1, d), lambda i: (0, 0))],
        out_specs=pl.BlockSpec((tr, d), lambda i: (i, 0)), out_shape=jax.ShapeDtypeStruct((s, d), out_dtype),
        compiler_params=_cp(("parallel",)), name=name)(x, g.reshape(1, d))


def rmsnorm_bwd(x, g, dh, dres, name):
    s, d = x.shape
    tr = _tile(s, (256, 128, 64, 32, 16))

    def body(*refs):
        if dres is None:
            x_ref, g_ref, dh_ref, dx_ref, dg_ref = refs
        else:
            x_ref, g_ref, dh_ref, dr_ref, dx_ref, dg_ref = refs
        xv = x_ref[...]
        rstd = lax.rsqrt(jnp.mean(xv * xv, axis=-1, keepdims=True) + NORM_EPS)
        xhat = xv * rstd
        dhv = dh_ref[...].astype(F32)
        dhg = dhv * g_ref[...]
        dx = rstd * (dhg - xhat * jnp.mean(dhg * xhat, axis=-1, keepdims=True))
        if dres is not None:
            dx = dx + dr_ref[...]
        dx_ref[...] = dx

        @pl.when(pl.program_id(0) == 0)
        def _():
            dg_ref[...] = jnp.zeros_like(dg_ref)

        dg_ref[...] += jnp.sum(dhv * xhat, axis=0, keepdims=True)

    row = pl.BlockSpec((tr, d), lambda i: (i, 0))
    vec = pl.BlockSpec((1, d), lambda i: (0, 0))
    ins = [x, g.reshape(1, d), dh] + ([] if dres is None else [dres])
    dx, dg = pl.pallas_call(
        body, grid=(s // tr,), in_specs=[row, vec, row] + ([] if dres is None else [row]), out_specs=[row, vec],
        out_shape=[jax.ShapeDtypeStruct((s, d), F32), jax.ShapeDtypeStruct((1, d), F32)],
        compiler_params=_cp(("arbitrary",)), name=name)(*ins)
    return dx, dg.reshape(d)


def final_loss(x, g, target, name):
    s, d = x.shape
    tr = _tile(s, (256, 128, 64, 32, 16))

    def body(x_ref, g_ref, t_ref, l_ref, dx_ref, dg_ref):
        xv = x_ref[...]
        rstd = lax.rsqrt(jnp.mean(xv * xv, axis=-1, keepdims=True) + NORM_EPS)
        xhat = xv * rstd
        err = xhat * g_ref[...] - t_ref[...]
        dy = err * (1.0 / d)
        dhg = dy * g_ref[...]
        dx_ref[...] = rstd * (dhg - xhat * jnp.mean(dhg * xhat, axis=-1, keepdims=True))

        @pl.when(pl.program_id(0) == 0)
        def _():
            dg_ref[...] = jnp.zeros_like(dg_ref)
            l_ref[...] = jnp.zeros_like(l_ref)

        dg_ref[...] += jnp.sum(dy * xhat, axis=0, keepdims=True)
        part = 0.5 * jnp.sum(jnp.mean(err * err, axis=-1, keepdims=True), axis=0, keepdims=True)
        l_ref[...] += jnp.broadcast_to(part, l_ref.shape)

    row = pl.BlockSpec((tr, d), lambda i: (i, 0))
    vec = pl.BlockSpec((1, d), lambda i: (0, 0))
    lspec = pl.BlockSpec((1, LANES), lambda i: (0, 0))
    loss, dx, dg = pl.pallas_call(
        body, grid=(s // tr,), in_specs=[row, vec, row], out_specs=[lspec, row, vec],
        out_shape=[jax.ShapeDtypeStruct((1, LANES), F32), jax.ShapeDtypeStruct((s, d), F32),
                   jax.ShapeDtypeStruct((1, d), F32)],
        compiler_params=_cp(("arbitrary",)), name=name)(x, g.reshape(1, d), target)
    return loss[0, 0], dx, dg.reshape(d)


def _shift_down(tile, prev8, j):
    if j == 0:
        return tile
    rt = pltpu.roll(tile, j, 0)
    rp = pltpu.roll(prev8, j, 0)
    rows = lax.broadcasted_iota(jnp.int32, prev8.shape, 0)
    top = jnp.where(rows < j, rp, rt[:SUBLANES])
    return jnp.concatenate([top, rt[SUBLANES:]], axis=0)


def _shift_up(tile, next8, j):
    if j == 0:
        return tile
    t = tile.shape[0]
    rt = pltpu.roll(tile, t - j, 0)
    rn = pltpu.roll(next8, SUBLANES - j, 0)
    rows = lax.broadcasted_iota(jnp.int32, next8.shape, 0)
    bot = jnp.where(rows >= SUBLANES - j, rn, rt[t - SUBLANES:])
    return jnp.concatenate([rt[:t - SUBLANES], bot], axis=0)


def _halo_specs(t_rows, s_rows, cols, col_of):
    per, last = t_rows // SUBLANES, s_rows // SUBLANES - 1
    prev = pl.BlockSpec((SUBLANES, cols), lambda j, i: (jnp.maximum(i * per - 1, 0), col_of(j)))
    nxt = pl.BlockSpec((SUBLANES, cols), lambda j, i: (jnp.minimum((i + 1) * per, last), col_of(j)))
    return prev, nxt


def ffn_act_fwd(u0, conv, name):
    s, two_f = u0.shape
    f = two_f // 2
    t, c = _tile(s, (256, 128, 64)), 512
    nc = f // c

    def body(g_ref, v_ref, gp_ref, vp_ref, wg_ref, wv_ref, a_ref):
        first = (pl.program_id(1) > 0).astype(F32)

        def conv_of(x_ref, p_ref, w_ref):
            x, p = x_ref[...], p_ref[...] * first
            return (w_ref[0:1, :] * _shift_down(x, p, 2) + w_ref[1:2, :] * _shift_down(x, p, 1) + w_ref[2:3, :] * x)

        ug = conv_of(g_ref, gp_ref, wg_ref)
        uv = conv_of(v_ref, vp_ref, wv_ref)
        a_ref[...] = (_silu(ug) * uv).astype(a_ref.dtype)

    gate = pl.BlockSpec((t, c), lambda j, i: (i, j))
    val = pl.BlockSpec((t, c), lambda j, i: (i, j + nc))
    gp, _ = _halo_specs(t, s, c, lambda j: j)
    vp, _ = _halo_specs(t, s, c, lambda j: j + nc)
    wg = pl.BlockSpec((FFN_CONV, c), lambda j, i: (0, j))
    wv = pl.BlockSpec((FFN_CONV, c), lambda j, i: (0, j + nc))
    return pl.pallas_call(
        body, grid=(nc, s // t), in_specs=[gate, val, gp, vp, wg, wv], out_specs=pl.BlockSpec((t, c), lambda j, i: (i, j)),
        out_shape=jax.ShapeDtypeStruct((s, f), BF16), compiler_params=_cp(("parallel", "parallel")),
        name=name)(u0, u0, u0, u0, conv, conv)


def ffn_act_bwd(u0, conv, da, name):
    s, two_f = u0.shape
    f = two_f // 2
    t, c = _tile(s, (256, 128, 64)), 512
    nc, nt = f // c, s // t

    def body(g_ref, v_ref, gp_ref, vp_ref, gn_ref, vn_ref, wg_ref, wv_ref, da_ref, dan_ref,
             dg_ref, dv_ref, dwg_ref, dwv_ref):
        i = pl.program_id(1)
        first, last = (i > 0).astype(F32), (i < nt - 1).astype(F32)
        zeros8 = jnp.zeros((SUBLANES, c), F32)

        def ext(x_ref, p_ref, n_ref):
            return jnp.concatenate([p_ref[...] * first, x_ref[...], n_ref[...] * last], axis=0)

        def taps(e):
            return pltpu.roll(e, 2, 0), pltpu.roll(e, 1, 0), e

        def conv_of(sh, w_ref):
            return w_ref[0:1, :] * sh[0] + w_ref[1:2, :] * sh[1] + w_ref[2:3, :] * sh[2]

        def conv_t(du, w_ref):
            n = du.shape[0]
            return w_ref[2:3, :] * du + w_ref[1:2, :] * pltpu.roll(du, n - 1, 0) + w_ref[0:1, :] * pltpu.roll(du, n - 2, 0)

        sg, sv = taps(ext(g_ref, gp_ref, gn_ref)), taps(ext(v_ref, vp_ref, vn_ref))
        ug, uv = conv_of(sg, wg_ref), conv_of(sv, wv_ref)
        dae = jnp.concatenate([zeros8, da_ref[...], dan_ref[...] * last], axis=0)
        sig = _sigmoid(ug)
        dug = dae * uv * (sig * (1.0 + ug * (1.0 - sig)))
        duv = dae * (ug * sig)
        dg_ref[...] = conv_t(dug, wg_ref)[SUBLANES:t + SUBLANES].astype(dg_ref.dtype)
        dv_ref[...] = conv_t(duv, wv_ref)[SUBLANES:t + SUBLANES].astype(dv_ref.dtype)

        @pl.when(i == 0)
        def _():
            dwg_ref[...] = jnp.zeros_like(dwg_ref)
            dwv_ref[...] = jnp.zeros_like(dwv_ref)

        def dconv(du, sh):
            d = du[SUBLANES:t + SUBLANES]
            return jnp.concatenate([jnp.sum(d * x[SUBLANES:t + SUBLANES], axis=0, keepdims=True) for x in sh], axis=0)

        dwg_ref[...] += dconv(dug, sg)
        dwv_ref[...] += dconv(duv, sv)

    gate = pl.BlockSpec((t, c), lambda j, i: (i, j))
    val = pl.BlockSpec((t, c), lambda j, i: (i, j + nc))
    gp, gn = _halo_specs(t, s, c, lambda j: j)
    vp, vn = _halo_specs(t, s, c, lambda j: j + nc)
    wg = pl.BlockSpec((FFN_CONV, c), lambda j, i: (0, j))
    wv = pl.BlockSpec((FFN_CONV, c), lambda j, i: (0, j + nc))
    wout = pl.BlockSpec((FFN_CONV, c), lambda j, i: (0, j))
    half = jax.ShapeDtypeStruct((s, f), BF16)
    dwh = jax.ShapeDtypeStruct((FFN_CONV, f), F32)
    return pl.pallas_call(
        body, grid=(nc, nt), in_specs=[gate, val, gp, vp, gn, vn, wg, wv, gate, gn],
        out_specs=[gate, gate, wout, wout], out_shape=[half, half, dwh, dwh],
        compiler_params=_cp(("parallel", "arbitrary")), name=name)(u0, u0, u0, u0, u0, u0, conv, conv, da, da)


def _softmax_rows(s, extra=None):
    m = jnp.max(s, axis=-1, keepdims=True)
    if extra is not None:
        m = jnp.maximum(m, extra)
    m = lax.stop_gradient(m)
    e = jnp.exp(s - m)
    den = jnp.sum(e, axis=-1, keepdims=True)
    if extra is not None:
        den = den + jnp.exp(extra - m)
    return e / den


def _mem_attn_fn(qs, ks, vs):
    outs = []
    for q, k, v in zip(qs, ks, vs):
        p = _softmax_rows(_dot_nt(q, k) * (MEM_HEAD_DIM ** -0.5))
        outs.append(_dot(p, v))
    return outs


def _mem_heads(q_ref, kv_ref):
    d = MEM_HEAD_DIM
    qs = [q_ref[:, h * d:(h + 1) * d] for h in range(MEM_HEADS)]
    ks = [kv_ref[:, h * d:(h + 1) * d] for h in range(MEM_HEADS)]
    vs = [kv_ref[:, MEM_WIDTH + h * d:MEM_WIDTH + (h + 1) * d] for h in range(MEM_HEADS)]
    return qs, ks, vs


def mem_attn_fwd(p, q_col, kv, name):
    s = p.shape[0]
    t = _tile(s, (256, 128))
    m = kv.shape[0]

    def body(q_ref, kv_ref, o_ref):
        outs = _mem_attn_fn(*_mem_heads(q_ref, kv_ref))
        o_ref[...] = jnp.concatenate(outs, axis=1).astype(o_ref.dtype)

    return pl.pallas_call(
        body, grid=(s // t,),
        in_specs=[pl.BlockSpec((t, MEM_WIDTH), lambda i: (i, q_col // MEM_WIDTH)),
                  pl.BlockSpec((m, 2 * MEM_WIDTH), lambda i: (0, 0))],
        out_specs=pl.BlockSpec((t, MEM_WIDTH), lambda i: (i, 0)), out_shape=jax.ShapeDtypeStruct((s, MEM_WIDTH), BF16),
        compiler_params=_cp(("parallel",)), name=name)(p, kv)


def mem_attn_bwd(p, q_col, kv, dcat, name):
    s = p.shape[0]
    t = _tile(s, (256, 128))
    m = kv.shape[0]
    d = MEM_HEAD_DIM

    def body(q_ref, kv_ref, dy_ref, dq_ref, dkv_ref):
        qs, ks, vs = _mem_heads(q_ref, kv_ref)
        _, vjp = jax.vjp(_mem_attn_fn, qs, ks, vs)
        dqs, dks, dvs = vjp([dy_ref[:, h * d:(h + 1) * d] for h in range(MEM_HEADS)])
        dq_ref[...] = jnp.concatenate(dqs, axis=1).astype(dq_ref.dtype)

        @pl.when(pl.program_id(0) == 0)
        def _():
            dkv_ref[...] = jnp.zeros_like(dkv_ref)

        dkv_ref[...] += jnp.concatenate(dks + dvs, axis=1)

    return pl.pallas_call(
        body, grid=(s // t,),
        in_specs=[pl.BlockSpec((t, MEM_WIDTH), lambda i: (i, q_col // MEM_WIDTH)),
                  pl.BlockSpec((m, 2 * MEM_WIDTH), lambda i: (0, 0)),
                  pl.BlockSpec((t, MEM_WIDTH), lambda i: (i, MIX_WIDTH // MEM_WIDTH))],
        out_specs=[pl.BlockSpec((t, MEM_WIDTH), lambda i: (i, 0)), pl.BlockSpec((m, 2 * MEM_WIDTH), lambda i: (0, 0))],
        out_shape=[jax.ShapeDtypeStruct((s, MEM_WIDTH), BF16), jax.ShapeDtypeStruct((m, 2 * MEM_WIDTH), F32)],
        compiler_params=_cp(("arbitrary",)), name=name)(p, kv, dcat)


def _swa_fn(qs, kcs, kps, vcs, vps, sinks, not_first):
    t = SWA_BLOCK
    qi = lax.broadcasted_iota(jnp.int32, (t, 2 * t), 0)
    kj = lax.broadcasted_iota(jnp.int32, (t, 2 * t), 1)
    dist = t + qi - kj
    valid = (dist >= 0) & (dist < t) & ((kj >= t) | not_first)
    distf = dist.astype(F32)
    outs = []
    for kh in range(SWA_KV_HEADS):
        kb = jnp.concatenate([kps[kh], kcs[kh]], axis=0)
        vb = jnp.concatenate([vps[kh], vcs[kh]], axis=0)
        for g in range(SWA_GROUP):
            h = kh * SWA_GROUP + g
            slope = 2.0 ** (-8.0 * (h + 1) / SWA_Q_HEADS)
            sc = _dot_nt(qs[h], kb) * (SWA_HEAD_DIM ** -0.5) - slope * distf
            sc = jnp.where(valid, sc, -jnp.inf)
            outs.append(_dot(_softmax_rows(sc, extra=sinks[h]), vb))
    return outs


def _swa_args(q_ref, kc_ref, kp_ref, vc_ref, vp_ref, sink_ref):
    d = SWA_HEAD_DIM
    qs = [q_ref[:, h * d:(h + 1) * d] for h in range(SWA_Q_HEADS)]
    per_kv = lambda ref: [ref[:, h * d:(h + 1) * d] for h in range(SWA_KV_HEADS)]
    sinks = [sink_ref[0:1, h:h + 1] for h in range(SWA_Q_HEADS)]
    return qs, per_kv(kc_ref), per_kv(kp_ref), per_kv(vc_ref), per_kv(vp_ref), sinks


def _swa_specs(nb, order):
    t, kvw = SWA_BLOCK, SWA_KV_HEADS * SWA_HEAD_DIM
    k_col, v_col = MIX_WIDTH // kvw, MIX_WIDTH // kvw + 1
    q = pl.BlockSpec((t, MIX_WIDTH), lambda n: (order(n), 0))
    kc = pl.BlockSpec((t, kvw), lambda n: (order(n), k_col))
    kp = pl.BlockSpec((t, kvw), lambda n: (jnp.maximum(order(n) - 1, 0), k_col))
    vc = pl.BlockSpec((t, kvw), lambda n: (order(n), v_col))
    vp = pl.BlockSpec((t, kvw), lambda n: (jnp.maximum(order(n) - 1, 0), v_col))
    sink = pl.BlockSpec((1, LANES), lambda n: (0, 0))
    return [q, kc, kp, vc, vp, sink]


def _pad_lanes(v):
    return jnp.pad(v.reshape(1, -1), ((0, 0), (0, LANES - v.size)))


def swa_fwd(p, sinks, name):
    s = p.shape[0]
    nb = s // SWA_BLOCK

    def body(q_ref, kc_ref, kp_ref, vc_ref, vp_ref, sink_ref, o_ref):
        outs = _swa_fn(*_swa_args(q_ref, kc_ref, kp_ref, vc_ref, vp_ref, sink_ref), pl.program_id(0) > 0)
        o_ref[...] = jnp.concatenate(outs, axis=1).astype(o_ref.dtype)

    return pl.pallas_call(
        body, grid=(nb,), in_specs=_swa_specs(nb, lambda n: n),
        out_specs=pl.BlockSpec((SWA_BLOCK, MIX_WIDTH), lambda n: (n, 0)),
        out_shape=jax.ShapeDtypeStruct((s, MIX_WIDTH), BF16), compiler_params=_cp(("parallel",)),
        name=name)(p, p, p, p, p, _pad_lanes(sinks))


def swa_bwd(p, sinks, dcat, name):
    s = p.shape[0]
    nb = s // SWA_BLOCK
    t, d, kvw = SWA_BLOCK, SWA_HEAD_DIM, SWA_KV_HEADS * SWA_HEAD_DIM
    rev = lambda n: nb - 1 - n

    def body(q_ref, kc_ref, kp_ref, vc_ref, vp_ref, sink_ref, dy_ref, dq_ref, dk_ref, dv_ref, ds_ref, ck, cv):
        n = pl.program_id(0)

        @pl.when(n == 0)
        def _():
            ck[...] = jnp.zeros_like(ck)
            cv[...] = jnp.zeros_like(cv)
            ds_ref[...] = jnp.zeros_like(ds_ref)

        args = _swa_args(q_ref, kc_ref, kp_ref, vc_ref, vp_ref, sink_ref)
        _, vjp = jax.vjp(functools.partial(_swa_fn, not_first=rev(n) > 0), *args)
        dqs, dkcs, dkps, dvcs, dvps, dsinks = vjp([dy_ref[:, h * d:(h + 1) * d] for h in range(SWA_Q_HEADS)])
        dq_ref[...] = jnp.concatenate(dqs, axis=1).astype(dq_ref.dtype)
        dk_ref[...] = (jnp.concatenate(dkcs, axis=1) + ck[...]).astype(dk_ref.dtype)
        dv_ref[...] = (jnp.concatenate(dvcs, axis=1) + cv[...]).astype(dv_ref.dtype)
        ck[...] = jnp.concatenate(dkps, axis=1)
        cv[...] = jnp.concatenate(dvps, axis=1)
        lane = lax.broadcasted_iota(jnp.int32, (1, LANES), 1)
        acc = jnp.zeros((1, LANES), F32)
        for h in range(SWA_Q_HEADS):
            acc = acc + jnp.where(lane == h, dsinks[h], 0.0)
        ds_ref[...] += acc

    dy = pl.BlockSpec((t, MIX_WIDTH), lambda n: (rev(n), 0))
    kv_out = pl.BlockSpec((t, kvw), lambda n: (rev(n), 0))
    dq, dk, dv, ds = pl.pallas_call(
        body, grid=(nb,), in_specs=_swa_specs(nb, rev) + [dy],
        out_specs=[dy, kv_out, kv_out, pl.BlockSpec((1, LANES), lambda n: (0, 0))],
        out_shape=[jax.ShapeDtypeStruct((s, MIX_WIDTH), BF16), jax.ShapeDtypeStruct((s, kvw), BF16),
                   jax.ShapeDtypeStruct((s, kvw), BF16), jax.ShapeDtypeStruct((1, LANES), F32)],
        scratch_shapes=[pltpu.VMEM((t, kvw), F32), pltpu.VMEM((t, kvw), F32)],
        compiler_params=_cp(("arbitrary",)), name=name)(p, p, p, p, p, _pad_lanes(sinks), dcat)
    return dq, dk, dv, ds[0, :SWA_Q_HEADS]


RW_SHIFT = 5120
RW_R, RW_K, RW_V, RW_WD, RW_AD, RW_GD = 0, 1536, 3072, 4608, 4736, 4864


def _head_matrix(width, head_dim):
    e = (np.arange(width)[:, None] // head_dim == np.arange(LANES)[None, :]).astype(np.float32)
    return jnp.asarray(e), jnp.asarray(e.T)


def _rwkv_pre_fn(pieces, shifted, mus, w0, wdu, a0, wiu, wgu, k_k, k_a, e, et):
    r, k, v, wd, ad, gd = [p + (s - p) * mu for p, s, mu in zip(pieces, shifted, mus)]
    w_log = -_softplus(-(w0 + _dot(jnp.tanh(wd), wdu))) - 0.5
    lw = -jnp.exp(w_log)
    a = _sigmoid(a0 + _dot(ad, wiu))
    g = _dot(_sigmoid(gd), wgu)
    kkr = k * k_k
    kk = kkr * _dot(lax.rsqrt(_dot(kkr * kkr, e) + 1e-6), et)
    k2 = k * (1.0 + (a - 1.0) * k_a)
    return r, lw, k2, v, kk, kk * a, g


_RW_GROUPS = ((RW_R, MIX_WIDTH), (RW_K, MIX_WIDTH), (RW_V, MIX_WIDTH), (RW_WD, LANES), (RW_AD, LANES), (RW_GD, 2 * LANES))


def _rwkv_pre_inputs(p_ref, prev_ref, mu_ref, first):
    pieces = [p_ref[:, o:o + n] for o, n in _RW_GROUPS]
    shifted = [_shift_down(p_ref[:, o:o + n], prev_ref[:, o:o + n] * first, 1) for o, n in _RW_GROUPS]
    mus = [mu_ref[:, o:o + n] for o, n in _RW_GROUPS]
    return pieces, shifted, mus


def _rwkv_param_specs():
    vec = lambda n: pl.BlockSpec((1, n), lambda i: (0, 0))
    mat = lambda r, c: pl.BlockSpec((r, c), lambda i: (0, 0))
    return [vec(RW_SHIFT), vec(MIX_WIDTH), mat(LANES, MIX_WIDTH), vec(MIX_WIDTH), mat(LANES, MIX_WIDTH),
            mat(2 * LANES, MIX_WIDTH), vec(MIX_WIDTH), vec(MIX_WIDTH), mat(MIX_WIDTH, LANES), mat(LANES, MIX_WIDTH)]


def rwkv_pre_fwd(p, params, name):
    s = p.shape[0]
    t = _tile(s, (128, 64))

    def body(p_ref, prev_ref, mu_ref, *rest):
        prm, outs = rest[:9], rest[9:]
        first = (pl.program_id(0) > 0).astype(F32)
        pieces, shifted, mus = _rwkv_pre_inputs(p_ref, prev_ref, mu_ref, first)
        res = _rwkv_pre_fn(pieces, shifted, mus, *[q[...] for q in prm])
        for o_ref, val in zip(outs, res):
            o_ref[...] = val

    row = pl.BlockSpec((t, RW_SHIFT), lambda i: (i, 0))
    prev = pl.BlockSpec((SUBLANES, RW_SHIFT), lambda i: (jnp.maximum(i * (t // SUBLANES) - 1, 0), 0))
    out = pl.BlockSpec((t, MIX_WIDTH), lambda i: (i, 0))
    return pl.pallas_call(
        body, grid=(s // t,), in_specs=[row, prev] + _rwkv_param_specs(), out_specs=[out] * 7,
        out_shape=[jax.ShapeDtypeStruct((s, MIX_WIDTH), F32)] * 7, compiler_params=_cp(("parallel",)),
        name=name)(p, p, *params)


def rwkv_pre_bwd(p, params, cots, name):
    s = p.shape[0]
    t = _tile(s, (64, 32))

    def body(p_ref, prev_ref, mu_ref, *rest):
        prm, cot, outs = rest[:9], rest[9:19], rest[19:]
        dp_ref, dps_ref, grads = outs[0], outs[1], outs[2:]
        i = pl.program_id(0)
        first = (i > 0).astype(F32)
        pieces, shifted, mus = _rwkv_pre_inputs(p_ref, prev_ref, mu_ref, first)
        prm_v = [q[...] for q in prm]
        fn = lambda pieces, shifted, mus, *small: _rwkv_pre_fn(pieces, shifted, mus, *small, prm_v[7], prm_v[8])
        _, vjp = jax.vjp(fn, pieces, shifted, mus, *prm_v[:7])
        dr, dw, dk2, dv, dkk, db, dr2, dk22, dv2, dg = [c[...] for c in cot]
        res = vjp((dr + dr2, dw, dk2 + dk22, dv + dv2, dkk, db, dg))
        dpieces, dshifted, dmus, dsmall = res[0], res[1], res[2], res[3:]
        for (o, n), dpi, dsi in zip(_RW_GROUPS, dpieces, dshifted):
            dp_ref[:, o:o + n] = dpi
            dps_ref[:, o:o + n] = dsi

        @pl.when(i == 0)
        def _():
            for g_ref in grads:
                g_ref[...] = jnp.zeros_like(g_ref)

        for (o, n), dmu in zip(_RW_GROUPS, dmus):
            grads[0][:, o:o + n] += dmu
        for g_ref, dval in zip(grads[1:], dsmall):
            g_ref[...] += dval

    row = pl.BlockSpec((t, RW_SHIFT), lambda i: (i, 0))
    prev = pl.BlockSpec((SUBLANES, RW_SHIFT), lambda i: (jnp.maximum(i * (t // SUBLANES) - 1, 0), 0))
    act = pl.BlockSpec((t, MIX_WIDTH), lambda i: (i, 0))
    pspecs = _rwkv_param_specs()
    full = jax.ShapeDtypeStruct((s, RW_SHIFT), F32)
    gshapes = [jax.ShapeDtypeStruct(q.shape, F32) for q in params[:8]]
    return pl.pallas_call(
        body, grid=(s // t,), in_specs=[row, prev] + pspecs + [act] * 10, out_specs=[row, row] + pspecs[:8],
        out_shape=[full, full] + gshapes, compiler_params=_cp(("arbitrary",)), name=name)(p, p, *params, *cots)


def shift_add(a, b, js, out_dtype, name):
    s, c = a.shape
    t = _tile(s, (256, 128, 64))
    tc = _tile(c, (1024, 768, 512, 640, 384, 256, 128))
    nt, nb = s // t, len(b)

    def body(a_ref, *rest):
        b_refs, n_refs, o_ref = rest[:nb], rest[nb:2 * nb], rest[2 * nb]
        last = (pl.program_id(1) < nt - 1).astype(F32)
        acc = a_ref[...]
        for b_ref, n_ref, j in zip(b_refs, n_refs, js):
            acc = acc + _shift_up(b_ref[...], n_ref[...] * last, j)
        o_ref[...] = acc.astype(o_ref.dtype)

    tile = pl.BlockSpec((t, tc), lambda j, i: (i, j))
    _, nxt = _halo_specs(t, s, tc, lambda j: j)
    return pl.pallas_call(
        body, grid=(c // tc, nt), in_specs=[tile] * (1 + nb) + [nxt] * nb, out_specs=tile,
        out_shape=jax.ShapeDtypeStruct((s, c), out_dtype), compiler_params=_cp(("parallel", "parallel")),
        name=name)(a, *b, *b)


def _rwkv_post_fn(y, r, k2, v, g, gn_g, gn_b, r_k, e, et):
    n = RWKV_HEAD_DIM
    yc = y - _dot(_dot(y, e), et) * (1.0 / n)
    rstd = lax.rsqrt(_dot(yc * yc, e) * (1.0 / n) + RWKV_GN_EPS)
    yn = yc * _dot(rstd, et) * gn_g + gn_b
    bonus = _dot(_dot(r * k2 * r_k, e), et) * v
    return (yn + bonus) * g


def rwkv_post_fwd(acts, params, name):
    s = acts[0].shape[0]
    t = _tile(s, (256, 128))

    def body(*refs):
        vals = [q[...] for q in refs[:10]]
        refs[10][...] = _rwkv_post_fn(*vals).astype(refs[10].dtype)

    act = pl.BlockSpec((t, MIX_WIDTH), lambda i: (i, 0))
    vec = pl.BlockSpec((1, MIX_WIDTH), lambda i: (0, 0))
    mats = [pl.BlockSpec((MIX_WIDTH, LANES), lambda i: (0, 0)), pl.BlockSpec((LANES, MIX_WIDTH), lambda i: (0, 0))]
    return pl.pallas_call(
        body, grid=(s // t,), in_specs=[act] * 5 + [vec] * 3 + mats, out_specs=act,
        out_shape=jax.ShapeDtypeStruct((s, MIX_WIDTH), BF16), compiler_params=_cp(("parallel",)),
        name=name)(*acts, *params)


def rwkv_post_bwd(acts, params, dcat, name):
    s = acts[0].shape[0]
    t = _tile(s, (128, 64))

    def body(*refs):
        ins, dy_ref, outs = refs[:10], refs[10], refs[11:]
        vals = [q[...] for q in ins]
        fn = lambda *a: _rwkv_post_fn(*a, vals[8], vals[9])
        _, vjp = jax.vjp(fn, *vals[:8])
        res = vjp(dy_ref[...])
        for o_ref, val in zip(outs[:5], res[:5]):
            o_ref[...] = val

        @pl.when(pl.program_id(0) == 0)
        def _():
            for g_ref in outs[5:]:
                g_ref[...] = jnp.zeros_like(g_ref)

        for g_ref, val in zip(outs[5:], res[5:]):
            g_ref[...] += val

    act = pl.BlockSpec((t, MIX_WIDTH), lambda i: (i, 0))
    vec = pl.BlockSpec((1, MIX_WIDTH), lambda i: (0, 0))
    mats = [pl.BlockSpec((MIX_WIDTH, LANES), lambda i: (0, 0)), pl.BlockSpec((LANES, MIX_WIDTH), lambda i: (0, 0))]
    a_shape = jax.ShapeDtypeStruct((s, MIX_WIDTH), F32)
    v_shape = jax.ShapeDtypeStruct((1, MIX_WIDTH), F32)
    return pl.pallas_call(
        body, grid=(s // t,), in_specs=[act] * 5 + [vec] * 3 + mats + [act], out_specs=[act] * 5 + [vec] * 3,
        out_shape=[a_shape] * 5 + [v_shape] * 3, compiler_params=_cp(("arbitrary",)), name=name)(*acts, *params, dcat)


RW_CHUNK = 64


RW_HEADS_PER_STEP = 4


def _heads(f, *per_head):
    return [f(*xs) for xs in zip(*per_head)]


def _rwkv_chunk_fn(r, lw, k, v, kk, b, st):
    c = RW_CHUNK
    ri = lax.broadcasted_iota(jnp.int32, (c, c), 0)
    ci = lax.broadcasted_iota(jnp.int32, (c, c), 1)
    incl, strict = ri >= ci, ri > ci
    ltri = incl.astype(F32)
    eye = (ri == ci).astype(F32)
    last_col = (ci == c - 1).astype(F32)
    last_row = (ri == c - 1).astype(F32)
    gc = _heads(lambda lw: _dot(ltri, lw), lw)
    a_t = _heads(lambda kk, gc, lw: -kk * jnp.exp(gc - lw), kk, gc, lw)
    e_neg = _heads(lambda gc: jnp.exp(-gc), gc)
    b_t = _heads(jnp.multiply, b, e_neg)
    k_t = _heads(jnp.multiply, k, e_neg)
    r_t = _heads(lambda r, gc: r * jnp.exp(gc), r, gc)
    m_ab = _heads(lambda x, y: jnp.where(strict, _dot_nt(x, y), 0.0), a_t, b_t)
    m_ak = _heads(lambda x, y: jnp.where(strict, _dot_nt(x, y), 0.0), a_t, k_t)
    m_rb = _heads(lambda x, y: jnp.where(incl, _dot_nt(x, y), 0.0), r_t, b_t)
    m_rk = _heads(lambda x, y: jnp.where(incl, _dot_nt(x, y), 0.0), r_t, k_t)
    tinv, pw = _heads(lambda m: eye + m, m_ab), m_ab
    for _ in range(5):
        pw = _heads(lambda p: _dot(p, p), pw)
        tinv = _heads(lambda t, p: t + _dot(t, p), tinv, pw)
    rhs = _heads(lambda a, s, m, v: _dot(a, s) + _dot(m, v), a_t, st, m_ak, v)
    u = _heads(_dot, tinv, rhs)
    y = _heads(lambda r, s, mb, u, mk, v: _dot(r, s) + _dot(mb, u) + _dot(mk, v), r_t, st, m_rb, u, m_rk, v)
    dec = _heads(lambda gc: jnp.exp(_dot(last_col, gc) - gc), gc)
    g_end = _heads(lambda gc: _dot_tn(gc, last_row), gc)
    new_st = _heads(lambda s, g, b, d, u, k, v: s * jnp.exp(g) + _dot_tn(b * d, u) + _dot_tn(k * d, v),
                    st, g_end, b, dec, u, k, v)
    return y, new_st


def rwkv_scan_fwd(r, lw, k, v, kk, b, name):
    s = r.shape[0]
    n, hp = RWKV_HEAD_DIM, RW_HEADS_PER_STEP
    nchunk, width = s // RW_CHUNK, RWKV_HEAD_DIM * RW_HEADS_PER_STEP

    def body(r_ref, w_ref, k_ref, v_ref, kk_ref, b_ref, y_ref, ck_ref, carry):
        @pl.when(pl.program_id(1) == 0)
        def _():
            carry[...] = jnp.zeros_like(carry)

        ck_ref[0] = carry[...]
        per_head = lambda ref: [ref[:, h * n:(h + 1) * n] for h in range(hp)]
        ys, sts = _rwkv_chunk_fn(*[per_head(q) for q in (r_ref, w_ref, k_ref, v_ref, kk_ref, b_ref, carry)])
        y_ref[...] = jnp.concatenate(ys, axis=1)
        carry[...] = jnp.concatenate(sts, axis=1)

    blk = pl.BlockSpec((RW_CHUNK, width), lambda j, c: (c, j))
    return pl.pallas_call(
        body, grid=(MIX_WIDTH // width, nchunk), in_specs=[blk] * 6,
        out_specs=[blk, pl.BlockSpec((1, n, width), lambda j, c: (c, 0, j))],
        out_shape=[jax.ShapeDtypeStruct((s, MIX_WIDTH), F32), jax.ShapeDtypeStruct((nchunk, n, MIX_WIDTH), F32)],
        scratch_shapes=[pltpu.VMEM((n, width), F32)],
        compiler_params=_cp(("parallel", "arbitrary")), name=name)(r, lw, k, v, kk, b)


def rwkv_scan_bwd(r, lw, k, v, kk, b, ck, dy, name):
    s = r.shape[0]
    n, hp = RWKV_HEAD_DIM, RW_HEADS_PER_STEP
    nchunk, width = s // RW_CHUNK, RWKV_HEAD_DIM * RW_HEADS_PER_STEP
    rev = lambda c: nchunk - 1 - c

    def body(r_ref, w_ref, k_ref, v_ref, kk_ref, b_ref, ck_ref, dy_ref, *rest):
        outs, carry = rest[:6], rest[6]

        @pl.when(pl.program_id(1) == 0)
        def _():
            carry[...] = jnp.zeros_like(carry)

        per_head = lambda ref: [ref[:, h * n:(h + 1) * n] for h in range(hp)]
        args = [per_head(q) for q in (r_ref, w_ref, k_ref, v_ref, kk_ref, b_ref)]
        args.append([ck_ref[0, :, h * n:(h + 1) * n] for h in range(hp)])
        _, vjp = jax.vjp(_rwkv_chunk_fn, *args)
        grads = vjp((per_head(dy_ref), per_head(carry)))
        for q in range(6):
            outs[q][...] = jnp.concatenate(grads[q], axis=1)
        carry[...] = jnp.concatenate(grads[6], axis=1)

    blk = pl.BlockSpec((RW_CHUNK, width), lambda j, c: (rev(c), j))
    out = jax.ShapeDtypeStruct((s, MIX_WIDTH), F32)
    return pl.pallas_call(
        body, grid=(MIX_WIDTH // width, nchunk),
        in_specs=[blk] * 6 + [pl.BlockSpec((1, n, width), lambda j, c: (rev(c), 0, j)), blk],
        out_specs=[blk] * 6, out_shape=[out] * 6, scratch_shapes=[pltpu.VMEM((n, width), F32)],
        compiler_params=_cp(("parallel", "arbitrary")), name=name)(r, lw, k, v, kk, b, ck, dy)


GD_Q, GD_K, GD_V, GD_Z, GD_QMEM, GD_BT, GD_AT, GD_COLS = 0, 768, 1536, 3072, 4608, 5120, 5248, 5376
_GD_GROUPS = ((GD_Q, GDN_QK_WIDTH), (GD_K, GDN_QK_WIDTH), (GD_V, MIX_WIDTH))


def _gdn_pre_fn(xs, convs, bt, at, a_log, dt_bias, e6, e6t, ebc):
    k_w = GDN_CONV
    acts = [_silu(sum(convs[g][j] * xs[g][k_w - 1 - j] for j in range(k_w))) for g in range(3)]
    l2 = lambda x: x * _dot(lax.rsqrt(_dot(x * x, e6) + 1e-6), e6t)
    beta = _sigmoid(bt)
    g = -jnp.exp(a_log) * _softplus(at + dt_bias)
    return l2(acts[0]), l2(acts[1]), acts[2], _dot(g, ebc), _dot(beta, ebc)


def _gdn_pre_inputs(x_ref, prev_ref, conv_ref, first):
    xs = [[_shift_down(x_ref[:, o:o + n], prev_ref[:, o:o + n] * first, j) for j in range(GDN_CONV)]
          for o, n in _GD_GROUPS]
    convs = [[conv_ref[j:j + 1, o:o + n] for j in range(GDN_CONV)] for o, n in _GD_GROUPS]
    return xs, convs


def _gdn_pre_specs(t):
    x = pl.BlockSpec((t, GDN_CONV_WIDTH), lambda i: (i, 0))
    prev = pl.BlockSpec((SUBLANES, GDN_CONV_WIDTH), lambda i: (jnp.maximum(i * (t // SUBLANES) - 1, 0), 0))
    bta = pl.BlockSpec((t, 2 * LANES), lambda i: (i, GD_BT // (2 * LANES)))
    conv = pl.BlockSpec((GDN_CONV, GDN_CONV_WIDTH), lambda i: (0, 0))
    vec = pl.BlockSpec((1, LANES), lambda i: (0, 0))
    mats = [pl.BlockSpec((GDN_QK_WIDTH, LANES), lambda i: (0, 0)), pl.BlockSpec((LANES, GDN_QK_WIDTH), lambda i: (0, 0)),
            pl.BlockSpec((LANES, MIX_WIDTH), lambda i: (0, 0))]
    return [x, prev, bta, conv, vec, vec] + mats


def gdn_pre_fwd(p, params, name):
    s = p.shape[0]
    t = _tile(s, (128, 64))

    def body(x_ref, prev_ref, bta_ref, conv_ref, al_ref, dt_ref, e6_ref, e6t_ref, ebc_ref, *outs):
        first = (pl.program_id(0) > 0).astype(F32)
        xs, convs = _gdn_pre_inputs(x_ref, prev_ref, conv_ref, first)
        res = _gdn_pre_fn(xs, convs, bta_ref[:, :LANES], bta_ref[:, LANES:], al_ref[...], dt_ref[...],
                          e6_ref[...], e6t_ref[...], ebc_ref[...])
        for o_ref, val in zip(outs, res):
            o_ref[...] = val

    qk = pl.BlockSpec((t, GDN_QK_WIDTH), lambda i: (i, 0))
    wide = pl.BlockSpec((t, MIX_WIDTH), lambda i: (i, 0))
    qk_s, wide_s = jax.ShapeDtypeStruct((s, GDN_QK_WIDTH), F32), jax.ShapeDtypeStruct((s, MIX_WIDTH), F32)
    return pl.pallas_call(
        body, grid=(s // t,), in_specs=_gdn_pre_specs(t), out_specs=[qk, qk, wide, wide, wide],
        out_shape=[qk_s, qk_s, wide_s, wide_s, wide_s], compiler_params=_cp(("parallel",)), name=name)(p, p, p, *params)


def gdn_pre_bwd(p, params, cots, name):
    s = p.shape[0]
    t = _tile(s, (64, 32))

    def body(x_ref, prev_ref, bta_ref, conv_ref, al_ref, dt_ref, e6_ref, e6t_ref, ebc_ref, *rest):
        cot, outs = rest[:5], rest[5:]
        dxs, dbta_ref, dconv_ref, dal_ref, ddt_ref = outs[:4], outs[4], outs[5], outs[6], outs[7]
        i = pl.program_id(0)
        first = (i > 0).astype(F32)
        xs, convs = _gdn_pre_inputs(x_ref, prev_ref, conv_ref, first)
        mats = (e6_ref[...], e6t_ref[...], ebc_ref[...])
        fn = lambda xs, convs, bt, at, al, dt: _gdn_pre_fn(xs, convs, bt, at, al, dt, *mats)
        _, vjp = jax.vjp(fn, xs, convs, bta_ref[:, :LANES], bta_ref[:, LANES:], al_ref[...], dt_ref[...])
        d_xs, d_convs, d_bt, d_at, d_al, d_dt = vjp(tuple(c[...] for c in cot))
        for g, (o, n) in enumerate(_GD_GROUPS):
            for j in range(GDN_CONV):
                dxs[j][:, o:o + n] = d_xs[g][j]
        dbta_ref[...] = jnp.concatenate([d_bt, d_at], axis=1).astype(dbta_ref.dtype)

        @pl.when(i == 0)
        def _():
            dconv_ref[...] = jnp.zeros_like(dconv_ref)
            dal_ref[...] = jnp.zeros_like(dal_ref)
            ddt_ref[...] = jnp.zeros_like(ddt_ref)

        for g, (o, n) in enumerate(_GD_GROUPS):
            for j in range(GDN_CONV):
                dconv_ref[j:j + 1, o:o + n] += d_convs[g][j]
        dal_ref[...] += d_al
        ddt_ref[...] += d_dt

    specs = _gdn_pre_specs(t)
    qk = pl.BlockSpec((t, GDN_QK_WIDTH), lambda i: (i, 0))
    wide = pl.BlockSpec((t, MIX_WIDTH), lambda i: (i, 0))
    x_s = jax.ShapeDtypeStruct((s, GDN_CONV_WIDTH), F32)
    vec_s = jax.ShapeDtypeStruct((1, LANES), F32)
    return pl.pallas_call(
        body, grid=(s // t,), in_specs=specs + [qk, qk, wide, wide, wide],
        out_specs=[specs[0]] * 4 + [pl.BlockSpec((t, 2 * LANES), lambda i: (i, 0)), specs[3], specs[4], specs[5]],
        out_shape=[x_s] * 4 + [jax.ShapeDtypeStruct((s, 2 * LANES), BF16),
                               jax.ShapeDtypeStruct((GDN_CONV, GDN_CONV_WIDTH), F32), vec_s, vec_s],
        compiler_params=_cp(("arbitrary",)), name=name)(p, p, p, *params, *cots)


def _gdn_post_fn(o, z, norm_g, e12, e12t, trep):
    rstd = lax.rsqrt(_dot(o * o, e12) * (1.0 / GDN_HEAD_DIM) + NORM_EPS)
    return o * _dot(rstd, e12t) * _dot(norm_g, trep) * _silu(z)


def _gdn_post_specs(t):
    act = pl.BlockSpec((t, MIX_WIDTH), lambda i: (i, 0))
    z = pl.BlockSpec((t, MIX_WIDTH), lambda i: (i, GD_Z // MIX_WIDTH))
    mats = [pl.BlockSpec((SUBLANES, LANES), lambda i: (0, 0)), pl.BlockSpec((MIX_WIDTH, LANES), lambda i: (0, 0)),
            pl.BlockSpec((LANES, MIX_WIDTH), lambda i: (0, 0)), pl.BlockSpec((LANES, MIX_WIDTH), lambda i: (0, 0))]
    return [act, z] + mats


def gdn_post_fwd(o, p, params, name):
    s = o.shape[0]
    t = _tile(s, (256, 128))

    def body(o_ref, z_ref, ng_ref, e_ref, et_ref, tr_ref, out_ref):
        res = _gdn_post_fn(o_ref[...], z_ref[...], ng_ref[0:1, :], e_ref[...], et_ref[...], tr_ref[...])
        out_ref[...] = res.astype(out_ref.dtype)

    act = pl.BlockSpec((t, MIX_WIDTH), lambda i: (i, 0))
    return pl.pallas_call(
        body, grid=(s // t,), in_specs=_gdn_post_specs(t), out_specs=act,
        out_shape=jax.ShapeDtypeStruct((s, MIX_WIDTH), BF16), compiler_params=_cp(("parallel",)),
        name=name)(o, p, *params)


def gdn_post_bwd(o, p, params, dcat, name):
    s = o.shape[0]
    t = _tile(s, (128, 64))

    def body(o_ref, z_ref, ng_ref, e_ref, et_ref, tr_ref, dy_ref, do_ref, dz_ref, dng_ref):
        mats = (e_ref[...], et_ref[...], tr_ref[...])
        fn = lambda o, z, ng: _gdn_post_fn(o, z, ng, *mats)
        _, vjp = jax.vjp(fn, o_ref[...], z_ref[...], ng_ref[0:1, :])
        d_o, d_z, d_ng = vjp(dy_ref[...])
        do_ref[...] = d_o
        dz_ref[...] = d_z.astype(dz_ref.dtype)

        @pl.when(pl.program_id(0) == 0)
        def _():
            dng_ref[...] = jnp.zeros_like(dng_ref)

        dng_ref[...] += d_ng

    act = pl.BlockSpec((t, MIX_WIDTH), lambda i: (i, 0))
    return pl.pallas_call(
        body, grid=(s // t,), in_specs=_gdn_post_specs(t) + [act],
        out_specs=[act, act, pl.BlockSpec((1, LANES), lambda i: (0, 0))],
        out_shape=[jax.ShapeDtypeStruct((s, MIX_WIDTH), F32), jax.ShapeDtypeStruct((s, MIX_WIDTH), BF16),
                   jax.ShapeDtypeStruct((1, LANES), F32)],
        compiler_params=_cp(("arbitrary",)), name=name)(o, p, *params, dcat)


GDN_REP = GDN_V_HEADS // GDN_QK_HEADS
GDN_QK_PER_STEP = 2


def _gdn_chunk_fn(q, k, v, gb, bb, gb64, state):
    c = GDN_CHUNK
    ri = lax.broadcasted_iota(jnp.int32, (c, c), 0)
    ci = lax.broadcasted_iota(jnp.int32, (c, c), 1)
    causal, strict = ri >= ci, ri > ci
    ltri = causal.astype(F32)
    eye = (ri == ci).astype(F32)
    first_col = (ci == 0).astype(F32)
    last_col = (ci == c - 1).astype(F32)
    last_col_tall = (lax.broadcasted_iota(jnp.int32, (GDN_HEAD_DIM, c), 1) == c - 1).astype(F32)
    of_value_head = lambda per_qk: [per_qk[h // GDN_REP] for h in range(len(v))]

    qs = of_value_head(_heads(lambda q: q * (GDN_HEAD_DIM ** -0.5), q))
    k = of_value_head(k)
    gc = _heads(lambda g: _dot(ltri, g), gb)
    gd = _heads(lambda g: _dot(ltri, g), gb64)
    decay = _heads(lambda gd: jnp.exp(jnp.where(causal, gd - _dot_nt(first_col, gd), -jnp.inf)), gd)
    kb = _heads(jnp.multiply, k, bb)
    lmat = _heads(lambda kb, k, dc: jnp.where(strict, _dot_nt(kb, k) * dc, 0.0), kb, k, decay)
    tmat, pw = _heads(lambda l: eye - l, lmat), lmat
    for _ in range(5):
        pw = _heads(lambda p: _dot(p, p), pw)
        tmat = _heads(lambda t, p: t + _dot(t, p), tmat, pw)
    eg = _heads(jnp.exp, gc)
    u = _heads(lambda t, v, bb: _dot(t, v * bb), tmat, v, bb)
    w = _heads(lambda t, kb, eg: _dot(t, kb * eg), tmat, kb, eg)
    a_qk = _heads(lambda q, k, dc: jnp.where(causal, _dot_nt(q, k) * dc, 0.0), qs, k, decay)
    k_dec = _heads(lambda k, gc: k * jnp.exp(_dot(last_col, gc) - gc), k, gc)
    v_new = _heads(lambda u, w, s: u - _dot(w, s), u, w, state)
    out = _heads(lambda q, eg, s, a, vn: _dot(q * eg, s) + _dot(a, vn), qs, eg, state, a_qk, v_new)
    new_state = _heads(lambda s, gc, kd, vn: s * jnp.exp(_dot(last_col_tall, gc)) + _dot_tn(kd, vn),
                       state, gc, k_dec, v_new)
    return out, new_state


def _gdn_chunk_specs(order):
    c, d, nq = GDN_CHUNK, GDN_HEAD_DIM, GDN_QK_PER_STEP
    qk = pl.BlockSpec((c, nq * d), lambda j, n: (order(n), j))
    vh = pl.BlockSpec((c, nq * GDN_REP * d), lambda j, n: (order(n), j))
    st = pl.BlockSpec((nq * GDN_REP, 1, d, d), lambda j, n: (j, order(n), 0, 0))
    return qk, vh, st


def _gdn_chunk_args(q_ref, k_ref, v_ref, gb_ref, bb_ref):
    d, nq = GDN_HEAD_DIM, GDN_QK_PER_STEP
    nv = nq * GDN_REP
    per = lambda ref, n: [ref[:, h * d:(h + 1) * d] for h in range(n)]
    lead = [gb_ref[:, h * d:h * d + GDN_CHUNK] for h in range(nv)]
    return per(q_ref, nq), per(k_ref, nq), per(v_ref, nv), per(gb_ref, nv), per(bb_ref, nv), lead


def gdn_chunk_fwd(q, k, v, gb, bb, name):
    s = q.shape[0]
    nc, d, nv = s // GDN_CHUNK, GDN_HEAD_DIM, GDN_QK_PER_STEP * GDN_REP

    def body(q_ref, k_ref, v_ref, gb_ref, bb_ref, o_ref, st_ref, carry):
        @pl.when(pl.program_id(1) == 0)
        def _():
            carry[...] = jnp.zeros_like(carry)

        states = [carry[h] for h in range(nv)]
        for h in range(nv):
            st_ref[h, 0] = states[h]
        outs, new_states = _gdn_chunk_fn(*_gdn_chunk_args(q_ref, k_ref, v_ref, gb_ref, bb_ref), states)
        o_ref[...] = jnp.concatenate(outs, axis=1)
        for h in range(nv):
            carry[h] = new_states[h]

    qk, vh, st = _gdn_chunk_specs(lambda n: n)
    return pl.pallas_call(
        body, grid=(GDN_QK_HEADS // GDN_QK_PER_STEP, nc), in_specs=[qk, qk, vh, vh, vh], out_specs=[vh, st],
        out_shape=[jax.ShapeDtypeStruct((s, MIX_WIDTH), F32), jax.ShapeDtypeStruct((GDN_V_HEADS, nc, d, d), F32)],
        scratch_shapes=[pltpu.VMEM((nv, d, d), F32)],
        compiler_params=_cp(("parallel", "arbitrary")), name=name)(q, k, v, gb, bb)


def gdn_chunk_bwd(q, k, v, gb, bb, states, do, name):
    s = q.shape[0]
    nc, d, nv = s // GDN_CHUNK, GDN_HEAD_DIM, GDN_QK_PER_STEP * GDN_REP
    rev = lambda n: nc - 1 - n

    def body(q_ref, k_ref, v_ref, gb_ref, bb_ref, st_ref, do_ref, dq_ref, dk_ref, dv_ref, dg_ref, db_ref, carry):
        @pl.when(pl.program_id(1) == 0)
        def _():
            carry[...] = jnp.zeros_like(carry)

        args = _gdn_chunk_args(q_ref, k_ref, v_ref, gb_ref, bb_ref) + ([st_ref[h, 0] for h in range(nv)],)
        _, vjp = jax.vjp(_gdn_chunk_fn, *args)
        cot = ([do_ref[:, h * d:(h + 1) * d] for h in range(nv)], [carry[h] for h in range(nv)])
        d_q, d_k, d_v, d_gb, d_bb, d_gb64, d_state = vjp(cot)
        dq_ref[...] = jnp.concatenate(d_q, axis=1)
        dk_ref[...] = jnp.concatenate(d_k, axis=1)
        dv_ref[...] = jnp.concatenate(d_v, axis=1)
        db_ref[...] = jnp.concatenate(d_bb, axis=1)
        dg_ref[...] = jnp.concatenate(d_gb, axis=1)
        for h in range(nv):
            dg_ref[:, h * d:h * d + GDN_CHUNK] += d_gb64[h]
            carry[h] = d_state[h]

    qk, vh, st = _gdn_chunk_specs(rev)
    qk_s, wide_s = jax.ShapeDtypeStruct((s, GDN_QK_WIDTH), F32), jax.ShapeDtypeStruct((s, MIX_WIDTH), F32)
    return pl.pallas_call(
        body, grid=(GDN_QK_HEADS // GDN_QK_PER_STEP, nc), in_specs=[qk, qk, vh, vh, vh, st, vh],
        out_specs=[qk, qk, vh, vh, vh], out_shape=[qk_s, qk_s, wide_s, wide_s, wide_s],
        scratch_shapes=[pltpu.VMEM((nv, d, d), F32)],
        compiler_params=_cp(("parallel", "arbitrary")), name=name)(q, k, v, gb, bb, states, do)


WEIGHTS = ['attn_norm', 'mem_norm', 'w_mem_kv', 'w_out', 'ffn_norm', 'w_ffn_up', 'ffn_conv', 'w_ffn_down', 'final_norm',
           'a_w_in', 'a_sinks', 'b_w_in', 'b_mu', 'b_w0', 'b_w_decay_up', 'b_a0', 'b_w_iclr_up', 'b_w_gate_up', 'b_k_k',
           'b_k_a', 'b_r_k', 'b_gn_g', 'b_gn_b', 'c_w_in', 'c_conv', 'c_a_log', 'c_dt_bias', 'c_norm_g']
INPUTS = ['x', 'mem'] + WEIGHTS + ['loss_target'] + ['m_' + n for n in WEIGHTS] + ['v_' + n for n in WEIGHTS]
REPLICATED = ['attn_norm', 'mem_norm', 'ffn_norm', 'final_norm', 'a_sinks', 'b_mu', 'b_w0', 'b_a0', 'b_k_k', 'b_k_a',
              'b_r_k', 'b_gn_g', 'b_gn_b', 'c_a_log', 'c_dt_bias', 'c_norm_g']
C_MIX = GDN_CONV_WIDTH + MIX_WIDTH
GATHER_ID = 1


def _cols_to_shards(full):
    rows, cols = full.shape
    return full.reshape(rows, N_DEV, cols // N_DEV).transpose(1, 0, 2)


def _shards_to_cols(g):
    return g.transpose(1, 0, 2).reshape(g.shape[1], N_DEV * g.shape[2])


def _pad_to(x, n, axis):
    pad = [(0, 0)] * x.ndim
    pad[axis] = (0, n - x.shape[axis])
    return jnp.pad(x, pad)


def _b_pad_cols(w):
    parts = [w[..., :4608], _pad_to(w[..., 4608:4704], LANES, -1), _pad_to(w[..., 4704:4800], LANES, -1), w[..., 4800:5056]]
    if w.shape[-1] > 5056:
        parts.append(w[..., 5056:])
    return jnp.concatenate(parts, axis=-1)


def _b_unpad_cols(w):
    parts = [w[..., :4608], w[..., RW_WD:RW_WD + RWKV_DECAY_RANK], w[..., RW_AD:RW_AD + RWKV_ICLR_RANK], w[..., RW_GD:RW_SHIFT]]
    if w.shape[-1] > RW_SHIFT:
        parts.append(w[..., RW_SHIFT:])
    return jnp.concatenate(parts, axis=-1)


def _c_pad_cols(w):
    return jnp.concatenate([w[..., :C_MIX], w[..., C_MIX + 24:], _pad_to(w[..., C_MIX:C_MIX + 12], LANES, -1),
                            _pad_to(w[..., C_MIX + 12:C_MIX + 24], LANES, -1)], axis=-1)


def _c_unpad_cols(w):
    return jnp.concatenate([w[..., :C_MIX], w[..., GD_BT:GD_BT + GDN_V_HEADS], w[..., GD_AT:GD_AT + GDN_V_HEADS],
                            w[..., GD_QMEM:GD_BT]], axis=-1)


def _pack(arrays):
    flat = jnp.concatenate([a.reshape(-1).astype(F32) for a in arrays])
    unit = SUBLANES * LANES
    return _pad_to(flat, -(-flat.size // unit) * unit, 0).reshape(-1, LANES)


def _unpack(packed, shapes):
    flat, out, at = packed.reshape(-1), [], 0
    for shp in shapes:
        n = int(np.prod(shp))
        out.append(flat[at:at + n].reshape(shp))
        at += n
    return out


def kernel(*args):
    a = dict(zip(INPUTS, args))
    x0, mem, target = a['x'][0], a['mem'][0], a['loss_target'][0]
    s = x0.shape[0]
    e64, e64t = _head_matrix(MIX_WIDTH, RWKV_HEAD_DIM)
    e6, e6t = _head_matrix(GDN_QK_WIDTH, GDN_HEAD_DIM)
    e12, e12t = _head_matrix(MIX_WIDTH, GDN_HEAD_DIM)
    trep = jnp.asarray((np.arange(LANES)[:, None] == np.arange(MIX_WIDTH)[None, :] % LANES).astype(np.float32))
    row = lambda v: v.reshape(1, -1)

    def in_proj_shard(l):
        kind, j = l % 3, l // 3
        return (a['a_w_in'], a['b_w_in'], a['c_w_in'])[kind][j]

    def small_shards(l):
        kind, j = l % 3, l // 3
        if kind == 1:
            return [a['b_w_decay_up'][j], a['b_w_iclr_up'][j], a['b_w_gate_up'][j]]
        if kind == 2:
            return [a['c_conv'][j]]
        return []

    gathered = []
    for l in range(DEPTH):
        big = [a['w_mem_kv'][l], a['w_out'][l], a['w_ffn_up'][l], a['w_ffn_down'][l], in_proj_shard(l)]
        shards = [w.astype(BF16) for w in big] + [a['ffn_conv'][l]] + small_shards(l)
        if gathered:
            shards, _ = lax.optimization_barrier((shards, gathered[-1]))
        gathered.append(all_gather_many_async(shards, f"gather_weights_{l}", GATHER_ID + l))

    def layer_weights(l, g):
        kind = l % 3
        w_in = _shards_to_cols(g[4])
        lw = dict(w_kv=g[0].reshape(D_MODEL, 2 * MEM_WIDTH), w_out=g[1].reshape(D_MODEL, D_MODEL),
                  w_up=_shards_to_cols(g[2]), w_down=g[3].reshape(D_FF, D_MODEL), conv=_shards_to_cols(g[5]))
        if kind == 0:
            lw['w_in'] = w_in
        elif kind == 1:
            lw['w_in'] = _b_pad_cols(w_in)
            lw['wdu'] = _pad_to(_shards_to_cols(g[6]), LANES, 0)
            lw['wiu'] = _pad_to(_shards_to_cols(g[7]), LANES, 0)
            lw['wgu'] = _shards_to_cols(g[8])
        else:
            lw['w_in'] = _c_pad_cols(w_in)
            lw['c_conv'] = _shards_to_cols(g[6])
        return lw

    def rwkv_params(j, lw):
        return (row(_b_pad_cols(a['b_mu'][j])), row(a['b_w0'][j]), lw['wdu'], row(a['b_a0'][j]), lw['wiu'], lw['wgu'],
                row(a['b_k_k'][j]), row(a['b_k_a'][j]), e64, e64t)

    def rwkv_post_params(j):
        return (row(a['b_gn_g'][j]), row(a['b_gn_b'][j]), row(a['b_r_k'][j]), e64, e64t)

    def gdn_params(j, lw):
        return (lw['c_conv'], _pad_lanes(a['c_a_log'][j]), _pad_lanes(a['c_dt_bias'][j]), e6, e6t, e12t)

    def gdn_post_params(j):
        return (jnp.tile(row(a['c_norm_g'][j]), (SUBLANES, 1)), e12, e12t, trep)

    x = x0
    saved, layers = [], []
    for l in range(DEPTH):
        kind, j = l % 3, l // 3
        g = gathered[l]
        if l > 0:
            x, g = lax.optimization_barrier((x, g))
        lw = layer_weights(l, g)
        layers.append(lw)
        sv = dict(x=x)
        h = rmsnorm_fwd(x, a['attn_norm'][l], BF16, f"attn_norm_{l}")
        memn = rmsnorm_fwd(mem, a['mem_norm'][l], BF16, f"mem_norm_{l}")
        mem_kv = mm(memn, lw['w_kv'], name=f"mem_kv_{l}")
        p = mm(h, lw['w_in'], name=f"in_proj_{l}")
        if kind == 0:
            y = swa_fwd(p, a['a_sinks'][j], f"swa_{l}")
            q_col = MIX_WIDTH + 2 * SWA_KV_HEADS * SWA_HEAD_DIM
        elif kind == 1:
            pre = rwkv_pre_fwd(p, rwkv_params(j, lw), f"rwkv_pre_{l}")
            yscan, ck = rwkv_scan_fwd(*pre[:6], f"rwkv_scan_{l}")
            post_in = (yscan, pre[0], pre[2], pre[3], pre[6])
            y = rwkv_post_fwd(post_in, rwkv_post_params(j), f"rwkv_post_{l}")
            sv.update(pre=pre, ck=ck, post_in=post_in)
            q_col = RW_SHIFT
        else:
            pre = gdn_pre_fwd(p, gdn_params(j, lw), f"gdn_pre_{l}")
            o, states = gdn_chunk_fwd(*pre, f"gdn_chunk_{l}")
            y = gdn_post_fwd(o, p, gdn_post_params(j), f"gdn_post_{l}")
            sv.update(pre=pre, o=o, states=states)
            q_col = GD_QMEM
        y_mem = mem_attn_fwd(p, q_col, mem_kv, f"mem_attn_{l}")
        cat = jnp.concatenate([y, y_mem], axis=1)
        x1 = mm(cat, lw['w_out'], res=x, name=f"out_proj_{l}")
        hf = rmsnorm_fwd(x1, a['ffn_norm'][l], BF16, f"ffn_norm_{l}")
        u0 = mm(hf, lw['w_up'], name=f"ffn_up_{l}")
        act = ffn_act_fwd(u0, lw['conv'], f"ffn_act_{l}")
        x = mm(act, lw['w_down'], res=x1, name=f"ffn_down_{l}")
        sv.update(h=h, memn=memn, mem_kv=mem_kv, p=p, q_col=q_col, cat=cat, x1=x1, hf=hf, u0=u0, act=act)
        saved.append(sv)

    loss_part, dx, d_final_norm = final_loss(x, a['final_norm'], target, "final_loss")

    rep_grads = {n: [None] * a[n].shape[0] for n in ('attn_norm', 'mem_norm', 'ffn_norm', 'a_sinks')}
    rep_grads['final_norm'] = d_final_norm
    results = {}
    exchanged, pending = {}, {}

    def apply_adam(name, idx, pieces, tag):
        w, m, v = a[name][idx], a['m_' + name][idx], a['v_' + name][idx]
        shp = w.shape
        two_d = (-1, shp[-1])
        out = adamw_sum(pieces.reshape((N_DEV,) + w.reshape(two_d).shape), w.reshape(two_d), m.reshape(two_d),
                        v.reshape(two_d), f"adamw_{name}_{tag}")
        results.setdefault(name, {})[idx] = [o.reshape(shp) for o in out]

    for l in reversed(range(DEPTH)):
        kind, j = l % 3, l // 3
        lw, sv = layers[l], saved[l]
        p, q_col = sv['p'], sv['q_col']
        d_act = mm(dx, lw['w_down'], tb=True, name=f"d_ffn_act_{l}")
        dw_down = mm(sv['act'], dx, ta=True, out_dtype=BF16, name=f"dw_ffn_down_{l}")
        dug, duv, dcg, dcv = ffn_act_bwd(sv['u0'], lw['conv'], d_act, f"ffn_act_bwd_{l}")
        du0 = jnp.concatenate([dug, duv], axis=1)
        d_conv = jnp.concatenate([dcg, dcv], axis=1)
        d_hf = mm(du0, lw['w_up'], tb=True, name=f"d_ffn_norm_out_{l}")
        dw_up = mm(sv['hf'], du0, ta=True, out_dtype=BF16, name=f"dw_ffn_up_{l}")
        dx1, rep_grads['ffn_norm'][l] = rmsnorm_bwd(sv['x1'], a['ffn_norm'][l], d_hf, dx, f"ffn_norm_bwd_{l}")
        dcat = mm(dx1, lw['w_out'], tb=True, name=f"d_cat_{l}")
        dw_out = mm(sv['cat'], dx1, ta=True, out_dtype=BF16, name=f"dw_out_{l}")
        dq_mem, d_mem_kv = mem_attn_bwd(p, q_col, sv['mem_kv'], dcat, f"mem_attn_bwd_{l}")
        small_grads = []
        if kind == 0:
            dq, dk, dv, rep_grads['a_sinks'][j] = swa_bwd(p, a['a_sinks'][j], dcat, f"swa_bwd_{l}")
            dp = jnp.concatenate([dq, dk, dv, dq_mem], axis=1)
        elif kind == 1:
            post = rwkv_post_bwd(sv['post_in'], rwkv_post_params(j), dcat, f"rwkv_post_bwd_{l}")
            scan = rwkv_scan_bwd(*sv['pre'][:6], sv['ck'], post[0], f"rwkv_scan_bwd_{l}")
            res = rwkv_pre_bwd(p, rwkv_params(j, lw), tuple(scan) + tuple(post[1:5]), f"rwkv_pre_bwd_{l}")
            dp_mix = shift_add(res[0], [res[1]], [1], BF16, f"rwkv_shift_bwd_{l}")
            dp = jnp.concatenate([dp_mix, dq_mem], axis=1)
            for n, val in zip(('b_mu', 'b_w0', 'b_a0', 'b_k_k', 'b_k_a'), (_b_unpad_cols(res[2]), res[3], res[5], res[8], res[9])):
                rep_grads[n] = val
            rep_grads.update(b_gn_g=post[5], b_gn_b=post[6], b_r_k=post[7])
            small_grads = [_cols_to_shards(res[4][:RWKV_DECAY_RANK]), _cols_to_shards(res[6][:RWKV_ICLR_RANK]),
                           _cols_to_shards(res[7])]
        else:
            d_o, dz, rep_grads['c_norm_g'] = gdn_post_bwd(sv['o'], p, gdn_post_params(j), dcat, f"gdn_post_bwd_{l}")
            chunk = gdn_chunk_bwd(*sv['pre'], sv['states'], d_o, f"gdn_chunk_bwd_{l}")
            res = gdn_pre_bwd(p, gdn_params(j, lw), chunk, f"gdn_pre_bwd_{l}")
            dqkv = shift_add(res[0], list(res[1:4]), [1, 2, 3], BF16, f"gdn_shift_bwd_{l}")
            dp = jnp.concatenate([dqkv, dz, dq_mem, res[4]], axis=1)
            rep_grads.update(c_a_log=res[6][:, :GDN_V_HEADS], c_dt_bias=res[7][:, :GDN_V_HEADS])
            small_grads = [_cols_to_shards(res[5])]
        d_h = mm(dp, lw['w_in'], tb=True, name=f"d_attn_norm_out_{l}")
        dw_in = mm(sv['h'], dp, ta=True, out_dtype=BF16, name=f"dw_in_{l}")
        dx, rep_grads['attn_norm'][l] = rmsnorm_bwd(sv['x'], a['attn_norm'][l], d_h, dx1, f"attn_norm_bwd_{l}")
        d_memn = mm(d_mem_kv, lw['w_kv'], tb=True, name=f"d_mem_norm_out_{l}")
        dw_kv = mm(sv['memn'], d_mem_kv, ta=True, out_dtype=BF16, name=f"dw_mem_kv_{l}")
        _, rep_grads['mem_norm'][l] = rmsnorm_bwd(mem, a['mem_norm'][l], d_memn, None, f"mem_norm_bwd_{l}")

        if kind == 1:
            dw_in = _b_unpad_cols(dw_in)
        elif kind == 2:
            dw_in = _c_unpad_cols(dw_in)
        pieces = [dw_kv.reshape(N_DEV, -1, 2 * MEM_WIDTH), dw_out.reshape(N_DEV, -1, D_MODEL), _cols_to_shards(dw_up),
                  dw_down.reshape(N_DEV, -1, D_MODEL), _cols_to_shards(dw_in), _cols_to_shards(d_conv)] + small_grads
        if l + 1 < DEPTH:
            exchanged[l + 1] = all_to_all_wait(*pending.pop(l + 1), dx, f"exchange_wait_{l + 1}")
        send, recv, thru, lands, token = all_to_all_start(pieces, f"exchange_start_{l}")
        if l == 0:
            upper = {i: exchanged[i] for i in range(1, DEPTH)}
            upper, token = lax.optimization_barrier((upper, token))
            exchanged.update(upper)
        dx = dx + token[0, 0]
        pending[l] = (send, recv, thru, lands)

    for l in reversed(range(DEPTH)):
        kind, j = l % 3, l // 3
        if l == 0:
            keys = [(n, i) for n in ('w_ffn_up', 'w_ffn_down') for i in range(1, DEPTH)]
            done_above = lax.optimization_barrier(tuple(results[n][i][0] for n, i in keys))
            for (n, i), val in zip(keys, done_above):
                results[n][i][0] = val
            exchanged[0] = all_to_all_wait(*pending.pop(0), done_above[0], "exchange_wait_0")
        got = exchanged[l]
        in_name = ('a_w_in', 'b_w_in', 'c_w_in')[kind]
        for name, idx, pc in (('w_mem_kv', l, got[0]), ('w_out', l, got[1]), ('w_ffn_up', l, got[2]),
                              ('w_ffn_down', l, got[3]), (in_name, j, got[4]), ('ffn_conv', l, got[5])):
            apply_adam(name, idx, pc, l)
        if kind == 1:
            for name, pc in zip(('b_w_decay_up', 'b_w_iclr_up', 'b_w_gate_up'), got[6:]):
                apply_adam(name, j, pc, l)
        elif kind == 2:
            apply_adam('c_conv', j, got[6], l)

    rep_vals = []
    for n in REPLICATED:
        gval = rep_grads[n]
        gval = jnp.stack(gval) if isinstance(gval, list) else gval
        rep_vals.append(gval.reshape(a[n].shape))
    shapes = [a[n].shape for n in REPLICATED] + [(1,)]
    part = _pack(rep_vals + [loss_part.reshape(1)])
    gathered = all_gather_many([part], "gather_small_grads")[0]
    zero = jnp.zeros((1,), F32)
    packed = lambda pre: _pack([a[pre + n] for n in REPLICATED] + [zero])
    rep_out = adamw_sum(gathered, packed(''), packed('m_'), packed('v_'), "adamw_replicated")
    rep_out = [_unpack(o, shapes) for o in rep_out]
    loss = rep_out[0][-1][0]
    for i, n in enumerate(REPLICATED):
        results[n] = [o[i] for o in rep_out]

    def leaf(name, which):
        r = results[name]
        if isinstance(r, dict):
            return jnp.stack([r[i][which] for i in range(len(r))])
        return r[which]

    outs = [loss, dx[None]]
    for which in range(4):
        outs += [leaf(n, which) for n in WEIGHTS]
    return tuple(outs)
```

```python
import functools

import numpy as np
import jax
import jax.numpy as jnp
from jax import lax
from jax.experimental import pallas as pl
from jax.experimental.pallas import tpu as pltpu
from jax.experimental.pallas import tpu_sc as plsc

F32, BF16 = jnp.float32, jnp.bfloat16
HI = lax.Precision.HIGHEST
V7X_VMEM_BYTES = 64 * 1024 * 1024
VMEM_LIMIT = V7X_VMEM_BYTES - 8 * 1024 * 1024
MM_TILE_BUDGET = 40 * 1024 * 1024
SUBLANES, LANES = 8, 128
N_DEV = 8

D_MODEL = 2048
DEPTH = 4
MIX_WIDTH = 1536
MEM_HEADS, MEM_HEAD_DIM, MEM_WIDTH = 4, 128, 512
NORM_EPS = 1e-6
SWA_HEAD_DIM, SWA_Q_HEADS, SWA_KV_HEADS, SWA_GROUP, SWA_BLOCK = 64, 24, 4, 6, 128
RWKV_HEADS, RWKV_HEAD_DIM, RWKV_GN_EPS = 24, 64, 64e-5
RWKV_DECAY_RANK, RWKV_ICLR_RANK, RWKV_GATE_RANK = 96, 96, 256
GDN_HEAD_DIM, GDN_V_HEADS, GDN_QK_HEADS, GDN_CONV, GDN_CHUNK = 128, 12, 6, 4, 64
GDN_QK_WIDTH = GDN_QK_HEADS * GDN_HEAD_DIM
GDN_CONV_WIDTH = 2 * GDN_QK_WIDTH + MIX_WIDTH
D_FF, FFN_CONV = 5632, 3
ADAM_LR, ADAM_B1, ADAM_B2, ADAM_EPS, ADAM_WD, ADAM_STEP = 0.001, 0.9, 0.999, 1e-08, 0.01, 10
MESH = pl.DeviceIdType.MESH


def _cp(sem=None):
    return pltpu.CompilerParams(dimension_semantics=sem, vmem_limit_bytes=VMEM_LIMIT)


def _tile(n, cands):
    for c in cands:
        if n % c == 0:
            return c
    return n


_DIMS = {'nn': (((1,), (0,)), ((), ())), 'nt': (((1,), (1,)), ((), ())), 'tn': (((0,), (0,)), ((), ()))}


def _split(x):
    hi = lax.bitcast_convert_type(lax.bitcast_convert_type(x, jnp.uint32) & jnp.uint32(0xFFFF0000), F32)
    return hi.astype(BF16), (x - hi).astype(BF16)


def _dot3_raw(a, b, form):
    one = lambda p, q: lax.dot_general(p, q, _DIMS[form], preferred_element_type=F32)
    (a_hi, a_lo), (b_hi, b_lo) = _split(a), _split(b)
    return one(a_hi, b_hi) + (one(a_hi, b_lo) + one(a_lo, b_hi))


@functools.partial(jax.custom_vjp, nondiff_argnums=(2,))
def _dot3(a, b, form):
    return _dot3_raw(a, b, form)


def _dot3_fwd(a, b, form):
    return _dot3_raw(a, b, form), (a, b)


def _dot3_bwd(form, saved, dc):
    a, b = saved
    if form == 'nn':
        return _dot3(dc, b, 'nt'), _dot3(a, dc, 'tn')
    if form == 'nt':
        return _dot3(dc, b, 'nn'), _dot3(dc, a, 'tn')
    return _dot3(b, dc, 'nt'), _dot3(a, dc, 'nn')


_dot3.defvjp(_dot3_fwd, _dot3_bwd)


def _dot(a, b):
    return _dot3(a, b, 'nn')


def _dot_nt(a, b):
    return _dot3(a, b, 'nt')


def _dot_tn(a, b):
    return _dot3(a, b, 'tn')


def _sigmoid(x):
    return 1.0 / (1.0 + jnp.exp(-x))


def _softplus(x):
    return jnp.maximum(x, 0.0) + jnp.log(1.0 + jnp.exp(-jnp.abs(x)))


def _silu(x):
    return x * _sigmoid(x)


def mm(a, b, *, ta=False, tb=False, res=None, out_dtype=F32, col_shards=False, name):
    (k_a, m) = a.shape if ta else a.shape[::-1]
    (k_b, n) = b.shape[::-1] if tb else b.shape
    assert k_a == k_b, (a.shape, b.shape, ta, tb)
    kdim = k_a
    tm = _tile(m, (1024, 512, 256))
    tn = _tile(n, (1024, 768, 512, 384, 256, 128))
    if col_shards:
        tn = n // N_DEV
        assert res is None and not tb and tn % LANES == 0

    def vmem_bytes(tk):
        tiles = tk * (tm * a.dtype.itemsize + tn * b.dtype.itemsize) + tm * tn * jnp.dtype(out_dtype).itemsize
        tiles += 0 if res is None else tm * tn * res.dtype.itemsize
        return 2 * tiles + (0 if tk == kdim else tm * tn * 4)

    tks = [t for t in (kdim, kdim // 2, kdim // 4, 1024, 512, 256, 128) if kdim % t == 0 and t % LANES == 0]
    tk = next(t for t in tks if vmem_bytes(t) <= MM_TILE_BUDGET)
    nk = kdim // tk
    dims = (((0 if ta else 1,), (1 if tb else 0,)), ((), ()))

    def body(*refs):
        a_ref, b_ref = refs[:2]
        r_ref = None if res is None else refs[2]
        o_ref = refs[2 if res is None else 3]
        part = lax.dot_general(a_ref[...].astype(BF16), b_ref[...].astype(BF16), dims, preferred_element_type=F32)

        def finish(total):
            o_ref[...] = (total if res is None else total + r_ref[...]).astype(o_ref.dtype)

        if nk == 1:
            finish(part)
            return
        acc, kk = refs[-1], pl.program_id(2)

        @pl.when(kk == 0)
        def _():
            acc[...] = part

        @pl.when((kk > 0) & (kk < nk - 1))
        def _():
            acc[...] += part

        @pl.when(kk == nk - 1)
        def _():
            finish(acc[...] + part)

    a_spec = pl.BlockSpec((tk, tm), lambda i, j, k: (k, i)) if ta else pl.BlockSpec((tm, tk), lambda i, j, k: (i, k))
    b_spec = pl.BlockSpec((tn, tk), lambda i, j, k: (j, k)) if tb else pl.BlockSpec((tk, tn), lambda i, j, k: (k, j))
    o_spec = pl.BlockSpec((tm, tn), lambda i, j, k: (i, j))
    o_shape = (m, n)
    if col_shards:
        o_spec = pl.BlockSpec((None, tm, tn), lambda i, j, k: (j, i, 0))
        o_shape = (N_DEV, m, tn)
    in_specs, args = [a_spec, b_spec], [a, b]
    if res is not None:
        in_specs.append(o_spec)
        args.append(res)
    return pl.pallas_call(
        body, grid=(m // tm, n // tn, nk), in_specs=in_specs, out_specs=o_spec,
        out_shape=jax.ShapeDtypeStruct(o_shape, out_dtype),
        scratch_shapes=[] if nk == 1 else [pltpu.VMEM((tm, tn), F32)],
        compiler_params=_cp(("parallel", "parallel", "arbitrary")), name=name)(*args)


def _coords():
    return lax.axis_index("x"), lax.axis_index("y"), lax.axis_index("c")


def _block_index(p):
    return 4 * p[0] + 2 * p[1] + p[2]


def _all_gather_body(x_refs, o_refs, send, recv, loc):
    n = len(x_refs)
    x, y, c = _coords()
    me, sib = (x, y, c), (x, y, 1 - c)
    chips = [(1 - x, y), (x, 1 - y), (1 - x, 1 - y)]

    def cp(i, k, block, to, src=None):
        dst = o_refs[i].at[_block_index(block)]
        return pltpu.make_async_remote_copy(
            src_ref=dst if src is None else src, dst_ref=dst, send_sem=send.at[i, k], recv_sem=recv.at[i, k],
            device_id=to, device_id_type=MESH)

    mine = [pltpu.make_async_copy(x_refs[i], o_refs[i].at[_block_index(me)], loc.at[i]) for i in range(n)]
    for m_ in mine:
        m_.start()
    first = []
    for i in range(n):
        first.append(cp(i, 0, me, sib, src=x_refs[i]))
        for j, chip in enumerate(chips):
            first.append(cp(i, 1 + j, me, (*chip, c), src=x_refs[i]))
    for f in first:
        f.start()
    passed = []
    for j, chip in enumerate(chips):
        for i in range(n):
            cp(i, 1 + j, (*chip, c), me).wait_recv()
            p = cp(i, 4 + j, (*chip, c), sib)
            p.start()
            passed.append(p)
    for i in range(n):
        cp(i, 0, sib, me).wait_recv()
        for j, chip in enumerate(chips):
            cp(i, 4 + j, (*chip, 1 - c), me).wait_recv()
    for f in first + passed:
        f.wait_send()
    for m_ in mine:
        m_.wait()


def _all_gather_peers():
    x, y, c = _coords()
    return [(x, y, 1 - c), (1 - x, y, c), (x, 1 - y, c), (1 - x, 1 - y, c)]


def _all_to_all_peers():
    x, y, c = _coords()
    return [(1 - x if r & 4 else x, 1 - y if r & 2 else y, 1 - c if r & 1 else c) for r in range(1, N_DEV)]


def _comm_scratch(n):
    return [pltpu.SemaphoreType.DMA((n, 7)), pltpu.SemaphoreType.DMA((n, 7)), pltpu.SemaphoreType.DMA((n,))]


def all_gather_many(xs, name):
    n = len(xs)

    def body(*refs):
        _all_gather_body(refs[:n], refs[n:2 * n], *refs[2 * n:])

    any_spec = pl.BlockSpec(memory_space=pl.ANY)
    return pl.pallas_call(
        body, in_specs=[any_spec] * n, out_specs=[any_spec] * n,
        out_shape=[jax.ShapeDtypeStruct((N_DEV,) + x.shape, x.dtype) for x in xs],
        scratch_shapes=_comm_scratch(n), name=name)(*xs)


def _on_sequencer(exchange, peers, xs, out_shapes, name, collective_id):
    x_refs = [jax.new_ref(x, memory_space=pltpu.MemorySpace.HBM) for x in xs]
    o_refs = [jax.empty_ref(s, memory_space=pltpu.MemorySpace.HBM) for s in out_shapes]

    @pl.kernel(mesh=plsc.ScalarSubcoreMesh(axis_name="sequencer", num_cores=1), name=name,
               scratch_types=tuple(_comm_scratch(len(xs))),
               compiler_params=pltpu.CompilerParams(collective_id=collective_id))
    def launch(send, recv, loc):
        barrier = pltpu.get_barrier_semaphore()
        ids = peers()
        for peer in ids:
            pl.semaphore_signal(barrier, inc=1, device_id=peer, device_id_type=MESH)
        pl.semaphore_wait(barrier, len(ids))
        exchange(x_refs, o_refs, send, recv, loc)

    launch()
    return [o[...] for o in o_refs]


def _all_to_all_copies(x_refs, o_refs, send, recv, arrivals):
    x, y, c = _coords()
    me = _block_index((x, y, c))
    copies = []
    for r, peer in enumerate(_all_to_all_peers()):
        pidx = _block_index(peer)
        for i in range(len(x_refs)):
            copies.append(pltpu.make_async_remote_copy(
                src_ref=x_refs[i].at[pidx], dst_ref=o_refs[i].at[pidx if arrivals else me],
                send_sem=send.at[7 * i + r], recv_sem=recv.at[7 * i + r], device_id=peer, device_id_type=MESH))
    return copies


def all_to_all_start(xs, name):
    n = len(xs)
    x, y, c = _coords()
    me = _block_index((x, y, c))
    lands = [lax.dynamic_update_slice_in_dim(lax.empty(v.shape, v.dtype), lax.dynamic_slice_in_dim(v, me, 1, 0), me, 0)
             for v in xs]

    def body(*refs):
        x_refs, o_refs = refs[:n], refs[n:2 * n]
        send, recv = refs[2 * n], refs[2 * n + 1]
        token = refs[-1]
        for s in _all_to_all_copies(x_refs, o_refs, send, recv, arrivals=False):
            s.start()
        token[...] = jnp.zeros_like(token)

    hbm = pl.BlockSpec(memory_space=pltpu.HBM)
    sem = pl.BlockSpec(memory_space=pltpu.SEMAPHORE)
    out = pl.pallas_call(
        body, name=name,
        out_shape=(pltpu.SemaphoreType.DMA((7 * n,)), pltpu.SemaphoreType.DMA((7 * n,)),
                   *[pltpu.HBM(v.shape, v.dtype) for v in xs], *[pltpu.HBM(v.shape, v.dtype) for v in xs],
                   jax.ShapeDtypeStruct((SUBLANES, LANES), F32)),
        in_specs=[hbm] * (2 * n), out_specs=(sem, sem, *([hbm] * (2 * n)), pl.BlockSpec(memory_space=pltpu.VMEM)),
        input_output_aliases={i: 2 + i for i in range(2 * n)},
        compiler_params=pltpu.CompilerParams(has_side_effects=pltpu.SideEffectType.DATAFLOW_SIDE_EFFECTING),
    )(*[pltpu.with_memory_space_constraint(v, pltpu.HBM) for v in xs],
      *[pltpu.with_memory_space_constraint(v, pltpu.HBM) for v in lands])
    return out[0], out[1], list(out[2:2 + n]), list(out[2 + n:2 + 2 * n]), out[-1]


def all_to_all_wait(send, recv, xs, lands, after, name):
    n = len(xs)

    def body(*refs):
        x_refs, o_refs = refs[:n], refs[n:2 * n]
        send_ref, recv_ref = refs[2 * n], refs[2 * n + 1]
        for s in _all_to_all_copies(x_refs, o_refs, send_ref, recv_ref, arrivals=False):
            s.wait_send()
        for w in _all_to_all_copies(x_refs, o_refs, send_ref, recv_ref, arrivals=True):
            w.wait_recv()

    hbm = pl.BlockSpec(memory_space=pltpu.HBM)
    sem = pl.BlockSpec(memory_space=pltpu.SEMAPHORE)
    out = pl.pallas_call(
        body, name=name,
        out_shape=tuple(pltpu.HBM(v.shape, v.dtype) for v in list(xs) + list(lands)),
        in_specs=[hbm] * (2 * n) + [sem, sem, pl.BlockSpec(memory_space=pl.ANY)], out_specs=tuple([hbm] * (2 * n)),
        input_output_aliases={i: i for i in range(2 * n)},
        compiler_params=pltpu.CompilerParams(has_side_effects=pltpu.SideEffectType.DATAFLOW_SIDE_EFFECTING),
    )(*xs, *lands, send, recv, after)
    return list(out[n:])


def all_gather_many_async(xs, name, collective_id):
    shapes = [jax.ShapeDtypeStruct((N_DEV,) + x.shape, x.dtype) for x in xs]
    return _on_sequencer(_all_gather_body, _all_gather_peers, xs, shapes, name, collective_id)


def adamw_sum(pieces, w, m, v, name):
    rows, cols = w.shape
    tr = _tile(rows, (128, 64, 32, 16, 8))
    c1 = 1.0 - ADAM_B1 ** ADAM_STEP
    c2 = 1.0 - ADAM_B2 ** ADAM_STEP

    def body(p_ref, w_ref, m_ref, v_ref, g_out, d_out, m_out, v_out):
        g = p_ref[0].astype(F32)
        for s in range(1, N_DEV):
            g = g + p_ref[s].astype(F32)
        m_new = ADAM_B1 * m_ref[...] + (1.0 - ADAM_B1) * g
        v_new = ADAM_B2 * v_ref[...] + (1.0 - ADAM_B2) * (g * g)
        m_hat = m_new / c1
        v_hat = v_new / c2
        g_out[...] = g
        d_out[...] = -ADAM_LR * (m_hat / (jnp.sqrt(v_hat) + ADAM_EPS) + ADAM_WD * w_ref[...])
        m_out[...] = m_new
        v_out[...] = v_new

    spec = pl.BlockSpec((tr, cols), lambda i: (i, 0))
    out = jax.ShapeDtypeStruct((rows, cols), F32)
    return pl.pallas_call(
        body, grid=(rows // tr,), in_specs=[pl.BlockSpec((N_DEV, tr, cols), lambda i: (0, i, 0)), spec, spec, spec],
        out_specs=[spec] * 4, out_shape=[out] * 4, compiler_params=_cp(("parallel",)), name=name)(pieces, w, m, v)


def rmsnorm_fwd(x, g, out_dtype, name):
    s, d = x.shape
    tr = _tile(s, (256, 128, 64, 32, 16))

    def body(x_ref, g_ref, o_ref):
        xv = x_ref[...]
        rstd = lax.rsqrt(jnp.mean(xv * xv, axis=-1, keepdims=True) + NORM_EPS)
        o_ref[...] = (xv * rstd * g_ref[...]).astype(o_ref.dtype)

    return pl.pallas_call(
        body, grid=(s // tr,), in_specs=[pl.BlockSpec((tr, d), lambda i: (i, 0)), pl.BlockSpec((1, d), lambda i: (0, 0))],
        out_specs=pl.BlockSpec((tr, d), lambda i: (i, 0)), out_shape=jax.ShapeDtypeStruct((s, d), out_dtype),
        compiler_params=_cp(("parallel",)), name=name)(x, g.reshape(1, d))


def rmsnorm_bwd(x, g, dh, dres, name):
    s, d = x.shape
    tr = _tile(s, (256, 128, 64, 32, 16))

    def body(*refs):
        if dres is None:
            x_ref, g_ref, dh_ref, dx_ref, dg_ref = refs
        else:
            x_ref, g_ref, dh_ref, dr_ref, dx_ref, dg_ref = refs
        xv = x_ref[...]
        rstd = lax.rsqrt(jnp.mean(xv * xv, axis=-1, keepdims=True) + NORM_EPS)
        xhat = xv * rstd
        dhv = dh_ref[...].astype(F32)
        dhg = dhv * g_ref[...]
        dx = rstd * (dhg - xhat * jnp.mean(dhg * xhat, axis=-1, keepdims=True))
        if dres is not None:
            dx = dx + dr_ref[...]
        dx_ref[...] = dx

        @pl.when(pl.program_id(0) == 0)
        def _():
            dg_ref[...] = jnp.zeros_like(dg_ref)

        dg_ref[...] += jnp.sum(dhv * xhat, axis=0, keepdims=True)

    row = pl.BlockSpec((tr, d), lambda i: (i, 0))
    vec = pl.BlockSpec((1, d), lambda i: (0, 0))
    ins = [x, g.reshape(1, d), dh] + ([] if dres is None else [dres])
    dx, dg = pl.pallas_call(
        body, grid=(s // tr,), in_specs=[row, vec, row] + ([] if dres is None else [row]), out_specs=[row, vec],
        out_shape=[jax.ShapeDtypeStruct((s, d), F32), jax.ShapeDtypeStruct((1, d), F32)],
        compiler_params=_cp(("arbitrary",)), name=name)(*ins)
    return dx, dg.reshape(d)


def final_loss(x, g, target, name):
    s, d = x.shape
    tr = _tile(s, (256, 128, 64, 32, 16))

    def body(x_ref, g_ref, t_ref, l_ref, dx_ref, dg_ref):
        xv = x_ref[...]
        rstd = lax.rsqrt(jnp.mean(xv * xv, axis=-1, keepdims=True) + NORM_EPS)
        xhat = xv * rstd
        err = xhat * g_ref[...] - t_ref[...]
        dy = err * (1.0 / d)
        dhg = dy * g_ref[...]
        dx_ref[...] = rstd * (dhg - xhat * jnp.mean(dhg * xhat, axis=-1, keepdims=True))

        @pl.when(pl.program_id(0) == 0)
        def _():
            dg_ref[...] = jnp.zeros_like(dg_ref)
            l_ref[...] = jnp.zeros_like(l_ref)

        dg_ref[...] += jnp.sum(dy * xhat, axis=0, keepdims=True)
        part = 0.5 * jnp.sum(jnp.mean(err * err, axis=-1, keepdims=True), axis=0, keepdims=True)
        l_ref[...] += jnp.broadcast_to(part, l_ref.shape)

    row = pl.BlockSpec((tr, d), lambda i: (i, 0))
    vec = pl.BlockSpec((1, d), lambda i: (0, 0))
    lspec = pl.BlockSpec((1, LANES), lambda i: (0, 0))
    loss, dx, dg = pl.pallas_call(
        body, grid=(s // tr,), in_specs=[row, vec, row], out_specs=[lspec, row, vec],
        out_shape=[jax.ShapeDtypeStruct((1, LANES), F32), jax.ShapeDtypeStruct((s, d), F32),
                   jax.ShapeDtypeStruct((1, d), F32)],
        compiler_params=_cp(("arbitrary",)), name=name)(x, g.reshape(1, d), target)
    return loss[0, 0], dx, dg.reshape(d)


def _shift_down(tile, prev8, j):
    if j == 0:
        return tile
    rt = pltpu.roll(tile, j, 0)
    rp = pltpu.roll(prev8, j, 0)
    rows = lax.broadcasted_iota(jnp.int32, prev8.shape, 0)
    top = jnp.where(rows < j, rp, rt[:SUBLANES])
    return jnp.concatenate([top, rt[SUBLANES:]], axis=0)


def _shift_up(tile, next8, j):
    if j == 0:
        return tile
    t = tile.shape[0]
    rt = pltpu.roll(tile, t - j, 0)
    rn = pltpu.roll(next8, SUBLANES - j, 0)
    rows = lax.broadcasted_iota(jnp.int32, next8.shape, 0)
    bot = jnp.where(rows >= SUBLANES - j, rn, rt[t - SUBLANES:])
    return jnp.concatenate([rt[:t - SUBLANES], bot], axis=0)


def _halo_specs(t_rows, s_rows, cols, col_of):
    per, last = t_rows // SUBLANES, s_rows // SUBLANES - 1
    prev = pl.BlockSpec((SUBLANES, cols), lambda j, i: (jnp.maximum(i * per - 1, 0), col_of(j)))
    nxt = pl.BlockSpec((SUBLANES, cols), lambda j, i: (jnp.minimum((i + 1) * per, last), col_of(j)))
    return prev, nxt


def ffn_act_fwd(u0, conv, name):
    s, two_f = u0.shape
    f = two_f // 2
    t, c = _tile(s, (256, 128, 64)), 512
    nc = f // c

    def body(g_ref, v_ref, gp_ref, vp_ref, wg_ref, wv_ref, a_ref):
        first = (pl.program_id(1) > 0).astype(F32)

        def conv_of(x_ref, p_ref, w_ref):
            x, p = x_ref[...], p_ref[...] * first
            return (w_ref[0:1, :] * _shift_down(x, p, 2) + w_ref[1:2, :] * _shift_down(x, p, 1) + w_ref[2:3, :] * x)

        ug = conv_of(g_ref, gp_ref, wg_ref)
        uv = conv_of(v_ref, vp_ref, wv_ref)
        a_ref[...] = (_silu(ug) * uv).astype(a_ref.dtype)

    gate = pl.BlockSpec((t, c), lambda j, i: (i, j))
    val = pl.BlockSpec((t, c), lambda j, i: (i, j + nc))
    gp, _ = _halo_specs(t, s, c, lambda j: j)
    vp, _ = _halo_specs(t, s, c, lambda j: j + nc)
    wg = pl.BlockSpec((FFN_CONV, c), lambda j, i: (0, j))
    wv = pl.BlockSpec((FFN_CONV, c), lambda j, i: (0, j + nc))
    return pl.pallas_call(
        body, grid=(nc, s // t), in_specs=[gate, val, gp, vp, wg, wv], out_specs=pl.BlockSpec((t, c), lambda j, i: (i, j)),
        out_shape=jax.ShapeDtypeStruct((s, f), BF16), compiler_params=_cp(("parallel", "parallel")),
        name=name)(u0, u0, u0, u0, conv, conv)


def ffn_act_bwd(u0, conv, da, name):
    s, two_f = u0.shape
    f = two_f // 2
    t, c = _tile(s, (256, 128, 64)), 512
    nc, nt = f // c, s // t

    def body(g_ref, v_ref, gp_ref, vp_ref, gn_ref, vn_ref, wg_ref, wv_ref, da_ref, dan_ref,
             dg_ref, dv_ref, dwg_ref, dwv_ref):
        i = pl.program_id(1)
        first, last = (i > 0).astype(F32), (i < nt - 1).astype(F32)
        zeros8 = jnp.zeros((SUBLANES, c), F32)

        def ext(x_ref, p_ref, n_ref):
            return jnp.concatenate([p_ref[...] * first, x_ref[...], n_ref[...] * last], axis=0)

        def taps(e):
            return pltpu.roll(e, 2, 0), pltpu.roll(e, 1, 0), e

        def conv_of(sh, w_ref):
            return w_ref[0:1, :] * sh[0] + w_ref[1:2, :] * sh[1] + w_ref[2:3, :] * sh[2]

        def conv_t(du, w_ref):
            n = du.shape[0]
            return w_ref[2:3, :] * du + w_ref[1:2, :] * pltpu.roll(du, n - 1, 0) + w_ref[0:1, :] * pltpu.roll(du, n - 2, 0)

        sg, sv = taps(ext(g_ref, gp_ref, gn_ref)), taps(ext(v_ref, vp_ref, vn_ref))
        ug, uv = conv_of(sg, wg_ref), conv_of(sv, wv_ref)
        dae = jnp.concatenate([zeros8, da_ref[...], dan_ref[...] * last], axis=0)
        sig = _sigmoid(ug)
        dug = dae * uv * (sig * (1.0 + ug * (1.0 - sig)))
        duv = dae * (ug * sig)
        dg_ref[...] = conv_t(dug, wg_ref)[SUBLANES:t + SUBLANES].astype(dg_ref.dtype)
        dv_ref[...] = conv_t(duv, wv_ref)[SUBLANES:t + SUBLANES].astype(dv_ref.dtype)

        @pl.when(i == 0)
        def _():
            dwg_ref[...] = jnp.zeros_like(dwg_ref)
            dwv_ref[...] = jnp.zeros_like(dwv_ref)

        def dconv(du, sh):
            d = du[SUBLANES:t + SUBLANES]
            return jnp.concatenate([jnp.sum(d * x[SUBLANES:t + SUBLANES], axis=0, keepdims=True) for x in sh], axis=0)

        dwg_ref[...] += dconv(dug, sg)
        dwv_ref[...] += dconv(duv, sv)

    gate = pl.BlockSpec((t, c), lambda j, i: (i, j))
    val = pl.BlockSpec((t, c), lambda j, i: (i, j + nc))
    gp, gn = _halo_specs(t, s, c, lambda j: j)
    vp, vn = _halo_specs(t, s, c, lambda j: j + nc)
    wg = pl.BlockSpec((FFN_CONV, c), lambda j, i: (0, j))
    wv = pl.BlockSpec((FFN_CONV, c), lambda j, i: (0, j + nc))
    wout = pl.BlockSpec((FFN_CONV, c), lambda j, i: (0, j))
    half = jax.ShapeDtypeStruct((s, f), BF16)
    dwh = jax.ShapeDtypeStruct((FFN_CONV, f), F32)
    return pl.pallas_call(
        body, grid=(nc, nt), in_specs=[gate, val, gp, vp, gn, vn, wg, wv, gate, gn],
        out_specs=[gate, gate, wout, wout], out_shape=[half, half, dwh, dwh],
        compiler_params=_cp(("parallel", "arbitrary")), name=name)(u0, u0, u0, u0, u0, u0, conv, conv, da, da)


def _softmax_rows(s, extra=None):
    m = jnp.max(s, axis=-1, keepdims=True)
    if extra is not None:
        m = jnp.maximum(m, extra)
    m = lax.stop_gradient(m)
    e = jnp.exp(s - m)
    den = jnp.sum(e, axis=-1, keepdims=True)
    if extra is not None:
        den = den + jnp.exp(extra - m)
    return e / den


def _mem_attn_fn(qs, ks, vs):
    outs = []
    for q, k, v in zip(qs, ks, vs):
        p = _softmax_rows(_dot_nt(q, k) * (MEM_HEAD_DIM ** -0.5))
        outs.append(_dot(p, v))
    return outs


def _mem_heads(q_ref, kv_ref):
    d = MEM_HEAD_DIM
    qs = [q_ref[:, h * d:(h + 1) * d] for h in range(MEM_HEADS)]
    ks = [kv_ref[:, h * d:(h + 1) * d] for h in range(MEM_HEADS)]
    vs = [kv_ref[:, MEM_WIDTH + h * d:MEM_WIDTH + (h + 1) * d] for h in range(MEM_HEADS)]
    return qs, ks, vs


def mem_attn_fwd(p, q_col, kv, name):
    s = p.shape[0]
    t = _tile(s, (256, 128))
    m = kv.shape[0]

    def body(q_ref, kv_ref, o_ref):
        outs = _mem_attn_fn(*_mem_heads(q_ref, kv_ref))
        o_ref[...] = jnp.concatenate(outs, axis=1).astype(o_ref.dtype)

    return pl.pallas_call(
        body, grid=(s // t,),
        in_specs=[pl.BlockSpec((t, MEM_WIDTH), lambda i: (i, q_col // MEM_WIDTH)),
                  pl.BlockSpec((m, 2 * MEM_WIDTH), lambda i: (0, 0))],
        out_specs=pl.BlockSpec((t, MEM_WIDTH), lambda i: (i, 0)), out_shape=jax.ShapeDtypeStruct((s, MEM_WIDTH), BF16),
        compiler_params=_cp(("parallel",)), name=name)(p, kv)


def mem_attn_bwd(p, q_col, kv, dcat, name):
    s = p.shape[0]
    t = _tile(s, (256, 128))
    m = kv.shape[0]
    d = MEM_HEAD_DIM

    def body(q_ref, kv_ref, dy_ref, dq_ref, dkv_ref):
        qs, ks, vs = _mem_heads(q_ref, kv_ref)
        _, vjp = jax.vjp(_mem_attn_fn, qs, ks, vs)
        dqs, dks, dvs = vjp([dy_ref[:, h * d:(h + 1) * d] for h in range(MEM_HEADS)])
        dq_ref[...] = jnp.concatenate(dqs, axis=1).astype(dq_ref.dtype)

        @pl.when(pl.program_id(0) == 0)
        def _():
            dkv_ref[...] = jnp.zeros_like(dkv_ref)

        dkv_ref[...] += jnp.concatenate(dks + dvs, axis=1)

    return pl.pallas_call(
        body, grid=(s // t,),
        in_specs=[pl.BlockSpec((t, MEM_WIDTH), lambda i: (i, q_col // MEM_WIDTH)),
                  pl.BlockSpec((m, 2 * MEM_WIDTH), lambda i: (0, 0)),
                  pl.BlockSpec((t, MEM_WIDTH), lambda i: (i, MIX_WIDTH // MEM_WIDTH))],
        out_specs=[pl.BlockSpec((t, MEM_WIDTH), lambda i: (i, 0)), pl.BlockSpec((m, 2 * MEM_WIDTH), lambda i: (0, 0))],
        out_shape=[jax.ShapeDtypeStruct((s, MEM_WIDTH), BF16), jax.ShapeDtypeStruct((m, 2 * MEM_WIDTH), F32)],
        compiler_params=_cp(("arbitrary",)), name=name)(p, kv, dcat)


def _swa_fn(qs, kcs, kps, vcs, vps, sinks, not_first):
    t = SWA_BLOCK
    qi = lax.broadcasted_iota(jnp.int32, (t, 2 * t), 0)
    kj = lax.broadcasted_iota(jnp.int32, (t, 2 * t), 1)
    dist = t + qi - kj
    valid = (dist >= 0) & (dist < t) & ((kj >= t) | not_first)
    distf = dist.astype(F32)
    outs = []
    for kh in range(SWA_KV_HEADS):
        kb = jnp.concatenate([kps[kh], kcs[kh]], axis=0)
        vb = jnp.concatenate([vps[kh], vcs[kh]], axis=0)
        for g in range(SWA_GROUP):
            h = kh * SWA_GROUP + g
            slope = 2.0 ** (-8.0 * (h + 1) / SWA_Q_HEADS)
            sc = _dot_nt(qs[h], kb) * (SWA_HEAD_DIM ** -0.5) - slope * distf
            sc = jnp.where(valid, sc, -jnp.inf)
            outs.append(_dot(_softmax_rows(sc, extra=sinks[h]), vb))
    return outs


def _swa_args(q_ref, kc_ref, kp_ref, vc_ref, vp_ref, sink_ref):
    d = SWA_HEAD_DIM
    qs = [q_ref[:, h * d:(h + 1) * d] for h in range(SWA_Q_HEADS)]
    per_kv = lambda ref: [ref[:, h * d:(h + 1) * d] for h in range(SWA_KV_HEADS)]
    sinks = [sink_ref[0:1, h:h + 1] for h in range(SWA_Q_HEADS)]
    return qs, per_kv(kc_ref), per_kv(kp_ref), per_kv(vc_ref), per_kv(vp_ref), sinks


def _swa_specs(nb, order):
    t, kvw = SWA_BLOCK, SWA_KV_HEADS * SWA_HEAD_DIM
    k_col, v_col = MIX_WIDTH // kvw, MIX_WIDTH // kvw + 1
    q = pl.BlockSpec((t, MIX_WIDTH), lambda n: (order(n), 0))
    kc = pl.BlockSpec((t, kvw), lambda n: (order(n), k_col))
    kp = pl.BlockSpec((t, kvw), lambda n: (jnp.maximum(order(n) - 1, 0), k_col))
    vc = pl.BlockSpec((t, kvw), lambda n: (order(n), v_col))
    vp = pl.BlockSpec((t, kvw), lambda n: (jnp.maximum(order(n) - 1, 0), v_col))
    sink = pl.BlockSpec((1, LANES), lambda n: (0, 0))
    return [q, kc, kp, vc, vp, sink]


def _pad_lanes(v):
    return jnp.pad(v.reshape(1, -1), ((0, 0), (0, LANES - v.size)))


def swa_fwd(p, sinks, name):
    s = p.shape[0]
    nb = s // SWA_BLOCK

    def body(q_ref, kc_ref, kp_ref, vc_ref, vp_ref, sink_ref, o_ref):
        outs = _swa_fn(*_swa_args(q_ref, kc_ref, kp_ref, vc_ref, vp_ref, sink_ref), pl.program_id(0) > 0)
        o_ref[...] = jnp.concatenate(outs, axis=1).astype(o_ref.dtype)

    return pl.pallas_call(
        body, grid=(nb,), in_specs=_swa_specs(nb, lambda n: n),
        out_specs=pl.BlockSpec((SWA_BLOCK, MIX_WIDTH), lambda n: (n, 0)),
        out_shape=jax.ShapeDtypeStruct((s, MIX_WIDTH), BF16), compiler_params=_cp(("parallel",)),
        name=name)(p, p, p, p, p, _pad_lanes(sinks))


def swa_bwd(p, sinks, dcat, name):
    s = p.shape[0]
    nb = s // SWA_BLOCK
    t, d, kvw = SWA_BLOCK, SWA_HEAD_DIM, SWA_KV_HEADS * SWA_HEAD_DIM
    rev = lambda n: nb - 1 - n

    def body(q_ref, kc_ref, kp_ref, vc_ref, vp_ref, sink_ref, dy_ref, dq_ref, dk_ref, dv_ref, ds_ref, ck, cv):
        n = pl.program_id(0)

        @pl.when(n == 0)
        def _():
            ck[...] = jnp.zeros_like(ck)
            cv[...] = jnp.zeros_like(cv)
            ds_ref[...] = jnp.zeros_like(ds_ref)

        args = _swa_args(q_ref, kc_ref, kp_ref, vc_ref, vp_ref, sink_ref)
        _, vjp = jax.vjp(functools.partial(_swa_fn, not_first=rev(n) > 0), *args)
        dqs, dkcs, dkps, dvcs, dvps, dsinks = vjp([dy_ref[:, h * d:(h + 1) * d] for h in range(SWA_Q_HEADS)])
        dq_ref[...] = jnp.concatenate(dqs, axis=1).astype(dq_ref.dtype)
        dk_ref[...] = (jnp.concatenate(dkcs, axis=1) + ck[...]).astype(dk_ref.dtype)
        dv_ref[...] = (jnp.concatenate(dvcs, axis=1) + cv[...]).astype(dv_ref.dtype)
        ck[...] = jnp.concatenate(dkps, axis=1)
        cv[...] = jnp.concatenate(dvps, axis=1)
        lane = lax.broadcasted_iota(jnp.int32, (1, LANES), 1)
        acc = jnp.zeros((1, LANES), F32)
        for h in range(SWA_Q_HEADS):
            acc = acc + jnp.where(lane == h, dsinks[h], 0.0)
        ds_ref[...] += acc

    dy = pl.BlockSpec((t, MIX_WIDTH), lambda n: (rev(n), 0))
    kv_out = pl.BlockSpec((t, kvw), lambda n: (rev(n), 0))
    dq, dk, dv, ds = pl.pallas_call(
        body, grid=(nb,), in_specs=_swa_specs(nb, rev) + [dy],
        out_specs=[dy, kv_out, kv_out, pl.BlockSpec((1, LANES), lambda n: (0, 0))],
        out_shape=[jax.ShapeDtypeStruct((s, MIX_WIDTH), BF16), jax.ShapeDtypeStruct((s, kvw), BF16),
                   jax.ShapeDtypeStruct((s, kvw), BF16), jax.ShapeDtypeStruct((1, LANES), F32)],
        scratch_shapes=[pltpu.VMEM((t, kvw), F32), pltpu.VMEM((t, kvw), F32)],
        compiler_params=_cp(("arbitrary",)), name=name)(p, p, p, p, p, _pad_lanes(sinks), dcat)
    return dq, dk, dv, ds[0, :SWA_Q_HEADS]


RW_SHIFT = 5120
RW_R, RW_K, RW_V, RW_WD, RW_AD, RW_GD = 0, 1536, 3072, 4608, 4736, 4864


def _head_matrix(width, head_dim):
    e = (np.arange(width)[:, None] // head_dim == np.arange(LANES)[None, :]).astype(np.float32)
    return jnp.asarray(e), jnp.asarray(e.T)


def _rwkv_pre_fn(pieces, shifted, mus, w0, wdu, a0, wiu, wgu, k_k, k_a, e, et):
    r, k, v, wd, ad, gd = [p + (s - p) * mu for p, s, mu in zip(pieces, shifted, mus)]
    w_log = -_softplus(-(w0 + _dot(jnp.tanh(wd), wdu))) - 0.5
    lw = -jnp.exp(w_log)
    a = _sigmoid(a0 + _dot(ad, wiu))
    g = _dot(_sigmoid(gd), wgu)
    kkr = k * k_k
    kk = kkr * _dot(lax.rsqrt(_dot(kkr * kkr, e) + 1e-6), et)
    k2 = k * (1.0 + (a - 1.0) * k_a)
    return r, lw, k2, v, kk, kk * a, g


_RW_GROUPS = ((RW_R, MIX_WIDTH), (RW_K, MIX_WIDTH), (RW_V, MIX_WIDTH), (RW_WD, LANES), (RW_AD, LANES), (RW_GD, 2 * LANES))


def _rwkv_pre_inputs(p_ref, prev_ref, mu_ref, first):
    pieces = [p_ref[:, o:o + n] for o, n in _RW_GROUPS]
    shifted = [_shift_down(p_ref[:, o:o + n], prev_ref[:, o:o + n] * first, 1) for o, n in _RW_GROUPS]
    mus = [mu_ref[:, o:o + n] for o, n in _RW_GROUPS]
    return pieces, shifted, mus


def _rwkv_param_specs():
    vec = lambda n: pl.BlockSpec((1, n), lambda i: (0, 0))
    mat = lambda r, c: pl.BlockSpec((r, c), lambda i: (0, 0))
    return [vec(RW_SHIFT), vec(MIX_WIDTH), mat(LANES, MIX_WIDTH), vec(MIX_WIDTH), mat(LANES, MIX_WIDTH),
            mat(2 * LANES, MIX_WIDTH), vec(MIX_WIDTH), vec(MIX_WIDTH), mat(MIX_WIDTH, LANES), mat(LANES, MIX_WIDTH)]


def rwkv_pre_fwd(p, params, name):
    s = p.shape[0]
    t = _tile(s, (128, 64))

    def body(p_ref, prev_ref, mu_ref, *rest):
        prm, outs = rest[:9], rest[9:]
        first = (pl.program_id(0) > 0).astype(F32)
        pieces, shifted, mus = _rwkv_pre_inputs(p_ref, prev_ref, mu_ref, first)
        res = _rwkv_pre_fn(pieces, shifted, mus, *[q[...] for q in prm])
        for o_ref, val in zip(outs, res):
            o_ref[...] = val

    row = pl.BlockSpec((t, RW_SHIFT), lambda i: (i, 0))
    prev = pl.BlockSpec((SUBLANES, RW_SHIFT), lambda i: (jnp.maximum(i * (t // SUBLANES) - 1, 0), 0))
    out = pl.BlockSpec((t, MIX_WIDTH), lambda i: (i, 0))
    return pl.pallas_call(
        body, grid=(s // t,), in_specs=[row, prev] + _rwkv_param_specs(), out_specs=[out] * 7,
        out_shape=[jax.ShapeDtypeStruct((s, MIX_WIDTH), F32)] * 7, compiler_params=_cp(("parallel",)),
        name=name)(p, p, *params)


def rwkv_pre_bwd(p, params, cots, name):
    s = p.shape[0]
    t = _tile(s, (64, 32))

    def body(p_ref, prev_ref, mu_ref, *rest):
        prm, cot, outs = rest[:9], rest[9:19], rest[19:]
        dp_ref, dps_ref, grads = outs[0], outs[1], outs[2:]
        i = pl.program_id(0)
        first = (i > 0).astype(F32)
        pieces, shifted, mus = _rwkv_pre_inputs(p_ref, prev_ref, mu_ref, first)
        prm_v = [q[...] for q in prm]
        fn = lambda pieces, shifted, mus, *small: _rwkv_pre_fn(pieces, shifted, mus, *small, prm_v[7], prm_v[8])
        _, vjp = jax.vjp(fn, pieces, shifted, mus, *prm_v[:7])
        dr, dw, dk2, dv, dkk, db, dr2, dk22, dv2, dg = [c[...] for c in cot]
        res = vjp((dr + dr2, dw, dk2 + dk22, dv + dv2, dkk, db, dg))
        dpieces, dshifted, dmus, dsmall = res[0], res[1], res[2], res[3:]
        for (o, n), dpi, dsi in zip(_RW_GROUPS, dpieces, dshifted):
            dp_ref[:, o:o + n] = dpi
            dps_ref[:, o:o + n] = dsi

        @pl.when(i == 0)
        def _():
            for g_ref in grads:
                g_ref[...] = jnp.zeros_like(g_ref)

        for (o, n), dmu in zip(_RW_GROUPS, dmus):
            grads[0][:, o:o + n] += dmu
        for g_ref, dval in zip(grads[1:], dsmall):
            g_ref[...] += dval

    row = pl.BlockSpec((t, RW_SHIFT), lambda i: (i, 0))
    prev = pl.BlockSpec((SUBLANES, RW_SHIFT), lambda i: (jnp.maximum(i * (t // SUBLANES) - 1, 0), 0))
    act = pl.BlockSpec((t, MIX_WIDTH), lambda i: (i, 0))
    pspecs = _rwkv_param_specs()
    full = jax.ShapeDtypeStruct((s, RW_SHIFT), F32)
    gshapes = [jax.ShapeDtypeStruct(q.shape, F32) for q in params[:8]]
    return pl.pallas_call(
        body, grid=(s // t,), in_specs=[row, prev] + pspecs + [act] * 10, out_specs=[row, row] + pspecs[:8],
        out_shape=[full, full] + gshapes, compiler_params=_cp(("arbitrary",)), name=name)(p, p, *params, *cots)


def shift_add(a, b, js, out_dtype, name):
    s, c = a.shape
    t = _tile(s, (256, 128, 64))
    tc = _tile(c, (1024, 768, 512, 640, 384, 256, 128))
    nt, nb = s // t, len(b)

    def body(a_ref, *rest):
        b_refs, n_refs, o_ref = rest[:nb], rest[nb:2 * nb], rest[2 * nb]
        last = (pl.program_id(1) < nt - 1).astype(F32)
        acc = a_ref[...]
        for b_ref, n_ref, j in zip(b_refs, n_refs, js):
            acc = acc + _shift_up(b_ref[...], n_ref[...] * last, j)
        o_ref[...] = acc.astype(o_ref.dtype)

    tile = pl.BlockSpec((t, tc), lambda j, i: (i, j))
    _, nxt = _halo_specs(t, s, tc, lambda j: j)
    return pl.pallas_call(
        body, grid=(c // tc, nt), in_specs=[tile] * (1 + nb) + [nxt] * nb, out_specs=tile,
        out_shape=jax.ShapeDtypeStruct((s, c), out_dtype), compiler_params=_cp(("parallel", "parallel")),
        name=name)(a, *b, *b)


def _rwkv_post_fn(y, r, k2, v, g, gn_g, gn_b, r_k, e, et):
    n = RWKV_HEAD_DIM
    yc = y - _dot(_dot(y, e), et) * (1.0 / n)
    rstd = lax.rsqrt(_dot(yc * yc, e) * (1.0 / n) + RWKV_GN_EPS)
    yn = yc * _dot(rstd, et) * gn_g + gn_b
    bonus = _dot(_dot(r * k2 * r_k, e), et) * v
    return (yn + bonus) * g


def rwkv_post_fwd(acts, params, name):
    s = acts[0].shape[0]
    t = _tile(s, (256, 128))

    def body(*refs):
        vals = [q[...] for q in refs[:10]]
        refs[10][...] = _rwkv_post_fn(*vals).astype(refs[10].dtype)

    act = pl.BlockSpec((t, MIX_WIDTH), lambda i: (i, 0))
    vec = pl.BlockSpec((1, MIX_WIDTH), lambda i: (0, 0))
    mats = [pl.BlockSpec((MIX_WIDTH, LANES), lambda i: (0, 0)), pl.BlockSpec((LANES, MIX_WIDTH), lambda i: (0, 0))]
    return pl.pallas_call(
        body, grid=(s // t,), in_specs=[act] * 5 + [vec] * 3 + mats, out_specs=act,
        out_shape=jax.ShapeDtypeStruct((s, MIX_WIDTH), BF16), compiler_params=_cp(("parallel",)),
        name=name)(*acts, *params)


def rwkv_post_bwd(acts, params, dcat, name):
    s = acts[0].shape[0]
    t = _tile(s, (128, 64))

    def body(*refs):
        ins, dy_ref, outs = refs[:10], refs[10], refs[11:]
        vals = [q[...] for q in ins]
        fn = lambda *a: _rwkv_post_fn(*a, vals[8], vals[9])
        _, vjp = jax.vjp(fn, *vals[:8])
        res = vjp(dy_ref[...])
        for o_ref, val in zip(outs[:5], res[:5]):
            o_ref[...] = val

        @pl.when(pl.program_id(0) == 0)
        def _():
            for g_ref in outs[5:]:
                g_ref[...] = jnp.zeros_like(g_ref)

        for g_ref, val in zip(outs[5:], res[5:]):
            g_ref[...] += val

    act = pl.BlockSpec((t, MIX_WIDTH), lambda i: (i, 0))
    vec = pl.BlockSpec((1, MIX_WIDTH), lambda i: (0, 0))
    mats = [pl.BlockSpec((MIX_WIDTH, LANES), lambda i: (0, 0)), pl.BlockSpec((LANES, MIX_WIDTH), lambda i: (0, 0))]
    a_shape = jax.ShapeDtypeStruct((s, MIX_WIDTH), F32)
    v_shape = jax.ShapeDtypeStruct((1, MIX_WIDTH), F32)
    return pl.pallas_call(
        body, grid=(s // t,), in_specs=[act] * 5 + [vec] * 3 + mats + [act], out_specs=[act] * 5 + [vec] * 3,
        out_shape=[a_shape] * 5 + [v_shape] * 3, compiler_params=_cp(("arbitrary",)), name=name)(*acts, *params, dcat)


RW_CHUNK = 64


RW_HEADS_PER_STEP = 4


def _heads(f, *per_head):
    return [f(*xs) for xs in zip(*per_head)]


def _rwkv_chunk_fn(r, lw, k, v, kk, b, st):
    c = RW_CHUNK
    ri = lax.broadcasted_iota(jnp.int32, (c, c), 0)
    ci = lax.broadcasted_iota(jnp.int32, (c, c), 1)
    incl, strict = ri >= ci, ri > ci
    ltri = incl.astype(F32)
    eye = (ri == ci).astype(F32)
    last_col = (ci == c - 1).astype(F32)
    last_row = (ri == c - 1).astype(F32)
    gc = _heads(lambda lw: _dot(ltri, lw), lw)
    a_t = _heads(lambda kk, gc, lw: -kk * jnp.exp(gc - lw), kk, gc, lw)
    e_neg = _heads(lambda gc: jnp.exp(-gc), gc)
    b_t = _heads(jnp.multiply, b, e_neg)
    k_t = _heads(jnp.multiply, k, e_neg)
    r_t = _heads(lambda r, gc: r * jnp.exp(gc), r, gc)
    m_ab = _heads(lambda x, y: jnp.where(strict, _dot_nt(x, y), 0.0), a_t, b_t)
    m_ak = _heads(lambda x, y: jnp.where(strict, _dot_nt(x, y), 0.0), a_t, k_t)
    m_rb = _heads(lambda x, y: jnp.where(incl, _dot_nt(x, y), 0.0), r_t, b_t)
    m_rk = _heads(lambda x, y: jnp.where(incl, _dot_nt(x, y), 0.0), r_t, k_t)
    tinv, pw = _heads(lambda m: eye + m, m_ab), m_ab
    for _ in range(5):
        pw = _heads(lambda p: _dot(p, p), pw)
        tinv = _heads(lambda t, p: t + _dot(t, p), tinv, pw)
    rhs = _heads(lambda a, s, m, v: _dot(a, s) + _dot(m, v), a_t, st, m_ak, v)
    u = _heads(_dot, tinv, rhs)
    y = _heads(lambda r, s, mb, u, mk, v: _dot(r, s) + _dot(mb, u) + _dot(mk, v), r_t, st, m_rb, u, m_rk, v)
    dec = _heads(lambda gc: jnp.exp(_dot(last_col, gc) - gc), gc)
    g_end = _heads(lambda gc: _dot_tn(gc, last_row), gc)
    new_st = _heads(lambda s, g, b, d, u, k, v: s * jnp.exp(g) + _dot_tn(b * d, u) + _dot_tn(k * d, v),
                    st, g_end, b, dec, u, k, v)
    return y, new_st


def rwkv_scan_fwd(r, lw, k, v, kk, b, name):
    s = r.shape[0]
    n, hp = RWKV_HEAD_DIM, RW_HEADS_PER_STEP
    nchunk, width = s // RW_CHUNK, RWKV_HEAD_DIM * RW_HEADS_PER_STEP

    def body(r_ref, w_ref, k_ref, v_ref, kk_ref, b_ref, y_ref, ck_ref, carry):
        @pl.when(pl.program_id(1) == 0)
        def _():
            carry[...] = jnp.zeros_like(carry)

        ck_ref[0] = carry[...]
        per_head = lambda ref: [ref[:, h * n:(h + 1) * n] for h in range(hp)]
        ys, sts = _rwkv_chunk_fn(*[per_head(q) for q in (r_ref, w_ref, k_ref, v_ref, kk_ref, b_ref, carry)])
        y_ref[...] = jnp.concatenate(ys, axis=1)
        carry[...] = jnp.concatenate(sts, axis=1)

    blk = pl.BlockSpec((RW_CHUNK, width), lambda j, c: (c, j))
    return pl.pallas_call(
        body, grid=(MIX_WIDTH // width, nchunk), in_specs=[blk] * 6,
        out_specs=[blk, pl.BlockSpec((1, n, width), lambda j, c: (c, 0, j))],
        out_shape=[jax.ShapeDtypeStruct((s, MIX_WIDTH), F32), jax.ShapeDtypeStruct((nchunk, n, MIX_WIDTH), F32)],
        scratch_shapes=[pltpu.VMEM((n, width), F32)],
        compiler_params=_cp(("parallel", "arbitrary")), name=name)(r, lw, k, v, kk, b)


def rwkv_scan_bwd(r, lw, k, v, kk, b, ck, dy, name):
    s = r.shape[0]
    n, hp = RWKV_HEAD_DIM, RW_HEADS_PER_STEP
    nchunk, width = s // RW_CHUNK, RWKV_HEAD_DIM * RW_HEADS_PER_STEP
    rev = lambda c: nchunk - 1 - c

    def body(r_ref, w_ref, k_ref, v_ref, kk_ref, b_ref, ck_ref, dy_ref, *rest):
        outs, carry = rest[:6], rest[6]

        @pl.when(pl.program_id(1) == 0)
        def _():
            carry[...] = jnp.zeros_like(carry)

        per_head = lambda ref: [ref[:, h * n:(h + 1) * n] for h in range(hp)]
        args = [per_head(q) for q in (r_ref, w_ref, k_ref, v_ref, kk_ref, b_ref)]
        args.append([ck_ref[0, :, h * n:(h + 1) * n] for h in range(hp)])
        _, vjp = jax.vjp(_rwkv_chunk_fn, *args)
        grads = vjp((per_head(dy_ref), per_head(carry)))
        for q in range(6):
            outs[q][...] = jnp.concatenate(grads[q], axis=1)
        carry[...] = jnp.concatenate(grads[6], axis=1)

    blk = pl.BlockSpec((RW_CHUNK, width), lambda j, c: (rev(c), j))
    out = jax.ShapeDtypeStruct((s, MIX_WIDTH), F32)
    return pl.pallas_call(
        body, grid=(MIX_WIDTH // width, nchunk),
        in_specs=[blk] * 6 + [pl.BlockSpec((1, n, width), lambda j, c: (rev(c), 0, j)), blk],
        out_specs=[blk] * 6, out_shape=[out] * 6, scratch_shapes=[pltpu.VMEM((n, width), F32)],
        compiler_params=_cp(("parallel", "arbitrary")), name=name)(r, lw, k, v, kk, b, ck, dy)


GD_Q, GD_K, GD_V, GD_Z, GD_QMEM, GD_BT, GD_AT, GD_COLS = 0, 768, 1536, 3072, 4608, 5120, 5248, 5376
_GD_GROUPS = ((GD_Q, GDN_QK_WIDTH), (GD_K, GDN_QK_WIDTH), (GD_V, MIX_WIDTH))


def _gdn_pre_fn(xs, convs, bt, at, a_log, dt_bias, e6, e6t, ebc):
    k_w = GDN_CONV
    acts = [_silu(sum(convs[g][j] * xs[g][k_w - 1 - j] for j in range(k_w))) for g in range(3)]
    l2 = lambda x: x * _dot(lax.rsqrt(_dot(x * x, e6) + 1e-6), e6t)
    beta = _sigmoid(bt)
    g = -jnp.exp(a_log) * _softplus(at + dt_bias)
    return l2(acts[0]), l2(acts[1]), acts[2], _dot(g, ebc), _dot(beta, ebc)


def _gdn_pre_inputs(x_ref, prev_ref, conv_ref, first):
    xs = [[_shift_down(x_ref[:, o:o + n], prev_ref[:, o:o + n] * first, j) for j in range(GDN_CONV)]
          for o, n in _GD_GROUPS]
    convs = [[conv_ref[j:j + 1, o:o + n] for j in range(GDN_CONV)] for o, n in _GD_GROUPS]
    return xs, convs


def _gdn_pre_specs(t):
    x = pl.BlockSpec((t, GDN_CONV_WIDTH), lambda i: (i, 0))
    prev = pl.BlockSpec((SUBLANES, GDN_CONV_WIDTH), lambda i: (jnp.maximum(i * (t // SUBLANES) - 1, 0), 0))
    bta = pl.BlockSpec((t, 2 * LANES), lambda i: (i, GD_BT // (2 * LANES)))
    conv = pl.BlockSpec((GDN_CONV, GDN_CONV_WIDTH), lambda i: (0, 0))
    vec = pl.BlockSpec((1, LANES), lambda i: (0, 0))
    mats = [pl.BlockSpec((GDN_QK_WIDTH, LANES), lambda i: (0, 0)), pl.BlockSpec((LANES, GDN_QK_WIDTH), lambda i: (0, 0)),
            pl.BlockSpec((LANES, MIX_WIDTH), lambda i: (0, 0))]
    return [x, prev, bta, conv, vec, vec] + mats


def gdn_pre_fwd(p, params, name):
    s = p.shape[0]
    t = _tile(s, (128, 64))

    def body(x_ref, prev_ref, bta_ref, conv_ref, al_ref, dt_ref, e6_ref, e6t_ref, ebc_ref, *outs):
        first = (pl.program_id(0) > 0).astype(F32)
        xs, convs = _gdn_pre_inputs(x_ref, prev_ref, conv_ref, first)
        res = _gdn_pre_fn(xs, convs, bta_ref[:, :LANES], bta_ref[:, LANES:], al_ref[...], dt_ref[...],
                          e6_ref[...], e6t_ref[...], ebc_ref[...])
        for o_ref, val in zip(outs, res):
            o_ref[...] = val

    qk = pl.BlockSpec((t, GDN_QK_WIDTH), lambda i: (i, 0))
    wide = pl.BlockSpec((t, MIX_WIDTH), lambda i: (i, 0))
    qk_s, wide_s = jax.ShapeDtypeStruct((s, GDN_QK_WIDTH), F32), jax.ShapeDtypeStruct((s, MIX_WIDTH), F32)
    return pl.pallas_call(
        body, grid=(s // t,), in_specs=_gdn_pre_specs(t), out_specs=[qk, qk, wide, wide, wide],
        out_shape=[qk_s, qk_s, wide_s, wide_s, wide_s], compiler_params=_cp(("parallel",)), name=name)(p, p, p, *params)


def gdn_pre_bwd(p, params, cots, name):
    s = p.shape[0]
    t = _tile(s, (64, 32))

    def body(x_ref, prev_ref, bta_ref, conv_ref, al_ref, dt_ref, e6_ref, e6t_ref, ebc_ref, *rest):
        cot, outs = rest[:5], rest[5:]
        dxs, dbta_ref, dconv_ref, dal_ref, ddt_ref = outs[:4], outs[4], outs[5], outs[6], outs[7]
        i = pl.program_id(0)
        first = (i > 0).astype(F32)
        xs, convs = _gdn_pre_inputs(x_ref, prev_ref, conv_ref, first)
        mats = (e6_ref[...], e6t_ref[...], ebc_ref[...])
        fn = lambda xs, convs, bt, at, al, dt: _gdn_pre_fn(xs, convs, bt, at, al, dt, *mats)
        _, vjp = jax.vjp(fn, xs, convs, bta_ref[:, :LANES], bta_ref[:, LANES:], al_ref[...], dt_ref[...])
        d_xs, d_convs, d_bt, d_at, d_al, d_dt = vjp(tuple(c[...] for c in cot))
        for g, (o, n) in enumerate(_GD_GROUPS):
            for j in range(GDN_CONV):
                dxs[j][:, o:o + n] = d_xs[g][j]
        dbta_ref[...] = jnp.concatenate([d_bt, d_at], axis=1).astype(dbta_ref.dtype)

        @pl.when(i == 0)
        def _():
            dconv_ref[...] = jnp.zeros_like(dconv_ref)
            dal_ref[...] = jnp.zeros_like(dal_ref)
            ddt_ref[...] = jnp.zeros_like(ddt_ref)

        for g, (o, n) in enumerate(_GD_GROUPS):
            for j in range(GDN_CONV):
                dconv_ref[j:j + 1, o:o + n] += d_convs[g][j]
        dal_ref[...] += d_al
        ddt_ref[...] += d_dt

    specs = _gdn_pre_specs(t)
    qk = pl.BlockSpec((t, GDN_QK_WIDTH), lambda i: (i, 0))
    wide = pl.BlockSpec((t, MIX_WIDTH), lambda i: (i, 0))
    x_s = jax.ShapeDtypeStruct((s, GDN_CONV_WIDTH), F32)
    vec_s = jax.ShapeDtypeStruct((1, LANES), F32)
    return pl.pallas_call(
        body, grid=(s // t,), in_specs=specs + [qk, qk, wide, wide, wide],
        out_specs=[specs[0]] * 4 + [pl.BlockSpec((t, 2 * LANES), lambda i: (i, 0)), specs[3], specs[4], specs[5]],
        out_shape=[x_s] * 4 + [jax.ShapeDtypeStruct((s, 2 * LANES), BF16),
                               jax.ShapeDtypeStruct((GDN_CONV, GDN_CONV_WIDTH), F32), vec_s, vec_s],
        compiler_params=_cp(("arbitrary",)), name=name)(p, p, p, *params, *cots)


def _gdn_post_fn(o, z, norm_g, e12, e12t, trep):
    rstd = lax.rsqrt(_dot(o * o, e12) * (1.0 / GDN_HEAD_DIM) + NORM_EPS)
    return o * _dot(rstd, e12t) * _dot(norm_g, trep) * _silu(z)


def _gdn_post_specs(t):
    act = pl.BlockSpec((t, MIX_WIDTH), lambda i: (i, 0))
    z = pl.BlockSpec((t, MIX_WIDTH), lambda i: (i, GD_Z // MIX_WIDTH))
    mats = [pl.BlockSpec((SUBLANES, LANES), lambda i: (0, 0)), pl.BlockSpec((MIX_WIDTH, LANES), lambda i: (0, 0)),
            pl.BlockSpec((LANES, MIX_WIDTH), lambda i: (0, 0)), pl.BlockSpec((LANES, MIX_WIDTH), lambda i: (0, 0))]
    return [act, z] + mats


def gdn_post_fwd(o, p, params, name):
    s = o.shape[0]
    t = _tile(s, (256, 128))

    def body(o_ref, z_ref, ng_ref, e_ref, et_ref, tr_ref, out_ref):
        res = _gdn_post_fn(o_ref[...], z_ref[...], ng_ref[0:1, :], e_ref[...], et_ref[...], tr_ref[...])
        out_ref[...] = res.astype(out_ref.dtype)

    act = pl.BlockSpec((t, MIX_WIDTH), lambda i: (i, 0))
    return pl.pallas_call(
        body, grid=(s // t,), in_specs=_gdn_post_specs(t), out_specs=act,
        out_shape=jax.ShapeDtypeStruct((s, MIX_WIDTH), BF16), compiler_params=_cp(("parallel",)),
        name=name)(o, p, *params)


def gdn_post_bwd(o, p, params, dcat, name):
    s = o.shape[0]
    t = _tile(s, (128, 64))

    def body(o_ref, z_ref, ng_ref, e_ref, et_ref, tr_ref, dy_ref, do_ref, dz_ref, dng_ref):
        mats = (e_ref[...], et_ref[...], tr_ref[...])
        fn = lambda o, z, ng: _gdn_post_fn(o, z, ng, *mats)
        _, vjp = jax.vjp(fn, o_ref[...], z_ref[...], ng_ref[0:1, :])
        d_o, d_z, d_ng = vjp(dy_ref[...])
        do_ref[...] = d_o
        dz_ref[...] = d_z.astype(dz_ref.dtype)

        @pl.when(pl.program_id(0) == 0)
        def _():
            dng_ref[...] = jnp.zeros_like(dng_ref)

        dng_ref[...] += d_ng

    act = pl.BlockSpec((t, MIX_WIDTH), lambda i: (i, 0))
    return pl.pallas_call(
        body, grid=(s // t,), in_specs=_gdn_post_specs(t) + [act],
        out_specs=[act, act, pl.BlockSpec((1, LANES), lambda i: (0, 0))],
        out_shape=[jax.ShapeDtypeStruct((s, MIX_WIDTH), F32), jax.ShapeDtypeStruct((s, MIX_WIDTH), BF16),
                   jax.ShapeDtypeStruct((1, LANES), F32)],
        compiler_params=_cp(("arbitrary",)), name=name)(o, p, *params, dcat)


GDN_REP = GDN_V_HEADS // GDN_QK_HEADS
GDN_QK_PER_STEP = 2


def _gdn_chunk_fn(q, k, v, gb, bb, gb64, state):
    c = GDN_CHUNK
    ri = lax.broadcasted_iota(jnp.int32, (c, c), 0)
    ci = lax.broadcasted_iota(jnp.int32, (c, c), 1)
    causal, strict = ri >= ci, ri > ci
    ltri = causal.astype(F32)
    eye = (ri == ci).astype(F32)
    first_col = (ci == 0).astype(F32)
    last_col = (ci == c - 1).astype(F32)
    last_col_tall = (lax.broadcasted_iota(jnp.int32, (GDN_HEAD_DIM, c), 1) == c - 1).astype(F32)
    of_value_head = lambda per_qk: [per_qk[h // GDN_REP] for h in range(len(v))]

    qs = of_value_head(_heads(lambda q: q * (GDN_HEAD_DIM ** -0.5), q))
    k = of_value_head(k)
    gc = _heads(lambda g: _dot(ltri, g), gb)
    gd = _heads(lambda g: _dot(ltri, g), gb64)
    decay = _heads(lambda gd: jnp.exp(jnp.where(causal, gd - _dot_nt(first_col, gd), -jnp.inf)), gd)
    kb = _heads(jnp.multiply, k, bb)
    lmat = _heads(lambda kb, k, dc: jnp.where(strict, _dot_nt(kb, k) * dc, 0.0), kb, k, decay)
    tmat, pw = _heads(lambda l: eye - l, lmat), lmat
    for _ in range(5):
        pw = _heads(lambda p: _dot(p, p), pw)
        tmat = _heads(lambda t, p: t + _dot(t, p), tmat, pw)
    eg = _heads(jnp.exp, gc)
    u = _heads(lambda t, v, bb: _dot(t, v * bb), tmat, v, bb)
    w = _heads(lambda t, kb, eg: _dot(t, kb * eg), tmat, kb, eg)
    a_qk = _heads(lambda q, k, dc: jnp.where(causal, _dot_nt(q, k) * dc, 0.0), qs, k, decay)
    k_dec = _heads(lambda k, gc: k * jnp.exp(_dot(last_col, gc) - gc), k, gc)
    v_new = _heads(lambda u, w, s: u - _dot(w, s), u, w, state)
    out = _heads(lambda q, eg, s, a, vn: _dot(q * eg, s) + _dot(a, vn), qs, eg, state, a_qk, v_new)
    new_state = _heads(lambda s, gc, kd, vn: s * jnp.exp(_dot(last_col_tall, gc)) + _dot_tn(kd, vn),
                       state, gc, k_dec, v_new)
    return out, new_state


def _gdn_chunk_specs(order):
    c, d, nq = GDN_CHUNK, GDN_HEAD_DIM, GDN_QK_PER_STEP
    qk = pl.BlockSpec((c, nq * d), lambda j, n: (order(n), j))
    vh = pl.BlockSpec((c, nq * GDN_REP * d), lambda j, n: (order(n), j))
    st = pl.BlockSpec((nq * GDN_REP, 1, d, d), lambda j, n: (j, order(n), 0, 0))
    return qk, vh, st


def _gdn_chunk_args(q_ref, k_ref, v_ref, gb_ref, bb_ref):
    d, nq = GDN_HEAD_DIM, GDN_QK_PER_STEP
    nv = nq * GDN_REP
    per = lambda ref, n: [ref[:, h * d:(h + 1) * d] for h in range(n)]
    lead = [gb_ref[:, h * d:h * d + GDN_CHUNK] for h in range(nv)]
    return per(q_ref, nq), per(k_ref, nq), per(v_ref, nv), per(gb_ref, nv), per(bb_ref, nv), lead


def gdn_chunk_fwd(q, k, v, gb, bb, name):
    s = q.shape[0]
    nc, d, nv = s // GDN_CHUNK, GDN_HEAD_DIM, GDN_QK_PER_STEP * GDN_REP

    def body(q_ref, k_ref, v_ref, gb_ref, bb_ref, o_ref, st_ref, carry):
        @pl.when(pl.program_id(1) == 0)
        def _():
            carry[...] = jnp.zeros_like(carry)

        states = [carry[h] for h in range(nv)]
        for h in range(nv):
            st_ref[h, 0] = states[h]
        outs, new_states = _gdn_chunk_fn(*_gdn_chunk_args(q_ref, k_ref, v_ref, gb_ref, bb_ref), states)
        o_ref[...] = jnp.concatenate(outs, axis=1)
        for h in range(nv):
            carry[h] = new_states[h]

    qk, vh, st = _gdn_chunk_specs(lambda n: n)
    return pl.pallas_call(
        body, grid=(GDN_QK_HEADS // GDN_QK_PER_STEP, nc), in_specs=[qk, qk, vh, vh, vh], out_specs=[vh, st],
        out_shape=[jax.ShapeDtypeStruct((s, MIX_WIDTH), F32), jax.ShapeDtypeStruct((GDN_V_HEADS, nc, d, d), F32)],
        scratch_shapes=[pltpu.VMEM((nv, d, d), F32)],
        compiler_params=_cp(("parallel", "arbitrary")), name=name)(q, k, v, gb, bb)


def gdn_chunk_bwd(q, k, v, gb, bb, states, do, name):
    s = q.shape[0]
    nc, d, nv = s // GDN_CHUNK, GDN_HEAD_DIM, GDN_QK_PER_STEP * GDN_REP
    rev = lambda n: nc - 1 - n

    def body(q_ref, k_ref, v_ref, gb_ref, bb_ref, st_ref, do_ref, dq_ref, dk_ref, dv_ref, dg_ref, db_ref, carry):
        @pl.when(pl.program_id(1) == 0)
        def _():
            carry[...] = jnp.zeros_like(carry)

        args = _gdn_chunk_args(q_ref, k_ref, v_ref, gb_ref, bb_ref) + ([st_ref[h, 0] for h in range(nv)],)
        _, vjp = jax.vjp(_gdn_chunk_fn, *args)
        cot = ([do_ref[:, h * d:(h + 1) * d] for h in range(nv)], [carry[h] for h in range(nv)])
        d_q, d_k, d_v, d_gb, d_bb, d_gb64, d_state = vjp(cot)
        dq_ref[...] = jnp.concatenate(d_q, axis=1)
        dk_ref[...] = jnp.concatenate(d_k, axis=1)
        dv_ref[...] = jnp.concatenate(d_v, axis=1)
        db_ref[...] = jnp.concatenate(d_bb, axis=1)
        dg_ref[...] = jnp.concatenate(d_gb, axis=1)
        for h in range(nv):
            dg_ref[:, h * d:h * d + GDN_CHUNK] += d_gb64[h]
            carry[h] = d_state[h]

    qk, vh, st = _gdn_chunk_specs(rev)
    qk_s, wide_s = jax.ShapeDtypeStruct((s, GDN_QK_WIDTH), F32), jax.ShapeDtypeStruct((s, MIX_WIDTH), F32)
    return pl.pallas_call(
        body, grid=(GDN_QK_HEADS // GDN_QK_PER_STEP, nc), in_specs=[qk, qk, vh, vh, vh, st, vh],
        out_specs=[qk, qk, vh, vh, vh], out_shape=[qk_s, qk_s, wide_s, wide_s, wide_s],
        scratch_shapes=[pltpu.VMEM((nv, d, d), F32)],
        compiler_params=_cp(("parallel", "arbitrary")), name=name)(q, k, v, gb, bb, states, do)


WEIGHTS = ['attn_norm', 'mem_norm', 'w_mem_kv', 'w_out', 'ffn_norm', 'w_ffn_up', 'ffn_conv', 'w_ffn_down', 'final_norm',
           'a_w_in', 'a_sinks', 'b_w_in', 'b_mu', 'b_w0', 'b_w_decay_up', 'b_a0', 'b_w_iclr_up', 'b_w_gate_up', 'b_k_k',
           'b_k_a', 'b_r_k', 'b_gn_g', 'b_gn_b', 'c_w_in', 'c_conv', 'c_a_log', 'c_dt_bias', 'c_norm_g']
INPUTS = ['x', 'mem'] + WEIGHTS + ['loss_target'] + ['m_' + n for n in WEIGHTS] + ['v_' + n for n in WEIGHTS]
REPLICATED = ['attn_norm', 'mem_norm', 'ffn_norm', 'final_norm', 'a_sinks', 'b_mu', 'b_w0', 'b_a0', 'b_k_k', 'b_k_a',
              'b_r_k', 'b_gn_g', 'b_gn_b', 'c_a_log', 'c_dt_bias', 'c_norm_g']
C_MIX = GDN_CONV_WIDTH + MIX_WIDTH
GATHER_ID = 1


def _cols_to_shards(full):
    rows, cols = full.shape
    return full.reshape(rows, N_DEV, cols // N_DEV).transpose(1, 0, 2)


def _shards_to_cols(g):
    return g.transpose(1, 0, 2).reshape(g.shape[1], N_DEV * g.shape[2])


def _pad_to(x, n, axis):
    pad = [(0, 0)] * x.ndim
    pad[axis] = (0, n - x.shape[axis])
    return jnp.pad(x, pad)


def _b_pad_cols(w):
    parts = [w[..., :4608], _pad_to(w[..., 4608:4704], LANES, -1), _pad_to(w[..., 4704:4800], LANES, -1), w[..., 4800:5056]]
    if w.shape[-1] > 5056:
        parts.append(w[..., 5056:])
    return jnp.concatenate(parts, axis=-1)


def _b_unpad_cols(w):
    parts = [w[..., :4608], w[..., RW_WD:RW_WD + RWKV_DECAY_RANK], w[..., RW_AD:RW_AD + RWKV_ICLR_RANK], w[..., RW_GD:RW_SHIFT]]
    if w.shape[-1] > RW_SHIFT:
        parts.append(w[..., RW_SHIFT:])
    return jnp.concatenate(parts, axis=-1)


def _c_pad_cols(w):
    return jnp.concatenate([w[..., :C_MIX], w[..., C_MIX + 24:], _pad_to(w[..., C_MIX:C_MIX + 12], LANES, -1),
                            _pad_to(w[..., C_MIX + 12:C_MIX + 24], LANES, -1)], axis=-1)


def _c_unpad_cols(w):
    return jnp.concatenate([w[..., :C_MIX], w[..., GD_BT:GD_BT + GDN_V_HEADS], w[..., GD_AT:GD_AT + GDN_V_HEADS],
                            w[..., GD_QMEM:GD_BT]], axis=-1)


def _pack(arrays):
    flat = jnp.concatenate([a.reshape(-1).astype(F32) for a in arrays])
    unit = SUBLANES * LANES
    return _pad_to(flat, -(-flat.size // unit) * unit, 0).reshape(-1, LANES)


def _unpack(packed, shapes):
    flat, out, at = packed.reshape(-1), [], 0
    for shp in shapes:
        n = int(np.prod(shp))
        out.append(flat[at:at + n].reshape(shp))
        at += n
    return out


def kernel(*args):
    a = dict(zip(INPUTS, args))
    x0, mem, target = a['x'][0], a['mem'][0], a['loss_target'][0]
    s = x0.shape[0]
    e64, e64t = _head_matrix(MIX_WIDTH, RWKV_HEAD_DIM)
    e6, e6t = _head_matrix(GDN_QK_WIDTH, GDN_HEAD_DIM)
    e12, e12t = _head_matrix(MIX_WIDTH, GDN_HEAD_DIM)
    trep = jnp.asarray((np.arange(LANES)[:, None] == np.arange(MIX_WIDTH)[None, :] % LANES).astype(np.float32))
    row = lambda v: v.reshape(1, -1)

    def in_proj_shard(l):
        kind, j = l % 3, l // 3
        return (a['a_w_in'], a['b_w_in'], a['c_w_in'])[kind][j]

    def small_shards(l):
        kind, j = l % 3, l // 3
        if kind == 1:
            return [a['b_w_decay_up'][j], a['b_w_iclr_up'][j], a['b_w_gate_up'][j]]
        if kind == 2:
            return [a['c_conv'][j]]
        return []

    gathered = []
    for l in range(DEPTH):
        big = [a['w_mem_kv'][l], a['w_out'][l], a['w_ffn_up'][l], a['w_ffn_down'][l], in_proj_shard(l)]
        shards = [w.astype(BF16) for w in big] + [a['ffn_conv'][l]] + small_shards(l)
        if gathered:
            shards, _ = lax.optimization_barrier((shards, gathered[-1]))
        gathered.append(all_gather_many_async(shards, f"gather_weights_{l}", GATHER_ID + l))

    def layer_weights(l, g):
        kind = l % 3
        w_in = _shards_to_cols(g[4])
        lw = dict(w_kv=g[0].reshape(D_MODEL, 2 * MEM_WIDTH), w_out=g[1].reshape(D_MODEL, D_MODEL),
                  w_up=_shards_to_cols(g[2]), w_down=g[3].reshape(D_FF, D_MODEL), conv=_shards_to_cols(g[5]))
        if kind == 0:
            lw['w_in'] = w_in
        elif kind == 1:
            lw['w_in'] = _b_pad_cols(w_in)
            lw['wdu'] = _pad_to(_shards_to_cols(g[6]), LANES, 0)
            lw['wiu'] = _pad_to(_shards_to_cols(g[7]), LANES, 0)
            lw['wgu'] = _shards_to_cols(g[8])
        else:
            lw['w_in'] = _c_pad_cols(w_in)
            lw['c_conv'] = _shards_to_cols(g[6])
        return lw

    def rwkv_params(j, lw):
        return (row(_b_pad_cols(a['b_mu'][j])), row(a['b_w0'][j]), lw['wdu'], row(a['b_a0'][j]), lw['wiu'], lw['wgu'],
                row(a['b_k_k'][j]), row(a['b_k_a'][j]), e64, e64t)

    def rwkv_post_params(j):
        return (row(a['b_gn_g'][j]), row(a['b_gn_b'][j]), row(a['b_r_k'][j]), e64, e64t)

    def gdn_params(j, lw):
        return (lw['c_conv'], _pad_lanes(a['c_a_log'][j]), _pad_lanes(a['c_dt_bias'][j]), e6, e6t, e12t)

    def gdn_post_params(j):
        return (jnp.tile(row(a['c_norm_g'][j]), (SUBLANES, 1)), e12, e12t, trep)

    x = x0
    saved, layers = [], []
    for l in range(DEPTH):
        kind, j = l % 3, l // 3
        g = gathered[l]
        if l > 0:
            x, g = lax.optimization_barrier((x, g))
        lw = layer_weights(l, g)
        layers.append(lw)
        sv = dict(x=x)
        h = rmsnorm_fwd(x, a['attn_norm'][l], BF16, f"attn_norm_{l}")
        memn = rmsnorm_fwd(mem, a['mem_norm'][l], BF16, f"mem_norm_{l}")
        mem_kv = mm(memn, lw['w_kv'], name=f"mem_kv_{l}")
        p = mm(h, lw['w_in'], name=f"in_proj_{l}")
        if kind == 0:
            y = swa_fwd(p, a['a_sinks'][j], f"swa_{l}")
            q_col = MIX_WIDTH + 2 * SWA_KV_HEADS * SWA_HEAD_DIM
        elif kind == 1:
            pre = rwkv_pre_fwd(p, rwkv_params(j, lw), f"rwkv_pre_{l}")
            yscan, ck = rwkv_scan_fwd(*pre[:6], f"rwkv_scan_{l}")
            post_in = (yscan, pre[0], pre[2], pre[3], pre[6])
            y = rwkv_post_fwd(post_in, rwkv_post_params(j), f"rwkv_post_{l}")
            sv.update(pre=pre, ck=ck, post_in=post_in)
            q_col = RW_SHIFT
        else:
            pre = gdn_pre_fwd(p, gdn_params(j, lw), f"gdn_pre_{l}")
            o, states = gdn_chunk_fwd(*pre, f"gdn_chunk_{l}")
            y = gdn_post_fwd(o, p, gdn_post_params(j), f"gdn_post_{l}")
            sv.update(pre=pre, o=o, states=states)
            q_col = GD_QMEM
        y_mem = mem_attn_fwd(p, q_col, mem_kv, f"mem_attn_{l}")
        cat = jnp.concatenate([y, y_mem], axis=1)
        x1 = mm(cat, lw['w_out'], res=x, name=f"out_proj_{l}")
        hf = rmsnorm_fwd(x1, a['ffn_norm'][l], BF16, f"ffn_norm_{l}")
        u0 = mm(hf, lw['w_up'], name=f"ffn_up_{l}")
        act = ffn_act_fwd(u0, lw['conv'], f"ffn_act_{l}")
        x = mm(act, lw['w_down'], res=x1, name=f"ffn_down_{l}")
        sv.update(h=h, memn=memn, mem_kv=mem_kv, p=p, q_col=q_col, cat=cat, x1=x1, hf=hf, u0=u0, act=act)
        saved.append(sv)

    loss_part, dx, d_final_norm = final_loss(x, a['final_norm'], target, "final_loss")

    rep_grads = {n: [None] * a[n].shape[0] for n in ('attn_norm', 'mem_norm', 'ffn_norm', 'a_sinks')}
    rep_grads['final_norm'] = d_final_norm
    results = {}
    exchanged, pending = {}, {}

    def apply_adam(name, idx, pieces, tag):
        w, m, v = a[name][idx], a['m_' + name][idx], a['v_' + name][idx]
        shp = w.shape
        two_d = (-1, shp[-1])
        out = adamw_sum(pieces.reshape((N_DEV,) + w.reshape(two_d).shape), w.reshape(two_d), m.reshape(two_d),
                        v.reshape(two_d), f"adamw_{name}_{tag}")
        results.setdefault(name, {})[idx] = [o.reshape(shp) for o in out]

    for l in reversed(range(DEPTH)):
        kind, j = l % 3, l // 3
        lw, sv = layers[l], saved[l]
        p, q_col = sv['p'], sv['q_col']
        d_act = mm(dx, lw['w_down'], tb=True, name=f"d_ffn_act_{l}")
        dw_down = mm(sv['act'], dx, ta=True, out_dtype=BF16, name=f"dw_ffn_down_{l}")
        dug, duv, dcg, dcv = ffn_act_bwd(sv['u0'], lw['conv'], d_act, f"ffn_act_bwd_{l}")
        du0 = jnp.concatenate([dug, duv], axis=1)
        d_conv = jnp.concatenate([dcg, dcv], axis=1)
        d_hf = mm(du0, lw['w_up'], tb=True, name=f"d_ffn_norm_out_{l}")
        dw_up = mm(sv['hf'], du0, ta=True, out_dtype=BF16, col_shards=True, name=f"dw_ffn_up_{l}")
        ffn_pieces = [dw_up, dw_down.reshape(N_DEV, -1, D_MODEL)]
        if l == 0:
            *early, token = all_to_all_start(ffn_pieces, "exchange_start_0_ffn")
            d_hf = d_hf + token[0, 0]
            pending['ffn'] = tuple(early)
        dx1, rep_grads['ffn_norm'][l] = rmsnorm_bwd(sv['x1'], a['ffn_norm'][l], d_hf, dx, f"ffn_norm_bwd_{l}")
        dcat = mm(dx1, lw['w_out'], tb=True, name=f"d_cat_{l}")
        dw_out = mm(sv['cat'], dx1, ta=True, out_dtype=BF16, name=f"dw_out_{l}")
        dq_mem, d_mem_kv = mem_attn_bwd(p, q_col, sv['mem_kv'], dcat, f"mem_attn_bwd_{l}")
        small_grads = []
        if kind == 0:
            dq, dk, dv, rep_grads['a_sinks'][j] = swa_bwd(p, a['a_sinks'][j], dcat, f"swa_bwd_{l}")
            dp = jnp.concatenate([dq, dk, dv, dq_mem], axis=1)
        elif kind == 1:
            post = rwkv_post_bwd(sv['post_in'], rwkv_post_params(j), dcat, f"rwkv_post_bwd_{l}")
            scan = rwkv_scan_bwd(*sv['pre'][:6], sv['ck'], post[0], f"rwkv_scan_bwd_{l}")
            res = rwkv_pre_bwd(p, rwkv_params(j, lw), tuple(scan) + tuple(post[1:5]), f"rwkv_pre_bwd_{l}")
            dp_mix = shift_add(res[0], [res[1]], [1], BF16, f"rwkv_shift_bwd_{l}")
            dp = jnp.concatenate([dp_mix, dq_mem], axis=1)
            for n, val in zip(('b_mu', 'b_w0', 'b_a0', 'b_k_k', 'b_k_a'), (_b_unpad_cols(res[2]), res[3], res[5], res[8], res[9])):
                rep_grads[n] = val
            rep_grads.update(b_gn_g=post[5], b_gn_b=post[6], b_r_k=post[7])
            small_grads = [_cols_to_shards(res[4][:RWKV_DECAY_RANK]), _cols_to_shards(res[6][:RWKV_ICLR_RANK]),
                           _cols_to_shards(res[7])]
        else:
            d_o, dz, rep_grads['c_norm_g'] = gdn_post_bwd(sv['o'], p, gdn_post_params(j), dcat, f"gdn_post_bwd_{l}")
            chunk = gdn_chunk_bwd(*sv['pre'], sv['states'], d_o, f"gdn_chunk_bwd_{l}")
            res = gdn_pre_bwd(p, gdn_params(j, lw), chunk, f"gdn_pre_bwd_{l}")
            dqkv = shift_add(res[0], list(res[1:4]), [1, 2, 3], BF16, f"gdn_shift_bwd_{l}")
            dp = jnp.concatenate([dqkv, dz, dq_mem, res[4]], axis=1)
            rep_grads.update(c_a_log=res[6][:, :GDN_V_HEADS], c_dt_bias=res[7][:, :GDN_V_HEADS])
            small_grads = [_cols_to_shards(res[5])]
        d_h = mm(dp, lw['w_in'], tb=True, name=f"d_attn_norm_out_{l}")
        dw_in = mm(sv['h'], dp, ta=True, out_dtype=BF16, name=f"dw_in_{l}")
        dx, rep_grads['attn_norm'][l] = rmsnorm_bwd(sv['x'], a['attn_norm'][l], d_h, dx1, f"attn_norm_bwd_{l}")
        d_memn = mm(d_mem_kv, lw['w_kv'], tb=True, name=f"d_mem_norm_out_{l}")
        dw_kv = mm(sv['memn'], d_mem_kv, ta=True, out_dtype=BF16, name=f"dw_mem_kv_{l}")
        _, rep_grads['mem_norm'][l] = rmsnorm_bwd(mem, a['mem_norm'][l], d_memn, None, f"mem_norm_bwd_{l}")

        if kind == 1:
            dw_in = _b_unpad_cols(dw_in)
        elif kind == 2:
            dw_in = _c_unpad_cols(dw_in)
        pieces = [dw_kv.reshape(N_DEV, -1, 2 * MEM_WIDTH), dw_out.reshape(N_DEV, -1, D_MODEL), _cols_to_shards(dw_in),
                  _cols_to_shards(d_conv)] + small_grads + ([] if l == 0 else ffn_pieces)
        send, recv, thru, lands, token = all_to_all_start(pieces, f"exchange_start_{l}")
        dx = dx + token[0, 0]
        pending[l] = (send, recv, thru, lands)

    for l in reversed(range(1, DEPTH)):
        exchanged[l] = all_to_all_wait(*pending.pop(l), dx, f"exchange_wait_{l}")
    for l in reversed(range(DEPTH)):
        kind, j = l % 3, l // 3
        if l == 0:
            keys = [(n, i) for n in ('w_ffn_up', 'w_ffn_down') for i in range(1, DEPTH)]
            done_above = lax.optimization_barrier(tuple(results[n][i][0] for n, i in keys))
            for (n, i), val in zip(keys, done_above):
                results[n][i][0] = val
            exchanged[0] = all_to_all_wait(*pending.pop(0), done_above[0], "exchange_wait_0")
            exchanged[0] += all_to_all_wait(*pending.pop('ffn'), done_above[0], "exchange_wait_0_ffn")
        got = exchanged[l]
        small = {1: ('b_w_decay_up', 'b_w_iclr_up', 'b_w_gate_up'), 2: ('c_conv',)}.get(kind, ())
        in_name = ('a_w_in', 'b_w_in', 'c_w_in')[kind]
        names = [('w_mem_kv', l), ('w_out', l), (in_name, j), ('ffn_conv', l)] + [(n, j) for n in small]
        names += [('w_ffn_up', l), ('w_ffn_down', l)]
        for (name, idx), pc in zip(names, got, strict=True):
            apply_adam(name, idx, pc, l)

    rep_vals = []
    for n in REPLICATED:
        gval = rep_grads[n]
        gval = jnp.stack(gval) if isinstance(gval, list) else gval
        rep_vals.append(gval.reshape(a[n].shape))
    shapes = [a[n].shape for n in REPLICATED] + [(1,)]
    part = _pack(rep_vals + [loss_part.reshape(1)])
    gathered = all_gather_many([part], "gather_small_grads")[0]
    zero = jnp.zeros((1,), F32)
    packed = lambda pre: _pack([a[pre + n] for n in REPLICATED] + [zero])
    rep_out = adamw_sum(gathered, packed(''), packed('m_'), packed('v_'), "adamw_replicated")
    rep_out = [_unpack(o, shapes) for o in rep_out]
    loss = rep_out[0][-1][0]
    for i, n in enumerate(REPLICATED):
        results[n] = [o[i] for o in rep_out]

    def leaf(name, which):
        r = results[name]
        if isinstance(r, dict):
            return jnp.stack([r[i][which] for i in range(len(r))])
        return r[which]

    outs = [loss, dx[None]]
    for which in range(4):
        outs += [leaf(n, which) for n in WEIGHTS]
    return tuple(outs)
```

```python
import functools

import numpy as np
import jax
import jax.numpy as jnp
from jax import lax
from jax.experimental import pallas as pl
from jax.experimental.pallas import tpu as pltpu
from jax.experimental.pallas import tpu_sc as plsc

F32, BF16 = jnp.float32, jnp.bfloat16
HI = lax.Precision.HIGHEST
V7X_VMEM_BYTES = 64 * 1024 * 1024
VMEM_LIMIT = V7X_VMEM_BYTES - 8 * 1024 * 1024
MM_TILE_BUDGET = 40 * 1024 * 1024
SUBLANES, LANES = 8, 128
N_DEV = 8

D_MODEL = 2048
DEPTH = 4
MIX_WIDTH = 1536
MEM_HEADS, MEM_HEAD_DIM, MEM_WIDTH = 4, 128, 512
NORM_EPS = 1e-6
SWA_HEAD_DIM, SWA_Q_HEADS, SWA_KV_HEADS, SWA_GROUP, SWA_BLOCK = 64, 24, 4, 6, 128
RWKV_HEADS, RWKV_HEAD_DIM, RWKV_GN_EPS = 24, 64, 64e-5
RWKV_DECAY_RANK, RWKV_ICLR_RANK, RWKV_GATE_RANK = 96, 96, 256
GDN_HEAD_DIM, GDN_V_HEADS, GDN_QK_HEADS, GDN_CONV, GDN_CHUNK = 128, 12, 6, 4, 64
GDN_QK_WIDTH = GDN_QK_HEADS * GDN_HEAD_DIM
GDN_CONV_WIDTH = 2 * GDN_QK_WIDTH + MIX_WIDTH
D_FF, FFN_CONV = 5632, 3
ADAM_LR, ADAM_B1, ADAM_B2, ADAM_EPS, ADAM_WD, ADAM_STEP = 0.001, 0.9, 0.999, 1e-08, 0.01, 10
MESH = pl.DeviceIdType.MESH


def _cp(sem=None):
    return pltpu.CompilerParams(dimension_semantics=sem, vmem_limit_bytes=VMEM_LIMIT)


def _tile(n, cands):
    for c in cands:
        if n % c == 0:
            return c
    return n


_DIMS = {'nn': (((1,), (0,)), ((), ())), 'nt': (((1,), (1,)), ((), ())), 'tn': (((0,), (0,)), ((), ()))}


def _split(x):
    hi = lax.bitcast_convert_type(lax.bitcast_convert_type(x, jnp.uint32) & jnp.uint32(0xFFFF0000), F32)
    return hi.astype(BF16), (x - hi).astype(BF16)


def _dot3_raw(a, b, form):
    one = lambda p, q: lax.dot_general(p, q, _DIMS[form], preferred_element_type=F32)
    (a_hi, a_lo), (b_hi, b_lo) = _split(a), _split(b)
    return one(a_hi, b_hi) + (one(a_hi, b_lo) + one(a_lo, b_hi))


@functools.partial(jax.custom_vjp, nondiff_argnums=(2,))
def _dot3(a, b, form):
    return _dot3_raw(a, b, form)


def _dot3_fwd(a, b, form):
    return _dot3_raw(a, b, form), (a, b)


def _dot3_bwd(form, saved, dc):
    a, b = saved
    if form == 'nn':
        return _dot3(dc, b, 'nt'), _dot3(a, dc, 'tn')
    if form == 'nt':
        return _dot3(dc, b, 'nn'), _dot3(dc, a, 'tn')
    return _dot3(b, dc, 'nt'), _dot3(a, dc, 'nn')


_dot3.defvjp(_dot3_fwd, _dot3_bwd)


def _dot(a, b):
    return _dot3(a, b, 'nn')


def _dot_nt(a, b):
    return _dot3(a, b, 'nt')


def _dot_tn(a, b):
    return _dot3(a, b, 'tn')


def _sigmoid(x):
    return 1.0 / (1.0 + jnp.exp(-x))


def _softplus(x):
    return jnp.maximum(x, 0.0) + jnp.log(1.0 + jnp.exp(-jnp.abs(x)))


def _silu(x):
    return x * _sigmoid(x)


def mm(a, b, *, ta=False, tb=False, res=None, out_dtype=F32, col_shards=False, name):
    (k_a, m) = a.shape if ta else a.shape[::-1]
    (k_b, n) = b.shape[::-1] if tb else b.shape
    assert k_a == k_b, (a.shape, b.shape, ta, tb)
    kdim = k_a
    tm = _tile(m, (1024, 512, 256))
    tn = _tile(n, (1024, 768, 512, 384, 256, 128))
    if col_shards:
        tn = n // N_DEV
        assert res is None and not tb and tn % LANES == 0

    def vmem_bytes(tk):
        tiles = tk * (tm * a.dtype.itemsize + tn * b.dtype.itemsize) + tm * tn * jnp.dtype(out_dtype).itemsize
        tiles += 0 if res is None else tm * tn * res.dtype.itemsize
        return 2 * tiles + (0 if tk == kdim else tm * tn * 4)

    tks = [t for t in (kdim, kdim // 2, kdim // 4, 1024, 512, 256, 128) if kdim % t == 0 and t % LANES == 0]
    tk = next(t for t in tks if vmem_bytes(t) <= MM_TILE_BUDGET)
    nk = kdim // tk
    dims = (((0 if ta else 1,), (1 if tb else 0,)), ((), ()))

    def body(*refs):
        a_ref, b_ref = refs[:2]
        r_ref = None if res is None else refs[2]
        o_ref = refs[2 if res is None else 3]
        part = lax.dot_general(a_ref[...].astype(BF16), b_ref[...].astype(BF16), dims, preferred_element_type=F32)

        def finish(total):
            o_ref[...] = (total if res is None else total + r_ref[...]).astype(o_ref.dtype)

        if nk == 1:
            finish(part)
            return
        acc, kk = refs[-1], pl.program_id(2)

        @pl.when(kk == 0)
        def _():
            acc[...] = part

        @pl.when((kk > 0) & (kk < nk - 1))
        def _():
            acc[...] += part

        @pl.when(kk == nk - 1)
        def _():
            finish(acc[...] + part)

    a_spec = pl.BlockSpec((tk, tm), lambda i, j, k: (k, i)) if ta else pl.BlockSpec((tm, tk), lambda i, j, k: (i, k))
    b_spec = pl.BlockSpec((tn, tk), lambda i, j, k: (j, k)) if tb else pl.BlockSpec((tk, tn), lambda i, j, k: (k, j))
    o_spec = pl.BlockSpec((tm, tn), lambda i, j, k: (i, j))
    o_shape = (m, n)
    if col_shards:
        o_spec = pl.BlockSpec((None, tm, tn), lambda i, j, k: (j, i, 0))
        o_shape = (N_DEV, m, tn)
    in_specs, args = [a_spec, b_spec], [a, b]
    if res is not None:
        in_specs.append(o_spec)
        args.append(res)
    return pl.pallas_call(
        body, grid=(m // tm, n // tn, nk), in_specs=in_specs, out_specs=o_spec,
        out_shape=jax.ShapeDtypeStruct(o_shape, out_dtype),
        scratch_shapes=[] if nk == 1 else [pltpu.VMEM((tm, tn), F32)],
        compiler_params=_cp(("parallel", "parallel", "arbitrary")), name=name)(*args)


def _coords():
    return lax.axis_index("x"), lax.axis_index("y"), lax.axis_index("c")


def _block_index(p):
    return 4 * p[0] + 2 * p[1] + p[2]


def _all_gather_body(x_refs, o_refs, send, recv, loc):
    n = len(x_refs)
    x, y, c = _coords()
    me, sib = (x, y, c), (x, y, 1 - c)
    chips = [(1 - x, y), (x, 1 - y), (1 - x, 1 - y)]

    def cp(i, k, block, to, src=None):
        dst = o_refs[i].at[_block_index(block)]
        return pltpu.make_async_remote_copy(
            src_ref=dst if src is None else src, dst_ref=dst, send_sem=send.at[i, k], recv_sem=recv.at[i, k],
            device_id=to, device_id_type=MESH)

    mine = [pltpu.make_async_copy(x_refs[i], o_refs[i].at[_block_index(me)], loc.at[i]) for i in range(n)]
    for m_ in mine:
        m_.start()
    first = []
    for i in range(n):
        first.append(cp(i, 0, me, sib, src=x_refs[i]))
        for j, chip in enumerate(chips):
            first.append(cp(i, 1 + j, me, (*chip, c), src=x_refs[i]))
    for f in first:
        f.start()
    passed = []
    for j, chip in enumerate(chips):
        for i in range(n):
            cp(i, 1 + j, (*chip, c), me).wait_recv()
            p = cp(i, 4 + j, (*chip, c), sib)
            p.start()
            passed.append(p)
    for i in range(n):
        cp(i, 0, sib, me).wait_recv()
        for j, chip in enumerate(chips):
            cp(i, 4 + j, (*chip, 1 - c), me).wait_recv()
    for f in first + passed:
        f.wait_send()
    for m_ in mine:
        m_.wait()


def _all_gather_peers():
    x, y, c = _coords()
    return [(x, y, 1 - c), (1 - x, y, c), (x, 1 - y, c), (1 - x, 1 - y, c)]


def _all_to_all_peers():
    x, y, c = _coords()
    return [(1 - x if r & 4 else x, 1 - y if r & 2 else y, 1 - c if r & 1 else c) for r in range(1, N_DEV)]


def _comm_scratch(n):
    return [pltpu.SemaphoreType.DMA((n, 7)), pltpu.SemaphoreType.DMA((n, 7)), pltpu.SemaphoreType.DMA((n,))]


def all_gather_many(xs, name):
    n = len(xs)

    def body(*refs):
        _all_gather_body(refs[:n], refs[n:2 * n], *refs[2 * n:])

    any_spec = pl.BlockSpec(memory_space=pl.ANY)
    return pl.pallas_call(
        body, in_specs=[any_spec] * n, out_specs=[any_spec] * n,
        out_shape=[jax.ShapeDtypeStruct((N_DEV,) + x.shape, x.dtype) for x in xs],
        scratch_shapes=_comm_scratch(n), name=name)(*xs)


def _on_sequencer(exchange, peers, xs, out_shapes, name, collective_id):
    x_refs = [jax.new_ref(x, memory_space=pltpu.MemorySpace.HBM) for x in xs]
    o_refs = [jax.empty_ref(s, memory_space=pltpu.MemorySpace.HBM) for s in out_shapes]

    @pl.kernel(mesh=plsc.ScalarSubcoreMesh(axis_name="sequencer", num_cores=1), name=name,
               scratch_types=tuple(_comm_scratch(len(xs))),
               compiler_params=pltpu.CompilerParams(collective_id=collective_id))
    def launch(send, recv, loc):
        barrier = pltpu.get_barrier_semaphore()
        ids = peers()
        for peer in ids:
            pl.semaphore_signal(barrier, inc=1, device_id=peer, device_id_type=MESH)
        pl.semaphore_wait(barrier, len(ids))
        exchange(x_refs, o_refs, send, recv, loc)

    launch()
    return [o[...] for o in o_refs]


def _all_to_all_copies(x_refs, o_refs, send, recv, arrivals):
    x, y, c = _coords()
    me = _block_index((x, y, c))
    copies = []
    for r, peer in enumerate(_all_to_all_peers()):
        pidx = _block_index(peer)
        for i in range(len(x_refs)):
            copies.append(pltpu.make_async_remote_copy(
                src_ref=x_refs[i].at[pidx], dst_ref=o_refs[i].at[pidx if arrivals else me],
                send_sem=send.at[7 * i + r], recv_sem=recv.at[7 * i + r], device_id=peer, device_id_type=MESH))
    return copies


def all_to_all_start(xs, name):
    n = len(xs)
    x, y, c = _coords()
    me = _block_index((x, y, c))
    lands = [lax.dynamic_update_slice_in_dim(lax.empty(v.shape, v.dtype), lax.dynamic_slice_in_dim(v, me, 1, 0), me, 0)
             for v in xs]

    def body(*refs):
        x_refs, o_refs = refs[:n], refs[n:2 * n]
        send, recv = refs[2 * n], refs[2 * n + 1]
        token = refs[-1]
        for s in _all_to_all_copies(x_refs, o_refs, send, recv, arrivals=False):
            s.start()
        token[...] = jnp.zeros_like(token)

    hbm = pl.BlockSpec(memory_space=pltpu.HBM)
    sem = pl.BlockSpec(memory_space=pltpu.SEMAPHORE)
    out = pl.pallas_call(
        body, name=name,
        out_shape=(pltpu.SemaphoreType.DMA((7 * n,)), pltpu.SemaphoreType.DMA((7 * n,)),
                   *[pltpu.HBM(v.shape, v.dtype) for v in xs], *[pltpu.HBM(v.shape, v.dtype) for v in xs],
                   jax.ShapeDtypeStruct((SUBLANES, LANES), F32)),
        in_specs=[hbm] * (2 * n), out_specs=(sem, sem, *([hbm] * (2 * n)), pl.BlockSpec(memory_space=pltpu.VMEM)),
        input_output_aliases={i: 2 + i for i in range(2 * n)},
        compiler_params=pltpu.CompilerParams(has_side_effects=pltpu.SideEffectType.DATAFLOW_SIDE_EFFECTING),
    )(*[pltpu.with_memory_space_constraint(v, pltpu.HBM) for v in xs],
      *[pltpu.with_memory_space_constraint(v, pltpu.HBM) for v in lands])
    return out[0], out[1], list(out[2:2 + n]), list(out[2 + n:2 + 2 * n]), out[-1]


def all_to_all_wait(send, recv, xs, lands, after, name):
    n = len(xs)

    def body(*refs):
        x_refs, o_refs = refs[:n], refs[n:2 * n]
        send_ref, recv_ref = refs[2 * n], refs[2 * n + 1]
        for s in _all_to_all_copies(x_refs, o_refs, send_ref, recv_ref, arrivals=False):
            s.wait_send()
        for w in _all_to_all_copies(x_refs, o_refs, send_ref, recv_ref, arrivals=True):
            w.wait_recv()

    hbm = pl.BlockSpec(memory_space=pltpu.HBM)
    sem = pl.BlockSpec(memory_space=pltpu.SEMAPHORE)
    out = pl.pallas_call(
        body, name=name,
        out_shape=tuple(pltpu.HBM(v.shape, v.dtype) for v in list(xs) + list(lands)),
        in_specs=[hbm] * (2 * n) + [sem, sem, pl.BlockSpec(memory_space=pl.ANY)], out_specs=tuple([hbm] * (2 * n)),
        input_output_aliases={i: i for i in range(2 * n)},
        compiler_params=pltpu.CompilerParams(has_side_effects=pltpu.SideEffectType.DATAFLOW_SIDE_EFFECTING),
    )(*xs, *lands, send, recv, after)
    return list(out[n:])


def all_gather_many_async(xs, name, collective_id):
    shapes = [jax.ShapeDtypeStruct((N_DEV,) + x.shape, x.dtype) for x in xs]
    return _on_sequencer(_all_gather_body, _all_gather_peers, xs, shapes, name, collective_id)


def adamw_sum(pieces, w, m, v, name):
    rows, cols = w.shape
    tr = _tile(rows, (128, 64, 32, 16, 8))
    c1 = 1.0 - ADAM_B1 ** ADAM_STEP
    c2 = 1.0 - ADAM_B2 ** ADAM_STEP

    def body(p_ref, w_ref, m_ref, v_ref, g_out, d_out, m_out, v_out):
        g = p_ref[0].astype(F32)
        for s in range(1, N_DEV):
            g = g + p_ref[s].astype(F32)
        m_new = ADAM_B1 * m_ref[...] + (1.0 - ADAM_B1) * g
        v_new = ADAM_B2 * v_ref[...] + (1.0 - ADAM_B2) * (g * g)
        m_hat = m_new / c1
        v_hat = v_new / c2
        g_out[...] = g
        d_out[...] = -ADAM_LR * (m_hat / (jnp.sqrt(v_hat) + ADAM_EPS) + ADAM_WD * w_ref[...])
        m_out[...] = m_new
        v_out[...] = v_new

    spec = pl.BlockSpec((tr, cols), lambda i: (i, 0))
    out = jax.ShapeDtypeStruct((rows, cols), F32)
    return pl.pallas_call(
        body, grid=(rows // tr,), in_specs=[pl.BlockSpec((N_DEV, tr, cols), lambda i: (0, i, 0)), spec, spec, spec],
        out_specs=[spec] * 4, out_shape=[out] * 4, compiler_params=_cp(("parallel",)), name=name)(pieces, w, m, v)


def rmsnorm_fwd(x, g, out_dtype, name):
    s, d = x.shape
    tr = _tile(s, (256, 128, 64, 32, 16))

    def body(x_ref, g_ref, o_ref):
        xv = x_ref[...]
        rstd = lax.rsqrt(jnp.mean(xv * xv, axis=-1, keepdims=True) + NORM_EPS)
        o_ref[...] = (xv * rstd * g_ref[...]).astype(o_ref.dtype)

    return pl.pallas_call(
        body, grid=(s // tr,), in_specs=[pl.BlockSpec((tr, d), lambda i: (i, 0)), pl.BlockSpec((1, d), lambda i: (0, 0))],
        out_specs=pl.BlockSpec((tr, d), lambda i: (i, 0)), out_shape=jax.ShapeDtypeStruct((s, d), out_dtype),
        compiler_params=_cp(("parallel",)), name=name)(x, g.reshape(1, d))


def rmsnorm_bwd(x, g, dh, dres, name):
    s, d = x.shape
    tr = _tile(s, (256, 128, 64, 32, 16))

    def body(*refs):
        if dres is None:
            x_ref, g_ref, dh_ref, dx_ref, dg_ref = refs
        else:
            x_ref, g_ref, dh_ref, dr_ref, dx_ref, dg_ref = refs
        xv = x_ref[...]
        rstd = lax.rsqrt(jnp.mean(xv * xv, axis=-1, keepdims=True) + NORM_EPS)
        xhat = xv * rstd
        dhv = dh_ref[...].astype(F32)
        dhg = dhv * g_ref[...]
        dx = rstd * (dhg - xhat * jnp.mean(dhg * xhat, axis=-1, keepdims=True))
        if dres is not None:
            dx = dx + dr_ref[...]
        dx_ref[...] = dx

        @pl.when(pl.program_id(0) == 0)
        def _():
            dg_ref[...] = jnp.zeros_like(dg_ref)

        dg_ref[...] += jnp.sum(dhv * xhat, axis=0, keepdims=True)

    row = pl.BlockSpec((tr, d), lambda i: (i, 0))
    vec = pl.BlockSpec((1, d), lambda i: (0, 0))
    ins = [x, g.reshape(1, d), dh] + ([] if dres is None else [dres])
    dx, dg = pl.pallas_call(
        body, grid=(s // tr,), in_specs=[row, vec, row] + ([] if dres is None else [row]), out_specs=[row, vec],
        out_shape=[jax.ShapeDtypeStruct((s, d), F32), jax.ShapeDtypeStruct((1, d), F32)],
        compiler_params=_cp(("arbitrary",)), name=name)(*ins)
    return dx, dg.reshape(d)


def final_loss(x, g, target, name):
    s, d = x.shape
    tr = _tile(s, (256, 128, 64, 32, 16))

    def body(x_ref, g_ref, t_ref, l_ref, dx_ref, dg_ref):
        xv = x_ref[...]
        rstd = lax.rsqrt(jnp.mean(xv * xv, axis=-1, keepdims=True) + NORM_EPS)
        xhat = xv * rstd
        err = xhat * g_ref[...] - t_ref[...]
        dy = err * (1.0 / d)
        dhg = dy * g_ref[...]
        dx_ref[...] = rstd * (dhg - xhat * jnp.mean(dhg * xhat, axis=-1, keepdims=True))

        @pl.when(pl.program_id(0) == 0)
        def _():
            dg_ref[...] = jnp.zeros_like(dg_ref)
            l_ref[...] = jnp.zeros_like(l_ref)

        dg_ref[...] += jnp.sum(dy * xhat, axis=0, keepdims=True)
        part = 0.5 * jnp.sum(jnp.mean(err * err, axis=-1, keepdims=True), axis=0, keepdims=True)
        l_ref[...] += jnp.broadcast_to(part, l_ref.shape)

    row = pl.BlockSpec((tr, d), lambda i: (i, 0))
    vec = pl.BlockSpec((1, d), lambda i: (0, 0))
    lspec = pl.BlockSpec((1, LANES), lambda i: (0, 0))
    loss, dx, dg = pl.pallas_call(
        body, grid=(s // tr,), in_specs=[row, vec, row], out_specs=[lspec, row, vec],
        out_shape=[jax.ShapeDtypeStruct((1, LANES), F32), jax.ShapeDtypeStruct((s, d), F32),
                   jax.ShapeDtypeStruct((1, d), F32)],
        compiler_params=_cp(("arbitrary",)), name=name)(x, g.reshape(1, d), target)
    return loss[0, 0], dx, dg.reshape(d)


def _shift_down(tile, prev8, j):
    if j == 0:
        return tile
    rt = pltpu.roll(tile, j, 0)
    rp = pltpu.roll(prev8, j, 0)
    rows = lax.broadcasted_iota(jnp.int32, prev8.shape, 0)
    top = jnp.where(rows < j, rp, rt[:SUBLANES])
    return jnp.concatenate([top, rt[SUBLANES:]], axis=0)


def _shift_up(tile, next8, j):
    if j == 0:
        return tile
    t = tile.shape[0]
    rt = pltpu.roll(tile, t - j, 0)
    rn = pltpu.roll(next8, SUBLANES - j, 0)
    rows = lax.broadcasted_iota(jnp.int32, next8.shape, 0)
    bot = jnp.where(rows >= SUBLANES - j, rn, rt[t - SUBLANES:])
    return jnp.concatenate([rt[:t - SUBLANES], bot], axis=0)


def _halo_specs(t_rows, s_rows, cols, col_of):
    per, last = t_rows // SUBLANES, s_rows // SUBLANES - 1
    prev = pl.BlockSpec((SUBLANES, cols), lambda j, i: (jnp.maximum(i * per - 1, 0), col_of(j)))
    nxt = pl.BlockSpec((SUBLANES, cols), lambda j, i: (jnp.minimum((i + 1) * per, last), col_of(j)))
    return prev, nxt


def ffn_act_fwd(u0, conv, name):
    s, two_f = u0.shape
    f = two_f // 2
    t, c = _tile(s, (256, 128, 64)), 512
    nc = f // c

    def body(g_ref, v_ref, gp_ref, vp_ref, wg_ref, wv_ref, a_ref):
        first = (pl.program_id(1) > 0).astype(F32)

        def conv_of(x_ref, p_ref, w_ref):
            x, p = x_ref[...], p_ref[...] * first
            return (w_ref[0:1, :] * _shift_down(x, p, 2) + w_ref[1:2, :] * _shift_down(x, p, 1) + w_ref[2:3, :] * x)

        ug = conv_of(g_ref, gp_ref, wg_ref)
        uv = conv_of(v_ref, vp_ref, wv_ref)
        a_ref[...] = (_silu(ug) * uv).astype(a_ref.dtype)

    gate = pl.BlockSpec((t, c), lambda j, i: (i, j))
    val = pl.BlockSpec((t, c), lambda j, i: (i, j + nc))
    gp, _ = _halo_specs(t, s, c, lambda j: j)
    vp, _ = _halo_specs(t, s, c, lambda j: j + nc)
    wg = pl.BlockSpec((FFN_CONV, c), lambda j, i: (0, j))
    wv = pl.BlockSpec((FFN_CONV, c), lambda j, i: (0, j + nc))
    return pl.pallas_call(
        body, grid=(nc, s // t), in_specs=[gate, val, gp, vp, wg, wv], out_specs=pl.BlockSpec((t, c), lambda j, i: (i, j)),
        out_shape=jax.ShapeDtypeStruct((s, f), BF16), compiler_params=_cp(("parallel", "parallel")),
        name=name)(u0, u0, u0, u0, conv, conv)


def ffn_act_bwd(u0, conv, da, name):
    s, two_f = u0.shape
    f = two_f // 2
    t, c = _tile(s, (256, 128, 64)), 512
    nc, nt = f // c, s // t

    def body(g_ref, v_ref, gp_ref, vp_ref, gn_ref, vn_ref, wg_ref, wv_ref, da_ref, dan_ref,
             dg_ref, dv_ref, dwg_ref, dwv_ref):
        i = pl.program_id(1)
        first, last = (i > 0).astype(F32), (i < nt - 1).astype(F32)
        zeros8 = jnp.zeros((SUBLANES, c), F32)

        def ext(x_ref, p_ref, n_ref):
            return jnp.concatenate([p_ref[...] * first, x_ref[...], n_ref[...] * last], axis=0)

        def taps(e):
            return pltpu.roll(e, 2, 0), pltpu.roll(e, 1, 0), e

        def conv_of(sh, w_ref):
            return w_ref[0:1, :] * sh[0] + w_ref[1:2, :] * sh[1] + w_ref[2:3, :] * sh[2]

        def conv_t(du, w_ref):
            n = du.shape[0]
            return w_ref[2:3, :] * du + w_ref[1:2, :] * pltpu.roll(du, n - 1, 0) + w_ref[0:1, :] * pltpu.roll(du, n - 2, 0)

        sg, sv = taps(ext(g_ref, gp_ref, gn_ref)), taps(ext(v_ref, vp_ref, vn_ref))
        ug, uv = conv_of(sg, wg_ref), conv_of(sv, wv_ref)
        dae = jnp.concatenate([zeros8, da_ref[...], dan_ref[...] * last], axis=0)
        sig = _sigmoid(ug)
        dug = dae * uv * (sig * (1.0 + ug * (1.0 - sig)))
        duv = dae * (ug * sig)
        dg_ref[...] = conv_t(dug, wg_ref)[SUBLANES:t + SUBLANES].astype(dg_ref.dtype)
        dv_ref[...] = conv_t(duv, wv_ref)[SUBLANES:t + SUBLANES].astype(dv_ref.dtype)

        @pl.when(i == 0)
        def _():
            dwg_ref[...] = jnp.zeros_like(dwg_ref)
            dwv_ref[...] = jnp.zeros_like(dwv_ref)

        def dconv(du, sh):
            d = du[SUBLANES:t + SUBLANES]
            return jnp.concatenate([jnp.sum(d * x[SUBLANES:t + SUBLANES], axis=0, keepdims=True) for x in sh], axis=0)

        dwg_ref[...] += dconv(dug, sg)
        dwv_ref[...] += dconv(duv, sv)

    gate = pl.BlockSpec((t, c), lambda j, i: (i, j))
    val = pl.BlockSpec((t, c), lambda j, i: (i, j + nc))
    gp, gn = _halo_specs(t, s, c, lambda j: j)
    vp, vn = _halo_specs(t, s, c, lambda j: j + nc)
    wg = pl.BlockSpec((FFN_CONV, c), lambda j, i: (0, j))
    wv = pl.BlockSpec((FFN_CONV, c), lambda j, i: (0, j + nc))
    wout = pl.BlockSpec((FFN_CONV, c), lambda j, i: (0, j))
    half = jax.ShapeDtypeStruct((s, f), BF16)
    dwh = jax.ShapeDtypeStruct((FFN_CONV, f), F32)
    return pl.pallas_call(
        body, grid=(nc, nt), in_specs=[gate, val, gp, vp, gn, vn, wg, wv, gate, gn],
        out_specs=[gate, gate, wout, wout], out_shape=[half, half, dwh, dwh],
        compiler_params=_cp(("parallel", "arbitrary")), name=name)(u0, u0, u0, u0, u0, u0, conv, conv, da, da)


def _softmax_rows(s, extra=None):
    m = jnp.max(s, axis=-1, keepdims=True)
    if extra is not None:
        m = jnp.maximum(m, extra)
    m = lax.stop_gradient(m)
    e = jnp.exp(s - m)
    den = jnp.sum(e, axis=-1, keepdims=True)
    if extra is not None:
        den = den + jnp.exp(extra - m)
    return e / den


def _mem_attn_fn(qs, ks, vs):
    outs = []
    for q, k, v in zip(qs, ks, vs):
        p = _softmax_rows(_dot_nt(q, k) * (MEM_HEAD_DIM ** -0.5))
        outs.append(_dot(p, v))
    return outs


def _mem_heads(q_ref, kv_ref):
    d = MEM_HEAD_DIM
    qs = [q_ref[:, h * d:(h + 1) * d] for h in range(MEM_HEADS)]
    ks = [kv_ref[:, h * d:(h + 1) * d] for h in range(MEM_HEADS)]
    vs = [kv_ref[:, MEM_WIDTH + h * d:MEM_WIDTH + (h + 1) * d] for h in range(MEM_HEADS)]
    return qs, ks, vs


def mem_attn_fwd(p, q_col, kv, name):
    s = p.shape[0]
    t = _tile(s, (256, 128))
    m = kv.shape[0]

    def body(q_ref, kv_ref, o_ref):
        outs = _mem_attn_fn(*_mem_heads(q_ref, kv_ref))
        o_ref[...] = jnp.concatenate(outs, axis=1).astype(o_ref.dtype)

    return pl.pallas_call(
        body, grid=(s // t,),
        in_specs=[pl.BlockSpec((t, MEM_WIDTH), lambda i: (i, q_col // MEM_WIDTH)),
                  pl.BlockSpec((m, 2 * MEM_WIDTH), lambda i: (0, 0))],
        out_specs=pl.BlockSpec((t, MEM_WIDTH), lambda i: (i, 0)), out_shape=jax.ShapeDtypeStruct((s, MEM_WIDTH), BF16),
        compiler_params=_cp(("parallel",)), name=name)(p, kv)


def mem_attn_bwd(p, q_col, kv, dcat, name):
    s = p.shape[0]
    t = _tile(s, (256, 128))
    m = kv.shape[0]
    d = MEM_HEAD_DIM

    def body(q_ref, kv_ref, dy_ref, dq_ref, dkv_ref):
        qs, ks, vs = _mem_heads(q_ref, kv_ref)
        _, vjp = jax.vjp(_mem_attn_fn, qs, ks, vs)
        dqs, dks, dvs = vjp([dy_ref[:, h * d:(h + 1) * d] for h in range(MEM_HEADS)])
        dq_ref[...] = jnp.concatenate(dqs, axis=1).astype(dq_ref.dtype)

        @pl.when(pl.program_id(0) == 0)
        def _():
            dkv_ref[...] = jnp.zeros_like(dkv_ref)

        dkv_ref[...] += jnp.concatenate(dks + dvs, axis=1)

    return pl.pallas_call(
        body, grid=(s // t,),
        in_specs=[pl.BlockSpec((t, MEM_WIDTH), lambda i: (i, q_col // MEM_WIDTH)),
                  pl.BlockSpec((m, 2 * MEM_WIDTH), lambda i: (0, 0)),
                  pl.BlockSpec((t, MEM_WIDTH), lambda i: (i, MIX_WIDTH // MEM_WIDTH))],
        out_specs=[pl.BlockSpec((t, MEM_WIDTH), lambda i: (i, 0)), pl.BlockSpec((m, 2 * MEM_WIDTH), lambda i: (0, 0))],
        out_shape=[jax.ShapeDtypeStruct((s, MEM_WIDTH), BF16), jax.ShapeDtypeStruct((m, 2 * MEM_WIDTH), F32)],
        compiler_params=_cp(("arbitrary",)), name=name)(p, kv, dcat)


def _swa_fn(qs, kcs, kps, vcs, vps, sinks, not_first):
    t = SWA_BLOCK
    qi = lax.broadcasted_iota(jnp.int32, (t, 2 * t), 0)
    kj = lax.broadcasted_iota(jnp.int32, (t, 2 * t), 1)
    dist = t + qi - kj
    valid = (dist >= 0) & (dist < t) & ((kj >= t) | not_first)
    distf = dist.astype(F32)
    outs = []
    for kh in range(SWA_KV_HEADS):
        kb = jnp.concatenate([kps[kh], kcs[kh]], axis=0)
        vb = jnp.concatenate([vps[kh], vcs[kh]], axis=0)
        for g in range(SWA_GROUP):
            h = kh * SWA_GROUP + g
            slope = 2.0 ** (-8.0 * (h + 1) / SWA_Q_HEADS)
            sc = _dot_nt(qs[h], kb) * (SWA_HEAD_DIM ** -0.5) - slope * distf
            sc = jnp.where(valid, sc, -jnp.inf)
            outs.append(_dot(_softmax_rows(sc, extra=sinks[h]), vb))
    return outs


def _swa_args(q_ref, kc_ref, kp_ref, vc_ref, vp_ref, sink_ref):
    d = SWA_HEAD_DIM
    qs = [q_ref[:, h * d:(h + 1) * d] for h in range(SWA_Q_HEADS)]
    per_kv = lambda ref: [ref[:, h * d:(h + 1) * d] for h in range(SWA_KV_HEADS)]
    sinks = [sink_ref[0:1, h:h + 1] for h in range(SWA_Q_HEADS)]
    return qs, per_kv(kc_ref), per_kv(kp_ref), per_kv(vc_ref), per_kv(vp_ref), sinks


def _swa_specs(nb, order):
    t, kvw = SWA_BLOCK, SWA_KV_HEADS * SWA_HEAD_DIM
    k_col, v_col = MIX_WIDTH // kvw, MIX_WIDTH // kvw + 1
    q = pl.BlockSpec((t, MIX_WIDTH), lambda n: (order(n), 0))
    kc = pl.BlockSpec((t, kvw), lambda n: (order(n), k_col))
    kp = pl.BlockSpec((t, kvw), lambda n: (jnp.maximum(order(n) - 1, 0), k_col))
    vc = pl.BlockSpec((t, kvw), lambda n: (order(n), v_col))
    vp = pl.BlockSpec((t, kvw), lambda n: (jnp.maximum(order(n) - 1, 0), v_col))
    sink = pl.BlockSpec((1, LANES), lambda n: (0, 0))
    return [q, kc, kp, vc, vp, sink]


def _pad_lanes(v):
    return jnp.pad(v.reshape(1, -1), ((0, 0), (0, LANES - v.size)))


def swa_fwd(p, sinks, name):
    s = p.shape[0]
    nb = s // SWA_BLOCK

    def body(q_ref, kc_ref, kp_ref, vc_ref, vp_ref, sink_ref, o_ref):
        outs = _swa_fn(*_swa_args(q_ref, kc_ref, kp_ref, vc_ref, vp_ref, sink_ref), pl.program_id(0) > 0)
        o_ref[...] = jnp.concatenate(outs, axis=1).astype(o_ref.dtype)

    return pl.pallas_call(
        body, grid=(nb,), in_specs=_swa_specs(nb, lambda n: n),
        out_specs=pl.BlockSpec((SWA_BLOCK, MIX_WIDTH), lambda n: (n, 0)),
        out_shape=jax.ShapeDtypeStruct((s, MIX_WIDTH), BF16), compiler_params=_cp(("parallel",)),
        name=name)(p, p, p, p, p, _pad_lanes(sinks))


def swa_bwd(p, sinks, dcat, name):
    s = p.shape[0]
    nb = s // SWA_BLOCK
    t, d, kvw = SWA_BLOCK, SWA_HEAD_DIM, SWA_KV_HEADS * SWA_HEAD_DIM
    rev = lambda n: nb - 1 - n

    def body(q_ref, kc_ref, kp_ref, vc_ref, vp_ref, sink_ref, dy_ref, dq_ref, dk_ref, dv_ref, ds_ref, ck, cv):
        n = pl.program_id(0)

        @pl.when(n == 0)
        def _():
            ck[...] = jnp.zeros_like(ck)
            cv[...] = jnp.zeros_like(cv)
            ds_ref[...] = jnp.zeros_like(ds_ref)

        args = _swa_args(q_ref, kc_ref, kp_ref, vc_ref, vp_ref, sink_ref)
        _, vjp = jax.vjp(functools.partial(_swa_fn, not_first=rev(n) > 0), *args)
        dqs, dkcs, dkps, dvcs, dvps, dsinks = vjp([dy_ref[:, h * d:(h + 1) * d] for h in range(SWA_Q_HEADS)])
        dq_ref[...] = jnp.concatenate(dqs, axis=1).astype(dq_ref.dtype)
        dk_ref[...] = (jnp.concatenate(dkcs, axis=1) + ck[...]).astype(dk_ref.dtype)
        dv_ref[...] = (jnp.concatenate(dvcs, axis=1) + cv[...]).astype(dv_ref.dtype)
        ck[...] = jnp.concatenate(dkps, axis=1)
        cv[...] = jnp.concatenate(dvps, axis=1)
        lane = lax.broadcasted_iota(jnp.int32, (1, LANES), 1)
        acc = jnp.zeros((1, LANES), F32)
        for h in range(SWA_Q_HEADS):
            acc = acc + jnp.where(lane == h, dsinks[h], 0.0)
        ds_ref[...] += acc

    dy = pl.BlockSpec((t, MIX_WIDTH), lambda n: (rev(n), 0))
    kv_out = pl.BlockSpec((t, kvw), lambda n: (rev(n), 0))
    dq, dk, dv, ds = pl.pallas_call(
        body, grid=(nb,), in_specs=_swa_specs(nb, rev) + [dy],
        out_specs=[dy, kv_out, kv_out, pl.BlockSpec((1, LANES), lambda n: (0, 0))],
        out_shape=[jax.ShapeDtypeStruct((s, MIX_WIDTH), BF16), jax.ShapeDtypeStruct((s, kvw), BF16),
                   jax.ShapeDtypeStruct((s, kvw), BF16), jax.ShapeDtypeStruct((1, LANES), F32)],
        scratch_shapes=[pltpu.VMEM((t, kvw), F32), pltpu.VMEM((t, kvw), F32)],
        compiler_params=_cp(("arbitrary",)), name=name)(p, p, p, p, p, _pad_lanes(sinks), dcat)
    return dq, dk, dv, ds[0, :SWA_Q_HEADS]


RW_SHIFT = 5120
RW_R, RW_K, RW_V, RW_WD, RW_AD, RW_GD = 0, 1536, 3072, 4608, 4736, 4864


def _head_matrix(width, head_dim):
    e = (np.arange(width)[:, None] // head_dim == np.arange(LANES)[None, :]).astype(np.float32)
    return jnp.asarray(e), jnp.asarray(e.T)


def _rwkv_pre_fn(pieces, shifted, mus, w0, wdu, a0, wiu, wgu, k_k, k_a, e, et):
    r, k, v, wd, ad, gd = [p + (s - p) * mu for p, s, mu in zip(pieces, shifted, mus)]
    w_log = -_softplus(-(w0 + _dot(jnp.tanh(wd), wdu))) - 0.5
    lw = -jnp.exp(w_log)
    a = _sigmoid(a0 + _dot(ad, wiu))
    g = _dot(_sigmoid(gd), wgu)
    kkr = k * k_k
    kk = kkr * _dot(lax.rsqrt(_dot(kkr * kkr, e) + 1e-6), et)
    k2 = k * (1.0 + (a - 1.0) * k_a)
    return r, lw, k2, v, kk, kk * a, g


_RW_GROUPS = ((RW_R, MIX_WIDTH), (RW_K, MIX_WIDTH), (RW_V, MIX_WIDTH), (RW_WD, LANES), (RW_AD, LANES), (RW_GD, 2 * LANES))


def _rwkv_pre_inputs(p_ref, prev_ref, mu_ref, first):
    pieces = [p_ref[:, o:o + n] for o, n in _RW_GROUPS]
    shifted = [_shift_down(p_ref[:, o:o + n], prev_ref[:, o:o + n] * first, 1) for o, n in _RW_GROUPS]
    mus = [mu_ref[:, o:o + n] for o, n in _RW_GROUPS]
    return pieces, shifted, mus


def _rwkv_param_specs():
    vec = lambda n: pl.BlockSpec((1, n), lambda i: (0, 0))
    mat = lambda r, c: pl.BlockSpec((r, c), lambda i: (0, 0))
    return [vec(RW_SHIFT), vec(MIX_WIDTH), mat(LANES, MIX_WIDTH), vec(MIX_WIDTH), mat(LANES, MIX_WIDTH),
            mat(2 * LANES, MIX_WIDTH), vec(MIX_WIDTH), vec(MIX_WIDTH), mat(MIX_WIDTH, LANES), mat(LANES, MIX_WIDTH)]


def rwkv_pre_fwd(p, params, name):
    s = p.shape[0]
    t = _tile(s, (128, 64))

    def body(p_ref, prev_ref, mu_ref, *rest):
        prm, outs = rest[:9], rest[9:]
        first = (pl.program_id(0) > 0).astype(F32)
        pieces, shifted, mus = _rwkv_pre_inputs(p_ref, prev_ref, mu_ref, first)
        res = _rwkv_pre_fn(pieces, shifted, mus, *[q[...] for q in prm])
        for o_ref, val in zip(outs, res):
            o_ref[...] = val

    row = pl.BlockSpec((t, RW_SHIFT), lambda i: (i, 0))
    prev = pl.BlockSpec((SUBLANES, RW_SHIFT), lambda i: (jnp.maximum(i * (t // SUBLANES) - 1, 0), 0))
    out = pl.BlockSpec((t, MIX_WIDTH), lambda i: (i, 0))
    return pl.pallas_call(
        body, grid=(s // t,), in_specs=[row, prev] + _rwkv_param_specs(), out_specs=[out] * 7,
        out_shape=[jax.ShapeDtypeStruct((s, MIX_WIDTH), F32)] * 7, compiler_params=_cp(("parallel",)),
        name=name)(p, p, *params)


def rwkv_pre_bwd(p, params, cots, name):
    s = p.shape[0]
    t = _tile(s, (64, 32))

    def body(p_ref, prev_ref, mu_ref, *rest):
        prm, cot, outs = rest[:9], rest[9:19], rest[19:]
        dp_ref, dps_ref, grads = outs[0], outs[1], outs[2:]
        i = pl.program_id(0)
        first = (i > 0).astype(F32)
        pieces, shifted, mus = _rwkv_pre_inputs(p_ref, prev_ref, mu_ref, first)
        prm_v = [q[...] for q in prm]
        fn = lambda pieces, shifted, mus, *small: _rwkv_pre_fn(pieces, shifted, mus, *small, prm_v[7], prm_v[8])
        _, vjp = jax.vjp(fn, pieces, shifted, mus, *prm_v[:7])
        dr, dw, dk2, dv, dkk, db, dr2, dk22, dv2, dg = [c[...] for c in cot]
        res = vjp((dr + dr2, dw, dk2 + dk22, dv + dv2, dkk, db, dg))
        dpieces, dshifted, dmus, dsmall = res[0], res[1], res[2], res[3:]
        for (o, n), dpi, dsi in zip(_RW_GROUPS, dpieces, dshifted):
            dp_ref[:, o:o + n] = dpi
            dps_ref[:, o:o + n] = dsi

        @pl.when(i == 0)
        def _():
            for g_ref in grads:
                g_ref[...] = jnp.zeros_like(g_ref)

        for (o, n), dmu in zip(_RW_GROUPS, dmus):
            grads[0][:, o:o + n] += dmu
        for g_ref, dval in zip(grads[1:], dsmall):
            g_ref[...] += dval

    row = pl.BlockSpec((t, RW_SHIFT), lambda i: (i, 0))
    prev = pl.BlockSpec((SUBLANES, RW_SHIFT), lambda i: (jnp.maximum(i * (t // SUBLANES) - 1, 0), 0))
    act = pl.BlockSpec((t, MIX_WIDTH), lambda i: (i, 0))
    pspecs = _rwkv_param_specs()
    full = jax.ShapeDtypeStruct((s, RW_SHIFT), F32)
    gshapes = [jax.ShapeDtypeStruct(q.shape, F32) for q in params[:8]]
    return pl.pallas_call(
        body, grid=(s // t,), in_specs=[row, prev] + pspecs + [act] * 10, out_specs=[row, row] + pspecs[:8],
        out_shape=[full, full] + gshapes, compiler_params=_cp(("arbitrary",)), name=name)(p, p, *params, *cots)


def shift_add(a, b, js, out_dtype, name):
    s, c = a.shape
    t = _tile(s, (256, 128, 64))
    tc = _tile(c, (1024, 768, 512, 640, 384, 256, 128))
    nt, nb = s // t, len(b)

    def body(a_ref, *rest):
        b_refs, n_refs, o_ref = rest[:nb], rest[nb:2 * nb], rest[2 * nb]
        last = (pl.program_id(1) < nt - 1).astype(F32)
        acc = a_ref[...]
        for b_ref, n_ref, j in zip(b_refs, n_refs, js):
            acc = acc + _shift_up(b_ref[...], n_ref[...] * last, j)
        o_ref[...] = acc.astype(o_ref.dtype)

    tile = pl.BlockSpec((t, tc), lambda j, i: (i, j))
    _, nxt = _halo_specs(t, s, tc, lambda j: j)
    return pl.pallas_call(
        body, grid=(c // tc, nt), in_specs=[tile] * (1 + nb) + [nxt] * nb, out_specs=tile,
        out_shape=jax.ShapeDtypeStruct((s, c), out_dtype), compiler_params=_cp(("parallel", "parallel")),
        name=name)(a, *b, *b)


def _rwkv_post_fn(y, r, k2, v, g, gn_g, gn_b, r_k, e, et):
    n = RWKV_HEAD_DIM
    yc = y - _dot(_dot(y, e), et) * (1.0 / n)
    rstd = lax.rsqrt(_dot(yc * yc, e) * (1.0 / n) + RWKV_GN_EPS)
    yn = yc * _dot(rstd, et) * gn_g + gn_b
    bonus = _dot(_dot(r * k2 * r_k, e), et) * v
    return (yn + bonus) * g


def rwkv_post_fwd(acts, params, name):
    s = acts[0].shape[0]
    t = _tile(s, (256, 128))

    def body(*refs):
        vals = [q[...] for q in refs[:10]]
        refs[10][...] = _rwkv_post_fn(*vals).astype(refs[10].dtype)

    act = pl.BlockSpec((t, MIX_WIDTH), lambda i: (i, 0))
    vec = pl.BlockSpec((1, MIX_WIDTH), lambda i: (0, 0))
    mats = [pl.BlockSpec((MIX_WIDTH, LANES), lambda i: (0, 0)), pl.BlockSpec((LANES, MIX_WIDTH), lambda i: (0, 0))]
    return pl.pallas_call(
        body, grid=(s // t,), in_specs=[act] * 5 + [vec] * 3 + mats, out_specs=act,
        out_shape=jax.ShapeDtypeStruct((s, MIX_WIDTH), BF16), compiler_params=_cp(("parallel",)),
        name=name)(*acts, *params)


def rwkv_post_bwd(acts, params, dcat, name):
    s = acts[0].shape[0]
    t = _tile(s, (128, 64))

    def body(*refs):
        ins, dy_ref, outs = refs[:10], refs[10], refs[11:]
        vals = [q[...] for q in ins]
        fn = lambda *a: _rwkv_post_fn(*a, vals[8], vals[9])
        _, vjp = jax.vjp(fn, *vals[:8])
        res = vjp(dy_ref[...])
        for o_ref, val in zip(outs[:5], res[:5]):
            o_ref[...] = val

        @pl.when(pl.program_id(0) == 0)
        def _():
            for g_ref in outs[5:]:
                g_ref[...] = jnp.zeros_like(g_ref)

        for g_ref, val in zip(outs[5:], res[5:]):
            g_ref[...] += val

    act = pl.BlockSpec((t, MIX_WIDTH), lambda i: (i, 0))
    vec = pl.BlockSpec((1, MIX_WIDTH), lambda i: (0, 0))
    mats = [pl.BlockSpec((MIX_WIDTH, LANES), lambda i: (0, 0)), pl.BlockSpec((LANES, MIX_WIDTH), lambda i: (0, 0))]
    a_shape = jax.ShapeDtypeStruct((s, MIX_WIDTH), F32)
    v_shape = jax.ShapeDtypeStruct((1, MIX_WIDTH), F32)
    return pl.pallas_call(
        body, grid=(s // t,), in_specs=[act] * 5 + [vec] * 3 + mats + [act], out_specs=[act] * 5 + [vec] * 3,
        out_shape=[a_shape] * 5 + [v_shape] * 3, compiler_params=_cp(("arbitrary",)), name=name)(*acts, *params, dcat)


RW_CHUNK = 64


RW_HEADS_PER_STEP = 4


def _heads(f, *per_head):
    return [f(*xs) for xs in zip(*per_head)]


def _rwkv_chunk_fn(r, lw, k, v, kk, b, st):
    c = RW_CHUNK
    ri = lax.broadcasted_iota(jnp.int32, (c, c), 0)
    ci = lax.broadcasted_iota(jnp.int32, (c, c), 1)
    incl, strict = ri >= ci, ri > ci
    ltri = incl.astype(F32)
    eye = (ri == ci).astype(F32)
    last_col = (ci == c - 1).astype(F32)
    last_row = (ri == c - 1).astype(F32)
    gc = _heads(lambda lw: _dot(ltri, lw), lw)
    a_t = _heads(lambda kk, gc, lw: -kk * jnp.exp(gc - lw), kk, gc, lw)
    e_neg = _heads(lambda gc: jnp.exp(-gc), gc)
    b_t = _heads(jnp.multiply, b, e_neg)
    k_t = _heads(jnp.multiply, k, e_neg)
    r_t = _heads(lambda r, gc: r * jnp.exp(gc), r, gc)
    m_ab = _heads(lambda x, y: jnp.where(strict, _dot_nt(x, y), 0.0), a_t, b_t)
    m_ak = _heads(lambda x, y: jnp.where(strict, _dot_nt(x, y), 0.0), a_t, k_t)
    m_rb = _heads(lambda x, y: jnp.where(incl, _dot_nt(x, y), 0.0), r_t, b_t)
    m_rk = _heads(lambda x, y: jnp.where(incl, _dot_nt(x, y), 0.0), r_t, k_t)
    tinv, pw = _heads(lambda m: eye + m, m_ab), m_ab
    for _ in range(5):
        pw = _heads(lambda p: _dot(p, p), pw)
        tinv = _heads(lambda t, p: t + _dot(t, p), tinv, pw)
    rhs = _heads(lambda a, s, m, v: _dot(a, s) + _dot(m, v), a_t, st, m_ak, v)
    u = _heads(_dot, tinv, rhs)
    y = _heads(lambda r, s, mb, u, mk, v: _dot(r, s) + _dot(mb, u) + _dot(mk, v), r_t, st, m_rb, u, m_rk, v)
    dec = _heads(lambda gc: jnp.exp(_dot(last_col, gc) - gc), gc)
    g_end = _heads(lambda gc: _dot_tn(gc, last_row), gc)
    new_st = _heads(lambda s, g, b, d, u, k, v: s * jnp.exp(g) + _dot_tn(b * d, u) + _dot_tn(k * d, v),
                    st, g_end, b, dec, u, k, v)
    return y, new_st


def rwkv_scan_fwd(r, lw, k, v, kk, b, name):
    s = r.shape[0]
    n, hp = RWKV_HEAD_DIM, RW_HEADS_PER_STEP
    nchunk, width = s // RW_CHUNK, RWKV_HEAD_DIM * RW_HEADS_PER_STEP

    def body(r_ref, w_ref, k_ref, v_ref, kk_ref, b_ref, y_ref, ck_ref, carry):
        @pl.when(pl.program_id(1) == 0)
        def _():
            carry[...] = jnp.zeros_like(carry)

        ck_ref[0] = carry[...]
        per_head = lambda ref: [ref[:, h * n:(h + 1) * n] for h in range(hp)]
        ys, sts = _rwkv_chunk_fn(*[per_head(q) for q in (r_ref, w_ref, k_ref, v_ref, kk_ref, b_ref, carry)])
        y_ref[...] = jnp.concatenate(ys, axis=1)
        carry[...] = jnp.concatenate(sts, axis=1)

    blk = pl.BlockSpec((RW_CHUNK, width), lambda j, c: (c, j))
    return pl.pallas_call(
        body, grid=(MIX_WIDTH // width, nchunk), in_specs=[blk] * 6,
        out_specs=[blk, pl.BlockSpec((1, n, width), lambda j, c: (c, 0, j))],
        out_shape=[jax.ShapeDtypeStruct((s, MIX_WIDTH), F32), jax.ShapeDtypeStruct((nchunk, n, MIX_WIDTH), F32)],
        scratch_shapes=[pltpu.VMEM((n, width), F32)],
        compiler_params=_cp(("parallel", "arbitrary")), name=name)(r, lw, k, v, kk, b)


def rwkv_scan_bwd(r, lw, k, v, kk, b, ck, dy, name):
    s = r.shape[0]
    n, hp = RWKV_HEAD_DIM, RW_HEADS_PER_STEP
    nchunk, width = s // RW_CHUNK, RWKV_HEAD_DIM * RW_HEADS_PER_STEP
    rev = lambda c: nchunk - 1 - c

    def body(r_ref, w_ref, k_ref, v_ref, kk_ref, b_ref, ck_ref, dy_ref, *rest):
        outs, carry = rest[:6], rest[6]

        @pl.when(pl.program_id(1) == 0)
        def _():
            carry[...] = jnp.zeros_like(carry)

        per_head = lambda ref: [ref[:, h * n:(h + 1) * n] for h in range(hp)]
        args = [per_head(q) for q in (r_ref, w_ref, k_ref, v_ref, kk_ref, b_ref)]
        args.append([ck_ref[0, :, h * n:(h + 1) * n] for h in range(hp)])
        _, vjp = jax.vjp(_rwkv_chunk_fn, *args)
        grads = vjp((per_head(dy_ref), per_head(carry)))
        for q in range(6):
            outs[q][...] = jnp.concatenate(grads[q], axis=1)
        carry[...] = jnp.concatenate(grads[6], axis=1)

    blk = pl.BlockSpec((RW_CHUNK, width), lambda j, c: (rev(c), j))
    out = jax.ShapeDtypeStruct((s, MIX_WIDTH), F32)
    return pl.pallas_call(
        body, grid=(MIX_WIDTH // width, nchunk),
        in_specs=[blk] * 6 + [pl.BlockSpec((1, n, width), lambda j, c: (rev(c), 0, j)), blk],
        out_specs=[blk] * 6, out_shape=[out] * 6, scratch_shapes=[pltpu.VMEM((n, width), F32)],
        compiler_params=_cp(("parallel", "arbitrary")), name=name)(r, lw, k, v, kk, b, ck, dy)


GD_Q, GD_K, GD_V, GD_Z, GD_QMEM, GD_BT, GD_AT, GD_COLS = 0, 768, 1536, 3072, 4608, 5120, 5248, 5376
_GD_GROUPS = ((GD_Q, GDN_QK_WIDTH), (GD_K, GDN_QK_WIDTH), (GD_V, MIX_WIDTH))


def _gdn_pre_fn(xs, convs, bt, at, a_log, dt_bias, e6, e6t, ebc):
    k_w = GDN_CONV
    acts = [_silu(sum(convs[g][j] * xs[g][k_w - 1 - j] for j in range(k_w))) for g in range(3)]
    l2 = lambda x: x * _dot(lax.rsqrt(_dot(x * x, e6) + 1e-6), e6t)
    beta = _sigmoid(bt)
    g = -jnp.exp(a_log) * _softplus(at + dt_bias)
    return l2(acts[0]), l2(acts[1]), acts[2], _dot(g, ebc), _dot(beta, ebc)


def _gdn_pre_inputs(x_ref, prev_ref, conv_ref, first):
    xs = [[_shift_down(x_ref[:, o:o + n], prev_ref[:, o:o + n] * first, j) for j in range(GDN_CONV)]
          for o, n in _GD_GROUPS]
    convs = [[conv_ref[j:j + 1, o:o + n] for j in range(GDN_CONV)] for o, n in _GD_GROUPS]
    return xs, convs


def _gdn_pre_specs(t):
    x = pl.BlockSpec((t, GDN_CONV_WIDTH), lambda i: (i, 0))
    prev = pl.BlockSpec((SUBLANES, GDN_CONV_WIDTH), lambda i: (jnp.maximum(i * (t // SUBLANES) - 1, 0), 0))
    bta = pl.BlockSpec((t, 2 * LANES), lambda i: (i, GD_BT // (2 * LANES)))
    conv = pl.BlockSpec((GDN_CONV, GDN_CONV_WIDTH), lambda i: (0, 0))
    vec = pl.BlockSpec((1, LANES), lambda i: (0, 0))
    mats = [pl.BlockSpec((GDN_QK_WIDTH, LANES), lambda i: (0, 0)), pl.BlockSpec((LANES, GDN_QK_WIDTH), lambda i: (0, 0)),
            pl.BlockSpec((LANES, MIX_WIDTH), lambda i: (0, 0))]
    return [x, prev, bta, conv, vec, vec] + mats


def gdn_pre_fwd(p, params, name):
    s = p.shape[0]
    t = _tile(s, (128, 64))

    def body(x_ref, prev_ref, bta_ref, conv_ref, al_ref, dt_ref, e6_ref, e6t_ref, ebc_ref, *outs):
        first = (pl.program_id(0) > 0).astype(F32)
        xs, convs = _gdn_pre_inputs(x_ref, prev_ref, conv_ref, first)
        res = _gdn_pre_fn(xs, convs, bta_ref[:, :LANES], bta_ref[:, LANES:], al_ref[...], dt_ref[...],
                          e6_ref[...], e6t_ref[...], ebc_ref[...])
        for o_ref, val in zip(outs, res):
            o_ref[...] = val

    qk = pl.BlockSpec((t, GDN_QK_WIDTH), lambda i: (i, 0))
    wide = pl.BlockSpec((t, MIX_WIDTH), lambda i: (i, 0))
    qk_s, wide_s = jax.ShapeDtypeStruct((s, GDN_QK_WIDTH), F32), jax.ShapeDtypeStruct((s, MIX_WIDTH), F32)
    return pl.pallas_call(
        body, grid=(s // t,), in_specs=_gdn_pre_specs(t), out_specs=[qk, qk, wide, wide, wide],
        out_shape=[qk_s, qk_s, wide_s, wide_s, wide_s], compiler_params=_cp(("parallel",)), name=name)(p, p, p, *params)


def gdn_pre_bwd(p, params, cots, name):
    s = p.shape[0]
    t = _tile(s, (64, 32))

    def body(x_ref, prev_ref, bta_ref, conv_ref, al_ref, dt_ref, e6_ref, e6t_ref, ebc_ref, *rest):
        cot, outs = rest[:5], rest[5:]
        dxs, dbta_ref, dconv_ref, dal_ref, ddt_ref = outs[:4], outs[4], outs[5], outs[6], outs[7]
        i = pl.program_id(0)
        first = (i > 0).astype(F32)
        xs, convs = _gdn_pre_inputs(x_ref, prev_ref, conv_ref, first)
        mats = (e6_ref[...], e6t_ref[...], ebc_ref[...])
        fn = lambda xs, convs, bt, at, al, dt: _gdn_pre_fn(xs, convs, bt, at, al, dt, *mats)
        _, vjp = jax.vjp(fn, xs, convs, bta_ref[:, :LANES], bta_ref[:, LANES:], al_ref[...], dt_ref[...])
        d_xs, d_convs, d_bt, d_at, d_al, d_dt = vjp(tuple(c[...] for c in cot))
        for g, (o, n) in enumerate(_GD_GROUPS):
            for j in range(GDN_CONV):
                dxs[j][:, o:o + n] = d_xs[g][j]
        dbta_ref[...] = jnp.concatenate([d_bt, d_at], axis=1).astype(dbta_ref.dtype)

        @pl.when(i == 0)
        def _():
            dconv_ref[...] = jnp.zeros_like(dconv_ref)
            dal_ref[...] = jnp.zeros_like(dal_ref)
            ddt_ref[...] = jnp.zeros_like(ddt_ref)

        for g, (o, n) in enumerate(_GD_GROUPS):
            for j in range(GDN_CONV):
                dconv_ref[j:j + 1, o:o + n] += d_convs[g][j]
        dal_ref[...] += d_al
        ddt_ref[...] += d_dt

    specs = _gdn_pre_specs(t)
    qk = pl.BlockSpec((t, GDN_QK_WIDTH), lambda i: (i, 0))
    wide = pl.BlockSpec((t, MIX_WIDTH), lambda i: (i, 0))
    x_s = jax.ShapeDtypeStruct((s, GDN_CONV_WIDTH), F32)
    vec_s = jax.ShapeDtypeStruct((1, LANES), F32)
    return pl.pallas_call(
        body, grid=(s // t,), in_specs=specs + [qk, qk, wide, wide, wide],
        out_specs=[specs[0]] * 4 + [pl.BlockSpec((t, 2 * LANES), lambda i: (i, 0)), specs[3], specs[4], specs[5]],
        out_shape=[x_s] * 4 + [jax.ShapeDtypeStruct((s, 2 * LANES), BF16),
                               jax.ShapeDtypeStruct((GDN_CONV, GDN_CONV_WIDTH), F32), vec_s, vec_s],
        compiler_params=_cp(("arbitrary",)), name=name)(p, p, p, *params, *cots)


def _gdn_post_fn(o, z, norm_g, e12, e12t, trep):
    rstd = lax.rsqrt(_dot(o * o, e12) * (1.0 / GDN_HEAD_DIM) + NORM_EPS)
    return o * _dot(rstd, e12t) * _dot(norm_g, trep) * _silu(z)


def _gdn_post_specs(t):
    act = pl.BlockSpec((t, MIX_WIDTH), lambda i: (i, 0))
    z = pl.BlockSpec((t, MIX_WIDTH), lambda i: (i, GD_Z // MIX_WIDTH))
    mats = [pl.BlockSpec((SUBLANES, LANES), lambda i: (0, 0)), pl.BlockSpec((MIX_WIDTH, LANES), lambda i: (0, 0)),
            pl.BlockSpec((LANES, MIX_WIDTH), lambda i: (0, 0)), pl.BlockSpec((LANES, MIX_WIDTH), lambda i: (0, 0))]
    return [act, z] + mats


def gdn_post_fwd(o, p, params, name):
    s = o.shape[0]
    t = _tile(s, (256, 128))

    def body(o_ref, z_ref, ng_ref, e_ref, et_ref, tr_ref, out_ref):
        res = _gdn_post_fn(o_ref[...], z_ref[...], ng_ref[0:1, :], e_ref[...], et_ref[...], tr_ref[...])
        out_ref[...] = res.astype(out_ref.dtype)

    act = pl.BlockSpec((t, MIX_WIDTH), lambda i: (i, 0))
    return pl.pallas_call(
        body, grid=(s // t,), in_specs=_gdn_post_specs(t), out_specs=act,
        out_shape=jax.ShapeDtypeStruct((s, MIX_WIDTH), BF16), compiler_params=_cp(("parallel",)),
        name=name)(o, p, *params)


def gdn_post_bwd(o, p, params, dcat, name):
    s = o.shape[0]
    t = _tile(s, (128, 64))

    def body(o_ref, z_ref, ng_ref, e_ref, et_ref, tr_ref, dy_ref, do_ref, dz_ref, dng_ref):
        mats = (e_ref[...], et_ref[...], tr_ref[...])
        fn = lambda o, z, ng: _gdn_post_fn(o, z, ng, *mats)
        _, vjp = jax.vjp(fn, o_ref[...], z_ref[...], ng_ref[0:1, :])
        d_o, d_z, d_ng = vjp(dy_ref[...])
        do_ref[...] = d_o
        dz_ref[...] = d_z.astype(dz_ref.dtype)

        @pl.when(pl.program_id(0) == 0)
        def _():
            dng_ref[...] = jnp.zeros_like(dng_ref)

        dng_ref[...] += d_ng

    act = pl.BlockSpec((t, MIX_WIDTH), lambda i: (i, 0))
    return pl.pallas_call(
        body, grid=(s // t,), in_specs=_gdn_post_specs(t) + [act],
        out_specs=[act, act, pl.BlockSpec((1, LANES), lambda i: (0, 0))],
        out_shape=[jax.ShapeDtypeStruct((s, MIX_WIDTH), F32), jax.ShapeDtypeStruct((s, MIX_WIDTH), BF16),
                   jax.ShapeDtypeStruct((1, LANES), F32)],
        compiler_params=_cp(("arbitrary",)), name=name)(o, p, *params, dcat)


GDN_REP = GDN_V_HEADS // GDN_QK_HEADS
GDN_QK_PER_STEP = 2


def _gdn_chunk_fn(q, k, v, gb, bb, gb64, state):
    c = GDN_CHUNK
    ri = lax.broadcasted_iota(jnp.int32, (c, c), 0)
    ci = lax.broadcasted_iota(jnp.int32, (c, c), 1)
    causal, strict = ri >= ci, ri > ci
    ltri = causal.astype(F32)
    eye = (ri == ci).astype(F32)
    first_col = (ci == 0).astype(F32)
    last_col = (ci == c - 1).astype(F32)
    last_col_tall = (lax.broadcasted_iota(jnp.int32, (GDN_HEAD_DIM, c), 1) == c - 1).astype(F32)
    of_value_head = lambda per_qk: [per_qk[h // GDN_REP] for h in range(len(v))]

    qs = of_value_head(_heads(lambda q: q * (GDN_HEAD_DIM ** -0.5), q))
    k = of_value_head(k)
    gc = _heads(lambda g: _dot(ltri, g), gb)
    gd = _heads(lambda g: _dot(ltri, g), gb64)
    decay = _heads(lambda gd: jnp.exp(jnp.where(causal, gd - _dot_nt(first_col, gd), -jnp.inf)), gd)
    kb = _heads(jnp.multiply, k, bb)
    lmat = _heads(lambda kb, k, dc: jnp.where(strict, _dot_nt(kb, k) * dc, 0.0), kb, k, decay)
    tmat, pw = _heads(lambda l: eye - l, lmat), lmat
    for _ in range(5):
        pw = _heads(lambda p: _dot(p, p), pw)
        tmat = _heads(lambda t, p: t + _dot(t, p), tmat, pw)
    eg = _heads(jnp.exp, gc)
    u = _heads(lambda t, v, bb: _dot(t, v * bb), tmat, v, bb)
    w = _heads(lambda t, kb, eg: _dot(t, kb * eg), tmat, kb, eg)
    a_qk = _heads(lambda q, k, dc: jnp.where(causal, _dot_nt(q, k) * dc, 0.0), qs, k, decay)
    k_dec = _heads(lambda k, gc: k * jnp.exp(_dot(last_col, gc) - gc), k, gc)
    v_new = _heads(lambda u, w, s: u - _dot(w, s), u, w, state)
    out = _heads(lambda q, eg, s, a, vn: _dot(q * eg, s) + _dot(a, vn), qs, eg, state, a_qk, v_new)
    new_state = _heads(lambda s, gc, kd, vn: s * jnp.exp(_dot(last_col_tall, gc)) + _dot_tn(kd, vn),
                       state, gc, k_dec, v_new)
    return out, new_state


def _gdn_chunk_specs(order):
    c, d, nq = GDN_CHUNK, GDN_HEAD_DIM, GDN_QK_PER_STEP
    qk = pl.BlockSpec((c, nq * d), lambda j, n: (order(n), j))
    vh = pl.BlockSpec((c, nq * GDN_REP * d), lambda j, n: (order(n), j))
    st = pl.BlockSpec((nq * GDN_REP, 1, d, d), lambda j, n: (j, order(n), 0, 0))
    return qk, vh, st


def _gdn_chunk_args(q_ref, k_ref, v_ref, gb_ref, bb_ref):
    d, nq = GDN_HEAD_DIM, GDN_QK_PER_STEP
    nv = nq * GDN_REP
    per = lambda ref, n: [ref[:, h * d:(h + 1) * d] for h in range(n)]
    lead = [gb_ref[:, h * d:h * d + GDN_CHUNK] for h in range(nv)]
    return per(q_ref, nq), per(k_ref, nq), per(v_ref, nv), per(gb_ref, nv), per(bb_ref, nv), lead


def gdn_chunk_fwd(q, k, v, gb, bb, name):
    s = q.shape[0]
    nc, d, nv = s // GDN_CHUNK, GDN_HEAD_DIM, GDN_QK_PER_STEP * GDN_REP

    def body(q_ref, k_ref, v_ref, gb_ref, bb_ref, o_ref, st_ref, carry):
        @pl.when(pl.program_id(1) == 0)
        def _():
            carry[...] = jnp.zeros_like(carry)

        states = [carry[h] for h in range(nv)]
        for h in range(nv):
            st_ref[h, 0] = states[h]
        outs, new_states = _gdn_chunk_fn(*_gdn_chunk_args(q_ref, k_ref, v_ref, gb_ref, bb_ref), states)
        o_ref[...] = jnp.concatenate(outs, axis=1)
        for h in range(nv):
            carry[h] = new_states[h]

    qk, vh, st = _gdn_chunk_specs(lambda n: n)
    return pl.pallas_call(
        body, grid=(GDN_QK_HEADS // GDN_QK_PER_STEP, nc), in_specs=[qk, qk, vh, vh, vh], out_specs=[vh, st],
        out_shape=[jax.ShapeDtypeStruct((s, MIX_WIDTH), F32), jax.ShapeDtypeStruct((GDN_V_HEADS, nc, d, d), F32)],
        scratch_shapes=[pltpu.VMEM((nv, d, d), F32)],
        compiler_params=_cp(("parallel", "arbitrary")), name=name)(q, k, v, gb, bb)


def gdn_chunk_bwd(q, k, v, gb, bb, states, do, name):
    s = q.shape[0]
    nc, d, nv = s // GDN_CHUNK, GDN_HEAD_DIM, GDN_QK_PER_STEP * GDN_REP
    rev = lambda n: nc - 1 - n

    def body(q_ref, k_ref, v_ref, gb_ref, bb_ref, st_ref, do_ref, dq_ref, dk_ref, dv_ref, dg_ref, db_ref, carry):
        @pl.when(pl.program_id(1) == 0)
        def _():
            carry[...] = jnp.zeros_like(carry)

        args = _gdn_chunk_args(q_ref, k_ref, v_ref, gb_ref, bb_ref) + ([st_ref[h, 0] for h in range(nv)],)
        _, vjp = jax.vjp(_gdn_chunk_fn, *args)
        cot = ([do_ref[:, h * d:(h + 1) * d] for h in range(nv)], [carry[h] for h in range(nv)])
        d_q, d_k, d_v, d_gb, d_bb, d_gb64, d_state = vjp(cot)
        dq_ref[...] = jnp.concatenate(d_q, axis=1)
        dk_ref[...] = jnp.concatenate(d_k, axis=1)
        dv_ref[...] = jnp.concatenate(d_v, axis=1)
        db_ref[...] = jnp.concatenate(d_bb, axis=1)
        dg_ref[...] = jnp.concatenate(d_gb, axis=1)
        for h in range(nv):
            dg_ref[:, h * d:h * d + GDN_CHUNK] += d_gb64[h]
            carry[h] = d_state[h]

    qk, vh, st = _gdn_chunk_specs(rev)
    qk_s, wide_s = jax.ShapeDtypeStruct((s, GDN_QK_WIDTH), F32), jax.ShapeDtypeStruct((s, MIX_WIDTH), F32)
    return pl.pallas_call(
        body, grid=(GDN_QK_HEADS // GDN_QK_PER_STEP, nc), in_specs=[qk, qk, vh, vh, vh, st, vh],
        out_specs=[qk, qk, vh, vh, vh], out_shape=[qk_s, qk_s, wide_s, wide_s, wide_s],
        scratch_shapes=[pltpu.VMEM((nv, d, d), F32)],
        compiler_params=_cp(("parallel", "arbitrary")), name=name)(q, k, v, gb, bb, states, do)


WEIGHTS = ['attn_norm', 'mem_norm', 'w_mem_kv', 'w_out', 'ffn_norm', 'w_ffn_up', 'ffn_conv', 'w_ffn_down', 'final_norm',
           'a_w_in', 'a_sinks', 'b_w_in', 'b_mu', 'b_w0', 'b_w_decay_up', 'b_a0', 'b_w_iclr_up', 'b_w_gate_up', 'b_k_k',
           'b_k_a', 'b_r_k', 'b_gn_g', 'b_gn_b', 'c_w_in', 'c_conv', 'c_a_log', 'c_dt_bias', 'c_norm_g']
INPUTS = ['x', 'mem'] + WEIGHTS + ['loss_target'] + ['m_' + n for n in WEIGHTS] + ['v_' + n for n in WEIGHTS]
REPLICATED = ['attn_norm', 'mem_norm', 'ffn_norm', 'final_norm', 'a_sinks', 'b_mu', 'b_w0', 'b_a0', 'b_k_k', 'b_k_a',
              'b_r_k', 'b_gn_g', 'b_gn_b', 'c_a_log', 'c_dt_bias', 'c_norm_g']
C_MIX = GDN_CONV_WIDTH + MIX_WIDTH
GATHER_ID = 1


def _cols_to_shards(full):
    rows, cols = full.shape
    return full.reshape(rows, N_DEV, cols // N_DEV).transpose(1, 0, 2)


def _shards_to_cols(g):
    return g.transpose(1, 0, 2).reshape(g.shape[1], N_DEV * g.shape[2])


def _pad_to(x, n, axis):
    pad = [(0, 0)] * x.ndim
    pad[axis] = (0, n - x.shape[axis])
    return jnp.pad(x, pad)


def _b_pad_cols(w):
    parts = [w[..., :4608], _pad_to(w[..., 4608:4704], LANES, -1), _pad_to(w[..., 4704:4800], LANES, -1), w[..., 4800:5056]]
    if w.shape[-1] > 5056:
        parts.append(w[..., 5056:])
    return jnp.concatenate(parts, axis=-1)


def _b_unpad_cols(w):
    parts = [w[..., :4608], w[..., RW_WD:RW_WD + RWKV_DECAY_RANK], w[..., RW_AD:RW_AD + RWKV_ICLR_RANK], w[..., RW_GD:RW_SHIFT]]
    if w.shape[-1] > RW_SHIFT:
        parts.append(w[..., RW_SHIFT:])
    return jnp.concatenate(parts, axis=-1)


def _c_pad_cols(w):
    return jnp.concatenate([w[..., :C_MIX], w[..., C_MIX + 24:], _pad_to(w[..., C_MIX:C_MIX + 12], LANES, -1),
                            _pad_to(w[..., C_MIX + 12:C_MIX + 24], LANES, -1)], axis=-1)


def _c_unpad_cols(w):
    return jnp.concatenate([w[..., :C_MIX], w[..., GD_BT:GD_BT + GDN_V_HEADS], w[..., GD_AT:GD_AT + GDN_V_HEADS],
                            w[..., GD_QMEM:GD_BT]], axis=-1)


def _pack(arrays):
    flat = jnp.concatenate([a.reshape(-1).astype(F32) for a in arrays])
    unit = SUBLANES * LANES
    return _pad_to(flat, -(-flat.size // unit) * unit, 0).reshape(-1, LANES)


def _unpack(packed, shapes):
    flat, out, at = packed.reshape(-1), [], 0
    for shp in shapes:
        n = int(np.prod(shp))
        out.append(flat[at:at + n].reshape(shp))
        at += n
    return out


def kernel(*args):
    a = dict(zip(INPUTS, args))
    x0, mem, target = a['x'][0], a['mem'][0], a['loss_target'][0]
    s = x0.shape[0]
    e64, e64t = _head_matrix(MIX_WIDTH, RWKV_HEAD_DIM)
    e6, e6t = _head_matrix(GDN_QK_WIDTH, GDN_HEAD_DIM)
    e12, e12t = _head_matrix(MIX_WIDTH, GDN_HEAD_DIM)
    trep = jnp.asarray((np.arange(LANES)[:, None] == np.arange(MIX_WIDTH)[None, :] % LANES).astype(np.float32))
    row = lambda v: v.reshape(1, -1)

    def in_proj_shard(l):
        kind, j = l % 3, l // 3
        return (a['a_w_in'], a['b_w_in'], a['c_w_in'])[kind][j]

    def small_shards(l):
        kind, j = l % 3, l // 3
        if kind == 1:
            return [a['b_w_decay_up'][j], a['b_w_iclr_up'][j], a['b_w_gate_up'][j]]
        if kind == 2:
            return [a['c_conv'][j]]
        return []

    gathered = []
    for l in range(DEPTH):
        big = [a['w_mem_kv'][l], a['w_out'][l], a['w_ffn_up'][l], a['w_ffn_down'][l], in_proj_shard(l)]
        shards = [w.astype(BF16) for w in big] + [a['ffn_conv'][l]] + small_shards(l)
        if gathered:
            shards, gathered[-1] = lax.optimization_barrier((shards, gathered[-1]))
            gathered.append(all_gather_many_async(shards, f"gather_weights_{l}", GATHER_ID + l))
        else:
            first = all_gather_many_async([shards[0], shards[4]], "gather_weights_0_attn", GATHER_ID + DEPTH)
            rest, first = lax.optimization_barrier(([shards[1], shards[2], shards[3], shards[5]], first))
            rest = all_gather_many_async(rest, "gather_weights_0", GATHER_ID)
            gathered.append([first[0], rest[0], rest[1], rest[2], first[1], rest[3]])

    def ffn_weights(g):
        return dict(w_out=g[1].reshape(D_MODEL, D_MODEL), w_up=_shards_to_cols(g[2]),
                    w_down=g[3].reshape(D_FF, D_MODEL), conv=_shards_to_cols(g[5]))

    def layer_weights(l, g):
        kind = l % 3
        w_in = _shards_to_cols(g[4])
        lw = dict(w_kv=g[0].reshape(D_MODEL, 2 * MEM_WIDTH))
        if kind == 0:
            lw['w_in'] = w_in
        elif kind == 1:
            lw['w_in'] = _b_pad_cols(w_in)
            lw['wdu'] = _pad_to(_shards_to_cols(g[6]), LANES, 0)
            lw['wiu'] = _pad_to(_shards_to_cols(g[7]), LANES, 0)
            lw['wgu'] = _shards_to_cols(g[8])
        else:
            lw['w_in'] = _c_pad_cols(w_in)
            lw['c_conv'] = _shards_to_cols(g[6])
        return lw

    def rwkv_params(j, lw):
        return (row(_b_pad_cols(a['b_mu'][j])), row(a['b_w0'][j]), lw['wdu'], row(a['b_a0'][j]), lw['wiu'], lw['wgu'],
                row(a['b_k_k'][j]), row(a['b_k_a'][j]), e64, e64t)

    def rwkv_post_params(j):
        return (row(a['b_gn_g'][j]), row(a['b_gn_b'][j]), row(a['b_r_k'][j]), e64, e64t)

    def gdn_params(j, lw):
        return (lw['c_conv'], _pad_lanes(a['c_a_log'][j]), _pad_lanes(a['c_dt_bias'][j]), e6, e6t, e12t)

    def gdn_post_params(j):
        return (jnp.tile(row(a['c_norm_g'][j]), (SUBLANES, 1)), e12, e12t, trep)

    x = x0
    saved, layers = [], []
    for l in range(DEPTH):
        kind, j = l % 3, l // 3
        g = gathered[l]
        if l > 0:
            x, g = lax.optimization_barrier((x, g))
        lw = layer_weights(l, g)
        if l > 0:
            lw.update(ffn_weights(g))
        layers.append(lw)
        sv = dict(x=x)
        h = rmsnorm_fwd(x, a['attn_norm'][l], BF16, f"attn_norm_{l}")
        memn = rmsnorm_fwd(mem, a['mem_norm'][l], BF16, f"mem_norm_{l}")
        mem_kv = mm(memn, lw['w_kv'], name=f"mem_kv_{l}")
        p = mm(h, lw['w_in'], name=f"in_proj_{l}")
        if kind == 0:
            y = swa_fwd(p, a['a_sinks'][j], f"swa_{l}")
            q_col = MIX_WIDTH + 2 * SWA_KV_HEADS * SWA_HEAD_DIM
        elif kind == 1:
            pre = rwkv_pre_fwd(p, rwkv_params(j, lw), f"rwkv_pre_{l}")
            yscan, ck = rwkv_scan_fwd(*pre[:6], f"rwkv_scan_{l}")
            post_in = (yscan, pre[0], pre[2], pre[3], pre[6])
            y = rwkv_post_fwd(post_in, rwkv_post_params(j), f"rwkv_post_{l}")
            sv.update(pre=pre, ck=ck, post_in=post_in)
            q_col = RW_SHIFT
        else:
            pre = gdn_pre_fwd(p, gdn_params(j, lw), f"gdn_pre_{l}")
            o, states = gdn_chunk_fwd(*pre, f"gdn_chunk_{l}")
            y = gdn_post_fwd(o, p, gdn_post_params(j), f"gdn_post_{l}")
            sv.update(pre=pre, o=o, states=states)
            q_col = GD_QMEM
        y_mem = mem_attn_fwd(p, q_col, mem_kv, f"mem_attn_{l}")
        if l == 0:
            y_mem, g = lax.optimization_barrier((y_mem, g))
            lw.update(ffn_weights(g))
        cat = jnp.concatenate([y, y_mem], axis=1)
        x1 = mm(cat, lw['w_out'], res=x, name=f"out_proj_{l}")
        hf = rmsnorm_fwd(x1, a['ffn_norm'][l], BF16, f"ffn_norm_{l}")
        u0 = mm(hf, lw['w_up'], name=f"ffn_up_{l}")
        act = ffn_act_fwd(u0, lw['conv'], f"ffn_act_{l}")
        x = mm(act, lw['w_down'], res=x1, name=f"ffn_down_{l}")
        sv.update(h=h, memn=memn, mem_kv=mem_kv, p=p, q_col=q_col, cat=cat, x1=x1, hf=hf, u0=u0, act=act)
        saved.append(sv)

    loss_part, dx, d_final_norm = final_loss(x, a['final_norm'], target, "final_loss")

    rep_grads = {n: [None] * a[n].shape[0] for n in ('attn_norm', 'mem_norm', 'ffn_norm', 'a_sinks')}
    rep_grads['final_norm'] = d_final_norm
    results = {}
    exchanged, pending = {}, {}

    def apply_adam(name, idx, pieces, tag):
        w, m, v = a[name][idx], a['m_' + name][idx], a['v_' + name][idx]
        shp = w.shape
        two_d = (-1, shp[-1])
        out = adamw_sum(pieces.reshape((N_DEV,) + w.reshape(two_d).shape), w.reshape(two_d), m.reshape(two_d),
                        v.reshape(two_d), f"adamw_{name}_{tag}")
        results.setdefault(name, {})[idx] = [o.reshape(shp) for o in out]

    for l in reversed(range(DEPTH)):
        kind, j = l % 3, l // 3
        lw, sv = layers[l], saved[l]
        p, q_col = sv['p'], sv['q_col']
        d_act = mm(dx, lw['w_down'], tb=True, name=f"d_ffn_act_{l}")
        dw_down = mm(sv['act'], dx, ta=True, out_dtype=BF16, name=f"dw_ffn_down_{l}")
        dug, duv, dcg, dcv = ffn_act_bwd(sv['u0'], lw['conv'], d_act, f"ffn_act_bwd_{l}")
        du0 = jnp.concatenate([dug, duv], axis=1)
        d_conv = jnp.concatenate([dcg, dcv], axis=1)
        d_hf = mm(du0, lw['w_up'], tb=True, name=f"d_ffn_norm_out_{l}")
        dw_up = mm(sv['hf'], du0, ta=True, out_dtype=BF16, col_shards=True, name=f"dw_ffn_up_{l}")
        ffn_pieces = [dw_up, dw_down.reshape(N_DEV, -1, D_MODEL)]
        if l == 0:
            *early, token = all_to_all_start(ffn_pieces, "exchange_start_0_ffn")
            d_hf = d_hf + token[0, 0]
            pending['ffn'] = tuple(early)
        dx1, rep_grads['ffn_norm'][l] = rmsnorm_bwd(sv['x1'], a['ffn_norm'][l], d_hf, dx, f"ffn_norm_bwd_{l}")
        dcat = mm(dx1, lw['w_out'], tb=True, name=f"d_cat_{l}")
        dw_out = mm(sv['cat'], dx1, ta=True, out_dtype=BF16, name=f"dw_out_{l}")
        dq_mem, d_mem_kv = mem_attn_bwd(p, q_col, sv['mem_kv'], dcat, f"mem_attn_bwd_{l}")
        small_grads = []
        if kind == 0:
            dq, dk, dv, rep_grads['a_sinks'][j] = swa_bwd(p, a['a_sinks'][j], dcat, f"swa_bwd_{l}")
            dp = jnp.concatenate([dq, dk, dv, dq_mem], axis=1)
        elif kind == 1:
            post = rwkv_post_bwd(sv['post_in'], rwkv_post_params(j), dcat, f"rwkv_post_bwd_{l}")
            scan = rwkv_scan_bwd(*sv['pre'][:6], sv['ck'], post[0], f"rwkv_scan_bwd_{l}")
            res = rwkv_pre_bwd(p, rwkv_params(j, lw), tuple(scan) + tuple(post[1:5]), f"rwkv_pre_bwd_{l}")
            dp_mix = shift_add(res[0], [res[1]], [1], BF16, f"rwkv_shift_bwd_{l}")
            dp = jnp.concatenate([dp_mix, dq_mem], axis=1)
            for n, val in zip(('b_mu', 'b_w0', 'b_a0', 'b_k_k', 'b_k_a'), (_b_unpad_cols(res[2]), res[3], res[5], res[8], res[9])):
                rep_grads[n] = val
            rep_grads.update(b_gn_g=post[5], b_gn_b=post[6], b_r_k=post[7])
            small_grads = [_cols_to_shards(res[4][:RWKV_DECAY_RANK]), _cols_to_shards(res[6][:RWKV_ICLR_RANK]),
                           _cols_to_shards(res[7])]
        else:
            d_o, dz, rep_grads['c_norm_g'] = gdn_post_bwd(sv['o'], p, gdn_post_params(j), dcat, f"gdn_post_bwd_{l}")
            chunk = gdn_chunk_bwd(*sv['pre'], sv['states'], d_o, f"gdn_chunk_bwd_{l}")
            res = gdn_pre_bwd(p, gdn_params(j, lw), chunk, f"gdn_pre_bwd_{l}")
            dqkv = shift_add(res[0], list(res[1:4]), [1, 2, 3], BF16, f"gdn_shift_bwd_{l}")
            dp = jnp.concatenate([dqkv, dz, dq_mem, res[4]], axis=1)
            rep_grads.update(c_a_log=res[6][:, :GDN_V_HEADS], c_dt_bias=res[7][:, :GDN_V_HEADS])
            small_grads = [_cols_to_shards(res[5])]
        d_h = mm(dp, lw['w_in'], tb=True, name=f"d_attn_norm_out_{l}")
        dw_in = mm(sv['h'], dp, ta=True, out_dtype=BF16, name=f"dw_in_{l}")
        dx, rep_grads['attn_norm'][l] = rmsnorm_bwd(sv['x'], a['attn_norm'][l], d_h, dx1, f"attn_norm_bwd_{l}")
        d_memn = mm(d_mem_kv, lw['w_kv'], tb=True, name=f"d_mem_norm_out_{l}")
        dw_kv = mm(sv['memn'], d_mem_kv, ta=True, out_dtype=BF16, name=f"dw_mem_kv_{l}")
        _, rep_grads['mem_norm'][l] = rmsnorm_bwd(mem, a['mem_norm'][l], d_memn, None, f"mem_norm_bwd_{l}")

        if kind == 1:
            dw_in = _b_unpad_cols(dw_in)
        elif kind == 2:
            dw_in = _c_unpad_cols(dw_in)
        pieces = [dw_kv.reshape(N_DEV, -1, 2 * MEM_WIDTH), dw_out.reshape(N_DEV, -1, D_MODEL), _cols_to_shards(dw_in),
                  _cols_to_shards(d_conv)] + small_grads + ([] if l == 0 else ffn_pieces)
        if l == 0:
            rep_vals = []
            for n in REPLICATED:
                gval = rep_grads[n]
                gval = jnp.stack(gval) if isinstance(gval, list) else gval
                rep_vals.append(gval.reshape(a[n].shape))
            part = _pack(rep_vals + [loss_part.reshape(1)])
            rep_gathered = all_gather_many([part], "gather_small_grads")[0]
            pieces, rep_gathered = lax.optimization_barrier((pieces, rep_gathered))
        send, recv, thru, lands, token = all_to_all_start(pieces, f"exchange_start_{l}")
        dx = dx + token[0, 0]
        pending[l] = (send, recv, thru, lands)

    for l in reversed(range(1, DEPTH)):
        exchanged[l] = all_to_all_wait(*pending.pop(l), dx, f"exchange_wait_{l}")
    for l in reversed(range(DEPTH)):
        kind, j = l % 3, l // 3
        if l == 0:
            keys = [(n, i) for n in ('w_ffn_up', 'w_ffn_down') for i in range(1, DEPTH)]
            done_above = lax.optimization_barrier(tuple(results[n][i][0] for n, i in keys))
            for (n, i), val in zip(keys, done_above):
                results[n][i][0] = val
            exchanged[0] = all_to_all_wait(*pending.pop(0), done_above[0], "exchange_wait_0")
            exchanged[0] += all_to_all_wait(*pending.pop('ffn'), done_above[0], "exchange_wait_0_ffn")
        got = exchanged[l]
        small = {1: ('b_w_decay_up', 'b_w_iclr_up', 'b_w_gate_up'), 2: ('c_conv',)}.get(kind, ())
        in_name = ('a_w_in', 'b_w_in', 'c_w_in')[kind]
        names = [('w_mem_kv', l), ('w_out', l), (in_name, j), ('ffn_conv', l)] + [(n, j) for n in small]
        names += [('w_ffn_up', l), ('w_ffn_down', l)]
        for (name, idx), pc in zip(names, got, strict=True):
            apply_adam(name, idx, pc, l)

    shapes = [a[n].shape for n in REPLICATED] + [(1,)]
    zero = jnp.zeros((1,), F32)
    packed = lambda pre: _pack([a[pre + n] for n in REPLICATED] + [zero])
    rep_out = adamw_sum(rep_gathered, packed(''), packed('m_'), packed('v_'), "adamw_replicated")
    rep_out = [_unpack(o, shapes) for o in rep_out]
    loss = rep_out[0][-1][0]
    for i, n in enumerate(REPLICATED):
        results[n] = [o[i] for o in rep_out]

    def leaf(name, which):
        r = results[name]
        if isinstance(r, dict):
            return jnp.stack([r[i][which] for i in range(len(r))])
        return r[which]

    outs = [loss, dx[None]]
    for which in range(4):
        outs += [leaf(n, which) for n in WEIGHTS]
    return tuple(outs)
```

```python
import functools

import numpy as np
import jax
import jax.numpy as jnp
from jax import lax
from jax.experimental import pallas as pl
from jax.experimental.pallas import tpu as pltpu
from jax.experimental.pallas import tpu_sc as plsc

F32, BF16 = jnp.float32, jnp.bfloat16
HI = lax.Precision.HIGHEST
V7X_VMEM_BYTES = 64 * 1024 * 1024
VMEM_LIMIT = V7X_VMEM_BYTES - 8 * 1024 * 1024
MM_TILE_BUDGET = 40 * 1024 * 1024
SUBLANES, LANES = 8, 128
N_DEV = 8

D_MODEL = 2048
DEPTH = 4
MIX_WIDTH = 1536
MEM_HEADS, MEM_HEAD_DIM, MEM_WIDTH = 4, 128, 512
NORM_EPS = 1e-6
SWA_HEAD_DIM, SWA_Q_HEADS, SWA_KV_HEADS, SWA_GROUP, SWA_BLOCK = 64, 24, 4, 6, 128
RWKV_HEADS, RWKV_HEAD_DIM, RWKV_GN_EPS = 24, 64, 64e-5
RWKV_DECAY_RANK, RWKV_ICLR_RANK, RWKV_GATE_RANK = 96, 96, 256
GDN_HEAD_DIM, GDN_V_HEADS, GDN_QK_HEADS, GDN_CONV, GDN_CHUNK = 128, 12, 6, 4, 64
GDN_QK_WIDTH = GDN_QK_HEADS * GDN_HEAD_DIM
GDN_CONV_WIDTH = 2 * GDN_QK_WIDTH + MIX_WIDTH
D_FF, FFN_CONV = 5632, 3
ADAM_LR, ADAM_B1, ADAM_B2, ADAM_EPS, ADAM_WD, ADAM_STEP = 0.001, 0.9, 0.999, 1e-08, 0.01, 10
MESH = pl.DeviceIdType.MESH


def _cp(sem=None):
    return pltpu.CompilerParams(dimension_semantics=sem, vmem_limit_bytes=VMEM_LIMIT)


def _tile(n, cands):
    for c in cands:
        if n % c == 0:
            return c
    return n


_DIMS = {'nn': (((1,), (0,)), ((), ())), 'nt': (((1,), (1,)), ((), ())), 'tn': (((0,), (0,)), ((), ()))}


def _split(x):
    hi = lax.bitcast_convert_type(lax.bitcast_convert_type(x, jnp.uint32) & jnp.uint32(0xFFFF0000), F32)
    return hi.astype(BF16), (x - hi).astype(BF16)


def _dot3_raw(a, b, form):
    one = lambda p, q: lax.dot_general(p, q, _DIMS[form], preferred_element_type=F32)
    (a_hi, a_lo), (b_hi, b_lo) = _split(a), _split(b)
    return one(a_hi, b_hi) + (one(a_hi, b_lo) + one(a_lo, b_hi))


@functools.partial(jax.custom_vjp, nondiff_argnums=(2,))
def _dot3(a, b, form):
    return _dot3_raw(a, b, form)


def _dot3_fwd(a, b, form):
    return _dot3_raw(a, b, form), (a, b)


def _dot3_bwd(form, saved, dc):
    a, b = saved
    if form == 'nn':
        return _dot3(dc, b, 'nt'), _dot3(a, dc, 'tn')
    if form == 'nt':
        return _dot3(dc, b, 'nn'), _dot3(dc, a, 'tn')
    return _dot3(b, dc, 'nt'), _dot3(a, dc, 'nn')


_dot3.defvjp(_dot3_fwd, _dot3_bwd)


def _dot(a, b):
    return _dot3(a, b, 'nn')


def _dot_nt(a, b):
    return _dot3(a, b, 'nt')


def _dot_tn(a, b):
    return _dot3(a, b, 'tn')


def _sigmoid(x):
    return 1.0 / (1.0 + jnp.exp(-x))


def _softplus(x):
    return jnp.maximum(x, 0.0) + jnp.log(1.0 + jnp.exp(-jnp.abs(x)))


def _silu(x):
    return x * _sigmoid(x)


def mm(a, b, *, ta=False, tb=False, res=None, out_dtype=F32, col_shards=False, name):
    (k_a, m) = a.shape if ta else a.shape[::-1]
    (k_b, n) = b.shape[::-1] if tb else b.shape
    assert k_a == k_b, (a.shape, b.shape, ta, tb)
    kdim = k_a
    tm = _tile(m, (1024, 512, 256))
    tn = _tile(n, (1024, 768, 512, 384, 256, 128))
    if col_shards:
        tn = n // N_DEV
        assert res is None and not tb and tn % LANES == 0

    def vmem_bytes(tk):
        tiles = tk * (tm * a.dtype.itemsize + tn * b.dtype.itemsize) + tm * tn * jnp.dtype(out_dtype).itemsize
        tiles += 0 if res is None else tm * tn * res.dtype.itemsize
        return 2 * tiles + (0 if tk == kdim else tm * tn * 4)

    tks = [t for t in (kdim, kdim // 2, kdim // 4, 1024, 512, 256, 128) if kdim % t == 0 and t % LANES == 0]
    tk = next(t for t in tks if vmem_bytes(t) <= MM_TILE_BUDGET)
    nk = kdim // tk
    dims = (((0 if ta else 1,), (1 if tb else 0,)), ((), ()))

    def body(*refs):
        a_ref, b_ref = refs[:2]
        r_ref = None if res is None else refs[2]
        o_ref = refs[2 if res is None else 3]
        part = lax.dot_general(a_ref[...].astype(BF16), b_ref[...].astype(BF16), dims, preferred_element_type=F32)

        def finish(total):
            o_ref[...] = (total if res is None else total + r_ref[...]).astype(o_ref.dtype)

        if nk == 1:
            finish(part)
            return
        acc, kk = refs[-1], pl.program_id(2)

        @pl.when(kk == 0)
        def _():
            acc[...] = part

        @pl.when((kk > 0) & (kk < nk - 1))
        def _():
            acc[...] += part

        @pl.when(kk == nk - 1)
        def _():
            finish(acc[...] + part)

    a_spec = pl.BlockSpec((tk, tm), lambda i, j, k: (k, i)) if ta else pl.BlockSpec((tm, tk), lambda i, j, k: (i, k))
    b_spec = pl.BlockSpec((tn, tk), lambda i, j, k: (j, k)) if tb else pl.BlockSpec((tk, tn), lambda i, j, k: (k, j))
    o_spec = pl.BlockSpec((tm, tn), lambda i, j, k: (i, j))
    o_shape = (m, n)
    if col_shards:
        o_spec = pl.BlockSpec((None, tm, tn), lambda i, j, k: (j, i, 0))
        o_shape = (N_DEV, m, tn)
    in_specs, args = [a_spec, b_spec], [a, b]
    if res is not None:
        in_specs.append(o_spec)
        args.append(res)
    return pl.pallas_call(
        body, grid=(m // tm, n // tn, nk), in_specs=in_specs, out_specs=o_spec,
        out_shape=jax.ShapeDtypeStruct(o_shape, out_dtype),
        scratch_shapes=[] if nk == 1 else [pltpu.VMEM((tm, tn), F32)],
        compiler_params=_cp(("parallel", "parallel", "arbitrary")), name=name)(*args)


def _coords():
    return lax.axis_index("x"), lax.axis_index("y"), lax.axis_index("c")


def _block_index(p):
    return 4 * p[0] + 2 * p[1] + p[2]


def _all_gather_body(x_refs, o_refs, send, recv, loc):
    n = len(x_refs)
    x, y, c = _coords()
    me, sib = (x, y, c), (x, y, 1 - c)
    chips = [(1 - x, y), (x, 1 - y), (1 - x, 1 - y)]

    def cp(i, k, block, to, src=None):
        dst = o_refs[i].at[_block_index(block)]
        return pltpu.make_async_remote_copy(
            src_ref=dst if src is None else src, dst_ref=dst, send_sem=send.at[i, k], recv_sem=recv.at[i, k],
            device_id=to, device_id_type=MESH)

    mine = [pltpu.make_async_copy(x_refs[i], o_refs[i].at[_block_index(me)], loc.at[i]) for i in range(n)]
    for m_ in mine:
        m_.start()
    first = []
    for i in range(n):
        first.append(cp(i, 0, me, sib, src=x_refs[i]))
        for j, chip in enumerate(chips):
            first.append(cp(i, 1 + j, me, (*chip, c), src=x_refs[i]))
    for f in first:
        f.start()
    passed = []
    for j, chip in enumerate(chips):
        for i in range(n):
            cp(i, 1 + j, (*chip, c), me).wait_recv()
            p = cp(i, 4 + j, (*chip, c), sib)
            p.start()
            passed.append(p)
    for i in range(n):
        cp(i, 0, sib, me).wait_recv()
        for j, chip in enumerate(chips):
            cp(i, 4 + j, (*chip, 1 - c), me).wait_recv()
    for f in first + passed:
        f.wait_send()
    for m_ in mine:
        m_.wait()


def _all_gather_peers():
    x, y, c = _coords()
    return [(x, y, 1 - c), (1 - x, y, c), (x, 1 - y, c), (1 - x, 1 - y, c)]


def _all_to_all_peers():
    x, y, c = _coords()
    return [(1 - x if r & 4 else x, 1 - y if r & 2 else y, 1 - c if r & 1 else c) for r in range(1, N_DEV)]


def _comm_scratch(n):
    return [pltpu.SemaphoreType.DMA((n, 7)), pltpu.SemaphoreType.DMA((n, 7)), pltpu.SemaphoreType.DMA((n,))]


def all_gather_many(xs, name):
    n = len(xs)

    def body(*refs):
        _all_gather_body(refs[:n], refs[n:2 * n], *refs[2 * n:])

    any_spec = pl.BlockSpec(memory_space=pl.ANY)
    return pl.pallas_call(
        body, in_specs=[any_spec] * n, out_specs=[any_spec] * n,
        out_shape=[jax.ShapeDtypeStruct((N_DEV,) + x.shape, x.dtype) for x in xs],
        scratch_shapes=_comm_scratch(n), name=name)(*xs)


def _on_sequencer(exchange, peers, xs, out_shapes, name, collective_id):
    x_refs = [jax.new_ref(x, memory_space=pltpu.MemorySpace.HBM) for x in xs]
    o_refs = [jax.empty_ref(s, memory_space=pltpu.MemorySpace.HBM) for s in out_shapes]

    @pl.kernel(mesh=plsc.ScalarSubcoreMesh(axis_name="sequencer", num_cores=1), name=name,
               scratch_types=tuple(_comm_scratch(len(xs))),
               compiler_params=pltpu.CompilerParams(collective_id=collective_id))
    def launch(send, recv, loc):
        barrier = pltpu.get_barrier_semaphore()
        ids = peers()
        for peer in ids:
            pl.semaphore_signal(barrier, inc=1, device_id=peer, device_id_type=MESH)
        pl.semaphore_wait(barrier, len(ids))
        exchange(x_refs, o_refs, send, recv, loc)

    launch()
    return [o[...] for o in o_refs]


def _all_to_all_copies(x_refs, o_refs, send, recv, arrivals):
    x, y, c = _coords()
    me = _block_index((x, y, c))
    copies = []
    for r, peer in enumerate(_all_to_all_peers()):
        pidx = _block_index(peer)
        for i in range(len(x_refs)):
            copies.append(pltpu.make_async_remote_copy(
                src_ref=x_refs[i].at[pidx], dst_ref=o_refs[i].at[pidx if arrivals else me],
                send_sem=send.at[7 * i + r], recv_sem=recv.at[7 * i + r], device_id=peer, device_id_type=MESH))
    return copies


def all_to_all_start(xs, name):
    n = len(xs)
    lands = [lax.empty(v.shape, v.dtype) for v in xs]

    def body(*refs):
        x_refs, o_refs = refs[:n], refs[n:2 * n]
        send, recv = refs[2 * n], refs[2 * n + 1]
        token = refs[-1]
        for s in _all_to_all_copies(x_refs, o_refs, send, recv, arrivals=False):
            s.start()
        token[...] = jnp.zeros_like(token)

    hbm = pl.BlockSpec(memory_space=pltpu.HBM)
    sem = pl.BlockSpec(memory_space=pltpu.SEMAPHORE)
    out = pl.pallas_call(
        body, name=name,
        out_shape=(pltpu.SemaphoreType.DMA((7 * n,)), pltpu.SemaphoreType.DMA((7 * n,)),
                   *[pltpu.HBM(v.shape, v.dtype) for v in xs], *[pltpu.HBM(v.shape, v.dtype) for v in xs],
                   jax.ShapeDtypeStruct((SUBLANES, LANES), F32)),
        in_specs=[hbm] * (2 * n), out_specs=(sem, sem, *([hbm] * (2 * n)), pl.BlockSpec(memory_space=pltpu.VMEM)),
        input_output_aliases={i: 2 + i for i in range(2 * n)},
        compiler_params=pltpu.CompilerParams(has_side_effects=pltpu.SideEffectType.DATAFLOW_SIDE_EFFECTING),
    )(*[pltpu.with_memory_space_constraint(v, pltpu.HBM) for v in xs],
      *[pltpu.with_memory_space_constraint(v, pltpu.HBM) for v in lands])
    return out[0], out[1], list(out[2:2 + n]), list(out[2 + n:2 + 2 * n]), out[-1]


def all_to_all_wait(send, recv, xs, lands, after, name):
    n = len(xs)

    def body(*refs):
        x_refs, o_refs = refs[:n], refs[n:2 * n]
        send_ref, recv_ref = refs[2 * n], refs[2 * n + 1]
        for s in _all_to_all_copies(x_refs, o_refs, send_ref, recv_ref, arrivals=False):
            s.wait_send()
        for w in _all_to_all_copies(x_refs, o_refs, send_ref, recv_ref, arrivals=True):
            w.wait_recv()

    hbm = pl.BlockSpec(memory_space=pltpu.HBM)
    sem = pl.BlockSpec(memory_space=pltpu.SEMAPHORE)
    out = pl.pallas_call(
        body, name=name,
        out_shape=tuple(pltpu.HBM(v.shape, v.dtype) for v in list(xs) + list(lands)),
        in_specs=[hbm] * (2 * n) + [sem, sem, pl.BlockSpec(memory_space=pl.ANY)], out_specs=tuple([hbm] * (2 * n)),
        input_output_aliases={i: i for i in range(2 * n)},
        compiler_params=pltpu.CompilerParams(has_side_effects=pltpu.SideEffectType.DATAFLOW_SIDE_EFFECTING),
    )(*xs, *lands, send, recv, after)
    return [(sent, landed) for sent, landed in zip(out[:n], out[n:])]


def all_gather_many_async(xs, name, collective_id):
    shapes = [jax.ShapeDtypeStruct((N_DEV,) + x.shape, x.dtype) for x in xs]
    return _on_sequencer(_all_gather_body, _all_gather_peers, xs, shapes, name, collective_id)


def adamw_sum(pieces, w, m, v, name, own=None, mine=None):
    rows, cols = w.shape
    tr = _tile(rows, (128, 64, 32, 16, 8))
    c1 = 1.0 - ADAM_B1 ** ADAM_STEP
    c2 = 1.0 - ADAM_B2 ** ADAM_STEP

    def body(*refs):
        if own is None:
            p_ref, w_ref, m_ref, v_ref, g_out, d_out, m_out, v_out = refs
            slot = lambda s: p_ref[s].astype(F32)
        else:
            p_ref, own_ref, mine_ref, w_ref, m_ref, v_ref, g_out, d_out, m_out, v_out = refs
            slot = lambda s: jnp.where(mine_ref[s, 0:1, 0:1] > 0.5, own_ref[...], p_ref[s]).astype(F32)
        g = slot(0)
        for s in range(1, N_DEV):
            g = g + slot(s)
        m_new = ADAM_B1 * m_ref[...] + (1.0 - ADAM_B1) * g
        v_new = ADAM_B2 * v_ref[...] + (1.0 - ADAM_B2) * (g * g)
        m_hat = m_new / c1
        v_hat = v_new / c2
        g_out[...] = g
        d_out[...] = -ADAM_LR * (m_hat / (jnp.sqrt(v_hat) + ADAM_EPS) + ADAM_WD * w_ref[...])
        m_out[...] = m_new
        v_out[...] = v_new

    spec = pl.BlockSpec((tr, cols), lambda i: (i, 0))
    slots = pl.BlockSpec((N_DEV, tr, cols), lambda i: (0, i, 0))
    out = jax.ShapeDtypeStruct((rows, cols), F32)
    extra_specs = [] if own is None else [spec, pl.BlockSpec((N_DEV, SUBLANES, LANES), lambda i: (0, 0, 0))]
    extra = [] if own is None else [own, mine]
    return pl.pallas_call(
        body, grid=(rows // tr,), in_specs=[slots] + extra_specs + [spec, spec, spec],
        out_specs=[spec] * 4, out_shape=[out] * 4, compiler_params=_cp(("parallel",)),
        name=name)(pieces, *extra, w, m, v)


def rmsnorm_fwd(x, g, out_dtype, name):
    s, d = x.shape
    tr = _tile(s, (256, 128, 64, 32, 16))

    def body(x_ref, g_ref, o_ref):
        xv = x_ref[...]
        rstd = lax.rsqrt(jnp.mean(xv * xv, axis=-1, keepdims=True) + NORM_EPS)
        o_ref[...] = (xv * rstd * g_ref[...]).astype(o_ref.dtype)

    return pl.pallas_call(
        body, grid=(s // tr,), in_specs=[pl.BlockSpec((tr, d), lambda i: (i, 0)), pl.BlockSpec((1, d), lambda i: (0, 0))],
        out_specs=pl.BlockSpec((tr, d), lambda i: (i, 0)), out_shape=jax.ShapeDtypeStruct((s, d), out_dtype),
        compiler_params=_cp(("parallel",)), name=name)(x, g.reshape(1, d))


def rmsnorm_bwd(x, g, dh, dres, name):
    s, d = x.shape
    tr = _tile(s, (256, 128, 64, 32, 16))

    def body(*refs):
        if dres is None:
            x_ref, g_ref, dh_ref, dx_ref, dg_ref = refs
        else:
            x_ref, g_ref, dh_ref, dr_ref, dx_ref, dg_ref = refs
        xv = x_ref[...]
        rstd = lax.rsqrt(jnp.mean(xv * xv, axis=-1, keepdims=True) + NORM_EPS)
        xhat = xv * rstd
        dhv = dh_ref[...].astype(F32)
        dhg = dhv * g_ref[...]
        dx = rstd * (dhg - xhat * jnp.mean(dhg * xhat, axis=-1, keepdims=True))
        if dres is not None:
            dx = dx + dr_ref[...]
        dx_ref[...] = dx

        @pl.when(pl.program_id(0) == 0)
        def _():
            dg_ref[...] = jnp.zeros_like(dg_ref)

        dg_ref[...] += jnp.sum(dhv * xhat, axis=0, keepdims=True)

    row = pl.BlockSpec((tr, d), lambda i: (i, 0))
    vec = pl.BlockSpec((1, d), lambda i: (0, 0))
    ins = [x, g.reshape(1, d), dh] + ([] if dres is None else [dres])
    dx, dg = pl.pallas_call(
        body, grid=(s // tr,), in_specs=[row, vec, row] + ([] if dres is None else [row]), out_specs=[row, vec],
        out_shape=[jax.ShapeDtypeStruct((s, d), F32), jax.ShapeDtypeStruct((1, d), F32)],
        compiler_params=_cp(("arbitrary",)), name=name)(*ins)
    return dx, dg.reshape(d)


def final_loss(x, g, target, name):
    s, d = x.shape
    tr = _tile(s, (256, 128, 64, 32, 16))

    def body(x_ref, g_ref, t_ref, l_ref, dx_ref, dg_ref):
        xv = x_ref[...]
        rstd = lax.rsqrt(jnp.mean(xv * xv, axis=-1, keepdims=True) + NORM_EPS)
        xhat = xv * rstd
        err = xhat * g_ref[...] - t_ref[...]
        dy = err * (1.0 / d)
        dhg = dy * g_ref[...]
        dx_ref[...] = rstd * (dhg - xhat * jnp.mean(dhg * xhat, axis=-1, keepdims=True))

        @pl.when(pl.program_id(0) == 0)
        def _():
            dg_ref[...] = jnp.zeros_like(dg_ref)
            l_ref[...] = jnp.zeros_like(l_ref)

        dg_ref[...] += jnp.sum(dy * xhat, axis=0, keepdims=True)
        part = 0.5 * jnp.sum(jnp.mean(err * err, axis=-1, keepdims=True), axis=0, keepdims=True)
        l_ref[...] += jnp.broadcast_to(part, l_ref.shape)

    row = pl.BlockSpec((tr, d), lambda i: (i, 0))
    vec = pl.BlockSpec((1, d), lambda i: (0, 0))
    lspec = pl.BlockSpec((1, LANES), lambda i: (0, 0))
    loss, dx, dg = pl.pallas_call(
        body, grid=(s // tr,), in_specs=[row, vec, row], out_specs=[lspec, row, vec],
        out_shape=[jax.ShapeDtypeStruct((1, LANES), F32), jax.ShapeDtypeStruct((s, d), F32),
                   jax.ShapeDtypeStruct((1, d), F32)],
        compiler_params=_cp(("arbitrary",)), name=name)(x, g.reshape(1, d), target)
    return loss[0, 0], dx, dg.reshape(d)


def _shift_down(tile, prev8, j):
    if j == 0:
        return tile
    rt = pltpu.roll(tile, j, 0)
    rp = pltpu.roll(prev8, j, 0)
    rows = lax.broadcasted_iota(jnp.int32, prev8.shape, 0)
    top = jnp.where(rows < j, rp, rt[:SUBLANES])
    return jnp.concatenate([top, rt[SUBLANES:]], axis=0)


def _shift_up(tile, next8, j):
    if j == 0:
        return tile
    t = tile.shape[0]
    rt = pltpu.roll(tile, t - j, 0)
    rn = pltpu.roll(next8, SUBLANES - j, 0)
    rows = lax.broadcasted_iota(jnp.int32, next8.shape, 0)
    bot = jnp.where(rows >= SUBLANES - j, rn, rt[t - SUBLANES:])
    return jnp.concatenate([rt[:t - SUBLANES], bot], axis=0)


def _halo_specs(t_rows, s_rows, cols, col_of):
    per, last = t_rows // SUBLANES, s_rows // SUBLANES - 1
    prev = pl.BlockSpec((SUBLANES, cols), lambda j, i: (jnp.maximum(i * per - 1, 0), col_of(j)))
    nxt = pl.BlockSpec((SUBLANES, cols), lambda j, i: (jnp.minimum((i + 1) * per, last), col_of(j)))
    return prev, nxt


def ffn_act_fwd(u0, conv, name):
    s, two_f = u0.shape
    f = two_f // 2
    t, c = _tile(s, (256, 128, 64)), 512
    nc = f // c

    def body(g_ref, v_ref, gp_ref, vp_ref, wg_ref, wv_ref, a_ref):
        first = (pl.program_id(1) > 0).astype(F32)

        def conv_of(x_ref, p_ref, w_ref):
            x, p = x_ref[...], p_ref[...] * first
            return (w_ref[0:1, :] * _shift_down(x, p, 2) + w_ref[1:2, :] * _shift_down(x, p, 1) + w_ref[2:3, :] * x)

        ug = conv_of(g_ref, gp_ref, wg_ref)
        uv = conv_of(v_ref, vp_ref, wv_ref)
        a_ref[...] = (_silu(ug) * uv).astype(a_ref.dtype)

    gate = pl.BlockSpec((t, c), lambda j, i: (i, j))
    val = pl.BlockSpec((t, c), lambda j, i: (i, j + nc))
    gp, _ = _halo_specs(t, s, c, lambda j: j)
    vp, _ = _halo_specs(t, s, c, lambda j: j + nc)
    wg = pl.BlockSpec((FFN_CONV, c), lambda j, i: (0, j))
    wv = pl.BlockSpec((FFN_CONV, c), lambda j, i: (0, j + nc))
    return pl.pallas_call(
        body, grid=(nc, s // t), in_specs=[gate, val, gp, vp, wg, wv], out_specs=pl.BlockSpec((t, c), lambda j, i: (i, j)),
        out_shape=jax.ShapeDtypeStruct((s, f), BF16), compiler_params=_cp(("parallel", "parallel")),
        name=name)(u0, u0, u0, u0, conv, conv)


def ffn_act_bwd(u0, conv, da, name):
    s, two_f = u0.shape
    f = two_f // 2
    t, c = _tile(s, (256, 128, 64)), 512
    nc, nt = f // c, s // t

    def body(g_ref, v_ref, gp_ref, vp_ref, gn_ref, vn_ref, wg_ref, wv_ref, da_ref, dan_ref,
             dg_ref, dv_ref, dwg_ref, dwv_ref):
        i = pl.program_id(1)
        first, last = (i > 0).astype(F32), (i < nt - 1).astype(F32)
        zeros8 = jnp.zeros((SUBLANES, c), F32)

        def ext(x_ref, p_ref, n_ref):
            return jnp.concatenate([p_ref[...] * first, x_ref[...], n_ref[...] * last], axis=0)

        def taps(e):
            return pltpu.roll(e, 2, 0), pltpu.roll(e, 1, 0), e

        def conv_of(sh, w_ref):
            return w_ref[0:1, :] * sh[0] + w_ref[1:2, :] * sh[1] + w_ref[2:3, :] * sh[2]

        def conv_t(du, w_ref):
            n = du.shape[0]
            return w_ref[2:3, :] * du + w_ref[1:2, :] * pltpu.roll(du, n - 1, 0) + w_ref[0:1, :] * pltpu.roll(du, n - 2, 0)

        sg, sv = taps(ext(g_ref, gp_ref, gn_ref)), taps(ext(v_ref, vp_ref, vn_ref))
        ug, uv = conv_of(sg, wg_ref), conv_of(sv, wv_ref)
        dae = jnp.concatenate([zeros8, da_ref[...], dan_ref[...] * last], axis=0)
        sig = _sigmoid(ug)
        dug = dae * uv * (sig * (1.0 + ug * (1.0 - sig)))
        duv = dae * (ug * sig)
        dg_ref[...] = conv_t(dug, wg_ref)[SUBLANES:t + SUBLANES].astype(dg_ref.dtype)
        dv_ref[...] = conv_t(duv, wv_ref)[SUBLANES:t + SUBLANES].astype(dv_ref.dtype)

        @pl.when(i == 0)
        def _():
            dwg_ref[...] = jnp.zeros_like(dwg_ref)
            dwv_ref[...] = jnp.zeros_like(dwv_ref)

        def dconv(du, sh):
            d = du[SUBLANES:t + SUBLANES]
            return jnp.concatenate([jnp.sum(d * x[SUBLANES:t + SUBLANES], axis=0, keepdims=True) for x in sh], axis=0)

        dwg_ref[...] += dconv(dug, sg)
        dwv_ref[...] += dconv(duv, sv)

    gate = pl.BlockSpec((t, c), lambda j, i: (i, j))
    val = pl.BlockSpec((t, c), lambda j, i: (i, j + nc))
    gp, gn = _halo_specs(t, s, c, lambda j: j)
    vp, vn = _halo_specs(t, s, c, lambda j: j + nc)
    wg = pl.BlockSpec((FFN_CONV, c), lambda j, i: (0, j))
    wv = pl.BlockSpec((FFN_CONV, c), lambda j, i: (0, j + nc))
    wout = pl.BlockSpec((FFN_CONV, c), lambda j, i: (0, j))
    half = jax.ShapeDtypeStruct((s, f), BF16)
    dwh = jax.ShapeDtypeStruct((FFN_CONV, f), F32)
    return pl.pallas_call(
        body, grid=(nc, nt), in_specs=[gate, val, gp, vp, gn, vn, wg, wv, gate, gn],
        out_specs=[gate, gate, wout, wout], out_shape=[half, half, dwh, dwh],
        compiler_params=_cp(("parallel", "arbitrary")), name=name)(u0, u0, u0, u0, u0, u0, conv, conv, da, da)


def _softmax_rows(s, extra=None):
    m = jnp.max(s, axis=-1, keepdims=True)
    if extra is not None:
        m = jnp.maximum(m, extra)
    m = lax.stop_gradient(m)
    e = jnp.exp(s - m)
    den = jnp.sum(e, axis=-1, keepdims=True)
    if extra is not None:
        den = den + jnp.exp(extra - m)
    return e / den


def _mem_attn_fn(qs, ks, vs):
    outs = []
    for q, k, v in zip(qs, ks, vs):
        p = _softmax_rows(_dot_nt(q, k) * (MEM_HEAD_DIM ** -0.5))
        outs.append(_dot(p, v))
    return outs


def _mem_heads(q_ref, kv_ref):
    d = MEM_HEAD_DIM
    qs = [q_ref[:, h * d:(h + 1) * d] for h in range(MEM_HEADS)]
    ks = [kv_ref[:, h * d:(h + 1) * d] for h in range(MEM_HEADS)]
    vs = [kv_ref[:, MEM_WIDTH + h * d:MEM_WIDTH + (h + 1) * d] for h in range(MEM_HEADS)]
    return qs, ks, vs


def mem_attn_fwd(p, q_col, kv, name):
    s = p.shape[0]
    t = _tile(s, (256, 128))
    m = kv.shape[0]

    def body(q_ref, kv_ref, o_ref):
        outs = _mem_attn_fn(*_mem_heads(q_ref, kv_ref))
        o_ref[...] = jnp.concatenate(outs, axis=1).astype(o_ref.dtype)

    return pl.pallas_call(
        body, grid=(s // t,),
        in_specs=[pl.BlockSpec((t, MEM_WIDTH), lambda i: (i, q_col // MEM_WIDTH)),
                  pl.BlockSpec((m, 2 * MEM_WIDTH), lambda i: (0, 0))],
        out_specs=pl.BlockSpec((t, MEM_WIDTH), lambda i: (i, 0)), out_shape=jax.ShapeDtypeStruct((s, MEM_WIDTH), BF16),
        compiler_params=_cp(("parallel",)), name=name)(p, kv)


def mem_attn_bwd(p, q_col, kv, dcat, name):
    s = p.shape[0]
    t = _tile(s, (256, 128))
    m = kv.shape[0]
    d = MEM_HEAD_DIM

    def body(q_ref, kv_ref, dy_ref, dq_ref, dkv_ref):
        qs, ks, vs = _mem_heads(q_ref, kv_ref)
        _, vjp = jax.vjp(_mem_attn_fn, qs, ks, vs)
        dqs, dks, dvs = vjp([dy_ref[:, h * d:(h + 1) * d] for h in range(MEM_HEADS)])
        dq_ref[...] = jnp.concatenate(dqs, axis=1).astype(dq_ref.dtype)

        @pl.when(pl.program_id(0) == 0)
        def _():
            dkv_ref[...] = jnp.zeros_like(dkv_ref)

        dkv_ref[...] += jnp.concatenate(dks + dvs, axis=1)

    return pl.pallas_call(
        body, grid=(s // t,),
        in_specs=[pl.BlockSpec((t, MEM_WIDTH), lambda i: (i, q_col // MEM_WIDTH)),
                  pl.BlockSpec((m, 2 * MEM_WIDTH), lambda i: (0, 0)),
                  pl.BlockSpec((t, MEM_WIDTH), lambda i: (i, MIX_WIDTH // MEM_WIDTH))],
        out_specs=[pl.BlockSpec((t, MEM_WIDTH), lambda i: (i, 0)), pl.BlockSpec((m, 2 * MEM_WIDTH), lambda i: (0, 0))],
        out_shape=[jax.ShapeDtypeStruct((s, MEM_WIDTH), BF16), jax.ShapeDtypeStruct((m, 2 * MEM_WIDTH), F32)],
        compiler_params=_cp(("arbitrary",)), name=name)(p, kv, dcat)


def _swa_fn(qs, kcs, kps, vcs, vps, sinks, not_first):
    t = SWA_BLOCK
    qi = lax.broadcasted_iota(jnp.int32, (t, 2 * t), 0)
    kj = lax.broadcasted_iota(jnp.int32, (t, 2 * t), 1)
    dist = t + qi - kj
    valid = (dist >= 0) & (dist < t) & ((kj >= t) | not_first)
    distf = dist.astype(F32)
    outs = []
    for kh in range(SWA_KV_HEADS):
        kb = jnp.concatenate([kps[kh], kcs[kh]], axis=0)
        vb = jnp.concatenate([vps[kh], vcs[kh]], axis=0)
        for g in range(SWA_GROUP):
            h = kh * SWA_GROUP + g
            slope = 2.0 ** (-8.0 * (h + 1) / SWA_Q_HEADS)
            sc = _dot_nt(qs[h], kb) * (SWA_HEAD_DIM ** -0.5) - slope * distf
            sc = jnp.where(valid, sc, -jnp.inf)
            outs.append(_dot(_softmax_rows(sc, extra=sinks[h]), vb))
    return outs


def _swa_args(q_ref, kc_ref, kp_ref, vc_ref, vp_ref, sink_ref):
    d = SWA_HEAD_DIM
    qs = [q_ref[:, h * d:(h + 1) * d] for h in range(SWA_Q_HEADS)]
    per_kv = lambda ref: [ref[:, h * d:(h + 1) * d] for h in range(SWA_KV_HEADS)]
    sinks = [sink_ref[0:1, h:h + 1] for h in range(SWA_Q_HEADS)]
    return qs, per_kv(kc_ref), per_kv(kp_ref), per_kv(vc_ref), per_kv(vp_ref), sinks


def _swa_specs(nb, order):
    t, kvw = SWA_BLOCK, SWA_KV_HEADS * SWA_HEAD_DIM
    k_col, v_col = MIX_WIDTH // kvw, MIX_WIDTH // kvw + 1
    q = pl.BlockSpec((t, MIX_WIDTH), lambda n: (order(n), 0))
    kc = pl.BlockSpec((t, kvw), lambda n: (order(n), k_col))
    kp = pl.BlockSpec((t, kvw), lambda n: (jnp.maximum(order(n) - 1, 0), k_col))
    vc = pl.BlockSpec((t, kvw), lambda n: (order(n), v_col))
    vp = pl.BlockSpec((t, kvw), lambda n: (jnp.maximum(order(n) - 1, 0), v_col))
    sink = pl.BlockSpec((1, LANES), lambda n: (0, 0))
    return [q, kc, kp, vc, vp, sink]


def _pad_lanes(v):
    return jnp.pad(v.reshape(1, -1), ((0, 0), (0, LANES - v.size)))


def swa_fwd(p, sinks, name):
    s = p.shape[0]
    nb = s // SWA_BLOCK

    def body(q_ref, kc_ref, kp_ref, vc_ref, vp_ref, sink_ref, o_ref):
        outs = _swa_fn(*_swa_args(q_ref, kc_ref, kp_ref, vc_ref, vp_ref, sink_ref), pl.program_id(0) > 0)
        o_ref[...] = jnp.concatenate(outs, axis=1).astype(o_ref.dtype)

    return pl.pallas_call(
        body, grid=(nb,), in_specs=_swa_specs(nb, lambda n: n),
        out_specs=pl.BlockSpec((SWA_BLOCK, MIX_WIDTH), lambda n: (n, 0)),
        out_shape=jax.ShapeDtypeStruct((s, MIX_WIDTH), BF16), compiler_params=_cp(("parallel",)),
        name=name)(p, p, p, p, p, _pad_lanes(sinks))


def swa_bwd(p, sinks, dcat, name):
    s = p.shape[0]
    nb = s // SWA_BLOCK
    t, d, kvw = SWA_BLOCK, SWA_HEAD_DIM, SWA_KV_HEADS * SWA_HEAD_DIM
    rev = lambda n: nb - 1 - n

    def body(q_ref, kc_ref, kp_ref, vc_ref, vp_ref, sink_ref, dy_ref, dq_ref, dk_ref, dv_ref, ds_ref, ck, cv):
        n = pl.program_id(0)

        @pl.when(n == 0)
        def _():
            ck[...] = jnp.zeros_like(ck)
            cv[...] = jnp.zeros_like(cv)
            ds_ref[...] = jnp.zeros_like(ds_ref)

        args = _swa_args(q_ref, kc_ref, kp_ref, vc_ref, vp_ref, sink_ref)
        _, vjp = jax.vjp(functools.partial(_swa_fn, not_first=rev(n) > 0), *args)
        dqs, dkcs, dkps, dvcs, dvps, dsinks = vjp([dy_ref[:, h * d:(h + 1) * d] for h in range(SWA_Q_HEADS)])
        dq_ref[...] = jnp.concatenate(dqs, axis=1).astype(dq_ref.dtype)
        dk_ref[...] = (jnp.concatenate(dkcs, axis=1) + ck[...]).astype(dk_ref.dtype)
        dv_ref[...] = (jnp.concatenate(dvcs, axis=1) + cv[...]).astype(dv_ref.dtype)
        ck[...] = jnp.concatenate(dkps, axis=1)
        cv[...] = jnp.concatenate(dvps, axis=1)
        lane = lax.broadcasted_iota(jnp.int32, (1, LANES), 1)
        acc = jnp.zeros((1, LANES), F32)
        for h in range(SWA_Q_HEADS):
            acc = acc + jnp.where(lane == h, dsinks[h], 0.0)
        ds_ref[...] += acc

    dy = pl.BlockSpec((t, MIX_WIDTH), lambda n: (rev(n), 0))
    kv_out = pl.BlockSpec((t, kvw), lambda n: (rev(n), 0))
    dq, dk, dv, ds = pl.pallas_call(
        body, grid=(nb,), in_specs=_swa_specs(nb, rev) + [dy],
        out_specs=[dy, kv_out, kv_out, pl.BlockSpec((1, LANES), lambda n: (0, 0))],
        out_shape=[jax.ShapeDtypeStruct((s, MIX_WIDTH), BF16), jax.ShapeDtypeStruct((s, kvw), BF16),
                   jax.ShapeDtypeStruct((s, kvw), BF16), jax.ShapeDtypeStruct((1, LANES), F32)],
        scratch_shapes=[pltpu.VMEM((t, kvw), F32), pltpu.VMEM((t, kvw), F32)],
        compiler_params=_cp(("arbitrary",)), name=name)(p, p, p, p, p, _pad_lanes(sinks), dcat)
    return dq, dk, dv, ds[0, :SWA_Q_HEADS]


RW_SHIFT = 5120
RW_R, RW_K, RW_V, RW_WD, RW_AD, RW_GD = 0, 1536, 3072, 4608, 4736, 4864


def _head_matrix(width, head_dim):
    e = (np.arange(width)[:, None] // head_dim == np.arange(LANES)[None, :]).astype(np.float32)
    return jnp.asarray(e), jnp.asarray(e.T)


def _rwkv_pre_fn(pieces, shifted, mus, w0, wdu, a0, wiu, wgu, k_k, k_a, e, et):
    r, k, v, wd, ad, gd = [p + (s - p) * mu for p, s, mu in zip(pieces, shifted, mus)]
    w_log = -_softplus(-(w0 + _dot(jnp.tanh(wd), wdu))) - 0.5
    lw = -jnp.exp(w_log)
    a = _sigmoid(a0 + _dot(ad, wiu))
    g = _dot(_sigmoid(gd), wgu)
    kkr = k * k_k
    kk = kkr * _dot(lax.rsqrt(_dot(kkr * kkr, e) + 1e-6), et)
    k2 = k * (1.0 + (a - 1.0) * k_a)
    return r, lw, k2, v, kk, kk * a, g


_RW_GROUPS = ((RW_R, MIX_WIDTH), (RW_K, MIX_WIDTH), (RW_V, MIX_WIDTH), (RW_WD, LANES), (RW_AD, LANES), (RW_GD, 2 * LANES))


def _rwkv_pre_inputs(p_ref, prev_ref, mu_ref, first):
    pieces = [p_ref[:, o:o + n] for o, n in _RW_GROUPS]
    shifted = [_shift_down(p_ref[:, o:o + n], prev_ref[:, o:o + n] * first, 1) for o, n in _RW_GROUPS]
    mus = [mu_ref[:, o:o + n] for o, n in _RW_GROUPS]
    return pieces, shifted, mus


def _rwkv_param_specs():
    vec = lambda n: pl.BlockSpec((1, n), lambda i: (0, 0))
    mat = lambda r, c: pl.BlockSpec((r, c), lambda i: (0, 0))
    return [vec(RW_SHIFT), vec(MIX_WIDTH), mat(LANES, MIX_WIDTH), vec(MIX_WIDTH), mat(LANES, MIX_WIDTH),
            mat(2 * LANES, MIX_WIDTH), vec(MIX_WIDTH), vec(MIX_WIDTH), mat(MIX_WIDTH, LANES), mat(LANES, MIX_WIDTH)]


def rwkv_pre_fwd(p, params, name):
    s = p.shape[0]
    t = _tile(s, (128, 64))

    def body(p_ref, prev_ref, mu_ref, *rest):
        prm, outs = rest[:9], rest[9:]
        first = (pl.program_id(0) > 0).astype(F32)
        pieces, shifted, mus = _rwkv_pre_inputs(p_ref, prev_ref, mu_ref, first)
        res = _rwkv_pre_fn(pieces, shifted, mus, *[q[...] for q in prm])
        for o_ref, val in zip(outs, res):
            o_ref[...] = val

    row = pl.BlockSpec((t, RW_SHIFT), lambda i: (i, 0))
    prev = pl.BlockSpec((SUBLANES, RW_SHIFT), lambda i: (jnp.maximum(i * (t // SUBLANES) - 1, 0), 0))
    out = pl.BlockSpec((t, MIX_WIDTH), lambda i: (i, 0))
    return pl.pallas_call(
        body, grid=(s // t,), in_specs=[row, prev] + _rwkv_param_specs(), out_specs=[out] * 7,
        out_shape=[jax.ShapeDtypeStruct((s, MIX_WIDTH), F32)] * 7, compiler_params=_cp(("parallel",)),
        name=name)(p, p, *params)


def rwkv_pre_bwd(p, params, cots, name):
    s = p.shape[0]
    t = _tile(s, (64, 32))

    def body(p_ref, prev_ref, mu_ref, *rest):
        prm, cot, outs = rest[:9], rest[9:19], rest[19:]
        dp_ref, dps_ref, grads = outs[0], outs[1], outs[2:]
        i = pl.program_id(0)
        first = (i > 0).astype(F32)
        pieces, shifted, mus = _rwkv_pre_inputs(p_ref, prev_ref, mu_ref, first)
        prm_v = [q[...] for q in prm]
        fn = lambda pieces, shifted, mus, *small: _rwkv_pre_fn(pieces, shifted, mus, *small, prm_v[7], prm_v[8])
        _, vjp = jax.vjp(fn, pieces, shifted, mus, *prm_v[:7])
        dr, dw, dk2, dv, dkk, db, dr2, dk22, dv2, dg = [c[...] for c in cot]
        res = vjp((dr + dr2, dw, dk2 + dk22, dv + dv2, dkk, db, dg))
        dpieces, dshifted, dmus, dsmall = res[0], res[1], res[2], res[3:]
        for (o, n), dpi, dsi in zip(_RW_GROUPS, dpieces, dshifted):
            dp_ref[:, o:o + n] = dpi
            dps_ref[:, o:o + n] = dsi

        @pl.when(i == 0)
        def _():
            for g_ref in grads:
                g_ref[...] = jnp.zeros_like(g_ref)

        for (o, n), dmu in zip(_RW_GROUPS, dmus):
            grads[0][:, o:o + n] += dmu
        for g_ref, dval in zip(grads[1:], dsmall):
            g_ref[...] += dval

    row = pl.BlockSpec((t, RW_SHIFT), lambda i: (i, 0))
    prev = pl.BlockSpec((SUBLANES, RW_SHIFT), lambda i: (jnp.maximum(i * (t // SUBLANES) - 1, 0), 0))
    act = pl.BlockSpec((t, MIX_WIDTH), lambda i: (i, 0))
    pspecs = _rwkv_param_specs()
    full = jax.ShapeDtypeStruct((s, RW_SHIFT), F32)
    gshapes = [jax.ShapeDtypeStruct(q.shape, F32) for q in params[:8]]
    return pl.pallas_call(
        body, grid=(s // t,), in_specs=[row, prev] + pspecs + [act] * 10, out_specs=[row, row] + pspecs[:8],
        out_shape=[full, full] + gshapes, compiler_params=_cp(("arbitrary",)), name=name)(p, p, *params, *cots)


def shift_add(a, b, js, out_dtype, name):
    s, c = a.shape
    t = _tile(s, (256, 128, 64))
    tc = _tile(c, (1024, 768, 512, 640, 384, 256, 128))
    nt, nb = s // t, len(b)

    def body(a_ref, *rest):
        b_refs, n_refs, o_ref = rest[:nb], rest[nb:2 * nb], rest[2 * nb]
        last = (pl.program_id(1) < nt - 1).astype(F32)
        acc = a_ref[...]
        for b_ref, n_ref, j in zip(b_refs, n_refs, js):
            acc = acc + _shift_up(b_ref[...], n_ref[...] * last, j)
        o_ref[...] = acc.astype(o_ref.dtype)

    tile = pl.BlockSpec((t, tc), lambda j, i: (i, j))
    _, nxt = _halo_specs(t, s, tc, lambda j: j)
    return pl.pallas_call(
        body, grid=(c // tc, nt), in_specs=[tile] * (1 + nb) + [nxt] * nb, out_specs=tile,
        out_shape=jax.ShapeDtypeStruct((s, c), out_dtype), compiler_params=_cp(("parallel", "parallel")),
        name=name)(a, *b, *b)


def _rwkv_post_fn(y, r, k2, v, g, gn_g, gn_b, r_k, e, et):
    n = RWKV_HEAD_DIM
    yc = y - _dot(_dot(y, e), et) * (1.0 / n)
    rstd = lax.rsqrt(_dot(yc * yc, e) * (1.0 / n) + RWKV_GN_EPS)
    yn = yc * _dot(rstd, et) * gn_g + gn_b
    bonus = _dot(_dot(r * k2 * r_k, e), et) * v
    return (yn + bonus) * g


def rwkv_post_fwd(acts, params, name):
    s = acts[0].shape[0]
    t = _tile(s, (256, 128))

    def body(*refs):
        vals = [q[...] for q in refs[:10]]
        refs[10][...] = _rwkv_post_fn(*vals).astype(refs[10].dtype)

    act = pl.BlockSpec((t, MIX_WIDTH), lambda i: (i, 0))
    vec = pl.BlockSpec((1, MIX_WIDTH), lambda i: (0, 0))
    mats = [pl.BlockSpec((MIX_WIDTH, LANES), lambda i: (0, 0)), pl.BlockSpec((LANES, MIX_WIDTH), lambda i: (0, 0))]
    return pl.pallas_call(
        body, grid=(s // t,), in_specs=[act] * 5 + [vec] * 3 + mats, out_specs=act,
        out_shape=jax.ShapeDtypeStruct((s, MIX_WIDTH), BF16), compiler_params=_cp(("parallel",)),
        name=name)(*acts, *params)


def rwkv_post_bwd(acts, params, dcat, name):
    s = acts[0].shape[0]
    t = _tile(s, (128, 64))

    def body(*refs):
        ins, dy_ref, outs = refs[:10], refs[10], refs[11:]
        vals = [q[...] for q in ins]
        fn = lambda *a: _rwkv_post_fn(*a, vals[8], vals[9])
        _, vjp = jax.vjp(fn, *vals[:8])
        res = vjp(dy_ref[...])
        for o_ref, val in zip(outs[:5], res[:5]):
            o_ref[...] = val

        @pl.when(pl.program_id(0) == 0)
        def _():
            for g_ref in outs[5:]:
                g_ref[...] = jnp.zeros_like(g_ref)

        for g_ref, val in zip(outs[5:], res[5:]):
            g_ref[...] += val

    act = pl.BlockSpec((t, MIX_WIDTH), lambda i: (i, 0))
    vec = pl.BlockSpec((1, MIX_WIDTH), lambda i: (0, 0))
    mats = [pl.BlockSpec((MIX_WIDTH, LANES), lambda i: (0, 0)), pl.BlockSpec((LANES, MIX_WIDTH), lambda i: (0, 0))]
    a_shape = jax.ShapeDtypeStruct((s, MIX_WIDTH), F32)
    v_shape = jax.ShapeDtypeStruct((1, MIX_WIDTH), F32)
    return pl.pallas_call(
        body, grid=(s // t,), in_specs=[act] * 5 + [vec] * 3 + mats + [act], out_specs=[act] * 5 + [vec] * 3,
        out_shape=[a_shape] * 5 + [v_shape] * 3, compiler_params=_cp(("arbitrary",)), name=name)(*acts, *params, dcat)


RW_CHUNK = 64


RW_HEADS_PER_STEP = 4


def _heads(f, *per_head):
    return [f(*xs) for xs in zip(*per_head)]


def _rwkv_chunk_fn(r, lw, k, v, kk, b, st):
    c = RW_CHUNK
    ri = lax.broadcasted_iota(jnp.int32, (c, c), 0)
    ci = lax.broadcasted_iota(jnp.int32, (c, c), 1)
    incl, strict = ri >= ci, ri > ci
    ltri = incl.astype(F32)
    eye = (ri == ci).astype(F32)
    last_col = (ci == c - 1).astype(F32)
    last_row = (ri == c - 1).astype(F32)
    gc = _heads(lambda lw: _dot(ltri, lw), lw)
    a_t = _heads(lambda kk, gc, lw: -kk * jnp.exp(gc - lw), kk, gc, lw)
    e_neg = _heads(lambda gc: jnp.exp(-gc), gc)
    b_t = _heads(jnp.multiply, b, e_neg)
    k_t = _heads(jnp.multiply, k, e_neg)
    r_t = _heads(lambda r, gc: r * jnp.exp(gc), r, gc)
    m_ab = _heads(lambda x, y: jnp.where(strict, _dot_nt(x, y), 0.0), a_t, b_t)
    m_ak = _heads(lambda x, y: jnp.where(strict, _dot_nt(x, y), 0.0), a_t, k_t)
    m_rb = _heads(lambda x, y: jnp.where(incl, _dot_nt(x, y), 0.0), r_t, b_t)
    m_rk = _heads(lambda x, y: jnp.where(incl, _dot_nt(x, y), 0.0), r_t, k_t)
    tinv, pw = _heads(lambda m: eye + m, m_ab), m_ab
    for _ in range(5):
        pw = _heads(lambda p: _dot(p, p), pw)
        tinv = _heads(lambda t, p: t + _dot(t, p), tinv, pw)
    rhs = _heads(lambda a, s, m, v: _dot(a, s) + _dot(m, v), a_t, st, m_ak, v)
    u = _heads(_dot, tinv, rhs)
    y = _heads(lambda r, s, mb, u, mk, v: _dot(r, s) + _dot(mb, u) + _dot(mk, v), r_t, st, m_rb, u, m_rk, v)
    dec = _heads(lambda gc: jnp.exp(_dot(last_col, gc) - gc), gc)
    g_end = _heads(lambda gc: _dot_tn(gc, last_row), gc)
    new_st = _heads(lambda s, g, b, d, u, k, v: s * jnp.exp(g) + _dot_tn(b * d, u) + _dot_tn(k * d, v),
                    st, g_end, b, dec, u, k, v)
    return y, new_st


def rwkv_scan_fwd(r, lw, k, v, kk, b, name):
    s = r.shape[0]
    n, hp = RWKV_HEAD_DIM, RW_HEADS_PER_STEP
    nchunk, width = s // RW_CHUNK, RWKV_HEAD_DIM * RW_HEADS_PER_STEP

    def body(r_ref, w_ref, k_ref, v_ref, kk_ref, b_ref, y_ref, ck_ref, carry):
        @pl.when(pl.program_id(1) == 0)
        def _():
            carry[...] = jnp.zeros_like(carry)

        ck_ref[0] = carry[...]
        per_head = lambda ref: [ref[:, h * n:(h + 1) * n] for h in range(hp)]
        ys, sts = _rwkv_chunk_fn(*[per_head(q) for q in (r_ref, w_ref, k_ref, v_ref, kk_ref, b_ref, carry)])
        y_ref[...] = jnp.concatenate(ys, axis=1)
        carry[...] = jnp.concatenate(sts, axis=1)

    blk = pl.BlockSpec((RW_CHUNK, width), lambda j, c: (c, j))
    return pl.pallas_call(
        body, grid=(MIX_WIDTH // width, nchunk), in_specs=[blk] * 6,
        out_specs=[blk, pl.BlockSpec((1, n, width), lambda j, c: (c, 0, j))],
        out_shape=[jax.ShapeDtypeStruct((s, MIX_WIDTH), F32), jax.ShapeDtypeStruct((nchunk, n, MIX_WIDTH), F32)],
        scratch_shapes=[pltpu.VMEM((n, width), F32)],
        compiler_params=_cp(("parallel", "arbitrary")), name=name)(r, lw, k, v, kk, b)


def rwkv_scan_bwd(r, lw, k, v, kk, b, ck, dy, name):
    s = r.shape[0]
    n, hp = RWKV_HEAD_DIM, RW_HEADS_PER_STEP
    nchunk, width = s // RW_CHUNK, RWKV_HEAD_DIM * RW_HEADS_PER_STEP
    rev = lambda c: nchunk - 1 - c

    def body(r_ref, w_ref, k_ref, v_ref, kk_ref, b_ref, ck_ref, dy_ref, *rest):
        outs, carry = rest[:6], rest[6]

        @pl.when(pl.program_id(1) == 0)
        def _():
            carry[...] = jnp.zeros_like(carry)

        per_head = lambda ref: [ref[:, h * n:(h + 1) * n] for h in range(hp)]
        args = [per_head(q) for q in (r_ref, w_ref, k_ref, v_ref, kk_ref, b_ref)]
        args.append([ck_ref[0, :, h * n:(h + 1) * n] for h in range(hp)])
        _, vjp = jax.vjp(_rwkv_chunk_fn, *args)
        grads = vjp((per_head(dy_ref), per_head(carry)))
        for q in range(6):
            outs[q][...] = jnp.concatenate(grads[q], axis=1)
        carry[...] = jnp.concatenate(grads[6], axis=1)

    blk = pl.BlockSpec((RW_CHUNK, width), lambda j, c: (rev(c), j))
    out = jax.ShapeDtypeStruct((s, MIX_WIDTH), F32)
    return pl.pallas_call(
        body, grid=(MIX_WIDTH // width, nchunk),
        in_specs=[blk] * 6 + [pl.BlockSpec((1, n, width), lambda j, c: (rev(c), 0, j)), blk],
        out_specs=[blk] * 6, out_shape=[out] * 6, scratch_shapes=[pltpu.VMEM((n, width), F32)],
        compiler_params=_cp(("parallel", "arbitrary")), name=name)(r, lw, k, v, kk, b, ck, dy)


GD_Q, GD_K, GD_V, GD_Z, GD_QMEM, GD_BT, GD_AT, GD_COLS = 0, 768, 1536, 3072, 4608, 5120, 5248, 5376
_GD_GROUPS = ((GD_Q, GDN_QK_WIDTH), (GD_K, GDN_QK_WIDTH), (GD_V, MIX_WIDTH))


def _gdn_pre_fn(xs, convs, bt, at, a_log, dt_bias, e6, e6t, ebc):
    k_w = GDN_CONV
    acts = [_silu(sum(convs[g][j] * xs[g][k_w - 1 - j] for j in range(k_w))) for g in range(3)]
    l2 = lambda x: x * _dot(lax.rsqrt(_dot(x * x, e6) + 1e-6), e6t)
    beta = _sigmoid(bt)
    g = -jnp.exp(a_log) * _softplus(at + dt_bias)
    return l2(acts[0]), l2(acts[1]), acts[2], _dot(g, ebc), _dot(beta, ebc)


def _gdn_pre_inputs(x_ref, prev_ref, conv_ref, first):
    xs = [[_shift_down(x_ref[:, o:o + n], prev_ref[:, o:o + n] * first, j) for j in range(GDN_CONV)]
          for o, n in _GD_GROUPS]
    convs = [[conv_ref[j:j + 1, o:o + n] for j in range(GDN_CONV)] for o, n in _GD_GROUPS]
    return xs, convs


def _gdn_pre_specs(t):
    x = pl.BlockSpec((t, GDN_CONV_WIDTH), lambda i: (i, 0))
    prev = pl.BlockSpec((SUBLANES, GDN_CONV_WIDTH), lambda i: (jnp.maximum(i * (t // SUBLANES) - 1, 0), 0))
    bta = pl.BlockSpec((t, 2 * LANES), lambda i: (i, GD_BT // (2 * LANES)))
    conv = pl.BlockSpec((GDN_CONV, GDN_CONV_WIDTH), lambda i: (0, 0))
    vec = pl.BlockSpec((1, LANES), lambda i: (0, 0))
    mats = [pl.BlockSpec((GDN_QK_WIDTH, LANES), lambda i: (0, 0)), pl.BlockSpec((LANES, GDN_QK_WIDTH), lambda i: (0, 0)),
            pl.BlockSpec((LANES, MIX_WIDTH), lambda i: (0, 0))]
    return [x, prev, bta, conv, vec, vec] + mats


def gdn_pre_fwd(p, params, name):
    s = p.shape[0]
    t = _tile(s, (128, 64))

    def body(x_ref, prev_ref, bta_ref, conv_ref, al_ref, dt_ref, e6_ref, e6t_ref, ebc_ref, *outs):
        first = (pl.program_id(0) > 0).astype(F32)
        xs, convs = _gdn_pre_inputs(x_ref, prev_ref, conv_ref, first)
        res = _gdn_pre_fn(xs, convs, bta_ref[:, :LANES], bta_ref[:, LANES:], al_ref[...], dt_ref[...],
                          e6_ref[...], e6t_ref[...], ebc_ref[...])
        for o_ref, val in zip(outs, res):
            o_ref[...] = val

    qk = pl.BlockSpec((t, GDN_QK_WIDTH), lambda i: (i, 0))
    wide = pl.BlockSpec((t, MIX_WIDTH), lambda i: (i, 0))
    qk_s, wide_s = jax.ShapeDtypeStruct((s, GDN_QK_WIDTH), F32), jax.ShapeDtypeStruct((s, MIX_WIDTH), F32)
    return pl.pallas_call(
        body, grid=(s // t,), in_specs=_gdn_pre_specs(t), out_specs=[qk, qk, wide, wide, wide],
        out_shape=[qk_s, qk_s, wide_s, wide_s, wide_s], compiler_params=_cp(("parallel",)), name=name)(p, p, p, *params)


def gdn_pre_bwd(p, params, cots, name):
    s = p.shape[0]
    t = _tile(s, (64, 32))

    def body(x_ref, prev_ref, bta_ref, conv_ref, al_ref, dt_ref, e6_ref, e6t_ref, ebc_ref, *rest):
        cot, outs = rest[:5], rest[5:]
        dxs, dbta_ref, dconv_ref, dal_ref, ddt_ref = outs[:4], outs[4], outs[5], outs[6], outs[7]
        i = pl.program_id(0)
        first = (i > 0).astype(F32)
        xs, convs = _gdn_pre_inputs(x_ref, prev_ref, conv_ref, first)
        mats = (e6_ref[...], e6t_ref[...], ebc_ref[...])
        fn = lambda xs, convs, bt, at, al, dt: _gdn_pre_fn(xs, convs, bt, at, al, dt, *mats)
        _, vjp = jax.vjp(fn, xs, convs, bta_ref[:, :LANES], bta_ref[:, LANES:], al_ref[...], dt_ref[...])
        d_xs, d_convs, d_bt, d_at, d_al, d_dt = vjp(tuple(c[...] for c in cot))
        for g, (o, n) in enumerate(_GD_GROUPS):
            for j in range(GDN_CONV):
                dxs[j][:, o:o + n] = d_xs[g][j]
        dbta_ref[...] = jnp.concatenate([d_bt, d_at], axis=1).astype(dbta_ref.dtype)

        @pl.when(i == 0)
        def _():
            dconv_ref[...] = jnp.zeros_like(dconv_ref)
            dal_ref[...] = jnp.zeros_like(dal_ref)
            ddt_ref[...] = jnp.zeros_like(ddt_ref)

        for g, (o, n) in enumerate(_GD_GROUPS):
            for j in range(GDN_CONV):
                dconv_ref[j:j + 1, o:o + n] += d_convs[g][j]
        dal_ref[...] += d_al
        ddt_ref[...] += d_dt

    specs = _gdn_pre_specs(t)
    qk = pl.BlockSpec((t, GDN_QK_WIDTH), lambda i: (i, 0))
    wide = pl.BlockSpec((t, MIX_WIDTH), lambda i: (i, 0))
    x_s = jax.ShapeDtypeStruct((s, GDN_CONV_WIDTH), F32)
    vec_s = jax.ShapeDtypeStruct((1, LANES), F32)
    return pl.pallas_call(
        body, grid=(s // t,), in_specs=specs + [qk, qk, wide, wide, wide],
        out_specs=[specs[0]] * 4 + [pl.BlockSpec((t, 2 * LANES), lambda i: (i, 0)), specs[3], specs[4], specs[5]],
        out_shape=[x_s] * 4 + [jax.ShapeDtypeStruct((s, 2 * LANES), BF16),
                               jax.ShapeDtypeStruct((GDN_CONV, GDN_CONV_WIDTH), F32), vec_s, vec_s],
        compiler_params=_cp(("arbitrary",)), name=name)(p, p, p, *params, *cots)


def _gdn_post_fn(o, z, norm_g, e12, e12t, trep):
    rstd = lax.rsqrt(_dot(o * o, e12) * (1.0 / GDN_HEAD_DIM) + NORM_EPS)
    return o * _dot(rstd, e12t) * _dot(norm_g, trep) * _silu(z)


def _gdn_post_specs(t):
    act = pl.BlockSpec((t, MIX_WIDTH), lambda i: (i, 0))
    z = pl.BlockSpec((t, MIX_WIDTH), lambda i: (i, GD_Z // MIX_WIDTH))
    mats = [pl.BlockSpec((SUBLANES, LANES), lambda i: (0, 0)), pl.BlockSpec((MIX_WIDTH, LANES), lambda i: (0, 0)),
            pl.BlockSpec((LANES, MIX_WIDTH), lambda i: (0, 0)), pl.BlockSpec((LANES, MIX_WIDTH), lambda i: (0, 0))]
    return [act, z] + mats


def gdn_post_fwd(o, p, params, name):
    s = o.shape[0]
    t = _tile(s, (256, 128))

    def body(o_ref, z_ref, ng_ref, e_ref, et_ref, tr_ref, out_ref):
        res = _gdn_post_fn(o_ref[...], z_ref[...], ng_ref[0:1, :], e_ref[...], et_ref[...], tr_ref[...])
        out_ref[...] = res.astype(out_ref.dtype)

    act = pl.BlockSpec((t, MIX_WIDTH), lambda i: (i, 0))
    return pl.pallas_call(
        body, grid=(s // t,), in_specs=_gdn_post_specs(t), out_specs=act,
        out_shape=jax.ShapeDtypeStruct((s, MIX_WIDTH), BF16), compiler_params=_cp(("parallel",)),
        name=name)(o, p, *params)


def gdn_post_bwd(o, p, params, dcat, name):
    s = o.shape[0]
    t = _tile(s, (128, 64))

    def body(o_ref, z_ref, ng_ref, e_ref, et_ref, tr_ref, dy_ref, do_ref, dz_ref, dng_ref):
        mats = (e_ref[...], et_ref[...], tr_ref[...])
        fn = lambda o, z, ng: _gdn_post_fn(o, z, ng, *mats)
        _, vjp = jax.vjp(fn, o_ref[...], z_ref[...], ng_ref[0:1, :])
        d_o, d_z, d_ng = vjp(dy_ref[...])
        do_ref[...] = d_o
        dz_ref[...] = d_z.astype(dz_ref.dtype)

        @pl.when(pl.program_id(0) == 0)
        def _():
            dng_ref[...] = jnp.zeros_like(dng_ref)

        dng_ref[...] += d_ng

    act = pl.BlockSpec((t, MIX_WIDTH), lambda i: (i, 0))
    return pl.pallas_call(
        body, grid=(s // t,), in_specs=_gdn_post_specs(t) + [act],
        out_specs=[act, act, pl.BlockSpec((1, LANES), lambda i: (0, 0))],
        out_shape=[jax.ShapeDtypeStruct((s, MIX_WIDTH), F32), jax.ShapeDtypeStruct((s, MIX_WIDTH), BF16),
                   jax.ShapeDtypeStruct((1, LANES), F32)],
        compiler_params=_cp(("arbitrary",)), name=name)(o, p, *params, dcat)


GDN_REP = GDN_V_HEADS // GDN_QK_HEADS
GDN_QK_PER_STEP = 2


def _gdn_chunk_fn(q, k, v, gb, bb, gb64, state):
    c = GDN_CHUNK
    ri = lax.broadcasted_iota(jnp.int32, (c, c), 0)
    ci = lax.broadcasted_iota(jnp.int32, (c, c), 1)
    causal, strict = ri >= ci, ri > ci
    ltri = causal.astype(F32)
    eye = (ri == ci).astype(F32)
    first_col = (ci == 0).astype(F32)
    last_col = (ci == c - 1).astype(F32)
    last_col_tall = (lax.broadcasted_iota(jnp.int32, (GDN_HEAD_DIM, c), 1) == c - 1).astype(F32)
    of_value_head = lambda per_qk: [per_qk[h // GDN_REP] for h in range(len(v))]

    qs = of_value_head(_heads(lambda q: q * (GDN_HEAD_DIM ** -0.5), q))
    k = of_value_head(k)
    gc = _heads(lambda g: _dot(ltri, g), gb)
    gd = _heads(lambda g: _dot(ltri, g), gb64)
    decay = _heads(lambda gd: jnp.exp(jnp.where(causal, gd - _dot_nt(first_col, gd), -jnp.inf)), gd)
    kb = _heads(jnp.multiply, k, bb)
    lmat = _heads(lambda kb, k, dc: jnp.where(strict, _dot_nt(kb, k) * dc, 0.0), kb, k, decay)
    tmat, pw = _heads(lambda l: eye - l, lmat), lmat
    for _ in range(5):
        pw = _heads(lambda p: _dot(p, p), pw)
        tmat = _heads(lambda t, p: t + _dot(t, p), tmat, pw)
    eg = _heads(jnp.exp, gc)
    u = _heads(lambda t, v, bb: _dot(t, v * bb), tmat, v, bb)
    w = _heads(lambda t, kb, eg: _dot(t, kb * eg), tmat, kb, eg)
    a_qk = _heads(lambda q, k, dc: jnp.where(causal, _dot_nt(q, k) * dc, 0.0), qs, k, decay)
    k_dec = _heads(lambda k, gc: k * jnp.exp(_dot(last_col, gc) - gc), k, gc)
    v_new = _heads(lambda u, w, s: u - _dot(w, s), u, w, state)
    out = _heads(lambda q, eg, s, a, vn: _dot(q * eg, s) + _dot(a, vn), qs, eg, state, a_qk, v_new)
    new_state = _heads(lambda s, gc, kd, vn: s * jnp.exp(_dot(last_col_tall, gc)) + _dot_tn(kd, vn),
                       state, gc, k_dec, v_new)
    return out, new_state


def _gdn_chunk_specs(order):
    c, d, nq = GDN_CHUNK, GDN_HEAD_DIM, GDN_QK_PER_STEP
    qk = pl.BlockSpec((c, nq * d), lambda j, n: (order(n), j))
    vh = pl.BlockSpec((c, nq * GDN_REP * d), lambda j, n: (order(n), j))
    st = pl.BlockSpec((nq * GDN_REP, 1, d, d), lambda j, n: (j, order(n), 0, 0))
    return qk, vh, st


def _gdn_chunk_args(q_ref, k_ref, v_ref, gb_ref, bb_ref):
    d, nq = GDN_HEAD_DIM, GDN_QK_PER_STEP
    nv = nq * GDN_REP
    per = lambda ref, n: [ref[:, h * d:(h + 1) * d] for h in range(n)]
    lead = [gb_ref[:, h * d:h * d + GDN_CHUNK] for h in range(nv)]
    return per(q_ref, nq), per(k_ref, nq), per(v_ref, nv), per(gb_ref, nv), per(bb_ref, nv), lead


def gdn_chunk_fwd(q, k, v, gb, bb, name):
    s = q.shape[0]
    nc, d, nv = s // GDN_CHUNK, GDN_HEAD_DIM, GDN_QK_PER_STEP * GDN_REP

    def body(q_ref, k_ref, v_ref, gb_ref, bb_ref, o_ref, st_ref, carry):
        @pl.when(pl.program_id(1) == 0)
        def _():
            carry[...] = jnp.zeros_like(carry)

        states = [carry[h] for h in range(nv)]
        for h in range(nv):
            st_ref[h, 0] = states[h]
        outs, new_states = _gdn_chunk_fn(*_gdn_chunk_args(q_ref, k_ref, v_ref, gb_ref, bb_ref), states)
        o_ref[...] = jnp.concatenate(outs, axis=1)
        for h in range(nv):
            carry[h] = new_states[h]

    qk, vh, st = _gdn_chunk_specs(lambda n: n)
    return pl.pallas_call(
        body, grid=(GDN_QK_HEADS // GDN_QK_PER_STEP, nc), in_specs=[qk, qk, vh, vh, vh], out_specs=[vh, st],
        out_shape=[jax.ShapeDtypeStruct((s, MIX_WIDTH), F32), jax.ShapeDtypeStruct((GDN_V_HEADS, nc, d, d), F32)],
        scratch_shapes=[pltpu.VMEM((nv, d, d), F32)],
        compiler_params=_cp(("parallel", "arbitrary")), name=name)(q, k, v, gb, bb)


def gdn_chunk_bwd(q, k, v, gb, bb, states, do, name):
    s = q.shape[0]
    nc, d, nv = s // GDN_CHUNK, GDN_HEAD_DIM, GDN_QK_PER_STEP * GDN_REP
    rev = lambda n: nc - 1 - n

    def body(q_ref, k_ref, v_ref, gb_ref, bb_ref, st_ref, do_ref, dq_ref, dk_ref, dv_ref, dg_ref, db_ref, carry):
        @pl.when(pl.program_id(1) == 0)
        def _():
            carry[...] = jnp.zeros_like(carry)

        args = _gdn_chunk_args(q_ref, k_ref, v_ref, gb_ref, bb_ref) + ([st_ref[h, 0] for h in range(nv)],)
        _, vjp = jax.vjp(_gdn_chunk_fn, *args)
        cot = ([do_ref[:, h * d:(h + 1) * d] for h in range(nv)], [carry[h] for h in range(nv)])
        d_q, d_k, d_v, d_gb, d_bb, d_gb64, d_state = vjp(cot)
        dq_ref[...] = jnp.concatenate(d_q, axis=1)
        dk_ref[...] = jnp.concatenate(d_k, axis=1)
        dv_ref[...] = jnp.concatenate(d_v, axis=1)
        db_ref[...] = jnp.concatenate(d_bb, axis=1)
        dg_ref[...] = jnp.concatenate(d_gb, axis=1)
        for h in range(nv):
            dg_ref[:, h * d:h * d + GDN_CHUNK] += d_gb64[h]
            carry[h] = d_state[h]

    qk, vh, st = _gdn_chunk_specs(rev)
    qk_s, wide_s = jax.ShapeDtypeStruct((s, GDN_QK_WIDTH), F32), jax.ShapeDtypeStruct((s, MIX_WIDTH), F32)
    return pl.pallas_call(
        body, grid=(GDN_QK_HEADS // GDN_QK_PER_STEP, nc), in_specs=[qk, qk, vh, vh, vh, st, vh],
        out_specs=[qk, qk, vh, vh, vh], out_shape=[qk_s, qk_s, wide_s, wide_s, wide_s],
        scratch_shapes=[pltpu.VMEM((nv, d, d), F32)],
        compiler_params=_cp(("parallel", "arbitrary")), name=name)(q, k, v, gb, bb, states, do)


WEIGHTS = ['attn_norm', 'mem_norm', 'w_mem_kv', 'w_out', 'ffn_norm', 'w_ffn_up', 'ffn_conv', 'w_ffn_down', 'final_norm',
           'a_w_in', 'a_sinks', 'b_w_in', 'b_mu', 'b_w0', 'b_w_decay_up', 'b_a0', 'b_w_iclr_up', 'b_w_gate_up', 'b_k_k',
           'b_k_a', 'b_r_k', 'b_gn_g', 'b_gn_b', 'c_w_in', 'c_conv', 'c_a_log', 'c_dt_bias', 'c_norm_g']
INPUTS = ['x', 'mem'] + WEIGHTS + ['loss_target'] + ['m_' + n for n in WEIGHTS] + ['v_' + n for n in WEIGHTS]
REPLICATED = ['attn_norm', 'mem_norm', 'ffn_norm', 'final_norm', 'a_sinks', 'b_mu', 'b_w0', 'b_a0', 'b_k_k', 'b_k_a',
              'b_r_k', 'b_gn_g', 'b_gn_b', 'c_a_log', 'c_dt_bias', 'c_norm_g']
C_MIX = GDN_CONV_WIDTH + MIX_WIDTH
GATHER_ID = 1


def _cols_to_shards(full):
    rows, cols = full.shape
    return full.reshape(rows, N_DEV, cols // N_DEV).transpose(1, 0, 2)


def _shards_to_cols(g):
    return g.transpose(1, 0, 2).reshape(g.shape[1], N_DEV * g.shape[2])


def _pad_to(x, n, axis):
    pad = [(0, 0)] * x.ndim
    pad[axis] = (0, n - x.shape[axis])
    return jnp.pad(x, pad)


def _b_pad_cols(w):
    parts = [w[..., :4608], _pad_to(w[..., 4608:4704], LANES, -1), _pad_to(w[..., 4704:4800], LANES, -1), w[..., 4800:5056]]
    if w.shape[-1] > 5056:
        parts.append(w[..., 5056:])
    return jnp.concatenate(parts, axis=-1)


def _b_unpad_cols(w):
    parts = [w[..., :4608], w[..., RW_WD:RW_WD + RWKV_DECAY_RANK], w[..., RW_AD:RW_AD + RWKV_ICLR_RANK], w[..., RW_GD:RW_SHIFT]]
    if w.shape[-1] > RW_SHIFT:
        parts.append(w[..., RW_SHIFT:])
    return jnp.concatenate(parts, axis=-1)


def _c_pad_cols(w):
    return jnp.concatenate([w[..., :C_MIX], w[..., C_MIX + 24:], _pad_to(w[..., C_MIX:C_MIX + 12], LANES, -1),
                            _pad_to(w[..., C_MIX + 12:C_MIX + 24], LANES, -1)], axis=-1)


def _c_unpad_cols(w):
    return jnp.concatenate([w[..., :C_MIX], w[..., GD_BT:GD_BT + GDN_V_HEADS], w[..., GD_AT:GD_AT + GDN_V_HEADS],
                            w[..., GD_QMEM:GD_BT]], axis=-1)


def _pack(arrays):
    flat = jnp.concatenate([a.reshape(-1).astype(F32) for a in arrays])
    unit = SUBLANES * LANES
    return _pad_to(flat, -(-flat.size // unit) * unit, 0).reshape(-1, LANES)


def _unpack(packed, shapes):
    flat, out, at = packed.reshape(-1), [], 0
    for shp in shapes:
        n = int(np.prod(shp))
        out.append(flat[at:at + n].reshape(shp))
        at += n
    return out


def kernel(*args):
    a = dict(zip(INPUTS, args))
    x0, mem, target = a['x'][0], a['mem'][0], a['loss_target'][0]
    s = x0.shape[0]
    e64, e64t = _head_matrix(MIX_WIDTH, RWKV_HEAD_DIM)
    e6, e6t = _head_matrix(GDN_QK_WIDTH, GDN_HEAD_DIM)
    e12, e12t = _head_matrix(MIX_WIDTH, GDN_HEAD_DIM)
    trep = jnp.asarray((np.arange(LANES)[:, None] == np.arange(MIX_WIDTH)[None, :] % LANES).astype(np.float32))
    row = lambda v: v.reshape(1, -1)

    def in_proj_shard(l):
        kind, j = l % 3, l // 3
        return (a['a_w_in'], a['b_w_in'], a['c_w_in'])[kind][j]

    def small_shards(l):
        kind, j = l % 3, l // 3
        if kind == 1:
            return [a['b_w_decay_up'][j], a['b_w_iclr_up'][j], a['b_w_gate_up'][j]]
        if kind == 2:
            return [a['c_conv'][j]]
        return []

    gathered = []
    for l in range(DEPTH):
        big = [a['w_mem_kv'][l], a['w_out'][l], a['w_ffn_up'][l], a['w_ffn_down'][l], in_proj_shard(l)]
        shards = [w.astype(BF16) for w in big] + [a['ffn_conv'][l]] + small_shards(l)
        if gathered:
            shards, gathered[-1] = lax.optimization_barrier((shards, gathered[-1]))
            gathered.append(all_gather_many_async(shards, f"gather_weights_{l}", GATHER_ID + l))
        else:
            first = all_gather_many_async([shards[0], shards[4]], "gather_weights_0_attn", GATHER_ID + DEPTH)
            rest, first = lax.optimization_barrier(([shards[1], shards[2], shards[3], shards[5]], first))
            rest = all_gather_many_async(rest, "gather_weights_0", GATHER_ID)
            gathered.append([first[0], rest[0], rest[1], rest[2], first[1], rest[3]])

    def ffn_weights(g):
        return dict(w_out=g[1].reshape(D_MODEL, D_MODEL), w_up=_shards_to_cols(g[2]),
                    w_down=g[3].reshape(D_FF, D_MODEL), conv=_shards_to_cols(g[5]))

    def layer_weights(l, g):
        kind = l % 3
        w_in = _shards_to_cols(g[4])
        lw = dict(w_kv=g[0].reshape(D_MODEL, 2 * MEM_WIDTH))
        if kind == 0:
            lw['w_in'] = w_in
        elif kind == 1:
            lw['w_in'] = _b_pad_cols(w_in)
            lw['wdu'] = _pad_to(_shards_to_cols(g[6]), LANES, 0)
            lw['wiu'] = _pad_to(_shards_to_cols(g[7]), LANES, 0)
            lw['wgu'] = _shards_to_cols(g[8])
        else:
            lw['w_in'] = _c_pad_cols(w_in)
            lw['c_conv'] = _shards_to_cols(g[6])
        return lw

    def rwkv_params(j, lw):
        return (row(_b_pad_cols(a['b_mu'][j])), row(a['b_w0'][j]), lw['wdu'], row(a['b_a0'][j]), lw['wiu'], lw['wgu'],
                row(a['b_k_k'][j]), row(a['b_k_a'][j]), e64, e64t)

    def rwkv_post_params(j):
        return (row(a['b_gn_g'][j]), row(a['b_gn_b'][j]), row(a['b_r_k'][j]), e64, e64t)

    def gdn_params(j, lw):
        return (lw['c_conv'], _pad_lanes(a['c_a_log'][j]), _pad_lanes(a['c_dt_bias'][j]), e6, e6t, e12t)

    def gdn_post_params(j):
        return (jnp.tile(row(a['c_norm_g'][j]), (SUBLANES, 1)), e12, e12t, trep)

    x = x0
    saved, layers = [], []
    for l in range(DEPTH):
        kind, j = l % 3, l // 3
        g = gathered[l]
        if l > 0:
            x, g = lax.optimization_barrier((x, g))
        lw = layer_weights(l, g)
        if l > 0:
            lw.update(ffn_weights(g))
        layers.append(lw)
        sv = dict(x=x)
        h = rmsnorm_fwd(x, a['attn_norm'][l], BF16, f"attn_norm_{l}")
        memn = rmsnorm_fwd(mem, a['mem_norm'][l], BF16, f"mem_norm_{l}")
        mem_kv = mm(memn, lw['w_kv'], name=f"mem_kv_{l}")
        p = mm(h, lw['w_in'], name=f"in_proj_{l}")
        if kind == 0:
            y = swa_fwd(p, a['a_sinks'][j], f"swa_{l}")
            q_col = MIX_WIDTH + 2 * SWA_KV_HEADS * SWA_HEAD_DIM
        elif kind == 1:
            pre = rwkv_pre_fwd(p, rwkv_params(j, lw), f"rwkv_pre_{l}")
            yscan, ck = rwkv_scan_fwd(*pre[:6], f"rwkv_scan_{l}")
            post_in = (yscan, pre[0], pre[2], pre[3], pre[6])
            y = rwkv_post_fwd(post_in, rwkv_post_params(j), f"rwkv_post_{l}")
            sv.update(pre=pre, ck=ck, post_in=post_in)
            q_col = RW_SHIFT
        else:
            pre = gdn_pre_fwd(p, gdn_params(j, lw), f"gdn_pre_{l}")
            o, states = gdn_chunk_fwd(*pre, f"gdn_chunk_{l}")
            y = gdn_post_fwd(o, p, gdn_post_params(j), f"gdn_post_{l}")
            sv.update(pre=pre, o=o, states=states)
            q_col = GD_QMEM
        y_mem = mem_attn_fwd(p, q_col, mem_kv, f"mem_attn_{l}")
        if l == 0:
            y_mem, g = lax.optimization_barrier((y_mem, g))
            lw.update(ffn_weights(g))
        cat = jnp.concatenate([y, y_mem], axis=1)
        x1 = mm(cat, lw['w_out'], res=x, name=f"out_proj_{l}")
        hf = rmsnorm_fwd(x1, a['ffn_norm'][l], BF16, f"ffn_norm_{l}")
        u0 = mm(hf, lw['w_up'], name=f"ffn_up_{l}")
        act = ffn_act_fwd(u0, lw['conv'], f"ffn_act_{l}")
        x = mm(act, lw['w_down'], res=x1, name=f"ffn_down_{l}")
        sv.update(h=h, memn=memn, mem_kv=mem_kv, p=p, q_col=q_col, cat=cat, x1=x1, hf=hf, u0=u0, act=act)
        saved.append(sv)

    loss_part, dx, d_final_norm = final_loss(x, a['final_norm'], target, "final_loss")

    rep_grads = {n: [None] * a[n].shape[0] for n in ('attn_norm', 'mem_norm', 'ffn_norm', 'a_sinks')}
    rep_grads['final_norm'] = d_final_norm
    results = {}
    exchanged, pending = {}, {}

    my_block = _block_index(_coords())
    mine = (lax.broadcasted_iota(jnp.int32, (N_DEV, SUBLANES, LANES), 0) == my_block).astype(F32)

    def apply_adam(name, idx, pieces, tag):
        sent, landed = pieces
        w, m, v = a[name][idx], a['m_' + name][idx], a['v_' + name][idx]
        shp = w.shape
        two_d = (-1, shp[-1])
        shape_2d = w.reshape(two_d).shape
        own = lax.dynamic_index_in_dim(sent, my_block, 0, keepdims=False).reshape(shape_2d)
        out = adamw_sum(landed.reshape((N_DEV,) + shape_2d), w.reshape(two_d), m.reshape(two_d),
                        v.reshape(two_d), f"adamw_{name}_{tag}", own=own, mine=mine)
        results.setdefault(name, {})[idx] = [o.reshape(shp) for o in out]

    for l in reversed(range(DEPTH)):
        kind, j = l % 3, l // 3
        lw, sv = layers[l], saved[l]
        p, q_col = sv['p'], sv['q_col']
        d_act = mm(dx, lw['w_down'], tb=True, name=f"d_ffn_act_{l}")
        dw_down = mm(sv['act'], dx, ta=True, out_dtype=BF16, name=f"dw_ffn_down_{l}")
        dug, duv, dcg, dcv = ffn_act_bwd(sv['u0'], lw['conv'], d_act, f"ffn_act_bwd_{l}")
        du0 = jnp.concatenate([dug, duv], axis=1)
        d_conv = jnp.concatenate([dcg, dcv], axis=1)
        d_hf = mm(du0, lw['w_up'], tb=True, name=f"d_ffn_norm_out_{l}")
        dw_up = mm(sv['hf'], du0, ta=True, out_dtype=BF16, col_shards=True, name=f"dw_ffn_up_{l}")
        ffn_pieces = [dw_up, dw_down.reshape(N_DEV, -1, D_MODEL)]
        if l == 0:
            *early, token = all_to_all_start(ffn_pieces, "exchange_start_0_ffn")
            d_hf = d_hf + token[0, 0]
            pending['ffn'] = tuple(early)
        dx1, rep_grads['ffn_norm'][l] = rmsnorm_bwd(sv['x1'], a['ffn_norm'][l], d_hf, dx, f"ffn_norm_bwd_{l}")
        dcat = mm(dx1, lw['w_out'], tb=True, name=f"d_cat_{l}")
        dw_out = mm(sv['cat'], dx1, ta=True, out_dtype=BF16, name=f"dw_out_{l}")
        dq_mem, d_mem_kv = mem_attn_bwd(p, q_col, sv['mem_kv'], dcat, f"mem_attn_bwd_{l}")
        small_grads = []
        if kind == 0:
            dq, dk, dv, rep_grads['a_sinks'][j] = swa_bwd(p, a['a_sinks'][j], dcat, f"swa_bwd_{l}")
            dp = jnp.concatenate([dq, dk, dv, dq_mem], axis=1)
        elif kind == 1:
            post = rwkv_post_bwd(sv['post_in'], rwkv_post_params(j), dcat, f"rwkv_post_bwd_{l}")
            scan = rwkv_scan_bwd(*sv['pre'][:6], sv['ck'], post[0], f"rwkv_scan_bwd_{l}")
            res = rwkv_pre_bwd(p, rwkv_params(j, lw), tuple(scan) + tuple(post[1:5]), f"rwkv_pre_bwd_{l}")
            dp_mix = shift_add(res[0], [res[1]], [1], BF16, f"rwkv_shift_bwd_{l}")
            dp = jnp.concatenate([dp_mix, dq_mem], axis=1)
            for n, val in zip(('b_mu', 'b_w0', 'b_a0', 'b_k_k', 'b_k_a'), (_b_unpad_cols(res[2]), res[3], res[5], res[8], res[9])):
                rep_grads[n] = val
            rep_grads.update(b_gn_g=post[5], b_gn_b=post[6], b_r_k=post[7])
            small_grads = [_cols_to_shards(res[4][:RWKV_DECAY_RANK]), _cols_to_shards(res[6][:RWKV_ICLR_RANK]),
                           _cols_to_shards(res[7])]
        else:
            d_o, dz, rep_grads['c_norm_g'] = gdn_post_bwd(sv['o'], p, gdn_post_params(j), dcat, f"gdn_post_bwd_{l}")
            chunk = gdn_chunk_bwd(*sv['pre'], sv['states'], d_o, f"gdn_chunk_bwd_{l}")
            res = gdn_pre_bwd(p, gdn_params(j, lw), chunk, f"gdn_pre_bwd_{l}")
            dqkv = shift_add(res[0], list(res[1:4]), [1, 2, 3], BF16, f"gdn_shift_bwd_{l}")
            dp = jnp.concatenate([dqkv, dz, dq_mem, res[4]], axis=1)
            rep_grads.update(c_a_log=res[6][:, :GDN_V_HEADS], c_dt_bias=res[7][:, :GDN_V_HEADS])
            small_grads = [_cols_to_shards(res[5])]
        d_h = mm(dp, lw['w_in'], tb=True, name=f"d_attn_norm_out_{l}")
        dw_in = mm(sv['h'], dp, ta=True, out_dtype=BF16, name=f"dw_in_{l}")
        dx, rep_grads['attn_norm'][l] = rmsnorm_bwd(sv['x'], a['attn_norm'][l], d_h, dx1, f"attn_norm_bwd_{l}")
        d_memn = mm(d_mem_kv, lw['w_kv'], tb=True, name=f"d_mem_norm_out_{l}")
        dw_kv = mm(sv['memn'], d_mem_kv, ta=True, out_dtype=BF16, name=f"dw_mem_kv_{l}")
        _, rep_grads['mem_norm'][l] = rmsnorm_bwd(mem, a['mem_norm'][l], d_memn, None, f"mem_norm_bwd_{l}")

        if kind == 1:
            dw_in = _b_unpad_cols(dw_in)
        elif kind == 2:
            dw_in = _c_unpad_cols(dw_in)
        pieces = [dw_kv.reshape(N_DEV, -1, 2 * MEM_WIDTH), dw_out.reshape(N_DEV, -1, D_MODEL), _cols_to_shards(dw_in),
                  _cols_to_shards(d_conv)] + small_grads + ([] if l == 0 else ffn_pieces)
        if l == 0:
            rep_vals = []
            for n in REPLICATED:
                gval = rep_grads[n]
                gval = jnp.stack(gval) if isinstance(gval, list) else gval
                rep_vals.append(gval.reshape(a[n].shape))
            part = _pack(rep_vals + [loss_part.reshape(1)])
            rep_gathered = all_gather_many([part], "gather_small_grads")[0]
            pieces, rep_gathered = lax.optimization_barrier((pieces, rep_gathered))
        send, recv, thru, lands, token = all_to_all_start(pieces, f"exchange_start_{l}")
        dx = dx + token[0, 0]
        pending[l] = (send, recv, thru, lands)

    for l in reversed(range(1, DEPTH)):
        exchanged[l] = all_to_all_wait(*pending.pop(l), dx, f"exchange_wait_{l}")
    for l in reversed(range(DEPTH)):
        kind, j = l % 3, l // 3
        if l == 0:
            keys = [(n, i) for n in ('w_ffn_up', 'w_ffn_down') for i in range(1, DEPTH)]
            done_above = lax.optimization_barrier(tuple(results[n][i][0] for n, i in keys))
            for (n, i), val in zip(keys, done_above):
                results[n][i][0] = val
            exchanged[0] = all_to_all_wait(*pending.pop(0), done_above[0], "exchange_wait_0")
            exchanged[0] += all_to_all_wait(*pending.pop('ffn'), done_above[0], "exchange_wait_0_ffn")
        got = exchanged[l]
        small = {1: ('b_w_decay_up', 'b_w_iclr_up', 'b_w_gate_up'), 2: ('c_conv',)}.get(kind, ())
        in_name = ('a_w_in', 'b_w_in', 'c_w_in')[kind]
        names = [('w_mem_kv', l), ('w_out', l), (in_name, j), ('ffn_conv', l)] + [(n, j) for n in small]
        names += [('w_ffn_up', l), ('w_ffn_down', l)]
        for (name, idx), pc in zip(names, got, strict=True):
            apply_adam(name, idx, pc, l)

    shapes = [a[n].shape for n in REPLICATED] + [(1,)]
    zero = jnp.zeros((1,), F32)
    packed = lambda pre: _pack([a[pre + n] for n in REPLICATED] + [zero])
    rep_out = adamw_sum(rep_gathered, packed(''), packed('m_'), packed('v_'), "adamw_replicated")
    rep_out = [_unpack(o, shapes) for o in rep_out]
    loss = rep_out[0][-1][0]
    for i, n in enumerate(REPLICATED):
        results[n] = [o[i] for o in rep_out]

    def leaf(name, which):
        r = results[name]
        if isinstance(r, dict):
            return jnp.stack([r[i][which] for i in range(len(r))])
        return r[which]

    outs = [loss, dx[None]]
    for which in range(4):
        outs += [leaf(n, which) for n in WEIGHTS]
    return tuple(outs)
```
